```python
import jax, jax.numpy as jnp
from jax import lax
import numpy as np

D_MODEL = 1024
BATCH = 8
SEQ = 4096
DEPTH = 2

GRID_W = 64
CTX_LEN = 256
HEAD_DIM = 64
N_Q_HEADS = 8
N_KV_HEADS = 2
Q_GROUP = N_Q_HEADS // N_KV_HEADS
ATTN_WIDTH = N_Q_HEADS * HEAD_DIM
KV_WIDTH = N_KV_HEADS * HEAD_DIM
CHUNK = 128
N_SG_GROUPS = 4
SG_WIDTH = D_MODEL - ATTN_WIDTH
SG_GROUP_DIM = SG_WIDTH // N_SG_GROUPS
IN_WIDTH = ATTN_WIDTH + 2 * KV_WIDTH + 2 * SG_WIDTH
Q_BLOCK = 128
ROPE_THETA = 10000.0
AXIS_ROPE_DIM = HEAD_DIM // 2
CONV_WIDTH = 31
D_FF = ((8 * D_MODEL // 3 + 255) // 256) * 256
N_EVEN = (DEPTH + 1) // 2
N_ODD = DEPTH // 2
EPS = 1e-6

kernel_name = "hybrid_attn_sgmlp_conformer_dit"


def rms_norm(x, g):
    xf = x.astype(jnp.float32)
    y = xf * lax.rsqrt(jnp.mean(xf * xf, axis=-1, keepdims=True) + EPS)
    return (y * g.astype(jnp.float32)).astype(x.dtype)


def layer_norm(x, g=None, b=None):
    xf = x.astype(jnp.float32)
    mu = jnp.mean(xf, axis=-1, keepdims=True)
    var = jnp.mean(jnp.square(xf - mu), axis=-1, keepdims=True)
    y = (xf - mu) * lax.rsqrt(var + EPS)
    if g is not None:
        y = y * g.astype(jnp.float32) + b.astype(jnp.float32)
    return y.astype(x.dtype)


def axial_rope_tables(n):
    rows = n // GRID_W
    row = jnp.broadcast_to(jnp.arange(rows)[:, None], (rows, GRID_W)).reshape(-1).astype(jnp.float32)
    col = jnp.broadcast_to(jnp.arange(GRID_W)[None, :], (rows, GRID_W)).reshape(-1).astype(jnp.float32)
    inv = ROPE_THETA ** (-jnp.arange(0, AXIS_ROPE_DIM, 2, dtype=jnp.float32) / AXIS_ROPE_DIM)
    ang_r = row[:, None] * inv[None, :]
    ang_c = col[:, None] * inv[None, :]
    return (jnp.cos(ang_r), jnp.sin(ang_r), jnp.cos(ang_c), jnp.sin(ang_c))


def rotate_half(x, cos, sin):
    x1, x2 = jnp.split(x, 2, axis=-1)
    cos = cos[None, :, None, :]
    sin = sin[None, :, None, :]
    return jnp.concatenate([x1 * cos - x2 * sin, x2 * cos + x1 * sin], axis=-1)


def apply_axial_rope(x, tables):
    cos_r, sin_r, cos_c, sin_c = tables
    xf = x.astype(jnp.float32)
    xr, xc = jnp.split(xf, 2, axis=-1)
    out = jnp.concatenate([rotate_half(xr, cos_r, sin_r), rotate_half(xc, cos_c, sin_c)], axis=-1)
    return out.astype(x.dtype)


def block_attention(q, k, v):
    b, n = q.shape[0], q.shape[1]
    nb = n // Q_BLOCK
    qb = (q * HEAD_DIM ** -0.5).reshape(b, nb, Q_BLOCK, N_KV_HEADS, Q_GROUP, HEAD_DIM)
    qb = qb.transpose(1, 0, 2, 3, 4, 5)

    def one_block(q_blk):
        s = jnp.einsum('bqkgd,bskd->bkgqs', q_blk, k).astype(jnp.float32)
        p = jax.nn.softmax(s, axis=-1).astype(v.dtype)
        return jnp.einsum('bkgqs,bskd->bqkgd', p, v)

    o = lax.map(one_block, qb)
    return o.transpose(1, 0, 2, 3, 4, 5).reshape(b, n, ATTN_WIDTH)


def spatial_gating(u, v, w_sp, b_sp):
    b, n = u.shape[0], u.shape[1]
    shape = (b, n // CHUNK, CHUNK, N_SG_GROUPS, SG_GROUP_DIM)
    vn = layer_norm(v.reshape(shape))
    mixed = jnp.einsum('gpq,bmqgc->bmpgc', w_sp, vn) + b_sp.T[:, :, None]
    return (u.reshape(shape) * mixed).reshape(b, n, SG_WIDTH)


def split_in_proj(p):
    o1 = ATTN_WIDTH
    o2 = o1 + KV_WIDTH
    o3 = o2 + KV_WIDTH
    o4 = o3 + SG_WIDTH
    return jnp.split(p, [o1, o2, o3, o4], axis=-1)


def even_mixer(xm, xc, w_in, q_gain, k_gain, w_sp, b_sp, w_out, rope, ctx_out):
    b, n, _ = xm.shape
    bc, m, _ = xc.shape
    q, k, v, su, sv = split_in_proj(xm @ w_in)
    q = apply_axial_rope(rms_norm(q.reshape(b, n, N_Q_HEADS, HEAD_DIM), q_gain), rope)
    k = apply_axial_rope(rms_norm(k.reshape(b, n, N_KV_HEADS, HEAD_DIM), k_gain), rope)
    v = v.reshape(b, n, N_KV_HEADS, HEAD_DIM)
    if ctx_out:
        qc, kc, vc, suc, svc = split_in_proj(xc @ w_in)
    else:
        kc, vc = jnp.split(xc @ w_in[:, ATTN_WIDTH:ATTN_WIDTH + 2 * KV_WIDTH], 2, axis=-1)
    kc = rms_norm(kc.reshape(bc, m, N_KV_HEADS, HEAD_DIM), k_gain)
    vc = vc.reshape(bc, m, N_KV_HEADS, HEAD_DIM)
    attn = block_attention(q, jnp.concatenate([kc, k], axis=1), jnp.concatenate([vc, v], axis=1))
    sg = spatial_gating(jax.nn.gelu(su), jax.nn.gelu(sv), w_sp, b_sp)
    y = jnp.concatenate([attn, sg], axis=-1) @ w_out
    yc = None
    if ctx_out:
        qc = rms_norm(qc.reshape(bc, m, N_Q_HEADS, HEAD_DIM), q_gain)
        attn_c = block_attention(qc, kc, vc)
        sg_c = spatial_gating(jax.nn.gelu(suc), jax.nn.gelu(svc), w_sp, b_sp)
        yc = jnp.concatenate([attn_c, sg_c], axis=-1) @ w_out
    return y, yc


def conformer_conv(x, w_pw1, b_pw1, w_dw, b_dw, ln_g, ln_b, w_pw2, b_pw2):
    a, gate = jnp.split(x @ w_pw1 + b_pw1, 2, axis=-1)
    h = a * jax.nn.sigmoid(gate)
    h = lax.conv_general_dilated(h, w_dw[:, None, :].astype(h.dtype), window_strides=(1,),
                                 padding=[(CONV_WIDTH // 2, CONV_WIDTH // 2)],
                                 dimension_numbers=('NWC', 'WIO', 'NWC'),
                                 feature_group_count=D_MODEL) + b_dw
    h = jax.nn.silu(layer_norm(h, ln_g, ln_b))
    return h @ w_pw2 + b_pw2


def swiglu_ffn(x, w_in, w_out):
    g, u = jnp.split(x @ w_in, 2, axis=-1)
    return (jax.nn.silu(g) * u) @ w_out


def adaln(cvec, w_mod, b_mod):
    return jnp.split(jax.nn.silu(cvec) @ w_mod + b_mod, 6, axis=-1)


def _fwd_setup_inputs(seed: int = 0) -> dict:
    key = jax.random.key(seed)
    ks = iter(jax.random.split(key, 32))
    f32 = jnp.float32

    def nrm(shape, scale):
        return jax.random.normal(next(ks), shape, f32) * scale

    D = D_MODEL
    return {
        'x': nrm((BATCH, SEQ, D), 1.0),
        'c': nrm((BATCH, D), 1.0),
        'ctx': nrm((BATCH, CTX_LEN, D), 1.0),
        'c_ctx': nrm((D,), 1.0),
        'w_mod': nrm((DEPTH, D, 6 * D), 0.5 * D ** -0.5),
        'b_mod': nrm((DEPTH, 6 * D), 0.01),
        'g_mix': 1.0 + nrm((DEPTH, D), 0.01),
        'g_ffn': 1.0 + nrm((DEPTH, D), 0.01),
        'w_ffn_in': nrm((DEPTH, D, 2 * D_FF), D ** -0.5),
        'w_ffn_out': nrm((DEPTH, D_FF, D), D_FF ** -0.5),
        'w_in': nrm((N_EVEN, D, IN_WIDTH), D ** -0.5),
        'q_gain': 1.0 + nrm((N_EVEN, HEAD_DIM), 0.01),
        'k_gain': 1.0 + nrm((N_EVEN, HEAD_DIM), 0.01),
        'w_sp': nrm((N_EVEN, N_SG_GROUPS, CHUNK, CHUNK), CHUNK ** -0.5),
        'b_sp': 1.0 + nrm((N_EVEN, N_SG_GROUPS, CHUNK), 0.01),
        'w_out': nrm((N_EVEN, ATTN_WIDTH + SG_WIDTH, D), (ATTN_WIDTH + SG_WIDTH) ** -0.5),
        'w_pw1': nrm((N_ODD, D, 2 * D), D ** -0.5),
        'b_pw1': nrm((N_ODD, 2 * D), 0.01),
        'w_dw': nrm((N_ODD, CONV_WIDTH, D), CONV_WIDTH ** -0.5),
        'b_dw': nrm((N_ODD, D), 0.01),
        'ln_g': 1.0 + nrm((N_ODD, D), 0.01),
        'ln_b': nrm((N_ODD, D), 0.01),
        'w_pw2': nrm((N_ODD, D, D), D ** -0.5),
        'b_pw2': nrm((N_ODD, D), 0.01),
        'g_final': 1.0 + nrm((D,), 0.01),
    }


def _fwd_reference(x, c, ctx, c_ctx, w_mod, b_mod, g_mix, g_ffn, w_ffn_in, w_ffn_out,
              w_in, q_gain, k_gain, w_sp, b_sp, w_out,
              w_pw1, b_pw1, w_dw, b_dw, ln_g, ln_b, w_pw2, b_pw2, g_final):
    n = x.shape[1]
    rope = axial_rope_tables(n)
    h, hc = x, ctx
    for l in range(DEPTH):
        even = (l % 2 == 0)
        i = l // 2
        ctx_after = any(j % 2 == 0 for j in range(l + 1, DEPTH))
        sh1, sc1, gt1, sh2, sc2, gt2 = [t[:, None, :] for t in adaln(c, w_mod[l], b_mod[l])]
        xm = rms_norm(h, g_mix[l]) * (1.0 + sc1) + sh1
        if even or ctx_after:
            csh1, csc1, cgt1, csh2, csc2, cgt2 = adaln(c_ctx, w_mod[l], b_mod[l])
            xc = rms_norm(hc, g_mix[l]) * (1.0 + csc1) + csh1
        if even:
            y, yc = even_mixer(xm, xc, w_in[i], q_gain[i], k_gain[i], w_sp[i], b_sp[i], w_out[i],
                               rope, ctx_after)
        else:
            conv_p = (w_pw1[i], b_pw1[i], w_dw[i], b_dw[i], ln_g[i], ln_b[i], w_pw2[i], b_pw2[i])
            y = conformer_conv(xm, *conv_p)
            yc = conformer_conv(xc, *conv_p) if ctx_after else None
        h = h + gt1 * y
        h = h + gt2 * swiglu_ffn(rms_norm(h, g_ffn[l]) * (1.0 + sc2) + sh2, w_ffn_in[l], w_ffn_out[l])
        if ctx_after:
            hc = hc + cgt1 * yc
            hc = hc + cgt2 * swiglu_ffn(rms_norm(hc, g_ffn[l]) * (1.0 + csc2) + csh2,
                                        w_ffn_in[l], w_ffn_out[l])
    return rms_norm(h, g_final)


import jax as _jax
import jax.numpy as _jnp

TWIN_FORMAT = 'train_step'
FWD_PARAMS = ['x', 'c', 'ctx', 'c_ctx', 'w_mod', 'b_mod', 'g_mix', 'g_ffn', 'w_ffn_in', 'w_ffn_out', 'w_in', 'q_gain', 'k_gain', 'w_sp', 'b_sp', 'w_out', 'w_pw1', 'b_pw1', 'w_dw', 'b_dw', 'ln_g', 'ln_b', 'w_pw2', 'b_pw2', 'g_final']
TWIN_WEIGHTS = ['c_ctx', 'w_mod', 'b_mod', 'g_mix', 'g_ffn', 'w_ffn_in', 'w_ffn_out', 'w_in', 'q_gain', 'k_gain', 'w_sp', 'b_sp', 'w_out', 'w_pw1', 'b_pw1', 'w_dw', 'b_dw', 'ln_g', 'ln_b', 'w_pw2', 'b_pw2', 'g_final']
TWIN_DIFF_INPUT = 'x'
TWIN_INPUTS = ['x', 'c', 'ctx', 'c_ctx', 'w_mod', 'b_mod', 'g_mix', 'g_ffn', 'w_ffn_in', 'w_ffn_out', 'w_in', 'q_gain', 'k_gain', 'w_sp', 'b_sp', 'w_out', 'w_pw1', 'b_pw1', 'w_dw', 'b_dw', 'ln_g', 'ln_b', 'w_pw2', 'b_pw2', 'g_final', 'loss_target', 'm_c_ctx', 'm_w_mod', 'm_b_mod', 'm_g_mix', 'm_g_ffn', 'm_w_ffn_in', 'm_w_ffn_out', 'm_w_in', 'm_q_gain', 'm_k_gain', 'm_w_sp', 'm_b_sp', 'm_w_out', 'm_w_pw1', 'm_b_pw1', 'm_w_dw', 'm_b_dw', 'm_ln_g', 'm_ln_b', 'm_w_pw2', 'm_b_pw2', 'm_g_final', 'v_c_ctx', 'v_w_mod', 'v_b_mod', 'v_g_mix', 'v_g_ffn', 'v_w_ffn_in', 'v_w_ffn_out', 'v_w_in', 'v_q_gain', 'v_k_gain', 'v_w_sp', 'v_b_sp', 'v_w_out', 'v_w_pw1', 'v_b_pw1', 'v_w_dw', 'v_b_dw', 'v_ln_g', 'v_ln_b', 'v_w_pw2', 'v_b_pw2', 'v_g_final']
TWIN_OUTPUTS = ['loss', 'grad_x', 'grad_c_ctx', 'grad_w_mod', 'grad_b_mod', 'grad_g_mix', 'grad_g_ffn', 'grad_w_ffn_in', 'grad_w_ffn_out', 'grad_w_in', 'grad_q_gain', 'grad_k_gain', 'grad_w_sp', 'grad_b_sp', 'grad_w_out', 'grad_w_pw1', 'grad_b_pw1', 'grad_w_dw', 'grad_b_dw', 'grad_ln_g', 'grad_ln_b', 'grad_w_pw2', 'grad_b_pw2', 'grad_g_final', 'delta_c_ctx', 'delta_w_mod', 'delta_b_mod', 'delta_g_mix', 'delta_g_ffn', 'delta_w_ffn_in', 'delta_w_ffn_out', 'delta_w_in', 'delta_q_gain', 'delta_k_gain', 'delta_w_sp', 'delta_b_sp', 'delta_w_out', 'delta_w_pw1', 'delta_b_pw1', 'delta_w_dw', 'delta_b_dw', 'delta_ln_g', 'delta_ln_b', 'delta_w_pw2', 'delta_b_pw2', 'delta_g_final', 'new_m_c_ctx', 'new_m_w_mod', 'new_m_b_mod', 'new_m_g_mix', 'new_m_g_ffn', 'new_m_w_ffn_in', 'new_m_w_ffn_out', 'new_m_w_in', 'new_m_q_gain', 'new_m_k_gain', 'new_m_w_sp', 'new_m_b_sp', 'new_m_w_out', 'new_m_w_pw1', 'new_m_b_pw1', 'new_m_w_dw', 'new_m_b_dw', 'new_m_ln_g', 'new_m_ln_b', 'new_m_w_pw2', 'new_m_b_pw2', 'new_m_g_final', 'new_v_c_ctx', 'new_v_w_mod', 'new_v_b_mod', 'new_v_g_mix', 'new_v_g_ffn', 'new_v_w_ffn_in', 'new_v_w_ffn_out', 'new_v_w_in', 'new_v_q_gain', 'new_v_k_gain', 'new_v_w_sp', 'new_v_b_sp', 'new_v_w_out', 'new_v_w_pw1', 'new_v_b_pw1', 'new_v_w_dw', 'new_v_b_dw', 'new_v_ln_g', 'new_v_ln_b', 'new_v_w_pw2', 'new_v_b_pw2', 'new_v_g_final']
TWIN_LEAF_KINDS = {'loss': 'loss', 'grad_x': 'grad_x', 'grad_c_ctx': 'grad_w', 'grad_w_mod': 'grad_w', 'grad_b_mod': 'grad_w', 'grad_g_mix': 'grad_w', 'grad_g_ffn': 'grad_w', 'grad_w_ffn_in': 'grad_w', 'grad_w_ffn_out': 'grad_w', 'grad_w_in': 'grad_w', 'grad_q_gain': 'grad_w', 'grad_k_gain': 'grad_w', 'grad_w_sp': 'grad_w', 'grad_b_sp': 'grad_w', 'grad_w_out': 'grad_w', 'grad_w_pw1': 'grad_w', 'grad_b_pw1': 'grad_w', 'grad_w_dw': 'grad_w', 'grad_b_dw': 'grad_w', 'grad_ln_g': 'grad_w', 'grad_ln_b': 'grad_w', 'grad_w_pw2': 'grad_w', 'grad_b_pw2': 'grad_w', 'grad_g_final': 'grad_w', 'delta_c_ctx': 'delta_w', 'delta_w_mod': 'delta_w', 'delta_b_mod': 'delta_w', 'delta_g_mix': 'delta_w', 'delta_g_ffn': 'delta_w', 'delta_w_ffn_in': 'delta_w', 'delta_w_ffn_out': 'delta_w', 'delta_w_in': 'delta_w', 'delta_q_gain': 'delta_w', 'delta_k_gain': 'delta_w', 'delta_w_sp': 'delta_w', 'delta_b_sp': 'delta_w', 'delta_w_out': 'delta_w', 'delta_w_pw1': 'delta_w', 'delta_b_pw1': 'delta_w', 'delta_w_dw': 'delta_w', 'delta_b_dw': 'delta_w', 'delta_ln_g': 'delta_w', 'delta_ln_b': 'delta_w', 'delta_w_pw2': 'delta_w', 'delta_b_pw2': 'delta_w', 'delta_g_final': 'delta_w', 'new_m_c_ctx': 'new_m', 'new_m_w_mod': 'new_m', 'new_m_b_mod': 'new_m', 'new_m_g_mix': 'new_m', 'new_m_g_ffn': 'new_m', 'new_m_w_ffn_in': 'new_m', 'new_m_w_ffn_out': 'new_m', 'new_m_w_in': 'new_m', 'new_m_q_gain': 'new_m', 'new_m_k_gain': 'new_m', 'new_m_w_sp': 'new_m', 'new_m_b_sp': 'new_m', 'new_m_w_out': 'new_m', 'new_m_w_pw1': 'new_m', 'new_m_b_pw1': 'new_m', 'new_m_w_dw': 'new_m', 'new_m_b_dw': 'new_m', 'new_m_ln_g': 'new_m', 'new_m_ln_b': 'new_m', 'new_m_w_pw2': 'new_m', 'new_m_b_pw2': 'new_m', 'new_m_g_final': 'new_m', 'new_v_c_ctx': 'new_v', 'new_v_w_mod': 'new_v', 'new_v_b_mod': 'new_v', 'new_v_g_mix': 'new_v', 'new_v_g_ffn': 'new_v', 'new_v_w_ffn_in': 'new_v', 'new_v_w_ffn_out': 'new_v', 'new_v_w_in': 'new_v', 'new_v_q_gain': 'new_v', 'new_v_k_gain': 'new_v', 'new_v_w_sp': 'new_v', 'new_v_b_sp': 'new_v', 'new_v_w_out': 'new_v', 'new_v_w_pw1': 'new_v', 'new_v_b_pw1': 'new_v', 'new_v_w_dw': 'new_v', 'new_v_b_dw': 'new_v', 'new_v_ln_g': 'new_v', 'new_v_ln_b': 'new_v', 'new_v_w_pw2': 'new_v', 'new_v_b_pw2': 'new_v', 'new_v_g_final': 'new_v'}


def _forward(args):
    return _fwd_reference(*[args[k] for k in FWD_PARAMS])


def _output_shape():
    out = _jax.eval_shape(lambda: _forward(_fwd_setup_inputs(0)))
    return out.shape, out.dtype

N_MICROBATCH = 1
ADAM_LR = 0.001
ADAM_B1 = 0.9
ADAM_B2 = 0.999
ADAM_EPS = 1e-08
ADAM_WD = 0.01
ADAM_STEP = 10
PER_EXAMPLE_BATCH_AXIS = {'x': 0, 'c': 0, 'ctx': 0, 'loss_target': 0}
SHARED_INPUTS = []
_WEIGHT_DTYPES = {'c_ctx': _jnp.float32, 'w_mod': _jnp.float32, 'b_mod': _jnp.float32, 'g_mix': _jnp.float32, 'g_ffn': _jnp.float32, 'w_ffn_in': _jnp.float32, 'w_ffn_out': _jnp.float32, 'w_in': _jnp.float32, 'q_gain': _jnp.float32, 'k_gain': _jnp.float32, 'w_sp': _jnp.float32, 'b_sp': _jnp.float32, 'w_out': _jnp.float32, 'w_pw1': _jnp.float32, 'b_pw1': _jnp.float32, 'w_dw': _jnp.float32, 'b_dw': _jnp.float32, 'ln_g': _jnp.float32, 'ln_b': _jnp.float32, 'w_pw2': _jnp.float32, 'b_pw2': _jnp.float32, 'g_final': _jnp.float32}
MOMENT_SCALE = {'c_ctx': 6.902832e-03, 'w_mod': 5.044975e-02, 'b_mod': 8.483850e-02, 'g_mix': 4.345703e-02, 'g_ffn': 5.436987e-02, 'w_ffn_in': 2.313693e-02, 'w_ffn_out': 3.776482e-02, 'w_in': 4.125440e-02, 'q_gain': 1.328430e-02, 'k_gain': 1.366246e-02, 'w_sp': 4.067269e-02, 'b_sp': 4.154194e-02, 'w_out': 4.170216e-02, 'w_pw1': 2.479555e-02, 'b_pw1': 2.463972e-02, 'w_dw': 3.240562e-02, 'b_dw': 5.546279e-02, 'ln_g': 3.855990e-02, 'ln_b': 3.209889e-02, 'w_pw2': 3.146977e-02, 'b_pw2': 5.698505e-02, 'g_final': 3.205569e+01}


def _to_microbatches(a, axis):
    t = _jnp.moveaxis(a, axis, 0)
    t = t.reshape((N_MICROBATCH, t.shape[0] // N_MICROBATCH) + t.shape[1:])
    return _jnp.moveaxis(t, 1, axis + 1)


def setup_inputs(seed: int = 0) -> dict:
    inp = _fwd_setup_inputs(seed)
    key = _jax.random.fold_in(_jax.random.key(seed), 7919)
    shape, _ = _output_shape()
    out = dict(inp)
    out["loss_target"] = _jax.random.normal(_jax.random.fold_in(key, 0), shape, _jnp.float32)
    for i, name in enumerate(TWIN_WEIGHTS):
        w = inp[name].astype(_jnp.float32)
        if MOMENT_SCALE is None:
            s = _jnp.sqrt(_jnp.mean(_jnp.square(w)) + 1e-30)
        else:
            s = MOMENT_SCALE[name]
        km, kv = _jax.random.split(_jax.random.fold_in(key, i + 1))
        out[name] = w
        out["m_" + name] = s * _jax.random.normal(km, w.shape, _jnp.float32)
        out["v_" + name] = (s * s) * _jax.random.uniform(kv, w.shape, _jnp.float32, 0.5, 1.5)
    if N_MICROBATCH > 1:
        for name, axis in PER_EXAMPLE_BATCH_AXIS.items():
            out[name] = _to_microbatches(out[name], axis)
    return {'x': out['x'], 'c': out['c'], 'ctx': out['ctx'], 'c_ctx': out['c_ctx'], 'w_mod': out['w_mod'], 'b_mod': out['b_mod'], 'g_mix': out['g_mix'], 'g_ffn': out['g_ffn'], 'w_ffn_in': out['w_ffn_in'], 'w_ffn_out': out['w_ffn_out'], 'w_in': out['w_in'], 'q_gain': out['q_gain'], 'k_gain': out['k_gain'], 'w_sp': out['w_sp'], 'b_sp': out['b_sp'], 'w_out': out['w_out'], 'w_pw1': out['w_pw1'], 'b_pw1': out['b_pw1'], 'w_dw': out['w_dw'], 'b_dw': out['b_dw'], 'ln_g': out['ln_g'], 'ln_b': out['ln_b'], 'w_pw2': out['w_pw2'], 'b_pw2': out['b_pw2'], 'g_final': out['g_final'], 'loss_target': out['loss_target'], 'm_c_ctx': out['m_c_ctx'], 'm_w_mod': out['m_w_mod'], 'm_b_mod': out['m_b_mod'], 'm_g_mix': out['m_g_mix'], 'm_g_ffn': out['m_g_ffn'], 'm_w_ffn_in': out['m_w_ffn_in'], 'm_w_ffn_out': out['m_w_ffn_out'], 'm_w_in': out['m_w_in'], 'm_q_gain': out['m_q_gain'], 'm_k_gain': out['m_k_gain'], 'm_w_sp': out['m_w_sp'], 'm_b_sp': out['m_b_sp'], 'm_w_out': out['m_w_out'], 'm_w_pw1': out['m_w_pw1'], 'm_b_pw1': out['m_b_pw1'], 'm_w_dw': out['m_w_dw'], 'm_b_dw': out['m_b_dw'], 'm_ln_g': out['m_ln_g'], 'm_ln_b': out['m_ln_b'], 'm_w_pw2': out['m_w_pw2'], 'm_b_pw2': out['m_b_pw2'], 'm_g_final': out['m_g_final'], 'v_c_ctx': out['v_c_ctx'], 'v_w_mod': out['v_w_mod'], 'v_b_mod': out['v_b_mod'], 'v_g_mix': out['v_g_mix'], 'v_g_ffn': out['v_g_ffn'], 'v_w_ffn_in': out['v_w_ffn_in'], 'v_w_ffn_out': out['v_w_ffn_out'], 'v_w_in': out['v_w_in'], 'v_q_gain': out['v_q_gain'], 'v_k_gain': out['v_k_gain'], 'v_w_sp': out['v_w_sp'], 'v_b_sp': out['v_b_sp'], 'v_w_out': out['v_w_out'], 'v_w_pw1': out['v_w_pw1'], 'v_b_pw1': out['v_b_pw1'], 'v_w_dw': out['v_w_dw'], 'v_b_dw': out['v_b_dw'], 'v_ln_g': out['v_ln_g'], 'v_ln_b': out['v_ln_b'], 'v_w_pw2': out['v_w_pw2'], 'v_b_pw2': out['v_b_pw2'], 'v_g_final': out['v_g_final']}


def _loss(weights, diff, rest, loss_target):
    with _jax.named_scope("forward"):
        args = {**rest, TWIN_DIFF_INPUT: diff, **{k: w.astype(_WEIGHT_DTYPES[k]) for k, w in weights.items()}}
        y = _forward(args)
    with _jax.named_scope("loss_head"):
        err = _jnp.square(y.astype(_jnp.float32) - loss_target)
        return 0.5 * _jnp.sum(_jnp.mean(err, axis=-1)) if err.ndim else 0.5 * err


def _adamw(w, g, m, v):
    m = ADAM_B1 * m + (1.0 - ADAM_B1) * g
    v = ADAM_B2 * v + (1.0 - ADAM_B2) * _jnp.square(g)
    m_hat = m / (1.0 - ADAM_B1 ** ADAM_STEP)
    v_hat = v / (1.0 - ADAM_B2 ** ADAM_STEP)
    delta = -ADAM_LR * (m_hat / (_jnp.sqrt(v_hat) + ADAM_EPS) + ADAM_WD * w)
    return delta, m, v


def reference(x, c, ctx, c_ctx, w_mod, b_mod, g_mix, g_ffn, w_ffn_in, w_ffn_out, w_in, q_gain, k_gain, w_sp, b_sp, w_out, w_pw1, b_pw1, w_dw, b_dw, ln_g, ln_b, w_pw2, b_pw2, g_final, loss_target, m_c_ctx, m_w_mod, m_b_mod, m_g_mix, m_g_ffn, m_w_ffn_in, m_w_ffn_out, m_w_in, m_q_gain, m_k_gain, m_w_sp, m_b_sp, m_w_out, m_w_pw1, m_b_pw1, m_w_dw, m_b_dw, m_ln_g, m_ln_b, m_w_pw2, m_b_pw2, m_g_final, v_c_ctx, v_w_mod, v_b_mod, v_g_mix, v_g_ffn, v_w_ffn_in, v_w_ffn_out, v_w_in, v_q_gain, v_k_gain, v_w_sp, v_b_sp, v_w_out, v_w_pw1, v_b_pw1, v_w_dw, v_b_dw, v_ln_g, v_ln_b, v_w_pw2, v_b_pw2, v_g_final):
    given = dict(x=x, c=c, ctx=ctx, c_ctx=c_ctx, w_mod=w_mod, b_mod=b_mod, g_mix=g_mix, g_ffn=g_ffn, w_ffn_in=w_ffn_in, w_ffn_out=w_ffn_out, w_in=w_in, q_gain=q_gain, k_gain=k_gain, w_sp=w_sp, b_sp=b_sp, w_out=w_out, w_pw1=w_pw1, b_pw1=b_pw1, w_dw=w_dw, b_dw=b_dw, ln_g=ln_g, ln_b=ln_b, w_pw2=w_pw2, b_pw2=b_pw2, g_final=g_final, loss_target=loss_target, m_c_ctx=m_c_ctx, m_w_mod=m_w_mod, m_b_mod=m_b_mod, m_g_mix=m_g_mix, m_g_ffn=m_g_ffn, m_w_ffn_in=m_w_ffn_in, m_w_ffn_out=m_w_ffn_out, m_w_in=m_w_in, m_q_gain=m_q_gain, m_k_gain=m_k_gain, m_w_sp=m_w_sp, m_b_sp=m_b_sp, m_w_out=m_w_out, m_w_pw1=m_w_pw1, m_b_pw1=m_b_pw1, m_w_dw=m_w_dw, m_b_dw=m_b_dw, m_ln_g=m_ln_g, m_ln_b=m_ln_b, m_w_pw2=m_w_pw2, m_b_pw2=m_b_pw2, m_g_final=m_g_final, v_c_ctx=v_c_ctx, v_w_mod=v_w_mod, v_b_mod=v_b_mod, v_g_mix=v_g_mix, v_g_ffn=v_g_ffn, v_w_ffn_in=v_w_ffn_in, v_w_ffn_out=v_w_ffn_out, v_w_in=v_w_in, v_q_gain=v_q_gain, v_k_gain=v_k_gain, v_w_sp=v_w_sp, v_b_sp=v_b_sp, v_w_out=v_w_out, v_w_pw1=v_w_pw1, v_b_pw1=v_b_pw1, v_w_dw=v_w_dw, v_b_dw=v_b_dw, v_ln_g=v_ln_g, v_ln_b=v_ln_b, v_w_pw2=v_w_pw2, v_b_pw2=v_b_pw2, v_g_final=v_g_final)
    weights = {n: given[n] for n in TWIN_WEIGHTS}
    shared = {n: given[n] for n in SHARED_INPUTS}
    per_example = {n: given[n] for n in ['x', 'c', 'ctx']}
    grad_fn = _jax.value_and_grad(_loss, argnums=(0, 1))

    def one_microbatch(ex, loss_target):
        ex = dict(ex)
        diff = ex.pop(TWIN_DIFF_INPUT)
        return grad_fn(weights, diff, {**shared, **ex}, loss_target)

    if N_MICROBATCH == 1:
        loss, (grad_w, grad_x) = one_microbatch(per_example, given["loss_target"])
    else:
        def body(carry, xs):
            loss_sum, grad_sum = carry
            l_k, (gw_k, gx_k) = one_microbatch(xs[0], xs[1])
            with _jax.named_scope("update"):
                return (loss_sum + l_k, _jax.tree.map(_jnp.add, grad_sum, gw_k)), gx_k

        init = (_jnp.zeros((), _jnp.float32), _jax.tree.map(_jnp.zeros_like, weights))
        (loss, grad_w), grad_x = _jax.lax.scan(body, init, (per_example, given["loss_target"]))
    with _jax.named_scope("update"):
        delta_w, new_m, new_v = {}, {}, {}
        for n in TWIN_WEIGHTS:
            delta_w[n], new_m[n], new_v[n] = _adamw(weights[n], grad_w[n], given["m_" + n], given["v_" + n])
    return (loss, grad_x, *[grad_w[n] for n in TWIN_WEIGHTS], *[delta_w[n] for n in TWIN_WEIGHTS],
            *[new_m[n] for n in TWIN_WEIGHTS], *[new_v[n] for n in TWIN_WEIGHTS])
```

```python
import functools
import math

import jax
import jax.numpy as jnp
from jax import lax
from jax.experimental import pallas as pl
from jax.experimental.pallas import tpu as pltpu

F32 = jnp.float32
BF16 = jnp.bfloat16
MESH = pl.DeviceIdType.MESH

D_MODEL = 1024
HEAD_DIM = 64
N_Q_HEADS = 8
N_KV_HEADS = 2
Q_GROUP = N_Q_HEADS // N_KV_HEADS
ATTN_W = N_Q_HEADS * HEAD_DIM
KV_W = N_KV_HEADS * HEAD_DIM
QKV_W = ATTN_W + 2 * KV_W
SG_W = D_MODEL - ATTN_W
N_SG = 4
CHUNK = 128
GRID_W = 64
ROPE_THETA = 10000.0
CONV_W = 31
CONV_HALO = 16
D_FF = 2816
MOD_W = 6 * D_MODEL
N_CHIP = 4
N_DEV = 8
FF_BLK = 2 * D_FF // N_CHIP
PW1_BLK = 2 * D_MODEL // N_CHIP
MOD_BLK = MOD_W // N_CHIP
EPS = 1e-6
GELU_C = math.sqrt(2.0 / math.pi)
GELU_A = 0.044715

ADAM_LR = 0.001
ADAM_B1 = 0.9
ADAM_B2 = 0.999
ADAM_EPS = 1e-08
ADAM_WD = 0.01
ADAM_STEP = 10

VMEM_LIMIT_BYTES = 48 * 1024 * 1024

WEIGHT_NAMES = ['c_ctx', 'w_mod', 'b_mod', 'g_mix', 'g_ffn', 'w_ffn_in', 'w_ffn_out', 'w_in', 'q_gain', 'k_gain',
                'w_sp', 'b_sp', 'w_out', 'w_pw1', 'b_pw1', 'w_dw', 'b_dw', 'ln_g', 'ln_b', 'w_pw2', 'b_pw2',
                'g_final']


def _params():
    return pltpu.CompilerParams(vmem_limit_bytes=VMEM_LIMIT_BYTES)


def _ri(arr, width=None, col_block=0, row_off=0):
    return (arr, arr.shape[1] if width is None else width, col_block, row_off)


def _rowwise(name, body, n_tiles, tm, row_ins, vec_ins, row_outs, vec_outs):
    nri, nvi, nro = len(row_ins), len(vec_ins), len(row_outs)

    def kern(*refs):
        ri = refs[:nri]
        vi = refs[nri:nri + nvi]
        ro = refs[nri + nvi:nri + nvi + nro]
        vo = refs[nri + nvi + nro:]
        if vo:
            @pl.when(pl.program_id(0) == 0)
            def _():
                for r in vo:
                    r[...] = jnp.zeros(r.shape, r.dtype)
        body(ri, vi, ro, vo)

    in_specs = [pl.BlockSpec((tm, bw), lambda i, cb=cb, off=off: (i + off, cb)) for (_, bw, cb, off) in row_ins]
    in_specs += [pl.BlockSpec(a.shape, lambda i, nd=a.ndim: (0,) * nd) for a in vec_ins]
    out_specs = [pl.BlockSpec((tm, w), lambda i: (i, 0)) for (w, _) in row_outs]
    out_specs += [pl.BlockSpec(s, lambda i, nd=len(s): (0,) * nd) for s in vec_outs]
    out_shape = [jax.ShapeDtypeStruct((n_tiles * tm, w), dt) for (w, dt) in row_outs]
    out_shape += [jax.ShapeDtypeStruct(s, F32) for s in vec_outs]
    return pl.pallas_call(kern, name=name, grid=(n_tiles,), in_specs=in_specs, out_specs=out_specs,
                          out_shape=out_shape, compiler_params=_params())(*[a for (a, _, _, _) in row_ins], *vec_ins)


def _colsum(v):
    return jnp.sum(v, axis=0, keepdims=True)


def _rowmean(v):
    return jnp.mean(v, axis=-1, keepdims=True)


def _rms(h):
    r = lax.rsqrt(_rowmean(h * h) + EPS)
    return h * r, r


def _rms_bwd(n, r, gy):
    return r * (gy - n * _rowmean(gy * n))


def _ln_stats(v):
    xc = v - _rowmean(v)
    r = lax.rsqrt(_rowmean(xc * xc) + EPS)
    return xc * r, r


def _ln_bwd(xh, r, dxh):
    return r * (dxh - _rowmean(dxh) - xh * _rowmean(dxh * xh))


def _sigmoid(v):
    return 1.0 / (1.0 + jnp.exp(-v))


def _gelu(v):
    t = jnp.tanh(GELU_C * (v + GELU_A * (v * v * v)))
    return 0.5 * v * (1.0 + t), t


def _gelu_grad(v, t):
    return 0.5 * (1.0 + t) + 0.5 * v * (1.0 - t * t) * (GELU_C * (1.0 + 3.0 * GELU_A * (v * v)))


def _modnorm_fwd(name, h, g, sc, sh, tm):
    def body(ri, vi, ro, vo):
        n, _ = _rms(ri[0][...])
        ro[0][...] = ((n * vi[0][...]) * (1.0 + vi[1][...]) + vi[2][...]).astype(BF16)

    return _rowwise(name, body, h.shape[0] // tm, tm, [_ri(h)], [g, sc, sh], [(D_MODEL, BF16)], [])[0]


def _modnorm_bwd(name, h, dxs, dh_in, g, sc, tm):
    ndx = len(dxs)

    def body(ri, vi, ro, vo):
        dxm = ri[1][...].astype(F32)
        for r in ri[2:1 + ndx]:
            dxm = dxm + r[...].astype(F32)
        gv = vi[0][...]
        n, r = _rms(ri[0][...])
        vo[0][...] += _colsum(dxm)
        vo[1][...] += _colsum(dxm * (n * gv))
        dy = dxm * (1.0 + vi[1][...])
        vo[2][...] += _colsum(dy * n)
        if dh_in is not None:
            ro[0][...] = ri[1 + ndx][...] + _rms_bwd(n, r, dy * gv)

    row_ins = [_ri(h)] + list(dxs) + ([_ri(dh_in)] if dh_in is not None else [])
    row_outs = [(D_MODEL, F32)] if dh_in is not None else []
    outs = _rowwise(name, body, h.shape[0] // tm, tm, row_ins, [g, sc], row_outs, [(1, D_MODEL)] * 3)
    if dh_in is None:
        return (None, *outs)
    return tuple(outs)


def _resgate_fwd(name, h, y, gt, tm):
    def body(ri, vi, ro, vo):
        ro[0][...] = ri[0][...] + vi[0][...] * ri[1][...]

    return _rowwise(name, body, h.shape[0] // tm, tm, [_ri(h), _ri(y)], [gt], [(D_MODEL, F32)], [])[0]


def _resgate_bwd(name, dh, y, gt, tm):
    def body(ri, vi, ro, vo):
        d = ri[0][...]
        dy = d * vi[0][...]
        ro[0][...] = dy.astype(BF16)
        vo[0][...] += _colsum(d * ri[1][...])
        vo[1][...] += _colsum(dy)

    return _rowwise(name, body, dh.shape[0] // tm, tm, [_ri(dh), _ri(y)], [gt], [(D_MODEL, BF16)],
                    [(1, D_MODEL)] * 2)


def _swiglu_fwd(name, gu, tm):
    def body(ri, vi, ro, vo):
        g = ri[0][...]
        ro[0][...] = (g * _sigmoid(g) * ri[1][...]).astype(BF16)

    return _rowwise(name, body, gu.shape[0] // tm, tm, [_ri(gu, D_FF, 0), _ri(gu, D_FF, 1)], [], [(D_FF, BF16)], [])[0]


def _swiglu_bwd(name, gu, da, tm):
    def body(ri, vi, ro, vo):
        g = ri[0][...]
        u = ri[1][...]
        d = ri[2][...]
        sg = _sigmoid(g)
        ro[0][:, :D_FF] = (d * u * (sg * (1.0 + g * (1.0 - sg)))).astype(BF16)
        ro[0][:, D_FF:] = (d * (g * sg)).astype(BF16)

    return _rowwise(name, body, gu.shape[0] // tm, tm, [_ri(gu, D_FF, 0), _ri(gu, D_FF, 1), _ri(da)], [],
                    [(2 * D_FF, BF16)], [])[0]


def _final_loss(name, h, tgt, g, tm):
    def body(ri, vi, ro, vo):
        gv = vi[0][...]
        n, r = _rms(ri[0][...])
        e = n * gv - ri[1][...]
        vo[1][...] += _colsum(e * e) * (0.5 / D_MODEL)
        dout = e * (1.0 / D_MODEL)
        vo[0][...] += _colsum(dout * n)
        ro[0][...] = _rms_bwd(n, r, dout * gv)

    return _rowwise(name, body, h.shape[0] // tm, tm, [_ri(h), _ri(tgt)], [g], [(D_MODEL, F32)],
                    [(1, D_MODEL)] * 2)


def _head_mean(v, bd):
    hi = v.astype(BF16)
    lo = (v - hi.astype(F32)).astype(BF16)
    s = jnp.dot(hi, bd, preferred_element_type=F32) + jnp.dot(lo, bd, preferred_element_type=F32)
    return s * (1.0 / HEAD_DIM)


def _swap16(v):
    w = v.shape[1]
    lane = lax.broadcasted_iota(jnp.int32, v.shape, 1)
    return jnp.where((lane & 16) == 0, pltpu.roll(v, w - 16, 1), pltpu.roll(v, 16, 1))


def _rope(v, cos, sin):
    return v * cos + _swap16(v) * sin


def _rope_bwd(d, cos, sin):
    return d * cos + _swap16(d * sin)


def _qk_prep_fwd(name, p_qkv, cos, sin, gq, gk, bdq, bdk, tm):
    def body(ri, vi, ro, vo):
        pq = ri[0][...]
        pkv = ri[1][...]
        cosv = ri[2][...]
        sinv = ri[3][...]
        rq = lax.rsqrt(_head_mean(pq * pq, vi[2][...]) + EPS)
        ro[0][...] = (_rope(pq * rq * vi[0][...], cosv, sinv) * (HEAD_DIM ** -0.5)).astype(BF16)
        pk = pkv[:, :KV_W]
        rk = lax.rsqrt(_head_mean(pk * pk, vi[3][...]) + EPS)
        ro[1][...] = _rope(pk * rk * vi[1][...], cosv[:, :KV_W], sinv[:, :KV_W]).astype(BF16)
        ro[2][...] = pkv[:, KV_W:].astype(BF16)

    return _rowwise(name, body, p_qkv.shape[0] // tm, tm,
                    [_ri(p_qkv, ATTN_W, 0), _ri(p_qkv, 2 * KV_W, 2), _ri(cos), _ri(sin)], [gq, gk, bdq, bdk],
                    [(ATTN_W, BF16), (KV_W, BF16), (KV_W, BF16)], [])


def _qk_prep_bwd(name, p_qkv, cos, sin, dq, dk, dv, gq, gk, bdq, bdk, tm):
    def norm_rope_bwd(p, gain, bd, cosv, sinv, dout):
        r = lax.rsqrt(_head_mean(p * p, bd) + EPS)
        n = p * r
        dqn = _rope_bwd(dout, cosv, sinv)
        gy = dqn * gain
        return r * (gy - n * _head_mean(gy * n, bd)), _colsum(dqn * n)

    def body(ri, vi, ro, vo):
        pkv = ri[1][...]
        cosv = ri[2][...]
        sinv = ri[3][...]
        dxq, dgq = norm_rope_bwd(ri[0][...], vi[0][...], vi[2][...], cosv, sinv, ri[4][...] * (HEAD_DIM ** -0.5))
        dxk, dgk = norm_rope_bwd(pkv[:, :KV_W], vi[1][...], vi[3][...], cosv[:, :KV_W], sinv[:, :KV_W], ri[5][...])
        ro[0][:, :ATTN_W] = dxq.astype(BF16)
        ro[0][:, ATTN_W:ATTN_W + KV_W] = dxk.astype(BF16)
        ro[0][:, ATTN_W + KV_W:] = ri[6][...].astype(BF16)
        vo[0][...] += dgq
        vo[1][...] += dgk

    return _rowwise(name, body, p_qkv.shape[0] // tm, tm,
                    [_ri(p_qkv, ATTN_W, 0), _ri(p_qkv, 2 * KV_W, 2), _ri(cos), _ri(sin), _ri(dq), _ri(dk), _ri(dv)],
                    [gq, gk, bdq, bdk], [(QKV_W, BF16)], [(1, ATTN_W), (1, KV_W)])


def _sg_fwd(name, p_sg, w_sp, bsp, tm):
    def body(ri, vi, ro, vo):
        p = ri[0][...]
        gu, _ = _gelu(p[:, :SG_W])
        gv, _ = _gelu(p[:, SG_W:])
        for g in range(N_SG):
            sl = slice(g * CHUNK, (g + 1) * CHUNK)
            vn, _ = _ln_stats(gv[:, sl])
            vnb = vn.astype(BF16)
            for m in range(tm // CHUNK):
                rs = slice(m * CHUNK, (m + 1) * CHUNK)
                mixed = jnp.dot(vi[0][g], vnb[rs], preferred_element_type=F32) + vi[1][:, sl]
                ro[0][rs, sl] = (gu[rs, sl] * mixed).astype(BF16)

    return _rowwise(name, body, p_sg.shape[0] // tm, tm, [_ri(p_sg)], [w_sp, bsp], [(SG_W, BF16)], [])[0]


def _sg_bwd(name, p_sg, dcat, w_sp, w_sp_t, bsp, tm):
    def body(ri, vi, ro, vo):
        p = ri[0][...]
        dsg = ri[1][...]
        su = p[:, :SG_W]
        sv = p[:, SG_W:]
        gu, tu = _gelu(su)
        gv, tv = _gelu(sv)
        for g in range(N_SG):
            sl = slice(g * CHUNK, (g + 1) * CHUNK)
            vn, r = _ln_stats(gv[:, sl])
            vnb = vn.astype(BF16)
            for m in range(tm // CHUNK):
                rs = slice(m * CHUNK, (m + 1) * CHUNK)
                mixed = jnp.dot(vi[0][g], vnb[rs], preferred_element_type=F32) + vi[2][:, sl]
                d_o = dsg[rs, sl]
                dm = d_o * gu[rs, sl]
                vo[1][:, sl] += dm
                dmb = dm.astype(BF16)
                vo[0][g] += lax.dot_general(dmb, vnb[rs], (((1,), (1,)), ((), ())), preferred_element_type=F32)
                dvn = jnp.dot(vi[1][g], dmb, preferred_element_type=F32)
                dgv = _ln_bwd(vn[rs], r[rs], dvn)
                ro[0][rs, sl] = (d_o * mixed * _gelu_grad(su[rs, sl], tu[rs, sl])).astype(BF16)
                ro[0][rs, SG_W + g * CHUNK:SG_W + (g + 1) * CHUNK] = (
                    dgv * _gelu_grad(sv[rs, sl], tv[rs, sl])).astype(BF16)

    return _rowwise(name, body, p_sg.shape[0] // tm, tm, [_ri(p_sg), _ri(dcat, SG_W, 1)], [w_sp, w_sp_t, bsp],
                    [(2 * SG_W, BF16)], [(N_SG, CHUNK, CHUNK), (CHUNK, SG_W)])


def _glu_fwd(name, ag, b, tm):
    def body(ri, vi, ro, vo):
        v = ri[0][...] + vi[0][...]
        ro[0][...] = v[:, :D_MODEL] * _sigmoid(v[:, D_MODEL:])

    return _rowwise(name, body, ag.shape[0] // tm, tm, [_ri(ag)], [b], [(D_MODEL, F32)], [])[0]


def _glu_bwd(name, ag, dhg, b, tm):
    def body(ri, vi, ro, vo):
        v = ri[0][...] + vi[0][...]
        d = ri[1][...]
        a = v[:, :D_MODEL]
        sg = _sigmoid(v[:, D_MODEL:])
        da = d * sg
        dgate = d * a * sg * (1.0 - sg)
        ro[0][:, :D_MODEL] = da.astype(BF16)
        ro[0][:, D_MODEL:] = dgate.astype(BF16)
        vo[0][:, :D_MODEL] += _colsum(da)
        vo[0][:, D_MODEL:] += _colsum(dgate)

    return _rowwise(name, body, ag.shape[0] // tm, tm, [_ri(ag), _ri(dhg)], [b], [(2 * D_MODEL, BF16)],
                    [(1, 2 * D_MODEL)])


def _lnsilu_fwd(name, hc, g, b, tm):
    def body(ri, vi, ro, vo):
        xh, _ = _ln_stats(ri[0][...])
        ln = xh * vi[0][...] + vi[1][...]
        ro[0][...] = (ln * _sigmoid(ln)).astype(BF16)

    return _rowwise(name, body, hc.shape[0] // tm, tm, [_ri(hc)], [g, b], [(D_MODEL, BF16)], [])[0]


def _lnsilu_bwd(name, hc, ds, g, b, tm):
    def body(ri, vi, ro, vo):
        gv = vi[0][...]
        xh, r = _ln_stats(ri[0][...])
        ln = xh * gv + vi[1][...]
        sg = _sigmoid(ln)
        dln = ri[1][...] * (sg * (1.0 + ln * (1.0 - sg)))
        vo[0][...] += _colsum(dln * xh)
        vo[1][...] += _colsum(dln)
        ro[0][...] = _ln_bwd(xh, r, dln * gv)

    return _rowwise(name, body, hc.shape[0] // tm, tm, [_ri(hc), _ri(ds)], [g, b], [(D_MODEL, F32)],
                    [(1, D_MODEL)] * 2)


def _conv_blocks(seq):
    cb = 256
    tt = 128 if seq % 128 == 0 else seq
    return cb, tt


def _fill_padded(pad_ref, x_ref, seq):
    zeros = jnp.zeros((CONV_HALO, pad_ref.shape[1]), F32)
    pad_ref[0:CONV_HALO, :] = zeros
    pad_ref[seq + CONV_HALO:seq + 2 * CONV_HALO, :] = zeros
    pad_ref[CONV_HALO:seq + CONV_HALO, :] = x_ref[...]


def _dwconv(name, xin, w, b):
    seq = xin.shape[0]
    cb, tt = _conv_blocks(seq)

    def kern(x_ref, w_ref, b_ref, o_ref, pad_ref):
        _fill_padded(pad_ref, x_ref, seq)
        wv = w_ref[...]
        bv = b_ref[...]

        def step(t, carry):
            base = pl.multiple_of(t * tt, tt)
            win = pad_ref[pl.ds(base, tt + 2 * CONV_HALO), :]
            acc = jnp.zeros((tt, cb), F32) + bv
            for j in range(CONV_W):
                acc = acc + wv[j:j + 1, :] * win[j + 1:j + 1 + tt, :]
            o_ref[pl.ds(base, tt), :] = acc
            return carry

        lax.fori_loop(0, seq // tt, step, 0)

    return pl.pallas_call(
        kern, name=name, grid=(D_MODEL // cb,),
        in_specs=[pl.BlockSpec((seq, cb), lambda j: (0, j)), pl.BlockSpec((CONV_W, cb), lambda j: (0, j)),
                  pl.BlockSpec((1, cb), lambda j: (0, j))],
        out_specs=pl.BlockSpec((seq, cb), lambda j: (0, j)),
        out_shape=jax.ShapeDtypeStruct((seq, D_MODEL), F32),
        scratch_shapes=[pltpu.VMEM((seq + 2 * CONV_HALO, cb), F32)],
        compiler_params=_params())(xin, w, b)


def _dwconv_wgrad(name, xin, dout):
    seq = xin.shape[0]
    cb, tt = _conv_blocks(seq)

    def kern(x_ref, d_ref, dw_ref, db_ref, pad_ref):
        _fill_padded(pad_ref, x_ref, seq)
        dw_ref[...] = jnp.zeros(dw_ref.shape, F32)
        db_ref[...] = jnp.zeros(db_ref.shape, F32)

        def step(t, carry):
            base = pl.multiple_of(t * tt, tt)
            win = pad_ref[pl.ds(base, tt + 2 * CONV_HALO), :]
            d = d_ref[pl.ds(base, tt), :]
            db_ref[...] += _colsum(d)
            for j in range(CONV_W):
                dw_ref[j:j + 1, :] += _colsum(d * win[j + 1:j + 1 + tt, :])
            return carry

        lax.fori_loop(0, seq // tt, step, 0)

    return pl.pallas_call(
        kern, name=name, grid=(D_MODEL // cb,),
        in_specs=[pl.BlockSpec((seq, cb), lambda j: (0, j)), pl.BlockSpec((seq, cb), lambda j: (0, j))],
        out_specs=[pl.BlockSpec((CONV_W, cb), lambda j: (0, j)), pl.BlockSpec((1, cb), lambda j: (0, j))],
        out_shape=[jax.ShapeDtypeStruct((CONV_W, D_MODEL), F32), jax.ShapeDtypeStruct((1, D_MODEL), F32)],
        scratch_shapes=[pltpu.VMEM((seq + 2 * CONV_HALO, cb), F32)],
        compiler_params=_params())(xin, dout)


def _mm(name, a, b, out_sds, grid, a_spec, b_spec, o_spec, contract, k_axis=None, bias=None):
    dn = (contract, ((), ()))

    def kern(*refs):
        if bias is None:
            a_ref, b_ref, o_ref = refs
        else:
            a_ref, b_ref, bias_ref, o_ref = refs
        p = lax.dot_general(a_ref[...].astype(BF16), b_ref[...].astype(BF16), dn, preferred_element_type=F32)
        if bias is not None:
            p = p + bias_ref[...]
        if k_axis is None:
            o_ref[...] = p.astype(o_ref.dtype)
        else:
            k = pl.program_id(k_axis)

            @pl.when(k == 0)
            def _():
                o_ref[...] = p

            @pl.when(k > 0)
            def _():
                o_ref[...] += p

    in_specs = [a_spec, b_spec]
    args = [a, b]
    if bias is not None:
        in_specs.append(pl.BlockSpec(bias.shape, lambda *_: (0,) * bias.ndim))
        args.append(bias)
    return pl.pallas_call(kern, name=name, grid=grid, in_specs=in_specs, out_specs=o_spec, out_shape=out_sds,
                          compiler_params=_params())(*args)


def _mm_rows(name, a, b, tm, trans_b=False, bias=None):
    m, k = a.shape
    n = b.shape[0] if trans_b else b.shape[1]
    contract = ((1,), (1,)) if trans_b else ((1,), (0,))
    return _mm(name, a, b, jax.ShapeDtypeStruct((m, n), F32), (m // tm,),
               pl.BlockSpec((tm, k), lambda i: (i, 0)), pl.BlockSpec(b.shape, lambda i: (0, 0)),
               pl.BlockSpec((tm, n), lambda i: (i, 0)), contract, bias=bias)


def _mm_blk(name, a, bblk, tm):
    m, k = a.shape
    nb, _, nblk = bblk.shape
    return _mm(name, a, bblk, jax.ShapeDtypeStruct((m, nb * nblk), F32), (nb, m // tm),
               pl.BlockSpec((tm, k), lambda j, i: (i, 0)), pl.BlockSpec((None, k, nblk), lambda j, i: (j, 0, 0)),
               pl.BlockSpec((tm, nblk), lambda j, i: (i, j)), ((1,), (0,)))


def _mm_blk_t(name, a, bblk, tm):
    m = a.shape[0]
    nb, k, nblk = bblk.shape
    return _mm(name, a, bblk, jax.ShapeDtypeStruct((m, k), F32), (m // tm, nb),
               pl.BlockSpec((tm, nblk), lambda i, j: (i, j)), pl.BlockSpec((None, k, nblk), lambda i, j: (j, 0, 0)),
               pl.BlockSpec((tm, k), lambda i, j: (i, 0)), ((1,), (1,)), k_axis=1)


def _mm_wgrad(name, a, b, tk):
    t, k = a.shape
    n = b.shape[1]
    tn = n if k * n * 4 <= 4 * 1024 * 1024 else 512
    return _mm(name, a, b, jax.ShapeDtypeStruct((k, n), F32), (n // tn, t // tk),
               pl.BlockSpec((tk, k), lambda j, i: (i, 0)), pl.BlockSpec((tk, tn), lambda j, i: (i, j)),
               pl.BlockSpec((k, tn), lambda j, i: (0, j)), ((0,), (0,)), k_axis=1)


def _mm_wgrad_blk(name, a, b, nb, tk):
    t, k = a.shape
    nblk = b.shape[1] // nb
    return _mm(name, a, b, jax.ShapeDtypeStruct((nb, k, nblk), F32), (nb, t // tk),
               pl.BlockSpec((tk, k), lambda j, i: (i, 0)), pl.BlockSpec((tk, nblk), lambda j, i: (i, j)),
               pl.BlockSpec((None, k, nblk), lambda j, i: (j, 0, 0)), ((0,), (0,)), k_axis=1)


def _attn_fwd(name, q, k, v, tq):
    nh, s, hd = q.shape
    r = k.shape[1]

    def kern(q_ref, k_ref, v_ref, o_ref, lse_ref):
        sc = lax.dot_general(q_ref[...], k_ref[...], (((1,), (1,)), ((), ())), preferred_element_type=F32)
        m = jnp.max(sc, axis=-1, keepdims=True)
        p = jnp.exp(sc - m)
        l = jnp.sum(p, axis=-1, keepdims=True)
        o = jnp.dot(p.astype(BF16), v_ref[...], preferred_element_type=F32)
        o_ref[...] = o / l
        lse_ref[...] = m + jnp.log(l)

    return pl.pallas_call(
        kern, name=name, grid=(nh, s // tq),
        in_specs=[pl.BlockSpec((None, tq, hd), lambda h, i: (h, i, 0)),
                  pl.BlockSpec((None, r, hd), lambda h, i: (h // Q_GROUP, 0, 0)),
                  pl.BlockSpec((None, r, hd), lambda h, i: (h // Q_GROUP, 0, 0))],
        out_specs=[pl.BlockSpec((None, tq, hd), lambda h, i: (h, i, 0)),
                   pl.BlockSpec((None, tq, 1), lambda h, i: (h, i, 0))],
        out_shape=[jax.ShapeDtypeStruct((nh, s, hd), F32), jax.ShapeDtypeStruct((nh, s, 1), F32)],
        compiler_params=_params())(q, k, v)


def _attn_bwd(name, q, k, v, o, lse, do, tq):
    nh, s, hd = q.shape
    nkv, r, _ = k.shape

    def kern(q_ref, k_ref, v_ref, o_ref, lse_ref, do_ref, dq_ref, dk_ref, dv_ref):
        @pl.when((pl.program_id(1) == 0) & (pl.program_id(2) == 0))
        def _():
            dk_ref[...] = jnp.zeros(dk_ref.shape, F32)
            dv_ref[...] = jnp.zeros(dv_ref.shape, F32)

        qv = q_ref[...]
        kv = k_ref[...]
        dov = do_ref[...]
        delta = jnp.sum(dov * o_ref[...], axis=-1, keepdims=True)
        sc = lax.dot_general(qv, kv, (((1,), (1,)), ((), ())), preferred_element_type=F32)
        p = jnp.exp(sc - lse_ref[...])
        dob = dov.astype(BF16)
        dp = lax.dot_general(dob, v_ref[...], (((1,), (1,)), ((), ())), preferred_element_type=F32)
        dsb = (p * (dp - delta)).astype(BF16)
        dq_ref[...] = jnp.dot(dsb, kv, preferred_element_type=F32)
        dk_ref[...] += lax.dot_general(dsb, qv, (((0,), (0,)), ((), ())), preferred_element_type=F32)
        dv_ref[...] += lax.dot_general(p.astype(BF16), dob, (((0,), (0,)), ((), ())), preferred_element_type=F32)

    qmap = lambda j, g, i: (j * Q_GROUP + g, i, 0)
    kvmap = lambda j, g, i: (j, 0, 0)
    return pl.pallas_call(
        kern, name=name, grid=(nkv, Q_GROUP, s // tq),
        in_specs=[pl.BlockSpec((None, tq, hd), qmap), pl.BlockSpec((None, r, hd), kvmap),
                  pl.BlockSpec((None, r, hd), kvmap), pl.BlockSpec((None, tq, hd), qmap),
                  pl.BlockSpec((None, tq, 1), qmap), pl.BlockSpec((None, tq, hd), qmap)],
        out_specs=[pl.BlockSpec((None, tq, hd), qmap), pl.BlockSpec((None, r, hd), kvmap),
                   pl.BlockSpec((None, r, hd), kvmap)],
        out_shape=[jax.ShapeDtypeStruct((nh, s, hd), F32), jax.ShapeDtypeStruct((nkv, r, hd), F32),
                   jax.ShapeDtypeStruct((nkv, r, hd), F32)],
        compiler_params=_params())(q, k, v, o, lse, do)


def _mod_fwd(name, c16, wm, bm):
    n = wm.shape[2]

    def kern(c_ref, w_ref, b_ref, o_ref):
        cv = c_ref[...]
        a = (cv * _sigmoid(cv)).astype(BF16)
        o_ref[...] = jnp.dot(a, w_ref[...], preferred_element_type=F32) + b_ref[...]

    return pl.pallas_call(
        kern, name=name, grid=(2,),
        in_specs=[pl.BlockSpec(c16.shape, lambda l: (0, 0)), pl.BlockSpec((None, D_MODEL, n), lambda l: (l, 0, 0)),
                  pl.BlockSpec((None, 1, n), lambda l: (l, 0, 0))],
        out_specs=pl.BlockSpec((None, 16, n), lambda l: (l, 0, 0)),
        out_shape=jax.ShapeDtypeStruct((2, 16, n), F32), compiler_params=_params())(c16, wm, bm)


def _mod_wgrad(name, c16, dm):
    n = dm.shape[2]
    tn = 512

    def kern(c_ref, d_ref, o_ref):
        cv = c_ref[...]
        a = cv * _sigmoid(cv)
        o_ref[...] = lax.dot_general(a, d_ref[...], (((0,), (0,)), ((), ())), preferred_element_type=F32,
                                     precision=lax.Precision.HIGHEST)

    return pl.pallas_call(
        kern, name=name, grid=(2, n // tn),
        in_specs=[pl.BlockSpec(c16.shape, lambda l, j: (0, 0)), pl.BlockSpec((None, 16, tn), lambda l, j: (l, 0, j))],
        out_specs=pl.BlockSpec((None, D_MODEL, tn), lambda l, j: (l, 0, j)),
        out_shape=jax.ShapeDtypeStruct((2, D_MODEL, n), F32), compiler_params=_params())(c16, dm)


def _cctx_grad(name, parts, c_ctx):
    def kern(p_ref, c_ref, o_ref):
        d = p_ref[0, 0:1, :] + p_ref[2, 0:1, :] + p_ref[4, 0:1, :] + p_ref[6, 0:1, :]
        cv = c_ref[...]
        sg = _sigmoid(cv)
        o_ref[...] = d * (sg * (1.0 + cv * (1.0 - sg)))

    return pl.pallas_call(kern, name=name, out_shape=jax.ShapeDtypeStruct((1, D_MODEL), F32),
                          compiler_params=_params())(parts, c_ctx)


def _sum_slots(name, g, tr):
    n, rows, cols = g.shape

    def kern(g_ref, o_ref):
        acc = g_ref[0].astype(F32)
        for i in range(1, n):
            acc = acc + g_ref[i].astype(F32)
        o_ref[...] = acc

    return pl.pallas_call(kern, name=name, grid=(rows // tr,),
                          in_specs=[pl.BlockSpec((n, tr, cols), lambda i: (0, i, 0))],
                          out_specs=pl.BlockSpec((tr, cols), lambda i: (i, 0)),
                          out_shape=jax.ShapeDtypeStruct((rows, cols), F32), compiler_params=_params())(g)


def _row_tile(rows, cols, max_bytes=1024 * 1024):
    if rows * cols * 4 <= 2 * max_bytes:
        return rows
    best = None
    for t in range(16, rows + 1, 16):
        if rows % t == 0 and t * cols * 4 <= max_bytes:
            best = t
    assert best is not None, (rows, cols)
    return best


def _adamw(name, w, g, m, v):
    shape = w.shape
    cols = shape[-1]
    rows = w.size // cols
    tr = _row_tile(rows, cols)

    def kern(w_ref, g_ref, m_ref, v_ref, d_ref, nm_ref, nv_ref):
        gv = g_ref[...]
        m2 = ADAM_B1 * m_ref[...] + (1.0 - ADAM_B1) * gv
        v2 = ADAM_B2 * v_ref[...] + (1.0 - ADAM_B2) * (gv * gv)
        m_hat = m2 / (1.0 - ADAM_B1 ** ADAM_STEP)
        v_hat = v2 / (1.0 - ADAM_B2 ** ADAM_STEP)
        d_ref[...] = -ADAM_LR * (m_hat / (jnp.sqrt(v_hat) + ADAM_EPS) + ADAM_WD * w_ref[...])
        nm_ref[...] = m2
        nv_ref[...] = v2

    spec = pl.BlockSpec((tr, cols), lambda i: (i, 0))
    sds = jax.ShapeDtypeStruct((rows, cols), F32)
    outs = pl.pallas_call(kern, name=name, grid=(rows // tr,), in_specs=[spec] * 4, out_specs=[spec] * 3,
                          out_shape=[sds] * 3, compiler_params=_params())(
        w.reshape(rows, cols), g.reshape(rows, cols), m.reshape(rows, cols), v.reshape(rows, cols))
    return tuple(o.reshape(shape) for o in outs)


_ANY = pl.BlockSpec(memory_space=pl.ANY)


def _mesh_pos():
    return lax.axis_index("x"), lax.axis_index("y"), lax.axis_index("c")


def _allgather(name, items):
    n = len(items)

    def body(*refs):
        srcs = refs[:n]
        outs = refs[n:2 * n]
        send_sems, recv_sems, local_sems = refs[2 * n:]
        x, y, c = _mesh_pos()
        me = (x, y, c)
        sibling = (x, y, 1 - c)
        chips = [(1 - x, y), (x, 1 - y), (1 - x, 1 - y)]

        def slot(px, py, pc):
            return 4 * px + 2 * py + pc

        def copy(t, k, s, to, src=None):
            dst = outs[t].at[s]
            return pltpu.make_async_remote_copy(src_ref=dst if src is None else src, dst_ref=dst,
                                                send_sem=send_sems.at[t, k], recv_sem=recv_sems.at[t, k],
                                                device_id=to, device_id_type=MESH)

        mine = [pltpu.make_async_copy(srcs[t], outs[t].at[slot(*me)], local_sems.at[t]) for t in range(n)]
        for cp in mine:
            cp.start()
        first = []
        for t in range(n):
            first.append(copy(t, 0, slot(*me), sibling, src=srcs[t]))
            first += [copy(t, 1 + j, slot(*me), (px, py, c), src=srcs[t]) for j, (px, py) in enumerate(chips)]
        for cp in first:
            cp.start()
        passed = []
        for j, (px, py) in enumerate(chips):
            for t in range(n):
                copy(t, 1 + j, slot(px, py, c), me).wait_recv()
                fwd = copy(t, 4 + j, slot(px, py, c), sibling)
                fwd.start()
                passed.append(fwd)
        for t in range(n):
            copy(t, 0, slot(x, y, 1 - c), me).wait_recv()
            for j, (px, py) in enumerate(chips):
                copy(t, 4 + j, slot(px, py, 1 - c), me).wait_recv()
        for cp in first + passed:
            cp.wait_send()
        for cp in mine:
            cp.wait()

    return pl.pallas_call(
        body, name=name, in_specs=[_ANY] * n, out_specs=[_ANY] * n,
        out_shape=[jax.ShapeDtypeStruct((N_DEV,) + a.shape, a.dtype) for a in items],
        scratch_shapes=[pltpu.SemaphoreType.DMA((n, 7)), pltpu.SemaphoreType.DMA((n, 7)),
                        pltpu.SemaphoreType.DMA((n,))],
        compiler_params=_params())(*items)


def _rs_pair_send(name, gs):
    n = len(gs)

    def body(*refs):
        srcs = refs[:n]
        bufs = refs[n:2 * n]
        send_sems, recv_sems = refs[2 * n:]
        x, y, c = _mesh_pos()
        copies = []
        for t in range(n):
            for k in range(N_CHIP):
                copies.append(pltpu.make_async_remote_copy(
                    src_ref=srcs[t].at[k, 1 - c], dst_ref=bufs[t].at[k], send_sem=send_sems.at[t, k],
                    recv_sem=recv_sems.at[t, k], device_id=(x, y, 1 - c), device_id_type=MESH))
        for cp in copies:
            cp.start()
        for cp in copies:
            cp.wait()

    return pl.pallas_call(
        body, name=name, in_specs=[_ANY] * n, out_specs=[_ANY] * n,
        out_shape=[jax.ShapeDtypeStruct((N_CHIP,) + g.shape[2:], g.dtype) for g in gs],
        scratch_shapes=[pltpu.SemaphoreType.DMA((n, N_CHIP)), pltpu.SemaphoreType.DMA((n, N_CHIP))],
        compiler_params=_params())(*gs)


def _rs_pair_add(name, g, buf, cvec):
    _, _, r, cc = g.shape
    tr = _row_tile(r, cc)

    def kern(c_ref, g_ref, b_ref, o_ref):
        o_ref[...] = (g_ref[...] + b_ref[...]).astype(BF16)

    grid_spec = pltpu.PrefetchScalarGridSpec(
        num_scalar_prefetch=1, grid=(N_CHIP, r // tr),
        in_specs=[pl.BlockSpec((None, None, tr, cc), lambda k, i, c_ref: (k, c_ref[0], i, 0)),
                  pl.BlockSpec((None, tr, cc), lambda k, i, c_ref: (k, i, 0))],
        out_specs=pl.BlockSpec((None, tr, cc), lambda k, i, c_ref: (k, i, 0)))
    return pl.pallas_call(kern, name=name, grid_spec=grid_spec,
                          out_shape=jax.ShapeDtypeStruct((N_CHIP, r, cc), BF16),
                          compiler_params=_params())(cvec, g, buf)


def _rs_chip_exchange(name, ss):
    n = len(ss)

    def body(*refs):
        srcs = refs[:n]
        bufs = refs[n:2 * n]
        send_sems, recv_sems, local_sems = refs[2 * n:]
        x, y, c = _mesh_pos()
        kme = 2 * x + y
        chips = [(1 - x, y), (x, 1 - y), (1 - x, 1 - y)]
        local = [pltpu.make_async_copy(srcs[t].at[kme], bufs[t].at[kme], local_sems.at[t]) for t in range(n)]
        for cp in local:
            cp.start()
        copies = []
        for t in range(n):
            for j, (px, py) in enumerate(chips):
                copies.append(pltpu.make_async_remote_copy(
                    src_ref=srcs[t].at[2 * px + py], dst_ref=bufs[t].at[kme], send_sem=send_sems.at[t, j],
                    recv_sem=recv_sems.at[t, j], device_id=(px, py, c), device_id_type=MESH))
        for cp in copies:
            cp.start()
        for cp in copies:
            cp.wait()
        for cp in local:
            cp.wait()

    return pl.pallas_call(
        body, name=name, in_specs=[_ANY] * n, out_specs=[_ANY] * n,
        out_shape=[jax.ShapeDtypeStruct(s.shape, s.dtype) for s in ss],
        scratch_shapes=[pltpu.SemaphoreType.DMA((n, 3)), pltpu.SemaphoreType.DMA((n, 3)),
                        pltpu.SemaphoreType.DMA((n,))],
        compiler_params=_params())(*ss)


def _rs_pair_share(name, ts, dst_shapes, dst_of):
    n = len(ts)
    nd = len(dst_shapes)

    def body(*refs):
        srcs = refs[:n]
        dsts = refs[n:n + nd]
        send_sems, recv_sems, local_sems = refs[n + nd:]
        x, y, c = _mesh_pos()
        local = []
        remote = []
        for t in range(n):
            d, l = dst_of[t]
            dref = dsts[d].at[c] if l is None else dsts[d].at[l, c]
            local.append(pltpu.make_async_copy(srcs[t], dref, local_sems.at[t]))
            remote.append(pltpu.make_async_remote_copy(
                src_ref=srcs[t], dst_ref=dref, send_sem=send_sems.at[t], recv_sem=recv_sems.at[t],
                device_id=(x, y, 1 - c), device_id_type=MESH))
        for cp in local + remote:
            cp.start()
        for cp in remote + local:
            cp.wait()

    return pl.pallas_call(
        body, name=name, in_specs=[_ANY] * n, out_specs=[_ANY] * nd,
        out_shape=[jax.ShapeDtypeStruct(s, F32) for s in dst_shapes],
        scratch_shapes=[pltpu.SemaphoreType.DMA((n,)), pltpu.SemaphoreType.DMA((n,)),
                        pltpu.SemaphoreType.DMA((n,))],
        compiler_params=_params())(*ts)


def _rope_tables(seq, n_ctx):
    t = jnp.arange(seq)
    row = (t // GRID_W).astype(F32)
    col = (t % GRID_W).astype(F32)
    inv = ROPE_THETA ** (-jnp.arange(0, HEAD_DIM // 2, 2, dtype=F32) / (HEAD_DIM // 2))
    ang_r = row[:, None] * inv[None, :]
    ang_c = col[:, None] * inv[None, :]
    cos = jnp.concatenate([jnp.cos(ang_r)] * 2 + [jnp.cos(ang_c)] * 2, axis=1)
    sin = jnp.concatenate([-jnp.sin(ang_r), jnp.sin(ang_r), -jnp.sin(ang_c), jnp.sin(ang_c)], axis=1)
    cos = jnp.concatenate([jnp.ones((n_ctx, HEAD_DIM), F32), cos], axis=0)
    sin = jnp.concatenate([jnp.zeros((n_ctx, HEAD_DIM), F32), sin], axis=0)
    return jnp.tile(cos, (1, N_Q_HEADS)), jnp.tile(sin, (1, N_Q_HEADS))


def _to_heads(a, nh):
    return a.reshape(a.shape[0], nh, HEAD_DIM).transpose(1, 0, 2)


def _from_heads(a):
    return a.transpose(1, 0, 2).reshape(a.shape[1], a.shape[0] * HEAD_DIM)


def _sample_step(x, ctx, tgt, mod, cmod, W):
    seq = x.shape[0]
    n_ctx = ctx.shape[0]
    tm = min(256, seq)
    tmc = min(256, n_ctx)
    tmm = 512 if seq % 512 == 0 else tm
    tq = min(256, seq)
    tms = min(128, seq)
    assert n_ctx % tm == 0 and seq % tm == 0

    def mv(l, j):
        return mod[l, j * D_MODEL:(j + 1) * D_MODEL].reshape(1, D_MODEL)

    csh1 = cmod[:D_MODEL].reshape(1, D_MODEL)
    csc1 = cmod[D_MODEL:2 * D_MODEL].reshape(1, D_MODEL)
    g_mix = [W['g_mix'][l:l + 1] for l in range(2)]
    g_ffn = [W['g_ffn'][l:l + 1] for l in range(2)]
    cos, sin = _rope_tables(seq, n_ctx)
    gq = jnp.tile(W['q_gain'], (1, N_Q_HEADS))
    gk = jnp.tile(W['k_gain'], (1, N_KV_HEADS))
    bdq = jnp.kron(jnp.eye(N_Q_HEADS, dtype=F32), jnp.ones((HEAD_DIM, HEAD_DIM), F32)).astype(BF16)
    bdk = bdq[:KV_W, :KV_W]
    w_sp = W['w_sp'].astype(BF16)
    w_sp_t = w_sp.transpose(0, 2, 1)
    bsp = jnp.broadcast_to(W['b_sp'].T[:, :, None], (CHUNK, N_SG, CHUNK)).reshape(CHUNK, SG_W)

    def ffn_fwd(l, h_mid):
        xf = _modnorm_fwd(f'l{l}_ffn_norm', h_mid, g_ffn[l], mv(l, 4), mv(l, 3), tm)
        gu = _mm_blk(f'l{l}_ffn_in', xf, W['ffn_in'][l], tmm)
        a = _swiglu_fwd(f'l{l}_swiglu', gu, tms)
        f = _mm_rows(f'l{l}_ffn_out', a, W['ffn_out'][l], tmm)
        h_out = _resgate_fwd(f'l{l}_ffn_res', h_mid, f, mv(l, 5), tm)
        return h_out, (h_mid, xf, gu, a, f)

    def ffn_bwd(l, dh_out, saved):
        h_mid, xf, gu, a, f = saved
        dy, d_gt2, _ = _resgate_bwd(f'l{l}_ffn_res_bwd', dh_out, f, mv(l, 5), tm)
        da = _mm_rows(f'l{l}_ffn_out_dx', dy, W['ffn_out'][l], tmm, trans_b=True)
        d_wo = _mm_wgrad(f'l{l}_ffn_out_dw', a, dy, tmm)
        dgu = _swiglu_bwd(f'l{l}_swiglu_bwd', gu, da, tms)
        dxf = _mm_blk_t(f'l{l}_ffn_in_dx', dgu, W['ffn_in'][l], tmm)
        d_wi = _mm_wgrad_blk(f'l{l}_ffn_in_dw', xf, dgu, N_CHIP, tmm)
        dh_mid, d_sh2, d_sc2, d_g = _modnorm_bwd(f'l{l}_ffn_norm_bwd', h_mid, [_ri(dxf)], dh_out, g_ffn[l],
                                                 mv(l, 4), tm)
        return dh_mid, d_wi, d_wo, d_g, (d_sh2, d_sc2, d_gt2)

    xm0 = _modnorm_fwd('l0_mix_norm', x, g_mix[0], mv(0, 1), mv(0, 0), tm)
    xc0 = _modnorm_fwd('l0_ctx_norm', ctx, g_mix[0], csc1, csh1, tmc)
    xall = jnp.concatenate([xc0, xm0], axis=0)
    p_qkv = _mm_rows('l0_qkv', xall, W['w_qkv'], tm)
    p_sg = _mm_rows('l0_sg_in', xm0, W['w_sgp'], tmm)
    q_all, k_all, v_all = _qk_prep_fwd('l0_qk_prep', p_qkv, cos, sin, gq, gk, bdq, bdk, tm)
    qh = _to_heads(q_all[n_ctx:], N_Q_HEADS)
    kh = _to_heads(k_all, N_KV_HEADS)
    vh = _to_heads(v_all, N_KV_HEADS)
    o, lse = _attn_fwd('l0_attn', qh, kh, vh, tq)
    sg = _sg_fwd('l0_sg', p_sg, w_sp, bsp, tm)
    cat = jnp.concatenate([_from_heads(o).astype(BF16), sg], axis=1)
    y0 = _mm_rows('l0_out', cat, W['w_out'], tmm)
    h1 = _resgate_fwd('l0_mix_res', x, y0, mv(0, 2), tm)
    h2, ffn0 = ffn_fwd(0, h1)

    xm1 = _modnorm_fwd('l1_mix_norm', h2, g_mix[1], mv(1, 1), mv(1, 0), tm)
    ag = _mm_blk('l1_pw1', xm1, W['pw1'], tmm)
    hg = _glu_fwd('l1_glu', ag, W['b_pw1'], tm)
    hc = _dwconv('l1_conv', hg, W['w_dw'], W['b_dw'])
    s1 = _lnsilu_fwd('l1_ln_silu', hc, W['ln_g'], W['ln_b'], tm)
    y1 = _mm_rows('l1_pw2', s1, W['pw2'], tmm, bias=W['b_pw2'])
    h3 = _resgate_fwd('l1_mix_res', h2, y1, mv(1, 2), tm)
    h4, ffn1 = ffn_fwd(1, h3)

    dh4, d_g_final, loss_vec = _final_loss('final_loss', h4, tgt, W['g_final'], tm)

    dh3, d_wi1, d_wo1, d_gffn1, dmod1_ffn = ffn_bwd(1, dh4, ffn1)
    dy1, d_gt1_1, d_b_pw2 = _resgate_bwd('l1_mix_res_bwd', dh3, y1, mv(1, 2), tm)
    ds1 = _mm_rows('l1_pw2_dx', dy1, W['pw2'], tmm, trans_b=True)
    d_pw2 = _mm_wgrad('l1_pw2_dw', s1, dy1, tmm)
    dhc, d_ln_g, d_ln_b = _lnsilu_bwd('l1_ln_silu_bwd', hc, ds1, W['ln_g'], W['ln_b'], tm)
    dhg = _dwconv('l1_conv_dx', dhc, W['w_dw'][::-1], jnp.zeros((1, D_MODEL), F32))
    d_w_dw, d_b_dw = _dwconv_wgrad('l1_conv_dw', hg, dhc)
    dag, d_b_pw1 = _glu_bwd('l1_glu_bwd', ag, dhg, W['b_pw1'], tm)
    dxm1 = _mm_blk_t('l1_pw1_dx', dag, W['pw1'], tmm)
    d_pw1 = _mm_wgrad_blk('l1_pw1_dw', xm1, dag, N_CHIP, tmm)
    dh2, d_sh1_1, d_sc1_1, d_gmix1 = _modnorm_bwd('l1_mix_norm_bwd', h2, [_ri(dxm1)], dh3, g_mix[1], mv(1, 1), tm)

    dh1, d_wi0, d_wo0, d_gffn0, dmod0_ffn = ffn_bwd(0, dh2, ffn0)
    dy0, d_gt1_0, _ = _resgate_bwd('l0_mix_res_bwd', dh1, y0, mv(0, 2), tm)
    dcat = _mm_rows('l0_out_dx', dy0, W['w_out'], tmm, trans_b=True)
    d_w_out = _mm_wgrad('l0_out_dw', cat, dy0, tmm)
    do = _to_heads(dcat[:, :ATTN_W], N_Q_HEADS)
    dq, dk, dv = _attn_bwd('l0_attn_bwd', qh, kh, vh, o, lse, do, min(128, seq))
    dq_all = jnp.concatenate([jnp.zeros((n_ctx, ATTN_W), F32), _from_heads(dq)], axis=0)
    dp_qkv, d_gq, d_gk = _qk_prep_bwd('l0_qk_prep_bwd', p_qkv, cos, sin, dq_all, _from_heads(dk), _from_heads(dv),
                                      gq, gk, bdq, bdk, tm)
    dp_sg, d_w_sp, d_bsp = _sg_bwd('l0_sg_bwd', p_sg, dcat, w_sp, w_sp_t, bsp, tm)
    d_w_qkv = _mm_wgrad('l0_qkv_dw', xall, dp_qkv, tm)
    d_w_sgp = _mm_wgrad('l0_sg_in_dw', xm0, dp_sg, tmm)
    dxall = _mm_rows('l0_qkv_dx', dp_qkv, W['w_qkv'], tm, trans_b=True)
    dxm_sg = _mm_rows('l0_sg_in_dx', dp_sg, W['w_sgp'], tmm, trans_b=True)
    dx, d_sh1_0, d_sc1_0, d_gmix0 = _modnorm_bwd('l0_mix_norm_bwd', x, [_ri(dxall, None, 0, n_ctx // tm), _ri(dxm_sg)],
                                                 dh1, g_mix[0], mv(0, 1), tm)
    _, d_csh1, d_csc1, d_gmix0c = _modnorm_bwd('l0_ctx_norm_bwd', ctx, [_ri(dxall)], None, g_mix[0], csc1, tmc)

    zero = jnp.zeros((1, D_MODEL), F32)
    dmod = jnp.concatenate([d_sh1_0, d_sc1_0, d_gt1_0, *dmod0_ffn, d_sh1_1, d_sc1_1, d_gt1_1, *dmod1_ffn], axis=0)
    dcmod = jnp.concatenate([d_csh1, d_csc1, zero, zero, zero, zero], axis=0)
    grads = dict(
        ffn_in=[d_wi0, d_wi1], ffn_out=[d_wo0, d_wo1], w_in=jnp.concatenate([d_w_qkv, d_w_sgp], axis=1),
        w_out=d_w_out, pw1=d_pw1, pw2=d_pw2,
        g_mix=jnp.concatenate([d_gmix0 + d_gmix0c, d_gmix1], axis=0),
        g_ffn=jnp.concatenate([d_gffn0, d_gffn1], axis=0), g_final=d_g_final, gq=d_gq, gk=d_gk,
        w_sp=d_w_sp, bsp=d_bsp, b_pw1=d_b_pw1, w_dw=d_w_dw, b_dw=d_b_dw, ln_g=d_ln_g, ln_b=d_ln_b, b_pw2=d_b_pw2,
        dmod=dmod, dcmod=dcmod)
    return loss_vec, dx, grads


def kernel(x, c, ctx, c_ctx, w_mod, b_mod, g_mix, g_ffn, w_ffn_in, w_ffn_out, w_in, q_gain, k_gain, w_sp, b_sp, w_out, w_pw1, b_pw1, w_dw, b_dw, ln_g, ln_b, w_pw2, b_pw2, g_final, loss_target, m_c_ctx, m_w_mod, m_b_mod, m_g_mix, m_g_ffn, m_w_ffn_in, m_w_ffn_out, m_w_in, m_q_gain, m_k_gain, m_w_sp, m_b_sp, m_w_out, m_w_pw1, m_b_pw1, m_w_dw, m_b_dw, m_ln_g, m_ln_b, m_w_pw2, m_b_pw2, m_g_final, v_c_ctx, v_w_mod, v_b_mod, v_g_mix, v_g_ffn, v_w_ffn_in, v_w_ffn_out, v_w_in, v_q_gain, v_k_gain, v_w_sp, v_b_sp, v_w_out, v_w_pw1, v_b_pw1, v_w_dw, v_b_dw, v_ln_g, v_ln_b, v_w_pw2, v_b_pw2, v_g_final):
    given = dict(locals())
    ix, iy, ic = _mesh_pos()
    chip = 2 * ix + iy
    me = 2 * chip + ic
    d = D_MODEL
    q4 = d // N_CHIP

    small = jnp.concatenate([c.reshape(4, q4), b_pw1.reshape(2, q4), w_dw[0], b_dw, ln_g, ln_b, b_pw2,
                             jnp.zeros((7, q4), F32)], axis=0)
    sgath = _allgather('ag_small', [small])[0]
    c_all = sgath[:, 0:4].reshape(N_DEV, d)
    per_chip = sgath[0::2]
    b_pw1_f = per_chip[:, 4:6].reshape(1, 2 * d)
    w_dw_f = per_chip[:, 6:6 + CONV_W].transpose(1, 0, 2).reshape(CONV_W, d)
    b_dw_f, ln_g_f, ln_b_f, b_pw2_f = [per_chip[:, 37 + i].reshape(1, d) for i in range(4)]

    c16 = jnp.concatenate([c_all, c_ctx[None, :], jnp.zeros((7, d), F32)], axis=0)
    wm_b = w_mod.astype(BF16)
    bm_sh = lax.dynamic_slice_in_dim(b_mod, chip * MOD_BLK, MOD_BLK, axis=1).reshape(2, 1, MOD_BLK)
    mod_piece = _mod_fwd('mod_fwd', c16, wm_b, bm_sh)
    mgath = _allgather('ag_mod', [mod_piece.reshape(32, MOD_BLK)])[0]
    mod_all = mgath[0::2].reshape(N_CHIP, 2, 16, MOD_BLK).transpose(1, 2, 0, 3).reshape(2, 16, MOD_W)
    mod_me = lax.dynamic_index_in_dim(mod_all, me, axis=1, keepdims=False)
    cmod = mod_all[0, N_DEV]

    def my_half(a):
        r = a.shape[0] // 2
        return lax.dynamic_slice_in_dim(a, ic * r, r, axis=0).astype(BF16)

    gw = _allgather('ag_weights', [my_half(w_ffn_in[0]), my_half(w_ffn_in[1]), my_half(w_ffn_out[0]),
                                   my_half(w_ffn_out[1]), my_half(w_in[0]), my_half(w_out[0]), my_half(w_pw1[0]),
                                   my_half(w_pw2[0])])
    w_in_f = gw[4].reshape(N_CHIP, d, -1).transpose(1, 0, 2).reshape(d, -1)
    W = dict(
        ffn_in=[gw[0].reshape(N_CHIP, d, FF_BLK), gw[1].reshape(N_CHIP, d, FF_BLK)],
        ffn_out=[gw[2].reshape(D_FF, d), gw[3].reshape(D_FF, d)],
        w_qkv=w_in_f[:, :QKV_W], w_sgp=w_in_f[:, QKV_W:], w_out=gw[5].reshape(d, d),
        pw1=gw[6].reshape(N_CHIP, d, PW1_BLK), pw2=gw[7].reshape(d, d),
        g_mix=g_mix, g_ffn=g_ffn, g_final=g_final.reshape(1, d), q_gain=q_gain, k_gain=k_gain, w_sp=w_sp[0],
        b_sp=b_sp[0], b_pw1=b_pw1_f, w_dw=w_dw_f, b_dw=b_dw_f, ln_g=ln_g_f, ln_b=ln_b_f, b_pw2=b_pw2_f)

    loss_vec, dx, G = _sample_step(x[0], ctx[0], loss_target[0], mod_me, cmod, W)
    loss = lax.psum(jnp.sum(loss_vec), ("x", "y", "c"))

    pack = jnp.concatenate([
        G['g_mix'], G['g_ffn'], G['g_final'],
        jnp.concatenate([G['gq'], G['gk'], jnp.zeros((1, d - ATTN_W - KV_W), F32)], axis=1),
        G['b_pw1'].reshape(2, d), G['w_dw'], G['b_dw'], G['ln_g'], G['ln_b'], G['b_pw2'],
        G['dmod'], G['dcmod'], jnp.zeros((3, d), F32),
        G['w_sp'].reshape(64, d), G['bsp'].reshape(64, d)], axis=0)
    assert pack.shape == (192, d)
    pg = _allgather('ag_small_grads', [pack])[0]
    s8 = _sum_slots('sum_small_grads', pg, 64)

    def chip_cols(a, width):
        return lax.dynamic_slice_in_dim(a, chip * width, width, axis=a.ndim - 1)

    grads = {}
    grads['g_mix'] = s8[0:2]
    grads['g_ffn'] = s8[2:4]
    grads['g_final'] = s8[4]
    grads['q_gain'] = s8[5, :ATTN_W].reshape(N_Q_HEADS, HEAD_DIM).sum(axis=0)[None, :]
    grads['k_gain'] = s8[5, ATTN_W:ATTN_W + KV_W].reshape(N_KV_HEADS, HEAD_DIM).sum(axis=0)[None, :]
    grads['b_pw1'] = chip_cols(s8[6:8].reshape(1, 2 * d), 2 * q4)
    grads['w_dw'] = chip_cols(s8[8:8 + CONV_W], q4)[None]
    grads['b_dw'] = chip_cols(s8[39:40], q4)
    grads['ln_g'] = chip_cols(s8[40:41], q4)
    grads['ln_b'] = chip_cols(s8[41:42], q4)
    grads['b_pw2'] = chip_cols(s8[42:43], q4)
    grads['w_sp'] = s8[64:128].reshape(1, N_SG, CHUNK, CHUNK)
    grads['b_sp'] = s8[128:192].reshape(CHUNK, N_SG, CHUNK).sum(axis=-1).T[None]

    dm_rows = pg[:, 43:55].reshape(N_DEV, 2, MOD_W)
    ctx_row = s8[55:61].reshape(1, MOD_W)
    dm = jnp.stack([jnp.concatenate([dm_rows[:, 0], ctx_row, jnp.zeros((7, MOD_W), F32)], axis=0),
                    jnp.concatenate([dm_rows[:, 1], jnp.zeros((8, MOD_W), F32)], axis=0)], axis=0)
    grads['b_mod'] = jnp.stack([s8[43:49].reshape(MOD_W) + ctx_row[0], s8[49:55].reshape(MOD_W)], axis=0)
    dm_sh = chip_cols(dm, MOD_BLK)
    grads['w_mod'] = _mod_wgrad('mod_dw', c16, dm_sh)
    d_silu = _mm_rows('mod_dx', dm_sh[0], wm_b[0], 16, trans_b=True)
    parts = _allgather('ag_cctx', [d_silu[8:16]])[0]
    grads['c_ctx'] = _cctx_grad('cctx_grad', parts, c_ctx.reshape(1, d))[0]

    def blocked(a):
        cols = a.shape[-1]
        return a.reshape(N_CHIP, 2, a.size // (2 * N_CHIP * cols), cols)

    gs = [blocked(G['ffn_in'][0]), blocked(G['ffn_in'][1]), blocked(G['ffn_out'][0]), blocked(G['ffn_out'][1]),
          blocked(G['w_in'].reshape(d, N_CHIP, -1).transpose(1, 0, 2)), blocked(G['w_out']), blocked(G['pw1']),
          blocked(G['pw2'])]
    sib = _rs_pair_send('rs_pair_send', gs)
    cvec = jnp.reshape(ic, (1,)).astype(jnp.int32)
    ss = [_rs_pair_add(f'rs_pair_add{t}', gs[t], sib[t], cvec) for t in range(len(gs))]
    xs = _rs_chip_exchange('rs_chip_exchange', ss)
    ts = [_sum_slots(f'rs_chip_sum{t}', xs[t], _row_tile(xs[t].shape[1], xs[t].shape[2], 512 * 1024))
          for t in range(len(gs))]
    shared = _rs_pair_share('rs_pair_share', ts,
                            [(2, 2) + gs[0].shape[2:], (2, 2) + gs[2].shape[2:], (2,) + gs[4].shape[2:],
                             (2,) + gs[5].shape[2:], (2,) + gs[6].shape[2:], (2,) + gs[7].shape[2:]],
                            [(0, 0), (0, 1), (1, 0), (1, 1), (2, None), (3, None), (4, None), (5, None)])
    grads['w_ffn_in'] = shared[0].reshape(w_ffn_in.shape)
    grads['w_ffn_out'] = shared[1].reshape(w_ffn_out.shape)
    grads['w_in'] = shared[2].reshape(w_in.shape)
    grads['w_out'] = shared[3].reshape(w_out.shape)
    grads['w_pw1'] = shared[4].reshape(w_pw1.shape)
    grads['w_pw2'] = shared[5].reshape(w_pw2.shape)

    deltas, new_m, new_v = {}, {}, {}
    for n in WEIGHT_NAMES:
        grads[n] = grads[n].reshape(given[n].shape)
        deltas[n], new_m[n], new_v[n] = _adamw(f'adamw_{n}', given[n], grads[n], given['m_' + n], given['v_' + n])

    return (loss, dx[None], *[grads[n] for n in WEIGHT_NAMES], *[deltas[n] for n in WEIGHT_NAMES],
            *[new_m[n] for n in WEIGHT_NAMES], *[new_v[n] for n in WEIGHT_NAMES])
```

```python
import functools
import math

import jax
import jax.numpy as jnp
from jax import lax
from jax.experimental import pallas as pl
from jax.experimental.pallas import tpu as pltpu

F32 = jnp.float32
BF16 = jnp.bfloat16
MESH = pl.DeviceIdType.MESH

D_MODEL = 1024
HEAD_DIM = 64
N_Q_HEADS = 8
N_KV_HEADS = 2
Q_GROUP = N_Q_HEADS // N_KV_HEADS
ATTN_W = N_Q_HEADS * HEAD_DIM
KV_W = N_KV_HEADS * HEAD_DIM
QKV_W = ATTN_W + 2 * KV_W
SG_W = D_MODEL - ATTN_W
N_SG = 4
CHUNK = 128
GRID_W = 64
ROPE_THETA = 10000.0
CONV_W = 31
CONV_HALO = 16
D_FF = 2816
MOD_W = 6 * D_MODEL
N_CHIP = 4
N_DEV = 8
FF_BLK = 2 * D_FF // N_CHIP
PW1_BLK = 2 * D_MODEL // N_CHIP
MOD_BLK = MOD_W // N_CHIP
EPS = 1e-6
GELU_C = math.sqrt(2.0 / math.pi)
GELU_A = 0.044715

ADAM_LR = 0.001
ADAM_B1 = 0.9
ADAM_B2 = 0.999
ADAM_EPS = 1e-08
ADAM_WD = 0.01
ADAM_STEP = 10

VMEM_LIMIT_BYTES = 48 * 1024 * 1024

WEIGHT_NAMES = ['c_ctx', 'w_mod', 'b_mod', 'g_mix', 'g_ffn', 'w_ffn_in', 'w_ffn_out', 'w_in', 'q_gain', 'k_gain',
                'w_sp', 'b_sp', 'w_out', 'w_pw1', 'b_pw1', 'w_dw', 'b_dw', 'ln_g', 'ln_b', 'w_pw2', 'b_pw2',
                'g_final']


def _params():
    return pltpu.CompilerParams(vmem_limit_bytes=VMEM_LIMIT_BYTES)


def _ri(arr, width=None, col_block=0, row_off=0):
    return (arr, arr.shape[1] if width is None else width, col_block, row_off)


def _rowwise(name, body, n_tiles, tm, row_ins, vec_ins, row_outs, vec_outs):
    nri, nvi, nro = len(row_ins), len(vec_ins), len(row_outs)

    def kern(*refs):
        ri = refs[:nri]
        vi = refs[nri:nri + nvi]
        ro = refs[nri + nvi:nri + nvi + nro]
        vo = refs[nri + nvi + nro:]
        if vo:
            @pl.when(pl.program_id(0) == 0)
            def _():
                for r in vo:
                    r[...] = jnp.zeros(r.shape, r.dtype)
        body(ri, vi, ro, vo)

    in_specs = [pl.BlockSpec((tm, bw), lambda i, cb=cb, off=off: (i + off, cb)) for (_, bw, cb, off) in row_ins]
    in_specs += [pl.BlockSpec(a.shape, lambda i, nd=a.ndim: (0,) * nd) for a in vec_ins]
    out_specs = [pl.BlockSpec((tm, w), lambda i: (i, 0)) for (w, _) in row_outs]
    out_specs += [pl.BlockSpec(s, lambda i, nd=len(s): (0,) * nd) for s in vec_outs]
    out_shape = [jax.ShapeDtypeStruct((n_tiles * tm, w), dt) for (w, dt) in row_outs]
    out_shape += [jax.ShapeDtypeStruct(s, F32) for s in vec_outs]
    return pl.pallas_call(kern, name=name, grid=(n_tiles,), in_specs=in_specs, out_specs=out_specs,
                          out_shape=out_shape, compiler_params=_params())(*[a for (a, _, _, _) in row_ins], *vec_ins)


def _colsum(v):
    return jnp.sum(v, axis=0, keepdims=True)


def _rowmean(v):
    return jnp.mean(v, axis=-1, keepdims=True)


def _rms(h):
    r = lax.rsqrt(_rowmean(h * h) + EPS)
    return h * r, r


def _rms_bwd(n, r, gy):
    return r * (gy - n * _rowmean(gy * n))


def _ln_stats(v):
    xc = v - _rowmean(v)
    r = lax.rsqrt(_rowmean(xc * xc) + EPS)
    return xc * r, r


def _ln_bwd(xh, r, dxh):
    return r * (dxh - _rowmean(dxh) - xh * _rowmean(dxh * xh))


def _sigmoid(v):
    return 1.0 / (1.0 + jnp.exp(-v))


def _gelu(v):
    t = jnp.tanh(GELU_C * (v + GELU_A * (v * v * v)))
    return 0.5 * v * (1.0 + t), t


def _gelu_grad(v, t):
    return 0.5 * (1.0 + t) + 0.5 * v * (1.0 - t * t) * (GELU_C * (1.0 + 3.0 * GELU_A * (v * v)))


def _modnorm_fwd(name, h, g, sc, sh, tm):
    def body(ri, vi, ro, vo):
        n, _ = _rms(ri[0][...])
        ro[0][...] = ((n * vi[0][...]) * (1.0 + vi[1][...]) + vi[2][...]).astype(BF16)

    return _rowwise(name, body, h.shape[0] // tm, tm, [_ri(h)], [g, sc, sh], [(D_MODEL, BF16)], [])[0]


def _modnorm_bwd(name, h, dxs, dh_in, g, sc, tm):
    ndx = len(dxs)

    def body(ri, vi, ro, vo):
        dxm = ri[1][...].astype(F32)
        for r in ri[2:1 + ndx]:
            dxm = dxm + r[...].astype(F32)
        gv = vi[0][...]
        n, r = _rms(ri[0][...])
        vo[0][...] += _colsum(dxm)
        vo[1][...] += _colsum(dxm * (n * gv))
        dy = dxm * (1.0 + vi[1][...])
        vo[2][...] += _colsum(dy * n)
        if dh_in is not None:
            ro[0][...] = ri[1 + ndx][...] + _rms_bwd(n, r, dy * gv)

    row_ins = [_ri(h)] + list(dxs) + ([_ri(dh_in)] if dh_in is not None else [])
    row_outs = [(D_MODEL, F32)] if dh_in is not None else []
    outs = _rowwise(name, body, h.shape[0] // tm, tm, row_ins, [g, sc], row_outs, [(1, D_MODEL)] * 3)
    if dh_in is None:
        return (None, *outs)
    return tuple(outs)


def _resgate_fwd(name, h, y, gt, tm):
    def body(ri, vi, ro, vo):
        ro[0][...] = ri[0][...] + vi[0][...] * ri[1][...]

    return _rowwise(name, body, h.shape[0] // tm, tm, [_ri(h), _ri(y)], [gt], [(D_MODEL, F32)], [])[0]


def _resgate_bwd(name, dh, y, gt, tm):
    def body(ri, vi, ro, vo):
        d = ri[0][...]
        dy = d * vi[0][...]
        ro[0][...] = dy.astype(BF16)
        vo[0][...] += _colsum(d * ri[1][...])
        vo[1][...] += _colsum(dy)

    return _rowwise(name, body, dh.shape[0] // tm, tm, [_ri(dh), _ri(y)], [gt], [(D_MODEL, BF16)],
                    [(1, D_MODEL)] * 2)


def _swiglu_fwd(name, gu, tm):
    def body(ri, vi, ro, vo):
        g = ri[0][...]
        ro[0][...] = (g * _sigmoid(g) * ri[1][...]).astype(BF16)

    return _rowwise(name, body, gu.shape[0] // tm, tm, [_ri(gu, D_FF, 0), _ri(gu, D_FF, 1)], [], [(D_FF, BF16)], [])[0]


def _swiglu_bwd(name, gu, da, tm):
    def body(ri, vi, ro, vo):
        g = ri[0][...]
        u = ri[1][...]
        d = ri[2][...]
        sg = _sigmoid(g)
        ro[0][:, :D_FF] = (d * u * (sg * (1.0 + g * (1.0 - sg)))).astype(BF16)
        ro[0][:, D_FF:] = (d * (g * sg)).astype(BF16)

    return _rowwise(name, body, gu.shape[0] // tm, tm, [_ri(gu, D_FF, 0), _ri(gu, D_FF, 1), _ri(da)], [],
                    [(2 * D_FF, BF16)], [])[0]


def _final_loss(name, h, tgt, g, tm):
    def body(ri, vi, ro, vo):
        gv = vi[0][...]
        n, r = _rms(ri[0][...])
        e = n * gv - ri[1][...]
        vo[1][...] += _colsum(e * e) * (0.5 / D_MODEL)
        dout = e * (1.0 / D_MODEL)
        vo[0][...] += _colsum(dout * n)
        ro[0][...] = _rms_bwd(n, r, dout * gv)

    return _rowwise(name, body, h.shape[0] // tm, tm, [_ri(h), _ri(tgt)], [g], [(D_MODEL, F32)],
                    [(1, D_MODEL)] * 2)


def _head_mean(v, bd):
    hi = v.astype(BF16)
    lo = (v - hi.astype(F32)).astype(BF16)
    s = jnp.dot(hi, bd, preferred_element_type=F32) + jnp.dot(lo, bd, preferred_element_type=F32)
    return s * (1.0 / HEAD_DIM)


def _swap16(v):
    w = v.shape[1]
    lane = lax.broadcasted_iota(jnp.int32, v.shape, 1)
    return jnp.where((lane & 16) == 0, pltpu.roll(v, w - 16, 1), pltpu.roll(v, 16, 1))


def _rope(v, cos, sin):
    return v * cos + _swap16(v) * sin


def _rope_bwd(d, cos, sin):
    return d * cos + _swap16(d * sin)


def _qk_prep_fwd(name, p_qkv, cos, sin, gq, gk, bdq, bdk, tm):
    def body(ri, vi, ro, vo):
        pq = ri[0][...]
        pkv = ri[1][...]
        cosv = ri[2][...]
        sinv = ri[3][...]
        rq = lax.rsqrt(_head_mean(pq * pq, vi[2][...]) + EPS)
        ro[0][...] = (_rope(pq * rq * vi[0][...], cosv, sinv) * (HEAD_DIM ** -0.5)).astype(BF16)
        pk = pkv[:, :KV_W]
        rk = lax.rsqrt(_head_mean(pk * pk, vi[3][...]) + EPS)
        ro[1][...] = _rope(pk * rk * vi[1][...], cosv[:, :KV_W], sinv[:, :KV_W]).astype(BF16)
        ro[2][...] = pkv[:, KV_W:].astype(BF16)

    return _rowwise(name, body, p_qkv.shape[0] // tm, tm,
                    [_ri(p_qkv, ATTN_W, 0), _ri(p_qkv, 2 * KV_W, 2), _ri(cos), _ri(sin)], [gq, gk, bdq, bdk],
                    [(ATTN_W, BF16), (KV_W, BF16), (KV_W, BF16)], [])


def _qk_prep_bwd(name, p_qkv, cos, sin, dq, dk, dv, gq, gk, bdq, bdk, tm):
    def norm_rope_bwd(p, gain, bd, cosv, sinv, dout):
        r = lax.rsqrt(_head_mean(p * p, bd) + EPS)
        n = p * r
        dqn = _rope_bwd(dout, cosv, sinv)
        gy = dqn * gain
        return r * (gy - n * _head_mean(gy * n, bd)), _colsum(dqn * n)

    def body(ri, vi, ro, vo):
        pkv = ri[1][...]
        cosv = ri[2][...]
        sinv = ri[3][...]
        dxq, dgq = norm_rope_bwd(ri[0][...], vi[0][...], vi[2][...], cosv, sinv, ri[4][...] * (HEAD_DIM ** -0.5))
        dxk, dgk = norm_rope_bwd(pkv[:, :KV_W], vi[1][...], vi[3][...], cosv[:, :KV_W], sinv[:, :KV_W], ri[5][...])
        ro[0][:, :ATTN_W] = dxq.astype(BF16)
        ro[0][:, ATTN_W:ATTN_W + KV_W] = dxk.astype(BF16)
        ro[0][:, ATTN_W + KV_W:] = ri[6][...].astype(BF16)
        vo[0][...] += dgq
        vo[1][...] += dgk

    return _rowwise(name, body, p_qkv.shape[0] // tm, tm,
                    [_ri(p_qkv, ATTN_W, 0), _ri(p_qkv, 2 * KV_W, 2), _ri(cos), _ri(sin), _ri(dq), _ri(dk), _ri(dv)],
                    [gq, gk, bdq, bdk], [(QKV_W, BF16)], [(1, ATTN_W), (1, KV_W)])


def _sg_fwd(name, p_sg, w_sp, bsp, tm):
    def body(ri, vi, ro, vo):
        p = ri[0][...]
        gu, _ = _gelu(p[:, :SG_W])
        gv, _ = _gelu(p[:, SG_W:])
        for g in range(N_SG):
            sl = slice(g * CHUNK, (g + 1) * CHUNK)
            vn, _ = _ln_stats(gv[:, sl])
            vnb = vn.astype(BF16)
            for m in range(tm // CHUNK):
                rs = slice(m * CHUNK, (m + 1) * CHUNK)
                mixed = jnp.dot(vi[0][g], vnb[rs], preferred_element_type=F32) + vi[1][:, sl]
                ro[0][rs, sl] = (gu[rs, sl] * mixed).astype(BF16)

    return _rowwise(name, body, p_sg.shape[0] // tm, tm, [_ri(p_sg)], [w_sp, bsp], [(SG_W, BF16)], [])[0]


def _sg_bwd(name, p_sg, dcat, w_sp, w_sp_t, bsp, tm):
    def body(ri, vi, ro, vo):
        p = ri[0][...]
        dsg = ri[1][...]
        su = p[:, :SG_W]
        sv = p[:, SG_W:]
        gu, tu = _gelu(su)
        gv, tv = _gelu(sv)
        for g in range(N_SG):
            sl = slice(g * CHUNK, (g + 1) * CHUNK)
            vn, r = _ln_stats(gv[:, sl])
            vnb = vn.astype(BF16)
            for m in range(tm // CHUNK):
                rs = slice(m * CHUNK, (m + 1) * CHUNK)
                mixed = jnp.dot(vi[0][g], vnb[rs], preferred_element_type=F32) + vi[2][:, sl]
                d_o = dsg[rs, sl]
                dm = d_o * gu[rs, sl]
                vo[1][:, sl] += dm
                dmb = dm.astype(BF16)
                vo[0][g] += lax.dot_general(dmb, vnb[rs], (((1,), (1,)), ((), ())), preferred_element_type=F32)
                dvn = jnp.dot(vi[1][g], dmb, preferred_element_type=F32)
                dgv = _ln_bwd(vn[rs], r[rs], dvn)
                ro[0][rs, sl] = (d_o * mixed * _gelu_grad(su[rs, sl], tu[rs, sl])).astype(BF16)
                ro[0][rs, SG_W + g * CHUNK:SG_W + (g + 1) * CHUNK] = (
                    dgv * _gelu_grad(sv[rs, sl], tv[rs, sl])).astype(BF16)

    return _rowwise(name, body, p_sg.shape[0] // tm, tm, [_ri(p_sg), _ri(dcat, SG_W, 1)], [w_sp, w_sp_t, bsp],
                    [(2 * SG_W, BF16)], [(N_SG, CHUNK, CHUNK), (CHUNK, SG_W)])


def _glu_fwd(name, ag, b, tm):
    def body(ri, vi, ro, vo):
        v = ri[0][...] + vi[0][...]
        ro[0][...] = v[:, :D_MODEL] * _sigmoid(v[:, D_MODEL:])

    return _rowwise(name, body, ag.shape[0] // tm, tm, [_ri(ag)], [b], [(D_MODEL, F32)], [])[0]


def _glu_bwd(name, ag, dhg, b, tm):
    def body(ri, vi, ro, vo):
        v = ri[0][...] + vi[0][...]
        d = ri[1][...]
        a = v[:, :D_MODEL]
        sg = _sigmoid(v[:, D_MODEL:])
        da = d * sg
        dgate = d * a * sg * (1.0 - sg)
        ro[0][:, :D_MODEL] = da.astype(BF16)
        ro[0][:, D_MODEL:] = dgate.astype(BF16)
        vo[0][:, :D_MODEL] += _colsum(da)
        vo[0][:, D_MODEL:] += _colsum(dgate)

    return _rowwise(name, body, ag.shape[0] // tm, tm, [_ri(ag), _ri(dhg)], [b], [(2 * D_MODEL, BF16)],
                    [(1, 2 * D_MODEL)])


def _lnsilu_fwd(name, hc, g, b, tm):
    def body(ri, vi, ro, vo):
        xh, _ = _ln_stats(ri[0][...])
        ln = xh * vi[0][...] + vi[1][...]
        ro[0][...] = (ln * _sigmoid(ln)).astype(BF16)

    return _rowwise(name, body, hc.shape[0] // tm, tm, [_ri(hc)], [g, b], [(D_MODEL, BF16)], [])[0]


def _lnsilu_bwd(name, hc, ds, g, b, tm):
    def body(ri, vi, ro, vo):
        gv = vi[0][...]
        xh, r = _ln_stats(ri[0][...])
        ln = xh * gv + vi[1][...]
        sg = _sigmoid(ln)
        dln = ri[1][...] * (sg * (1.0 + ln * (1.0 - sg)))
        vo[0][...] += _colsum(dln * xh)
        vo[1][...] += _colsum(dln)
        ro[0][...] = _ln_bwd(xh, r, dln * gv)

    return _rowwise(name, body, hc.shape[0] // tm, tm, [_ri(hc), _ri(ds)], [g, b], [(D_MODEL, F32)],
                    [(1, D_MODEL)] * 2)


def _conv_blocks(seq):
    cb = 256
    tt = 128 if seq % 128 == 0 else seq
    return cb, tt


def _fill_padded(pad_ref, x_ref, seq):
    zeros = jnp.zeros((CONV_HALO, pad_ref.shape[1]), F32)
    pad_ref[0:CONV_HALO, :] = zeros
    pad_ref[seq + CONV_HALO:seq + 2 * CONV_HALO, :] = zeros
    pad_ref[CONV_HALO:seq + CONV_HALO, :] = x_ref[...]


def _dwconv(name, xin, w, b):
    seq = xin.shape[0]
    cb, tt = _conv_blocks(seq)

    def kern(x_ref, w_ref, b_ref, o_ref, pad_ref):
        _fill_padded(pad_ref, x_ref, seq)
        wv = w_ref[...]
        bv = b_ref[...]

        def step(t, carry):
            base = pl.multiple_of(t * tt, tt)
            win = pad_ref[pl.ds(base, tt + 2 * CONV_HALO), :]
            acc = jnp.zeros((tt, cb), F32) + bv
            for j in range(CONV_W):
                acc = acc + wv[j:j + 1, :] * win[j + 1:j + 1 + tt, :]
            o_ref[pl.ds(base, tt), :] = acc
            return carry

        lax.fori_loop(0, seq // tt, step, 0)

    return pl.pallas_call(
        kern, name=name, grid=(D_MODEL // cb,),
        in_specs=[pl.BlockSpec((seq, cb), lambda j: (0, j)), pl.BlockSpec((CONV_W, cb), lambda j: (0, j)),
                  pl.BlockSpec((1, cb), lambda j: (0, j))],
        out_specs=pl.BlockSpec((seq, cb), lambda j: (0, j)),
        out_shape=jax.ShapeDtypeStruct((seq, D_MODEL), F32),
        scratch_shapes=[pltpu.VMEM((seq + 2 * CONV_HALO, cb), F32)],
        compiler_params=_params())(xin, w, b)


def _dwconv_wgrad(name, xin, dout):
    seq = xin.shape[0]
    cb, tt = _conv_blocks(seq)

    def kern(x_ref, d_ref, dw_ref, db_ref, pad_ref):
        _fill_padded(pad_ref, x_ref, seq)
        dw_ref[...] = jnp.zeros(dw_ref.shape, F32)
        db_ref[...] = jnp.zeros(db_ref.shape, F32)

        def step(t, carry):
            base = pl.multiple_of(t * tt, tt)
            win = pad_ref[pl.ds(base, tt + 2 * CONV_HALO), :]
            d = d_ref[pl.ds(base, tt), :]
            db_ref[...] += _colsum(d)
            for j in range(CONV_W):
                dw_ref[j:j + 1, :] += _colsum(d * win[j + 1:j + 1 + tt, :])
            return carry

        lax.fori_loop(0, seq // tt, step, 0)

    return pl.pallas_call(
        kern, name=name, grid=(D_MODEL // cb,),
        in_specs=[pl.BlockSpec((seq, cb), lambda j: (0, j)), pl.BlockSpec((seq, cb), lambda j: (0, j))],
        out_specs=[pl.BlockSpec((CONV_W, cb), lambda j: (0, j)), pl.BlockSpec((1, cb), lambda j: (0, j))],
        out_shape=[jax.ShapeDtypeStruct((CONV_W, D_MODEL), F32), jax.ShapeDtypeStruct((1, D_MODEL), F32)],
        scratch_shapes=[pltpu.VMEM((seq + 2 * CONV_HALO, cb), F32)],
        compiler_params=_params())(xin, dout)


def _mm(name, a, b, out_sds, grid, a_spec, b_spec, o_spec, contract, k_axis=None, bias=None):
    dn = (contract, ((), ()))

    def kern(*refs):
        if bias is None:
            a_ref, b_ref, o_ref = refs
        else:
            a_ref, b_ref, bias_ref, o_ref = refs
        p = lax.dot_general(a_ref[...].astype(BF16), b_ref[...].astype(BF16), dn, preferred_element_type=F32)
        if bias is not None:
            p = p + bias_ref[...]
        if k_axis is None:
            o_ref[...] = p.astype(o_ref.dtype)
        else:
            k = pl.program_id(k_axis)

            @pl.when(k == 0)
            def _():
                o_ref[...] = p

            @pl.when(k > 0)
            def _():
                o_ref[...] += p

    in_specs = [a_spec, b_spec]
    args = [a, b]
    if bias is not None:
        in_specs.append(pl.BlockSpec(bias.shape, lambda *_: (0,) * bias.ndim))
        args.append(bias)
    return pl.pallas_call(kern, name=name, grid=grid, in_specs=in_specs, out_specs=o_spec, out_shape=out_sds,
                          compiler_params=_params())(*args)


def _mm_rows(name, a, b, tm, trans_b=False, bias=None):
    m, k = a.shape
    n = b.shape[0] if trans_b else b.shape[1]
    contract = ((1,), (1,)) if trans_b else ((1,), (0,))
    return _mm(name, a, b, jax.ShapeDtypeStruct((m, n), F32), (m // tm,),
               pl.BlockSpec((tm, k), lambda i: (i, 0)), pl.BlockSpec(b.shape, lambda i: (0, 0)),
               pl.BlockSpec((tm, n), lambda i: (i, 0)), contract, bias=bias)


def _mm_blk(name, a, bblk, tm):
    m, k = a.shape
    nb, _, nblk = bblk.shape
    return _mm(name, a, bblk, jax.ShapeDtypeStruct((m, nb * nblk), F32), (nb, m // tm),
               pl.BlockSpec((tm, k), lambda j, i: (i, 0)), pl.BlockSpec((None, k, nblk), lambda j, i: (j, 0, 0)),
               pl.BlockSpec((tm, nblk), lambda j, i: (i, j)), ((1,), (0,)))


def _mm_blk_t(name, a, bblk, tm):
    m = a.shape[0]
    nb, k, nblk = bblk.shape
    return _mm(name, a, bblk, jax.ShapeDtypeStruct((m, k), F32), (m // tm, nb),
               pl.BlockSpec((tm, nblk), lambda i, j: (i, j)), pl.BlockSpec((None, k, nblk), lambda i, j: (j, 0, 0)),
               pl.BlockSpec((tm, k), lambda i, j: (i, 0)), ((1,), (1,)), k_axis=1)


def _mm_wgrad(name, a, b, tk):
    t, k = a.shape
    n = b.shape[1]
    tn = n if k * n * 4 <= 4 * 1024 * 1024 else 512
    return _mm(name, a, b, jax.ShapeDtypeStruct((k, n), F32), (n // tn, t // tk),
               pl.BlockSpec((tk, k), lambda j, i: (i, 0)), pl.BlockSpec((tk, tn), lambda j, i: (i, j)),
               pl.BlockSpec((k, tn), lambda j, i: (0, j)), ((0,), (0,)), k_axis=1)


def _mm_wgrad_blk(name, a, b, nb, tk):
    t, k = a.shape
    nblk = b.shape[1] // nb
    return _mm(name, a, b, jax.ShapeDtypeStruct((nb, k, nblk), F32), (nb, t // tk),
               pl.BlockSpec((tk, k), lambda j, i: (i, 0)), pl.BlockSpec((tk, nblk), lambda j, i: (i, j)),
               pl.BlockSpec((None, k, nblk), lambda j, i: (j, 0, 0)), ((0,), (0,)), k_axis=1)


def _attn_fwd(name, q, k, v, tq, gather=()):
    nh, s, hd = q.shape
    r = k.shape[1]
    ng = len(gather)
    n_i = s // tq
    last = nh * n_i - 1
    forward_at = (7 * last) // 8

    def kern(q_ref, k_ref, v_ref, *rest):
        o_ref, lse_ref = rest[ng], rest[ng + 1]
        if ng:
            start, forward, finish = _gather_phases(rest[:ng], rest[ng + 2:2 * ng + 2], *rest[2 * ng + 2:])
            step = pl.program_id(0) * n_i + pl.program_id(1)
            pl.when(step == 0)(start)
            pl.when(step == forward_at)(forward)
        sc = lax.dot_general(q_ref[...], k_ref[...], (((1,), (1,)), ((), ())), preferred_element_type=F32)
        m = jnp.max(sc, axis=-1, keepdims=True)
        p = jnp.exp(sc - m)
        l = jnp.sum(p, axis=-1, keepdims=True)
        o = jnp.dot(p.astype(BF16), v_ref[...], preferred_element_type=F32)
        o_ref[...] = o / l
        lse_ref[...] = m + jnp.log(l)
        if ng:
            pl.when(step == last)(finish)

    return pl.pallas_call(
        kern, name=name, grid=(nh, n_i),
        in_specs=[pl.BlockSpec((None, tq, hd), lambda h, i: (h, i, 0)),
                  pl.BlockSpec((None, r, hd), lambda h, i: (h // Q_GROUP, 0, 0)),
                  pl.BlockSpec((None, r, hd), lambda h, i: (h // Q_GROUP, 0, 0))] + [_ANY] * ng,
        out_specs=[pl.BlockSpec((None, tq, hd), lambda h, i: (h, i, 0)),
                   pl.BlockSpec((None, tq, 1), lambda h, i: (h, i, 0))] + [_ANY] * ng,
        out_shape=[jax.ShapeDtypeStruct((nh, s, hd), F32), jax.ShapeDtypeStruct((nh, s, 1), F32)]
        + [jax.ShapeDtypeStruct((N_DEV,) + a.shape, a.dtype) for a in gather],
        scratch_shapes=_gather_scratch(ng) if ng else [],
        compiler_params=_params())(q, k, v, *gather)


def _attn_bwd(name, q, k, v, o, lse, do, tq, exchange=()):
    nh, s, hd = q.shape
    nkv, r, _ = k.shape
    ne = len(exchange)
    n_i = s // tq
    last = nh * n_i - 1

    def kern(q_ref, k_ref, v_ref, o_ref, lse_ref, do_ref, *rest):
        dq_ref, dk_ref, dv_ref = rest[ne:ne + 3]
        if ne:
            start, finish = _exchange_phases(rest[:ne], rest[ne + 3:2 * ne + 3], *rest[2 * ne + 3:])
            step = (pl.program_id(0) * Q_GROUP + pl.program_id(1)) * n_i + pl.program_id(2)
            pl.when(step == 0)(start)

        @pl.when((pl.program_id(1) == 0) & (pl.program_id(2) == 0))
        def _():
            dk_ref[...] = jnp.zeros(dk_ref.shape, F32)
            dv_ref[...] = jnp.zeros(dv_ref.shape, F32)

        qv = q_ref[...]
        kv = k_ref[...]
        dov = do_ref[...]
        delta = jnp.sum(dov * o_ref[...], axis=-1, keepdims=True)
        sc = lax.dot_general(qv, kv, (((1,), (1,)), ((), ())), preferred_element_type=F32)
        p = jnp.exp(sc - lse_ref[...])
        dob = dov.astype(BF16)
        dp = lax.dot_general(dob, v_ref[...], (((1,), (1,)), ((), ())), preferred_element_type=F32)
        dsb = (p * (dp - delta)).astype(BF16)
        dq_ref[...] = jnp.dot(dsb, kv, preferred_element_type=F32)
        dk_ref[...] += lax.dot_general(dsb, qv, (((0,), (0,)), ((), ())), preferred_element_type=F32)
        dv_ref[...] += lax.dot_general(p.astype(BF16), dob, (((0,), (0,)), ((), ())), preferred_element_type=F32)
        if ne:
            pl.when(step == last)(finish)

    qmap = lambda j, g, i: (j * Q_GROUP + g, i, 0)
    kvmap = lambda j, g, i: (j, 0, 0)
    return pl.pallas_call(
        kern, name=name, grid=(nkv, Q_GROUP, n_i),
        in_specs=[pl.BlockSpec((None, tq, hd), qmap), pl.BlockSpec((None, r, hd), kvmap),
                  pl.BlockSpec((None, r, hd), kvmap), pl.BlockSpec((None, tq, hd), qmap),
                  pl.BlockSpec((None, tq, 1), qmap), pl.BlockSpec((None, tq, hd), qmap)] + [_ANY] * ne,
        out_specs=[pl.BlockSpec((None, tq, hd), qmap), pl.BlockSpec((None, r, hd), kvmap),
                   pl.BlockSpec((None, r, hd), kvmap)] + [_ANY] * ne,
        out_shape=[jax.ShapeDtypeStruct((nh, s, hd), F32), jax.ShapeDtypeStruct((nkv, r, hd), F32),
                   jax.ShapeDtypeStruct((nkv, r, hd), F32)]
        + [jax.ShapeDtypeStruct(a.shape, a.dtype) for a in exchange],
        scratch_shapes=_exchange_scratch(ne) if ne else [],
        compiler_params=_params())(q, k, v, o, lse, do, *exchange)


def _mod_fwd(name, c16, wm, bm):
    n = wm.shape[2]

    def kern(c_ref, w_ref, b_ref, o_ref):
        cv = c_ref[...]
        a = (cv * _sigmoid(cv)).astype(BF16)
        o_ref[...] = jnp.dot(a, w_ref[...], preferred_element_type=F32) + b_ref[...]

    return pl.pallas_call(
        kern, name=name, grid=(2,),
        in_specs=[pl.BlockSpec(c16.shape, lambda l: (0, 0)), pl.BlockSpec((None, D_MODEL, n), lambda l: (l, 0, 0)),
                  pl.BlockSpec((None, 1, n), lambda l: (l, 0, 0))],
        out_specs=pl.BlockSpec((None, 16, n), lambda l: (l, 0, 0)),
        out_shape=jax.ShapeDtypeStruct((2, 16, n), F32), compiler_params=_params())(c16, wm, bm)


def _mod_wgrad(name, c16, dm):
    n = dm.shape[2]
    tn = 512

    def kern(c_ref, d_ref, o_ref):
        cv = c_ref[...]
        a = cv * _sigmoid(cv)
        o_ref[...] = lax.dot_general(a, d_ref[...], (((0,), (0,)), ((), ())), preferred_element_type=F32,
                                     precision=lax.Precision.HIGHEST)

    return pl.pallas_call(
        kern, name=name, grid=(2, n // tn),
        in_specs=[pl.BlockSpec(c16.shape, lambda l, j: (0, 0)), pl.BlockSpec((None, 16, tn), lambda l, j: (l, 0, j))],
        out_specs=pl.BlockSpec((None, D_MODEL, tn), lambda l, j: (l, 0, j)),
        out_shape=jax.ShapeDtypeStruct((2, D_MODEL, n), F32), compiler_params=_params())(c16, dm)


def _cctx_grad(name, parts, c_ctx):
    def kern(p_ref, c_ref, o_ref):
        d = p_ref[0, 0:1, :] + p_ref[2, 0:1, :] + p_ref[4, 0:1, :] + p_ref[6, 0:1, :]
        cv = c_ref[...]
        sg = _sigmoid(cv)
        o_ref[...] = d * (sg * (1.0 + cv * (1.0 - sg)))

    return pl.pallas_call(kern, name=name, out_shape=jax.ShapeDtypeStruct((1, D_MODEL), F32),
                          compiler_params=_params())(parts, c_ctx)


def _sum_slots(name, g, tr):
    n, rows, cols = g.shape

    def kern(g_ref, o_ref):
        acc = g_ref[0].astype(F32)
        for i in range(1, n):
            acc = acc + g_ref[i].astype(F32)
        o_ref[...] = acc

    return pl.pallas_call(kern, name=name, grid=(rows // tr,),
                          in_specs=[pl.BlockSpec((n, tr, cols), lambda i: (0, i, 0))],
                          out_specs=pl.BlockSpec((tr, cols), lambda i: (i, 0)),
                          out_shape=jax.ShapeDtypeStruct((rows, cols), F32), compiler_params=_params())(g)


def _sum_into(name, g, cvec, n_layers, layer, prev=None):
    n, r, cc = g.shape
    tr = _row_tile(r, cc, 512 * 1024)

    def kern(c_ref, g_ref, *rest):
        acc = g_ref[0].astype(F32)
        for i in range(1, n):
            acc = acc + g_ref[i].astype(F32)
        rest[-1][...] = acc

    in_specs = [pl.BlockSpec((n, tr, cc), lambda i, c_ref: (0, i, 0))]
    args = [cvec, g]
    aliases = {}
    if prev is not None:
        in_specs.append(_ANY)
        args.append(prev)
        aliases = {2: 0}
    grid_spec = pltpu.PrefetchScalarGridSpec(
        num_scalar_prefetch=1, grid=(r // tr,), in_specs=in_specs,
        out_specs=pl.BlockSpec((None, None, tr, cc), lambda i, c_ref: (layer, c_ref[0], i, 0)))
    return pl.pallas_call(kern, name=name, grid_spec=grid_spec,
                          out_shape=jax.ShapeDtypeStruct((n_layers, 2, r, cc), F32),
                          input_output_aliases=aliases, compiler_params=_params())(*args)


def _row_tile(rows, cols, max_bytes=1024 * 1024):
    if rows * cols * 4 <= 2 * max_bytes:
        return rows
    best = None
    for t in range(16, rows + 1, 16):
        if rows % t == 0 and t * cols * 4 <= max_bytes:
            best = t
    assert best is not None, (rows, cols)
    return best


def _adamw(name, w, g, m, v):
    shape = w.shape
    cols = shape[-1]
    rows = w.size // cols
    tr = _row_tile(rows, cols)

    def kern(w_ref, g_ref, m_ref, v_ref, d_ref, nm_ref, nv_ref):
        gv = g_ref[...]
        m2 = ADAM_B1 * m_ref[...] + (1.0 - ADAM_B1) * gv
        v2 = ADAM_B2 * v_ref[...] + (1.0 - ADAM_B2) * (gv * gv)
        m_hat = m2 / (1.0 - ADAM_B1 ** ADAM_STEP)
        v_hat = v2 / (1.0 - ADAM_B2 ** ADAM_STEP)
        d_ref[...] = -ADAM_LR * (m_hat / (jnp.sqrt(v_hat) + ADAM_EPS) + ADAM_WD * w_ref[...])
        nm_ref[...] = m2
        nv_ref[...] = v2

    spec = pl.BlockSpec((tr, cols), lambda i: (i, 0))
    sds = jax.ShapeDtypeStruct((rows, cols), F32)
    outs = pl.pallas_call(kern, name=name, grid=(rows // tr,), in_specs=[spec] * 4, out_specs=[spec] * 3,
                          out_shape=[sds] * 3, compiler_params=_params())(
        w.reshape(rows, cols), g.reshape(rows, cols), m.reshape(rows, cols), v.reshape(rows, cols))
    return tuple(o.reshape(shape) for o in outs)


_ANY = pl.BlockSpec(memory_space=pl.ANY)


def _mesh_pos():
    return lax.axis_index("x"), lax.axis_index("y"), lax.axis_index("c")


def _gather_phases(srcs, outs, send_sems, recv_sems, local_sems):
    n = len(srcs)
    x, y, c = _mesh_pos()
    me = (x, y, c)
    sibling = (x, y, 1 - c)
    chips = [(1 - x, y), (x, 1 - y), (1 - x, 1 - y)]

    def slot(px, py, pc):
        return 4 * px + 2 * py + pc

    def copy(t, k, s, to, src=None):
        dst = outs[t].at[s]
        return pltpu.make_async_remote_copy(src_ref=dst if src is None else src, dst_ref=dst,
                                            send_sem=send_sems.at[t, k], recv_sem=recv_sems.at[t, k],
                                            device_id=to, device_id_type=MESH)

    def mine(t):
        return pltpu.make_async_copy(srcs[t], outs[t].at[slot(*me)], local_sems.at[t])

    def first(t):
        return [copy(t, 0, slot(*me), sibling, src=srcs[t])] + [
            copy(t, 1 + j, slot(*me), (px, py, c), src=srcs[t]) for j, (px, py) in enumerate(chips)]

    def passed(t, j):
        px, py = chips[j]
        return copy(t, 4 + j, slot(px, py, c), sibling)

    def start():
        for t in range(n):
            mine(t).start()
        for t in range(n):
            for cp in first(t):
                cp.start()

    def forward():
        for j, (px, py) in enumerate(chips):
            for t in range(n):
                copy(t, 1 + j, slot(px, py, c), me).wait_recv()
                passed(t, j).start()

    def finish():
        for t in range(n):
            copy(t, 0, slot(x, y, 1 - c), me).wait_recv()
            for j, (px, py) in enumerate(chips):
                copy(t, 4 + j, slot(px, py, 1 - c), me).wait_recv()
        for t in range(n):
            for cp in first(t) + [passed(t, j) for j in range(len(chips))]:
                cp.wait_send()
            mine(t).wait()

    return start, forward, finish


def _gather_scratch(n):
    return [pltpu.SemaphoreType.DMA((n, 7)), pltpu.SemaphoreType.DMA((n, 7)), pltpu.SemaphoreType.DMA((n,))]


def _allgather(name, items):
    n = len(items)

    def body(*refs):
        start, forward, finish = _gather_phases(refs[:n], refs[n:2 * n], *refs[2 * n:])
        start()
        forward()
        finish()

    return pl.pallas_call(
        body, name=name, in_specs=[_ANY] * n, out_specs=[_ANY] * n,
        out_shape=[jax.ShapeDtypeStruct((N_DEV,) + a.shape, a.dtype) for a in items],
        scratch_shapes=_gather_scratch(n), compiler_params=_params())(*items)


def _rs_pair_send(name, gs):
    n = len(gs)

    def body(*refs):
        srcs = refs[:n]
        bufs = refs[n:2 * n]
        send_sems, recv_sems = refs[2 * n:]
        x, y, c = _mesh_pos()
        copies = []
        for t in range(n):
            for k in range(N_CHIP):
                copies.append(pltpu.make_async_remote_copy(
                    src_ref=srcs[t].at[k, 1 - c], dst_ref=bufs[t].at[k], send_sem=send_sems.at[t, k],
                    recv_sem=recv_sems.at[t, k], device_id=(x, y, 1 - c), device_id_type=MESH))
        for cp in copies:
            cp.start()
        for cp in copies:
            cp.wait()

    return pl.pallas_call(
        body, name=name, in_specs=[_ANY] * n, out_specs=[_ANY] * n,
        out_shape=[jax.ShapeDtypeStruct((N_CHIP,) + g.shape[2:], g.dtype) for g in gs],
        scratch_shapes=[pltpu.SemaphoreType.DMA((n, N_CHIP)), pltpu.SemaphoreType.DMA((n, N_CHIP))],
        compiler_params=_params())(*gs)


def _rs_pair_add(name, g, buf, cvec):
    _, _, r, cc = g.shape
    tr = _row_tile(r, cc)

    def kern(c_ref, g_ref, b_ref, o_ref):
        o_ref[...] = (g_ref[...] + b_ref[...]).astype(BF16)

    grid_spec = pltpu.PrefetchScalarGridSpec(
        num_scalar_prefetch=1, grid=(N_CHIP, r // tr),
        in_specs=[pl.BlockSpec((None, None, tr, cc), lambda k, i, c_ref: (k, c_ref[0], i, 0)),
                  pl.BlockSpec((None, tr, cc), lambda k, i, c_ref: (k, i, 0))],
        out_specs=pl.BlockSpec((None, tr, cc), lambda k, i, c_ref: (k, i, 0)))
    return pl.pallas_call(kern, name=name, grid_spec=grid_spec,
                          out_shape=jax.ShapeDtypeStruct((N_CHIP, r, cc), BF16),
                          compiler_params=_params())(cvec, g, buf)


def _exchange_phases(srcs, bufs, send_sems, recv_sems, local_sems):
    n = len(srcs)
    x, y, c = _mesh_pos()
    kme = 2 * x + y
    chips = [(1 - x, y), (x, 1 - y), (1 - x, 1 - y)]

    def copies():
        local = [pltpu.make_async_copy(srcs[t].at[kme], bufs[t].at[kme], local_sems.at[t]) for t in range(n)]
        remote = []
        for t in range(n):
            for j, (px, py) in enumerate(chips):
                remote.append(pltpu.make_async_remote_copy(
                    src_ref=srcs[t].at[2 * px + py], dst_ref=bufs[t].at[kme], send_sem=send_sems.at[t, j],
                    recv_sem=recv_sems.at[t, j], device_id=(px, py, c), device_id_type=MESH))
        return local, remote

    def start():
        local, remote = copies()
        for cp in local + remote:
            cp.start()

    def finish():
        local, remote = copies()
        for cp in remote + local:
            cp.wait()

    return start, finish


def _exchange_scratch(n):
    return [pltpu.SemaphoreType.DMA((n, 3)), pltpu.SemaphoreType.DMA((n, 3)), pltpu.SemaphoreType.DMA((n,))]


def _rs_chip_exchange(name, ss):
    n = len(ss)

    def body(*refs):
        start, finish = _exchange_phases(refs[:n], refs[n:2 * n], *refs[2 * n:])
        start()
        finish()

    return pl.pallas_call(
        body, name=name, in_specs=[_ANY] * n, out_specs=[_ANY] * n,
        out_shape=[jax.ShapeDtypeStruct(s.shape, s.dtype) for s in ss],
        scratch_shapes=_exchange_scratch(n), compiler_params=_params())(*ss)


def _rs_pair_share(name, dsts):
    nd = len(dsts)

    def body(*refs):
        ins = refs[:nd]
        outs = refs[nd:2 * nd]
        send_sems, recv_sems = refs[2 * nd:]
        x, y, c = _mesh_pos()
        copies = []
        for d in range(nd):
            for l in range(dsts[d].shape[0]):
                copies.append(pltpu.make_async_remote_copy(
                    src_ref=ins[d].at[l, c], dst_ref=outs[d].at[l, c], send_sem=send_sems.at[d, l],
                    recv_sem=recv_sems.at[d, l], device_id=(x, y, 1 - c), device_id_type=MESH))
        for cp in copies:
            cp.start()
        for cp in copies:
            cp.wait()

    return pl.pallas_call(
        body, name=name, in_specs=[_ANY] * nd, out_specs=[_ANY] * nd,
        out_shape=[jax.ShapeDtypeStruct(a.shape, a.dtype) for a in dsts],
        input_output_aliases={d: d for d in range(nd)},
        scratch_shapes=[pltpu.SemaphoreType.DMA((nd, 2)), pltpu.SemaphoreType.DMA((nd, 2))],
        compiler_params=_params())(*dsts)


def _rope_tables(seq, n_ctx):
    t = jnp.arange(seq)
    row = (t // GRID_W).astype(F32)
    col = (t % GRID_W).astype(F32)
    inv = ROPE_THETA ** (-jnp.arange(0, HEAD_DIM // 2, 2, dtype=F32) / (HEAD_DIM // 2))
    ang_r = row[:, None] * inv[None, :]
    ang_c = col[:, None] * inv[None, :]
    cos = jnp.concatenate([jnp.cos(ang_r)] * 2 + [jnp.cos(ang_c)] * 2, axis=1)
    sin = jnp.concatenate([-jnp.sin(ang_r), jnp.sin(ang_r), -jnp.sin(ang_c), jnp.sin(ang_c)], axis=1)
    cos = jnp.concatenate([jnp.ones((n_ctx, HEAD_DIM), F32), cos], axis=0)
    sin = jnp.concatenate([jnp.zeros((n_ctx, HEAD_DIM), F32), sin], axis=0)
    return jnp.tile(cos, (1, N_Q_HEADS)), jnp.tile(sin, (1, N_Q_HEADS))


def _to_heads(a, nh):
    return a.reshape(a.shape[0], nh, HEAD_DIM).transpose(1, 0, 2)


def _from_heads(a):
    return a.transpose(1, 0, 2).reshape(a.shape[1], a.shape[0] * HEAD_DIM)


def _sample_step(x, ctx, tgt, mod, cmod, W, pending, cvec):
    W = dict(W)
    seq = x.shape[0]
    n_ctx = ctx.shape[0]
    tm = min(256, seq)
    tmc = min(256, n_ctx)
    tmm = 512 if seq % 512 == 0 else tm
    tq = min(256, seq)
    tms = min(128, seq)
    assert n_ctx % tm == 0 and seq % tm == 0

    def mv(l, j):
        return mod[l, j * D_MODEL:(j + 1) * D_MODEL].reshape(1, D_MODEL)

    csh1 = cmod[:D_MODEL].reshape(1, D_MODEL)
    csc1 = cmod[D_MODEL:2 * D_MODEL].reshape(1, D_MODEL)
    g_mix = [W['g_mix'][l:l + 1] for l in range(2)]
    g_ffn = [W['g_ffn'][l:l + 1] for l in range(2)]
    cos, sin = _rope_tables(seq, n_ctx)
    gq = jnp.tile(W['q_gain'], (1, N_Q_HEADS))
    gk = jnp.tile(W['k_gain'], (1, N_KV_HEADS))
    bdq = jnp.kron(jnp.eye(N_Q_HEADS, dtype=F32), jnp.ones((HEAD_DIM, HEAD_DIM), F32)).astype(BF16)
    bdk = bdq[:KV_W, :KV_W]
    w_sp = W['w_sp'].astype(BF16)
    w_sp_t = w_sp.transpose(0, 2, 1)
    bsp = jnp.broadcast_to(W['b_sp'].T[:, :, None], (CHUNK, N_SG, CHUNK)).reshape(CHUNK, SG_W)

    def ffn_fwd(l, h_mid):
        xf = _modnorm_fwd(f'l{l}_ffn_norm', h_mid, g_ffn[l], mv(l, 4), mv(l, 3), tm)
        gu = _mm_blk(f'l{l}_ffn_in', xf, W['ffn_in'][l], tmm)
        a = _swiglu_fwd(f'l{l}_swiglu', gu, tms)
        f = _mm_rows(f'l{l}_ffn_out', a, W['ffn_out'][l], tmm)
        h_out = _resgate_fwd(f'l{l}_ffn_res', h_mid, f, mv(l, 5), tm)
        return h_out, (h_mid, xf, gu, a, f)

    def ffn_bwd(l, dh_out, saved):
        h_mid, xf, gu, a, f = saved
        dy, d_gt2, _ = _resgate_bwd(f'l{l}_ffn_res_bwd', dh_out, f, mv(l, 5), tm)
        da = _mm_rows(f'l{l}_ffn_out_dx', dy, W['ffn_out'][l], tmm, trans_b=True)
        d_wo = _mm_wgrad(f'l{l}_ffn_out_dw', a, dy, tmm)
        dgu = _swiglu_bwd(f'l{l}_swiglu_bwd', gu, da, tms)
        dxf = _mm_blk_t(f'l{l}_ffn_in_dx', dgu, W['ffn_in'][l], tmm)
        d_wi = _mm_wgrad_blk(f'l{l}_ffn_in_dw', xf, dgu, N_CHIP, tmm)
        dh_mid, d_sh2, d_sc2, d_g = _modnorm_bwd(f'l{l}_ffn_norm_bwd', h_mid, [_ri(dxf)], dh_out, g_ffn[l],
                                                 mv(l, 4), tm)
        return dh_mid, d_wi, d_wo, d_g, (d_sh2, d_sc2, d_gt2)

    xm0 = _modnorm_fwd('l0_mix_norm', x, g_mix[0], mv(0, 1), mv(0, 0), tm)
    xc0 = _modnorm_fwd('l0_ctx_norm', ctx, g_mix[0], csc1, csh1, tmc)
    xall = jnp.concatenate([xc0, xm0], axis=0)
    p_qkv = _mm_rows('l0_qkv', xall, W['w_qkv'], tm)
    p_sg = _mm_rows('l0_sg_in', xm0, W['w_sgp'], tmm)
    q_all, k_all, v_all = _qk_prep_fwd('l0_qk_prep', p_qkv, cos, sin, gq, gk, bdq, bdk, tm)
    qh = _to_heads(q_all[n_ctx:], N_Q_HEADS)
    kh = _to_heads(k_all, N_KV_HEADS)
    vh = _to_heads(v_all, N_KV_HEADS)
    o, lse, *gw = _attn_fwd('l0_attn', qh, kh, vh, tq, gather=pending)
    W['ffn_in'] = [gw[0].reshape(N_CHIP, D_MODEL, FF_BLK), gw[4].reshape(N_CHIP, D_MODEL, FF_BLK)]
    W['ffn_out'] = [gw[1].reshape(D_FF, D_MODEL), gw[5].reshape(D_FF, D_MODEL)]
    W['pw1'] = gw[2].reshape(N_CHIP, D_MODEL, PW1_BLK)
    W['pw2'] = gw[3].reshape(D_MODEL, D_MODEL)
    sg = _sg_fwd('l0_sg', p_sg, w_sp, bsp, tm)
    cat = jnp.concatenate([_from_heads(o).astype(BF16), sg], axis=1)
    y0 = _mm_rows('l0_out', cat, W['w_out'], tmm)
    h1 = _resgate_fwd('l0_mix_res', x, y0, mv(0, 2), tm)
    h2, ffn0 = ffn_fwd(0, h1)

    xm1 = _modnorm_fwd('l1_mix_norm', h2, g_mix[1], mv(1, 1), mv(1, 0), tm)
    ag = _mm_blk('l1_pw1', xm1, W['pw1'], tmm)
    hg = _glu_fwd('l1_glu', ag, W['b_pw1'], tm)
    hc = _dwconv('l1_conv', hg, W['w_dw'], W['b_dw'])
    s1 = _lnsilu_fwd('l1_ln_silu', hc, W['ln_g'], W['ln_b'], tm)
    y1 = _mm_rows('l1_pw2', s1, W['pw2'], tmm, bias=W['b_pw2'])
    h3 = _resgate_fwd('l1_mix_res', h2, y1, mv(1, 2), tm)
    h4, ffn1 = ffn_fwd(1, h3)

    dh4, d_g_final, loss_vec = _final_loss('final_loss', h4, tgt, W['g_final'], tm)

    dh3, d_wi1, d_wo1, d_gffn1, dmod1_ffn = ffn_bwd(1, dh4, ffn1)
    dy1, d_gt1_1, d_b_pw2 = _resgate_bwd('l1_mix_res_bwd', dh3, y1, mv(1, 2), tm)
    ds1 = _mm_rows('l1_pw2_dx', dy1, W['pw2'], tmm, trans_b=True)
    d_pw2 = _mm_wgrad('l1_pw2_dw', s1, dy1, tmm)
    dhc, d_ln_g, d_ln_b = _lnsilu_bwd('l1_ln_silu_bwd', hc, ds1, W['ln_g'], W['ln_b'], tm)
    dhg = _dwconv('l1_conv_dx', dhc, W['w_dw'][::-1], jnp.zeros((1, D_MODEL), F32))
    d_w_dw, d_b_dw = _dwconv_wgrad('l1_conv_dw', hg, dhc)
    dag, d_b_pw1 = _glu_bwd('l1_glu_bwd', ag, dhg, W['b_pw1'], tm)
    dxm1 = _mm_blk_t('l1_pw1_dx', dag, W['pw1'], tmm)
    d_pw1 = _mm_wgrad_blk('l1_pw1_dw', xm1, dag, N_CHIP, tmm)
    dh2, d_sh1_1, d_sc1_1, d_gmix1 = _modnorm_bwd('l1_mix_norm_bwd', h2, [_ri(dxm1)], dh3, g_mix[1], mv(1, 1), tm)

    dh1, d_wi0, d_wo0, d_gffn0, dmod0_ffn = ffn_bwd(0, dh2, ffn0)
    dy0, d_gt1_0, _ = _resgate_bwd('l0_mix_res_bwd', dh1, y0, mv(0, 2), tm)
    dcat = _mm_rows('l0_out_dx', dy0, W['w_out'], tmm, trans_b=True)
    d_w_out = _mm_wgrad('l0_out_dw', cat, dy0, tmm)
    do = _to_heads(dcat[:, :ATTN_W], N_Q_HEADS)

    def blocked(a):
        cols = a.shape[-1]
        return a.reshape(N_CHIP, 2, a.size // (2 * N_CHIP * cols), cols)

    gs = [blocked(g) for g in (d_wi0, d_wi1, d_wo0, d_wo1, d_w_out, d_pw1, d_pw2)]
    sib = _rs_pair_send('rs_pair_send', gs)
    ss = [_rs_pair_add(f'rs_pair_add{t}', gs[t], sib[t], cvec) for t in range(len(gs))]
    dq, dk, dv, *xs = _attn_bwd('l0_attn_bwd', qh, kh, vh, o, lse, do, min(128, seq), exchange=ss)
    dq_all = jnp.concatenate([jnp.zeros((n_ctx, ATTN_W), F32), _from_heads(dq)], axis=0)
    dp_qkv, d_gq, d_gk = _qk_prep_bwd('l0_qk_prep_bwd', p_qkv, cos, sin, dq_all, _from_heads(dk), _from_heads(dv),
                                      gq, gk, bdq, bdk, tm)
    dp_sg, d_w_sp, d_bsp = _sg_bwd('l0_sg_bwd', p_sg, dcat, w_sp, w_sp_t, bsp, tm)
    d_w_qkv = _mm_wgrad('l0_qkv_dw', xall, dp_qkv, tm)
    d_w_sgp = _mm_wgrad('l0_sg_in_dw', xm0, dp_sg, tmm)
    dxall = _mm_rows('l0_qkv_dx', dp_qkv, W['w_qkv'], tm, trans_b=True)
    dxm_sg = _mm_rows('l0_sg_in_dx', dp_sg, W['w_sgp'], tmm, trans_b=True)
    dx, d_sh1_0, d_sc1_0, d_gmix0 = _modnorm_bwd('l0_mix_norm_bwd', x, [_ri(dxall, None, 0, n_ctx // tm), _ri(dxm_sg)],
                                                 dh1, g_mix[0], mv(0, 1), tm)
    _, d_csh1, d_csc1, d_gmix0c = _modnorm_bwd('l0_ctx_norm_bwd', ctx, [_ri(dxall)], None, g_mix[0], csc1, tmc)

    g_in = blocked(jnp.concatenate([d_w_qkv, d_w_sgp], axis=1).reshape(D_MODEL, N_CHIP, -1).transpose(1, 0, 2))
    sib_in = _rs_pair_send('rs_pair_send_w_in', [g_in])
    xs_in = _rs_chip_exchange('rs_chip_exchange_w_in', [_rs_pair_add('rs_pair_add_w_in', g_in, sib_in[0], cvec)])
    t_ffn_in = _sum_into('rs_sum_ffn_in1', xs[1], cvec, 2, 1, prev=_sum_into('rs_sum_ffn_in0', xs[0], cvec, 2, 0))
    t_ffn_out = _sum_into('rs_sum_ffn_out1', xs[3], cvec, 2, 1, prev=_sum_into('rs_sum_ffn_out0', xs[2], cvec, 2, 0))
    reduced = _rs_pair_share('rs_pair_share', [
        t_ffn_in, t_ffn_out, _sum_into('rs_sum_w_in', xs_in[0], cvec, 1, 0), _sum_into('rs_sum_w_out', xs[4], cvec, 1, 0),
        _sum_into('rs_sum_pw1', xs[5], cvec, 1, 0), _sum_into('rs_sum_pw2', xs[6], cvec, 1, 0)])
    big = dict(zip(['w_ffn_in', 'w_ffn_out', 'w_in', 'w_out', 'w_pw1', 'w_pw2'], reduced))

    zero = jnp.zeros((1, D_MODEL), F32)
    dmod = jnp.concatenate([d_sh1_0, d_sc1_0, d_gt1_0, *dmod0_ffn, d_sh1_1, d_sc1_1, d_gt1_1, *dmod1_ffn], axis=0)
    dcmod = jnp.concatenate([d_csh1, d_csc1, zero, zero, zero, zero], axis=0)
    grads = dict(
        g_mix=jnp.concatenate([d_gmix0 + d_gmix0c, d_gmix1], axis=0),
        g_ffn=jnp.concatenate([d_gffn0, d_gffn1], axis=0), g_final=d_g_final, gq=d_gq, gk=d_gk,
        w_sp=d_w_sp, bsp=d_bsp, b_pw1=d_b_pw1, w_dw=d_w_dw, b_dw=d_b_dw, ln_g=d_ln_g, ln_b=d_ln_b, b_pw2=d_b_pw2,
        dmod=dmod, dcmod=dcmod)
    return loss_vec, dx, grads, big


def kernel(x, c, ctx, c_ctx, w_mod, b_mod, g_mix, g_ffn, w_ffn_in, w_ffn_out, w_in, q_gain, k_gain, w_sp, b_sp, w_out, w_pw1, b_pw1, w_dw, b_dw, ln_g, ln_b, w_pw2, b_pw2, g_final, loss_target, m_c_ctx, m_w_mod, m_b_mod, m_g_mix, m_g_ffn, m_w_ffn_in, m_w_ffn_out, m_w_in, m_q_gain, m_k_gain, m_w_sp, m_b_sp, m_w_out, m_w_pw1, m_b_pw1, m_w_dw, m_b_dw, m_ln_g, m_ln_b, m_w_pw2, m_b_pw2, m_g_final, v_c_ctx, v_w_mod, v_b_mod, v_g_mix, v_g_ffn, v_w_ffn_in, v_w_ffn_out, v_w_in, v_q_gain, v_k_gain, v_w_sp, v_b_sp, v_w_out, v_w_pw1, v_b_pw1, v_w_dw, v_b_dw, v_ln_g, v_ln_b, v_w_pw2, v_b_pw2, v_g_final):
    given = dict(locals())
    ix, iy, ic = _mesh_pos()
    chip = 2 * ix + iy
    me = 2 * chip + ic
    d = D_MODEL
    q4 = d // N_CHIP

    small = jnp.concatenate([c.reshape(4, q4), b_pw1.reshape(2, q4), w_dw[0], b_dw, ln_g, ln_b, b_pw2,
                             jnp.zeros((7, q4), F32)], axis=0)
    def my_half(a):
        r = a.shape[0] // 2
        return lax.dynamic_slice_in_dim(a, ic * r, r, axis=0).astype(BF16)

    sgath, g_w_in, g_w_out = _allgather('ag_first', [small, my_half(w_in[0]), my_half(w_out[0])])
    c_all = sgath[:, 0:4].reshape(N_DEV, d)
    per_chip = sgath[0::2]
    b_pw1_f = per_chip[:, 4:6].reshape(1, 2 * d)
    w_dw_f = per_chip[:, 6:6 + CONV_W].transpose(1, 0, 2).reshape(CONV_W, d)
    b_dw_f, ln_g_f, ln_b_f, b_pw2_f = [per_chip[:, 37 + i].reshape(1, d) for i in range(4)]

    c16 = jnp.concatenate([c_all, c_ctx[None, :], jnp.zeros((7, d), F32)], axis=0)
    wm_b = w_mod.astype(BF16)
    bm_sh = lax.dynamic_slice_in_dim(b_mod, chip * MOD_BLK, MOD_BLK, axis=1).reshape(2, 1, MOD_BLK)
    mod_piece = _mod_fwd('mod_fwd', c16, wm_b, bm_sh)
    mgath = _allgather('ag_mod', [mod_piece.reshape(32, MOD_BLK)])[0]
    mod_all = mgath[0::2].reshape(N_CHIP, 2, 16, MOD_BLK).transpose(1, 2, 0, 3).reshape(2, 16, MOD_W)
    mod_me = lax.dynamic_index_in_dim(mod_all, me, axis=1, keepdims=False)
    cmod = mod_all[0, N_DEV]

    w_in_f = g_w_in.reshape(N_CHIP, d, -1).transpose(1, 0, 2).reshape(d, -1)
    W = dict(
        w_qkv=w_in_f[:, :QKV_W], w_sgp=w_in_f[:, QKV_W:], w_out=g_w_out.reshape(d, d),
        g_mix=g_mix, g_ffn=g_ffn, g_final=g_final.reshape(1, d), q_gain=q_gain, k_gain=k_gain, w_sp=w_sp[0],
        b_sp=b_sp[0], b_pw1=b_pw1_f, w_dw=w_dw_f, b_dw=b_dw_f, ln_g=ln_g_f, ln_b=ln_b_f, b_pw2=b_pw2_f)
    pending = [my_half(w_ffn_in[0]), my_half(w_ffn_out[0]), my_half(w_pw1[0]), my_half(w_pw2[0]),
               my_half(w_ffn_in[1]), my_half(w_ffn_out[1])]
    cvec = jnp.reshape(ic, (1,)).astype(jnp.int32)

    loss_vec, dx, G, big = _sample_step(x[0], ctx[0], loss_target[0], mod_me, cmod, W, pending, cvec)
    loss = lax.psum(jnp.sum(loss_vec), ("x", "y", "c"))

    pack = jnp.concatenate([
        G['g_mix'], G['g_ffn'], G['g_final'],
        jnp.concatenate([G['gq'], G['gk'], jnp.zeros((1, d - ATTN_W - KV_W), F32)], axis=1),
        G['b_pw1'].reshape(2, d), G['w_dw'], G['b_dw'], G['ln_g'], G['ln_b'], G['b_pw2'],
        G['dmod'], G['dcmod'], jnp.zeros((3, d), F32),
        G['w_sp'].reshape(64, d), G['bsp'].reshape(64, d)], axis=0)
    assert pack.shape == (192, d)
    pg = _allgather('ag_small_grads', [pack])[0]
    s8 = _sum_slots('sum_small_grads', pg, 64)

    def chip_cols(a, width):
        return lax.dynamic_slice_in_dim(a, chip * width, width, axis=a.ndim - 1)

    grads = {}
    grads['g_mix'] = s8[0:2]
    grads['g_ffn'] = s8[2:4]
    grads['g_final'] = s8[4]
    grads['q_gain'] = s8[5, :ATTN_W].reshape(N_Q_HEADS, HEAD_DIM).sum(axis=0)[None, :]
    grads['k_gain'] = s8[5, ATTN_W:ATTN_W + KV_W].reshape(N_KV_HEADS, HEAD_DIM).sum(axis=0)[None, :]
    grads['b_pw1'] = chip_cols(s8[6:8].reshape(1, 2 * d), 2 * q4)
    grads['w_dw'] = chip_cols(s8[8:8 + CONV_W], q4)[None]
    grads['b_dw'] = chip_cols(s8[39:40], q4)
    grads['ln_g'] = chip_cols(s8[40:41], q4)
    grads['ln_b'] = chip_cols(s8[41:42], q4)
    grads['b_pw2'] = chip_cols(s8[42:43], q4)
    grads['w_sp'] = s8[64:128].reshape(1, N_SG, CHUNK, CHUNK)
    grads['b_sp'] = s8[128:192].reshape(CHUNK, N_SG, CHUNK).sum(axis=-1).T[None]

    dm_rows = pg[:, 43:55].reshape(N_DEV, 2, MOD_W)
    ctx_row = s8[55:61].reshape(1, MOD_W)
    dm = jnp.stack([jnp.concatenate([dm_rows[:, 0], ctx_row, jnp.zeros((7, MOD_W), F32)], axis=0),
                    jnp.concatenate([dm_rows[:, 1], jnp.zeros((8, MOD_W), F32)], axis=0)], axis=0)
    grads['b_mod'] = jnp.stack([s8[43:49].reshape(MOD_W) + ctx_row[0], s8[49:55].reshape(MOD_W)], axis=0)
    dm_sh = chip_cols(dm, MOD_BLK)
    grads['w_mod'] = _mod_wgrad('mod_dw', c16, dm_sh)
    d_silu = _mm_rows('mod_dx', dm_sh[0], wm_b[0], 16, trans_b=True)
    parts = _allgather('ag_cctx', [d_silu[8:16]])[0]
    grads['c_ctx'] = _cctx_grad('cctx_grad', parts, c_ctx.reshape(1, d))[0]

    grads.update(big)

    deltas, new_m, new_v = {}, {}, {}
    for n in WEIGHT_NAMES:
        grads[n] = grads[n].reshape(given[n].shape)
        deltas[n], new_m[n], new_v[n] = _adamw(f'adamw_{n}', given[n], grads[n], given['m_' + n], given['v_' + n])

    return (loss, dx[None], *[grads[n] for n in WEIGHT_NAMES], *[deltas[n] for n in WEIGHT_NAMES],
            *[new_m[n] for n in WEIGHT_NAMES], *[new_v[n] for n in WEIGHT_NAMES])
```

```python
import functools
import math

import jax
import jax.numpy as jnp
from jax import lax
from jax.experimental import pallas as pl
from jax.experimental.pallas import tpu as pltpu

F32 = jnp.float32
BF16 = jnp.bfloat16
MESH = pl.DeviceIdType.MESH

D_MODEL = 1024
HEAD_DIM = 64
N_Q_HEADS = 8
N_KV_HEADS = 2
Q_GROUP = N_Q_HEADS // N_KV_HEADS
ATTN_W = N_Q_HEADS * HEAD_DIM
KV_W = N_KV_HEADS * HEAD_DIM
QKV_W = ATTN_W + 2 * KV_W
SG_W = D_MODEL - ATTN_W
N_SG = 4
CHUNK = 128
GRID_W = 64
ROPE_THETA = 10000.0
CONV_W = 31
CONV_HALO = 16
D_FF = 2816
MOD_W = 6 * D_MODEL
N_CHIP = 4
N_DEV = 8
FF_BLK = 2 * D_FF // N_CHIP
PW1_BLK = 2 * D_MODEL // N_CHIP
MOD_BLK = MOD_W // N_CHIP
EPS = 1e-6
GELU_C = math.sqrt(2.0 / math.pi)
GELU_A = 0.044715

ADAM_LR = 0.001
ADAM_B1 = 0.9
ADAM_B2 = 0.999
ADAM_EPS = 1e-08
ADAM_WD = 0.01
ADAM_STEP = 10

VMEM_LIMIT_BYTES = 48 * 1024 * 1024

WEIGHT_NAMES = ['c_ctx', 'w_mod', 'b_mod', 'g_mix', 'g_ffn', 'w_ffn_in', 'w_ffn_out', 'w_in', 'q_gain', 'k_gain',
                'w_sp', 'b_sp', 'w_out', 'w_pw1', 'b_pw1', 'w_dw', 'b_dw', 'ln_g', 'ln_b', 'w_pw2', 'b_pw2',
                'g_final']


def _params():
    return pltpu.CompilerParams(vmem_limit_bytes=VMEM_LIMIT_BYTES)


def _ri(arr, width=None, col_block=0, row_off=0):
    return (arr, arr.shape[1] if width is None else width, col_block, row_off)


def _rowwise(name, body, n_tiles, tm, row_ins, vec_ins, row_outs, vec_outs):
    nri, nvi, nro = len(row_ins), len(vec_ins), len(row_outs)

    def kern(*refs):
        ri = refs[:nri]
        vi = refs[nri:nri + nvi]
        ro = refs[nri + nvi:nri + nvi + nro]
        vo = refs[nri + nvi + nro:]
        if vo:
            @pl.when(pl.program_id(0) == 0)
            def _():
                for r in vo:
                    r[...] = jnp.zeros(r.shape, r.dtype)
        body(ri, vi, ro, vo)

    in_specs = [pl.BlockSpec((tm, bw), lambda i, cb=cb, off=off: (i + off, cb)) for (_, bw, cb, off) in row_ins]
    in_specs += [pl.BlockSpec(a.shape, lambda i, nd=a.ndim: (0,) * nd) for a in vec_ins]
    out_specs = [pl.BlockSpec((tm, w), lambda i: (i, 0)) for (w, _) in row_outs]
    out_specs += [pl.BlockSpec(s, lambda i, nd=len(s): (0,) * nd) for s in vec_outs]
    out_shape = [jax.ShapeDtypeStruct((n_tiles * tm, w), dt) for (w, dt) in row_outs]
    out_shape += [jax.ShapeDtypeStruct(s, F32) for s in vec_outs]
    return pl.pallas_call(kern, name=name, grid=(n_tiles,), in_specs=in_specs, out_specs=out_specs,
                          out_shape=out_shape, compiler_params=_params())(*[a for (a, _, _, _) in row_ins], *vec_ins)


def _colsum(v):
    return jnp.sum(v, axis=0, keepdims=True)


def _rowmean(v):
    return jnp.mean(v, axis=-1, keepdims=True)


def _rms(h):
    r = lax.rsqrt(_rowmean(h * h) + EPS)
    return h * r, r


def _rms_bwd(n, r, gy):
    return r * (gy - n * _rowmean(gy * n))


def _ln_stats(v):
    xc = v - _rowmean(v)
    r = lax.rsqrt(_rowmean(xc * xc) + EPS)
    return xc * r, r


def _ln_bwd(xh, r, dxh):
    return r * (dxh - _rowmean(dxh) - xh * _rowmean(dxh * xh))


def _sigmoid(v):
    return 1.0 / (1.0 + jnp.exp(-v))


def _gelu(v):
    t = jnp.tanh(GELU_C * (v + GELU_A * (v * v * v)))
    return 0.5 * v * (1.0 + t), t


def _gelu_grad(v, t):
    return 0.5 * (1.0 + t) + 0.5 * v * (1.0 - t * t) * (GELU_C * (1.0 + 3.0 * GELU_A * (v * v)))


def _modnorm_fwd(name, h, g, sc, sh, tm):
    def body(ri, vi, ro, vo):
        n, _ = _rms(ri[0][...])
        ro[0][...] = ((n * vi[0][...]) * (1.0 + vi[1][...]) + vi[2][...]).astype(BF16)

    return _rowwise(name, body, h.shape[0] // tm, tm, [_ri(h)], [g, sc, sh], [(D_MODEL, BF16)], [])[0]


def _modnorm_bwd(name, h, dxs, dh_in, g, sc, tm):
    ndx = len(dxs)

    def body(ri, vi, ro, vo):
        dxm = ri[1][...].astype(F32)
        for r in ri[2:1 + ndx]:
            dxm = dxm + r[...].astype(F32)
        gv = vi[0][...]
        n, r = _rms(ri[0][...])
        vo[0][...] += _colsum(dxm)
        vo[1][...] += _colsum(dxm * (n * gv))
        dy = dxm * (1.0 + vi[1][...])
        vo[2][...] += _colsum(dy * n)
        if dh_in is not None:
            ro[0][...] = ri[1 + ndx][...] + _rms_bwd(n, r, dy * gv)

    row_ins = [_ri(h)] + list(dxs) + ([_ri(dh_in)] if dh_in is not None else [])
    row_outs = [(D_MODEL, F32)] if dh_in is not None else []
    outs = _rowwise(name, body, h.shape[0] // tm, tm, row_ins, [g, sc], row_outs, [(1, D_MODEL)] * 3)
    if dh_in is None:
        return (None, *outs)
    return tuple(outs)


def _resgate_fwd(name, h, y, gt, tm):
    def body(ri, vi, ro, vo):
        ro[0][...] = ri[0][...] + vi[0][...] * ri[1][...]

    return _rowwise(name, body, h.shape[0] // tm, tm, [_ri(h), _ri(y)], [gt], [(D_MODEL, F32)], [])[0]


def _resgate_bwd(name, dh, y, gt, tm):
    def body(ri, vi, ro, vo):
        d = ri[0][...]
        dy = d * vi[0][...]
        ro[0][...] = dy.astype(BF16)
        vo[0][...] += _colsum(d * ri[1][...])
        vo[1][...] += _colsum(dy)

    return _rowwise(name, body, dh.shape[0] // tm, tm, [_ri(dh), _ri(y)], [gt], [(D_MODEL, BF16)],
                    [(1, D_MODEL)] * 2)


def _final_loss(name, h, tgt, g, tm):
    def body(ri, vi, ro, vo):
        gv = vi[0][...]
        n, r = _rms(ri[0][...])
        e = n * gv - ri[1][...]
        vo[1][...] += _colsum(e * e) * (0.5 / D_MODEL)
        dout = e * (1.0 / D_MODEL)
        vo[0][...] += _colsum(dout * n)
        ro[0][...] = _rms_bwd(n, r, dout * gv)

    return _rowwise(name, body, h.shape[0] // tm, tm, [_ri(h), _ri(tgt)], [g], [(D_MODEL, F32)],
                    [(1, D_MODEL)] * 2)


def _head_mean(v, bd):
    hi = v.astype(BF16)
    lo = (v - hi.astype(F32)).astype(BF16)
    s = jnp.dot(hi, bd, preferred_element_type=F32) + jnp.dot(lo, bd, preferred_element_type=F32)
    return s * (1.0 / HEAD_DIM)


def _swap16(v):
    w = v.shape[1]
    lane = lax.broadcasted_iota(jnp.int32, v.shape, 1)
    return jnp.where((lane & 16) == 0, pltpu.roll(v, w - 16, 1), pltpu.roll(v, 16, 1))


def _rope(v, cos, sin):
    return v * cos + _swap16(v) * sin


def _rope_bwd(d, cos, sin):
    return d * cos + _swap16(d * sin)


def _qk_prep_fwd(name, p_qkv, cos, sin, gq, gk, bdq, bdk, tm):
    def body(ri, vi, ro, vo):
        pq = ri[0][...]
        pkv = ri[1][...]
        cosv = ri[2][...]
        sinv = ri[3][...]
        rq = lax.rsqrt(_head_mean(pq * pq, vi[2][...]) + EPS)
        ro[0][...] = (_rope(pq * rq * vi[0][...], cosv, sinv) * (HEAD_DIM ** -0.5)).astype(BF16)
        pk = pkv[:, :KV_W]
        rk = lax.rsqrt(_head_mean(pk * pk, vi[3][...]) + EPS)
        ro[1][...] = _rope(pk * rk * vi[1][...], cosv[:, :KV_W], sinv[:, :KV_W]).astype(BF16)
        ro[2][...] = pkv[:, KV_W:].astype(BF16)

    return _rowwise(name, body, p_qkv.shape[0] // tm, tm,
                    [_ri(p_qkv, ATTN_W, 0), _ri(p_qkv, 2 * KV_W, 2), _ri(cos), _ri(sin)], [gq, gk, bdq, bdk],
                    [(ATTN_W, BF16), (KV_W, BF16), (KV_W, BF16)], [])


def _qk_prep_bwd(name, p_qkv, cos, sin, dq, dk, dv, gq, gk, bdq, bdk, tm):
    def norm_rope_bwd(p, gain, bd, cosv, sinv, dout):
        r = lax.rsqrt(_head_mean(p * p, bd) + EPS)
        n = p * r
        dqn = _rope_bwd(dout, cosv, sinv)
        gy = dqn * gain
        return r * (gy - n * _head_mean(gy * n, bd)), _colsum(dqn * n)

    def body(ri, vi, ro, vo):
        pkv = ri[1][...]
        cosv = ri[2][...]
        sinv = ri[3][...]
        dxq, dgq = norm_rope_bwd(ri[0][...], vi[0][...], vi[2][...], cosv, sinv, ri[4][...] * (HEAD_DIM ** -0.5))
        dxk, dgk = norm_rope_bwd(pkv[:, :KV_W], vi[1][...], vi[3][...], cosv[:, :KV_W], sinv[:, :KV_W], ri[5][...])
        ro[0][:, :ATTN_W] = dxq.astype(BF16)
        ro[0][:, ATTN_W:ATTN_W + KV_W] = dxk.astype(BF16)
        ro[0][:, ATTN_W + KV_W:] = ri[6][...].astype(BF16)
        vo[0][...] += dgq
        vo[1][...] += dgk

    return _rowwise(name, body, p_qkv.shape[0] // tm, tm,
                    [_ri(p_qkv, ATTN_W, 0), _ri(p_qkv, 2 * KV_W, 2), _ri(cos), _ri(sin), _ri(dq), _ri(dk), _ri(dv)],
                    [gq, gk, bdq, bdk], [(QKV_W, BF16)], [(1, ATTN_W), (1, KV_W)])


def _sg_fwd(name, p_sg, w_sp, bsp, tm):
    def body(ri, vi, ro, vo):
        p = ri[0][...]
        gu, _ = _gelu(p[:, :SG_W])
        gv, _ = _gelu(p[:, SG_W:])
        for g in range(N_SG):
            sl = slice(g * CHUNK, (g + 1) * CHUNK)
            vn, _ = _ln_stats(gv[:, sl])
            vnb = vn.astype(BF16)
            for m in range(tm // CHUNK):
                rs = slice(m * CHUNK, (m + 1) * CHUNK)
                mixed = jnp.dot(vi[0][g], vnb[rs], preferred_element_type=F32) + vi[1][:, sl]
                ro[0][rs, sl] = (gu[rs, sl] * mixed).astype(BF16)

    return _rowwise(name, body, p_sg.shape[0] // tm, tm, [_ri(p_sg)], [w_sp, bsp], [(SG_W, BF16)], [])[0]


def _sg_bwd(name, p_sg, dcat, w_sp, w_sp_t, bsp, tm):
    def body(ri, vi, ro, vo):
        p = ri[0][...]
        dsg = ri[1][...]
        su = p[:, :SG_W]
        sv = p[:, SG_W:]
        gu, tu = _gelu(su)
        gv, tv = _gelu(sv)
        for g in range(N_SG):
            sl = slice(g * CHUNK, (g + 1) * CHUNK)
            vn, r = _ln_stats(gv[:, sl])
            vnb = vn.astype(BF16)
            for m in range(tm // CHUNK):
                rs = slice(m * CHUNK, (m + 1) * CHUNK)
                mixed = jnp.dot(vi[0][g], vnb[rs], preferred_element_type=F32) + vi[2][:, sl]
                d_o = dsg[rs, sl]
                dm = d_o * gu[rs, sl]
                vo[1][:, sl] += dm
                dmb = dm.astype(BF16)
                vo[0][g] += lax.dot_general(dmb, vnb[rs], (((1,), (1,)), ((), ())), preferred_element_type=F32)
                dvn = jnp.dot(vi[1][g], dmb, preferred_element_type=F32)
                dgv = _ln_bwd(vn[rs], r[rs], dvn)
                ro[0][rs, sl] = (d_o * mixed * _gelu_grad(su[rs, sl], tu[rs, sl])).astype(BF16)
                ro[0][rs, SG_W + g * CHUNK:SG_W + (g + 1) * CHUNK] = (
                    dgv * _gelu_grad(sv[rs, sl], tv[rs, sl])).astype(BF16)

    return _rowwise(name, body, p_sg.shape[0] // tm, tm, [_ri(p_sg), _ri(dcat, SG_W, 1)], [w_sp, w_sp_t, bsp],
                    [(2 * SG_W, BF16)], [(N_SG, CHUNK, CHUNK), (CHUNK, SG_W)])


def _glu_fwd(name, ag, b, tm):
    def body(ri, vi, ro, vo):
        v = ri[0][...] + vi[0][...]
        ro[0][...] = v[:, :D_MODEL] * _sigmoid(v[:, D_MODEL:])

    return _rowwise(name, body, ag.shape[0] // tm, tm, [_ri(ag)], [b], [(D_MODEL, F32)], [])[0]


def _glu_bwd(name, ag, dhg, b, tm):
    def body(ri, vi, ro, vo):
        v = ri[0][...] + vi[0][...]
        d = ri[1][...]
        a = v[:, :D_MODEL]
        sg = _sigmoid(v[:, D_MODEL:])
        da = d * sg
        dgate = d * a * sg * (1.0 - sg)
        ro[0][:, :D_MODEL] = da.astype(BF16)
        ro[0][:, D_MODEL:] = dgate.astype(BF16)
        vo[0][:, :D_MODEL] += _colsum(da)
        vo[0][:, D_MODEL:] += _colsum(dgate)

    return _rowwise(name, body, ag.shape[0] // tm, tm, [_ri(ag), _ri(dhg)], [b], [(2 * D_MODEL, BF16)],
                    [(1, 2 * D_MODEL)])


def _lnsilu_fwd(name, hc, g, b, tm):
    def body(ri, vi, ro, vo):
        xh, _ = _ln_stats(ri[0][...])
        ln = xh * vi[0][...] + vi[1][...]
        ro[0][...] = (ln * _sigmoid(ln)).astype(BF16)

    return _rowwise(name, body, hc.shape[0] // tm, tm, [_ri(hc)], [g, b], [(D_MODEL, BF16)], [])[0]


def _lnsilu_bwd(name, hc, ds, g, b, tm):
    def body(ri, vi, ro, vo):
        gv = vi[0][...]
        xh, r = _ln_stats(ri[0][...])
        ln = xh * gv + vi[1][...]
        sg = _sigmoid(ln)
        dln = ri[1][...] * (sg * (1.0 + ln * (1.0 - sg)))
        vo[0][...] += _colsum(dln * xh)
        vo[1][...] += _colsum(dln)
        ro[0][...] = _ln_bwd(xh, r, dln * gv)

    return _rowwise(name, body, hc.shape[0] // tm, tm, [_ri(hc), _ri(ds)], [g, b], [(D_MODEL, F32)],
                    [(1, D_MODEL)] * 2)


def _conv_blocks(seq):
    cb = 128
    tt = 128 if seq % 128 == 0 else seq
    return cb, tt


def _conv_taps(win, tt):
    n = win.shape[0]
    for s in range(8):
        ws = win if s == 0 else pltpu.roll(win, n - s, 0)
        for q in range(4):
            j = 8 * q + s - 1
            if 0 <= j < CONV_W:
                yield j, ws[8 * q:8 * q + tt, :]


def _fill_padded(pad_ref, x_ref, seq):
    zeros = jnp.zeros((CONV_HALO, pad_ref.shape[1]), F32)
    pad_ref[0:CONV_HALO, :] = zeros
    pad_ref[seq + CONV_HALO:seq + 2 * CONV_HALO, :] = zeros
    pad_ref[CONV_HALO:seq + CONV_HALO, :] = x_ref[...]


def _dwconv(name, xin, w, b):
    seq = xin.shape[0]
    cb, tt = _conv_blocks(seq)

    def kern(x_ref, w_ref, b_ref, o_ref, pad_ref):
        _fill_padded(pad_ref, x_ref, seq)
        wv = w_ref[...]
        bv = b_ref[...]

        def step(t, carry):
            base = pl.multiple_of(t * tt, tt)
            win = pad_ref[pl.ds(base, tt + 2 * CONV_HALO), :]
            acc = jnp.zeros((tt, cb), F32) + bv
            for j, rows in _conv_taps(win, tt):
                acc = acc + wv[j:j + 1, :] * rows
            o_ref[pl.ds(base, tt), :] = acc
            return carry

        lax.fori_loop(0, seq // tt, step, 0)

    return pl.pallas_call(
        kern, name=name, grid=(D_MODEL // cb,),
        in_specs=[pl.BlockSpec((seq, cb), lambda j: (0, j)), pl.BlockSpec((CONV_W, cb), lambda j: (0, j)),
                  pl.BlockSpec((1, cb), lambda j: (0, j))],
        out_specs=pl.BlockSpec((seq, cb), lambda j: (0, j)),
        out_shape=jax.ShapeDtypeStruct((seq, D_MODEL), F32),
        scratch_shapes=[pltpu.VMEM((seq + 2 * CONV_HALO, cb), F32)],
        compiler_params=_params())(xin, w, b)


def _dwconv_wgrad(name, xin, dout):
    seq = xin.shape[0]
    cb, tt = _conv_blocks(seq)

    def kern(x_ref, d_ref, dw_ref, db_ref, pad_ref):
        _fill_padded(pad_ref, x_ref, seq)
        dw_ref[...] = jnp.zeros(dw_ref.shape, F32)
        db_ref[...] = jnp.zeros(db_ref.shape, F32)

        def step(t, carry):
            base = pl.multiple_of(t * tt, tt)
            win = pad_ref[pl.ds(base, tt + 2 * CONV_HALO), :]
            d = d_ref[pl.ds(base, tt), :]
            db_ref[...] += _colsum(d)
            for j, rows in _conv_taps(win, tt):
                dw_ref[j:j + 1, :] += _colsum(d * rows)
            return carry

        lax.fori_loop(0, seq // tt, step, 0)

    return pl.pallas_call(
        kern, name=name, grid=(D_MODEL // cb,),
        in_specs=[pl.BlockSpec((seq, cb), lambda j: (0, j)), pl.BlockSpec((seq, cb), lambda j: (0, j))],
        out_specs=[pl.BlockSpec((CONV_W, cb), lambda j: (0, j)), pl.BlockSpec((1, cb), lambda j: (0, j))],
        out_shape=[jax.ShapeDtypeStruct((CONV_W, D_MODEL), F32), jax.ShapeDtypeStruct((1, D_MODEL), F32)],
        scratch_shapes=[pltpu.VMEM((seq + 2 * CONV_HALO, cb), F32)],
        compiler_params=_params())(xin, dout)


def _mm(name, a, b, out_sds, grid, a_spec, b_spec, o_spec, contract, k_axis=None, bias=None):
    dn = (contract, ((), ()))

    def kern(*refs):
        if bias is None:
            a_ref, b_ref, o_ref = refs
        else:
            a_ref, b_ref, bias_ref, o_ref = refs
        p = lax.dot_general(a_ref[...].astype(BF16), b_ref[...].astype(BF16), dn, preferred_element_type=F32)
        if bias is not None:
            p = p + bias_ref[...]
        if k_axis is None:
            o_ref[...] = p.astype(o_ref.dtype)
        else:
            k = pl.program_id(k_axis)

            @pl.when(k == 0)
            def _():
                o_ref[...] = p

            @pl.when(k > 0)
            def _():
                o_ref[...] += p

    in_specs = [a_spec, b_spec]
    args = [a, b]
    if bias is not None:
        in_specs.append(pl.BlockSpec(bias.shape, lambda *_: (0,) * bias.ndim))
        args.append(bias)
    return pl.pallas_call(kern, name=name, grid=grid, in_specs=in_specs, out_specs=o_spec, out_shape=out_sds,
                          compiler_params=_params())(*args)


def _mm_rows(name, a, b, tm, trans_b=False, bias=None):
    m, k = a.shape
    n = b.shape[0] if trans_b else b.shape[1]
    contract = ((1,), (1,)) if trans_b else ((1,), (0,))
    return _mm(name, a, b, jax.ShapeDtypeStruct((m, n), F32), (m // tm,),
               pl.BlockSpec((tm, k), lambda i: (i, 0)), pl.BlockSpec(b.shape, lambda i: (0, 0)),
               pl.BlockSpec((tm, n), lambda i: (i, 0)), contract, bias=bias)


def _mm_blk(name, a, bblk, tm):
    m, k = a.shape
    nb, _, nblk = bblk.shape
    return _mm(name, a, bblk, jax.ShapeDtypeStruct((m, nb * nblk), F32), (nb, m // tm),
               pl.BlockSpec((tm, k), lambda j, i: (i, 0)), pl.BlockSpec((None, k, nblk), lambda j, i: (j, 0, 0)),
               pl.BlockSpec((tm, nblk), lambda j, i: (i, j)), ((1,), (0,)))


def _mm_blk_t(name, a, bblk, tm):
    m = a.shape[0]
    nb, k, nblk = bblk.shape
    return _mm(name, a, bblk, jax.ShapeDtypeStruct((m, k), F32), (m // tm, nb),
               pl.BlockSpec((tm, nblk), lambda i, j: (i, j)), pl.BlockSpec((None, k, nblk), lambda i, j: (j, 0, 0)),
               pl.BlockSpec((tm, k), lambda i, j: (i, 0)), ((1,), (1,)), k_axis=1)


def _ffn_in_swiglu(name, xf, wblk, tm):
    m, k = xf.shape
    nblk = wblk.shape[2]

    def kern(x_ref, wg_ref, wu_ref, gu_ref, a_ref):
        xv = x_ref[...]
        g = jnp.dot(xv, wg_ref[...], preferred_element_type=F32)
        u = jnp.dot(xv, wu_ref[...], preferred_element_type=F32)
        gu_ref[0] = g
        gu_ref[1] = u
        a_ref[...] = (g * _sigmoid(g) * u).astype(BF16)

    return pl.pallas_call(
        kern, name=name, grid=(2, m // tm),
        in_specs=[pl.BlockSpec((tm, k), lambda j, i: (i, 0)), pl.BlockSpec((None, k, nblk), lambda j, i: (j, 0, 0)),
                  pl.BlockSpec((None, k, nblk), lambda j, i: (j + 2, 0, 0))],
        out_specs=[pl.BlockSpec((2, tm, nblk), lambda j, i: (0, i, j)), pl.BlockSpec((tm, nblk), lambda j, i: (i, j))],
        out_shape=[jax.ShapeDtypeStruct((2, m, 2 * nblk), F32), jax.ShapeDtypeStruct((m, 2 * nblk), BF16)],
        compiler_params=_params())(xf, wblk, wblk)


def _ffn_out_dx_swiglu(name, dy, wo, gu, tm):
    m, k = dy.shape
    nblk = gu.shape[2] // 2

    def kern(dy_ref, w_ref, gu_ref, o_ref):
        da = lax.dot_general(dy_ref[...], w_ref[...], (((1,), (1,)), ((), ())), preferred_element_type=F32)
        g = gu_ref[0]
        sg = _sigmoid(g)
        o_ref[0] = (da * gu_ref[1] * (sg * (1.0 + g * (1.0 - sg)))).astype(BF16)
        o_ref[1] = (da * (g * sg)).astype(BF16)

    return pl.pallas_call(
        kern, name=name, grid=(2, m // tm),
        in_specs=[pl.BlockSpec((tm, k), lambda j, i: (i, 0)), pl.BlockSpec((nblk, k), lambda j, i: (j, 0)),
                  pl.BlockSpec((2, tm, nblk), lambda j, i: (0, i, j))],
        out_specs=pl.BlockSpec((2, tm, nblk), lambda j, i: (0, i, j)),
        out_shape=jax.ShapeDtypeStruct(gu.shape, BF16), compiler_params=_params())(dy, wo, gu)


def _mm_halves_blk_t(name, a2, bblk, tm):
    m = a2.shape[1]
    nb, k, nblk = bblk.shape
    return _mm(name, a2, bblk, jax.ShapeDtypeStruct((m, k), F32), (m // tm, nb),
               pl.BlockSpec((None, tm, nblk), lambda i, j: (j // 2, i, j % 2)),
               pl.BlockSpec((None, k, nblk), lambda i, j: (j, 0, 0)),
               pl.BlockSpec((tm, k), lambda i, j: (i, 0)), ((1,), (1,)), k_axis=1)


def _mm_halves_wgrad_blk(name, a, b2, tk):
    t, k = a.shape
    nblk = b2.shape[2] // 2
    return _mm(name, a, b2, jax.ShapeDtypeStruct((N_CHIP, k, nblk), F32), (N_CHIP, t // tk),
               pl.BlockSpec((tk, k), lambda j, i: (i, 0)),
               pl.BlockSpec((None, tk, nblk), lambda j, i: (j // 2, i, j % 2)),
               pl.BlockSpec((None, k, nblk), lambda j, i: (j, 0, 0)), ((0,), (0,)), k_axis=1)


def _mm_wgrad(name, a, b, tk):
    t, k = a.shape
    n = b.shape[1]
    tn = n if k * n * 4 <= 4 * 1024 * 1024 else 512
    return _mm(name, a, b, jax.ShapeDtypeStruct((k, n), F32), (n // tn, t // tk),
               pl.BlockSpec((tk, k), lambda j, i: (i, 0)), pl.BlockSpec((tk, tn), lambda j, i: (i, j)),
               pl.BlockSpec((k, tn), lambda j, i: (0, j)), ((0,), (0,)), k_axis=1)


def _mm_wgrad_blk(name, a, b, nb, tk):
    t, k = a.shape
    nblk = b.shape[1] // nb
    return _mm(name, a, b, jax.ShapeDtypeStruct((nb, k, nblk), F32), (nb, t // tk),
               pl.BlockSpec((tk, k), lambda j, i: (i, 0)), pl.BlockSpec((tk, nblk), lambda j, i: (i, j)),
               pl.BlockSpec((None, k, nblk), lambda j, i: (j, 0, 0)), ((0,), (0,)), k_axis=1)


def _attn_fwd(name, q, k, v, tq, gather=()):
    nh, s, hd = q.shape
    r = k.shape[1]
    ng = len(gather)
    n_i = s // tq
    last = nh * n_i - 1
    forward_at = (7 * last) // 8

    def kern(q_ref, k_ref, v_ref, *rest):
        o_ref, lse_ref = rest[ng], rest[ng + 1]
        if ng:
            start, forward, finish = _gather_phases(rest[:ng], rest[ng + 2:2 * ng + 2], *rest[2 * ng + 2:])
            step = pl.program_id(0) * n_i + pl.program_id(1)
            pl.when(step == 0)(start)
            pl.when(step == forward_at)(forward)
        sc = lax.dot_general(q_ref[...], k_ref[...], (((1,), (1,)), ((), ())), preferred_element_type=F32)
        m = jnp.max(sc, axis=-1, keepdims=True)
        p = jnp.exp(sc - m)
        l = jnp.sum(p, axis=-1, keepdims=True)
        o = jnp.dot(p.astype(BF16), v_ref[...], preferred_element_type=F32)
        o_ref[...] = o / l
        lse_ref[...] = m + jnp.log(l)
        if ng:
            pl.when(step == last)(finish)

    return pl.pallas_call(
        kern, name=name, grid=(nh, n_i),
        in_specs=[pl.BlockSpec((None, tq, hd), lambda h, i: (h, i, 0)),
                  pl.BlockSpec((None, r, hd), lambda h, i: (h // Q_GROUP, 0, 0)),
                  pl.BlockSpec((None, r, hd), lambda h, i: (h // Q_GROUP, 0, 0))] + [_ANY] * ng,
        out_specs=[pl.BlockSpec((None, tq, hd), lambda h, i: (h, i, 0)),
                   pl.BlockSpec((None, tq, 1), lambda h, i: (h, i, 0))] + [_ANY] * ng,
        out_shape=[jax.ShapeDtypeStruct((nh, s, hd), F32), jax.ShapeDtypeStruct((nh, s, 1), F32)]
        + [jax.ShapeDtypeStruct((N_DEV,) + a.shape, a.dtype) for a in gather],
        scratch_shapes=_gather_scratch(ng) if ng else [],
        compiler_params=_params())(q, k, v, *gather)


def _attn_bwd(name, q, k, v, o, lse, do, tq, exchange=()):
    nh, s, hd = q.shape
    nkv, r, _ = k.shape
    ne = len(exchange)
    n_i = s // tq
    last = nh * n_i - 1

    def kern(q_ref, k_ref, v_ref, o_ref, lse_ref, do_ref, *rest):
        dq_ref, dk_ref, dv_ref = rest[ne:ne + 3]
        if ne:
            start, finish = _exchange_phases(rest[:ne], rest[ne + 3:2 * ne + 3], *rest[2 * ne + 3:])
            step = (pl.program_id(0) * Q_GROUP + pl.program_id(1)) * n_i + pl.program_id(2)
            pl.when(step == 0)(start)

        @pl.when((pl.program_id(1) == 0) & (pl.program_id(2) == 0))
        def _():
            dk_ref[...] = jnp.zeros(dk_ref.shape, F32)
            dv_ref[...] = jnp.zeros(dv_ref.shape, F32)

        qv = q_ref[...]
        kv = k_ref[...]
        dov = do_ref[...]
        delta = jnp.sum(dov * o_ref[...], axis=-1, keepdims=True)
        sc = lax.dot_general(qv, kv, (((1,), (1,)), ((), ())), preferred_element_type=F32)
        p = jnp.exp(sc - lse_ref[...])
        dob = dov.astype(BF16)
        dp = lax.dot_general(dob, v_ref[...], (((1,), (1,)), ((), ())), preferred_element_type=F32)
        dsb = (p * (dp - delta)).astype(BF16)
        dq_ref[...] = jnp.dot(dsb, kv, preferred_element_type=F32)
        dk_ref[...] += lax.dot_general(dsb, qv, (((0,), (0,)), ((), ())), preferred_element_type=F32)
        dv_ref[...] += lax.dot_general(p.astype(BF16), dob, (((0,), (0,)), ((), ())), preferred_element_type=F32)
        if ne:
            pl.when(step == last)(finish)

    qmap = lambda j, g, i: (j * Q_GROUP + g, i, 0)
    kvmap = lambda j, g, i: (j, 0, 0)
    return pl.pallas_call(
        kern, name=name, grid=(nkv, Q_GROUP, n_i),
        in_specs=[pl.BlockSpec((None, tq, hd), qmap), pl.BlockSpec((None, r, hd), kvmap),
                  pl.BlockSpec((None, r, hd), kvmap), pl.BlockSpec((None, tq, hd), qmap),
                  pl.BlockSpec((None, tq, 1), qmap), pl.BlockSpec((None, tq, hd), qmap)] + [_ANY] * ne,
        out_specs=[pl.BlockSpec((None, tq, hd), qmap), pl.BlockSpec((None, r, hd), kvmap),
                   pl.BlockSpec((None, r, hd), kvmap)] + [_ANY] * ne,
        out_shape=[jax.ShapeDtypeStruct((nh, s, hd), F32), jax.ShapeDtypeStruct((nkv, r, hd), F32),
                   jax.ShapeDtypeStruct((nkv, r, hd), F32)]
        + [jax.ShapeDtypeStruct(a.shape, a.dtype) for a in exchange],
        scratch_shapes=_exchange_scratch(ne) if ne else [],
        compiler_params=_params())(q, k, v, o, lse, do, *exchange)


def _mod_fwd(name, c16, wm, bm):
    n = wm.shape[2]

    def kern(c_ref, w_ref, b_ref, o_ref):
        cv = c_ref[...]
        a = (cv * _sigmoid(cv)).astype(BF16)
        o_ref[...] = jnp.dot(a, w_ref[...], preferred_element_type=F32) + b_ref[...]

    return pl.pallas_call(
        kern, name=name, grid=(2,),
        in_specs=[pl.BlockSpec(c16.shape, lambda l: (0, 0)), pl.BlockSpec((None, D_MODEL, n), lambda l: (l, 0, 0)),
                  pl.BlockSpec((None, 1, n), lambda l: (l, 0, 0))],
        out_specs=pl.BlockSpec((None, 16, n), lambda l: (l, 0, 0)),
        out_shape=jax.ShapeDtypeStruct((2, 16, n), F32), compiler_params=_params())(c16, wm, bm)


def _mod_wgrad(name, c16, dm):
    n = dm.shape[2]
    tn = 512

    def kern(c_ref, d_ref, o_ref):
        cv = c_ref[...]
        a = cv * _sigmoid(cv)
        o_ref[...] = lax.dot_general(a, d_ref[...], (((0,), (0,)), ((), ())), preferred_element_type=F32,
                                     precision=lax.Precision.HIGHEST)

    return pl.pallas_call(
        kern, name=name, grid=(2, n // tn),
        in_specs=[pl.BlockSpec(c16.shape, lambda l, j: (0, 0)), pl.BlockSpec((None, 16, tn), lambda l, j: (l, 0, j))],
        out_specs=pl.BlockSpec((None, D_MODEL, tn), lambda l, j: (l, 0, j)),
        out_shape=jax.ShapeDtypeStruct((2, D_MODEL, n), F32), compiler_params=_params())(c16, dm)


def _cctx_grad(name, parts, c_ctx):
    def kern(p_ref, c_ref, o_ref):
        d = p_ref[0, 0:1, :] + p_ref[2, 0:1, :] + p_ref[4, 0:1, :] + p_ref[6, 0:1, :]
        cv = c_ref[...]
        sg = _sigmoid(cv)
        o_ref[...] = d * (sg * (1.0 + cv * (1.0 - sg)))

    return pl.pallas_call(kern, name=name, out_shape=jax.ShapeDtypeStruct((1, D_MODEL), F32),
                          compiler_params=_params())(parts, c_ctx)


def _sum_slots(name, g, tr):
    n, rows, cols = g.shape

    def kern(g_ref, o_ref):
        acc = g_ref[0].astype(F32)
        for i in range(1, n):
            acc = acc + g_ref[i].astype(F32)
        o_ref[...] = acc

    return pl.pallas_call(kern, name=name, grid=(rows // tr,),
                          in_specs=[pl.BlockSpec((n, tr, cols), lambda i: (0, i, 0))],
                          out_specs=pl.BlockSpec((tr, cols), lambda i: (i, 0)),
                          out_shape=jax.ShapeDtypeStruct((rows, cols), F32), compiler_params=_params())(g)


def _sum_into(name, g, cvec, n_layers, layer, prev=None):
    n, r, cc = g.shape
    tr = _row_tile(r, cc, 512 * 1024)

    def kern(c_ref, g_ref, *rest):
        acc = g_ref[0].astype(F32)
        for i in range(1, n):
            acc = acc + g_ref[i].astype(F32)
        rest[-1][...] = acc

    in_specs = [pl.BlockSpec((n, tr, cc), lambda i, c_ref: (0, i, 0))]
    args = [cvec, g]
    aliases = {}
    if prev is not None:
        in_specs.append(_ANY)
        args.append(prev)
        aliases = {2: 0}
    grid_spec = pltpu.PrefetchScalarGridSpec(
        num_scalar_prefetch=1, grid=(r // tr,), in_specs=in_specs,
        out_specs=pl.BlockSpec((None, None, tr, cc), lambda i, c_ref: (layer, c_ref[0], i, 0)))
    return pl.pallas_call(kern, name=name, grid_spec=grid_spec,
                          out_shape=jax.ShapeDtypeStruct((n_layers, 2, r, cc), F32),
                          input_output_aliases=aliases, compiler_params=_params())(*args)


def _row_tile(rows, cols, max_bytes=1024 * 1024):
    if rows * cols * 4 <= 2 * max_bytes:
        return rows
    best = None
    for t in range(16, rows + 1, 16):
        if rows % t == 0 and t * cols * 4 <= max_bytes:
            best = t
    assert best is not None, (rows, cols)
    return best


def _adamw(name, w, g, m, v):
    shape = w.shape
    cols = shape[-1]
    rows = w.size // cols
    tr = _row_tile(rows, cols)

    def kern(w_ref, g_ref, m_ref, v_ref, d_ref, nm_ref, nv_ref):
        gv = g_ref[...]
        m2 = ADAM_B1 * m_ref[...] + (1.0 - ADAM_B1) * gv
        v2 = ADAM_B2 * v_ref[...] + (1.0 - ADAM_B2) * (gv * gv)
        m_hat = m2 / (1.0 - ADAM_B1 ** ADAM_STEP)
        v_hat = v2 / (1.0 - ADAM_B2 ** ADAM_STEP)
        d_ref[...] = -ADAM_LR * (m_hat / (jnp.sqrt(v_hat) + ADAM_EPS) + ADAM_WD * w_ref[...])
        nm_ref[...] = m2
        nv_ref[...] = v2

    spec = pl.BlockSpec((tr, cols), lambda i: (i, 0))
    sds = jax.ShapeDtypeStruct((rows, cols), F32)
    outs = pl.pallas_call(kern, name=name, grid=(rows // tr,), in_specs=[spec] * 4, out_specs=[spec] * 3,
                          out_shape=[sds] * 3, compiler_params=_params())(
        w.reshape(rows, cols), g.reshape(rows, cols), m.reshape(rows, cols), v.reshape(rows, cols))
    return tuple(o.reshape(shape) for o in outs)


_ANY = pl.BlockSpec(memory_space=pl.ANY)


def _mesh_pos():
    return lax.axis_index("x"), lax.axis_index("y"), lax.axis_index("c")


def _gather_phases(srcs, outs, send_sems, recv_sems, local_sems):
    n = len(srcs)
    x, y, c = _mesh_pos()
    me = (x, y, c)
    sibling = (x, y, 1 - c)
    chips = [(1 - x, y), (x, 1 - y), (1 - x, 1 - y)]

    def slot(px, py, pc):
        return 4 * px + 2 * py + pc

    def copy(t, k, s, to, src=None):
        dst = outs[t].at[s]
        return pltpu.make_async_remote_copy(src_ref=dst if src is None else src, dst_ref=dst,
                                            send_sem=send_sems.at[t, k], recv_sem=recv_sems.at[t, k],
                                            device_id=to, device_id_type=MESH)

    def mine(t):
        return pltpu.make_async_copy(srcs[t], outs[t].at[slot(*me)], local_sems.at[t])

    def first(t):
        return [copy(t, 0, slot(*me), sibling, src=srcs[t])] + [
            copy(t, 1 + j, slot(*me), (px, py, c), src=srcs[t]) for j, (px, py) in enumerate(chips)]

    def passed(t, j):
        px, py = chips[j]
        return copy(t, 4 + j, slot(px, py, c), sibling)

    def start():
        for t in range(n):
            mine(t).start()
        for t in range(n):
            for cp in first(t):
                cp.start()

    def forward():
        for j, (px, py) in enumerate(chips):
            for t in range(n):
                copy(t, 1 + j, slot(px, py, c), me).wait_recv()
                passed(t, j).start()

    def finish():
        for t in range(n):
            copy(t, 0, slot(x, y, 1 - c), me).wait_recv()
            for j, (px, py) in enumerate(chips):
                copy(t, 4 + j, slot(px, py, 1 - c), me).wait_recv()
        for t in range(n):
            for cp in first(t) + [passed(t, j) for j in range(len(chips))]:
                cp.wait_send()
            mine(t).wait()

    return start, forward, finish


def _gather_scratch(n):
    return [pltpu.SemaphoreType.DMA((n, 7)), pltpu.SemaphoreType.DMA((n, 7)), pltpu.SemaphoreType.DMA((n,))]


def _allgather(name, items):
    n = len(items)

    def body(*refs):
        start, forward, finish = _gather_phases(refs[:n], refs[n:2 * n], *refs[2 * n:])
        start()
        forward()
        finish()

    return pl.pallas_call(
        body, name=name, in_specs=[_ANY] * n, out_specs=[_ANY] * n,
        out_shape=[jax.ShapeDtypeStruct((N_DEV,) + a.shape, a.dtype) for a in items],
        scratch_shapes=_gather_scratch(n), compiler_params=_params())(*items)


def _rs_pair_send(name, gs):
    n = len(gs)

    def body(*refs):
        srcs = refs[:n]
        bufs = refs[n:2 * n]
        send_sems, recv_sems = refs[2 * n:]
        x, y, c = _mesh_pos()
        copies = []
        for t in range(n):
            for k in range(N_CHIP):
                copies.append(pltpu.make_async_remote_copy(
                    src_ref=srcs[t].at[k, 1 - c], dst_ref=bufs[t].at[k], send_sem=send_sems.at[t, k],
                    recv_sem=recv_sems.at[t, k], device_id=(x, y, 1 - c), device_id_type=MESH))
        for cp in copies:
            cp.start()
        for cp in copies:
            cp.wait()

    return pl.pallas_call(
        body, name=name, in_specs=[_ANY] * n, out_specs=[_ANY] * n,
        out_shape=[jax.ShapeDtypeStruct((N_CHIP,) + g.shape[2:], g.dtype) for g in gs],
        scratch_shapes=[pltpu.SemaphoreType.DMA((n, N_CHIP)), pltpu.SemaphoreType.DMA((n, N_CHIP))],
        compiler_params=_params())(*gs)


def _rs_pair_add(name, g, buf, cvec):
    _, _, r, cc = g.shape
    tr = _row_tile(r, cc)

    def kern(c_ref, g_ref, b_ref, o_ref):
        o_ref[...] = (g_ref[...] + b_ref[...]).astype(BF16)

    grid_spec = pltpu.PrefetchScalarGridSpec(
        num_scalar_prefetch=1, grid=(N_CHIP, r // tr),
        in_specs=[pl.BlockSpec((None, None, tr, cc), lambda k, i, c_ref: (k, c_ref[0], i, 0)),
                  pl.BlockSpec((None, tr, cc), lambda k, i, c_ref: (k, i, 0))],
        out_specs=pl.BlockSpec((None, tr, cc), lambda k, i, c_ref: (k, i, 0)))
    return pl.pallas_call(kern, name=name, grid_spec=grid_spec,
                          out_shape=jax.ShapeDtypeStruct((N_CHIP, r, cc), BF16),
                          compiler_params=_params())(cvec, g, buf)


def _exchange_phases(srcs, bufs, send_sems, recv_sems, local_sems):
    n = len(srcs)
    x, y, c = _mesh_pos()
    kme = 2 * x + y
    chips = [(1 - x, y), (x, 1 - y), (1 - x, 1 - y)]

    def copies():
        local = [pltpu.make_async_copy(srcs[t].at[kme], bufs[t].at[kme], local_sems.at[t]) for t in range(n)]
        remote = []
        for t in range(n):
            for j, (px, py) in enumerate(chips):
                remote.append(pltpu.make_async_remote_copy(
                    src_ref=srcs[t].at[2 * px + py], dst_ref=bufs[t].at[kme], send_sem=send_sems.at[t, j],
                    recv_sem=recv_sems.at[t, j], device_id=(px, py, c), device_id_type=MESH))
        return local, remote

    def start():
        local, remote = copies()
        for cp in local + remote:
            cp.start()

    def finish():
        local, remote = copies()
        for cp in remote + local:
            cp.wait()

    return start, finish


def _exchange_scratch(n):
    return [pltpu.SemaphoreType.DMA((n, 3)), pltpu.SemaphoreType.DMA((n, 3)), pltpu.SemaphoreType.DMA((n,))]


def _rs_chip_exchange(name, ss):
    n = len(ss)

    def body(*refs):
        start, finish = _exchange_phases(refs[:n], refs[n:2 * n], *refs[2 * n:])
        start()
        finish()

    return pl.pallas_call(
        body, name=name, in_specs=[_ANY] * n, out_specs=[_ANY] * n,
        out_shape=[jax.ShapeDtypeStruct(s.shape, s.dtype) for s in ss],
        scratch_shapes=_exchange_scratch(n), compiler_params=_params())(*ss)


def _rs_pair_share(name, dsts):
    nd = len(dsts)

    def body(*refs):
        ins = refs[:nd]
        outs = refs[nd:2 * nd]
        send_sems, recv_sems = refs[2 * nd:]
        x, y, c = _mesh_pos()
        copies = []
        for d in range(nd):
            for l in range(dsts[d].shape[0]):
                copies.append(pltpu.make_async_remote_copy(
                    src_ref=ins[d].at[l, c], dst_ref=outs[d].at[l, c], send_sem=send_sems.at[d, l],
                    recv_sem=recv_sems.at[d, l], device_id=(x, y, 1 - c), device_id_type=MESH))
        for cp in copies:
            cp.start()
        for cp in copies:
            cp.wait()

    return pl.pallas_call(
        body, name=name, in_specs=[_ANY] * nd, out_specs=[_ANY] * nd,
        out_shape=[jax.ShapeDtypeStruct(a.shape, a.dtype) for a in dsts],
        input_output_aliases={d: d for d in range(nd)},
        scratch_shapes=[pltpu.SemaphoreType.DMA((nd, 2)), pltpu.SemaphoreType.DMA((nd, 2))],
        compiler_params=_params())(*dsts)


def _rope_tables(seq, n_ctx):
    t = jnp.arange(seq)
    row = (t // GRID_W).astype(F32)
    col = (t % GRID_W).astype(F32)
    inv = ROPE_THETA ** (-jnp.arange(0, HEAD_DIM // 2, 2, dtype=F32) / (HEAD_DIM // 2))
    ang_r = row[:, None] * inv[None, :]
    ang_c = col[:, None] * inv[None, :]
    cos = jnp.concatenate([jnp.cos(ang_r)] * 2 + [jnp.cos(ang_c)] * 2, axis=1)
    sin = jnp.concatenate([-jnp.sin(ang_r), jnp.sin(ang_r), -jnp.sin(ang_c), jnp.sin(ang_c)], axis=1)
    cos = jnp.concatenate([jnp.ones((n_ctx, HEAD_DIM), F32), cos], axis=0)
    sin = jnp.concatenate([jnp.zeros((n_ctx, HEAD_DIM), F32), sin], axis=0)
    return jnp.tile(cos, (1, N_Q_HEADS)), jnp.tile(sin, (1, N_Q_HEADS))


def _to_heads(a, nh):
    return a.reshape(a.shape[0], nh, HEAD_DIM).transpose(1, 0, 2)


def _from_heads(a):
    return a.transpose(1, 0, 2).reshape(a.shape[1], a.shape[0] * HEAD_DIM)


def _sample_step(x, ctx, tgt, mod, cmod, W, pending, cvec):
    W = dict(W)
    seq = x.shape[0]
    n_ctx = ctx.shape[0]
    tm = min(256, seq)
    tmc = min(256, n_ctx)
    tmm = 512 if seq % 512 == 0 else tm
    tq = min(256, seq)
    assert n_ctx % tm == 0 and seq % tm == 0

    def mv(l, j):
        return mod[l, j * D_MODEL:(j + 1) * D_MODEL].reshape(1, D_MODEL)

    csh1 = cmod[:D_MODEL].reshape(1, D_MODEL)
    csc1 = cmod[D_MODEL:2 * D_MODEL].reshape(1, D_MODEL)
    g_mix = [W['g_mix'][l:l + 1] for l in range(2)]
    g_ffn = [W['g_ffn'][l:l + 1] for l in range(2)]
    cos, sin = _rope_tables(seq, n_ctx)
    gq = jnp.tile(W['q_gain'], (1, N_Q_HEADS))
    gk = jnp.tile(W['k_gain'], (1, N_KV_HEADS))
    bdq = jnp.kron(jnp.eye(N_Q_HEADS, dtype=F32), jnp.ones((HEAD_DIM, HEAD_DIM), F32)).astype(BF16)
    bdk = bdq[:KV_W, :KV_W]
    w_sp = W['w_sp'].astype(BF16)
    w_sp_t = w_sp.transpose(0, 2, 1)
    bsp = jnp.broadcast_to(W['b_sp'].T[:, :, None], (CHUNK, N_SG, CHUNK)).reshape(CHUNK, SG_W)

    def ffn_fwd(l, h_mid):
        xf = _modnorm_fwd(f'l{l}_ffn_norm', h_mid, g_ffn[l], mv(l, 4), mv(l, 3), tm)
        gu, a = _ffn_in_swiglu(f'l{l}_ffn_in', xf, W['ffn_in'][l], tmm)
        f = _mm_rows(f'l{l}_ffn_out', a, W['ffn_out'][l], tmm)
        h_out = _resgate_fwd(f'l{l}_ffn_res', h_mid, f, mv(l, 5), tm)
        return h_out, (h_mid, xf, gu, a, f)

    def ffn_bwd(l, dh_out, saved):
        h_mid, xf, gu, a, f = saved
        dy, d_gt2, _ = _resgate_bwd(f'l{l}_ffn_res_bwd', dh_out, f, mv(l, 5), tm)
        dgu = _ffn_out_dx_swiglu(f'l{l}_ffn_out_dx', dy, W['ffn_out'][l], gu, tmm)
        d_wo = _mm_wgrad(f'l{l}_ffn_out_dw', a, dy, tmm)
        dxf = _mm_halves_blk_t(f'l{l}_ffn_in_dx', dgu, W['ffn_in'][l], tmm)
        d_wi = _mm_halves_wgrad_blk(f'l{l}_ffn_in_dw', xf, dgu, tmm)
        dh_mid, d_sh2, d_sc2, d_g = _modnorm_bwd(f'l{l}_ffn_norm_bwd', h_mid, [_ri(dxf)], dh_out, g_ffn[l],
                                                 mv(l, 4), tm)
        return dh_mid, d_wi, d_wo, d_g, (d_sh2, d_sc2, d_gt2)

    xm0 = _modnorm_fwd('l0_mix_norm', x, g_mix[0], mv(0, 1), mv(0, 0), tm)
    xc0 = _modnorm_fwd('l0_ctx_norm', ctx, g_mix[0], csc1, csh1, tmc)
    xall = jnp.concatenate([xc0, xm0], axis=0)
    p_qkv = _mm_rows('l0_qkv', xall, W['w_qkv'], tm)
    p_sg = _mm_rows('l0_sg_in', xm0, W['w_sgp'], tmm)
    q_all, k_all, v_all = _qk_prep_fwd('l0_qk_prep', p_qkv, cos, sin, gq, gk, bdq, bdk, tm)
    qh = _to_heads(q_all[n_ctx:], N_Q_HEADS)
    kh = _to_heads(k_all, N_KV_HEADS)
    vh = _to_heads(v_all, N_KV_HEADS)
    o, lse, *gw = _attn_fwd('l0_attn', qh, kh, vh, tq, gather=pending)
    W['ffn_in'] = [gw[0].reshape(N_CHIP, D_MODEL, FF_BLK), gw[4].reshape(N_CHIP, D_MODEL, FF_BLK)]
    W['ffn_out'] = [gw[1].reshape(D_FF, D_MODEL), gw[5].reshape(D_FF, D_MODEL)]
    W['pw1'] = gw[2].reshape(N_CHIP, D_MODEL, PW1_BLK)
    W['pw2'] = gw[3].reshape(D_MODEL, D_MODEL)
    sg = _sg_fwd('l0_sg', p_sg, w_sp, bsp, tm)
    cat = jnp.concatenate([_from_heads(o).astype(BF16), sg], axis=1)
    y0 = _mm_rows('l0_out', cat, W['w_out'], tmm)
    h1 = _resgate_fwd('l0_mix_res', x, y0, mv(0, 2), tm)
    h2, ffn0 = ffn_fwd(0, h1)

    xm1 = _modnorm_fwd('l1_mix_norm', h2, g_mix[1], mv(1, 1), mv(1, 0), tm)
    ag = _mm_blk('l1_pw1', xm1, W['pw1'], tmm)
    hg = _glu_fwd('l1_glu', ag, W['b_pw1'], tm)
    hc = _dwconv('l1_conv', hg, W['w_dw'], W['b_dw'])
    s1 = _lnsilu_fwd('l1_ln_silu', hc, W['ln_g'], W['ln_b'], tm)
    y1 = _mm_rows('l1_pw2', s1, W['pw2'], tmm, bias=W['b_pw2'])
    h3 = _resgate_fwd('l1_mix_res', h2, y1, mv(1, 2), tm)
    h4, ffn1 = ffn_fwd(1, h3)

    dh4, d_g_final, loss_vec = _final_loss('final_loss', h4, tgt, W['g_final'], tm)

    dh3, d_wi1, d_wo1, d_gffn1, dmod1_ffn = ffn_bwd(1, dh4, ffn1)
    dy1, d_gt1_1, d_b_pw2 = _resgate_bwd('l1_mix_res_bwd', dh3, y1, mv(1, 2), tm)
    ds1 = _mm_rows('l1_pw2_dx', dy1, W['pw2'], tmm, trans_b=True)
    d_pw2 = _mm_wgrad('l1_pw2_dw', s1, dy1, tmm)
    dhc, d_ln_g, d_ln_b = _lnsilu_bwd('l1_ln_silu_bwd', hc, ds1, W['ln_g'], W['ln_b'], tm)
    dhg = _dwconv('l1_conv_dx', dhc, W['w_dw'][::-1], jnp.zeros((1, D_MODEL), F32))
    d_w_dw, d_b_dw = _dwconv_wgrad('l1_conv_dw', hg, dhc)
    dag, d_b_pw1 = _glu_bwd('l1_glu_bwd', ag, dhg, W['b_pw1'], tm)
    dxm1 = _mm_blk_t('l1_pw1_dx', dag, W['pw1'], tmm)
    d_pw1 = _mm_wgrad_blk('l1_pw1_dw', xm1, dag, N_CHIP, tmm)
    dh2, d_sh1_1, d_sc1_1, d_gmix1 = _modnorm_bwd('l1_mix_norm_bwd', h2, [_ri(dxm1)], dh3, g_mix[1], mv(1, 1), tm)

    dh1, d_wi0, d_wo0, d_gffn0, dmod0_ffn = ffn_bwd(0, dh2, ffn0)
    dy0, d_gt1_0, _ = _resgate_bwd('l0_mix_res_bwd', dh1, y0, mv(0, 2), tm)
    dcat = _mm_rows('l0_out_dx', dy0, W['w_out'], tmm, trans_b=True)
    d_w_out = _mm_wgrad('l0_out_dw', cat, dy0, tmm)
    do = _to_heads(dcat[:, :ATTN_W], N_Q_HEADS)

    def blocked(a):
        cols = a.shape[-1]
        return a.reshape(N_CHIP, 2, a.size // (2 * N_CHIP * cols), cols)

    gs = [blocked(g) for g in (d_wi0, d_wi1, d_wo0, d_wo1, d_w_out, d_pw1, d_pw2)]
    sib = _rs_pair_send('rs_pair_send', gs)
    ss = [_rs_pair_add(f'rs_pair_add{t}', gs[t], sib[t], cvec) for t in range(len(gs))]
    dq, dk, dv, *xs = _attn_bwd('l0_attn_bwd', qh, kh, vh, o, lse, do, tq, exchange=ss)
    dq_all = jnp.concatenate([jnp.zeros((n_ctx, ATTN_W), F32), _from_heads(dq)], axis=0)
    dp_qkv, d_gq, d_gk = _qk_prep_bwd('l0_qk_prep_bwd', p_qkv, cos, sin, dq_all, _from_heads(dk), _from_heads(dv),
                                      gq, gk, bdq, bdk, tm)
    dp_sg, d_w_sp, d_bsp = _sg_bwd('l0_sg_bwd', p_sg, dcat, w_sp, w_sp_t, bsp, tm)
    d_w_qkv = _mm_wgrad('l0_qkv_dw', xall, dp_qkv, tm)
    d_w_sgp = _mm_wgrad('l0_sg_in_dw', xm0, dp_sg, tmm)
    dxall = _mm_rows('l0_qkv_dx', dp_qkv, W['w_qkv'], tm, trans_b=True)
    dxm_sg = _mm_rows('l0_sg_in_dx', dp_sg, W['w_sgp'], tmm, trans_b=True)
    dx, d_sh1_0, d_sc1_0, d_gmix0 = _modnorm_bwd('l0_mix_norm_bwd', x, [_ri(dxall, None, 0, n_ctx // tm), _ri(dxm_sg)],
                                                 dh1, g_mix[0], mv(0, 1), tm)
    _, d_csh1, d_csc1, d_gmix0c = _modnorm_bwd('l0_ctx_norm_bwd', ctx, [_ri(dxall)], None, g_mix[0], csc1, tmc)

    g_in = blocked(jnp.concatenate([d_w_qkv, d_w_sgp], axis=1).reshape(D_MODEL, N_CHIP, -1).transpose(1, 0, 2))
    sib_in = _rs_pair_send('rs_pair_send_w_in', [g_in])
    xs_in = _rs_chip_exchange('rs_chip_exchange_w_in', [_rs_pair_add('rs_pair_add_w_in', g_in, sib_in[0], cvec)])
    t_ffn_in = _sum_into('rs_sum_ffn_in1', xs[1], cvec, 2, 1, prev=_sum_into('rs_sum_ffn_in0', xs[0], cvec, 2, 0))
    t_ffn_out = _sum_into('rs_sum_ffn_out1', xs[3], cvec, 2, 1, prev=_sum_into('rs_sum_ffn_out0', xs[2], cvec, 2, 0))
    reduced = _rs_pair_share('rs_pair_share', [
        t_ffn_in, t_ffn_out, _sum_into('rs_sum_w_in', xs_in[0], cvec, 1, 0), _sum_into('rs_sum_w_out', xs[4], cvec, 1, 0),
        _sum_into('rs_sum_pw1', xs[5], cvec, 1, 0), _sum_into('rs_sum_pw2', xs[6], cvec, 1, 0)])
    big = dict(zip(['w_ffn_in', 'w_ffn_out', 'w_in', 'w_out', 'w_pw1', 'w_pw2'], reduced))

    zero = jnp.zeros((1, D_MODEL), F32)
    dmod = jnp.concatenate([d_sh1_0, d_sc1_0, d_gt1_0, *dmod0_ffn, d_sh1_1, d_sc1_1, d_gt1_1, *dmod1_ffn], axis=0)
    dcmod = jnp.concatenate([d_csh1, d_csc1, zero, zero, zero, zero], axis=0)
    grads = dict(
        g_mix=jnp.concatenate([d_gmix0 + d_gmix0c, d_gmix1], axis=0),
        g_ffn=jnp.concatenate([d_gffn0, d_gffn1], axis=0), g_final=d_g_final, gq=d_gq, gk=d_gk,
        w_sp=d_w_sp, bsp=d_bsp, b_pw1=d_b_pw1, w_dw=d_w_dw, b_dw=d_b_dw, ln_g=d_ln_g, ln_b=d_ln_b, b_pw2=d_b_pw2,
        dmod=dmod, dcmod=dcmod)
    return loss_vec, dx, grads, big


def kernel(x, c, ctx, c_ctx, w_mod, b_mod, g_mix, g_ffn, w_ffn_in, w_ffn_out, w_in, q_gain, k_gain, w_sp, b_sp, w_out, w_pw1, b_pw1, w_dw, b_dw, ln_g, ln_b, w_pw2, b_pw2, g_final, loss_target, m_c_ctx, m_w_mod, m_b_mod, m_g_mix, m_g_ffn, m_w_ffn_in, m_w_ffn_out, m_w_in, m_q_gain, m_k_gain, m_w_sp, m_b_sp, m_w_out, m_w_pw1, m_b_pw1, m_w_dw, m_b_dw, m_ln_g, m_ln_b, m_w_pw2, m_b_pw2, m_g_final, v_c_ctx, v_w_mod, v_b_mod, v_g_mix, v_g_ffn, v_w_ffn_in, v_w_ffn_out, v_w_in, v_q_gain, v_k_gain, v_w_sp, v_b_sp, v_w_out, v_w_pw1, v_b_pw1, v_w_dw, v_b_dw, v_ln_g, v_ln_b, v_w_pw2, v_b_pw2, v_g_final):
    given = dict(locals())
    ix, iy, ic = _mesh_pos()
    chip = 2 * ix + iy
    me = 2 * chip + ic
    d = D_MODEL
    q4 = d // N_CHIP

    small = jnp.concatenate([c.reshape(4, q4), b_pw1.reshape(2, q4), w_dw[0], b_dw, ln_g, ln_b, b_pw2,
                             jnp.zeros((7, q4), F32)], axis=0)
    def my_half(a):
        r = a.shape[0] // 2
        return lax.dynamic_slice_in_dim(a, ic * r, r, axis=0).astype(BF16)

    sgath, g_w_in, g_w_out = _allgather('ag_first', [small, my_half(w_in[0]), my_half(w_out[0])])
    c_all = sgath[:, 0:4].reshape(N_DEV, d)
    per_chip = sgath[0::2]
    b_pw1_f = per_chip[:, 4:6].reshape(1, 2 * d)
    w_dw_f = per_chip[:, 6:6 + CONV_W].transpose(1, 0, 2).reshape(CONV_W, d)
    b_dw_f, ln_g_f, ln_b_f, b_pw2_f = [per_chip[:, 37 + i].reshape(1, d) for i in range(4)]

    c16 = jnp.concatenate([c_all, c_ctx[None, :], jnp.zeros((7, d), F32)], axis=0)
    wm_b = w_mod.astype(BF16)
    bm_sh = lax.dynamic_slice_in_dim(b_mod, chip * MOD_BLK, MOD_BLK, axis=1).reshape(2, 1, MOD_BLK)
    mod_piece = _mod_fwd('mod_fwd', c16, wm_b, bm_sh)
    mgath = _allgather('ag_mod', [mod_piece.reshape(32, MOD_BLK)])[0]
    mod_all = mgath[0::2].reshape(N_CHIP, 2, 16, MOD_BLK).transpose(1, 2, 0, 3).reshape(2, 16, MOD_W)
    mod_me = lax.dynamic_index_in_dim(mod_all, me, axis=1, keepdims=False)
    cmod = mod_all[0, N_DEV]

    w_in_f = g_w_in.reshape(N_CHIP, d, -1).transpose(1, 0, 2).reshape(d, -1)
    W = dict(
        w_qkv=w_in_f[:, :QKV_W], w_sgp=w_in_f[:, QKV_W:], w_out=g_w_out.reshape(d, d),
        g_mix=g_mix, g_ffn=g_ffn, g_final=g_final.reshape(1, d), q_gain=q_gain, k_gain=k_gain, w_sp=w_sp[0],
        b_sp=b_sp[0], b_pw1=b_pw1_f, w_dw=w_dw_f, b_dw=b_dw_f, ln_g=ln_g_f, ln_b=ln_b_f, b_pw2=b_pw2_f)
    pending = [my_half(w_ffn_in[0]), my_half(w_ffn_out[0]), my_half(w_pw1[0]), my_half(w_pw2[0]),
               my_half(w_ffn_in[1]), my_half(w_ffn_out[1])]
    cvec = jnp.reshape(ic, (1,)).astype(jnp.int32)

    loss_vec, dx, G, big = _sample_step(x[0], ctx[0], loss_target[0], mod_me, cmod, W, pending, cvec)
    loss = lax.psum(jnp.sum(loss_vec), ("x", "y", "c"))

    pack = jnp.concatenate([
        G['g_mix'], G['g_ffn'], G['g_final'],
        jnp.concatenate([G['gq'], G['gk'], jnp.zeros((1, d - ATTN_W - KV_W), F32)], axis=1),
        G['b_pw1'].reshape(2, d), G['w_dw'], G['b_dw'], G['ln_g'], G['ln_b'], G['b_pw2'],
        G['dmod'], G['dcmod'], jnp.zeros((3, d), F32),
        G['w_sp'].reshape(64, d), G['bsp'].reshape(64, d)], axis=0)
    assert pack.shape == (192, d)
    pg = _allgather('ag_small_grads', [pack])[0]
    s8 = _sum_slots('sum_small_grads', pg, 64)

    def chip_cols(a, width):
        return lax.dynamic_slice_in_dim(a, chip * width, width, axis=a.ndim - 1)

    grads = {}
    grads['g_mix'] = s8[0:2]
    grads['g_ffn'] = s8[2:4]
    grads['g_final'] = s8[4]
    grads['q_gain'] = s8[5, :ATTN_W].reshape(N_Q_HEADS, HEAD_DIM).sum(axis=0)[None, :]
    grads['k_gain'] = s8[5, ATTN_W:ATTN_W + KV_W].reshape(N_KV_HEADS, HEAD_DIM).sum(axis=0)[None, :]
    grads['b_pw1'] = chip_cols(s8[6:8].reshape(1, 2 * d), 2 * q4)
    grads['w_dw'] = chip_cols(s8[8:8 + CONV_W], q4)[None]
    grads['b_dw'] = chip_cols(s8[39:40], q4)
    grads['ln_g'] = chip_cols(s8[40:41], q4)
    grads['ln_b'] = chip_cols(s8[41:42], q4)
    grads['b_pw2'] = chip_cols(s8[42:43], q4)
    grads['w_sp'] = s8[64:128].reshape(1, N_SG, CHUNK, CHUNK)
    grads['b_sp'] = s8[128:192].reshape(CHUNK, N_SG, CHUNK).sum(axis=-1).T[None]

    dm_rows = pg[:, 43:55].reshape(N_DEV, 2, MOD_W)
    ctx_row = s8[55:61].reshape(1, MOD_W)
    dm = jnp.stack([jnp.concatenate([dm_rows[:, 0], ctx_row, jnp.zeros((7, MOD_W), F32)], axis=0),
                    jnp.concatenate([dm_rows[:, 1], jnp.zeros((8, MOD_W), F32)], axis=0)], axis=0)
    grads['b_mod'] = jnp.stack([s8[43:49].reshape(MOD_W) + ctx_row[0], s8[49:55].reshape(MOD_W)], axis=0)
    dm_sh = chip_cols(dm, MOD_BLK)
    grads['w_mod'] = _mod_wgrad('mod_dw', c16, dm_sh)
    d_silu = _mm_rows('mod_dx', dm_sh[0], wm_b[0], 16, trans_b=True)
    parts = _allgather('ag_cctx', [d_silu[8:16]])[0]
    grads['c_ctx'] = _cctx_grad('cctx_grad', parts, c_ctx.reshape(1, d))[0]

    grads.update(big)

    deltas, new_m, new_v = {}, {}, {}
    for n in WEIGHT_NAMES:
        grads[n] = grads[n].reshape(given[n].shape)
        deltas[n], new_m[n], new_v[n] = _adamw(f'adamw_{n}', given[n], grads[n], given['m_' + n], given['v_' + n])

    return (loss, dx[None], *[grads[n] for n in WEIGHT_NAMES], *[deltas[n] for n in WEIGHT_NAMES],
            *[new_m[n] for n in WEIGHT_NAMES], *[new_v[n] for n in WEIGHT_NAMES])
```

```python
import functools
import math

import jax
import jax.numpy as jnp
from jax import lax
from jax.experimental import pallas as pl
from jax.experimental.pallas import tpu as pltpu

F32 = jnp.float32
BF16 = jnp.bfloat16
MESH = pl.DeviceIdType.MESH

D_MODEL = 1024
HEAD_DIM = 64
N_Q_HEADS = 8
N_KV_HEADS = 2
Q_GROUP = N_Q_HEADS // N_KV_HEADS
ATTN_W = N_Q_HEADS * HEAD_DIM
KV_W = N_KV_HEADS * HEAD_DIM
QKV_W = ATTN_W + 2 * KV_W
SG_W = D_MODEL - ATTN_W
N_SG = 4
CHUNK = 128
GRID_W = 64
ROPE_THETA = 10000.0
CONV_W = 31
CONV_HALO = 16
D_FF = 2816
MOD_W = 6 * D_MODEL
N_CHIP = 4
N_DEV = 8
FF_BLK = 2 * D_FF // N_CHIP
PW1_BLK = 2 * D_MODEL // N_CHIP
MOD_BLK = MOD_W // N_CHIP
EPS = 1e-6
GELU_C = math.sqrt(2.0 / math.pi)
GELU_A = 0.044715

ADAM_LR = 0.001
ADAM_B1 = 0.9
ADAM_B2 = 0.999
ADAM_EPS = 1e-08
ADAM_WD = 0.01
ADAM_STEP = 10

VMEM_LIMIT_BYTES = 56 * 1024 * 1024

WEIGHT_NAMES = ['c_ctx', 'w_mod', 'b_mod', 'g_mix', 'g_ffn', 'w_ffn_in', 'w_ffn_out', 'w_in', 'q_gain', 'k_gain',
                'w_sp', 'b_sp', 'w_out', 'w_pw1', 'b_pw1', 'w_dw', 'b_dw', 'ln_g', 'ln_b', 'w_pw2', 'b_pw2',
                'g_final']


def _params():
    return pltpu.CompilerParams(vmem_limit_bytes=VMEM_LIMIT_BYTES)


def _ri(arr, width=None, col_block=0, row_off=0, row_fn=None):
    return (arr, arr.shape[1] if width is None else width, col_block,
            (lambda i: i + row_off) if row_fn is None else row_fn)


def _rowwise(name, body, n_tiles, tm, row_ins, vec_ins, row_outs, vec_outs):
    nri, nvi, nro = len(row_ins), len(vec_ins), len(row_outs)

    def kern(*refs):
        ri = refs[:nri]
        vi = refs[nri:nri + nvi]
        ro = refs[nri + nvi:nri + nvi + nro]
        vo = refs[nri + nvi + nro:]
        if vo:
            @pl.when(pl.program_id(0) == 0)
            def _():
                for r in vo:
                    r[...] = jnp.zeros(r.shape, r.dtype)
        body(ri, vi, ro, vo)

    in_specs = [pl.BlockSpec((tm, bw), lambda i, cb=cb, rf=rf: (rf(i), cb)) for (_, bw, cb, rf) in row_ins]
    in_specs += [pl.BlockSpec(a.shape, lambda i, nd=a.ndim: (0,) * nd) for a in vec_ins]
    out_specs = [pl.BlockSpec((tm, w), lambda i: (i, 0)) for (w, _) in row_outs]
    out_specs += [pl.BlockSpec(s, lambda i, nd=len(s): (0,) * nd) for s in vec_outs]
    out_shape = [jax.ShapeDtypeStruct((n_tiles * tm, w), dt) for (w, dt) in row_outs]
    out_shape += [jax.ShapeDtypeStruct(s, F32) for s in vec_outs]
    return pl.pallas_call(kern, name=name, grid=(n_tiles,), in_specs=in_specs, out_specs=out_specs,
                          out_shape=out_shape, compiler_params=_params())(*[a for (a, _, _, _) in row_ins], *vec_ins)


def _colsum(v):
    return jnp.sum(v, axis=0, keepdims=True)


def _rowmean(v):
    return jnp.mean(v, axis=-1, keepdims=True)


def _rms(h):
    r = lax.rsqrt(_rowmean(h * h) + EPS)
    return h * r, r


def _rms_bwd(n, r, gy):
    return r * (gy - n * _rowmean(gy * n))


def _ln_stats(v):
    xc = v - _rowmean(v)
    r = lax.rsqrt(_rowmean(xc * xc) + EPS)
    return xc * r, r


def _ln_bwd(xh, r, dxh):
    return r * (dxh - _rowmean(dxh) - xh * _rowmean(dxh * xh))


def _sigmoid(v):
    return 1.0 / (1.0 + jnp.exp(-v))


def _gelu(v):
    t = jnp.tanh(GELU_C * (v + GELU_A * (v * v * v)))
    return 0.5 * v * (1.0 + t), t


def _gelu_grad(v, t):
    return 0.5 * (1.0 + t) + 0.5 * v * (1.0 - t * t) * (GELU_C * (1.0 + 3.0 * GELU_A * (v * v)))


def _modnorm_fwd(name, h, g, sc, sh, tm):
    def body(ri, vi, ro, vo):
        n, _ = _rms(ri[0][...])
        ro[0][...] = ((n * vi[0][...]) * (1.0 + vi[1][...]) + vi[2][...]).astype(BF16)

    return _rowwise(name, body, h.shape[0] // tm, tm, [_ri(h)], [g, sc, sh], [(D_MODEL, BF16)], [])[0]


def _modnorm_bwd(name, h, dxs, dh_in, g, sc, tm):
    ndx = len(dxs)

    def body(ri, vi, ro, vo):
        dxm = ri[1][...].astype(F32)
        for r in ri[2:1 + ndx]:
            dxm = dxm + r[...].astype(F32)
        gv = vi[0][...]
        n, r = _rms(ri[0][...])
        vo[0][...] += _colsum(dxm)
        vo[1][...] += _colsum(dxm * (n * gv))
        dy = dxm * (1.0 + vi[1][...])
        vo[2][...] += _colsum(dy * n)
        if dh_in is not None:
            ro[0][...] = ri[1 + ndx][...] + _rms_bwd(n, r, dy * gv)

    row_ins = [_ri(h)] + list(dxs) + ([_ri(dh_in)] if dh_in is not None else [])
    row_outs = [(D_MODEL, F32)] if dh_in is not None else []
    outs = _rowwise(name, body, h.shape[0] // tm, tm, row_ins, [g, sc], row_outs, [(1, D_MODEL)] * 3)
    if dh_in is None:
        return (None, *outs)
    return tuple(outs)


def _resgate_fwd(name, h, y, gt, tm):
    def body(ri, vi, ro, vo):
        ro[0][...] = ri[0][...] + vi[0][...] * ri[1][...]

    return _rowwise(name, body, h.shape[0] // tm, tm, [_ri(h), _ri(y)], [gt], [(D_MODEL, F32)], [])[0]


def _resgate_bwd(name, dh, y, gt, tm):
    def body(ri, vi, ro, vo):
        d = ri[0][...]
        dy = d * vi[0][...]
        ro[0][...] = dy.astype(BF16)
        vo[0][...] += _colsum(d * ri[1][...])
        vo[1][...] += _colsum(dy)

    return _rowwise(name, body, dh.shape[0] // tm, tm, [_ri(dh), _ri(y)], [gt], [(D_MODEL, BF16)],
                    [(1, D_MODEL)] * 2)


def _final_loss(name, h, tgt, g, tm):
    def body(ri, vi, ro, vo):
        gv = vi[0][...]
        n, r = _rms(ri[0][...])
        e = n * gv - ri[1][...]
        vo[1][...] += _colsum(e * e) * (0.5 / D_MODEL)
        dout = e * (1.0 / D_MODEL)
        vo[0][...] += _colsum(dout * n)
        ro[0][...] = _rms_bwd(n, r, dout * gv)

    return _rowwise(name, body, h.shape[0] // tm, tm, [_ri(h), _ri(tgt)], [g], [(D_MODEL, F32)],
                    [(1, D_MODEL)] * 2)


def _head_mean(v, bd):
    hi = v.astype(BF16)
    lo = (v - hi.astype(F32)).astype(BF16)
    s = jnp.dot(hi, bd, preferred_element_type=F32) + jnp.dot(lo, bd, preferred_element_type=F32)
    return s * (1.0 / HEAD_DIM)


def _swap16(v):
    w = v.shape[1]
    lane = lax.broadcasted_iota(jnp.int32, v.shape, 1)
    return jnp.where((lane & 16) == 0, pltpu.roll(v, w - 16, 1), pltpu.roll(v, 16, 1))


def _rope(v, cos, sin):
    return v * cos + _swap16(v) * sin


def _rope_bwd(d, cos, sin):
    return d * cos + _swap16(d * sin)


def _q_wide(t):
    return jnp.concatenate([t] * (ATTN_W // KV_W), axis=1)


def _qk_prep_fwd(name, p_qkv, cos, sin, gq, gk, bdq, bdk, tm):
    def body(ri, vi, ro, vo):
        pq = ri[0][...]
        pkv = ri[1][...]
        cosv = ri[2][...]
        sinv = ri[3][...]
        rq = lax.rsqrt(_head_mean(pq * pq, vi[2][...]) + EPS)
        ro[0][...] = (_rope(pq * rq * vi[0][...], _q_wide(cosv), _q_wide(sinv)) * (HEAD_DIM ** -0.5)).astype(BF16)
        pk = pkv[:, :KV_W]
        rk = lax.rsqrt(_head_mean(pk * pk, vi[3][...]) + EPS)
        ro[1][...] = _rope(pk * rk * vi[1][...], cosv, sinv).astype(BF16)
        ro[2][...] = pkv[:, KV_W:].astype(BF16)

    return _rowwise(name, body, p_qkv.shape[0] // tm, tm,
                    [_ri(p_qkv, ATTN_W, 0), _ri(p_qkv, 2 * KV_W, 2), _ri(cos), _ri(sin)], [gq, gk, bdq, bdk],
                    [(ATTN_W, BF16), (KV_W, BF16), (KV_W, BF16)], [])


def _qk_prep_bwd(name, p_qkv, cos, sin, dq, n_ctx_tiles, dk, dv, gq, gk, bdq, bdk, tm):
    def norm_rope_bwd(p, gain, bd, cosv, sinv, dout):
        r = lax.rsqrt(_head_mean(p * p, bd) + EPS)
        n = p * r
        dqn = _rope_bwd(dout, cosv, sinv)
        gy = dqn * gain
        return r * (gy - n * _head_mean(gy * n, bd)), _colsum(dqn * n)

    def body(ri, vi, ro, vo):
        pkv = ri[1][...]
        cosv = ri[2][...]
        sinv = ri[3][...]
        dqv = jnp.where(pl.program_id(0) >= n_ctx_tiles, ri[4][...] * (HEAD_DIM ** -0.5), 0.0)
        dxq, dgq = norm_rope_bwd(ri[0][...], vi[0][...], vi[2][...], _q_wide(cosv), _q_wide(sinv), dqv)
        dxk, dgk = norm_rope_bwd(pkv[:, :KV_W], vi[1][...], vi[3][...], cosv, sinv, ri[5][...])
        ro[0][:, :ATTN_W] = dxq.astype(BF16)
        ro[0][:, ATTN_W:ATTN_W + KV_W] = dxk.astype(BF16)
        ro[0][:, ATTN_W + KV_W:] = ri[6][...].astype(BF16)
        vo[0][...] += dgq
        vo[1][...] += dgk

    return _rowwise(name, body, p_qkv.shape[0] // tm, tm,
                    [_ri(p_qkv, ATTN_W, 0), _ri(p_qkv, 2 * KV_W, 2), _ri(cos), _ri(sin),
                     _ri(dq, row_fn=lambda i: jnp.maximum(i - n_ctx_tiles, 0)), _ri(dk), _ri(dv)],
                    [gq, gk, bdq, bdk], [(QKV_W, BF16)], [(1, ATTN_W), (1, KV_W)])


def _sg_fwd(name, p_sg, w_sp, bsp, tm):
    def body(ri, vi, ro, vo):
        p = ri[0][...]
        gu, _ = _gelu(p[:, :SG_W])
        gv, _ = _gelu(p[:, SG_W:])
        for g in range(N_SG):
            sl = slice(g * CHUNK, (g + 1) * CHUNK)
            vn, _ = _ln_stats(gv[:, sl])
            vnb = vn.astype(BF16)
            for m in range(tm // CHUNK):
                rs = slice(m * CHUNK, (m + 1) * CHUNK)
                mixed = jnp.dot(vi[0][g], vnb[rs], preferred_element_type=F32) + vi[1][:, sl]
                ro[0][rs, sl] = (gu[rs, sl] * mixed).astype(BF16)

    return _rowwise(name, body, p_sg.shape[0] // tm, tm, [_ri(p_sg)], [w_sp, bsp], [(SG_W, BF16)], [])[0]


def _sg_bwd(name, p_sg, dcat, w_sp, w_sp_t, bsp, tm):
    def body(ri, vi, ro, vo):
        p = ri[0][...]
        dsg = ri[1][...]
        su = p[:, :SG_W]
        sv = p[:, SG_W:]
        gu, tu = _gelu(su)
        gv, tv = _gelu(sv)
        for g in range(N_SG):
            sl = slice(g * CHUNK, (g + 1) * CHUNK)
            vn, r = _ln_stats(gv[:, sl])
            vnb = vn.astype(BF16)
            for m in range(tm // CHUNK):
                rs = slice(m * CHUNK, (m + 1) * CHUNK)
                mixed = jnp.dot(vi[0][g], vnb[rs], preferred_element_type=F32) + vi[2][:, sl]
                d_o = dsg[rs, sl]
                dm = d_o * gu[rs, sl]
                vo[1][:, sl] += dm
                dmb = dm.astype(BF16)
                vo[0][g] += lax.dot_general(dmb, vnb[rs], (((1,), (1,)), ((), ())), preferred_element_type=F32)
                dvn = jnp.dot(vi[1][g], dmb, preferred_element_type=F32)
                dgv = _ln_bwd(vn[rs], r[rs], dvn)
                ro[0][rs, sl] = (d_o * mixed * _gelu_grad(su[rs, sl], tu[rs, sl])).astype(BF16)
                ro[0][rs, SG_W + g * CHUNK:SG_W + (g + 1) * CHUNK] = (
                    dgv * _gelu_grad(sv[rs, sl], tv[rs, sl])).astype(BF16)

    return _rowwise(name, body, p_sg.shape[0] // tm, tm, [_ri(p_sg), _ri(dcat, SG_W, 1)], [w_sp, w_sp_t, bsp],
                    [(2 * SG_W, BF16)], [(N_SG, CHUNK, CHUNK), (CHUNK, SG_W)])


def _glu_fwd(name, ag, b, tm):
    def body(ri, vi, ro, vo):
        v = ri[0][...] + vi[0][...]
        ro[0][...] = v[:, :D_MODEL] * _sigmoid(v[:, D_MODEL:])

    return _rowwise(name, body, ag.shape[0] // tm, tm, [_ri(ag)], [b], [(D_MODEL, F32)], [])[0]


def _glu_bwd(name, ag, dhg, b, tm):
    def body(ri, vi, ro, vo):
        v = ri[0][...] + vi[0][...]
        d = ri[1][...]
        a = v[:, :D_MODEL]
        sg = _sigmoid(v[:, D_MODEL:])
        da = d * sg
        dgate = d * a * sg * (1.0 - sg)
        ro[0][:, :D_MODEL] = da.astype(BF16)
        ro[0][:, D_MODEL:] = dgate.astype(BF16)
        vo[0][:, :D_MODEL] += _colsum(da)
        vo[0][:, D_MODEL:] += _colsum(dgate)

    return _rowwise(name, body, ag.shape[0] // tm, tm, [_ri(ag), _ri(dhg)], [b], [(2 * D_MODEL, BF16)],
                    [(1, 2 * D_MODEL)])


def _lnsilu_fwd(name, hc, g, b, tm):
    def body(ri, vi, ro, vo):
        xh, _ = _ln_stats(ri[0][...])
        ln = xh * vi[0][...] + vi[1][...]
        ro[0][...] = (ln * _sigmoid(ln)).astype(BF16)

    return _rowwise(name, body, hc.shape[0] // tm, tm, [_ri(hc)], [g, b], [(D_MODEL, BF16)], [])[0]


def _lnsilu_bwd(name, hc, ds, g, b, tm):
    def body(ri, vi, ro, vo):
        gv = vi[0][...]
        xh, r = _ln_stats(ri[0][...])
        ln = xh * gv + vi[1][...]
        sg = _sigmoid(ln)
        dln = ri[1][...] * (sg * (1.0 + ln * (1.0 - sg)))
        vo[0][...] += _colsum(dln * xh)
        vo[1][...] += _colsum(dln)
        ro[0][...] = _ln_bwd(xh, r, dln * gv)

    return _rowwise(name, body, hc.shape[0] // tm, tm, [_ri(hc), _ri(ds)], [g, b], [(D_MODEL, F32)],
                    [(1, D_MODEL)] * 2)


def _conv_blocks(seq):
    cb = 128
    tt = 128 if seq % 128 == 0 else seq
    return cb, tt


def _conv_taps(win, tt):
    n = win.shape[0]
    for s in range(8):
        ws = win if s == 0 else pltpu.roll(win, n - s, 0)
        for q in range(4):
            j = 8 * q + s - 1
            if 0 <= j < CONV_W:
                yield j, ws[8 * q:8 * q + tt, :]


def _fill_padded(pad_ref, x_ref, seq):
    zeros = jnp.zeros((CONV_HALO, pad_ref.shape[1]), F32)
    pad_ref[0:CONV_HALO, :] = zeros
    pad_ref[seq + CONV_HALO:seq + 2 * CONV_HALO, :] = zeros
    pad_ref[CONV_HALO:seq + CONV_HALO, :] = x_ref[...]


def _dwconv(name, xin, w, b):
    seq = xin.shape[0]
    cb, tt = _conv_blocks(seq)

    def kern(x_ref, w_ref, b_ref, o_ref, pad_ref):
        _fill_padded(pad_ref, x_ref, seq)
        wv = w_ref[...]
        bv = b_ref[...]

        def step(t, carry):
            base = pl.multiple_of(t * tt, tt)
            win = pad_ref[pl.ds(base, tt + 2 * CONV_HALO), :]
            acc = jnp.zeros((tt, cb), F32) + bv
            for j, rows in _conv_taps(win, tt):
                acc = acc + wv[j:j + 1, :] * rows
            o_ref[pl.ds(base, tt), :] = acc
            return carry

        lax.fori_loop(0, seq // tt, step, 0)

    return pl.pallas_call(
        kern, name=name, grid=(D_MODEL // cb,),
        in_specs=[pl.BlockSpec((seq, cb), lambda j: (0, j)), pl.BlockSpec((CONV_W, cb), lambda j: (0, j)),
                  pl.BlockSpec((1, cb), lambda j: (0, j))],
        out_specs=pl.BlockSpec((seq, cb), lambda j: (0, j)),
        out_shape=jax.ShapeDtypeStruct((seq, D_MODEL), F32),
        scratch_shapes=[pltpu.VMEM((seq + 2 * CONV_HALO, cb), F32)],
        compiler_params=_params())(xin, w, b)


def _dwconv_wgrad(name, xin, dout):
    seq = xin.shape[0]
    cb, tt = _conv_blocks(seq)

    def kern(x_ref, d_ref, dw_ref, db_ref, pad_ref):
        _fill_padded(pad_ref, x_ref, seq)
        dw_ref[...] = jnp.zeros(dw_ref.shape, F32)
        db_ref[...] = jnp.zeros(db_ref.shape, F32)

        def step(t, carry):
            base = pl.multiple_of(t * tt, tt)
            win = pad_ref[pl.ds(base, tt + 2 * CONV_HALO), :]
            d = d_ref[pl.ds(base, tt), :]
            db_ref[...] += _colsum(d)
            for j, rows in _conv_taps(win, tt):
                dw_ref[j:j + 1, :] += _colsum(d * rows)
            return carry

        lax.fori_loop(0, seq // tt, step, 0)

    return pl.pallas_call(
        kern, name=name, grid=(D_MODEL // cb,),
        in_specs=[pl.BlockSpec((seq, cb), lambda j: (0, j)), pl.BlockSpec((seq, cb), lambda j: (0, j))],
        out_specs=[pl.BlockSpec((CONV_W, cb), lambda j: (0, j)), pl.BlockSpec((1, cb), lambda j: (0, j))],
        out_shape=[jax.ShapeDtypeStruct((CONV_W, D_MODEL), F32), jax.ShapeDtypeStruct((1, D_MODEL), F32)],
        scratch_shapes=[pltpu.VMEM((seq + 2 * CONV_HALO, cb), F32)],
        compiler_params=_params())(xin, dout)


def _mm(name, a, b, out_sds, grid, a_spec, b_spec, o_spec, contract, k_axis=None, bias=None):
    dn = (contract, ((), ()))

    def kern(*refs):
        if bias is None:
            a_ref, b_ref, o_ref = refs
        else:
            a_ref, b_ref, bias_ref, o_ref = refs
        p = lax.dot_general(a_ref[...].astype(BF16), b_ref[...].astype(BF16), dn, preferred_element_type=F32)
        if bias is not None:
            p = p + bias_ref[...]
        if k_axis is None:
            o_ref[...] = p.astype(o_ref.dtype)
        else:
            k = pl.program_id(k_axis)

            @pl.when(k == 0)
            def _():
                o_ref[...] = p

            @pl.when(k > 0)
            def _():
                o_ref[...] += p

    in_specs = [a_spec, b_spec]
    args = [a, b]
    if bias is not None:
        in_specs.append(pl.BlockSpec(bias.shape, lambda *_: (0,) * bias.ndim))
        args.append(bias)
    return pl.pallas_call(kern, name=name, grid=grid, in_specs=in_specs, out_specs=o_spec, out_shape=out_sds,
                          compiler_params=_params())(*args)


def _mm_rows(name, a, b, tm, trans_b=False, bias=None):
    m, k = a.shape
    n = b.shape[0] if trans_b else b.shape[1]
    contract = ((1,), (1,)) if trans_b else ((1,), (0,))
    return _mm(name, a, b, jax.ShapeDtypeStruct((m, n), F32), (m // tm,),
               pl.BlockSpec((tm, k), lambda i: (i, 0)), pl.BlockSpec(b.shape, lambda i: (0, 0)),
               pl.BlockSpec((tm, n), lambda i: (i, 0)), contract, bias=bias)


def _ffn_in_swiglu(name, xf, wblk, tm):
    m, k = xf.shape
    nblk = wblk.shape[2]

    def kern(x_ref, wg_ref, wu_ref, gu_ref, a_ref):
        xv = x_ref[...]
        g = jnp.dot(xv, wg_ref[...], preferred_element_type=F32)
        u = jnp.dot(xv, wu_ref[...], preferred_element_type=F32)
        gu_ref[0] = g
        gu_ref[1] = u
        a_ref[...] = (g * _sigmoid(g) * u).astype(BF16)

    return pl.pallas_call(
        kern, name=name, grid=(2, m // tm),
        in_specs=[pl.BlockSpec((tm, k), lambda j, i: (i, 0)), pl.BlockSpec((None, k, nblk), lambda j, i: (j, 0, 0)),
                  pl.BlockSpec((None, k, nblk), lambda j, i: (j + 2, 0, 0))],
        out_specs=[pl.BlockSpec((2, tm, nblk), lambda j, i: (0, i, j)), pl.BlockSpec((tm, nblk), lambda j, i: (i, j))],
        out_shape=[jax.ShapeDtypeStruct((2, m, 2 * nblk), F32), jax.ShapeDtypeStruct((m, 2 * nblk), BF16)],
        compiler_params=_params())(xf, wblk, wblk)


def _ffn_out_dx_swiglu(name, dy, wo, gu, tm):
    m, k = dy.shape
    nblk = gu.shape[2] // 2

    def kern(dy_ref, w_ref, gu_ref, o_ref):
        da = lax.dot_general(dy_ref[...], w_ref[...], (((1,), (1,)), ((), ())), preferred_element_type=F32)
        g = gu_ref[0]
        sg = _sigmoid(g)
        o_ref[0] = (da * gu_ref[1] * (sg * (1.0 + g * (1.0 - sg)))).astype(BF16)
        o_ref[1] = (da * (g * sg)).astype(BF16)

    return pl.pallas_call(
        kern, name=name, grid=(2, m // tm),
        in_specs=[pl.BlockSpec((tm, k), lambda j, i: (i, 0)), pl.BlockSpec((nblk, k), lambda j, i: (j, 0)),
                  pl.BlockSpec((2, tm, nblk), lambda j, i: (0, i, j))],
        out_specs=pl.BlockSpec((2, tm, nblk), lambda j, i: (0, i, j)),
        out_shape=jax.ShapeDtypeStruct(gu.shape, BF16), compiler_params=_params())(dy, wo, gu)


def _ffn_in_dx(name, dgu, wblk, tm):
    m = dgu.shape[1]
    nb, k, nblk = wblk.shape

    def kern(a_ref, w_ref, o_ref):
        acc = None
        for j in range(nb):
            p = lax.dot_general(a_ref[j // 2, :, (j % 2) * nblk:(j % 2 + 1) * nblk], w_ref[j],
                                (((1,), (1,)), ((), ())), preferred_element_type=F32)
            acc = p if acc is None else acc + p
        o_ref[...] = acc

    return pl.pallas_call(
        kern, name=name, grid=(m // tm,),
        in_specs=[pl.BlockSpec((2, tm, 2 * nblk), lambda i: (0, i, 0)), pl.BlockSpec(wblk.shape, lambda i: (0, 0, 0))],
        out_specs=pl.BlockSpec((tm, k), lambda i: (i, 0)),
        out_shape=jax.ShapeDtypeStruct((m, k), F32), compiler_params=_params())(dgu, wblk)


def _mm_halves_wgrad_blk(name, a, b2, tk):
    t, k = a.shape
    nblk = b2.shape[2] // 2
    return _mm(name, a, b2, jax.ShapeDtypeStruct((N_CHIP, k, nblk), F32), (N_CHIP, t // tk),
               pl.BlockSpec((tk, k), lambda j, i: (i, 0)),
               pl.BlockSpec((None, tk, nblk), lambda j, i: (j // 2, i, j % 2)),
               pl.BlockSpec((None, k, nblk), lambda j, i: (j, 0, 0)), ((0,), (0,)), k_axis=1)


def _mm_wgrad(name, a, b, tk):
    t, k = a.shape
    n = b.shape[1]
    tn = n if k * n * 4 <= 4 * 1024 * 1024 else 512
    return _mm(name, a, b, jax.ShapeDtypeStruct((k, n), F32), (n // tn, t // tk),
               pl.BlockSpec((tk, k), lambda j, i: (i, 0)), pl.BlockSpec((tk, tn), lambda j, i: (i, j)),
               pl.BlockSpec((k, tn), lambda j, i: (0, j)), ((0,), (0,)), k_axis=1)


def _attn_fwd(name, q_all, k, v, n_ctx, tq, gather=()):
    r = k.shape[1]
    s = q_all.shape[0] - n_ctx
    gw = Q_GROUP * HEAD_DIM
    ng = len(gather)
    n_i = s // tq
    last = N_KV_HEADS * n_i - 1
    forward_at = (7 * last) // 8

    def kern(q_ref, k_ref, v_ref, *rest):
        o_ref, lse_ref = rest[ng], rest[ng + 1]
        if ng:
            start, forward, finish = _gather_phases(rest[:ng], rest[ng + 2:2 * ng + 2], *rest[2 * ng + 2:])
            step = pl.program_id(0) * n_i + pl.program_id(1)
            pl.when(step == 0)(start)
            pl.when(step == forward_at)(forward)
        kv = k_ref[...]
        vv = v_ref[...]
        for g in range(Q_GROUP):
            sl = slice(g * HEAD_DIM, (g + 1) * HEAD_DIM)
            sc = lax.dot_general(q_ref[:, sl], kv, (((1,), (1,)), ((), ())), preferred_element_type=F32)
            m = jnp.max(sc, axis=-1, keepdims=True)
            p = jnp.exp(sc - m)
            l = jnp.sum(p, axis=-1, keepdims=True)
            o_ref[:, sl] = jnp.dot(p.astype(BF16), vv, preferred_element_type=F32) / l
            lse_ref[g] = m + jnp.log(l)
        if ng:
            pl.when(step == last)(finish)

    return pl.pallas_call(
        kern, name=name, grid=(N_KV_HEADS, n_i),
        in_specs=[pl.BlockSpec((tq, gw), lambda j, i: (i + n_ctx // tq, j)),
                  pl.BlockSpec((None, r, HEAD_DIM), lambda j, i: (j, 0, 0)),
                  pl.BlockSpec((None, r, HEAD_DIM), lambda j, i: (j, 0, 0))] + [_ANY] * ng,
        out_specs=[pl.BlockSpec((tq, gw), lambda j, i: (i, j)),
                   pl.BlockSpec((Q_GROUP, tq, 1), lambda j, i: (j, i, 0))] + [_ANY] * ng,
        out_shape=[jax.ShapeDtypeStruct((s, ATTN_W), F32), jax.ShapeDtypeStruct((N_Q_HEADS, s, 1), F32)]
        + [jax.ShapeDtypeStruct((N_DEV,) + a.shape, a.dtype) for a in gather],
        scratch_shapes=_gather_scratch(ng) if ng else [],
        compiler_params=_params())(q_all, k, v, *gather)


def _attn_bwd(name, q_all, k, v, o, lse, dcat, n_ctx, tq, exchange=()):
    nkv, r, _ = k.shape
    s = o.shape[0]
    gw = Q_GROUP * HEAD_DIM
    ne = len(exchange)
    n_i = s // tq
    last = nkv * n_i - 1

    def kern(q_ref, k_ref, v_ref, o_ref, lse_ref, do_ref, *rest):
        dq_ref, dk_ref, dv_ref = rest[ne:ne + 3]
        if ne:
            start, finish = _exchange_phases(rest[:ne], rest[ne + 3:2 * ne + 3], *rest[2 * ne + 3:])
            step = pl.program_id(0) * n_i + pl.program_id(1)
            pl.when(step == 0)(start)

        @pl.when(pl.program_id(1) == 0)
        def _():
            dk_ref[...] = jnp.zeros(dk_ref.shape, F32)
            dv_ref[...] = jnp.zeros(dv_ref.shape, F32)

        kv = k_ref[...]
        vv = v_ref[...]
        for g in range(Q_GROUP):
            sl = slice(g * HEAD_DIM, (g + 1) * HEAD_DIM)
            qv = q_ref[:, sl]
            dov = do_ref[:, sl]
            delta = jnp.sum(dov * o_ref[:, sl], axis=-1, keepdims=True)
            sc = lax.dot_general(qv, kv, (((1,), (1,)), ((), ())), preferred_element_type=F32)
            p = jnp.exp(sc - lse_ref[g])
            dob = dov.astype(BF16)
            dp = lax.dot_general(dob, vv, (((1,), (1,)), ((), ())), preferred_element_type=F32)
            dsb = (p * (dp - delta)).astype(BF16)
            dq_ref[:, sl] = jnp.dot(dsb, kv, preferred_element_type=F32)
            dk_ref[...] += lax.dot_general(dsb, qv, (((0,), (0,)), ((), ())), preferred_element_type=F32)
            dv_ref[...] += lax.dot_general(p.astype(BF16), dob, (((0,), (0,)), ((), ())),
                                           preferred_element_type=F32)
        if ne:
            pl.when(step == last)(finish)

    kvmap = lambda j, i: (j, 0, 0)
    return pl.pallas_call(
        kern, name=name, grid=(nkv, n_i),
        in_specs=[pl.BlockSpec((tq, gw), lambda j, i: (i + n_ctx // tq, j)), pl.BlockSpec((None, r, HEAD_DIM), kvmap),
                  pl.BlockSpec((None, r, HEAD_DIM), kvmap), pl.BlockSpec((tq, gw), lambda j, i: (i, j)),
                  pl.BlockSpec((Q_GROUP, tq, 1), lambda j, i: (j, i, 0)),
                  pl.BlockSpec((tq, gw), lambda j, i: (i, j))] + [_ANY] * ne,
        out_specs=[pl.BlockSpec((tq, gw), lambda j, i: (i, j)), pl.BlockSpec((None, r, HEAD_DIM), kvmap),
                   pl.BlockSpec((None, r, HEAD_DIM), kvmap)] + [_ANY] * ne,
        out_shape=[jax.ShapeDtypeStruct((s, ATTN_W), F32), jax.ShapeDtypeStruct((nkv, r, HEAD_DIM), F32),
                   jax.ShapeDtypeStruct((nkv, r, HEAD_DIM), F32)]
        + [jax.ShapeDtypeStruct(a.shape, a.dtype) for a in exchange],
        scratch_shapes=_exchange_scratch(ne) if ne else [],
        compiler_params=_params())(q_all, k, v, o, lse, dcat, *exchange)


def _mod_fwd(name, c16, wm, bm):
    n = wm.shape[2]

    def kern(c_ref, w_ref, b_ref, o_ref):
        cv = c_ref[...]
        a = (cv * _sigmoid(cv)).astype(BF16)
        o_ref[...] = jnp.dot(a, w_ref[...].astype(BF16), preferred_element_type=F32) + b_ref[...]

    return pl.pallas_call(
        kern, name=name, grid=(2,),
        in_specs=[pl.BlockSpec(c16.shape, lambda l: (0, 0)), pl.BlockSpec((None, D_MODEL, n), lambda l: (l, 0, 0)),
                  pl.BlockSpec((None, 1, n), lambda l: (l, 0, 0))],
        out_specs=pl.BlockSpec((None, 16, n), lambda l: (l, 0, 0)),
        out_shape=jax.ShapeDtypeStruct((2, 16, n), F32), compiler_params=_params())(c16, wm, bm)


def _mod_wgrad(name, c16, dm):
    n = dm.shape[2]
    tn = 512

    def kern(c_ref, d_ref, o_ref):
        cv = c_ref[...]
        a = cv * _sigmoid(cv)
        o_ref[...] = lax.dot_general(a, d_ref[...], (((0,), (0,)), ((), ())), preferred_element_type=F32,
                                     precision=lax.Precision.HIGHEST)

    return pl.pallas_call(
        kern, name=name, grid=(2, n // tn),
        in_specs=[pl.BlockSpec(c16.shape, lambda l, j: (0, 0)), pl.BlockSpec((None, 16, tn), lambda l, j: (l, 0, j))],
        out_specs=pl.BlockSpec((None, D_MODEL, tn), lambda l, j: (l, 0, j)),
        out_shape=jax.ShapeDtypeStruct((2, D_MODEL, n), F32), compiler_params=_params())(c16, dm)


def _cctx_grad(name, parts, c_ctx):
    def kern(p_ref, c_ref, o_ref):
        d = p_ref[0, 0:1, :] + p_ref[2, 0:1, :] + p_ref[4, 0:1, :] + p_ref[6, 0:1, :]
        cv = c_ref[...]
        sg = _sigmoid(cv)
        o_ref[...] = d * (sg * (1.0 + cv * (1.0 - sg)))

    return pl.pallas_call(kern, name=name, out_shape=jax.ShapeDtypeStruct((1, D_MODEL), F32),
                          compiler_params=_params())(parts, c_ctx)


def _sum_slots(name, g, tr):
    n, rows, cols = g.shape

    def kern(g_ref, o_ref):
        acc = g_ref[0].astype(F32)
        for i in range(1, n):
            acc = acc + g_ref[i].astype(F32)
        o_ref[...] = acc

    return pl.pallas_call(kern, name=name, grid=(rows // tr,),
                          in_specs=[pl.BlockSpec((n, tr, cols), lambda i: (0, i, 0))],
                          out_specs=pl.BlockSpec((tr, cols), lambda i: (i, 0)),
                          out_shape=jax.ShapeDtypeStruct((rows, cols), F32), compiler_params=_params())(g)


def _sum_into(name, g, cvec, n_layers, layer, prev=None):
    n, r, cc = g.shape
    tr = _row_tile(r, cc, 512 * 1024)

    def kern(c_ref, g_ref, *rest):
        acc = g_ref[0].astype(F32)
        for i in range(1, n):
            acc = acc + g_ref[i].astype(F32)
        rest[-1][...] = acc

    in_specs = [pl.BlockSpec((n, tr, cc), lambda i, c_ref: (0, i, 0))]
    args = [cvec, g]
    aliases = {}
    if prev is not None:
        in_specs.append(_ANY)
        args.append(prev)
        aliases = {2: 0}
    grid_spec = pltpu.PrefetchScalarGridSpec(
        num_scalar_prefetch=1, grid=(r // tr,), in_specs=in_specs,
        out_specs=pl.BlockSpec((None, None, tr, cc), lambda i, c_ref: (layer, c_ref[0], i, 0)))
    return pl.pallas_call(kern, name=name, grid_spec=grid_spec,
                          out_shape=jax.ShapeDtypeStruct((n_layers, 2, r, cc), F32),
                          input_output_aliases=aliases, compiler_params=_params())(*args)


def _row_tile(rows, cols, max_bytes=1024 * 1024):
    if rows * cols * 4 <= 2 * max_bytes:
        return rows
    best = None
    for t in range(16, rows + 1, 16):
        if rows % t == 0 and t * cols * 4 <= max_bytes:
            best = t
    assert best is not None, (rows, cols)
    return best


def _adamw(name, w, g, m, v):
    shape = w.shape
    cols = shape[-1]
    rows = w.size // cols
    tr = _row_tile(rows, cols)

    def kern(w_ref, g_ref, m_ref, v_ref, d_ref, nm_ref, nv_ref):
        gv = g_ref[...]
        m2 = ADAM_B1 * m_ref[...] + (1.0 - ADAM_B1) * gv
        v2 = ADAM_B2 * v_ref[...] + (1.0 - ADAM_B2) * (gv * gv)
        m_hat = m2 / (1.0 - ADAM_B1 ** ADAM_STEP)
        v_hat = v2 / (1.0 - ADAM_B2 ** ADAM_STEP)
        d_ref[...] = -ADAM_LR * (m_hat / (jnp.sqrt(v_hat) + ADAM_EPS) + ADAM_WD * w_ref[...])
        nm_ref[...] = m2
        nv_ref[...] = v2

    spec = pl.BlockSpec((tr, cols), lambda i: (i, 0))
    sds = jax.ShapeDtypeStruct((rows, cols), F32)
    outs = pl.pallas_call(kern, name=name, grid=(rows // tr,), in_specs=[spec] * 4, out_specs=[spec] * 3,
                          out_shape=[sds] * 3, compiler_params=_params())(
        w.reshape(rows, cols), g.reshape(rows, cols), m.reshape(rows, cols), v.reshape(rows, cols))
    return tuple(o.reshape(shape) for o in outs)


_ANY = pl.BlockSpec(memory_space=pl.ANY)


def _mesh_pos():
    return lax.axis_index("x"), lax.axis_index("y"), lax.axis_index("c")


def _gather_phases(srcs, outs, send_sems, recv_sems, local_sems):
    n = len(srcs)
    x, y, c = _mesh_pos()
    me = (x, y, c)
    sibling = (x, y, 1 - c)
    chips = [(1 - x, y), (x, 1 - y), (1 - x, 1 - y)]

    def slot(px, py, pc):
        return 4 * px + 2 * py + pc

    def copy(t, k, s, to, src=None):
        dst = outs[t].at[s]
        return pltpu.make_async_remote_copy(src_ref=dst if src is None else src, dst_ref=dst,
                                            send_sem=send_sems.at[t, k], recv_sem=recv_sems.at[t, k],
                                            device_id=to, device_id_type=MESH)

    def mine(t):
        return pltpu.make_async_copy(srcs[t], outs[t].at[slot(*me)], local_sems.at[t])

    def first(t):
        return [copy(t, 0, slot(*me), sibling, src=srcs[t])] + [
            copy(t, 1 + j, slot(*me), (px, py, c), src=srcs[t]) for j, (px, py) in enumerate(chips)]

    def passed(t, j):
        px, py = chips[j]
        return copy(t, 4 + j, slot(px, py, c), sibling)

    def start():
        for t in range(n):
            mine(t).start()
        for t in range(n):
            for cp in first(t):
                cp.start()

    def forward():
        for j, (px, py) in enumerate(chips):
            for t in range(n):
                copy(t, 1 + j, slot(px, py, c), me).wait_recv()
                passed(t, j).start()

    def finish():
        for t in range(n):
            copy(t, 0, slot(x, y, 1 - c), me).wait_recv()
            for j, (px, py) in enumerate(chips):
                copy(t, 4 + j, slot(px, py, 1 - c), me).wait_recv()
        for t in range(n):
            for cp in first(t) + [passed(t, j) for j in range(len(chips))]:
                cp.wait_send()
            mine(t).wait()

    return start, forward, finish


def _gather_scratch(n):
    return [pltpu.SemaphoreType.DMA((n, 7)), pltpu.SemaphoreType.DMA((n, 7)), pltpu.SemaphoreType.DMA((n,))]


def _allgather(name, items):
    n = len(items)

    def body(*refs):
        start, forward, finish = _gather_phases(refs[:n], refs[n:2 * n], *refs[2 * n:])
        start()
        forward()
        finish()

    return pl.pallas_call(
        body, name=name, in_specs=[_ANY] * n, out_specs=[_ANY] * n,
        out_shape=[jax.ShapeDtypeStruct((N_DEV,) + a.shape, a.dtype) for a in items],
        scratch_shapes=_gather_scratch(n), compiler_params=_params())(*items)


def _rs_pair_send(name, gs):
    n = len(gs)

    def body(*refs):
        srcs = refs[:n]
        bufs = refs[n:2 * n]
        send_sems, recv_sems = refs[2 * n:]
        x, y, c = _mesh_pos()
        copies = []
        for t in range(n):
            for k in range(N_CHIP):
                copies.append(pltpu.make_async_remote_copy(
                    src_ref=srcs[t].at[k, 1 - c], dst_ref=bufs[t].at[k], send_sem=send_sems.at[t, k],
                    recv_sem=recv_sems.at[t, k], device_id=(x, y, 1 - c), device_id_type=MESH))
        for cp in copies:
            cp.start()
        for cp in copies:
            cp.wait()

    return pl.pallas_call(
        body, name=name, in_specs=[_ANY] * n, out_specs=[_ANY] * n,
        out_shape=[jax.ShapeDtypeStruct((N_CHIP,) + g.shape[2:], g.dtype) for g in gs],
        scratch_shapes=[pltpu.SemaphoreType.DMA((n, N_CHIP)), pltpu.SemaphoreType.DMA((n, N_CHIP))],
        compiler_params=_params())(*gs)


def _rs_pair_add(name, g, buf, cvec):
    _, _, r, cc = g.shape
    tr = _row_tile(r, cc)

    def kern(c_ref, g_ref, b_ref, o_ref):
        o_ref[...] = (g_ref[...] + b_ref[...]).astype(BF16)

    grid_spec = pltpu.PrefetchScalarGridSpec(
        num_scalar_prefetch=1, grid=(N_CHIP, r // tr),
        in_specs=[pl.BlockSpec((None, None, tr, cc), lambda k, i, c_ref: (k, c_ref[0], i, 0)),
                  pl.BlockSpec((None, tr, cc), lambda k, i, c_ref: (k, i, 0))],
        out_specs=pl.BlockSpec((None, tr, cc), lambda k, i, c_ref: (k, i, 0)))
    return pl.pallas_call(kern, name=name, grid_spec=grid_spec,
                          out_shape=jax.ShapeDtypeStruct((N_CHIP, r, cc), BF16),
                          compiler_params=_params())(cvec, g, buf)


def _exchange_phases(srcs, bufs, send_sems, recv_sems, local_sems):
    n = len(srcs)
    x, y, c = _mesh_pos()
    kme = 2 * x + y
    chips = [(1 - x, y), (x, 1 - y), (1 - x, 1 - y)]

    def copies():
        local = [pltpu.make_async_copy(srcs[t].at[kme], bufs[t].at[kme], local_sems.at[t]) for t in range(n)]
        remote = []
        for t in range(n):
            for j, (px, py) in enumerate(chips):
                remote.append(pltpu.make_async_remote_copy(
                    src_ref=srcs[t].at[2 * px + py], dst_ref=bufs[t].at[kme], send_sem=send_sems.at[t, j],
                    recv_sem=recv_sems.at[t, j], device_id=(px, py, c), device_id_type=MESH))
        return local, remote

    def start():
        local, remote = copies()
        for cp in local + remote:
            cp.start()

    def finish():
        local, remote = copies()
        for cp in remote + local:
            cp.wait()

    return start, finish


def _exchange_scratch(n):
    return [pltpu.SemaphoreType.DMA((n, 3)), pltpu.SemaphoreType.DMA((n, 3)), pltpu.SemaphoreType.DMA((n,))]


def _rs_chip_exchange(name, ss):
    n = len(ss)

    def body(*refs):
        start, finish = _exchange_phases(refs[:n], refs[n:2 * n], *refs[2 * n:])
        start()
        finish()

    return pl.pallas_call(
        body, name=name, in_specs=[_ANY] * n, out_specs=[_ANY] * n,
        out_shape=[jax.ShapeDtypeStruct(s.shape, s.dtype) for s in ss],
        scratch_shapes=_exchange_scratch(n), compiler_params=_params())(*ss)


def _rs_pair_share(name, dsts):
    nd = len(dsts)

    def body(*refs):
        ins = refs[:nd]
        outs = refs[nd:2 * nd]
        send_sems, recv_sems = refs[2 * nd:]
        x, y, c = _mesh_pos()
        copies = []
        for d in range(nd):
            for l in range(dsts[d].shape[0]):
                copies.append(pltpu.make_async_remote_copy(
                    src_ref=ins[d].at[l, c], dst_ref=outs[d].at[l, c], send_sem=send_sems.at[d, l],
                    recv_sem=recv_sems.at[d, l], device_id=(x, y, 1 - c), device_id_type=MESH))
        for cp in copies:
            cp.start()
        for cp in copies:
            cp.wait()

    return pl.pallas_call(
        body, name=name, in_specs=[_ANY] * nd, out_specs=[_ANY] * nd,
        out_shape=[jax.ShapeDtypeStruct(a.shape, a.dtype) for a in dsts],
        input_output_aliases={d: d for d in range(nd)},
        scratch_shapes=[pltpu.SemaphoreType.DMA((nd, 2)), pltpu.SemaphoreType.DMA((nd, 2))],
        compiler_params=_params())(*dsts)


def _rope_tables(seq, n_ctx):
    t = jnp.arange(seq)
    row = (t // GRID_W).astype(F32)
    col = (t % GRID_W).astype(F32)
    inv = ROPE_THETA ** (-jnp.arange(0, HEAD_DIM // 2, 2, dtype=F32) / (HEAD_DIM // 2))
    ang_r = row[:, None] * inv[None, :]
    ang_c = col[:, None] * inv[None, :]
    cos = jnp.concatenate([jnp.cos(ang_r)] * 2 + [jnp.cos(ang_c)] * 2, axis=1)
    sin = jnp.concatenate([-jnp.sin(ang_r), jnp.sin(ang_r), -jnp.sin(ang_c), jnp.sin(ang_c)], axis=1)
    cos = jnp.concatenate([jnp.ones((n_ctx, HEAD_DIM), F32), cos], axis=0)
    sin = jnp.concatenate([jnp.zeros((n_ctx, HEAD_DIM), F32), sin], axis=0)
    return jnp.tile(cos, (1, N_KV_HEADS)), jnp.tile(sin, (1, N_KV_HEADS))


def _to_heads(a, nh):
    return a.reshape(a.shape[0], nh, HEAD_DIM).transpose(1, 0, 2)


def _from_heads(a):
    return a.transpose(1, 0, 2).reshape(a.shape[1], a.shape[0] * HEAD_DIM)


def _sample_step(x, ctx, tgt, mod, cmod, W, pending, cvec):
    W = dict(W)
    seq = x.shape[0]
    n_ctx = ctx.shape[0]
    tm = min(256, seq)
    tmc = min(256, n_ctx)
    tmm = 512 if seq % 512 == 0 else tm
    tmw = 1024 if seq % 1024 == 0 else tmm
    tmr = (n_ctx + seq) // 4 if (n_ctx + seq) % 64 == 0 else tm
    tq = min(256, seq)
    assert n_ctx % tm == 0 and seq % tm == 0

    def mv(l, j):
        return mod[l, j * D_MODEL:(j + 1) * D_MODEL].reshape(1, D_MODEL)

    csh1 = cmod[:D_MODEL].reshape(1, D_MODEL)
    csc1 = cmod[D_MODEL:2 * D_MODEL].reshape(1, D_MODEL)
    g_mix = [W['g_mix'][l:l + 1] for l in range(2)]
    g_ffn = [W['g_ffn'][l:l + 1] for l in range(2)]
    cos, sin = _rope_tables(seq, n_ctx)
    gq = jnp.tile(W['q_gain'], (1, N_Q_HEADS))
    gk = jnp.tile(W['k_gain'], (1, N_KV_HEADS))
    bdq = jnp.kron(jnp.eye(N_Q_HEADS, dtype=F32), jnp.ones((HEAD_DIM, HEAD_DIM), F32)).astype(BF16)
    bdk = bdq[:KV_W, :KV_W]
    w_sp = W['w_sp'].astype(BF16)
    w_sp_t = w_sp.transpose(0, 2, 1)
    bsp = jnp.broadcast_to(W['b_sp'].T[:, :, None], (CHUNK, N_SG, CHUNK)).reshape(CHUNK, SG_W)

    def ffn_fwd(l, h_mid):
        xf = _modnorm_fwd(f'l{l}_ffn_norm', h_mid, g_ffn[l], mv(l, 4), mv(l, 3), tm)
        gu, a = _ffn_in_swiglu(f'l{l}_ffn_in', xf, W['ffn_in'][l], tmm)
        f = _mm_rows(f'l{l}_ffn_out', a, W['ffn_out'][l], tmm)
        h_out = _resgate_fwd(f'l{l}_ffn_res', h_mid, f, mv(l, 5), tm)
        return h_out, (h_mid, xf, gu, a, f)

    def ffn_bwd(l, dh_out, saved):
        h_mid, xf, gu, a, f = saved
        dy, d_gt2, _ = _resgate_bwd(f'l{l}_ffn_res_bwd', dh_out, f, mv(l, 5), tm)
        dgu = _ffn_out_dx_swiglu(f'l{l}_ffn_out_dx', dy, W['ffn_out'][l], gu, tmm)
        d_wo = _mm_wgrad(f'l{l}_ffn_out_dw', a, dy, tmw)
        dxf = _ffn_in_dx(f'l{l}_ffn_in_dx', dgu, W['ffn_in'][l], tm)
        d_wi = _mm_halves_wgrad_blk(f'l{l}_ffn_in_dw', xf, dgu, tmw)
        dh_mid, d_sh2, d_sc2, d_g = _modnorm_bwd(f'l{l}_ffn_norm_bwd', h_mid, [_ri(dxf)], dh_out, g_ffn[l],
                                                 mv(l, 4), tm)
        return dh_mid, d_wi, d_wo, d_g, (d_sh2, d_sc2, d_gt2)

    xm0 = _modnorm_fwd('l0_mix_norm', x, g_mix[0], mv(0, 1), mv(0, 0), tm)
    xc0 = _modnorm_fwd('l0_ctx_norm', ctx, g_mix[0], csc1, csh1, tmc)
    xall = jnp.concatenate([xc0, xm0], axis=0)
    p_qkv = _mm_rows('l0_qkv', xall, W['w_qkv'], tmr)
    p_sg = _mm_rows('l0_sg_in', xm0, W['w_sgp'], tmw)
    q_all, k_all, v_all = _qk_prep_fwd('l0_qk_prep', p_qkv, cos, sin, gq, gk, bdq, bdk, tm)
    kh = _to_heads(k_all, N_KV_HEADS)
    vh = _to_heads(v_all, N_KV_HEADS)
    o, lse, *gw = _attn_fwd('l0_attn', q_all, kh, vh, n_ctx, tq, gather=pending)
    W['w_out'] = gw[0].reshape(D_MODEL, D_MODEL)
    W['ffn_in'] = [gw[1].reshape(N_CHIP, D_MODEL, FF_BLK), gw[5].reshape(N_CHIP, D_MODEL, FF_BLK)]
    W['ffn_out'] = [gw[2].reshape(D_FF, D_MODEL), gw[6].reshape(D_FF, D_MODEL)]
    W['pw1'] = gw[3].reshape(N_CHIP, D_MODEL, PW1_BLK).transpose(1, 0, 2).reshape(D_MODEL, 2 * D_MODEL)
    W['pw2'] = gw[4].reshape(D_MODEL, D_MODEL)
    sg = _sg_fwd('l0_sg', p_sg, w_sp, bsp, tm)
    cat = jnp.concatenate([o.astype(BF16), sg], axis=1)
    y0 = _mm_rows('l0_out', cat, W['w_out'], tmw)
    h1 = _resgate_fwd('l0_mix_res', x, y0, mv(0, 2), tm)
    h2, ffn0 = ffn_fwd(0, h1)

    xm1 = _modnorm_fwd('l1_mix_norm', h2, g_mix[1], mv(1, 1), mv(1, 0), tm)
    ag = _mm_rows('l1_pw1', xm1, W['pw1'], tmm)
    hg = _glu_fwd('l1_glu', ag, W['b_pw1'], tm)
    hc = _dwconv('l1_conv', hg, W['w_dw'], W['b_dw'])
    s1 = _lnsilu_fwd('l1_ln_silu', hc, W['ln_g'], W['ln_b'], tm)
    y1 = _mm_rows('l1_pw2', s1, W['pw2'], tmw, bias=W['b_pw2'])
    h3 = _resgate_fwd('l1_mix_res', h2, y1, mv(1, 2), tm)
    h4, ffn1 = ffn_fwd(1, h3)

    dh4, d_g_final, loss_vec = _final_loss('final_loss', h4, tgt, W['g_final'], tm)

    dh3, d_wi1, d_wo1, d_gffn1, dmod1_ffn = ffn_bwd(1, dh4, ffn1)
    dy1, d_gt1_1, d_b_pw2 = _resgate_bwd('l1_mix_res_bwd', dh3, y1, mv(1, 2), tm)
    ds1 = _mm_rows('l1_pw2_dx', dy1, W['pw2'], tmw, trans_b=True)
    d_pw2 = _mm_wgrad('l1_pw2_dw', s1, dy1, tmw)
    dhc, d_ln_g, d_ln_b = _lnsilu_bwd('l1_ln_silu_bwd', hc, ds1, W['ln_g'], W['ln_b'], tm)
    dhg = _dwconv('l1_conv_dx', dhc, W['w_dw'][::-1], jnp.zeros((1, D_MODEL), F32))
    d_w_dw, d_b_dw = _dwconv_wgrad('l1_conv_dw', hg, dhc)
    dag, d_b_pw1 = _glu_bwd('l1_glu_bwd', ag, dhg, W['b_pw1'], tm)
    dxm1 = _mm_rows('l1_pw1_dx', dag, W['pw1'], tmm, trans_b=True)
    d_pw1 = _mm_wgrad('l1_pw1_dw', xm1, dag, tmw).reshape(D_MODEL, N_CHIP, PW1_BLK).transpose(1, 0, 2)
    dh2, d_sh1_1, d_sc1_1, d_gmix1 = _modnorm_bwd('l1_mix_norm_bwd', h2, [_ri(dxm1)], dh3, g_mix[1], mv(1, 1), tm)

    dh1, d_wi0, d_wo0, d_gffn0, dmod0_ffn = ffn_bwd(0, dh2, ffn0)
    dy0, d_gt1_0, _ = _resgate_bwd('l0_mix_res_bwd', dh1, y0, mv(0, 2), tm)
    dcat = _mm_rows('l0_out_dx', dy0, W['w_out'], tmw, trans_b=True)
    d_w_out = _mm_wgrad('l0_out_dw', cat, dy0, tmw)

    def blocked(a):
        cols = a.shape[-1]
        return a.reshape(N_CHIP, 2, a.size // (2 * N_CHIP * cols), cols)

    gs = [blocked(g) for g in (d_wi0, d_wi1, d_wo0, d_wo1, d_w_out, d_pw1, d_pw2)]
    sib = _rs_pair_send('rs_pair_send', gs)
    ss = [_rs_pair_add(f'rs_pair_add{t}', gs[t], sib[t], cvec) for t in range(len(gs))]
    dq, dk, dv, *xs = _attn_bwd('l0_attn_bwd', q_all, kh, vh, o, lse, dcat, n_ctx, tq, exchange=ss)
    dp_qkv, d_gq, d_gk = _qk_prep_bwd('l0_qk_prep_bwd', p_qkv, cos, sin, dq, n_ctx // tm, _from_heads(dk),
                                      _from_heads(dv), gq, gk, bdq, bdk, tm)
    dp_sg, d_w_sp, d_bsp = _sg_bwd('l0_sg_bwd', p_sg, dcat, w_sp, w_sp_t, bsp, tm)
    d_w_qkv = _mm_wgrad('l0_qkv_dw', xall, dp_qkv, tmr)
    d_w_sgp = _mm_wgrad('l0_sg_in_dw', xm0, dp_sg, tmw)
    dxall = _mm_rows('l0_qkv_dx', dp_qkv, W['w_qkv'], tmr, trans_b=True)
    dxm_sg = _mm_rows('l0_sg_in_dx', dp_sg, W['w_sgp'], tmw, trans_b=True)
    dx, d_sh1_0, d_sc1_0, d_gmix0 = _modnorm_bwd('l0_mix_norm_bwd', x, [_ri(dxall, None, 0, n_ctx // tm), _ri(dxm_sg)],
                                                 dh1, g_mix[0], mv(0, 1), tm)
    _, d_csh1, d_csc1, d_gmix0c = _modnorm_bwd('l0_ctx_norm_bwd', ctx, [_ri(dxall)], None, g_mix[0], csc1, tmc)

    g_in = blocked(jnp.concatenate([d_w_qkv, d_w_sgp], axis=1).reshape(D_MODEL, N_CHIP, -1).transpose(1, 0, 2))
    sib_in = _rs_pair_send('rs_pair_send_w_in', [g_in])
    xs_in = _rs_chip_exchange('rs_chip_exchange_w_in', [_rs_pair_add('rs_pair_add_w_in', g_in, sib_in[0], cvec)])
    t_ffn_in = _sum_into('rs_sum_ffn_in1', xs[1], cvec, 2, 1, prev=_sum_into('rs_sum_ffn_in0', xs[0], cvec, 2, 0))
    t_ffn_out = _sum_into('rs_sum_ffn_out1', xs[3], cvec, 2, 1, prev=_sum_into('rs_sum_ffn_out0', xs[2], cvec, 2, 0))
    reduced = _rs_pair_share('rs_pair_share', [
        t_ffn_in, t_ffn_out, _sum_into('rs_sum_w_in', xs_in[0], cvec, 1, 0), _sum_into('rs_sum_w_out', xs[4], cvec, 1, 0),
        _sum_into('rs_sum_pw1', xs[5], cvec, 1, 0), _sum_into('rs_sum_pw2', xs[6], cvec, 1, 0)])
    big = dict(zip(['w_ffn_in', 'w_ffn_out', 'w_in', 'w_out', 'w_pw1', 'w_pw2'], reduced))

    zero = jnp.zeros((1, D_MODEL), F32)
    dmod = jnp.concatenate([d_sh1_0, d_sc1_0, d_gt1_0, *dmod0_ffn, d_sh1_1, d_sc1_1, d_gt1_1, *dmod1_ffn], axis=0)
    dcmod = jnp.concatenate([d_csh1, d_csc1, zero, zero, zero, zero], axis=0)
    grads = dict(
        g_mix=jnp.concatenate([d_gmix0 + d_gmix0c, d_gmix1], axis=0),
        g_ffn=jnp.concatenate([d_gffn0, d_gffn1], axis=0), g_final=d_g_final, gq=d_gq, gk=d_gk,
        w_sp=d_w_sp, bsp=d_bsp, b_pw1=d_b_pw1, w_dw=d_w_dw, b_dw=d_b_dw, ln_g=d_ln_g, ln_b=d_ln_b, b_pw2=d_b_pw2,
        dmod=dmod, dcmod=dcmod)
    return loss_vec, dx, grads, big


def kernel(x, c, ctx, c_ctx, w_mod, b_mod, g_mix, g_ffn, w_ffn_in, w_ffn_out, w_in, q_gain, k_gain, w_sp, b_sp, w_out, w_pw1, b_pw1, w_dw, b_dw, ln_g, ln_b, w_pw2, b_pw2, g_final, loss_target, m_c_ctx, m_w_mod, m_b_mod, m_g_mix, m_g_ffn, m_w_ffn_in, m_w_ffn_out, m_w_in, m_q_gain, m_k_gain, m_w_sp, m_b_sp, m_w_out, m_w_pw1, m_b_pw1, m_w_dw, m_b_dw, m_ln_g, m_ln_b, m_w_pw2, m_b_pw2, m_g_final, v_c_ctx, v_w_mod, v_b_mod, v_g_mix, v_g_ffn, v_w_ffn_in, v_w_ffn_out, v_w_in, v_q_gain, v_k_gain, v_w_sp, v_b_sp, v_w_out, v_w_pw1, v_b_pw1, v_w_dw, v_b_dw, v_ln_g, v_ln_b, v_w_pw2, v_b_pw2, v_g_final):
    given = dict(locals())
    ix, iy, ic = _mesh_pos()
    chip = 2 * ix + iy
    me = 2 * chip + ic
    d = D_MODEL
    q4 = d // N_CHIP

    small = jnp.concatenate([c.reshape(4, q4), b_pw1.reshape(2, q4), w_dw[0], b_dw, ln_g, ln_b, b_pw2,
                             jnp.zeros((7, q4), F32)], axis=0)
    def my_half(a):
        r = a.shape[0] // 2
        return lax.dynamic_slice_in_dim(a, ic * r, r, axis=0).astype(BF16)

    sgath, g_w_in = _allgather('ag_first', [small, my_half(w_in[0])])
    c_all = sgath[:, 0:4].reshape(N_DEV, d)
    per_chip = sgath[0::2]
    b_pw1_f = per_chip[:, 4:6].reshape(1, 2 * d)
    w_dw_f = per_chip[:, 6:6 + CONV_W].transpose(1, 0, 2).reshape(CONV_W, d)
    b_dw_f, ln_g_f, ln_b_f, b_pw2_f = [per_chip[:, 37 + i].reshape(1, d) for i in range(4)]

    c16 = jnp.concatenate([c_all, c_ctx[None, :], jnp.zeros((7, d), F32)], axis=0)
    bm_sh = lax.dynamic_slice_in_dim(b_mod, chip * MOD_BLK, MOD_BLK, axis=1).reshape(2, 1, MOD_BLK)
    mod_piece = _mod_fwd('mod_fwd', c16, w_mod, bm_sh)
    mgath = _allgather('ag_mod', [mod_piece.reshape(32, MOD_BLK)])[0]
    mod_all = mgath[0::2].reshape(N_CHIP, 2, 16, MOD_BLK).transpose(1, 2, 0, 3).reshape(2, 16, MOD_W)
    mod_me = lax.dynamic_index_in_dim(mod_all, me, axis=1, keepdims=False)
    cmod = mod_all[0, N_DEV]

    w_in_f = g_w_in.reshape(N_CHIP, d, -1).transpose(1, 0, 2).reshape(d, -1)
    W = dict(
        w_qkv=w_in_f[:, :QKV_W], w_sgp=w_in_f[:, QKV_W:],
        g_mix=g_mix, g_ffn=g_ffn, g_final=g_final.reshape(1, d), q_gain=q_gain, k_gain=k_gain, w_sp=w_sp[0],
        b_sp=b_sp[0], b_pw1=b_pw1_f, w_dw=w_dw_f, b_dw=b_dw_f, ln_g=ln_g_f, ln_b=ln_b_f, b_pw2=b_pw2_f)
    pending = [my_half(w_out[0]), my_half(w_ffn_in[0]), my_half(w_ffn_out[0]), my_half(w_pw1[0]), my_half(w_pw2[0]),
               my_half(w_ffn_in[1]), my_half(w_ffn_out[1])]
    cvec = jnp.reshape(ic, (1,)).astype(jnp.int32)

    loss_vec, dx, G, big = _sample_step(x[0], ctx[0], loss_target[0], mod_me, cmod, W, pending, cvec)
    loss = lax.psum(jnp.sum(loss_vec), ("x", "y", "c"))

    pack = jnp.concatenate([
        G['g_mix'], G['g_ffn'], G['g_final'],
        jnp.concatenate([G['gq'], G['gk'], jnp.zeros((1, d - ATTN_W - KV_W), F32)], axis=1),
        G['b_pw1'].reshape(2, d), G['w_dw'], G['b_dw'], G['ln_g'], G['ln_b'], G['b_pw2'],
        G['dmod'], G['dcmod'], jnp.zeros((3, d), F32),
        G['w_sp'].reshape(64, d), G['bsp'].reshape(64, d)], axis=0)
    assert pack.shape == (192, d)
    pg = _allgather('ag_small_grads', [pack])[0]
    s8 = _sum_slots('sum_small_grads', pg, 64)

    def chip_cols(a, width):
        return lax.dynamic_slice_in_dim(a, chip * width, width, axis=a.ndim - 1)

    grads = {}
    grads['g_mix'] = s8[0:2]
    grads['g_ffn'] = s8[2:4]
    grads['g_final'] = s8[4]
    grads['q_gain'] = s8[5, :ATTN_W].reshape(N_Q_HEADS, HEAD_DIM).sum(axis=0)[None, :]
    grads['k_gain'] = s8[5, ATTN_W:ATTN_W + KV_W].reshape(N_KV_HEADS, HEAD_DIM).sum(axis=0)[None, :]
    grads['b_pw1'] = chip_cols(s8[6:8].reshape(1, 2 * d), 2 * q4)
    grads['w_dw'] = chip_cols(s8[8:8 + CONV_W], q4)[None]
    grads['b_dw'] = chip_cols(s8[39:40], q4)
    grads['ln_g'] = chip_cols(s8[40:41], q4)
    grads['ln_b'] = chip_cols(s8[41:42], q4)
    grads['b_pw2'] = chip_cols(s8[42:43], q4)
    grads['w_sp'] = s8[64:128].reshape(1, N_SG, CHUNK, CHUNK)
    grads['b_sp'] = s8[128:192].reshape(CHUNK, N_SG, CHUNK).sum(axis=-1).T[None]

    dm_rows = pg[:, 43:55].reshape(N_DEV, 2, MOD_W)
    ctx_row = s8[55:61].reshape(1, MOD_W)
    dm = jnp.stack([jnp.concatenate([dm_rows[:, 0], ctx_row, jnp.zeros((7, MOD_W), F32)], axis=0),
                    jnp.concatenate([dm_rows[:, 1], jnp.zeros((8, MOD_W), F32)], axis=0)], axis=0)
    grads['b_mod'] = jnp.stack([s8[43:49].reshape(MOD_W) + ctx_row[0], s8[49:55].reshape(MOD_W)], axis=0)
    dm_sh = chip_cols(dm, MOD_BLK)
    grads['w_mod'] = _mod_wgrad('mod_dw', c16, dm_sh)
    d_silu = _mm_rows('mod_dx', dm_sh[0], w_mod[0], 16, trans_b=True)
    parts = _allgather('ag_cctx', [d_silu[8:16]])[0]
    grads['c_ctx'] = _cctx_grad('cctx_grad', parts, c_ctx.reshape(1, d))[0]

    grads.update(big)

    deltas, new_m, new_v = {}, {}, {}
    for n in WEIGHT_NAMES:
        grads[n] = grads[n].reshape(given[n].shape)
        deltas[n], new_m[n], new_v[n] = _adamw(f'adamw_{n}', given[n], grads[n], given['m_' + n], given['v_' + n])

    return (loss, dx[None], *[grads[n] for n in WEIGHT_NAMES], *[deltas[n] for n in WEIGHT_NAMES],
            *[new_m[n] for n in WEIGHT_NAMES], *[new_v[n] for n in WEIGHT_NAMES])
```

```python
import functools
import math

import jax
import jax.numpy as jnp
from jax import lax
from jax.experimental import pallas as pl
from jax.experimental.pallas import tpu as pltpu

F32 = jnp.float32
BF16 = jnp.bfloat16
MESH = pl.DeviceIdType.MESH

D_MODEL = 1024
HEAD_DIM = 64
N_Q_HEADS = 8
N_KV_HEADS = 2
Q_GROUP = N_Q_HEADS // N_KV_HEADS
ATTN_W = N_Q_HEADS * HEAD_DIM
KV_W = N_KV_HEADS * HEAD_DIM
QKV_W = ATTN_W + 2 * KV_W
SG_W = D_MODEL - ATTN_W
N_SG = 4
CHUNK = 128
GRID_W = 64
ROPE_THETA = 10000.0
CONV_W = 31
CONV_HALO = 16
D_FF = 2816
MOD_W = 6 * D_MODEL
N_CHIP = 4
N_DEV = 8
FF_BLK = 2 * D_FF // N_CHIP
PW1_BLK = 2 * D_MODEL // N_CHIP
MOD_BLK = MOD_W // N_CHIP
EPS = 1e-6
GELU_C = math.sqrt(2.0 / math.pi)
GELU_A = 0.044715

ADAM_LR = 0.001
ADAM_B1 = 0.9
ADAM_B2 = 0.999
ADAM_EPS = 1e-08
ADAM_WD = 0.01
ADAM_STEP = 10

VMEM_LIMIT_BYTES = 56 * 1024 * 1024

WEIGHT_NAMES = ['c_ctx', 'w_mod', 'b_mod', 'g_mix', 'g_ffn', 'w_ffn_in', 'w_ffn_out', 'w_in', 'q_gain', 'k_gain',
                'w_sp', 'b_sp', 'w_out', 'w_pw1', 'b_pw1', 'w_dw', 'b_dw', 'ln_g', 'ln_b', 'w_pw2', 'b_pw2',
                'g_final']


def _params():
    return pltpu.CompilerParams(vmem_limit_bytes=VMEM_LIMIT_BYTES)


def _ri(arr, width=None, col_block=0, row_off=0, row_fn=None):
    return (arr, arr.shape[1] if width is None else width, col_block,
            (lambda i: i + row_off) if row_fn is None else row_fn)


def _rowwise(name, body, n_tiles, tm, row_ins, vec_ins, row_outs, vec_outs):
    nri, nvi, nro = len(row_ins), len(vec_ins), len(row_outs)

    def kern(*refs):
        ri = refs[:nri]
        vi = refs[nri:nri + nvi]
        ro = refs[nri + nvi:nri + nvi + nro]
        vo = refs[nri + nvi + nro:]
        if vo:
            @pl.when(pl.program_id(0) == 0)
            def _():
                for r in vo:
                    r[...] = jnp.zeros(r.shape, r.dtype)
        body(ri, vi, ro, vo)

    in_specs = [pl.BlockSpec((tm, bw), lambda i, cb=cb, rf=rf: (rf(i), cb)) for (_, bw, cb, rf) in row_ins]
    in_specs += [pl.BlockSpec(a.shape, lambda i, nd=a.ndim: (0,) * nd) for a in vec_ins]
    out_specs = [pl.BlockSpec((tm, w), lambda i: (i, 0)) for (w, _) in row_outs]
    out_specs += [pl.BlockSpec(s, lambda i, nd=len(s): (0,) * nd) for s in vec_outs]
    out_shape = [jax.ShapeDtypeStruct((n_tiles * tm, w), dt) for (w, dt) in row_outs]
    out_shape += [jax.ShapeDtypeStruct(s, F32) for s in vec_outs]
    return pl.pallas_call(kern, name=name, grid=(n_tiles,), in_specs=in_specs, out_specs=out_specs,
                          out_shape=out_shape, compiler_params=_params())(*[a for (a, _, _, _) in row_ins], *vec_ins)


def _colsum(v):
    return jnp.sum(v, axis=0, keepdims=True)


def _rowmean(v):
    return jnp.mean(v, axis=-1, keepdims=True)


def _rms(h):
    r = lax.rsqrt(_rowmean(h * h) + EPS)
    return h * r, r


def _rms_bwd(n, r, gy):
    return r * (gy - n * _rowmean(gy * n))


def _ln_stats(v):
    xc = v - _rowmean(v)
    r = lax.rsqrt(_rowmean(xc * xc) + EPS)
    return xc * r, r


def _ln_bwd(xh, r, dxh):
    return r * (dxh - _rowmean(dxh) - xh * _rowmean(dxh * xh))


def _sigmoid(v):
    return 1.0 / (1.0 + jnp.exp(-v))


def _gelu(v):
    t = jnp.tanh(GELU_C * (v + GELU_A * (v * v * v)))
    return 0.5 * v * (1.0 + t), t


def _gelu_grad(v, t):
    return 0.5 * (1.0 + t) + 0.5 * v * (1.0 - t * t) * (GELU_C * (1.0 + 3.0 * GELU_A * (v * v)))


def _modnorm_fwd(name, h, g, sc, sh, tm):
    def body(ri, vi, ro, vo):
        n, _ = _rms(ri[0][...])
        ro[0][...] = ((n * vi[0][...]) * (1.0 + vi[1][...]) + vi[2][...]).astype(BF16)

    return _rowwise(name, body, h.shape[0] // tm, tm, [_ri(h)], [g, sc, sh], [(D_MODEL, BF16)], [])[0]


def _modnorm_bwd(name, h, dxs, dh_in, g, sc, tm):
    ndx = len(dxs)

    def body(ri, vi, ro, vo):
        dxm = ri[1][...].astype(F32)
        for r in ri[2:1 + ndx]:
            dxm = dxm + r[...].astype(F32)
        gv = vi[0][...]
        n, r = _rms(ri[0][...])
        vo[0][...] += _colsum(dxm)
        vo[1][...] += _colsum(dxm * (n * gv))
        dy = dxm * (1.0 + vi[1][...])
        vo[2][...] += _colsum(dy * n)
        if dh_in is not None:
            ro[0][...] = ri[1 + ndx][...] + _rms_bwd(n, r, dy * gv)

    row_ins = [_ri(h)] + list(dxs) + ([_ri(dh_in)] if dh_in is not None else [])
    row_outs = [(D_MODEL, F32)] if dh_in is not None else []
    outs = _rowwise(name, body, h.shape[0] // tm, tm, row_ins, [g, sc], row_outs, [(1, D_MODEL)] * 3)
    if dh_in is None:
        return (None, *outs)
    return tuple(outs)


def _resgate_bwd(name, dh, y, gt, tm):
    def body(ri, vi, ro, vo):
        d = ri[0][...]
        dy = d * vi[0][...]
        ro[0][...] = dy.astype(BF16)
        vo[0][...] += _colsum(d * ri[1][...])
        vo[1][...] += _colsum(dy)

    return _rowwise(name, body, dh.shape[0] // tm, tm, [_ri(dh), _ri(y)], [gt], [(D_MODEL, BF16)],
                    [(1, D_MODEL)] * 2)


def _final_loss(name, h, tgt, g, tm):
    def body(ri, vi, ro, vo):
        gv = vi[0][...]
        n, r = _rms(ri[0][...])
        e = n * gv - ri[1][...]
        vo[1][...] += _colsum(e * e) * (0.5 / D_MODEL)
        dout = e * (1.0 / D_MODEL)
        vo[0][...] += _colsum(dout * n)
        ro[0][...] = _rms_bwd(n, r, dout * gv)

    return _rowwise(name, body, h.shape[0] // tm, tm, [_ri(h), _ri(tgt)], [g], [(D_MODEL, F32)],
                    [(1, D_MODEL)] * 2)


def _head_mean(v, bd):
    hi = v.astype(BF16)
    lo = (v - hi.astype(F32)).astype(BF16)
    s = jnp.dot(hi, bd, preferred_element_type=F32) + jnp.dot(lo, bd, preferred_element_type=F32)
    return s * (1.0 / HEAD_DIM)


def _swap16(v):
    w = v.shape[1]
    lane = lax.broadcasted_iota(jnp.int32, v.shape, 1)
    return jnp.where((lane & 16) == 0, pltpu.roll(v, w - 16, 1), pltpu.roll(v, 16, 1))


def _rope(v, cos, sin):
    return v * cos + _swap16(v) * sin


def _rope_bwd(d, cos, sin):
    return d * cos + _swap16(d * sin)


def _q_wide(t):
    return jnp.concatenate([t] * (ATTN_W // KV_W), axis=1)


def _qk_prep_fwd(name, p_qkv, cos, sin, gq, gk, bdq, bdk, tm):
    def body(ri, vi, ro, vo):
        pq = ri[0][...]
        pkv = ri[1][...]
        cosv = ri[2][...]
        sinv = ri[3][...]
        rq = lax.rsqrt(_head_mean(pq * pq, vi[2][...]) + EPS)
        ro[0][...] = (_rope(pq * rq * vi[0][...], _q_wide(cosv), _q_wide(sinv)) * (HEAD_DIM ** -0.5)).astype(BF16)
        pk = pkv[:, :KV_W]
        rk = lax.rsqrt(_head_mean(pk * pk, vi[3][...]) + EPS)
        ro[1][...] = _rope(pk * rk * vi[1][...], cosv, sinv).astype(BF16)
        ro[2][...] = pkv[:, KV_W:].astype(BF16)

    return _rowwise(name, body, p_qkv.shape[0] // tm, tm,
                    [_ri(p_qkv, ATTN_W, 0), _ri(p_qkv, 2 * KV_W, 2), _ri(cos), _ri(sin)], [gq, gk, bdq, bdk],
                    [(ATTN_W, BF16), (KV_W, BF16), (KV_W, BF16)], [])


def _qk_prep_bwd(name, p_qkv, cos, sin, dq, n_ctx_tiles, dk, dv, gq, gk, bdq, bdk, tm):
    def norm_rope_bwd(p, gain, bd, cosv, sinv, dout):
        r = lax.rsqrt(_head_mean(p * p, bd) + EPS)
        n = p * r
        dqn = _rope_bwd(dout, cosv, sinv)
        gy = dqn * gain
        return r * (gy - n * _head_mean(gy * n, bd)), _colsum(dqn * n)

    def body(ri, vi, ro, vo):
        pkv = ri[1][...]
        cosv = ri[2][...]
        sinv = ri[3][...]
        dqv = jnp.where(pl.program_id(0) >= n_ctx_tiles, ri[4][...] * (HEAD_DIM ** -0.5), 0.0)
        dxq, dgq = norm_rope_bwd(ri[0][...], vi[0][...], vi[2][...], _q_wide(cosv), _q_wide(sinv), dqv)
        dxk, dgk = norm_rope_bwd(pkv[:, :KV_W], vi[1][...], vi[3][...], cosv, sinv, ri[5][...])
        ro[0][:, :ATTN_W] = dxq.astype(BF16)
        ro[0][:, ATTN_W:ATTN_W + KV_W] = dxk.astype(BF16)
        ro[0][:, ATTN_W + KV_W:] = ri[6][...].astype(BF16)
        vo[0][...] += dgq
        vo[1][...] += dgk

    return _rowwise(name, body, p_qkv.shape[0] // tm, tm,
                    [_ri(p_qkv, ATTN_W, 0), _ri(p_qkv, 2 * KV_W, 2), _ri(cos), _ri(sin),
                     _ri(dq, row_fn=lambda i: jnp.maximum(i - n_ctx_tiles, 0)), _ri(dk), _ri(dv)],
                    [gq, gk, bdq, bdk], [(QKV_W, BF16)], [(1, ATTN_W), (1, KV_W)])


def _sg_fwd(name, p_sg, w_sp, bsp, tm):
    def body(ri, vi, ro, vo):
        p = ri[0][...]
        gu, _ = _gelu(p[:, :SG_W])
        gv, _ = _gelu(p[:, SG_W:])
        for g in range(N_SG):
            sl = slice(g * CHUNK, (g + 1) * CHUNK)
            vn, _ = _ln_stats(gv[:, sl])
            vnb = vn.astype(BF16)
            for m in range(tm // CHUNK):
                rs = slice(m * CHUNK, (m + 1) * CHUNK)
                mixed = jnp.dot(vi[0][g], vnb[rs], preferred_element_type=F32) + vi[1][:, sl]
                ro[0][rs, sl] = (gu[rs, sl] * mixed).astype(BF16)

    return _rowwise(name, body, p_sg.shape[0] // tm, tm, [_ri(p_sg)], [w_sp, bsp], [(SG_W, BF16)], [])[0]


def _sg_bwd(name, p_sg, dcat, w_sp, w_sp_t, bsp, tm):
    def body(ri, vi, ro, vo):
        p = ri[0][...]
        dsg = ri[1][...]
        su = p[:, :SG_W]
        sv = p[:, SG_W:]
        gu, tu = _gelu(su)
        gv, tv = _gelu(sv)
        for g in range(N_SG):
            sl = slice(g * CHUNK, (g + 1) * CHUNK)
            vn, r = _ln_stats(gv[:, sl])
            vnb = vn.astype(BF16)
            for m in range(tm // CHUNK):
                rs = slice(m * CHUNK, (m + 1) * CHUNK)
                mixed = jnp.dot(vi[0][g], vnb[rs], preferred_element_type=F32) + vi[2][:, sl]
                d_o = dsg[rs, sl]
                dm = d_o * gu[rs, sl]
                vo[1][:, sl] += dm
                dmb = dm.astype(BF16)
                vo[0][g] += lax.dot_general(dmb, vnb[rs], (((1,), (1,)), ((), ())), preferred_element_type=F32)
                dvn = jnp.dot(vi[1][g], dmb, preferred_element_type=F32)
                dgv = _ln_bwd(vn[rs], r[rs], dvn)
                ro[0][rs, sl] = (d_o * mixed * _gelu_grad(su[rs, sl], tu[rs, sl])).astype(BF16)
                ro[0][rs, SG_W + g * CHUNK:SG_W + (g + 1) * CHUNK] = (
                    dgv * _gelu_grad(sv[rs, sl], tv[rs, sl])).astype(BF16)

    return _rowwise(name, body, p_sg.shape[0] // tm, tm, [_ri(p_sg), _ri(dcat, SG_W, 1)], [w_sp, w_sp_t, bsp],
                    [(2 * SG_W, BF16)], [(N_SG, CHUNK, CHUNK), (CHUNK, SG_W)])


def _glu_fwd(name, ag, b, tm):
    def body(ri, vi, ro, vo):
        v = ri[0][...] + vi[0][...]
        ro[0][...] = v[:, :D_MODEL] * _sigmoid(v[:, D_MODEL:])

    return _rowwise(name, body, ag.shape[0] // tm, tm, [_ri(ag)], [b], [(D_MODEL, F32)], [])[0]


def _glu_bwd(name, ag, dhg, b, tm):
    def body(ri, vi, ro, vo):
        v = ri[0][...] + vi[0][...]
        d = ri[1][...]
        a = v[:, :D_MODEL]
        sg = _sigmoid(v[:, D_MODEL:])
        da = d * sg
        dgate = d * a * sg * (1.0 - sg)
        ro[0][:, :D_MODEL] = da.astype(BF16)
        ro[0][:, D_MODEL:] = dgate.astype(BF16)
        vo[0][:, :D_MODEL] += _colsum(da)
        vo[0][:, D_MODEL:] += _colsum(dgate)

    return _rowwise(name, body, ag.shape[0] // tm, tm, [_ri(ag), _ri(dhg)], [b], [(2 * D_MODEL, BF16)],
                    [(1, 2 * D_MODEL)])


def _lnsilu_fwd(name, hc, g, b, tm):
    def body(ri, vi, ro, vo):
        xh, _ = _ln_stats(ri[0][...])
        ln = xh * vi[0][...] + vi[1][...]
        ro[0][...] = (ln * _sigmoid(ln)).astype(BF16)

    return _rowwise(name, body, hc.shape[0] // tm, tm, [_ri(hc)], [g, b], [(D_MODEL, BF16)], [])[0]


def _lnsilu_bwd(name, hc, ds, g, b, tm):
    def body(ri, vi, ro, vo):
        gv = vi[0][...]
        xh, r = _ln_stats(ri[0][...])
        ln = xh * gv + vi[1][...]
        sg = _sigmoid(ln)
        dln = ri[1][...] * (sg * (1.0 + ln * (1.0 - sg)))
        vo[0][...] += _colsum(dln * xh)
        vo[1][...] += _colsum(dln)
        ro[0][...] = _ln_bwd(xh, r, dln * gv)

    return _rowwise(name, body, hc.shape[0] // tm, tm, [_ri(hc), _ri(ds)], [g, b], [(D_MODEL, F32)],
                    [(1, D_MODEL)] * 2)


def _conv_blocks(seq):
    cb = 128
    tt = 128 if seq % 128 == 0 else seq
    return cb, tt


def _conv_taps(win, tt):
    n = win.shape[0]
    for s in range(8):
        ws = win if s == 0 else pltpu.roll(win, n - s, 0)
        for q in range(4):
            j = 8 * q + s - 1
            if 0 <= j < CONV_W:
                yield j, ws[8 * q:8 * q + tt, :]


def _fill_padded(pad_ref, x_ref, seq):
    zeros = jnp.zeros((CONV_HALO, pad_ref.shape[1]), F32)
    pad_ref[0:CONV_HALO, :] = zeros
    pad_ref[seq + CONV_HALO:seq + 2 * CONV_HALO, :] = zeros
    pad_ref[CONV_HALO:seq + CONV_HALO, :] = x_ref[...]


def _dwconv(name, xin, w, b):
    seq = xin.shape[0]
    cb, tt = _conv_blocks(seq)

    def kern(x_ref, w_ref, b_ref, o_ref, pad_ref):
        _fill_padded(pad_ref, x_ref, seq)
        wv = w_ref[...]
        bv = b_ref[...]

        def step(t, carry):
            base = pl.multiple_of(t * tt, tt)
            win = pad_ref[pl.ds(base, tt + 2 * CONV_HALO), :]
            acc = jnp.zeros((tt, cb), F32) + bv
            for j, rows in _conv_taps(win, tt):
                acc = acc + wv[j:j + 1, :] * rows
            o_ref[pl.ds(base, tt), :] = acc
            return carry

        lax.fori_loop(0, seq // tt, step, 0)

    return pl.pallas_call(
        kern, name=name, grid=(D_MODEL // cb,),
        in_specs=[pl.BlockSpec((seq, cb), lambda j: (0, j)), pl.BlockSpec((CONV_W, cb), lambda j: (0, j)),
                  pl.BlockSpec((1, cb), lambda j: (0, j))],
        out_specs=pl.BlockSpec((seq, cb), lambda j: (0, j)),
        out_shape=jax.ShapeDtypeStruct((seq, D_MODEL), F32),
        scratch_shapes=[pltpu.VMEM((seq + 2 * CONV_HALO, cb), F32)],
        compiler_params=_params())(xin, w, b)


def _dwconv_wgrad(name, xin, dout):
    seq = xin.shape[0]
    cb, tt = _conv_blocks(seq)

    def kern(x_ref, d_ref, dw_ref, db_ref, pad_ref):
        _fill_padded(pad_ref, x_ref, seq)
        dw_ref[...] = jnp.zeros(dw_ref.shape, F32)
        db_ref[...] = jnp.zeros(db_ref.shape, F32)

        def step(t, carry):
            base = pl.multiple_of(t * tt, tt)
            win = pad_ref[pl.ds(base, tt + 2 * CONV_HALO), :]
            d = d_ref[pl.ds(base, tt), :]
            db_ref[...] += _colsum(d)
            for j, rows in _conv_taps(win, tt):
                dw_ref[j:j + 1, :] += _colsum(d * rows)
            return carry

        lax.fori_loop(0, seq // tt, step, 0)

    return pl.pallas_call(
        kern, name=name, grid=(D_MODEL // cb,),
        in_specs=[pl.BlockSpec((seq, cb), lambda j: (0, j)), pl.BlockSpec((seq, cb), lambda j: (0, j))],
        out_specs=[pl.BlockSpec((CONV_W, cb), lambda j: (0, j)), pl.BlockSpec((1, cb), lambda j: (0, j))],
        out_shape=[jax.ShapeDtypeStruct((CONV_W, D_MODEL), F32), jax.ShapeDtypeStruct((1, D_MODEL), F32)],
        scratch_shapes=[pltpu.VMEM((seq + 2 * CONV_HALO, cb), F32)],
        compiler_params=_params())(xin, dout)


def _mm(name, a, b, out_sds, grid, a_spec, b_spec, o_spec, contract, k_axis=None, bias=None):
    dn = (contract, ((), ()))

    def kern(*refs):
        if bias is None:
            a_ref, b_ref, o_ref = refs
        else:
            a_ref, b_ref, bias_ref, o_ref = refs
        p = lax.dot_general(a_ref[...].astype(BF16), b_ref[...].astype(BF16), dn, preferred_element_type=F32)
        if bias is not None:
            p = p + bias_ref[...]
        if k_axis is None:
            o_ref[...] = p.astype(o_ref.dtype)
        else:
            k = pl.program_id(k_axis)

            @pl.when(k == 0)
            def _():
                o_ref[...] = p

            @pl.when(k > 0)
            def _():
                o_ref[...] += p

    in_specs = [a_spec, b_spec]
    args = [a, b]
    if bias is not None:
        in_specs.append(pl.BlockSpec(bias.shape, lambda *_: (0,) * bias.ndim))
        args.append(bias)
    return pl.pallas_call(kern, name=name, grid=grid, in_specs=in_specs, out_specs=o_spec, out_shape=out_sds,
                          compiler_params=_params())(*args)


def _mm_rows(name, a, b, tm, trans_b=False, bias=None):
    m, k = a.shape
    n = b.shape[0] if trans_b else b.shape[1]
    contract = ((1,), (1,)) if trans_b else ((1,), (0,))
    return _mm(name, a, b, jax.ShapeDtypeStruct((m, n), F32), (m // tm,),
               pl.BlockSpec((tm, k), lambda i: (i, 0)), pl.BlockSpec(b.shape, lambda i: (0, 0)),
               pl.BlockSpec((tm, n), lambda i: (i, 0)), contract, bias=bias)


def _mm_rows_res(name, a, b, h, gt, tm, bias=None):
    m, k = a.shape
    n = b.shape[1]

    def kern(a_ref, b_ref, h_ref, gt_ref, *rest):
        y = jnp.dot(a_ref[...], b_ref[...], preferred_element_type=F32)
        if bias is not None:
            y = y + rest[0][...]
        rest[-2][...] = y
        rest[-1][...] = h_ref[...] + gt_ref[...] * y

    row = pl.BlockSpec((tm, n), lambda i: (i, 0))
    vec = pl.BlockSpec((1, n), lambda i: (0, 0))
    return pl.pallas_call(
        kern, name=name, grid=(m // tm,),
        in_specs=[pl.BlockSpec((tm, k), lambda i: (i, 0)), pl.BlockSpec(b.shape, lambda i: (0, 0)), row, vec]
        + ([vec] if bias is not None else []),
        out_specs=[row, row], out_shape=[jax.ShapeDtypeStruct((m, n), F32)] * 2,
        compiler_params=_params())(a, b, h, gt, *([bias] if bias is not None else []))


def _ffn_in_swiglu(name, xf, wblk, tm):
    m, k = xf.shape
    nblk = wblk.shape[2]

    def kern(x_ref, wg_ref, wu_ref, gu_ref, a_ref):
        xv = x_ref[...]
        g = jnp.dot(xv, wg_ref[...], preferred_element_type=F32)
        u = jnp.dot(xv, wu_ref[...], preferred_element_type=F32)
        gu_ref[0] = g.astype(BF16)
        gu_ref[1] = u.astype(BF16)
        a_ref[...] = (g * _sigmoid(g) * u).astype(BF16)

    return pl.pallas_call(
        kern, name=name, grid=(2, m // tm),
        in_specs=[pl.BlockSpec((tm, k), lambda j, i: (i, 0)), pl.BlockSpec((None, k, nblk), lambda j, i: (j, 0, 0)),
                  pl.BlockSpec((None, k, nblk), lambda j, i: (j + 2, 0, 0))],
        out_specs=[pl.BlockSpec((2, tm, nblk), lambda j, i: (0, i, j)), pl.BlockSpec((tm, nblk), lambda j, i: (i, j))],
        out_shape=[jax.ShapeDtypeStruct((2, m, 2 * nblk), BF16), jax.ShapeDtypeStruct((m, 2 * nblk), BF16)],
        compiler_params=_params())(xf, wblk, wblk)


def _ffn_out_dx_swiglu(name, dy, wo, gu, tm):
    m, k = dy.shape
    nblk = gu.shape[2] // 2

    def kern(dy_ref, w_ref, gu_ref, o_ref):
        da = lax.dot_general(dy_ref[...], w_ref[...], (((1,), (1,)), ((), ())), preferred_element_type=F32)
        g = gu_ref[0].astype(F32)
        sg = _sigmoid(g)
        o_ref[0] = (da * gu_ref[1].astype(F32) * (sg * (1.0 + g * (1.0 - sg)))).astype(BF16)
        o_ref[1] = (da * (g * sg)).astype(BF16)

    return pl.pallas_call(
        kern, name=name, grid=(2, m // tm),
        in_specs=[pl.BlockSpec((tm, k), lambda j, i: (i, 0)), pl.BlockSpec((nblk, k), lambda j, i: (j, 0)),
                  pl.BlockSpec((2, tm, nblk), lambda j, i: (0, i, j))],
        out_specs=pl.BlockSpec((2, tm, nblk), lambda j, i: (0, i, j)),
        out_shape=jax.ShapeDtypeStruct(gu.shape, BF16), compiler_params=_params())(dy, wo, gu)


def _ffn_in_dx(name, dgu, wblk, tm):
    m = dgu.shape[1]
    nb, k, nblk = wblk.shape

    def kern(a_ref, w_ref, o_ref):
        acc = None
        for j in range(nb):
            p = lax.dot_general(a_ref[j // 2, :, (j % 2) * nblk:(j % 2 + 1) * nblk], w_ref[j],
                                (((1,), (1,)), ((), ())), preferred_element_type=F32)
            acc = p if acc is None else acc + p
        o_ref[...] = acc

    return pl.pallas_call(
        kern, name=name, grid=(m // tm,),
        in_specs=[pl.BlockSpec((2, tm, 2 * nblk), lambda i: (0, i, 0)), pl.BlockSpec(wblk.shape, lambda i: (0, 0, 0))],
        out_specs=pl.BlockSpec((tm, k), lambda i: (i, 0)),
        out_shape=jax.ShapeDtypeStruct((m, k), F32), compiler_params=_params())(dgu, wblk)


def _mm_halves_wgrad_blk(name, a, b2, tk):
    t, k = a.shape
    nblk = b2.shape[2] // 2
    return _mm(name, a, b2, jax.ShapeDtypeStruct((N_CHIP, k, nblk), F32), (N_CHIP, t // tk),
               pl.BlockSpec((tk, k), lambda j, i: (i, 0)),
               pl.BlockSpec((None, tk, nblk), lambda j, i: (j // 2, i, j % 2)),
               pl.BlockSpec((None, k, nblk), lambda j, i: (j, 0, 0)), ((0,), (0,)), k_axis=1)


def _mm_wgrad(name, a, b, tk):
    t, k = a.shape
    n = b.shape[1]
    tn = n if k * n * 4 <= 4 * 1024 * 1024 else 512
    return _mm(name, a, b, jax.ShapeDtypeStruct((k, n), F32), (n // tn, t // tk),
               pl.BlockSpec((tk, k), lambda j, i: (i, 0)), pl.BlockSpec((tk, tn), lambda j, i: (i, j)),
               pl.BlockSpec((k, tn), lambda j, i: (0, j)), ((0,), (0,)), k_axis=1)


def _attn_fwd(name, q_all, k, v, n_ctx, tq, gather=()):
    r = k.shape[1]
    s = q_all.shape[0] - n_ctx
    gw = Q_GROUP * HEAD_DIM
    ng = len(gather)
    n_i = s // tq
    last = N_KV_HEADS * n_i - 1
    forward_at = (7 * last) // 8

    def kern(q_ref, k_ref, v_ref, *rest):
        o_ref, lse_ref = rest[ng], rest[ng + 1]
        if ng:
            start, forward, finish = _gather_phases(rest[:ng], rest[ng + 2:2 * ng + 2], *rest[2 * ng + 2:])
            step = pl.program_id(0) * n_i + pl.program_id(1)
            pl.when(step == 0)(start)
            pl.when(step == forward_at)(forward)
        kv = k_ref[...]
        vv = v_ref[...]
        for g in range(Q_GROUP):
            sl = slice(g * HEAD_DIM, (g + 1) * HEAD_DIM)
            sc = lax.dot_general(q_ref[:, sl], kv, (((1,), (1,)), ((), ())), preferred_element_type=F32)
            m = jnp.max(sc, axis=-1, keepdims=True)
            p = jnp.exp(sc - m)
            l = jnp.sum(p, axis=-1, keepdims=True)
            o_ref[:, sl] = jnp.dot(p.astype(BF16), vv, preferred_element_type=F32) / l
            lse_ref[g] = m + jnp.log(l)
        if ng:
            pl.when(step == last)(finish)

    return pl.pallas_call(
        kern, name=name, grid=(N_KV_HEADS, n_i),
        in_specs=[pl.BlockSpec((tq, gw), lambda j, i: (i + n_ctx // tq, j)),
                  pl.BlockSpec((None, r, HEAD_DIM), lambda j, i: (j, 0, 0)),
                  pl.BlockSpec((None, r, HEAD_DIM), lambda j, i: (j, 0, 0))] + [_ANY] * ng,
        out_specs=[pl.BlockSpec((tq, gw), lambda j, i: (i, j)),
                   pl.BlockSpec((Q_GROUP, tq, 1), lambda j, i: (j, i, 0))] + [_ANY] * ng,
        out_shape=[jax.ShapeDtypeStruct((s, ATTN_W), F32), jax.ShapeDtypeStruct((N_Q_HEADS, s, 1), F32)]
        + [jax.ShapeDtypeStruct((N_DEV,) + a.shape, a.dtype) for a in gather],
        scratch_shapes=_gather_scratch(ng) if ng else [],
        compiler_params=_params())(q_all, k, v, *gather)


def _attn_bwd(name, q_all, k, v, o, lse, dcat, n_ctx, tq, exchange=()):
    nkv, r, _ = k.shape
    s = o.shape[0]
    gw = Q_GROUP * HEAD_DIM
    ne = len(exchange)
    n_i = s // tq
    last = nkv * n_i - 1

    def kern(q_ref, k_ref, v_ref, o_ref, lse_ref, do_ref, *rest):
        dq_ref, dk_ref, dv_ref = rest[ne:ne + 3]
        if ne:
            start, finish = _exchange_phases(rest[:ne], rest[ne + 3:2 * ne + 3], *rest[2 * ne + 3:])
            step = pl.program_id(0) * n_i + pl.program_id(1)
            pl.when(step == 0)(start)

        @pl.when(pl.program_id(1) == 0)
        def _():
            dk_ref[...] = jnp.zeros(dk_ref.shape, F32)
            dv_ref[...] = jnp.zeros(dv_ref.shape, F32)

        kv = k_ref[...]
        vv = v_ref[...]
        for g in range(Q_GROUP):
            sl = slice(g * HEAD_DIM, (g + 1) * HEAD_DIM)
            qv = q_ref[:, sl]
            dov = do_ref[:, sl]
            delta = jnp.sum(dov * o_ref[:, sl], axis=-1, keepdims=True)
            sc = lax.dot_general(qv, kv, (((1,), (1,)), ((), ())), preferred_element_type=F32)
            p = jnp.exp(sc - lse_ref[g])
            dob = dov.astype(BF16)
            dp = lax.dot_general(dob, vv, (((1,), (1,)), ((), ())), preferred_element_type=F32)
            dsb = (p * (dp - delta)).astype(BF16)
            dq_ref[:, sl] = jnp.dot(dsb, kv, preferred_element_type=F32)
            dk_ref[...] += lax.dot_general(dsb, qv, (((0,), (0,)), ((), ())), preferred_element_type=F32)
            dv_ref[...] += lax.dot_general(p.astype(BF16), dob, (((0,), (0,)), ((), ())),
                                           preferred_element_type=F32)
        if ne:
            pl.when(step == last)(finish)

    kvmap = lambda j, i: (j, 0, 0)
    return pl.pallas_call(
        kern, name=name, grid=(nkv, n_i),
        in_specs=[pl.BlockSpec((tq, gw), lambda j, i: (i + n_ctx // tq, j)), pl.BlockSpec((None, r, HEAD_DIM), kvmap),
                  pl.BlockSpec((None, r, HEAD_DIM), kvmap), pl.BlockSpec((tq, gw), lambda j, i: (i, j)),
                  pl.BlockSpec((Q_GROUP, tq, 1), lambda j, i: (j, i, 0)),
                  pl.BlockSpec((tq, gw), lambda j, i: (i, j))] + [_ANY] * ne,
        out_specs=[pl.BlockSpec((tq, gw), lambda j, i: (i, j)), pl.BlockSpec((None, r, HEAD_DIM), kvmap),
                   pl.BlockSpec((None, r, HEAD_DIM), kvmap)] + [_ANY] * ne,
        out_shape=[jax.ShapeDtypeStruct((s, ATTN_W), F32), jax.ShapeDtypeStruct((nkv, r, HEAD_DIM), F32),
                   jax.ShapeDtypeStruct((nkv, r, HEAD_DIM), F32)]
        + [jax.ShapeDtypeStruct(a.shape, a.dtype) for a in exchange],
        scratch_shapes=_exchange_scratch(ne) if ne else [],
        compiler_params=_params())(q_all, k, v, o, lse, dcat, *exchange)


def _mod_fwd(name, c16, wm, bm):
    n = wm.shape[2]

    def kern(c_ref, w_ref, b_ref, o_ref):
        cv = c_ref[...]
        a = (cv * _sigmoid(cv)).astype(BF16)
        o_ref[...] = jnp.dot(a, w_ref[...].astype(BF16), preferred_element_type=F32) + b_ref[...]

    return pl.pallas_call(
        kern, name=name, grid=(2,),
        in_specs=[pl.BlockSpec(c16.shape, lambda l: (0, 0)), pl.BlockSpec((None, D_MODEL, n), lambda l: (l, 0, 0)),
                  pl.BlockSpec((None, 1, n), lambda l: (l, 0, 0))],
        out_specs=pl.BlockSpec((None, 16, n), lambda l: (l, 0, 0)),
        out_shape=jax.ShapeDtypeStruct((2, 16, n), F32), compiler_params=_params())(c16, wm, bm)


def _mod_wgrad(name, c16, dm):
    n = dm.shape[2]
    tn = 512

    def kern(c_ref, d_ref, o_ref):
        cv = c_ref[...]
        a = cv * _sigmoid(cv)
        o_ref[...] = lax.dot_general(a, d_ref[...], (((0,), (0,)), ((), ())), preferred_element_type=F32,
                                     precision=lax.Precision.HIGHEST)

    return pl.pallas_call(
        kern, name=name, grid=(2, n // tn),
        in_specs=[pl.BlockSpec(c16.shape, lambda l, j: (0, 0)), pl.BlockSpec((None, 16, tn), lambda l, j: (l, 0, j))],
        out_specs=pl.BlockSpec((None, D_MODEL, tn), lambda l, j: (l, 0, j)),
        out_shape=jax.ShapeDtypeStruct((2, D_MODEL, n), F32), compiler_params=_params())(c16, dm)


def _cctx_grad(name, parts, c_ctx):
    def kern(p_ref, c_ref, o_ref):
        d = p_ref[0, 0:1, :] + p_ref[2, 0:1, :] + p_ref[4, 0:1, :] + p_ref[6, 0:1, :]
        cv = c_ref[...]
        sg = _sigmoid(cv)
        o_ref[...] = d * (sg * (1.0 + cv * (1.0 - sg)))

    return pl.pallas_call(kern, name=name, out_shape=jax.ShapeDtypeStruct((1, D_MODEL), F32),
                          compiler_params=_params())(parts, c_ctx)


def _sum_slots(name, g, tr):
    n, rows, cols = g.shape

    def kern(g_ref, o_ref):
        acc = g_ref[0].astype(F32)
        for i in range(1, n):
            acc = acc + g_ref[i].astype(F32)
        o_ref[...] = acc

    return pl.pallas_call(kern, name=name, grid=(rows // tr,),
                          in_specs=[pl.BlockSpec((n, tr, cols), lambda i: (0, i, 0))],
                          out_specs=pl.BlockSpec((tr, cols), lambda i: (i, 0)),
                          out_shape=jax.ShapeDtypeStruct((rows, cols), F32), compiler_params=_params())(g)


def _sum_into(name, g, cvec, n_layers, layer, prev=None):
    n, r, cc = g.shape
    tr = _row_tile(r, cc, 512 * 1024)

    def kern(c_ref, g_ref, *rest):
        acc = g_ref[0].astype(F32)
        for i in range(1, n):
            acc = acc + g_ref[i].astype(F32)
        rest[-1][...] = acc

    in_specs = [pl.BlockSpec((n, tr, cc), lambda i, c_ref: (0, i, 0))]
    args = [cvec, g]
    aliases = {}
    if prev is not None:
        in_specs.append(_ANY)
        args.append(prev)
        aliases = {2: 0}
    grid_spec = pltpu.PrefetchScalarGridSpec(
        num_scalar_prefetch=1, grid=(r // tr,), in_specs=in_specs,
        out_specs=pl.BlockSpec((None, None, tr, cc), lambda i, c_ref: (layer, c_ref[0], i, 0)))
    return pl.pallas_call(kern, name=name, grid_spec=grid_spec,
                          out_shape=jax.ShapeDtypeStruct((n_layers, 2, r, cc), F32),
                          input_output_aliases=aliases, compiler_params=_params())(*args)


def _row_tile(rows, cols, max_bytes=1024 * 1024):
    if rows * cols * 4 <= 2 * max_bytes:
        return rows
    best = None
    for t in range(16, rows + 1, 16):
        if rows % t == 0 and t * cols * 4 <= max_bytes:
            best = t
    assert best is not None, (rows, cols)
    return best


def _adamw(name, w, g, m, v, gather=()):
    shape = w.shape
    cols = shape[-1]
    rows = w.size // cols
    tr = _row_tile(rows, cols)
    ng = len(gather)
    last = rows // tr - 1

    def kern(w_ref, g_ref, m_ref, v_ref, *rest):
        d_ref, nm_ref, nv_ref = rest[ng:ng + 3]
        if ng:
            start, forward, finish = _gather_phases(rest[:ng], rest[ng + 3:2 * ng + 3], *rest[2 * ng + 3:])
            pl.when(pl.program_id(0) == 0)(start)
            pl.when(pl.program_id(0) == last // 2)(forward)
        gv = g_ref[...]
        m2 = ADAM_B1 * m_ref[...] + (1.0 - ADAM_B1) * gv
        v2 = ADAM_B2 * v_ref[...] + (1.0 - ADAM_B2) * (gv * gv)
        m_hat = m2 / (1.0 - ADAM_B1 ** ADAM_STEP)
        v_hat = v2 / (1.0 - ADAM_B2 ** ADAM_STEP)
        d_ref[...] = -ADAM_LR * (m_hat / (jnp.sqrt(v_hat) + ADAM_EPS) + ADAM_WD * w_ref[...])
        nm_ref[...] = m2
        nv_ref[...] = v2
        if ng:
            pl.when(pl.program_id(0) == last)(finish)

    spec = pl.BlockSpec((tr, cols), lambda i: (i, 0))
    sds = jax.ShapeDtypeStruct((rows, cols), F32)
    outs = pl.pallas_call(
        kern, name=name, grid=(rows // tr,), in_specs=[spec] * 4 + [_ANY] * ng, out_specs=[spec] * 3 + [_ANY] * ng,
        out_shape=[sds] * 3 + [jax.ShapeDtypeStruct((N_DEV,) + a.shape, a.dtype) for a in gather],
        scratch_shapes=_gather_scratch(ng) if ng else [], compiler_params=_params())(
        w.reshape(rows, cols), g.reshape(rows, cols), m.reshape(rows, cols), v.reshape(rows, cols), *gather)
    return tuple(o.reshape(shape) for o in outs[:3]) + tuple(outs[3:])


_ANY = pl.BlockSpec(memory_space=pl.ANY)


def _mesh_pos():
    return lax.axis_index("x"), lax.axis_index("y"), lax.axis_index("c")


def _gather_phases(srcs, outs, send_sems, recv_sems, local_sems):
    n = len(srcs)
    x, y, c = _mesh_pos()
    me = (x, y, c)
    sibling = (x, y, 1 - c)
    chips = [(1 - x, y), (x, 1 - y), (1 - x, 1 - y)]

    def slot(px, py, pc):
        return 4 * px + 2 * py + pc

    def copy(t, k, s, to, src=None):
        dst = outs[t].at[s]
        return pltpu.make_async_remote_copy(src_ref=dst if src is None else src, dst_ref=dst,
                                            send_sem=send_sems.at[t, k], recv_sem=recv_sems.at[t, k],
                                            device_id=to, device_id_type=MESH)

    def mine(t):
        return pltpu.make_async_copy(srcs[t], outs[t].at[slot(*me)], local_sems.at[t])

    def first(t):
        return [copy(t, 0, slot(*me), sibling, src=srcs[t])] + [
            copy(t, 1 + j, slot(*me), (px, py, c), src=srcs[t]) for j, (px, py) in enumerate(chips)]

    def passed(t, j):
        px, py = chips[j]
        return copy(t, 4 + j, slot(px, py, c), sibling)

    def start():
        for t in range(n):
            mine(t).start()
        for t in range(n):
            for cp in first(t):
                cp.start()

    def forward():
        for j, (px, py) in enumerate(chips):
            for t in range(n):
                copy(t, 1 + j, slot(px, py, c), me).wait_recv()
                passed(t, j).start()

    def finish():
        for t in range(n):
            copy(t, 0, slot(x, y, 1 - c), me).wait_recv()
            for j, (px, py) in enumerate(chips):
                copy(t, 4 + j, slot(px, py, 1 - c), me).wait_recv()
        for t in range(n):
            for cp in first(t) + [passed(t, j) for j in range(len(chips))]:
                cp.wait_send()
            mine(t).wait()

    return start, forward, finish


def _gather_scratch(n):
    return [pltpu.SemaphoreType.DMA((n, 7)), pltpu.SemaphoreType.DMA((n, 7)), pltpu.SemaphoreType.DMA((n,))]


def _allgather(name, items):
    n = len(items)

    def body(*refs):
        start, forward, finish = _gather_phases(refs[:n], refs[n:2 * n], *refs[2 * n:])
        start()
        forward()
        finish()

    return pl.pallas_call(
        body, name=name, in_specs=[_ANY] * n, out_specs=[_ANY] * n,
        out_shape=[jax.ShapeDtypeStruct((N_DEV,) + a.shape, a.dtype) for a in items],
        scratch_shapes=_gather_scratch(n), compiler_params=_params())(*items)


def _rs_pair_send(name, gs):
    n = len(gs)

    def body(*refs):
        srcs = refs[:n]
        bufs = refs[n:2 * n]
        send_sems, recv_sems = refs[2 * n:]
        x, y, c = _mesh_pos()
        copies = []
        for t in range(n):
            for k in range(N_CHIP):
                copies.append(pltpu.make_async_remote_copy(
                    src_ref=srcs[t].at[k, 1 - c], dst_ref=bufs[t].at[k], send_sem=send_sems.at[t, k],
                    recv_sem=recv_sems.at[t, k], device_id=(x, y, 1 - c), device_id_type=MESH))
        for cp in copies:
            cp.start()
        for cp in copies:
            cp.wait()

    return pl.pallas_call(
        body, name=name, in_specs=[_ANY] * n, out_specs=[_ANY] * n,
        out_shape=[jax.ShapeDtypeStruct((N_CHIP,) + g.shape[2:], g.dtype) for g in gs],
        scratch_shapes=[pltpu.SemaphoreType.DMA((n, N_CHIP)), pltpu.SemaphoreType.DMA((n, N_CHIP))],
        compiler_params=_params())(*gs)


def _rs_pair_add(name, g, buf, cvec):
    _, _, r, cc = g.shape
    tr = _row_tile(r, cc, 2 * 1024 * 1024)

    def kern(c_ref, g_ref, b_ref, o_ref):
        o_ref[...] = (g_ref[...] + b_ref[...]).astype(BF16)

    grid_spec = pltpu.PrefetchScalarGridSpec(
        num_scalar_prefetch=1, grid=(N_CHIP, r // tr),
        in_specs=[pl.BlockSpec((None, None, tr, cc), lambda k, i, c_ref: (k, c_ref[0], i, 0)),
                  pl.BlockSpec((None, tr, cc), lambda k, i, c_ref: (k, i, 0))],
        out_specs=pl.BlockSpec((None, tr, cc), lambda k, i, c_ref: (k, i, 0)))
    return pl.pallas_call(kern, name=name, grid_spec=grid_spec,
                          out_shape=jax.ShapeDtypeStruct((N_CHIP, r, cc), BF16),
                          compiler_params=_params())(cvec, g, buf)


def _exchange_phases(srcs, bufs, send_sems, recv_sems, local_sems):
    n = len(srcs)
    x, y, c = _mesh_pos()
    kme = 2 * x + y
    chips = [(1 - x, y), (x, 1 - y), (1 - x, 1 - y)]

    def copies():
        local = [pltpu.make_async_copy(srcs[t].at[kme], bufs[t].at[kme], local_sems.at[t]) for t in range(n)]
        remote = []
        for t in range(n):
            for j, (px, py) in enumerate(chips):
                remote.append(pltpu.make_async_remote_copy(
                    src_ref=srcs[t].at[2 * px + py], dst_ref=bufs[t].at[kme], send_sem=send_sems.at[t, j],
                    recv_sem=recv_sems.at[t, j], device_id=(px, py, c), device_id_type=MESH))
        return local, remote

    def start():
        local, remote = copies()
        for cp in local + remote:
            cp.start()

    def finish():
        local, remote = copies()
        for cp in remote + local:
            cp.wait()

    return start, finish


def _exchange_scratch(n):
    return [pltpu.SemaphoreType.DMA((n, 3)), pltpu.SemaphoreType.DMA((n, 3)), pltpu.SemaphoreType.DMA((n,))]


def _rs_chip_exchange(name, ss):
    n = len(ss)

    def body(*refs):
        start, finish = _exchange_phases(refs[:n], refs[n:2 * n], *refs[2 * n:])
        start()
        finish()

    return pl.pallas_call(
        body, name=name, in_specs=[_ANY] * n, out_specs=[_ANY] * n,
        out_shape=[jax.ShapeDtypeStruct(s.shape, s.dtype) for s in ss],
        scratch_shapes=_exchange_scratch(n), compiler_params=_params())(*ss)


def _rs_pair_share(name, dsts):
    nd = len(dsts)

    def body(*refs):
        ins = refs[:nd]
        outs = refs[nd:2 * nd]
        send_sems, recv_sems = refs[2 * nd:]
        x, y, c = _mesh_pos()
        copies = []
        for d in range(nd):
            for l in range(dsts[d].shape[0]):
                copies.append(pltpu.make_async_remote_copy(
                    src_ref=ins[d].at[l, c], dst_ref=outs[d].at[l, c], send_sem=send_sems.at[d, l],
                    recv_sem=recv_sems.at[d, l], device_id=(x, y, 1 - c), device_id_type=MESH))
        for cp in copies:
            cp.start()
        for cp in copies:
            cp.wait()

    return pl.pallas_call(
        body, name=name, in_specs=[_ANY] * nd, out_specs=[_ANY] * nd,
        out_shape=[jax.ShapeDtypeStruct(a.shape, a.dtype) for a in dsts],
        input_output_aliases={d: d for d in range(nd)},
        scratch_shapes=[pltpu.SemaphoreType.DMA((nd, 2)), pltpu.SemaphoreType.DMA((nd, 2))],
        compiler_params=_params())(*dsts)


def _rope_tables(seq, n_ctx):
    t = jnp.arange(seq)
    row = (t // GRID_W).astype(F32)
    col = (t % GRID_W).astype(F32)
    inv = ROPE_THETA ** (-jnp.arange(0, HEAD_DIM // 2, 2, dtype=F32) / (HEAD_DIM // 2))
    ang_r = row[:, None] * inv[None, :]
    ang_c = col[:, None] * inv[None, :]
    cos = jnp.concatenate([jnp.cos(ang_r)] * 2 + [jnp.cos(ang_c)] * 2, axis=1)
    sin = jnp.concatenate([-jnp.sin(ang_r), jnp.sin(ang_r), -jnp.sin(ang_c), jnp.sin(ang_c)], axis=1)
    cos = jnp.concatenate([jnp.ones((n_ctx, HEAD_DIM), F32), cos], axis=0)
    sin = jnp.concatenate([jnp.zeros((n_ctx, HEAD_DIM), F32), sin], axis=0)
    return jnp.tile(cos, (1, N_KV_HEADS)), jnp.tile(sin, (1, N_KV_HEADS))


def _to_heads(a, nh):
    return a.reshape(a.shape[0], nh, HEAD_DIM).transpose(1, 0, 2)


def _from_heads(a):
    return a.transpose(1, 0, 2).reshape(a.shape[1], a.shape[0] * HEAD_DIM)


def _sample_step(x, ctx, tgt, mod, cmod, W, pending, cvec):
    W = dict(W)
    seq = x.shape[0]
    n_ctx = ctx.shape[0]
    tm = min(256, seq)
    tmc = min(256, n_ctx)
    tmm = 512 if seq % 512 == 0 else tm
    tmw = 1024 if seq % 1024 == 0 else tmm
    tmr = (n_ctx + seq) // 4 if (n_ctx + seq) % 64 == 0 else tm
    tq = min(256, seq)
    assert n_ctx % tm == 0 and seq % tm == 0

    def mv(l, j):
        return mod[l, j * D_MODEL:(j + 1) * D_MODEL].reshape(1, D_MODEL)

    csh1 = cmod[:D_MODEL].reshape(1, D_MODEL)
    csc1 = cmod[D_MODEL:2 * D_MODEL].reshape(1, D_MODEL)
    g_mix = [W['g_mix'][l:l + 1] for l in range(2)]
    g_ffn = [W['g_ffn'][l:l + 1] for l in range(2)]
    cos, sin = _rope_tables(seq, n_ctx)
    gq = jnp.tile(W['q_gain'], (1, N_Q_HEADS))
    gk = jnp.tile(W['k_gain'], (1, N_KV_HEADS))
    bdq = jnp.kron(jnp.eye(N_Q_HEADS, dtype=F32), jnp.ones((HEAD_DIM, HEAD_DIM), F32)).astype(BF16)
    bdk = bdq[:KV_W, :KV_W]
    w_sp = W['w_sp'].astype(BF16)
    w_sp_t = w_sp.transpose(0, 2, 1)
    bsp = jnp.broadcast_to(W['b_sp'].T[:, :, None], (CHUNK, N_SG, CHUNK)).reshape(CHUNK, SG_W)

    def ffn_fwd(l, h_mid):
        xf = _modnorm_fwd(f'l{l}_ffn_norm', h_mid, g_ffn[l], mv(l, 4), mv(l, 3), tm)
        gu, a = _ffn_in_swiglu(f'l{l}_ffn_in', xf, W['ffn_in'][l], tmw)
        f, h_out = _mm_rows_res(f'l{l}_ffn_out', a, W['ffn_out'][l], h_mid, mv(l, 5), tmm)
        return h_out, (h_mid, xf, gu, a, f)

    def ffn_bwd(l, dh_out, saved):
        h_mid, xf, gu, a, f = saved
        dy, d_gt2, _ = _resgate_bwd(f'l{l}_ffn_res_bwd', dh_out, f, mv(l, 5), tm)
        dgu = _ffn_out_dx_swiglu(f'l{l}_ffn_out_dx', dy, W['ffn_out'][l], gu, tmw)
        d_wo = _mm_wgrad(f'l{l}_ffn_out_dw', a, dy, tmw)
        dxf = _ffn_in_dx(f'l{l}_ffn_in_dx', dgu, W['ffn_in'][l], tmm)
        d_wi = _mm_halves_wgrad_blk(f'l{l}_ffn_in_dw', xf, dgu, tmw)
        dh_mid, d_sh2, d_sc2, d_g = _modnorm_bwd(f'l{l}_ffn_norm_bwd', h_mid, [_ri(dxf)], dh_out, g_ffn[l],
                                                 mv(l, 4), tm)
        return dh_mid, d_wi, d_wo, d_g, (d_sh2, d_sc2, d_gt2)

    xm0 = _modnorm_fwd('l0_mix_norm', x, g_mix[0], mv(0, 1), mv(0, 0), tm)
    xc0 = _modnorm_fwd('l0_ctx_norm', ctx, g_mix[0], csc1, csh1, tmc)
    xall = jnp.concatenate([xc0, xm0], axis=0)
    p_qkv = _mm_rows('l0_qkv', xall, W['w_qkv'], tmr)
    p_sg = _mm_rows('l0_sg_in', xm0, W['w_sgp'], tmw)
    q_all, k_all, v_all = _qk_prep_fwd('l0_qk_prep', p_qkv, cos, sin, gq, gk, bdq, bdk, tm)
    kh = _to_heads(k_all, N_KV_HEADS)
    vh = _to_heads(v_all, N_KV_HEADS)
    o, lse, *gw = _attn_fwd('l0_attn', q_all, kh, vh, n_ctx, tq, gather=pending)
    W['w_out'] = gw[0].reshape(D_MODEL, D_MODEL)
    W['ffn_in'] = [gw[1].reshape(N_CHIP, D_MODEL, FF_BLK), gw[5].reshape(N_CHIP, D_MODEL, FF_BLK)]
    W['ffn_out'] = [gw[2].reshape(D_FF, D_MODEL), gw[6].reshape(D_FF, D_MODEL)]
    W['pw1'] = gw[3].reshape(N_CHIP, D_MODEL, PW1_BLK).transpose(1, 0, 2).reshape(D_MODEL, 2 * D_MODEL)
    W['pw2'] = gw[4].reshape(D_MODEL, D_MODEL)
    sg = _sg_fwd('l0_sg', p_sg, w_sp, bsp, tm)
    cat = jnp.concatenate([o.astype(BF16), sg], axis=1)
    y0, h1 = _mm_rows_res('l0_out', cat, W['w_out'], x, mv(0, 2), tmw)
    h2, ffn0 = ffn_fwd(0, h1)

    xm1 = _modnorm_fwd('l1_mix_norm', h2, g_mix[1], mv(1, 1), mv(1, 0), tm)
    ag = _mm_rows('l1_pw1', xm1, W['pw1'], tmm)
    hg = _glu_fwd('l1_glu', ag, W['b_pw1'], tm)
    hc = _dwconv('l1_conv', hg, W['w_dw'], W['b_dw'])
    s1 = _lnsilu_fwd('l1_ln_silu', hc, W['ln_g'], W['ln_b'], tm)
    y1, h3 = _mm_rows_res('l1_pw2', s1, W['pw2'], h2, mv(1, 2), tmw, bias=W['b_pw2'])
    h4, ffn1 = ffn_fwd(1, h3)

    dh4, d_g_final, loss_vec = _final_loss('final_loss', h4, tgt, W['g_final'], tm)

    dh3, d_wi1, d_wo1, d_gffn1, dmod1_ffn = ffn_bwd(1, dh4, ffn1)
    dy1, d_gt1_1, d_b_pw2 = _resgate_bwd('l1_mix_res_bwd', dh3, y1, mv(1, 2), tm)
    ds1 = _mm_rows('l1_pw2_dx', dy1, W['pw2'], tmw, trans_b=True)
    d_pw2 = _mm_wgrad('l1_pw2_dw', s1, dy1, tmw)
    dhc, d_ln_g, d_ln_b = _lnsilu_bwd('l1_ln_silu_bwd', hc, ds1, W['ln_g'], W['ln_b'], tm)
    dhg = _dwconv('l1_conv_dx', dhc, W['w_dw'][::-1], jnp.zeros((1, D_MODEL), F32))
    d_w_dw, d_b_dw = _dwconv_wgrad('l1_conv_dw', hg, dhc)
    dag, d_b_pw1 = _glu_bwd('l1_glu_bwd', ag, dhg, W['b_pw1'], tm)
    dxm1 = _mm_rows('l1_pw1_dx', dag, W['pw1'], tmm, trans_b=True)
    d_pw1 = _mm_wgrad('l1_pw1_dw', xm1, dag, tmw).reshape(D_MODEL, N_CHIP, PW1_BLK).transpose(1, 0, 2)
    dh2, d_sh1_1, d_sc1_1, d_gmix1 = _modnorm_bwd('l1_mix_norm_bwd', h2, [_ri(dxm1)], dh3, g_mix[1], mv(1, 1), tm)

    dh1, d_wi0, d_wo0, d_gffn0, dmod0_ffn = ffn_bwd(0, dh2, ffn0)
    dy0, d_gt1_0, _ = _resgate_bwd('l0_mix_res_bwd', dh1, y0, mv(0, 2), tm)
    dcat = _mm_rows('l0_out_dx', dy0, W['w_out'], tmw, trans_b=True)
    d_w_out = _mm_wgrad('l0_out_dw', cat, dy0, tmw)

    def blocked(a):
        cols = a.shape[-1]
        return a.reshape(N_CHIP, 2, a.size // (2 * N_CHIP * cols), cols)

    gs = [blocked(g) for g in (d_wi0, d_wi1, d_wo0, d_wo1, d_w_out, d_pw1, d_pw2)]
    sib = _rs_pair_send('rs_pair_send', gs)
    ss = [_rs_pair_add(f'rs_pair_add{t}', gs[t], sib[t], cvec) for t in range(len(gs))]
    dq, dk, dv, *xs = _attn_bwd('l0_attn_bwd', q_all, kh, vh, o, lse, dcat, n_ctx, tq, exchange=ss)
    dp_qkv, d_gq, d_gk = _qk_prep_bwd('l0_qk_prep_bwd', p_qkv, cos, sin, dq, n_ctx // tm, _from_heads(dk),
                                      _from_heads(dv), gq, gk, bdq, bdk, tm)
    dp_sg, d_w_sp, d_bsp = _sg_bwd('l0_sg_bwd', p_sg, dcat, w_sp, w_sp_t, bsp, tm)
    d_w_qkv = _mm_wgrad('l0_qkv_dw', xall, dp_qkv, tmr)
    d_w_sgp = _mm_wgrad('l0_sg_in_dw', xm0, dp_sg, tmw)
    dxall = _mm_rows('l0_qkv_dx', dp_qkv, W['w_qkv'], tmr, trans_b=True)
    dxm_sg = _mm_rows('l0_sg_in_dx', dp_sg, W['w_sgp'], tmw, trans_b=True)
    dx, d_sh1_0, d_sc1_0, d_gmix0 = _modnorm_bwd('l0_mix_norm_bwd', x, [_ri(dxall, None, 0, n_ctx // tm), _ri(dxm_sg)],
                                                 dh1, g_mix[0], mv(0, 1), tm)
    _, d_csh1, d_csc1, d_gmix0c = _modnorm_bwd('l0_ctx_norm_bwd', ctx, [_ri(dxall)], None, g_mix[0], csc1, tmc)

    g_in = blocked(jnp.concatenate([d_w_qkv, d_w_sgp], axis=1).reshape(D_MODEL, N_CHIP, -1).transpose(1, 0, 2))
    sib_in = _rs_pair_send('rs_pair_send_w_in', [g_in])
    xs_in = _rs_chip_exchange('rs_chip_exchange_w_in', [_rs_pair_add('rs_pair_add_w_in', g_in, sib_in[0], cvec)])
    t_ffn_in = _sum_into('rs_sum_ffn_in1', xs[1], cvec, 2, 1, prev=_sum_into('rs_sum_ffn_in0', xs[0], cvec, 2, 0))
    t_ffn_out = _sum_into('rs_sum_ffn_out1', xs[3], cvec, 2, 1, prev=_sum_into('rs_sum_ffn_out0', xs[2], cvec, 2, 0))
    reduced = _rs_pair_share('rs_pair_share', [
        t_ffn_in, t_ffn_out, _sum_into('rs_sum_w_in', xs_in[0], cvec, 1, 0), _sum_into('rs_sum_w_out', xs[4], cvec, 1, 0),
        _sum_into('rs_sum_pw1', xs[5], cvec, 1, 0), _sum_into('rs_sum_pw2', xs[6], cvec, 1, 0)])
    big = dict(zip(['w_ffn_in', 'w_ffn_out', 'w_in', 'w_out', 'w_pw1', 'w_pw2'], reduced))

    zero = jnp.zeros((1, D_MODEL), F32)
    dmod = jnp.concatenate([d_sh1_0, d_sc1_0, d_gt1_0, *dmod0_ffn, d_sh1_1, d_sc1_1, d_gt1_1, *dmod1_ffn], axis=0)
    dcmod = jnp.concatenate([d_csh1, d_csc1, zero, zero, zero, zero], axis=0)
    grads = dict(
        g_mix=jnp.concatenate([d_gmix0 + d_gmix0c, d_gmix1], axis=0),
        g_ffn=jnp.concatenate([d_gffn0, d_gffn1], axis=0), g_final=d_g_final, gq=d_gq, gk=d_gk,
        w_sp=d_w_sp, bsp=d_bsp, b_pw1=d_b_pw1, w_dw=d_w_dw, b_dw=d_b_dw, ln_g=d_ln_g, ln_b=d_ln_b, b_pw2=d_b_pw2,
        dmod=dmod, dcmod=dcmod)
    return loss_vec, dx, grads, big


def kernel(x, c, ctx, c_ctx, w_mod, b_mod, g_mix, g_ffn, w_ffn_in, w_ffn_out, w_in, q_gain, k_gain, w_sp, b_sp, w_out, w_pw1, b_pw1, w_dw, b_dw, ln_g, ln_b, w_pw2, b_pw2, g_final, loss_target, m_c_ctx, m_w_mod, m_b_mod, m_g_mix, m_g_ffn, m_w_ffn_in, m_w_ffn_out, m_w_in, m_q_gain, m_k_gain, m_w_sp, m_b_sp, m_w_out, m_w_pw1, m_b_pw1, m_w_dw, m_b_dw, m_ln_g, m_ln_b, m_w_pw2, m_b_pw2, m_g_final, v_c_ctx, v_w_mod, v_b_mod, v_g_mix, v_g_ffn, v_w_ffn_in, v_w_ffn_out, v_w_in, v_q_gain, v_k_gain, v_w_sp, v_b_sp, v_w_out, v_w_pw1, v_b_pw1, v_w_dw, v_b_dw, v_ln_g, v_ln_b, v_w_pw2, v_b_pw2, v_g_final):
    given = dict(locals())
    ix, iy, ic = _mesh_pos()
    chip = 2 * ix + iy
    me = 2 * chip + ic
    d = D_MODEL
    q4 = d // N_CHIP

    small = jnp.concatenate([c.reshape(4, q4), b_pw1.reshape(2, q4), w_dw[0], b_dw, ln_g, ln_b, b_pw2,
                             jnp.zeros((7, q4), F32)], axis=0)
    def my_half(a):
        r = a.shape[0] // 2
        return lax.dynamic_slice_in_dim(a, ic * r, r, axis=0).astype(BF16)

    sgath, g_w_in = _allgather('ag_first', [small, my_half(w_in[0])])
    c_all = sgath[:, 0:4].reshape(N_DEV, d)
    per_chip = sgath[0::2]
    b_pw1_f = per_chip[:, 4:6].reshape(1, 2 * d)
    w_dw_f = per_chip[:, 6:6 + CONV_W].transpose(1, 0, 2).reshape(CONV_W, d)
    b_dw_f, ln_g_f, ln_b_f, b_pw2_f = [per_chip[:, 37 + i].reshape(1, d) for i in range(4)]

    c16 = jnp.concatenate([c_all, c_ctx[None, :], jnp.zeros((7, d), F32)], axis=0)
    bm_sh = lax.dynamic_slice_in_dim(b_mod, chip * MOD_BLK, MOD_BLK, axis=1).reshape(2, 1, MOD_BLK)
    mod_piece = _mod_fwd('mod_fwd', c16, w_mod, bm_sh)
    mgath = _allgather('ag_mod', [mod_piece.reshape(32, MOD_BLK)])[0]
    mod_all = mgath[0::2].reshape(N_CHIP, 2, 16, MOD_BLK).transpose(1, 2, 0, 3).reshape(2, 16, MOD_W)
    mod_me = lax.dynamic_index_in_dim(mod_all, me, axis=1, keepdims=False)
    cmod = mod_all[0, N_DEV]

    w_in_f = g_w_in.reshape(N_CHIP, d, -1).transpose(1, 0, 2).reshape(d, -1)
    W = dict(
        w_qkv=w_in_f[:, :QKV_W], w_sgp=w_in_f[:, QKV_W:],
        g_mix=g_mix, g_ffn=g_ffn, g_final=g_final.reshape(1, d), q_gain=q_gain, k_gain=k_gain, w_sp=w_sp[0],
        b_sp=b_sp[0], b_pw1=b_pw1_f, w_dw=w_dw_f, b_dw=b_dw_f, ln_g=ln_g_f, ln_b=ln_b_f, b_pw2=b_pw2_f)
    pending = [my_half(w_out[0]), my_half(w_ffn_in[0]), my_half(w_ffn_out[0]), my_half(w_pw1[0]), my_half(w_pw2[0]),
               my_half(w_ffn_in[1]), my_half(w_ffn_out[1])]
    cvec = jnp.reshape(ic, (1,)).astype(jnp.int32)

    loss_vec, dx, G, big = _sample_step(x[0], ctx[0], loss_target[0], mod_me, cmod, W, pending, cvec)
    loss = lax.psum(jnp.sum(loss_vec), ("x", "y", "c"))

    pack = jnp.concatenate([
        G['g_mix'], G['g_ffn'], G['g_final'],
        jnp.concatenate([G['gq'], G['gk'], jnp.zeros((1, d - ATTN_W - KV_W), F32)], axis=1),
        G['b_pw1'].reshape(2, d), G['w_dw'], G['b_dw'], G['ln_g'], G['ln_b'], G['b_pw2'],
        G['dmod'], G['dcmod'], jnp.zeros((3, d), F32),
        G['w_sp'].reshape(64, d), G['bsp'].reshape(64, d)], axis=0)
    assert pack.shape == (192, d)
    grads = dict(big)
    deltas, new_m, new_v = {}, {}, {}

    def adamw(n, gather=()):
        grads[n] = grads[n].reshape(given[n].shape)
        deltas[n], new_m[n], new_v[n], *rest = _adamw(f'adamw_{n}', given[n], grads[n], given['m_' + n],
                                                     given['v_' + n], gather)
        return rest

    pg = adamw('w_ffn_in', gather=[pack])[0]
    s8 = _sum_slots('sum_small_grads', pg, 64)

    def chip_cols(a, width):
        return lax.dynamic_slice_in_dim(a, chip * width, width, axis=a.ndim - 1)

    grads['g_mix'] = s8[0:2]
    grads['g_ffn'] = s8[2:4]
    grads['g_final'] = s8[4]
    grads['q_gain'] = s8[5, :ATTN_W].reshape(N_Q_HEADS, HEAD_DIM).sum(axis=0)[None, :]
    grads['k_gain'] = s8[5, ATTN_W:ATTN_W + KV_W].reshape(N_KV_HEADS, HEAD_DIM).sum(axis=0)[None, :]
    grads['b_pw1'] = chip_cols(s8[6:8].reshape(1, 2 * d), 2 * q4)
    grads['w_dw'] = chip_cols(s8[8:8 + CONV_W], q4)[None]
    grads['b_dw'] = chip_cols(s8[39:40], q4)
    grads['ln_g'] = chip_cols(s8[40:41], q4)
    grads['ln_b'] = chip_cols(s8[41:42], q4)
    grads['b_pw2'] = chip_cols(s8[42:43], q4)
    grads['w_sp'] = s8[64:128].reshape(1, N_SG, CHUNK, CHUNK)
    grads['b_sp'] = s8[128:192].reshape(CHUNK, N_SG, CHUNK).sum(axis=-1).T[None]

    dm_rows = pg[:, 43:55].reshape(N_DEV, 2, MOD_W)
    ctx_row = s8[55:61].reshape(1, MOD_W)
    dm = jnp.stack([jnp.concatenate([dm_rows[:, 0], ctx_row, jnp.zeros((7, MOD_W), F32)], axis=0),
                    jnp.concatenate([dm_rows[:, 1], jnp.zeros((8, MOD_W), F32)], axis=0)], axis=0)
    grads['b_mod'] = jnp.stack([s8[43:49].reshape(MOD_W) + ctx_row[0], s8[49:55].reshape(MOD_W)], axis=0)
    dm_sh = chip_cols(dm, MOD_BLK)
    grads['w_mod'] = _mod_wgrad('mod_dw', c16, dm_sh)
    d_silu = _mm_rows('mod_dx', dm_sh[0], w_mod[0], 16, trans_b=True)
    parts = _allgather('ag_cctx', [d_silu[8:16]])[0]
    grads['c_ctx'] = _cctx_grad('cctx_grad', parts, c_ctx.reshape(1, d))[0]

    for n in WEIGHT_NAMES:
        if n != 'w_ffn_in':
            adamw(n)

    return (loss, dx[None], *[grads[n] for n in WEIGHT_NAMES], *[deltas[n] for n in WEIGHT_NAMES],
            *[new_m[n] for n in WEIGHT_NAMES], *[new_v[n] for n in WEIGHT_NAMES])
```

```python
import functools
import math

import jax
import jax.numpy as jnp
from jax import lax
from jax.experimental import pallas as pl
from jax.experimental.pallas import tpu as pltpu

F32 = jnp.float32
BF16 = jnp.bfloat16
MESH = pl.DeviceIdType.MESH

D_MODEL = 1024
HEAD_DIM = 64
N_Q_HEADS = 8
N_KV_HEADS = 2
Q_GROUP = N_Q_HEADS // N_KV_HEADS
ATTN_W = N_Q_HEADS * HEAD_DIM
KV_W = N_KV_HEADS * HEAD_DIM
QKV_W = ATTN_W + 2 * KV_W
SG_W = D_MODEL - ATTN_W
N_SG = 4
CHUNK = 128
GRID_W = 64
ROPE_THETA = 10000.0
CONV_W = 31
CONV_HALO = 16
D_FF = 2816
MOD_W = 6 * D_MODEL
N_CHIP = 4
N_DEV = 8
FF_BLK = 2 * D_FF // N_CHIP
PW1_BLK = 2 * D_MODEL // N_CHIP
MOD_BLK = MOD_W // N_CHIP
EPS = 1e-6
GELU_C = math.sqrt(2.0 / math.pi)
GELU_A = 0.044715

ADAM_LR = 0.001
ADAM_B1 = 0.9
ADAM_B2 = 0.999
ADAM_EPS = 1e-08
ADAM_WD = 0.01
ADAM_STEP = 10

VMEM_LIMIT_BYTES = 56 * 1024 * 1024

WEIGHT_NAMES = ['c_ctx', 'w_mod', 'b_mod', 'g_mix', 'g_ffn', 'w_ffn_in', 'w_ffn_out', 'w_in', 'q_gain', 'k_gain',
                'w_sp', 'b_sp', 'w_out', 'w_pw1', 'b_pw1', 'w_dw', 'b_dw', 'ln_g', 'ln_b', 'w_pw2', 'b_pw2',
                'g_final']


def _params():
    return pltpu.CompilerParams(vmem_limit_bytes=VMEM_LIMIT_BYTES)


def _ri(arr, width=None, col_block=0, row_off=0, row_fn=None):
    return (arr, arr.shape[1] if width is None else width, col_block,
            (lambda i: i + row_off) if row_fn is None else row_fn)


def _rowwise(name, body, n_tiles, tm, row_ins, vec_ins, row_outs, vec_outs):
    nri, nvi, nro = len(row_ins), len(vec_ins), len(row_outs)

    def kern(*refs):
        ri = refs[:nri]
        vi = refs[nri:nri + nvi]
        ro = refs[nri + nvi:nri + nvi + nro]
        vo = refs[nri + nvi + nro:]
        if vo:
            @pl.when(pl.program_id(0) == 0)
            def _():
                for r in vo:
                    r[...] = jnp.zeros(r.shape, r.dtype)
        body(ri, vi, ro, vo)

    in_specs = [pl.BlockSpec((tm, bw), lambda i, cb=cb, rf=rf: (rf(i), cb)) for (_, bw, cb, rf) in row_ins]
    in_specs += [pl.BlockSpec(a.shape, lambda i, nd=a.ndim: (0,) * nd) for a in vec_ins]
    out_specs = [pl.BlockSpec((tm, w), lambda i: (i, 0)) for (w, _) in row_outs]
    out_specs += [pl.BlockSpec(s, lambda i, nd=len(s): (0,) * nd) for s in vec_outs]
    out_shape = [jax.ShapeDtypeStruct((n_tiles * tm, w), dt) for (w, dt) in row_outs]
    out_shape += [jax.ShapeDtypeStruct(s, F32) for s in vec_outs]
    return pl.pallas_call(kern, name=name, grid=(n_tiles,), in_specs=in_specs, out_specs=out_specs,
                          out_shape=out_shape, compiler_params=_params())(*[a for (a, _, _, _) in row_ins], *vec_ins)


def _colsum(v):
    return jnp.sum(v, axis=0, keepdims=True)


def _rowmean(v):
    return jnp.mean(v, axis=-1, keepdims=True)


def _rms(h):
    r = lax.rsqrt(_rowmean(h * h) + EPS)
    return h * r, r


def _rms_bwd(n, r, gy):
    return r * (gy - n * _rowmean(gy * n))


def _ln_stats(v):
    xc = v - _rowmean(v)
    r = lax.rsqrt(_rowmean(xc * xc) + EPS)
    return xc * r, r


def _ln_bwd(xh, r, dxh):
    return r * (dxh - _rowmean(dxh) - xh * _rowmean(dxh * xh))


def _sigmoid(v):
    return 1.0 / (1.0 + jnp.exp(-v))


def _gelu(v):
    t = jnp.tanh(GELU_C * (v + GELU_A * (v * v * v)))
    return 0.5 * v * (1.0 + t), t


def _gelu_grad(v, t):
    return 0.5 * (1.0 + t) + 0.5 * v * (1.0 - t * t) * (GELU_C * (1.0 + 3.0 * GELU_A * (v * v)))


def _modnorm_fwd(name, h, g, sc, sh, tm):
    def body(ri, vi, ro, vo):
        n, _ = _rms(ri[0][...])
        ro[0][...] = ((n * vi[0][...]) * (1.0 + vi[1][...]) + vi[2][...]).astype(BF16)

    return _rowwise(name, body, h.shape[0] // tm, tm, [_ri(h)], [g, sc, sh], [(D_MODEL, BF16)], [])[0]


def _modnorm_bwd(name, h, dxs, dh_in, g, sc, tm):
    ndx = len(dxs)

    def body(ri, vi, ro, vo):
        dxm = ri[1][...].astype(F32)
        for r in ri[2:1 + ndx]:
            dxm = dxm + r[...].astype(F32)
        gv = vi[0][...]
        n, r = _rms(ri[0][...])
        vo[0][...] += _colsum(dxm)
        vo[1][...] += _colsum(dxm * (n * gv))
        dy = dxm * (1.0 + vi[1][...])
        vo[2][...] += _colsum(dy * n)
        if dh_in is not None:
            ro[0][...] = ri[1 + ndx][...] + _rms_bwd(n, r, dy * gv)

    row_ins = [_ri(h)] + list(dxs) + ([_ri(dh_in)] if dh_in is not None else [])
    row_outs = [(D_MODEL, F32)] if dh_in is not None else []
    outs = _rowwise(name, body, h.shape[0] // tm, tm, row_ins, [g, sc], row_outs, [(1, D_MODEL)] * 3)
    if dh_in is None:
        return (None, *outs)
    return tuple(outs)


def _resgate_bwd(name, dh, y, gt, tm):
    def body(ri, vi, ro, vo):
        d = ri[0][...]
        dy = d * vi[0][...]
        ro[0][...] = dy.astype(BF16)
        vo[0][...] += _colsum(d * ri[1][...])
        vo[1][...] += _colsum(dy)

    return _rowwise(name, body, dh.shape[0] // tm, tm, [_ri(dh), _ri(y)], [gt], [(D_MODEL, BF16)],
                    [(1, D_MODEL)] * 2)


def _final_loss(name, h, tgt, g, tm):
    def body(ri, vi, ro, vo):
        gv = vi[0][...]
        n, r = _rms(ri[0][...])
        e = n * gv - ri[1][...]
        vo[1][...] += _colsum(e * e) * (0.5 / D_MODEL)
        dout = e * (1.0 / D_MODEL)
        vo[0][...] += _colsum(dout * n)
        ro[0][...] = _rms_bwd(n, r, dout * gv)

    return _rowwise(name, body, h.shape[0] // tm, tm, [_ri(h), _ri(tgt)], [g], [(D_MODEL, F32)],
                    [(1, D_MODEL)] * 2)


def _head_mean(v, bd):
    hi = v.astype(BF16)
    lo = (v - hi.astype(F32)).astype(BF16)
    s = jnp.dot(hi, bd, preferred_element_type=F32) + jnp.dot(lo, bd, preferred_element_type=F32)
    return s * (1.0 / HEAD_DIM)


def _swap16(v):
    w = v.shape[1]
    lane = lax.broadcasted_iota(jnp.int32, v.shape, 1)
    return jnp.where((lane & 16) == 0, pltpu.roll(v, w - 16, 1), pltpu.roll(v, 16, 1))


def _rope(v, cos, sin):
    return v * cos + _swap16(v) * sin


def _rope_bwd(d, cos, sin):
    return d * cos + _swap16(d * sin)


def _q_wide(t):
    return jnp.concatenate([t] * (ATTN_W // KV_W), axis=1)


def _qk_prep_fwd(name, p_qkv, cos, sin, gq, gk, bdq, bdk, tm):
    def body(ri, vi, ro, vo):
        pq = ri[0][...]
        pkv = ri[1][...]
        cosv = ri[2][...]
        sinv = ri[3][...]
        rq = lax.rsqrt(_head_mean(pq * pq, vi[2][...]) + EPS)
        ro[0][...] = (_rope(pq * rq * vi[0][...], _q_wide(cosv), _q_wide(sinv)) * (HEAD_DIM ** -0.5)).astype(BF16)
        pk = pkv[:, :KV_W]
        rk = lax.rsqrt(_head_mean(pk * pk, vi[3][...]) + EPS)
        ro[1][...] = _rope(pk * rk * vi[1][...], cosv, sinv).astype(BF16)
        ro[2][...] = pkv[:, KV_W:].astype(BF16)

    return _rowwise(name, body, p_qkv.shape[0] // tm, tm,
                    [_ri(p_qkv, ATTN_W, 0), _ri(p_qkv, 2 * KV_W, 2), _ri(cos), _ri(sin)], [gq, gk, bdq, bdk],
                    [(ATTN_W, BF16), (KV_W, BF16), (KV_W, BF16)], [])


def _qk_prep_bwd(name, p_qkv, cos, sin, dq, n_ctx_tiles, dk, dv, gq, gk, bdq, bdk, tm):
    def norm_rope_bwd(p, gain, bd, cosv, sinv, dout):
        r = lax.rsqrt(_head_mean(p * p, bd) + EPS)
        n = p * r
        dqn = _rope_bwd(dout, cosv, sinv)
        gy = dqn * gain
        return r * (gy - n * _head_mean(gy * n, bd)), _colsum(dqn * n)

    def body(ri, vi, ro, vo):
        pkv = ri[1][...]
        cosv = ri[2][...]
        sinv = ri[3][...]
        dqv = jnp.where(pl.program_id(0) >= n_ctx_tiles, ri[4][...] * (HEAD_DIM ** -0.5), 0.0)
        dxq, dgq = norm_rope_bwd(ri[0][...], vi[0][...], vi[2][...], _q_wide(cosv), _q_wide(sinv), dqv)
        dxk, dgk = norm_rope_bwd(pkv[:, :KV_W], vi[1][...], vi[3][...], cosv, sinv, ri[5][...])
        ro[0][:, :ATTN_W] = dxq.astype(BF16)
        ro[0][:, ATTN_W:ATTN_W + KV_W] = dxk.astype(BF16)
        ro[0][:, ATTN_W + KV_W:] = ri[6][...].astype(BF16)
        vo[0][...] += dgq
        vo[1][...] += dgk

    return _rowwise(name, body, p_qkv.shape[0] // tm, tm,
                    [_ri(p_qkv, ATTN_W, 0), _ri(p_qkv, 2 * KV_W, 2), _ri(cos), _ri(sin),
                     _ri(dq, row_fn=lambda i: jnp.maximum(i - n_ctx_tiles, 0)), _ri(dk), _ri(dv)],
                    [gq, gk, bdq, bdk], [(QKV_W, BF16)], [(1, ATTN_W), (1, KV_W)])


def _sg_fwd(name, p_sg, w_sp, bsp, tm):
    def body(ri, vi, ro, vo):
        p = ri[0][...]
        gu, _ = _gelu(p[:, :SG_W])
        gv, _ = _gelu(p[:, SG_W:])
        for g in range(N_SG):
            sl = slice(g * CHUNK, (g + 1) * CHUNK)
            vn, _ = _ln_stats(gv[:, sl])
            vnb = vn.astype(BF16)
            for m in range(tm // CHUNK):
                rs = slice(m * CHUNK, (m + 1) * CHUNK)
                mixed = jnp.dot(vi[0][g], vnb[rs], preferred_element_type=F32) + vi[1][:, sl]
                ro[0][rs, sl] = (gu[rs, sl] * mixed).astype(BF16)

    return _rowwise(name, body, p_sg.shape[0] // tm, tm, [_ri(p_sg)], [w_sp, bsp], [(SG_W, BF16)], [])[0]


def _sg_bwd(name, p_sg, dcat, w_sp, w_sp_t, bsp, tm):
    def body(ri, vi, ro, vo):
        p = ri[0][...]
        dsg = ri[1][...]
        su = p[:, :SG_W]
        sv = p[:, SG_W:]
        gu, tu = _gelu(su)
        gv, tv = _gelu(sv)
        for g in range(N_SG):
            sl = slice(g * CHUNK, (g + 1) * CHUNK)
            vn, r = _ln_stats(gv[:, sl])
            vnb = vn.astype(BF16)
            for m in range(tm // CHUNK):
                rs = slice(m * CHUNK, (m + 1) * CHUNK)
                mixed = jnp.dot(vi[0][g], vnb[rs], preferred_element_type=F32) + vi[2][:, sl]
                d_o = dsg[rs, sl]
                dm = d_o * gu[rs, sl]
                vo[1][:, sl] += dm
                dmb = dm.astype(BF16)
                vo[0][g] += lax.dot_general(dmb, vnb[rs], (((1,), (1,)), ((), ())), preferred_element_type=F32)
                dvn = jnp.dot(vi[1][g], dmb, preferred_element_type=F32)
                dgv = _ln_bwd(vn[rs], r[rs], dvn)
                ro[0][rs, sl] = (d_o * mixed * _gelu_grad(su[rs, sl], tu[rs, sl])).astype(BF16)
                ro[0][rs, SG_W + g * CHUNK:SG_W + (g + 1) * CHUNK] = (
                    dgv * _gelu_grad(sv[rs, sl], tv[rs, sl])).astype(BF16)

    return _rowwise(name, body, p_sg.shape[0] // tm, tm, [_ri(p_sg), _ri(dcat, SG_W, 1)], [w_sp, w_sp_t, bsp],
                    [(2 * SG_W, BF16)], [(N_SG, CHUNK, CHUNK), (CHUNK, SG_W)])


def _glu_fwd(name, ag, b, tm):
    def body(ri, vi, ro, vo):
        v = ri[0][...] + vi[0][...]
        ro[0][...] = v[:, :D_MODEL] * _sigmoid(v[:, D_MODEL:])

    return _rowwise(name, body, ag.shape[0] // tm, tm, [_ri(ag)], [b], [(D_MODEL, F32)], [])[0]


def _glu_bwd(name, ag, dhg, b, tm):
    def body(ri, vi, ro, vo):
        v = ri[0][...] + vi[0][...]
        d = ri[1][...]
        a = v[:, :D_MODEL]
        sg = _sigmoid(v[:, D_MODEL:])
        da = d * sg
        dgate = d * a * sg * (1.0 - sg)
        ro[0][:, :D_MODEL] = da.astype(BF16)
        ro[0][:, D_MODEL:] = dgate.astype(BF16)
        vo[0][:, :D_MODEL] += _colsum(da)
        vo[0][:, D_MODEL:] += _colsum(dgate)

    return _rowwise(name, body, ag.shape[0] // tm, tm, [_ri(ag), _ri(dhg)], [b], [(2 * D_MODEL, BF16)],
                    [(1, 2 * D_MODEL)])


def _lnsilu_fwd(name, hc, g, b, tm):
    def body(ri, vi, ro, vo):
        xh, _ = _ln_stats(ri[0][...])
        ln = xh * vi[0][...] + vi[1][...]
        ro[0][...] = (ln * _sigmoid(ln)).astype(BF16)

    return _rowwise(name, body, hc.shape[0] // tm, tm, [_ri(hc)], [g, b], [(D_MODEL, BF16)], [])[0]


def _lnsilu_bwd(name, hc, ds, g, b, tm):
    def body(ri, vi, ro, vo):
        gv = vi[0][...]
        xh, r = _ln_stats(ri[0][...])
        ln = xh * gv + vi[1][...]
        sg = _sigmoid(ln)
        dln = ri[1][...] * (sg * (1.0 + ln * (1.0 - sg)))
        vo[0][...] += _colsum(dln * xh)
        vo[1][...] += _colsum(dln)
        ro[0][...] = _ln_bwd(xh, r, dln * gv)

    return _rowwise(name, body, hc.shape[0] // tm, tm, [_ri(hc), _ri(ds)], [g, b], [(D_MODEL, F32)],
                    [(1, D_MODEL)] * 2)


def _conv_blocks(seq):
    cb = 128
    tt = 128 if seq % 128 == 0 else seq
    return cb, tt


def _conv_taps(win, tt):
    n = win.shape[0]
    for s in range(8):
        ws = win if s == 0 else pltpu.roll(win, n - s, 0)
        for q in range(4):
            j = 8 * q + s - 1
            if 0 <= j < CONV_W:
                yield j, ws[8 * q:8 * q + tt, :]


def _fill_padded(pad_ref, x_ref, seq):
    zeros = jnp.zeros((CONV_HALO, pad_ref.shape[1]), F32)
    pad_ref[0:CONV_HALO, :] = zeros
    pad_ref[seq + CONV_HALO:seq + 2 * CONV_HALO, :] = zeros
    pad_ref[CONV_HALO:seq + CONV_HALO, :] = x_ref[...]


def _dwconv(name, xin, w, b):
    seq = xin.shape[0]
    cb, tt = _conv_blocks(seq)

    def kern(x_ref, w_ref, b_ref, o_ref, pad_ref):
        _fill_padded(pad_ref, x_ref, seq)
        wv = w_ref[...]
        bv = b_ref[...]

        def step(t, carry):
            base = pl.multiple_of(t * tt, tt)
            win = pad_ref[pl.ds(base, tt + 2 * CONV_HALO), :]
            acc = jnp.zeros((tt, cb), F32) + bv
            for j, rows in _conv_taps(win, tt):
                acc = acc + wv[j:j + 1, :] * rows
            o_ref[pl.ds(base, tt), :] = acc
            return carry

        lax.fori_loop(0, seq // tt, step, 0)

    return pl.pallas_call(
        kern, name=name, grid=(D_MODEL // cb,),
        in_specs=[pl.BlockSpec((seq, cb), lambda j: (0, j)), pl.BlockSpec((CONV_W, cb), lambda j: (0, j)),
                  pl.BlockSpec((1, cb), lambda j: (0, j))],
        out_specs=pl.BlockSpec((seq, cb), lambda j: (0, j)),
        out_shape=jax.ShapeDtypeStruct((seq, D_MODEL), F32),
        scratch_shapes=[pltpu.VMEM((seq + 2 * CONV_HALO, cb), F32)],
        compiler_params=_params())(xin, w, b)


def _dwconv_wgrad(name, xin, dout):
    seq = xin.shape[0]
    cb, tt = _conv_blocks(seq)

    def kern(x_ref, d_ref, dw_ref, db_ref, pad_ref):
        _fill_padded(pad_ref, x_ref, seq)
        dw_ref[...] = jnp.zeros(dw_ref.shape, F32)
        db_ref[...] = jnp.zeros(db_ref.shape, F32)

        def step(t, carry):
            base = pl.multiple_of(t * tt, tt)
            win = pad_ref[pl.ds(base, tt + 2 * CONV_HALO), :]
            d = d_ref[pl.ds(base, tt), :]
            db_ref[...] += _colsum(d)
            for j, rows in _conv_taps(win, tt):
                dw_ref[j:j + 1, :] += _colsum(d * rows)
            return carry

        lax.fori_loop(0, seq // tt, step, 0)

    return pl.pallas_call(
        kern, name=name, grid=(D_MODEL // cb,),
        in_specs=[pl.BlockSpec((seq, cb), lambda j: (0, j)), pl.BlockSpec((seq, cb), lambda j: (0, j))],
        out_specs=[pl.BlockSpec((CONV_W, cb), lambda j: (0, j)), pl.BlockSpec((1, cb), lambda j: (0, j))],
        out_shape=[jax.ShapeDtypeStruct((CONV_W, D_MODEL), F32), jax.ShapeDtypeStruct((1, D_MODEL), F32)],
        scratch_shapes=[pltpu.VMEM((seq + 2 * CONV_HALO, cb), F32)],
        compiler_params=_params())(xin, dout)


def _mm(name, a, b, out_sds, grid, a_spec, b_spec, o_spec, contract, k_axis=None, bias=None):
    dn = (contract, ((), ()))

    def kern(*refs):
        if bias is None:
            a_ref, b_ref, o_ref = refs
        else:
            a_ref, b_ref, bias_ref, o_ref = refs
        p = lax.dot_general(a_ref[...].astype(BF16), b_ref[...].astype(BF16), dn, preferred_element_type=F32)
        if bias is not None:
            p = p + bias_ref[...]
        if k_axis is None:
            o_ref[...] = p.astype(o_ref.dtype)
        else:
            k = pl.program_id(k_axis)

            @pl.when(k == 0)
            def _():
                o_ref[...] = p

            @pl.when(k > 0)
            def _():
                o_ref[...] += p

    in_specs = [a_spec, b_spec]
    args = [a, b]
    if bias is not None:
        in_specs.append(pl.BlockSpec(bias.shape, lambda *_: (0,) * bias.ndim))
        args.append(bias)
    return pl.pallas_call(kern, name=name, grid=grid, in_specs=in_specs, out_specs=o_spec, out_shape=out_sds,
                          compiler_params=_params())(*args)


def _mm_rows(name, a, b, tm, trans_b=False, bias=None):
    m, k = a.shape
    n = b.shape[0] if trans_b else b.shape[1]
    contract = ((1,), (1,)) if trans_b else ((1,), (0,))
    return _mm(name, a, b, jax.ShapeDtypeStruct((m, n), F32), (m // tm,),
               pl.BlockSpec((tm, k), lambda i: (i, 0)), pl.BlockSpec(b.shape, lambda i: (0, 0)),
               pl.BlockSpec((tm, n), lambda i: (i, 0)), contract, bias=bias)


def _mm_rows_res(name, a, b, h, gt, tm, bias=None):
    m, k = a.shape
    n = b.shape[1]

    def kern(a_ref, b_ref, h_ref, gt_ref, *rest):
        y = jnp.dot(a_ref[...], b_ref[...], preferred_element_type=F32)
        if bias is not None:
            y = y + rest[0][...]
        rest[-2][...] = y
        rest[-1][...] = h_ref[...] + gt_ref[...] * y

    row = pl.BlockSpec((tm, n), lambda i: (i, 0))
    vec = pl.BlockSpec((1, n), lambda i: (0, 0))
    return pl.pallas_call(
        kern, name=name, grid=(m // tm,),
        in_specs=[pl.BlockSpec((tm, k), lambda i: (i, 0)), pl.BlockSpec(b.shape, lambda i: (0, 0)), row, vec]
        + ([vec] if bias is not None else []),
        out_specs=[row, row], out_shape=[jax.ShapeDtypeStruct((m, n), F32)] * 2,
        compiler_params=_params())(a, b, h, gt, *([bias] if bias is not None else []))


def _ffn_in_swiglu(name, xf, wblk, tm):
    m, k = xf.shape
    nblk = wblk.shape[2]

    def kern(x_ref, wg_ref, wu_ref, gu_ref, a_ref):
        xv = x_ref[...]
        g = jnp.dot(xv, wg_ref[...], preferred_element_type=F32)
        u = jnp.dot(xv, wu_ref[...], preferred_element_type=F32)
        gu_ref[0] = g.astype(BF16)
        gu_ref[1] = u.astype(BF16)
        a_ref[...] = (g * _sigmoid(g) * u).astype(BF16)

    return pl.pallas_call(
        kern, name=name, grid=(2, m // tm),
        in_specs=[pl.BlockSpec((tm, k), lambda j, i: (i, 0)), pl.BlockSpec((None, k, nblk), lambda j, i: (j, 0, 0)),
                  pl.BlockSpec((None, k, nblk), lambda j, i: (j + 2, 0, 0))],
        out_specs=[pl.BlockSpec((2, tm, nblk), lambda j, i: (0, i, j)), pl.BlockSpec((tm, nblk), lambda j, i: (i, j))],
        out_shape=[jax.ShapeDtypeStruct((2, m, 2 * nblk), BF16), jax.ShapeDtypeStruct((m, 2 * nblk), BF16)],
        compiler_params=_params())(xf, wblk, wblk)


def _ffn_out_dx_swiglu(name, dy, wo, gu, tm):
    m, k = dy.shape
    nblk = gu.shape[2] // 2

    def kern(dy_ref, w_ref, gu_ref, o_ref):
        da = lax.dot_general(dy_ref[...], w_ref[...], (((1,), (1,)), ((), ())), preferred_element_type=F32)
        g = gu_ref[0].astype(F32)
        sg = _sigmoid(g)
        o_ref[0] = (da * gu_ref[1].astype(F32) * (sg * (1.0 + g * (1.0 - sg)))).astype(BF16)
        o_ref[1] = (da * (g * sg)).astype(BF16)

    return pl.pallas_call(
        kern, name=name, grid=(2, m // tm),
        in_specs=[pl.BlockSpec((tm, k), lambda j, i: (i, 0)), pl.BlockSpec((nblk, k), lambda j, i: (j, 0)),
                  pl.BlockSpec((2, tm, nblk), lambda j, i: (0, i, j))],
        out_specs=pl.BlockSpec((2, tm, nblk), lambda j, i: (0, i, j)),
        out_shape=jax.ShapeDtypeStruct(gu.shape, BF16), compiler_params=_params())(dy, wo, gu)


def _ffn_in_dx(name, dgu, wblk, tm):
    m = dgu.shape[1]
    nb, k, nblk = wblk.shape

    def kern(a_ref, w_ref, o_ref):
        acc = None
        for j in range(nb):
            p = lax.dot_general(a_ref[j // 2, :, (j % 2) * nblk:(j % 2 + 1) * nblk], w_ref[j],
                                (((1,), (1,)), ((), ())), preferred_element_type=F32)
            acc = p if acc is None else acc + p
        o_ref[...] = acc

    return pl.pallas_call(
        kern, name=name, grid=(m // tm,),
        in_specs=[pl.BlockSpec((2, tm, 2 * nblk), lambda i: (0, i, 0)), pl.BlockSpec(wblk.shape, lambda i: (0, 0, 0))],
        out_specs=pl.BlockSpec((tm, k), lambda i: (i, 0)),
        out_shape=jax.ShapeDtypeStruct((m, k), F32), compiler_params=_params())(dgu, wblk)


def _mm_halves_wgrad_blk(name, a, b2, tk):
    t, k = a.shape
    nblk = b2.shape[2] // 2
    return _mm(name, a, b2, jax.ShapeDtypeStruct((N_CHIP, k, nblk), F32), (N_CHIP, t // tk),
               pl.BlockSpec((tk, k), lambda j, i: (i, 0)),
               pl.BlockSpec((None, tk, nblk), lambda j, i: (j // 2, i, j % 2)),
               pl.BlockSpec((None, k, nblk), lambda j, i: (j, 0, 0)), ((0,), (0,)), k_axis=1)


def _mm_wgrad(name, a, b, tk):
    t, k = a.shape
    n = b.shape[1]
    tn = n if k * n * 4 <= 4 * 1024 * 1024 else 512
    return _mm(name, a, b, jax.ShapeDtypeStruct((k, n), F32), (n // tn, t // tk),
               pl.BlockSpec((tk, k), lambda j, i: (i, 0)), pl.BlockSpec((tk, tn), lambda j, i: (i, j)),
               pl.BlockSpec((k, tn), lambda j, i: (0, j)), ((0,), (0,)), k_axis=1)


def _attn_fwd(name, q_all, k, v, n_ctx, tq, gather=()):
    r = k.shape[1]
    s = q_all.shape[0] - n_ctx
    gw = Q_GROUP * HEAD_DIM
    ng = len(gather)
    n_i = s // tq
    last = N_KV_HEADS * n_i - 1
    forward_at = (7 * last) // 8

    def kern(q_ref, k_ref, v_ref, *rest):
        o_ref, lse_ref = rest[ng], rest[ng + 1]
        if ng:
            start, forward, finish = _gather_phases(rest[:ng], rest[ng + 2:2 * ng + 2], *rest[2 * ng + 2:])
            step = pl.program_id(0) * n_i + pl.program_id(1)
            pl.when(step == 0)(start)
            pl.when(step == forward_at)(forward)
        kv = k_ref[...]
        vv = v_ref[...]
        for g in range(Q_GROUP):
            sl = slice(g * HEAD_DIM, (g + 1) * HEAD_DIM)
            sc = lax.dot_general(q_ref[:, sl], kv, (((1,), (1,)), ((), ())), preferred_element_type=F32)
            m = jnp.max(sc, axis=-1, keepdims=True)
            p = jnp.exp(sc - m)
            ol = jnp.dot(p.astype(BF16), vv, preferred_element_type=F32)
            l = ol[:, HEAD_DIM:HEAD_DIM + 1]
            o_ref[:, sl] = ol[:, :HEAD_DIM] / l
            lse_ref[g] = m + jnp.log(l)
        if ng:
            pl.when(step == last)(finish)

    return pl.pallas_call(
        kern, name=name, grid=(N_KV_HEADS, n_i),
        in_specs=[pl.BlockSpec((tq, gw), lambda j, i: (i + n_ctx // tq, j)),
                  pl.BlockSpec((None, r, HEAD_DIM), lambda j, i: (j, 0, 0)),
                  pl.BlockSpec((None, r, 2 * HEAD_DIM), lambda j, i: (j, 0, 0))] + [_ANY] * ng,
        out_specs=[pl.BlockSpec((tq, gw), lambda j, i: (i, j)),
                   pl.BlockSpec((Q_GROUP, tq, 1), lambda j, i: (j, i, 0))] + [_ANY] * ng,
        out_shape=[jax.ShapeDtypeStruct((s, ATTN_W), F32), jax.ShapeDtypeStruct((N_Q_HEADS, s, 1), F32)]
        + [jax.ShapeDtypeStruct((N_DEV,) + a.shape, a.dtype) for a in gather],
        scratch_shapes=_gather_scratch(ng) if ng else [],
        compiler_params=_params())(q_all, k, v, *gather)


def _attn_bwd(name, q_all, k, v, o, lse, dcat, n_ctx, tq, exchange=()):
    nkv, r, _ = k.shape
    s = o.shape[0]
    gw = Q_GROUP * HEAD_DIM
    ne = len(exchange)
    n_i = s // tq
    last = nkv * n_i - 1

    def kern(q_ref, k_ref, v_ref, o_ref, lse_ref, do_ref, *rest):
        dq_ref, dk_ref, dv_ref = rest[ne:ne + 3]
        ds_all, p_all = rest[2 * ne + 3:2 * ne + 5]
        if ne:
            start, finish = _exchange_phases(rest[:ne], rest[ne + 3:2 * ne + 3], *rest[2 * ne + 5:])
            step = pl.program_id(0) * n_i + pl.program_id(1)
            pl.when(step == 0)(start)

        @pl.when(pl.program_id(1) == 0)
        def _():
            dk_ref[...] = jnp.zeros(dk_ref.shape, F32)
            dv_ref[...] = jnp.zeros(dv_ref.shape, F32)

        kv = k_ref[...]
        vv = v_ref[...]
        qs = []
        dos = []
        for g in range(Q_GROUP):
            sl = slice(g * HEAD_DIM, (g + 1) * HEAD_DIM)
            rows = slice(g * tq, (g + 1) * tq)
            qv = q_ref[:, sl]
            dov = do_ref[:, sl]
            delta = jnp.sum(dov * o_ref[:, sl], axis=-1, keepdims=True)
            sc = lax.dot_general(qv, kv, (((1,), (1,)), ((), ())), preferred_element_type=F32)
            p = jnp.exp(sc - lse_ref[g])
            dob = dov.astype(BF16)
            dp = lax.dot_general(dob, vv, (((1,), (1,)), ((), ())), preferred_element_type=F32)
            dsb = (p * (dp - delta)).astype(BF16)
            dq_ref[:, sl] = jnp.dot(dsb, kv, preferred_element_type=F32)
            ds_all[rows, :] = dsb
            p_all[rows, :] = p.astype(BF16)
            qs.append(qv)
            dos.append(dob)
        dk_ref[...] += lax.dot_general(jnp.concatenate(qs, axis=0), ds_all[...], (((0,), (0,)), ((), ())),
                                       preferred_element_type=F32)
        dv_ref[...] += lax.dot_general(jnp.concatenate(dos, axis=0), p_all[...], (((0,), (0,)), ((), ())),
                                       preferred_element_type=F32)
        if ne:
            pl.when(step == last)(finish)

    kvmap = lambda j, i: (j, 0, 0)
    return pl.pallas_call(
        kern, name=name, grid=(nkv, n_i),
        in_specs=[pl.BlockSpec((tq, gw), lambda j, i: (i + n_ctx // tq, j)), pl.BlockSpec((None, r, HEAD_DIM), kvmap),
                  pl.BlockSpec((None, r, HEAD_DIM), kvmap), pl.BlockSpec((tq, gw), lambda j, i: (i, j)),
                  pl.BlockSpec((Q_GROUP, tq, 1), lambda j, i: (j, i, 0)),
                  pl.BlockSpec((tq, gw), lambda j, i: (i, j))] + [_ANY] * ne,
        out_specs=[pl.BlockSpec((tq, gw), lambda j, i: (i, j)), pl.BlockSpec((None, HEAD_DIM, r), kvmap),
                   pl.BlockSpec((None, HEAD_DIM, r), kvmap)] + [_ANY] * ne,
        out_shape=[jax.ShapeDtypeStruct((s, ATTN_W), F32), jax.ShapeDtypeStruct((nkv, HEAD_DIM, r), F32),
                   jax.ShapeDtypeStruct((nkv, HEAD_DIM, r), F32)]
        + [jax.ShapeDtypeStruct(a.shape, a.dtype) for a in exchange],
        scratch_shapes=[pltpu.VMEM((Q_GROUP * tq, r), BF16)] * 2 + (_exchange_scratch(ne) if ne else []),
        compiler_params=_params())(q_all, k, v, o, lse, dcat, *exchange)


def _mod_fwd(name, c16, wm, bm):
    n = wm.shape[2]

    def kern(c_ref, w_ref, b_ref, o_ref):
        cv = c_ref[...]
        a = (cv * _sigmoid(cv)).astype(BF16)
        o_ref[...] = jnp.dot(a, w_ref[...].astype(BF16), preferred_element_type=F32) + b_ref[...]

    return pl.pallas_call(
        kern, name=name, grid=(2,),
        in_specs=[pl.BlockSpec(c16.shape, lambda l: (0, 0)), pl.BlockSpec((None, D_MODEL, n), lambda l: (l, 0, 0)),
                  pl.BlockSpec((None, 1, n), lambda l: (l, 0, 0))],
        out_specs=pl.BlockSpec((None, 16, n), lambda l: (l, 0, 0)),
        out_shape=jax.ShapeDtypeStruct((2, 16, n), F32), compiler_params=_params())(c16, wm, bm)


def _mod_wgrad(name, c16, dm):
    n = dm.shape[2]
    tn = 512

    def kern(c_ref, d_ref, o_ref):
        cv = c_ref[...]
        a = cv * _sigmoid(cv)
        o_ref[...] = lax.dot_general(a, d_ref[...], (((0,), (0,)), ((), ())), preferred_element_type=F32,
                                     precision=lax.Precision.HIGHEST)

    return pl.pallas_call(
        kern, name=name, grid=(2, n // tn),
        in_specs=[pl.BlockSpec(c16.shape, lambda l, j: (0, 0)), pl.BlockSpec((None, 16, tn), lambda l, j: (l, 0, j))],
        out_specs=pl.BlockSpec((None, D_MODEL, tn), lambda l, j: (l, 0, j)),
        out_shape=jax.ShapeDtypeStruct((2, D_MODEL, n), F32), compiler_params=_params())(c16, dm)


def _cctx_grad(name, parts, c_ctx):
    def kern(p_ref, c_ref, o_ref):
        d = p_ref[0, 0:1, :] + p_ref[2, 0:1, :] + p_ref[4, 0:1, :] + p_ref[6, 0:1, :]
        cv = c_ref[...]
        sg = _sigmoid(cv)
        o_ref[...] = d * (sg * (1.0 + cv * (1.0 - sg)))

    return pl.pallas_call(kern, name=name, out_shape=jax.ShapeDtypeStruct((1, D_MODEL), F32),
                          compiler_params=_params())(parts, c_ctx)


def _sum_slots(name, g, tr):
    n, rows, cols = g.shape

    def kern(g_ref, o_ref):
        acc = g_ref[0].astype(F32)
        for i in range(1, n):
            acc = acc + g_ref[i].astype(F32)
        o_ref[...] = acc

    return pl.pallas_call(kern, name=name, grid=(rows // tr,),
                          in_specs=[pl.BlockSpec((n, tr, cols), lambda i: (0, i, 0))],
                          out_specs=pl.BlockSpec((tr, cols), lambda i: (i, 0)),
                          out_shape=jax.ShapeDtypeStruct((rows, cols), F32), compiler_params=_params())(g)


def _sum_into(name, g, cvec, n_layers, layer, prev=None):
    n, r, cc = g.shape
    tr = _row_tile(r, cc, 512 * 1024)

    def kern(c_ref, g_ref, *rest):
        acc = g_ref[0].astype(F32)
        for i in range(1, n):
            acc = acc + g_ref[i].astype(F32)
        rest[-1][...] = acc

    in_specs = [pl.BlockSpec((n, tr, cc), lambda i, c_ref: (0, i, 0))]
    args = [cvec, g]
    aliases = {}
    if prev is not None:
        in_specs.append(_ANY)
        args.append(prev)
        aliases = {2: 0}
    grid_spec = pltpu.PrefetchScalarGridSpec(
        num_scalar_prefetch=1, grid=(r // tr,), in_specs=in_specs,
        out_specs=pl.BlockSpec((None, None, tr, cc), lambda i, c_ref: (layer, c_ref[0], i, 0)))
    return pl.pallas_call(kern, name=name, grid_spec=grid_spec,
                          out_shape=jax.ShapeDtypeStruct((n_layers, 2, r, cc), F32),
                          input_output_aliases=aliases, compiler_params=_params())(*args)


def _row_tile(rows, cols, max_bytes=1024 * 1024):
    if rows * cols * 4 <= 2 * max_bytes:
        return rows
    best = None
    for t in range(16, rows + 1, 16):
        if rows % t == 0 and t * cols * 4 <= max_bytes:
            best = t
    assert best is not None, (rows, cols)
    return best


def _adamw(name, w, g, m, v, gather=()):
    shape = w.shape
    cols = shape[-1]
    rows = w.size // cols
    tr = _row_tile(rows, cols)
    ng = len(gather)
    last = rows // tr - 1

    def kern(w_ref, g_ref, m_ref, v_ref, *rest):
        d_ref, nm_ref, nv_ref = rest[ng:ng + 3]
        if ng:
            start, forward, finish = _gather_phases(rest[:ng], rest[ng + 3:2 * ng + 3], *rest[2 * ng + 3:])
            pl.when(pl.program_id(0) == 0)(start)
            pl.when(pl.program_id(0) == last // 2)(forward)
        gv = g_ref[...]
        m2 = ADAM_B1 * m_ref[...] + (1.0 - ADAM_B1) * gv
        v2 = ADAM_B2 * v_ref[...] + (1.0 - ADAM_B2) * (gv * gv)
        m_hat = m2 / (1.0 - ADAM_B1 ** ADAM_STEP)
        v_hat = v2 / (1.0 - ADAM_B2 ** ADAM_STEP)
        d_ref[...] = -ADAM_LR * (m_hat / (jnp.sqrt(v_hat) + ADAM_EPS) + ADAM_WD * w_ref[...])
        nm_ref[...] = m2
        nv_ref[...] = v2
        if ng:
            pl.when(pl.program_id(0) == last)(finish)

    spec = pl.BlockSpec((tr, cols), lambda i: (i, 0))
    sds = jax.ShapeDtypeStruct((rows, cols), F32)
    outs = pl.pallas_call(
        kern, name=name, grid=(rows // tr,), in_specs=[spec] * 4 + [_ANY] * ng, out_specs=[spec] * 3 + [_ANY] * ng,
        out_shape=[sds] * 3 + [jax.ShapeDtypeStruct((N_DEV,) + a.shape, a.dtype) for a in gather],
        scratch_shapes=_gather_scratch(ng) if ng else [], compiler_params=_params())(
        w.reshape(rows, cols), g.reshape(rows, cols), m.reshape(rows, cols), v.reshape(rows, cols), *gather)
    return tuple(o.reshape(shape) for o in outs[:3]) + tuple(outs[3:])


_ANY = pl.BlockSpec(memory_space=pl.ANY)


def _mesh_pos():
    return lax.axis_index("x"), lax.axis_index("y"), lax.axis_index("c")


def _gather_phases(srcs, outs, send_sems, recv_sems, local_sems):
    n = len(srcs)
    x, y, c = _mesh_pos()
    me = (x, y, c)
    sibling = (x, y, 1 - c)
    chips = [(1 - x, y), (x, 1 - y), (1 - x, 1 - y)]

    def slot(px, py, pc):
        return 4 * px + 2 * py + pc

    def copy(t, k, s, to, src=None):
        dst = outs[t].at[s]
        return pltpu.make_async_remote_copy(src_ref=dst if src is None else src, dst_ref=dst,
                                            send_sem=send_sems.at[t, k], recv_sem=recv_sems.at[t, k],
                                            device_id=to, device_id_type=MESH)

    def mine(t):
        return pltpu.make_async_copy(srcs[t], outs[t].at[slot(*me)], local_sems.at[t])

    def first(t):
        return [copy(t, 0, slot(*me), sibling, src=srcs[t])] + [
            copy(t, 1 + j, slot(*me), (px, py, c), src=srcs[t]) for j, (px, py) in enumerate(chips)]

    def passed(t, j):
        px, py = chips[j]
        return copy(t, 4 + j, slot(px, py, c), sibling)

    def start():
        for t in range(n):
            mine(t).start()
        for t in range(n):
            for cp in first(t):
                cp.start()

    def forward():
        for j, (px, py) in enumerate(chips):
            for t in range(n):
                copy(t, 1 + j, slot(px, py, c), me).wait_recv()
                passed(t, j).start()

    def finish():
        for t in range(n):
            copy(t, 0, slot(x, y, 1 - c), me).wait_recv()
            for j, (px, py) in enumerate(chips):
                copy(t, 4 + j, slot(px, py, 1 - c), me).wait_recv()
        for t in range(n):
            for cp in first(t) + [passed(t, j) for j in range(len(chips))]:
                cp.wait_send()
            mine(t).wait()

    return start, forward, finish


def _gather_scratch(n):
    return [pltpu.SemaphoreType.DMA((n, 7)), pltpu.SemaphoreType.DMA((n, 7)), pltpu.SemaphoreType.DMA((n,))]


def _allgather(name, items):
    n = len(items)

    def body(*refs):
        start, forward, finish = _gather_phases(refs[:n], refs[n:2 * n], *refs[2 * n:])
        start()
        forward()
        finish()

    return pl.pallas_call(
        body, name=name, in_specs=[_ANY] * n, out_specs=[_ANY] * n,
        out_shape=[jax.ShapeDtypeStruct((N_DEV,) + a.shape, a.dtype) for a in items],
        scratch_shapes=_gather_scratch(n), compiler_params=_params())(*items)


def _rs_pair_send(name, gs):
    n = len(gs)

    def body(*refs):
        srcs = refs[:n]
        bufs = refs[n:2 * n]
        send_sems, recv_sems = refs[2 * n:]
        x, y, c = _mesh_pos()
        copies = []
        for t in range(n):
            for k in range(N_CHIP):
                copies.append(pltpu.make_async_remote_copy(
                    src_ref=srcs[t].at[k, 1 - c], dst_ref=bufs[t].at[k], send_sem=send_sems.at[t, k],
                    recv_sem=recv_sems.at[t, k], device_id=(x, y, 1 - c), device_id_type=MESH))
        for cp in copies:
            cp.start()
        for cp in copies:
            cp.wait()

    return pl.pallas_call(
        body, name=name, in_specs=[_ANY] * n, out_specs=[_ANY] * n,
        out_shape=[jax.ShapeDtypeStruct((N_CHIP,) + g.shape[2:], g.dtype) for g in gs],
        scratch_shapes=[pltpu.SemaphoreType.DMA((n, N_CHIP)), pltpu.SemaphoreType.DMA((n, N_CHIP))],
        compiler_params=_params())(*gs)


def _rs_pair_add(name, g, buf, cvec):
    _, _, r, cc = g.shape
    tr = _row_tile(r, cc, 2 * 1024 * 1024)

    def kern(c_ref, g_ref, b_ref, o_ref):
        o_ref[...] = (g_ref[...] + b_ref[...]).astype(BF16)

    grid_spec = pltpu.PrefetchScalarGridSpec(
        num_scalar_prefetch=1, grid=(N_CHIP, r // tr),
        in_specs=[pl.BlockSpec((None, None, tr, cc), lambda k, i, c_ref: (k, c_ref[0], i, 0)),
                  pl.BlockSpec((None, tr, cc), lambda k, i, c_ref: (k, i, 0))],
        out_specs=pl.BlockSpec((None, tr, cc), lambda k, i, c_ref: (k, i, 0)))
    return pl.pallas_call(kern, name=name, grid_spec=grid_spec,
                          out_shape=jax.ShapeDtypeStruct((N_CHIP, r, cc), BF16),
                          compiler_params=_params())(cvec, g, buf)


def _exchange_phases(srcs, bufs, send_sems, recv_sems, local_sems):
    n = len(srcs)
    x, y, c = _mesh_pos()
    kme = 2 * x + y
    chips = [(1 - x, y), (x, 1 - y), (1 - x, 1 - y)]

    def copies():
        local = [pltpu.make_async_copy(srcs[t].at[kme], bufs[t].at[kme], local_sems.at[t]) for t in range(n)]
        remote = []
        for t in range(n):
            for j, (px, py) in enumerate(chips):
                remote.append(pltpu.make_async_remote_copy(
                    src_ref=srcs[t].at[2 * px + py], dst_ref=bufs[t].at[kme], send_sem=send_sems.at[t, j],
                    recv_sem=recv_sems.at[t, j], device_id=(px, py, c), device_id_type=MESH))
        return local, remote

    def start():
        local, remote = copies()
        for cp in local + remote:
            cp.start()

    def finish():
        local, remote = copies()
        for cp in remote + local:
            cp.wait()

    return start, finish


def _exchange_scratch(n):
    return [pltpu.SemaphoreType.DMA((n, 3)), pltpu.SemaphoreType.DMA((n, 3)), pltpu.SemaphoreType.DMA((n,))]


def _rs_chip_exchange(name, ss):
    n = len(ss)

    def body(*refs):
        start, finish = _exchange_phases(refs[:n], refs[n:2 * n], *refs[2 * n:])
        start()
        finish()

    return pl.pallas_call(
        body, name=name, in_specs=[_ANY] * n, out_specs=[_ANY] * n,
        out_shape=[jax.ShapeDtypeStruct(s.shape, s.dtype) for s in ss],
        scratch_shapes=_exchange_scratch(n), compiler_params=_params())(*ss)


def _rs_pair_share(name, dsts):
    nd = len(dsts)

    def body(*refs):
        ins = refs[:nd]
        outs = refs[nd:2 * nd]
        send_sems, recv_sems = refs[2 * nd:]
        x, y, c = _mesh_pos()
        copies = []
        for d in range(nd):
            for l in range(dsts[d].shape[0]):
                copies.append(pltpu.make_async_remote_copy(
                    src_ref=ins[d].at[l, c], dst_ref=outs[d].at[l, c], send_sem=send_sems.at[d, l],
                    recv_sem=recv_sems.at[d, l], device_id=(x, y, 1 - c), device_id_type=MESH))
        for cp in copies:
            cp.start()
        for cp in copies:
            cp.wait()

    return pl.pallas_call(
        body, name=name, in_specs=[_ANY] * nd, out_specs=[_ANY] * nd,
        out_shape=[jax.ShapeDtypeStruct(a.shape, a.dtype) for a in dsts],
        input_output_aliases={d: d for d in range(nd)},
        scratch_shapes=[pltpu.SemaphoreType.DMA((nd, 2)), pltpu.SemaphoreType.DMA((nd, 2))],
        compiler_params=_params())(*dsts)


def _rope_tables(seq, n_ctx):
    t = jnp.arange(seq)
    row = (t // GRID_W).astype(F32)
    col = (t % GRID_W).astype(F32)
    inv = ROPE_THETA ** (-jnp.arange(0, HEAD_DIM // 2, 2, dtype=F32) / (HEAD_DIM // 2))
    ang_r = row[:, None] * inv[None, :]
    ang_c = col[:, None] * inv[None, :]
    cos = jnp.concatenate([jnp.cos(ang_r)] * 2 + [jnp.cos(ang_c)] * 2, axis=1)
    sin = jnp.concatenate([-jnp.sin(ang_r), jnp.sin(ang_r), -jnp.sin(ang_c), jnp.sin(ang_c)], axis=1)
    cos = jnp.concatenate([jnp.ones((n_ctx, HEAD_DIM), F32), cos], axis=0)
    sin = jnp.concatenate([jnp.zeros((n_ctx, HEAD_DIM), F32), sin], axis=0)
    return jnp.tile(cos, (1, N_KV_HEADS)), jnp.tile(sin, (1, N_KV_HEADS))


def _to_heads(a, nh):
    return a.reshape(a.shape[0], nh, HEAD_DIM).transpose(1, 0, 2)


def _sample_step(x, ctx, tgt, mod, cmod, W, pending, cvec):
    W = dict(W)
    seq = x.shape[0]
    n_ctx = ctx.shape[0]
    tm = min(256, seq)
    tmc = min(256, n_ctx)
    tmm = 512 if seq % 512 == 0 else tm
    tmw = 1024 if seq % 1024 == 0 else tmm
    tmr = (n_ctx + seq) // 4 if (n_ctx + seq) % 64 == 0 else tm
    tq = min(256, seq)
    assert n_ctx % tm == 0 and seq % tm == 0

    def mv(l, j):
        return mod[l, j * D_MODEL:(j + 1) * D_MODEL].reshape(1, D_MODEL)

    csh1 = cmod[:D_MODEL].reshape(1, D_MODEL)
    csc1 = cmod[D_MODEL:2 * D_MODEL].reshape(1, D_MODEL)
    g_mix = [W['g_mix'][l:l + 1] for l in range(2)]
    g_ffn = [W['g_ffn'][l:l + 1] for l in range(2)]
    cos, sin = _rope_tables(seq, n_ctx)
    gq = jnp.tile(W['q_gain'], (1, N_Q_HEADS))
    gk = jnp.tile(W['k_gain'], (1, N_KV_HEADS))
    bdq = jnp.kron(jnp.eye(N_Q_HEADS, dtype=F32), jnp.ones((HEAD_DIM, HEAD_DIM), F32)).astype(BF16)
    bdk = bdq[:KV_W, :KV_W]
    w_sp = W['w_sp'].astype(BF16)
    w_sp_t = w_sp.transpose(0, 2, 1)
    bsp = jnp.broadcast_to(W['b_sp'].T[:, :, None], (CHUNK, N_SG, CHUNK)).reshape(CHUNK, SG_W)

    def ffn_fwd(l, h_mid):
        xf = _modnorm_fwd(f'l{l}_ffn_norm', h_mid, g_ffn[l], mv(l, 4), mv(l, 3), tm)
        gu, a = _ffn_in_swiglu(f'l{l}_ffn_in', xf, W['ffn_in'][l], tmw)
        f, h_out = _mm_rows_res(f'l{l}_ffn_out', a, W['ffn_out'][l], h_mid, mv(l, 5), tmm)
        return h_out, (h_mid, xf, gu, a, f)

    def ffn_bwd(l, dh_out, saved):
        h_mid, xf, gu, a, f = saved
        dy, d_gt2, _ = _resgate_bwd(f'l{l}_ffn_res_bwd', dh_out, f, mv(l, 5), tm)
        dgu = _ffn_out_dx_swiglu(f'l{l}_ffn_out_dx', dy, W['ffn_out'][l], gu, tmw)
        d_wo = _mm_wgrad(f'l{l}_ffn_out_dw', a, dy, tmw)
        dxf = _ffn_in_dx(f'l{l}_ffn_in_dx', dgu, W['ffn_in'][l], tmm)
        d_wi = _mm_halves_wgrad_blk(f'l{l}_ffn_in_dw', xf, dgu, tmw)
        dh_mid, d_sh2, d_sc2, d_g = _modnorm_bwd(f'l{l}_ffn_norm_bwd', h_mid, [_ri(dxf)], dh_out, g_ffn[l],
                                                 mv(l, 4), tm)
        return dh_mid, d_wi, d_wo, d_g, (d_sh2, d_sc2, d_gt2)

    xm0 = _modnorm_fwd('l0_mix_norm', x, g_mix[0], mv(0, 1), mv(0, 0), tm)
    xc0 = _modnorm_fwd('l0_ctx_norm', ctx, g_mix[0], csc1, csh1, tmc)
    xall = jnp.concatenate([xc0, xm0], axis=0)
    p_qkv = _mm_rows('l0_qkv', xall, W['w_qkv'], tmr)
    p_sg = _mm_rows('l0_sg_in', xm0, W['w_sgp'], tmw)
    q_all, k_all, v_all = _qk_prep_fwd('l0_qk_prep', p_qkv, cos, sin, gq, gk, bdq, bdk, tm)
    kh = _to_heads(k_all, N_KV_HEADS)
    vh = _to_heads(v_all, N_KV_HEADS)
    v_ones = jnp.concatenate([vh, jnp.ones(vh.shape[:2] + (1,), BF16),
                              jnp.zeros(vh.shape[:2] + (HEAD_DIM - 1,), BF16)], axis=2)
    o, lse, *gw = _attn_fwd('l0_attn', q_all, kh, v_ones, n_ctx, tq, gather=pending)
    W['w_out'] = gw[0].reshape(D_MODEL, D_MODEL)
    W['ffn_in'] = [gw[1].reshape(N_CHIP, D_MODEL, FF_BLK), gw[5].reshape(N_CHIP, D_MODEL, FF_BLK)]
    W['ffn_out'] = [gw[2].reshape(D_FF, D_MODEL), gw[6].reshape(D_FF, D_MODEL)]
    W['pw1'] = gw[3].reshape(N_CHIP, D_MODEL, PW1_BLK).transpose(1, 0, 2).reshape(D_MODEL, 2 * D_MODEL)
    W['pw2'] = gw[4].reshape(D_MODEL, D_MODEL)
    sg = _sg_fwd('l0_sg', p_sg, w_sp, bsp, tm)
    cat = jnp.concatenate([o.astype(BF16), sg], axis=1)
    y0, h1 = _mm_rows_res('l0_out', cat, W['w_out'], x, mv(0, 2), tmw)
    h2, ffn0 = ffn_fwd(0, h1)

    xm1 = _modnorm_fwd('l1_mix_norm', h2, g_mix[1], mv(1, 1), mv(1, 0), tm)
    ag = _mm_rows('l1_pw1', xm1, W['pw1'], tmm)
    hg = _glu_fwd('l1_glu', ag, W['b_pw1'], tm)
    hc = _dwconv('l1_conv', hg, W['w_dw'], W['b_dw'])
    s1 = _lnsilu_fwd('l1_ln_silu', hc, W['ln_g'], W['ln_b'], tm)
    y1, h3 = _mm_rows_res('l1_pw2', s1, W['pw2'], h2, mv(1, 2), tmw, bias=W['b_pw2'])
    h4, ffn1 = ffn_fwd(1, h3)

    dh4, d_g_final, loss_vec = _final_loss('final_loss', h4, tgt, W['g_final'], tm)

    dh3, d_wi1, d_wo1, d_gffn1, dmod1_ffn = ffn_bwd(1, dh4, ffn1)
    dy1, d_gt1_1, d_b_pw2 = _resgate_bwd('l1_mix_res_bwd', dh3, y1, mv(1, 2), tm)
    ds1 = _mm_rows('l1_pw2_dx', dy1, W['pw2'], tmw, trans_b=True)
    d_pw2 = _mm_wgrad('l1_pw2_dw', s1, dy1, tmw)
    dhc, d_ln_g, d_ln_b = _lnsilu_bwd('l1_ln_silu_bwd', hc, ds1, W['ln_g'], W['ln_b'], tm)
    dhg = _dwconv('l1_conv_dx', dhc, W['w_dw'][::-1], jnp.zeros((1, D_MODEL), F32))
    d_w_dw, d_b_dw = _dwconv_wgrad('l1_conv_dw', hg, dhc)
    dag, d_b_pw1 = _glu_bwd('l1_glu_bwd', ag, dhg, W['b_pw1'], tm)
    dxm1 = _mm_rows('l1_pw1_dx', dag, W['pw1'], tmm, trans_b=True)
    d_pw1 = _mm_wgrad('l1_pw1_dw', xm1, dag, tmw).reshape(D_MODEL, N_CHIP, PW1_BLK).transpose(1, 0, 2)
    dh2, d_sh1_1, d_sc1_1, d_gmix1 = _modnorm_bwd('l1_mix_norm_bwd', h2, [_ri(dxm1)], dh3, g_mix[1], mv(1, 1), tm)

    dh1, d_wi0, d_wo0, d_gffn0, dmod0_ffn = ffn_bwd(0, dh2, ffn0)
    dy0, d_gt1_0, _ = _resgate_bwd('l0_mix_res_bwd', dh1, y0, mv(0, 2), tm)
    dcat = _mm_rows('l0_out_dx', dy0, W['w_out'], tmw, trans_b=True)
    d_w_out = _mm_wgrad('l0_out_dw', cat, dy0, tmw)

    def blocked(a):
        cols = a.shape[-1]
        return a.reshape(N_CHIP, 2, a.size // (2 * N_CHIP * cols), cols)

    gs = [blocked(g) for g in (d_wi0, d_wi1, d_wo0, d_wo1, d_w_out, d_pw1, d_pw2)]
    sib = _rs_pair_send('rs_pair_send', gs)
    ss = [_rs_pair_add(f'rs_pair_add{t}', gs[t], sib[t], cvec) for t in range(len(gs))]
    dq, dk, dv, *xs = _attn_bwd('l0_attn_bwd', q_all, kh, vh, o, lse, dcat, n_ctx, tq, exchange=ss)
    dk, dv = [t.transpose(2, 0, 1).reshape(n_ctx + seq, KV_W) for t in (dk, dv)]
    dp_qkv, d_gq, d_gk = _qk_prep_bwd('l0_qk_prep_bwd', p_qkv, cos, sin, dq, n_ctx // tm, dk, dv, gq, gk, bdq, bdk,
                                      tm)
    dp_sg, d_w_sp, d_bsp = _sg_bwd('l0_sg_bwd', p_sg, dcat, w_sp, w_sp_t, bsp, tm)
    d_w_qkv = _mm_wgrad('l0_qkv_dw', xall, dp_qkv, tmr)
    d_w_sgp = _mm_wgrad('l0_sg_in_dw', xm0, dp_sg, tmw)
    dxall = _mm_rows('l0_qkv_dx', dp_qkv, W['w_qkv'], tmr, trans_b=True)
    dxm_sg = _mm_rows('l0_sg_in_dx', dp_sg, W['w_sgp'], tmw, trans_b=True)
    dx, d_sh1_0, d_sc1_0, d_gmix0 = _modnorm_bwd('l0_mix_norm_bwd', x, [_ri(dxall, None, 0, n_ctx // tm), _ri(dxm_sg)],
                                                 dh1, g_mix[0], mv(0, 1), tm)
    _, d_csh1, d_csc1, d_gmix0c = _modnorm_bwd('l0_ctx_norm_bwd', ctx, [_ri(dxall)], None, g_mix[0], csc1, tmc)

    g_in = blocked(jnp.concatenate([d_w_qkv, d_w_sgp], axis=1).reshape(D_MODEL, N_CHIP, -1).transpose(1, 0, 2))
    sib_in = _rs_pair_send('rs_pair_send_w_in', [g_in])
    xs_in = _rs_chip_exchange('rs_chip_exchange_w_in', [_rs_pair_add('rs_pair_add_w_in', g_in, sib_in[0], cvec)])
    t_ffn_in = _sum_into('rs_sum_ffn_in1', xs[1], cvec, 2, 1, prev=_sum_into('rs_sum_ffn_in0', xs[0], cvec, 2, 0))
    t_ffn_out = _sum_into('rs_sum_ffn_out1', xs[3], cvec, 2, 1, prev=_sum_into('rs_sum_ffn_out0', xs[2], cvec, 2, 0))
    reduced = _rs_pair_share('rs_pair_share', [
        t_ffn_in, t_ffn_out, _sum_into('rs_sum_w_in', xs_in[0], cvec, 1, 0), _sum_into('rs_sum_w_out', xs[4], cvec, 1, 0),
        _sum_into('rs_sum_pw1', xs[5], cvec, 1, 0), _sum_into('rs_sum_pw2', xs[6], cvec, 1, 0)])
    big = dict(zip(['w_ffn_in', 'w_ffn_out', 'w_in', 'w_out', 'w_pw1', 'w_pw2'], reduced))

    zero = jnp.zeros((1, D_MODEL), F32)
    dmod = jnp.concatenate([d_sh1_0, d_sc1_0, d_gt1_0, *dmod0_ffn, d_sh1_1, d_sc1_1, d_gt1_1, *dmod1_ffn], axis=0)
    dcmod = jnp.concatenate([d_csh1, d_csc1, zero, zero, zero, zero], axis=0)
    grads = dict(
        g_mix=jnp.concatenate([d_gmix0 + d_gmix0c, d_gmix1], axis=0),
        g_ffn=jnp.concatenate([d_gffn0, d_gffn1], axis=0), g_final=d_g_final, gq=d_gq, gk=d_gk,
        w_sp=d_w_sp, bsp=d_bsp, b_pw1=d_b_pw1, w_dw=d_w_dw, b_dw=d_b_dw, ln_g=d_ln_g, ln_b=d_ln_b, b_pw2=d_b_pw2,
        dmod=dmod, dcmod=dcmod)
    return loss_vec, dx, grads, big


def kernel(x, c, ctx, c_ctx, w_mod, b_mod, g_mix, g_ffn, w_ffn_in, w_ffn_out, w_in, q_gain, k_gain, w_sp, b_sp, w_out, w_pw1, b_pw1, w_dw, b_dw, ln_g, ln_b, w_pw2, b_pw2, g_final, loss_target, m_c_ctx, m_w_mod, m_b_mod, m_g_mix, m_g_ffn, m_w_ffn_in, m_w_ffn_out, m_w_in, m_q_gain, m_k_gain, m_w_sp, m_b_sp, m_w_out, m_w_pw1, m_b_pw1, m_w_dw, m_b_dw, m_ln_g, m_ln_b, m_w_pw2, m_b_pw2, m_g_final, v_c_ctx, v_w_mod, v_b_mod, v_g_mix, v_g_ffn, v_w_ffn_in, v_w_ffn_out, v_w_in, v_q_gain, v_k_gain, v_w_sp, v_b_sp, v_w_out, v_w_pw1, v_b_pw1, v_w_dw, v_b_dw, v_ln_g, v_ln_b, v_w_pw2, v_b_pw2, v_g_final):
    given = dict(locals())
    ix, iy, ic = _mesh_pos()
    chip = 2 * ix + iy
    me = 2 * chip + ic
    d = D_MODEL
    q4 = d // N_CHIP

    small = jnp.concatenate([c.reshape(4, q4), b_pw1.reshape(2, q4), w_dw[0], b_dw, ln_g, ln_b, b_pw2,
                             jnp.zeros((7, q4), F32)], axis=0)
    def my_half(a):
        r = a.shape[0] // 2
        return lax.dynamic_slice_in_dim(a, ic * r, r, axis=0).astype(BF16)

    sgath, g_w_in = _allgather('ag_first', [small, my_half(w_in[0])])
    c_all = sgath[:, 0:4].reshape(N_DEV, d)
    per_chip = sgath[0::2]
    b_pw1_f = per_chip[:, 4:6].reshape(1, 2 * d)
    w_dw_f = per_chip[:, 6:6 + CONV_W].transpose(1, 0, 2).reshape(CONV_W, d)
    b_dw_f, ln_g_f, ln_b_f, b_pw2_f = [per_chip[:, 37 + i].reshape(1, d) for i in range(4)]

    c16 = jnp.concatenate([c_all, c_ctx[None, :], jnp.zeros((7, d), F32)], axis=0)
    bm_sh = lax.dynamic_slice_in_dim(b_mod, chip * MOD_BLK, MOD_BLK, axis=1).reshape(2, 1, MOD_BLK)
    mod_piece = _mod_fwd('mod_fwd', c16, w_mod, bm_sh)
    mgath = _allgather('ag_mod', [mod_piece.reshape(32, MOD_BLK)])[0]
    mod_all = mgath[0::2].reshape(N_CHIP, 2, 16, MOD_BLK).transpose(1, 2, 0, 3).reshape(2, 16, MOD_W)
    mod_me = lax.dynamic_index_in_dim(mod_all, me, axis=1, keepdims=False)
    cmod = mod_all[0, N_DEV]

    w_in_f = g_w_in.reshape(N_CHIP, d, -1).transpose(1, 0, 2).reshape(d, -1)
    W = dict(
        w_qkv=w_in_f[:, :QKV_W], w_sgp=w_in_f[:, QKV_W:],
        g_mix=g_mix, g_ffn=g_ffn, g_final=g_final.reshape(1, d), q_gain=q_gain, k_gain=k_gain, w_sp=w_sp[0],
        b_sp=b_sp[0], b_pw1=b_pw1_f, w_dw=w_dw_f, b_dw=b_dw_f, ln_g=ln_g_f, ln_b=ln_b_f, b_pw2=b_pw2_f)
    pending = [my_half(w_out[0]), my_half(w_ffn_in[0]), my_half(w_ffn_out[0]), my_half(w_pw1[0]), my_half(w_pw2[0]),
               my_half(w_ffn_in[1]), my_half(w_ffn_out[1])]
    cvec = jnp.reshape(ic, (1,)).astype(jnp.int32)

    loss_vec, dx, G, big = _sample_step(x[0], ctx[0], loss_target[0], mod_me, cmod, W, pending, cvec)
    loss = lax.psum(jnp.sum(loss_vec), ("x", "y", "c"))

    pack = jnp.concatenate([
        G['g_mix'], G['g_ffn'], G['g_final'],
        jnp.concatenate([G['gq'], G['gk'], jnp.zeros((1, d - ATTN_W - KV_W), F32)], axis=1),
        G['b_pw1'].reshape(2, d), G['w_dw'], G['b_dw'], G['ln_g'], G['ln_b'], G['b_pw2'],
        G['dmod'], G['dcmod'], jnp.zeros((3, d), F32),
        G['w_sp'].reshape(64, d), G['bsp'].reshape(64, d)], axis=0)
    assert pack.shape == (192, d)
    grads = dict(big)
    deltas, new_m, new_v = {}, {}, {}

    def adamw(n, gather=()):
        grads[n] = grads[n].reshape(given[n].shape)
        deltas[n], new_m[n], new_v[n], *rest = _adamw(f'adamw_{n}', given[n], grads[n], given['m_' + n],
                                                     given['v_' + n], gather)
        return rest

    pg = adamw('w_ffn_in', gather=[pack])[0]
    s8 = _sum_slots('sum_small_grads', pg, 64)

    def chip_cols(a, width):
        return lax.dynamic_slice_in_dim(a, chip * width, width, axis=a.ndim - 1)

    grads['g_mix'] = s8[0:2]
    grads['g_ffn'] = s8[2:4]
    grads['g_final'] = s8[4]
    grads['q_gain'] = s8[5, :ATTN_W].reshape(N_Q_HEADS, HEAD_DIM).sum(axis=0)[None, :]
    grads['k_gain'] = s8[5, ATTN_W:ATTN_W + KV_W].reshape(N_KV_HEADS, HEAD_DIM).sum(axis=0)[None, :]
    grads['b_pw1'] = chip_cols(s8[6:8].reshape(1, 2 * d), 2 * q4)
    grads['w_dw'] = chip_cols(s8[8:8 + CONV_W], q4)[None]
    grads['b_dw'] = chip_cols(s8[39:40], q4)
    grads['ln_g'] = chip_cols(s8[40:41], q4)
    grads['ln_b'] = chip_cols(s8[41:42], q4)
    grads['b_pw2'] = chip_cols(s8[42:43], q4)
    grads['w_sp'] = s8[64:128].reshape(1, N_SG, CHUNK, CHUNK)
    grads['b_sp'] = s8[128:192].reshape(CHUNK, N_SG, CHUNK).sum(axis=-1).T[None]

    dm_rows = pg[:, 43:55].reshape(N_DEV, 2, MOD_W)
    ctx_row = s8[55:61].reshape(1, MOD_W)
    dm = jnp.stack([jnp.concatenate([dm_rows[:, 0], ctx_row, jnp.zeros((7, MOD_W), F32)], axis=0),
                    jnp.concatenate([dm_rows[:, 1], jnp.zeros((8, MOD_W), F32)], axis=0)], axis=0)
    grads['b_mod'] = jnp.stack([s8[43:49].reshape(MOD_W) + ctx_row[0], s8[49:55].reshape(MOD_W)], axis=0)
    dm_sh = chip_cols(dm, MOD_BLK)
    grads['w_mod'] = _mod_wgrad('mod_dw', c16, dm_sh)
    d_silu = _mm_rows('mod_dx', dm_sh[0], w_mod[0], 16, trans_b=True)
    parts = _allgather('ag_cctx', [d_silu[8:16]])[0]
    grads['c_ctx'] = _cctx_grad('cctx_grad', parts, c_ctx.reshape(1, d))[0]

    for n in WEIGHT_NAMES:
        if n != 'w_ffn_in':
            adamw(n)

    return (loss, dx[None], *[grads[n] for n in WEIGHT_NAMES], *[deltas[n] for n in WEIGHT_NAMES],
            *[new_m[n] for n in WEIGHT_NAMES], *[new_v[n] for n in WEIGHT_NAMES])
```

```python
import functools
import math

import jax
import jax.numpy as jnp
from jax import lax
from jax.experimental import pallas as pl
from jax.experimental.pallas import tpu as pltpu

F32 = jnp.float32
BF16 = jnp.bfloat16
MESH = pl.DeviceIdType.MESH

D_MODEL = 1024
HEAD_DIM = 64
N_Q_HEADS = 8
N_KV_HEADS = 2
Q_GROUP = N_Q_HEADS // N_KV_HEADS
ATTN_W = N_Q_HEADS * HEAD_DIM
KV_W = N_KV_HEADS * HEAD_DIM
QKV_W = ATTN_W + 2 * KV_W
SG_W = D_MODEL - ATTN_W
N_SG = 4
CHUNK = 128
GRID_W = 64
ROPE_THETA = 10000.0
CONV_W = 31
CONV_HALO = 16
D_FF = 2816
MOD_W = 6 * D_MODEL
N_CHIP = 4
N_DEV = 8
FF_BLK = 2 * D_FF // N_CHIP
PW1_BLK = 2 * D_MODEL // N_CHIP
MOD_BLK = MOD_W // N_CHIP
EPS = 1e-6
GELU_C = math.sqrt(2.0 / math.pi)
GELU_A = 0.044715

ADAM_LR = 0.001
ADAM_B1 = 0.9
ADAM_B2 = 0.999
ADAM_EPS = 1e-08
ADAM_WD = 0.01
ADAM_STEP = 10

VMEM_LIMIT_BYTES = 56 * 1024 * 1024

WEIGHT_NAMES = ['c_ctx', 'w_mod', 'b_mod', 'g_mix', 'g_ffn', 'w_ffn_in', 'w_ffn_out', 'w_in', 'q_gain', 'k_gain',
                'w_sp', 'b_sp', 'w_out', 'w_pw1', 'b_pw1', 'w_dw', 'b_dw', 'ln_g', 'ln_b', 'w_pw2', 'b_pw2',
                'g_final']


def _params():
    return pltpu.CompilerParams(vmem_limit_bytes=VMEM_LIMIT_BYTES)


def _ri(arr, width=None, col_block=0, row_off=0, row_fn=None):
    return (arr, arr.shape[1] if width is None else width, col_block,
            (lambda i: i + row_off) if row_fn is None else row_fn)


def _rowwise(name, body, n_tiles, tm, row_ins, vec_ins, row_outs, vec_outs):
    nri, nvi, nro = len(row_ins), len(vec_ins), len(row_outs)

    def kern(*refs):
        ri = refs[:nri]
        vi = refs[nri:nri + nvi]
        ro = refs[nri + nvi:nri + nvi + nro]
        vo = refs[nri + nvi + nro:]
        if vo:
            @pl.when(pl.program_id(0) == 0)
            def _():
                for r in vo:
                    r[...] = jnp.zeros(r.shape, r.dtype)
        body(ri, vi, ro, vo)

    in_specs = [pl.BlockSpec((tm, bw), lambda i, cb=cb, rf=rf: (rf(i), cb)) for (_, bw, cb, rf) in row_ins]
    in_specs += [pl.BlockSpec(a.shape, lambda i, nd=a.ndim: (0,) * nd) for a in vec_ins]
    out_specs = [pl.BlockSpec((tm, w), lambda i: (i, 0)) for (w, _) in row_outs]
    out_specs += [pl.BlockSpec(s, lambda i, nd=len(s): (0,) * nd) for s in vec_outs]
    out_shape = [jax.ShapeDtypeStruct((n_tiles * tm, w), dt) for (w, dt) in row_outs]
    out_shape += [jax.ShapeDtypeStruct(s, F32) for s in vec_outs]
    return pl.pallas_call(kern, name=name, grid=(n_tiles,), in_specs=in_specs, out_specs=out_specs,
                          out_shape=out_shape, compiler_params=_params())(*[a for (a, _, _, _) in row_ins], *vec_ins)


def _colsum(v):
    return jnp.sum(v, axis=0, keepdims=True)


def _rowmean(v):
    return jnp.mean(v, axis=-1, keepdims=True)


def _rms(h):
    r = lax.rsqrt(_rowmean(h * h) + EPS)
    return h * r, r


def _rms_bwd(n, r, gy):
    return r * (gy - n * _rowmean(gy * n))


def _ln_stats(v):
    xc = v - _rowmean(v)
    r = lax.rsqrt(_rowmean(xc * xc) + EPS)
    return xc * r, r


def _ln_bwd(xh, r, dxh):
    return r * (dxh - _rowmean(dxh) - xh * _rowmean(dxh * xh))


def _sigmoid(v):
    return 1.0 / (1.0 + jnp.exp(-v))


def _gelu(v):
    t = jnp.tanh(GELU_C * (v + GELU_A * (v * v * v)))
    return 0.5 * v * (1.0 + t), t


def _gelu_grad(v, t):
    return 0.5 * (1.0 + t) + 0.5 * v * (1.0 - t * t) * (GELU_C * (1.0 + 3.0 * GELU_A * (v * v)))


def _modnorm_fwd(name, h, g, sc, sh, tm):
    def body(ri, vi, ro, vo):
        n, _ = _rms(ri[0][...])
        ro[0][...] = ((n * vi[0][...]) * (1.0 + vi[1][...]) + vi[2][...]).astype(BF16)

    return _rowwise(name, body, h.shape[0] // tm, tm, [_ri(h)], [g, sc, sh], [(D_MODEL, BF16)], [])[0]


def _modnorm_bwd_tile(hv, dxm, gv, scv):
    n, r = _rms(hv)
    dy = dxm * (1.0 + scv)
    return _rms_bwd(n, r, dy * gv), _colsum(dxm), _colsum(dxm * (n * gv)), _colsum(dy * n)


def _mm_norm_bwd(name, mm_args, mm_specs, dx_fn, h, dh_in, g, sc, tm):
    nmm = len(mm_args)

    def kern(*refs):
        h_ref, dh_ref, g_ref, sc_ref, o_ref, dsh_ref, dsc_ref, dg_ref = refs[nmm:]

        @pl.when(pl.program_id(0) == 0)
        def _():
            for r in (dsh_ref, dsc_ref, dg_ref):
                r[...] = jnp.zeros(r.shape, F32)

        dx, dsh, dsc, dg = _modnorm_bwd_tile(h_ref[...], dx_fn(*refs[:nmm]), g_ref[...], sc_ref[...])
        o_ref[...] = dh_ref[...] + dx
        dsh_ref[...] += dsh
        dsc_ref[...] += dsc
        dg_ref[...] += dg

    row = pl.BlockSpec((tm, D_MODEL), lambda i: (i, 0))
    vec = pl.BlockSpec((1, D_MODEL), lambda i: (0, 0))
    return pl.pallas_call(
        kern, name=name, grid=(h.shape[0] // tm,), in_specs=list(mm_specs) + [row, row, vec, vec],
        out_specs=[row, vec, vec, vec],
        out_shape=[jax.ShapeDtypeStruct(h.shape, F32)] + [jax.ShapeDtypeStruct((1, D_MODEL), F32)] * 3,
        compiler_params=_params())(*mm_args, h, dh_in, g, sc)


def _modnorm_bwd(name, h, dxs, dh_in, g, sc, tm):
    ndx = len(dxs)

    def body(ri, vi, ro, vo):
        dxm = ri[1][...].astype(F32)
        for r in ri[2:1 + ndx]:
            dxm = dxm + r[...].astype(F32)
        dx, dsh, dsc, dg = _modnorm_bwd_tile(ri[0][...], dxm, vi[0][...], vi[1][...])
        vo[0][...] += dsh
        vo[1][...] += dsc
        vo[2][...] += dg
        if dh_in is not None:
            ro[0][...] = ri[1 + ndx][...] + dx

    row_ins = [_ri(h)] + list(dxs) + ([_ri(dh_in)] if dh_in is not None else [])
    row_outs = [(D_MODEL, F32)] if dh_in is not None else []
    outs = _rowwise(name, body, h.shape[0] // tm, tm, row_ins, [g, sc], row_outs, [(1, D_MODEL)] * 3)
    if dh_in is None:
        return (None, *outs)
    return tuple(outs)


def _resgate_bwd(name, dh, y, gt, tm):
    def body(ri, vi, ro, vo):
        d = ri[0][...]
        dy = d * vi[0][...]
        ro[0][...] = dy.astype(BF16)
        vo[0][...] += _colsum(d * ri[1][...])
        vo[1][...] += _colsum(dy)

    return _rowwise(name, body, dh.shape[0] // tm, tm, [_ri(dh), _ri(y)], [gt], [(D_MODEL, BF16)],
                    [(1, D_MODEL)] * 2)


def _final_loss(name, h, tgt, g, tm):
    def body(ri, vi, ro, vo):
        gv = vi[0][...]
        n, r = _rms(ri[0][...])
        e = n * gv - ri[1][...]
        vo[1][...] += _colsum(e * e) * (0.5 / D_MODEL)
        dout = e * (1.0 / D_MODEL)
        vo[0][...] += _colsum(dout * n)
        ro[0][...] = _rms_bwd(n, r, dout * gv)

    return _rowwise(name, body, h.shape[0] // tm, tm, [_ri(h), _ri(tgt)], [g], [(D_MODEL, F32)],
                    [(1, D_MODEL)] * 2)


def _head_mean(v, bd):
    hi = v.astype(BF16)
    lo = (v - hi.astype(F32)).astype(BF16)
    s = jnp.dot(hi, bd, preferred_element_type=F32) + jnp.dot(lo, bd, preferred_element_type=F32)
    return s * (1.0 / HEAD_DIM)


def _swap16(v):
    w = v.shape[1]
    lane = lax.broadcasted_iota(jnp.int32, v.shape, 1)
    return jnp.where((lane & 16) == 0, pltpu.roll(v, w - 16, 1), pltpu.roll(v, 16, 1))


def _rope(v, cos, sin):
    return v * cos + _swap16(v) * sin


def _rope_bwd(d, cos, sin):
    return d * cos + _swap16(d * sin)


def _q_wide(t):
    return jnp.concatenate([t] * (ATTN_W // KV_W), axis=1)


def _qk_prep_fwd(name, p_qkv, cos, sin, gq, gk, bdq, bdk, tm):
    def body(ri, vi, ro, vo):
        pq = ri[0][...]
        pkv = ri[1][...]
        cosv = ri[2][...]
        sinv = ri[3][...]
        rq = lax.rsqrt(_head_mean(pq * pq, vi[2][...]) + EPS)
        ro[0][...] = (_rope(pq * rq * vi[0][...], _q_wide(cosv), _q_wide(sinv)) * (HEAD_DIM ** -0.5)).astype(BF16)
        pk = pkv[:, :KV_W]
        rk = lax.rsqrt(_head_mean(pk * pk, vi[3][...]) + EPS)
        ro[1][...] = _rope(pk * rk * vi[1][...], cosv, sinv).astype(BF16)
        ro[2][...] = pkv[:, KV_W:].astype(BF16)

    return _rowwise(name, body, p_qkv.shape[0] // tm, tm,
                    [_ri(p_qkv, ATTN_W, 0), _ri(p_qkv, 2 * KV_W, 2), _ri(cos), _ri(sin)], [gq, gk, bdq, bdk],
                    [(ATTN_W, BF16), (KV_W, BF16), (KV_W, BF16)], [])


def _qk_prep_bwd(name, p_qkv, cos, sin, dq, n_ctx_tiles, dk, dv, gq, gk, bdq, bdk, tm):
    def norm_rope_bwd(p, gain, bd, cosv, sinv, dout):
        r = lax.rsqrt(_head_mean(p * p, bd) + EPS)
        n = p * r
        dqn = _rope_bwd(dout, cosv, sinv)
        gy = dqn * gain
        return r * (gy - n * _head_mean(gy * n, bd)), _colsum(dqn * n)

    def body(ri, vi, ro, vo):
        pkv = ri[1][...]
        cosv = ri[2][...]
        sinv = ri[3][...]
        dqv = jnp.where(pl.program_id(0) >= n_ctx_tiles, ri[4][...] * (HEAD_DIM ** -0.5), 0.0)
        dxq, dgq = norm_rope_bwd(ri[0][...], vi[0][...], vi[2][...], _q_wide(cosv), _q_wide(sinv), dqv)
        dxk, dgk = norm_rope_bwd(pkv[:, :KV_W], vi[1][...], vi[3][...], cosv, sinv, ri[5][...])
        ro[0][:, :ATTN_W] = dxq.astype(BF16)
        ro[0][:, ATTN_W:ATTN_W + KV_W] = dxk.astype(BF16)
        ro[0][:, ATTN_W + KV_W:] = ri[6][...].astype(BF16)
        vo[0][...] += dgq
        vo[1][...] += dgk

    return _rowwise(name, body, p_qkv.shape[0] // tm, tm,
                    [_ri(p_qkv, ATTN_W, 0), _ri(p_qkv, 2 * KV_W, 2), _ri(cos), _ri(sin),
                     _ri(dq, row_fn=lambda i: jnp.maximum(i - n_ctx_tiles, 0)), _ri(dk), _ri(dv)],
                    [gq, gk, bdq, bdk], [(QKV_W, BF16)], [(1, ATTN_W), (1, KV_W)])


def _sg_fwd(name, p_sg, w_sp, bsp, tm):
    def body(ri, vi, ro, vo):
        p = ri[0][...]
        gu, _ = _gelu(p[:, :SG_W])
        gv, _ = _gelu(p[:, SG_W:])
        for g in range(N_SG):
            sl = slice(g * CHUNK, (g + 1) * CHUNK)
            vn, _ = _ln_stats(gv[:, sl])
            vnb = vn.astype(BF16)
            for m in range(tm // CHUNK):
                rs = slice(m * CHUNK, (m + 1) * CHUNK)
                mixed = jnp.dot(vi[0][g], vnb[rs], preferred_element_type=F32) + vi[1][:, sl]
                ro[0][rs, sl] = (gu[rs, sl] * mixed).astype(BF16)

    return _rowwise(name, body, p_sg.shape[0] // tm, tm, [_ri(p_sg)], [w_sp, bsp], [(SG_W, BF16)], [])[0]


def _sg_bwd(name, p_sg, dcat, w_sp, w_sp_t, bsp, tm):
    def body(ri, vi, ro, vo):
        p = ri[0][...]
        dsg = ri[1][...]
        su = p[:, :SG_W]
        sv = p[:, SG_W:]
        gu, tu = _gelu(su)
        gv, tv = _gelu(sv)
        for g in range(N_SG):
            sl = slice(g * CHUNK, (g + 1) * CHUNK)
            vn, r = _ln_stats(gv[:, sl])
            vnb = vn.astype(BF16)
            for m in range(tm // CHUNK):
                rs = slice(m * CHUNK, (m + 1) * CHUNK)
                mixed = jnp.dot(vi[0][g], vnb[rs], preferred_element_type=F32) + vi[2][:, sl]
                d_o = dsg[rs, sl]
                dm = d_o * gu[rs, sl]
                vo[1][:, sl] += dm
                dmb = dm.astype(BF16)
                vo[0][g] += lax.dot_general(dmb, vnb[rs], (((1,), (1,)), ((), ())), preferred_element_type=F32)
                dvn = jnp.dot(vi[1][g], dmb, preferred_element_type=F32)
                dgv = _ln_bwd(vn[rs], r[rs], dvn)
                ro[0][rs, sl] = (d_o * mixed * _gelu_grad(su[rs, sl], tu[rs, sl])).astype(BF16)
                ro[0][rs, SG_W + g * CHUNK:SG_W + (g + 1) * CHUNK] = (
                    dgv * _gelu_grad(sv[rs, sl], tv[rs, sl])).astype(BF16)

    return _rowwise(name, body, p_sg.shape[0] // tm, tm, [_ri(p_sg), _ri(dcat, SG_W, 1)], [w_sp, w_sp_t, bsp],
                    [(2 * SG_W, BF16)], [(N_SG, CHUNK, CHUNK), (CHUNK, SG_W)])


def _glu_fwd(name, ag, b, tm):
    def body(ri, vi, ro, vo):
        v = ri[0][...] + vi[0][...]
        ro[0][...] = v[:, :D_MODEL] * _sigmoid(v[:, D_MODEL:])

    return _rowwise(name, body, ag.shape[0] // tm, tm, [_ri(ag)], [b], [(D_MODEL, F32)], [])[0]


def _glu_bwd(name, ag, dhg, b, tm):
    def body(ri, vi, ro, vo):
        v = ri[0][...] + vi[0][...]
        d = ri[1][...]
        a = v[:, :D_MODEL]
        sg = _sigmoid(v[:, D_MODEL:])
        da = d * sg
        dgate = d * a * sg * (1.0 - sg)
        ro[0][:, :D_MODEL] = da.astype(BF16)
        ro[0][:, D_MODEL:] = dgate.astype(BF16)
        vo[0][:, :D_MODEL] += _colsum(da)
        vo[0][:, D_MODEL:] += _colsum(dgate)

    return _rowwise(name, body, ag.shape[0] // tm, tm, [_ri(ag), _ri(dhg)], [b], [(2 * D_MODEL, BF16)],
                    [(1, 2 * D_MODEL)])


def _lnsilu_fwd(name, hc, g, b, tm):
    def body(ri, vi, ro, vo):
        xh, _ = _ln_stats(ri[0][...])
        ln = xh * vi[0][...] + vi[1][...]
        ro[0][...] = (ln * _sigmoid(ln)).astype(BF16)

    return _rowwise(name, body, hc.shape[0] // tm, tm, [_ri(hc)], [g, b], [(D_MODEL, BF16)], [])[0]


def _lnsilu_bwd(name, hc, ds, g, b, tm):
    def body(ri, vi, ro, vo):
        gv = vi[0][...]
        xh, r = _ln_stats(ri[0][...])
        ln = xh * gv + vi[1][...]
        sg = _sigmoid(ln)
        dln = ri[1][...] * (sg * (1.0 + ln * (1.0 - sg)))
        vo[0][...] += _colsum(dln * xh)
        vo[1][...] += _colsum(dln)
        ro[0][...] = _ln_bwd(xh, r, dln * gv)

    return _rowwise(name, body, hc.shape[0] // tm, tm, [_ri(hc), _ri(ds)], [g, b], [(D_MODEL, F32)],
                    [(1, D_MODEL)] * 2)


def _conv_blocks(seq):
    cb = 128
    tt = 128 if seq % 128 == 0 else seq
    return cb, tt


def _conv_taps(win, tt):
    n = win.shape[0]
    for s in range(8):
        ws = win if s == 0 else pltpu.roll(win, n - s, 0)
        for q in range(4):
            j = 8 * q + s - 1
            if 0 <= j < CONV_W:
                yield j, ws[8 * q:8 * q + tt, :]


def _fill_padded(pad_ref, x_ref, seq):
    zeros = jnp.zeros((CONV_HALO, pad_ref.shape[1]), F32)
    pad_ref[0:CONV_HALO, :] = zeros
    pad_ref[seq + CONV_HALO:seq + 2 * CONV_HALO, :] = zeros
    pad_ref[CONV_HALO:seq + CONV_HALO, :] = x_ref[...]


def _dwconv(name, xin, w, b):
    seq = xin.shape[0]
    cb, tt = _conv_blocks(seq)

    def kern(x_ref, w_ref, b_ref, o_ref, pad_ref):
        _fill_padded(pad_ref, x_ref, seq)
        wv = w_ref[...]
        bv = b_ref[...]

        def step(t, carry):
            base = pl.multiple_of(t * tt, tt)
            win = pad_ref[pl.ds(base, tt + 2 * CONV_HALO), :]
            acc = jnp.zeros((tt, cb), F32) + bv
            for j, rows in _conv_taps(win, tt):
                acc = acc + wv[j:j + 1, :] * rows
            o_ref[pl.ds(base, tt), :] = acc
            return carry

        lax.fori_loop(0, seq // tt, step, 0)

    return pl.pallas_call(
        kern, name=name, grid=(D_MODEL // cb,),
        in_specs=[pl.BlockSpec((seq, cb), lambda j: (0, j)), pl.BlockSpec((CONV_W, cb), lambda j: (0, j)),
                  pl.BlockSpec((1, cb), lambda j: (0, j))],
        out_specs=pl.BlockSpec((seq, cb), lambda j: (0, j)),
        out_shape=jax.ShapeDtypeStruct((seq, D_MODEL), F32),
        scratch_shapes=[pltpu.VMEM((seq + 2 * CONV_HALO, cb), F32)],
        compiler_params=_params())(xin, w, b)


def _dwconv_wgrad(name, xin, dout):
    seq = xin.shape[0]
    cb, tt = _conv_blocks(seq)

    def kern(x_ref, d_ref, dw_ref, db_ref, pad_ref):
        _fill_padded(pad_ref, x_ref, seq)
        dw_ref[...] = jnp.zeros(dw_ref.shape, F32)
        db_ref[...] = jnp.zeros(db_ref.shape, F32)

        def step(t, carry):
            base = pl.multiple_of(t * tt, tt)
            win = pad_ref[pl.ds(base, tt + 2 * CONV_HALO), :]
            d = d_ref[pl.ds(base, tt), :]
            db_ref[...] += _colsum(d)
            for j, rows in _conv_taps(win, tt):
                dw_ref[j:j + 1, :] += _colsum(d * rows)
            return carry

        lax.fori_loop(0, seq // tt, step, 0)

    return pl.pallas_call(
        kern, name=name, grid=(D_MODEL // cb,),
        in_specs=[pl.BlockSpec((seq, cb), lambda j: (0, j)), pl.BlockSpec((seq, cb), lambda j: (0, j))],
        out_specs=[pl.BlockSpec((CONV_W, cb), lambda j: (0, j)), pl.BlockSpec((1, cb), lambda j: (0, j))],
        out_shape=[jax.ShapeDtypeStruct((CONV_W, D_MODEL), F32), jax.ShapeDtypeStruct((1, D_MODEL), F32)],
        scratch_shapes=[pltpu.VMEM((seq + 2 * CONV_HALO, cb), F32)],
        compiler_params=_params())(xin, dout)


def _mm(name, a, b, out_sds, grid, a_spec, b_spec, o_spec, contract, k_axis=None, bias=None, pair_send=()):
    dn = (contract, ((), ()))
    ns = len(pair_send)
    nb = 0 if bias is None else 1
    n_steps = math.prod(grid)

    def kern(*refs):
        a_ref, b_ref = refs[:2]
        bias_ref = refs[2] if nb else None
        o_ref = refs[2 + nb + ns]
        if ns:
            copies = _pair_send_copies(refs[2 + nb:2 + nb + ns], refs[3 + nb + ns:3 + nb + 2 * ns],
                                       *refs[3 + nb + 2 * ns:])
            step = pl.program_id(0)
            for ax in range(1, len(grid)):
                step = step * grid[ax] + pl.program_id(ax)

            @pl.when(step == 0)
            def _():
                for cp in copies():
                    cp.start()

        p = lax.dot_general(a_ref[...].astype(BF16), b_ref[...].astype(BF16), dn, preferred_element_type=F32)
        if bias is not None:
            p = p + bias_ref[...]
        if k_axis is None:
            o_ref[...] = p.astype(o_ref.dtype)
        else:
            k = pl.program_id(k_axis)

            @pl.when(k == 0)
            def _():
                o_ref[...] = p

            @pl.when(k > 0)
            def _():
                o_ref[...] += p

        if ns:
            @pl.when(step == n_steps - 1)
            def _():
                for cp in copies():
                    cp.wait()

    in_specs = [a_spec, b_spec]
    args = [a, b]
    if bias is not None:
        in_specs.append(pl.BlockSpec(bias.shape, lambda *_: (0,) * bias.ndim))
        args.append(bias)
    if not ns:
        return pl.pallas_call(kern, name=name, grid=grid, in_specs=in_specs, out_specs=o_spec, out_shape=out_sds,
                              compiler_params=_params())(*args)
    return pl.pallas_call(
        kern, name=name, grid=grid, in_specs=in_specs + [_ANY] * ns, out_specs=[o_spec] + [_ANY] * ns,
        out_shape=[out_sds] + [jax.ShapeDtypeStruct((N_CHIP,) + g.shape[2:], g.dtype) for g in pair_send],
        scratch_shapes=[pltpu.SemaphoreType.DMA((ns, N_CHIP)), pltpu.SemaphoreType.DMA((ns, N_CHIP))],
        compiler_params=_params())(*args, *pair_send)


def _mm_rows(name, a, b, tm, trans_b=False, bias=None):
    m, k = a.shape
    n = b.shape[0] if trans_b else b.shape[1]
    contract = ((1,), (1,)) if trans_b else ((1,), (0,))
    return _mm(name, a, b, jax.ShapeDtypeStruct((m, n), F32), (m // tm,),
               pl.BlockSpec((tm, k), lambda i: (i, 0)), pl.BlockSpec(b.shape, lambda i: (0, 0)),
               pl.BlockSpec((tm, n), lambda i: (i, 0)), contract, bias=bias)


def _mm_rows_res(name, a, b, h, gt, tm, bias=None):
    m, k = a.shape
    n = b.shape[1]

    def kern(a_ref, b_ref, h_ref, gt_ref, *rest):
        y = jnp.dot(a_ref[...], b_ref[...], preferred_element_type=F32)
        if bias is not None:
            y = y + rest[0][...]
        rest[-2][...] = y
        rest[-1][...] = h_ref[...] + gt_ref[...] * y

    row = pl.BlockSpec((tm, n), lambda i: (i, 0))
    vec = pl.BlockSpec((1, n), lambda i: (0, 0))
    return pl.pallas_call(
        kern, name=name, grid=(m // tm,),
        in_specs=[pl.BlockSpec((tm, k), lambda i: (i, 0)), pl.BlockSpec(b.shape, lambda i: (0, 0)), row, vec]
        + ([vec] if bias is not None else []),
        out_specs=[row, row], out_shape=[jax.ShapeDtypeStruct((m, n), F32)] * 2,
        compiler_params=_params())(a, b, h, gt, *([bias] if bias is not None else []))


def _ffn_in_swiglu(name, xf, wblk, tm):
    m, k = xf.shape
    nblk = wblk.shape[2]

    def kern(x_ref, wg_ref, wu_ref, gu_ref, a_ref):
        xv = x_ref[...]
        g = jnp.dot(xv, wg_ref[...], preferred_element_type=F32)
        u = jnp.dot(xv, wu_ref[...], preferred_element_type=F32)
        gu_ref[0] = g.astype(BF16)
        gu_ref[1] = u.astype(BF16)
        a_ref[...] = (g * _sigmoid(g) * u).astype(BF16)

    return pl.pallas_call(
        kern, name=name, grid=(2, m // tm),
        in_specs=[pl.BlockSpec((tm, k), lambda j, i: (i, 0)), pl.BlockSpec((None, k, nblk), lambda j, i: (j, 0, 0)),
                  pl.BlockSpec((None, k, nblk), lambda j, i: (j + 2, 0, 0))],
        out_specs=[pl.BlockSpec((2, tm, nblk), lambda j, i: (0, i, j)), pl.BlockSpec((tm, nblk), lambda j, i: (i, j))],
        out_shape=[jax.ShapeDtypeStruct((2, m, 2 * nblk), BF16), jax.ShapeDtypeStruct((m, 2 * nblk), BF16)],
        compiler_params=_params())(xf, wblk, wblk)


def _ffn_out_dx_swiglu(name, dy, wo, gu, tm):
    m, k = dy.shape
    nblk = gu.shape[2] // 2

    def kern(dy_ref, w_ref, gu_ref, o_ref):
        da = lax.dot_general(dy_ref[...], w_ref[...], (((1,), (1,)), ((), ())), preferred_element_type=F32)
        g = gu_ref[0].astype(F32)
        sg = _sigmoid(g)
        o_ref[0] = (da * gu_ref[1].astype(F32) * (sg * (1.0 + g * (1.0 - sg)))).astype(BF16)
        o_ref[1] = (da * (g * sg)).astype(BF16)

    return pl.pallas_call(
        kern, name=name, grid=(2, m // tm),
        in_specs=[pl.BlockSpec((tm, k), lambda j, i: (i, 0)), pl.BlockSpec((nblk, k), lambda j, i: (j, 0)),
                  pl.BlockSpec((2, tm, nblk), lambda j, i: (0, i, j))],
        out_specs=pl.BlockSpec((2, tm, nblk), lambda j, i: (0, i, j)),
        out_shape=jax.ShapeDtypeStruct(gu.shape, BF16), compiler_params=_params())(dy, wo, gu)


def _ffn_in_dx_norm_bwd(name, dgu, wblk, h, dh_in, g, sc, tm):
    nb, k, nblk = wblk.shape

    def dx_fn(a_ref, w_ref):
        acc = None
        for j in range(nb):
            p = lax.dot_general(a_ref[j // 2, :, (j % 2) * nblk:(j % 2 + 1) * nblk], w_ref[j],
                                (((1,), (1,)), ((), ())), preferred_element_type=F32)
            acc = p if acc is None else acc + p
        return acc

    return _mm_norm_bwd(name, [dgu, wblk],
                        [pl.BlockSpec((2, tm, 2 * nblk), lambda i: (0, i, 0)),
                         pl.BlockSpec(wblk.shape, lambda i: (0, 0, 0))], dx_fn, h, dh_in, g, sc, tm)


def _mm_t_norm_bwd(name, a, b, h, dh_in, g, sc, tm):
    def dx_fn(a_ref, b_ref):
        return lax.dot_general(a_ref[...], b_ref[...], (((1,), (1,)), ((), ())), preferred_element_type=F32)

    return _mm_norm_bwd(name, [a, b], [pl.BlockSpec((tm, a.shape[1]), lambda i: (i, 0)),
                                      pl.BlockSpec(b.shape, lambda i: (0, 0))], dx_fn, h, dh_in, g, sc, tm)


def _mm_halves_wgrad_blk(name, a, b2, tk, pair_send=()):
    t, k = a.shape
    nblk = b2.shape[2] // 2
    return _mm(name, a, b2, jax.ShapeDtypeStruct((N_CHIP, k, nblk), F32), (N_CHIP, t // tk),
               pl.BlockSpec((tk, k), lambda j, i: (i, 0)),
               pl.BlockSpec((None, tk, nblk), lambda j, i: (j // 2, i, j % 2)),
               pl.BlockSpec((None, k, nblk), lambda j, i: (j, 0, 0)), ((0,), (0,)), k_axis=1, pair_send=pair_send)


def _mm_wgrad(name, a, b, tk):
    t, k = a.shape
    n = b.shape[1]
    tn = n if k * n * 4 <= 4 * 1024 * 1024 else 512
    return _mm(name, a, b, jax.ShapeDtypeStruct((k, n), F32), (n // tn, t // tk),
               pl.BlockSpec((tk, k), lambda j, i: (i, 0)), pl.BlockSpec((tk, tn), lambda j, i: (i, j)),
               pl.BlockSpec((k, tn), lambda j, i: (0, j)), ((0,), (0,)), k_axis=1)


def _attn_fwd(name, q_all, k, v, n_ctx, tq, gather=()):
    r = k.shape[1]
    s = q_all.shape[0] - n_ctx
    gw = Q_GROUP * HEAD_DIM
    ng = len(gather)
    n_i = s // tq
    last = N_KV_HEADS * n_i - 1
    forward_at = (7 * last) // 8

    def kern(q_ref, k_ref, v_ref, *rest):
        o_ref, lse_ref = rest[ng], rest[ng + 1]
        if ng:
            start, forward, finish = _gather_phases(rest[:ng], rest[ng + 2:2 * ng + 2], *rest[2 * ng + 2:])
            step = pl.program_id(0) * n_i + pl.program_id(1)
            pl.when(step == 0)(start)
            pl.when(step == forward_at)(forward)
        kv = k_ref[...]
        vv = v_ref[...]
        for g in range(Q_GROUP):
            sl = slice(g * HEAD_DIM, (g + 1) * HEAD_DIM)
            sc = lax.dot_general(q_ref[:, sl], kv, (((1,), (1,)), ((), ())), preferred_element_type=F32)
            m = jnp.max(sc, axis=-1, keepdims=True)
            p = jnp.exp(sc - m)
            ol = jnp.dot(p.astype(BF16), vv, preferred_element_type=F32)
            l = ol[:, HEAD_DIM:HEAD_DIM + 1]
            o_ref[:, sl] = ol[:, :HEAD_DIM] / l
            lse_ref[g] = m + jnp.log(l)
        if ng:
            pl.when(step == last)(finish)

    return pl.pallas_call(
        kern, name=name, grid=(N_KV_HEADS, n_i),
        in_specs=[pl.BlockSpec((tq, gw), lambda j, i: (i + n_ctx // tq, j)),
                  pl.BlockSpec((None, r, HEAD_DIM), lambda j, i: (j, 0, 0)),
                  pl.BlockSpec((None, r, 2 * HEAD_DIM), lambda j, i: (j, 0, 0))] + [_ANY] * ng,
        out_specs=[pl.BlockSpec((tq, gw), lambda j, i: (i, j)),
                   pl.BlockSpec((Q_GROUP, tq, 1), lambda j, i: (j, i, 0))] + [_ANY] * ng,
        out_shape=[jax.ShapeDtypeStruct((s, ATTN_W), F32), jax.ShapeDtypeStruct((N_Q_HEADS, s, 1), F32)]
        + [jax.ShapeDtypeStruct((N_DEV,) + a.shape, a.dtype) for a in gather],
        scratch_shapes=_gather_scratch(ng) if ng else [],
        compiler_params=_params())(q_all, k, v, *gather)


def _attn_bwd(name, q_all, k, v, o, lse, dcat, n_ctx, tq, exchange=()):
    nkv, r, _ = k.shape
    s = o.shape[0]
    gw = Q_GROUP * HEAD_DIM
    ne = len(exchange)
    n_i = s // tq
    last = nkv * n_i - 1

    def kern(q_ref, k_ref, v_ref, o_ref, lse_ref, do_ref, *rest):
        dq_ref, dk_ref, dv_ref = rest[ne:ne + 3]
        ds_all, p_all = rest[2 * ne + 3:2 * ne + 5]
        if ne:
            start, finish = _exchange_phases(rest[:ne], rest[ne + 3:2 * ne + 3], *rest[2 * ne + 5:])
            step = pl.program_id(0) * n_i + pl.program_id(1)
            pl.when(step == 0)(start)

        @pl.when(pl.program_id(1) == 0)
        def _():
            dk_ref[...] = jnp.zeros(dk_ref.shape, F32)
            dv_ref[...] = jnp.zeros(dv_ref.shape, F32)

        kv = k_ref[...]
        vv = v_ref[...]
        qs = []
        dos = []
        for g in range(Q_GROUP):
            sl = slice(g * HEAD_DIM, (g + 1) * HEAD_DIM)
            rows = slice(g * tq, (g + 1) * tq)
            qv = q_ref[:, sl]
            dov = do_ref[:, sl]
            delta = jnp.sum(dov * o_ref[:, sl], axis=-1, keepdims=True)
            sc = lax.dot_general(qv, kv, (((1,), (1,)), ((), ())), preferred_element_type=F32)
            p = jnp.exp(sc - lse_ref[g])
            dob = dov.astype(BF16)
            dp = lax.dot_general(dob, vv, (((1,), (1,)), ((), ())), preferred_element_type=F32)
            dsb = (p * (dp - delta)).astype(BF16)
            dq_ref[:, sl] = jnp.dot(dsb, kv, preferred_element_type=F32)
            ds_all[rows, :] = dsb
            p_all[rows, :] = p.astype(BF16)
            qs.append(qv)
            dos.append(dob)
        dk_ref[...] += lax.dot_general(jnp.concatenate(qs, axis=0), ds_all[...], (((0,), (0,)), ((), ())),
                                       preferred_element_type=F32)
        dv_ref[...] += lax.dot_general(jnp.concatenate(dos, axis=0), p_all[...], (((0,), (0,)), ((), ())),
                                       preferred_element_type=F32)
        if ne:
            pl.when(step == last)(finish)

    kvmap = lambda j, i: (j, 0, 0)
    return pl.pallas_call(
        kern, name=name, grid=(nkv, n_i),
        in_specs=[pl.BlockSpec((tq, gw), lambda j, i: (i + n_ctx // tq, j)), pl.BlockSpec((None, r, HEAD_DIM), kvmap),
                  pl.BlockSpec((None, r, HEAD_DIM), kvmap), pl.BlockSpec((tq, gw), lambda j, i: (i, j)),
                  pl.BlockSpec((Q_GROUP, tq, 1), lambda j, i: (j, i, 0)),
                  pl.BlockSpec((tq, gw), lambda j, i: (i, j))] + [_ANY] * ne,
        out_specs=[pl.BlockSpec((tq, gw), lambda j, i: (i, j)), pl.BlockSpec((None, HEAD_DIM, r), kvmap),
                   pl.BlockSpec((None, HEAD_DIM, r), kvmap)] + [_ANY] * ne,
        out_shape=[jax.ShapeDtypeStruct((s, ATTN_W), F32), jax.ShapeDtypeStruct((nkv, HEAD_DIM, r), F32),
                   jax.ShapeDtypeStruct((nkv, HEAD_DIM, r), F32)]
        + [jax.ShapeDtypeStruct(a.shape, a.dtype) for a in exchange],
        scratch_shapes=[pltpu.VMEM((Q_GROUP * tq, r), BF16)] * 2 + (_exchange_scratch(ne) if ne else []),
        compiler_params=_params())(q_all, k, v, o, lse, dcat, *exchange)


def _mod_fwd(name, c16, wm, bm):
    n = wm.shape[2]

    def kern(c_ref, w_ref, b_ref, o_ref):
        cv = c_ref[...]
        a = (cv * _sigmoid(cv)).astype(BF16)
        o_ref[...] = jnp.dot(a, w_ref[...].astype(BF16), preferred_element_type=F32) + b_ref[...]

    return pl.pallas_call(
        kern, name=name, grid=(2,),
        in_specs=[pl.BlockSpec(c16.shape, lambda l: (0, 0)), pl.BlockSpec((None, D_MODEL, n), lambda l: (l, 0, 0)),
                  pl.BlockSpec((None, 1, n), lambda l: (l, 0, 0))],
        out_specs=pl.BlockSpec((None, 16, n), lambda l: (l, 0, 0)),
        out_shape=jax.ShapeDtypeStruct((2, 16, n), F32), compiler_params=_params())(c16, wm, bm)


def _mod_wgrad(name, c16, dm):
    n = dm.shape[2]
    tn = 512

    def kern(c_ref, d_ref, o_ref):
        cv = c_ref[...]
        a = cv * _sigmoid(cv)
        o_ref[...] = lax.dot_general(a, d_ref[...], (((0,), (0,)), ((), ())), preferred_element_type=F32,
                                     precision=lax.Precision.HIGHEST)

    return pl.pallas_call(
        kern, name=name, grid=(2, n // tn),
        in_specs=[pl.BlockSpec(c16.shape, lambda l, j: (0, 0)), pl.BlockSpec((None, 16, tn), lambda l, j: (l, 0, j))],
        out_specs=pl.BlockSpec((None, D_MODEL, tn), lambda l, j: (l, 0, j)),
        out_shape=jax.ShapeDtypeStruct((2, D_MODEL, n), F32), compiler_params=_params())(c16, dm)


def _cctx_grad(name, parts, c_ctx):
    def kern(p_ref, c_ref, o_ref):
        d = p_ref[0, 0:1, :] + p_ref[2, 0:1, :] + p_ref[4, 0:1, :] + p_ref[6, 0:1, :]
        cv = c_ref[...]
        sg = _sigmoid(cv)
        o_ref[...] = d * (sg * (1.0 + cv * (1.0 - sg)))

    return pl.pallas_call(kern, name=name, out_shape=jax.ShapeDtypeStruct((1, D_MODEL), F32),
                          compiler_params=_params())(parts, c_ctx)


def _sum_slots(name, g, tr):
    n, rows, cols = g.shape

    def kern(g_ref, o_ref):
        acc = g_ref[0].astype(F32)
        for i in range(1, n):
            acc = acc + g_ref[i].astype(F32)
        o_ref[...] = acc

    return pl.pallas_call(kern, name=name, grid=(rows // tr,),
                          in_specs=[pl.BlockSpec((n, tr, cols), lambda i: (0, i, 0))],
                          out_specs=pl.BlockSpec((tr, cols), lambda i: (i, 0)),
                          out_shape=jax.ShapeDtypeStruct((rows, cols), F32), compiler_params=_params())(g)


def _sum_into(name, g, cvec, n_layers, layer, prev=None):
    n, r, cc = g.shape
    tr = _row_tile(r, cc, 512 * 1024)

    def kern(c_ref, g_ref, *rest):
        acc = g_ref[0].astype(F32)
        for i in range(1, n):
            acc = acc + g_ref[i].astype(F32)
        rest[-1][...] = acc

    in_specs = [pl.BlockSpec((n, tr, cc), lambda i, c_ref: (0, i, 0))]
    args = [cvec, g]
    aliases = {}
    if prev is not None:
        in_specs.append(_ANY)
        args.append(prev)
        aliases = {2: 0}
    grid_spec = pltpu.PrefetchScalarGridSpec(
        num_scalar_prefetch=1, grid=(r // tr,), in_specs=in_specs,
        out_specs=pl.BlockSpec((None, None, tr, cc), lambda i, c_ref: (layer, c_ref[0], i, 0)))
    return pl.pallas_call(kern, name=name, grid_spec=grid_spec,
                          out_shape=jax.ShapeDtypeStruct((n_layers, 2, r, cc), F32),
                          input_output_aliases=aliases, compiler_params=_params())(*args)


def _row_tile(rows, cols, max_bytes=1024 * 1024):
    if rows * cols * 4 <= 2 * max_bytes:
        return rows
    best = None
    for t in range(16, rows + 1, 16):
        if rows % t == 0 and t * cols * 4 <= max_bytes:
            best = t
    assert best is not None, (rows, cols)
    return best


def _adamw(name, w, g, m, v, gather=()):
    shape = w.shape
    cols = shape[-1]
    rows = w.size // cols
    tr = _row_tile(rows, cols)
    ng = len(gather)
    last = rows // tr - 1

    def kern(w_ref, g_ref, m_ref, v_ref, *rest):
        d_ref, nm_ref, nv_ref = rest[ng:ng + 3]
        if ng:
            start, forward, finish = _gather_phases(rest[:ng], rest[ng + 3:2 * ng + 3], *rest[2 * ng + 3:])
            pl.when(pl.program_id(0) == 0)(start)
            pl.when(pl.program_id(0) == last // 2)(forward)
        gv = g_ref[...]
        m2 = ADAM_B1 * m_ref[...] + (1.0 - ADAM_B1) * gv
        v2 = ADAM_B2 * v_ref[...] + (1.0 - ADAM_B2) * (gv * gv)
        m_hat = m2 / (1.0 - ADAM_B1 ** ADAM_STEP)
        v_hat = v2 / (1.0 - ADAM_B2 ** ADAM_STEP)
        d_ref[...] = -ADAM_LR * (m_hat / (jnp.sqrt(v_hat) + ADAM_EPS) + ADAM_WD * w_ref[...])
        nm_ref[...] = m2
        nv_ref[...] = v2
        if ng:
            pl.when(pl.program_id(0) == last)(finish)

    spec = pl.BlockSpec((tr, cols), lambda i: (i, 0))
    sds = jax.ShapeDtypeStruct((rows, cols), F32)
    outs = pl.pallas_call(
        kern, name=name, grid=(rows // tr,), in_specs=[spec] * 4 + [_ANY] * ng, out_specs=[spec] * 3 + [_ANY] * ng,
        out_shape=[sds] * 3 + [jax.ShapeDtypeStruct((N_DEV,) + a.shape, a.dtype) for a in gather],
        scratch_shapes=_gather_scratch(ng) if ng else [], compiler_params=_params())(
        w.reshape(rows, cols), g.reshape(rows, cols), m.reshape(rows, cols), v.reshape(rows, cols), *gather)
    return tuple(o.reshape(shape) for o in outs[:3]) + tuple(outs[3:])


_ANY = pl.BlockSpec(memory_space=pl.ANY)


def _mesh_pos():
    return lax.axis_index("x"), lax.axis_index("y"), lax.axis_index("c")


def _gather_phases(srcs, outs, send_sems, recv_sems, local_sems):
    n = len(srcs)
    x, y, c = _mesh_pos()
    me = (x, y, c)
    sibling = (x, y, 1 - c)
    chips = [(1 - x, y), (x, 1 - y), (1 - x, 1 - y)]

    def slot(px, py, pc):
        return 4 * px + 2 * py + pc

    def copy(t, k, s, to, src=None):
        dst = outs[t].at[s]
        return pltpu.make_async_remote_copy(src_ref=dst if src is None else src, dst_ref=dst,
                                            send_sem=send_sems.at[t, k], recv_sem=recv_sems.at[t, k],
                                            device_id=to, device_id_type=MESH)

    def mine(t):
        return pltpu.make_async_copy(srcs[t], outs[t].at[slot(*me)], local_sems.at[t])

    def first(t):
        return [copy(t, 0, slot(*me), sibling, src=srcs[t])] + [
            copy(t, 1 + j, slot(*me), (px, py, c), src=srcs[t]) for j, (px, py) in enumerate(chips)]

    def passed(t, j):
        px, py = chips[j]
        return copy(t, 4 + j, slot(px, py, c), sibling)

    def start():
        for t in range(n):
            mine(t).start()
        for t in range(n):
            for cp in first(t):
                cp.start()

    def forward():
        for j, (px, py) in enumerate(chips):
            for t in range(n):
                copy(t, 1 + j, slot(px, py, c), me).wait_recv()
                passed(t, j).start()

    def finish():
        for t in range(n):
            copy(t, 0, slot(x, y, 1 - c), me).wait_recv()
            for j, (px, py) in enumerate(chips):
                copy(t, 4 + j, slot(px, py, 1 - c), me).wait_recv()
        for t in range(n):
            for cp in first(t) + [passed(t, j) for j in range(len(chips))]:
                cp.wait_send()
            mine(t).wait()

    return start, forward, finish


def _gather_scratch(n):
    return [pltpu.SemaphoreType.DMA((n, 7)), pltpu.SemaphoreType.DMA((n, 7)), pltpu.SemaphoreType.DMA((n,))]


def _allgather(name, items):
    n = len(items)

    def body(*refs):
        start, forward, finish = _gather_phases(refs[:n], refs[n:2 * n], *refs[2 * n:])
        start()
        forward()
        finish()

    return pl.pallas_call(
        body, name=name, in_specs=[_ANY] * n, out_specs=[_ANY] * n,
        out_shape=[jax.ShapeDtypeStruct((N_DEV,) + a.shape, a.dtype) for a in items],
        scratch_shapes=_gather_scratch(n), compiler_params=_params())(*items)


def _pair_send_copies(srcs, bufs, send_sems, recv_sems):
    x, y, c = _mesh_pos()

    def copies():
        return [pltpu.make_async_remote_copy(
            src_ref=srcs[t].at[k, 1 - c], dst_ref=bufs[t].at[k], send_sem=send_sems.at[t, k],
            recv_sem=recv_sems.at[t, k], device_id=(x, y, 1 - c), device_id_type=MESH)
            for t in range(len(srcs)) for k in range(N_CHIP)]

    return copies


def _rs_pair_send(name, gs):
    n = len(gs)

    def body(*refs):
        copies = _pair_send_copies(refs[:n], refs[n:2 * n], *refs[2 * n:])
        for cp in copies():
            cp.start()
        for cp in copies():
            cp.wait()

    return pl.pallas_call(
        body, name=name, in_specs=[_ANY] * n, out_specs=[_ANY] * n,
        out_shape=[jax.ShapeDtypeStruct((N_CHIP,) + g.shape[2:], g.dtype) for g in gs],
        scratch_shapes=[pltpu.SemaphoreType.DMA((n, N_CHIP)), pltpu.SemaphoreType.DMA((n, N_CHIP))],
        compiler_params=_params())(*gs)


def _rs_pair_add(name, g, buf, cvec):
    _, _, r, cc = g.shape
    tr = _row_tile(r, cc, 2 * 1024 * 1024)

    def kern(c_ref, g_ref, b_ref, o_ref):
        o_ref[...] = (g_ref[...] + b_ref[...]).astype(BF16)

    grid_spec = pltpu.PrefetchScalarGridSpec(
        num_scalar_prefetch=1, grid=(N_CHIP, r // tr),
        in_specs=[pl.BlockSpec((None, None, tr, cc), lambda k, i, c_ref: (k, c_ref[0], i, 0)),
                  pl.BlockSpec((None, tr, cc), lambda k, i, c_ref: (k, i, 0))],
        out_specs=pl.BlockSpec((None, tr, cc), lambda k, i, c_ref: (k, i, 0)))
    return pl.pallas_call(kern, name=name, grid_spec=grid_spec,
                          out_shape=jax.ShapeDtypeStruct((N_CHIP, r, cc), BF16),
                          compiler_params=_params())(cvec, g, buf)


def _exchange_phases(srcs, bufs, send_sems, recv_sems, local_sems):
    n = len(srcs)
    x, y, c = _mesh_pos()
    kme = 2 * x + y
    chips = [(1 - x, y), (x, 1 - y), (1 - x, 1 - y)]

    def copies():
        local = [pltpu.make_async_copy(srcs[t].at[kme], bufs[t].at[kme], local_sems.at[t]) for t in range(n)]
        remote = []
        for t in range(n):
            for j, (px, py) in enumerate(chips):
                remote.append(pltpu.make_async_remote_copy(
                    src_ref=srcs[t].at[2 * px + py], dst_ref=bufs[t].at[kme], send_sem=send_sems.at[t, j],
                    recv_sem=recv_sems.at[t, j], device_id=(px, py, c), device_id_type=MESH))
        return local, remote

    def start():
        local, remote = copies()
        for cp in local + remote:
            cp.start()

    def finish():
        local, remote = copies()
        for cp in remote + local:
            cp.wait()

    return start, finish


def _exchange_scratch(n):
    return [pltpu.SemaphoreType.DMA((n, 3)), pltpu.SemaphoreType.DMA((n, 3)), pltpu.SemaphoreType.DMA((n,))]


def _rs_chip_exchange(name, ss):
    n = len(ss)

    def body(*refs):
        start, finish = _exchange_phases(refs[:n], refs[n:2 * n], *refs[2 * n:])
        start()
        finish()

    return pl.pallas_call(
        body, name=name, in_specs=[_ANY] * n, out_specs=[_ANY] * n,
        out_shape=[jax.ShapeDtypeStruct(s.shape, s.dtype) for s in ss],
        scratch_shapes=_exchange_scratch(n), compiler_params=_params())(*ss)


def _rs_pair_share(name, dsts):
    nd = len(dsts)

    def body(*refs):
        ins = refs[:nd]
        outs = refs[nd:2 * nd]
        send_sems, recv_sems = refs[2 * nd:]
        x, y, c = _mesh_pos()
        copies = []
        for d in range(nd):
            for l in range(dsts[d].shape[0]):
                copies.append(pltpu.make_async_remote_copy(
                    src_ref=ins[d].at[l, c], dst_ref=outs[d].at[l, c], send_sem=send_sems.at[d, l],
                    recv_sem=recv_sems.at[d, l], device_id=(x, y, 1 - c), device_id_type=MESH))
        for cp in copies:
            cp.start()
        for cp in copies:
            cp.wait()

    return pl.pallas_call(
        body, name=name, in_specs=[_ANY] * nd, out_specs=[_ANY] * nd,
        out_shape=[jax.ShapeDtypeStruct(a.shape, a.dtype) for a in dsts],
        input_output_aliases={d: d for d in range(nd)},
        scratch_shapes=[pltpu.SemaphoreType.DMA((nd, 2)), pltpu.SemaphoreType.DMA((nd, 2))],
        compiler_params=_params())(*dsts)


def _rope_tables(seq, n_ctx):
    t = jnp.arange(seq)
    row = (t // GRID_W).astype(F32)
    col = (t % GRID_W).astype(F32)
    inv = ROPE_THETA ** (-jnp.arange(0, HEAD_DIM // 2, 2, dtype=F32) / (HEAD_DIM // 2))
    ang_r = row[:, None] * inv[None, :]
    ang_c = col[:, None] * inv[None, :]
    cos = jnp.concatenate([jnp.cos(ang_r)] * 2 + [jnp.cos(ang_c)] * 2, axis=1)
    sin = jnp.concatenate([-jnp.sin(ang_r), jnp.sin(ang_r), -jnp.sin(ang_c), jnp.sin(ang_c)], axis=1)
    cos = jnp.concatenate([jnp.ones((n_ctx, HEAD_DIM), F32), cos], axis=0)
    sin = jnp.concatenate([jnp.zeros((n_ctx, HEAD_DIM), F32), sin], axis=0)
    return jnp.tile(cos, (1, N_KV_HEADS)), jnp.tile(sin, (1, N_KV_HEADS))


def _to_heads(a, nh):
    return a.reshape(a.shape[0], nh, HEAD_DIM).transpose(1, 0, 2)


def _sample_step(x, ctx, tgt, mod, cmod, W, pending, cvec):
    W = dict(W)
    seq = x.shape[0]
    n_ctx = ctx.shape[0]
    tm = min(256, seq)
    tmc = min(256, n_ctx)
    tmm = 512 if seq % 512 == 0 else tm
    tmw = 1024 if seq % 1024 == 0 else tmm
    tmr = (n_ctx + seq) // 4 if (n_ctx + seq) % 64 == 0 else tm
    tq = min(256, seq)
    assert n_ctx % tm == 0 and seq % tm == 0

    def mv(l, j):
        return mod[l, j * D_MODEL:(j + 1) * D_MODEL].reshape(1, D_MODEL)

    csh1 = cmod[:D_MODEL].reshape(1, D_MODEL)
    csc1 = cmod[D_MODEL:2 * D_MODEL].reshape(1, D_MODEL)
    g_mix = [W['g_mix'][l:l + 1] for l in range(2)]
    g_ffn = [W['g_ffn'][l:l + 1] for l in range(2)]
    cos, sin = _rope_tables(seq, n_ctx)
    gq = jnp.tile(W['q_gain'], (1, N_Q_HEADS))
    gk = jnp.tile(W['k_gain'], (1, N_KV_HEADS))
    bdq = jnp.kron(jnp.eye(N_Q_HEADS, dtype=F32), jnp.ones((HEAD_DIM, HEAD_DIM), F32)).astype(BF16)
    bdk = bdq[:KV_W, :KV_W]
    w_sp = W['w_sp'].astype(BF16)
    w_sp_t = w_sp.transpose(0, 2, 1)
    bsp = jnp.broadcast_to(W['b_sp'].T[:, :, None], (CHUNK, N_SG, CHUNK)).reshape(CHUNK, SG_W)

    def ffn_fwd(l, h_mid):
        xf = _modnorm_fwd(f'l{l}_ffn_norm', h_mid, g_ffn[l], mv(l, 4), mv(l, 3), tm)
        gu, a = _ffn_in_swiglu(f'l{l}_ffn_in', xf, W['ffn_in'][l], tmw)
        f, h_out = _mm_rows_res(f'l{l}_ffn_out', a, W['ffn_out'][l], h_mid, mv(l, 5), tmm)
        return h_out, (h_mid, xf, gu, a, f)

    def blocked(a):
        cols = a.shape[-1]
        return a.reshape(N_CHIP, 2, a.size // (2 * N_CHIP * cols), cols)

    def ffn_bwd(l, dh_out, saved, pair_send=()):
        h_mid, xf, gu, a, f = saved
        dy, d_gt2, _ = _resgate_bwd(f'l{l}_ffn_res_bwd', dh_out, f, mv(l, 5), tm)
        dgu = _ffn_out_dx_swiglu(f'l{l}_ffn_out_dx', dy, W['ffn_out'][l], gu, tmw)
        d_wo = _mm_wgrad(f'l{l}_ffn_out_dw', a, dy, tmw)
        d_wi = _mm_halves_wgrad_blk(f'l{l}_ffn_in_dw', xf, dgu, tmw, pair_send=pair_send)
        if pair_send:
            d_wi, *sent = d_wi
        else:
            sent = []
        dh_mid, d_sh2, d_sc2, d_g = _ffn_in_dx_norm_bwd(f'l{l}_ffn_in_dx', dgu, W['ffn_in'][l], h_mid, dh_out,
                                                        g_ffn[l], mv(l, 4), tm)
        return dh_mid, d_wi, d_wo, d_g, (d_sh2, d_sc2, d_gt2), sent

    xm0 = _modnorm_fwd('l0_mix_norm', x, g_mix[0], mv(0, 1), mv(0, 0), tm)
    xc0 = _modnorm_fwd('l0_ctx_norm', ctx, g_mix[0], csc1, csh1, tmc)
    xall = jnp.concatenate([xc0, xm0], axis=0)
    p_qkv = _mm_rows('l0_qkv', xall, W['w_qkv'], tmr)
    p_sg = _mm_rows('l0_sg_in', xm0, W['w_sgp'], tmw)
    q_all, k_all, v_all = _qk_prep_fwd('l0_qk_prep', p_qkv, cos, sin, gq, gk, bdq, bdk, tm)
    kh = _to_heads(k_all, N_KV_HEADS)
    vh = _to_heads(v_all, N_KV_HEADS)
    v_ones = jnp.concatenate([vh, jnp.ones(vh.shape[:2] + (1,), BF16),
                              jnp.zeros(vh.shape[:2] + (HEAD_DIM - 1,), BF16)], axis=2)
    o, lse, *gw = _attn_fwd('l0_attn', q_all, kh, v_ones, n_ctx, tq, gather=pending)
    W['w_out'] = gw[0].reshape(D_MODEL, D_MODEL)
    W['ffn_in'] = [gw[1].reshape(N_CHIP, D_MODEL, FF_BLK), gw[5].reshape(N_CHIP, D_MODEL, FF_BLK)]
    W['ffn_out'] = [gw[2].reshape(D_FF, D_MODEL), gw[6].reshape(D_FF, D_MODEL)]
    W['pw1'] = gw[3].reshape(N_CHIP, D_MODEL, PW1_BLK).transpose(1, 0, 2).reshape(D_MODEL, 2 * D_MODEL)
    W['pw2'] = gw[4].reshape(D_MODEL, D_MODEL)
    sg = _sg_fwd('l0_sg', p_sg, w_sp, bsp, tm)
    cat = jnp.concatenate([o.astype(BF16), sg], axis=1)
    y0, h1 = _mm_rows_res('l0_out', cat, W['w_out'], x, mv(0, 2), tmw)
    h2, ffn0 = ffn_fwd(0, h1)

    xm1 = _modnorm_fwd('l1_mix_norm', h2, g_mix[1], mv(1, 1), mv(1, 0), tm)
    ag = _mm_rows('l1_pw1', xm1, W['pw1'], tmm)
    hg = _glu_fwd('l1_glu', ag, W['b_pw1'], tm)
    hc = _dwconv('l1_conv', hg, W['w_dw'], W['b_dw'])
    s1 = _lnsilu_fwd('l1_ln_silu', hc, W['ln_g'], W['ln_b'], tm)
    y1, h3 = _mm_rows_res('l1_pw2', s1, W['pw2'], h2, mv(1, 2), tmw, bias=W['b_pw2'])
    h4, ffn1 = ffn_fwd(1, h3)

    dh4, d_g_final, loss_vec = _final_loss('final_loss', h4, tgt, W['g_final'], tm)

    dh3, d_wi1, d_wo1, d_gffn1, dmod1_ffn, _ = ffn_bwd(1, dh4, ffn1)
    dy1, d_gt1_1, d_b_pw2 = _resgate_bwd('l1_mix_res_bwd', dh3, y1, mv(1, 2), tm)
    ds1 = _mm_rows('l1_pw2_dx', dy1, W['pw2'], tmw, trans_b=True)
    d_pw2 = _mm_wgrad('l1_pw2_dw', s1, dy1, tmw)
    dhc, d_ln_g, d_ln_b = _lnsilu_bwd('l1_ln_silu_bwd', hc, ds1, W['ln_g'], W['ln_b'], tm)
    dhg = _dwconv('l1_conv_dx', dhc, W['w_dw'][::-1], jnp.zeros((1, D_MODEL), F32))
    d_w_dw, d_b_dw = _dwconv_wgrad('l1_conv_dw', hg, dhc)
    dag, d_b_pw1 = _glu_bwd('l1_glu_bwd', ag, dhg, W['b_pw1'], tm)
    d_pw1 = _mm_wgrad('l1_pw1_dw', xm1, dag, tmw).reshape(D_MODEL, N_CHIP, PW1_BLK).transpose(1, 0, 2)
    dh2, d_sh1_1, d_sc1_1, d_gmix1 = _mm_t_norm_bwd('l1_pw1_dx', dag, W['pw1'], h2, dh3, g_mix[1], mv(1, 1), tmm)

    gs1 = [blocked(g) for g in (d_wi1, d_wo1, d_pw1, d_pw2)]
    dh1, d_wi0, d_wo0, d_gffn0, dmod0_ffn, sib1 = ffn_bwd(0, dh2, ffn0, pair_send=gs1)
    dy0, d_gt1_0, _ = _resgate_bwd('l0_mix_res_bwd', dh1, y0, mv(0, 2), tm)
    dcat = _mm_rows('l0_out_dx', dy0, W['w_out'], tmw, trans_b=True)
    d_w_out = _mm_wgrad('l0_out_dw', cat, dy0, tmw)

    gs0 = [blocked(g) for g in (d_wi0, d_wo0, d_w_out)]
    sib0 = _rs_pair_send('rs_pair_send', gs0)
    gs = [gs0[0], gs1[0], gs0[1], gs1[1], gs0[2], gs1[2], gs1[3]]
    sib = [sib0[0], sib1[0], sib0[1], sib1[1], sib0[2], sib1[2], sib1[3]]
    ss = [_rs_pair_add(f'rs_pair_add{t}', gs[t], sib[t], cvec) for t in range(len(gs))]
    dq, dk, dv, *xs = _attn_bwd('l0_attn_bwd', q_all, kh, vh, o, lse, dcat, n_ctx, tq, exchange=ss)
    dk, dv = [t.transpose(2, 0, 1).reshape(n_ctx + seq, KV_W) for t in (dk, dv)]
    dp_qkv, d_gq, d_gk = _qk_prep_bwd('l0_qk_prep_bwd', p_qkv, cos, sin, dq, n_ctx // tm, dk, dv, gq, gk, bdq, bdk,
                                      tm)
    dp_sg, d_w_sp, d_bsp = _sg_bwd('l0_sg_bwd', p_sg, dcat, w_sp, w_sp_t, bsp, tm)
    d_w_qkv = _mm_wgrad('l0_qkv_dw', xall, dp_qkv, tmr)
    d_w_sgp = _mm_wgrad('l0_sg_in_dw', xm0, dp_sg, tmw)
    dxall = _mm_rows('l0_qkv_dx', dp_qkv, W['w_qkv'], tmr, trans_b=True)
    dxm_sg = _mm_rows('l0_sg_in_dx', dp_sg, W['w_sgp'], tmw, trans_b=True)
    dx, d_sh1_0, d_sc1_0, d_gmix0 = _modnorm_bwd('l0_mix_norm_bwd', x, [_ri(dxall, None, 0, n_ctx // tm), _ri(dxm_sg)],
                                                 dh1, g_mix[0], mv(0, 1), tm)
    _, d_csh1, d_csc1, d_gmix0c = _modnorm_bwd('l0_ctx_norm_bwd', ctx, [_ri(dxall)], None, g_mix[0], csc1, tmc)

    g_in = blocked(jnp.concatenate([d_w_qkv, d_w_sgp], axis=1).reshape(D_MODEL, N_CHIP, -1).transpose(1, 0, 2))
    sib_in = _rs_pair_send('rs_pair_send_w_in', [g_in])
    xs_in = _rs_chip_exchange('rs_chip_exchange_w_in', [_rs_pair_add('rs_pair_add_w_in', g_in, sib_in[0], cvec)])
    t_ffn_in = _sum_into('rs_sum_ffn_in1', xs[1], cvec, 2, 1, prev=_sum_into('rs_sum_ffn_in0', xs[0], cvec, 2, 0))
    t_ffn_out = _sum_into('rs_sum_ffn_out1', xs[3], cvec, 2, 1, prev=_sum_into('rs_sum_ffn_out0', xs[2], cvec, 2, 0))
    reduced = _rs_pair_share('rs_pair_share', [
        t_ffn_in, t_ffn_out, _sum_into('rs_sum_w_in', xs_in[0], cvec, 1, 0), _sum_into('rs_sum_w_out', xs[4], cvec, 1, 0),
        _sum_into('rs_sum_pw1', xs[5], cvec, 1, 0), _sum_into('rs_sum_pw2', xs[6], cvec, 1, 0)])
    big = dict(zip(['w_ffn_in', 'w_ffn_out', 'w_in', 'w_out', 'w_pw1', 'w_pw2'], reduced))

    zero = jnp.zeros((1, D_MODEL), F32)
    dmod = jnp.concatenate([d_sh1_0, d_sc1_0, d_gt1_0, *dmod0_ffn, d_sh1_1, d_sc1_1, d_gt1_1, *dmod1_ffn], axis=0)
    dcmod = jnp.concatenate([d_csh1, d_csc1, zero, zero, zero, zero], axis=0)
    grads = dict(
        g_mix=jnp.concatenate([d_gmix0 + d_gmix0c, d_gmix1], axis=0),
        g_ffn=jnp.concatenate([d_gffn0, d_gffn1], axis=0), g_final=d_g_final, gq=d_gq, gk=d_gk,
        w_sp=d_w_sp, bsp=d_bsp, b_pw1=d_b_pw1, w_dw=d_w_dw, b_dw=d_b_dw, ln_g=d_ln_g, ln_b=d_ln_b, b_pw2=d_b_pw2,
        dmod=dmod, dcmod=dcmod)
    return loss_vec, dx, grads, big


def kernel(x, c, ctx, c_ctx, w_mod, b_mod, g_mix, g_ffn, w_ffn_in, w_ffn_out, w_in, q_gain, k_gain, w_sp, b_sp, w_out, w_pw1, b_pw1, w_dw, b_dw, ln_g, ln_b, w_pw2, b_pw2, g_final, loss_target, m_c_ctx, m_w_mod, m_b_mod, m_g_mix, m_g_ffn, m_w_ffn_in, m_w_ffn_out, m_w_in, m_q_gain, m_k_gain, m_w_sp, m_b_sp, m_w_out, m_w_pw1, m_b_pw1, m_w_dw, m_b_dw, m_ln_g, m_ln_b, m_w_pw2, m_b_pw2, m_g_final, v_c_ctx, v_w_mod, v_b_mod, v_g_mix, v_g_ffn, v_w_ffn_in, v_w_ffn_out, v_w_in, v_q_gain, v_k_gain, v_w_sp, v_b_sp, v_w_out, v_w_pw1, v_b_pw1, v_w_dw, v_b_dw, v_ln_g, v_ln_b, v_w_pw2, v_b_pw2, v_g_final):
    given = dict(locals())
    ix, iy, ic = _mesh_pos()
    chip = 2 * ix + iy
    me = 2 * chip + ic
    d = D_MODEL
    q4 = d // N_CHIP

    small = jnp.concatenate([c.reshape(4, q4), b_pw1.reshape(2, q4), w_dw[0], b_dw, ln_g, ln_b, b_pw2,
                             jnp.zeros((7, q4), F32)], axis=0)
    def my_half(a):
        r = a.shape[0] // 2
        return lax.dynamic_slice_in_dim(a, ic * r, r, axis=0).astype(BF16)

    sgath, g_w_in = _allgather('ag_first', [small, my_half(w_in[0])])
    c_all = sgath[:, 0:4].reshape(N_DEV, d)
    per_chip = sgath[0::2]
    b_pw1_f = per_chip[:, 4:6].reshape(1, 2 * d)
    w_dw_f = per_chip[:, 6:6 + CONV_W].transpose(1, 0, 2).reshape(CONV_W, d)
    b_dw_f, ln_g_f, ln_b_f, b_pw2_f = [per_chip[:, 37 + i].reshape(1, d) for i in range(4)]

    c16 = jnp.concatenate([c_all, c_ctx[None, :], jnp.zeros((7, d), F32)], axis=0)
    bm_sh = lax.dynamic_slice_in_dim(b_mod, chip * MOD_BLK, MOD_BLK, axis=1).reshape(2, 1, MOD_BLK)
    mod_piece = _mod_fwd('mod_fwd', c16, w_mod, bm_sh)
    mgath = _allgather('ag_mod', [mod_piece.reshape(32, MOD_BLK)])[0]
    mod_all = mgath[0::2].reshape(N_CHIP, 2, 16, MOD_BLK).transpose(1, 2, 0, 3).reshape(2, 16, MOD_W)
    mod_me = lax.dynamic_index_in_dim(mod_all, me, axis=1, keepdims=False)
    cmod = mod_all[0, N_DEV]

    w_in_f = g_w_in.reshape(N_CHIP, d, -1).transpose(1, 0, 2).reshape(d, -1)
    W = dict(
        w_qkv=w_in_f[:, :QKV_W], w_sgp=w_in_f[:, QKV_W:],
        g_mix=g_mix, g_ffn=g_ffn, g_final=g_final.reshape(1, d), q_gain=q_gain, k_gain=k_gain, w_sp=w_sp[0],
        b_sp=b_sp[0], b_pw1=b_pw1_f, w_dw=w_dw_f, b_dw=b_dw_f, ln_g=ln_g_f, ln_b=ln_b_f, b_pw2=b_pw2_f)
    pending = [my_half(w_out[0]), my_half(w_ffn_in[0]), my_half(w_ffn_out[0]), my_half(w_pw1[0]), my_half(w_pw2[0]),
               my_half(w_ffn_in[1]), my_half(w_ffn_out[1])]
    cvec = jnp.reshape(ic, (1,)).astype(jnp.int32)

    loss_vec, dx, G, big = _sample_step(x[0], ctx[0], loss_target[0], mod_me, cmod, W, pending, cvec)
    loss = lax.psum(jnp.sum(loss_vec), ("x", "y", "c"))

    pack = jnp.concatenate([
        G['g_mix'], G['g_ffn'], G['g_final'],
        jnp.concatenate([G['gq'], G['gk'], jnp.zeros((1, d - ATTN_W - KV_W), F32)], axis=1),
        G['b_pw1'].reshape(2, d), G['w_dw'], G['b_dw'], G['ln_g'], G['ln_b'], G['b_pw2'],
        G['dmod'], G['dcmod'], jnp.zeros((3, d), F32),
        G['w_sp'].reshape(64, d), G['bsp'].reshape(64, d)], axis=0)
    assert pack.shape == (192, d)
    grads = dict(big)
    deltas, new_m, new_v = {}, {}, {}

    def adamw(n, gather=()):
        grads[n] = grads[n].reshape(given[n].shape)
        deltas[n], new_m[n], new_v[n], *rest = _adamw(f'adamw_{n}', given[n], grads[n], given['m_' + n],
                                                     given['v_' + n], gather)
        return rest

    pg = adamw('w_ffn_in', gather=[pack])[0]
    s8 = _sum_slots('sum_small_grads', pg, 64)

    def chip_cols(a, width):
        return lax.dynamic_slice_in_dim(a, chip * width, width, axis=a.ndim - 1)

    grads['g_mix'] = s8[0:2]
    grads['g_ffn'] = s8[2:4]
    grads['g_final'] = s8[4]
    grads['q_gain'] = s8[5, :ATTN_W].reshape(N_Q_HEADS, HEAD_DIM).sum(axis=0)[None, :]
    grads['k_gain'] = s8[5, ATTN_W:ATTN_W + KV_W].reshape(N_KV_HEADS, HEAD_DIM).sum(axis=0)[None, :]
    grads['b_pw1'] = chip_cols(s8[6:8].reshape(1, 2 * d), 2 * q4)
    grads['w_dw'] = chip_cols(s8[8:8 + CONV_W], q4)[None]
    grads['b_dw'] = chip_cols(s8[39:40], q4)
    grads['ln_g'] = chip_cols(s8[40:41], q4)
    grads['ln_b'] = chip_cols(s8[41:42], q4)
    grads['b_pw2'] = chip_cols(s8[42:43], q4)
    grads['w_sp'] = s8[64:128].reshape(1, N_SG, CHUNK, CHUNK)
    grads['b_sp'] = s8[128:192].reshape(CHUNK, N_SG, CHUNK).sum(axis=-1).T[None]

    dm_rows = pg[:, 43:55].reshape(N_DEV, 2, MOD_W)
    ctx_row = s8[55:61].reshape(1, MOD_W)
    dm = jnp.stack([jnp.concatenate([dm_rows[:, 0], ctx_row, jnp.zeros((7, MOD_W), F32)], axis=0),
                    jnp.concatenate([dm_rows[:, 1], jnp.zeros((8, MOD_W), F32)], axis=0)], axis=0)
    grads['b_mod'] = jnp.stack([s8[43:49].reshape(MOD_W) + ctx_row[0], s8[49:55].reshape(MOD_W)], axis=0)
    dm_sh = chip_cols(dm, MOD_BLK)
    grads['w_mod'] = _mod_wgrad('mod_dw', c16, dm_sh)
    d_silu = _mm_rows('mod_dx', dm_sh[0], w_mod[0], 16, trans_b=True)
    parts = _allgather('ag_cctx', [d_silu[8:16]])[0]
    grads['c_ctx'] = _cctx_grad('cctx_grad', parts, c_ctx.reshape(1, d))[0]

    for n in WEIGHT_NAMES:
        if n != 'w_ffn_in':
            adamw(n)

    return (loss, dx[None], *[grads[n] for n in WEIGHT_NAMES], *[deltas[n] for n in WEIGHT_NAMES],
            *[new_m[n] for n in WEIGHT_NAMES], *[new_v[n] for n in WEIGHT_NAMES])
```

```python
import functools
import math

import jax
import jax.numpy as jnp
from jax import lax
from jax.experimental import pallas as pl
from jax.experimental.pallas import tpu as pltpu

F32 = jnp.float32
BF16 = jnp.bfloat16
MESH = pl.DeviceIdType.MESH

D_MODEL = 1024
HEAD_DIM = 64
N_Q_HEADS = 8
N_KV_HEADS = 2
Q_GROUP = N_Q_HEADS // N_KV_HEADS
ATTN_W = N_Q_HEADS * HEAD_DIM
KV_W = N_KV_HEADS * HEAD_DIM
QKV_W = ATTN_W + 2 * KV_W
SG_W = D_MODEL - ATTN_W
N_SG = 4
CHUNK = 128
GRID_W = 64
ROPE_THETA = 10000.0
CONV_W = 31
CONV_HALO = 16
D_FF = 2816
MOD_W = 6 * D_MODEL
N_CHIP = 4
N_DEV = 8
FF_BLK = 2 * D_FF // N_CHIP
PW1_BLK = 2 * D_MODEL // N_CHIP
MOD_BLK = MOD_W // N_CHIP
EPS = 1e-6
GELU_C = math.sqrt(2.0 / math.pi)
GELU_A = 0.044715

ADAM_LR = 0.001
ADAM_B1 = 0.9
ADAM_B2 = 0.999
ADAM_EPS = 1e-08
ADAM_WD = 0.01
ADAM_STEP = 10

VMEM_LIMIT_BYTES = 56 * 1024 * 1024

WEIGHT_NAMES = ['c_ctx', 'w_mod', 'b_mod', 'g_mix', 'g_ffn', 'w_ffn_in', 'w_ffn_out', 'w_in', 'q_gain', 'k_gain',
                'w_sp', 'b_sp', 'w_out', 'w_pw1', 'b_pw1', 'w_dw', 'b_dw', 'ln_g', 'ln_b', 'w_pw2', 'b_pw2',
                'g_final']


def _params():
    return pltpu.CompilerParams(vmem_limit_bytes=VMEM_LIMIT_BYTES)


def _ri(arr, width=None, col_block=0, row_off=0, row_fn=None):
    return (arr, arr.shape[1] if width is None else width, col_block,
            (lambda i: i + row_off) if row_fn is None else row_fn)


def _rowwise(name, body, n_tiles, tm, row_ins, vec_ins, row_outs, vec_outs):
    nri, nvi, nro = len(row_ins), len(vec_ins), len(row_outs)

    def kern(*refs):
        ri = refs[:nri]
        vi = refs[nri:nri + nvi]
        ro = refs[nri + nvi:nri + nvi + nro]
        vo = refs[nri + nvi + nro:]
        if vo:
            @pl.when(pl.program_id(0) == 0)
            def _():
                for r in vo:
                    r[...] = jnp.zeros(r.shape, r.dtype)
        body(ri, vi, ro, vo)

    in_specs = [pl.BlockSpec((tm, bw), lambda i, cb=cb, rf=rf: (rf(i), cb)) for (_, bw, cb, rf) in row_ins]
    in_specs += [pl.BlockSpec(a.shape, lambda i, nd=a.ndim: (0,) * nd) for a in vec_ins]
    out_specs = [pl.BlockSpec((tm, w), lambda i: (i, 0)) for (w, _) in row_outs]
    out_specs += [pl.BlockSpec(s, lambda i, nd=len(s): (0,) * nd) for s in vec_outs]
    out_shape = [jax.ShapeDtypeStruct((n_tiles * tm, w), dt) for (w, dt) in row_outs]
    out_shape += [jax.ShapeDtypeStruct(s, F32) for s in vec_outs]
    return pl.pallas_call(kern, name=name, grid=(n_tiles,), in_specs=in_specs, out_specs=out_specs,
                          out_shape=out_shape, compiler_params=_params())(*[a for (a, _, _, _) in row_ins], *vec_ins)


def _colsum(v):
    return jnp.sum(v, axis=0, keepdims=True)


def _rowmean(v):
    return jnp.mean(v, axis=-1, keepdims=True)


def _rms(h):
    r = lax.rsqrt(_rowmean(h * h) + EPS)
    return h * r, r


def _rms_bwd(n, r, gy):
    return r * (gy - n * _rowmean(gy * n))


def _ln_stats(v):
    xc = v - _rowmean(v)
    r = lax.rsqrt(_rowmean(xc * xc) + EPS)
    return xc * r, r


def _ln_bwd(xh, r, dxh):
    return r * (dxh - _rowmean(dxh) - xh * _rowmean(dxh * xh))


def _sigmoid(v):
    return 1.0 / (1.0 + jnp.exp(-v))


def _gelu(v):
    t = jnp.tanh(GELU_C * (v + GELU_A * (v * v * v)))
    return 0.5 * v * (1.0 + t), t


def _gelu_grad(v, t):
    return 0.5 * (1.0 + t) + 0.5 * v * (1.0 - t * t) * (GELU_C * (1.0 + 3.0 * GELU_A * (v * v)))


def _modnorm_fwd(name, h, g, sc, sh, tm):
    def body(ri, vi, ro, vo):
        n, _ = _rms(ri[0][...])
        ro[0][...] = ((n * vi[0][...]) * (1.0 + vi[1][...]) + vi[2][...]).astype(BF16)

    return _rowwise(name, body, h.shape[0] // tm, tm, [_ri(h)], [g, sc, sh], [(D_MODEL, BF16)], [])[0]


def _modnorm_bwd_tile(hv, dxm, gv, scv):
    n, r = _rms(hv)
    dy = dxm * (1.0 + scv)
    return _rms_bwd(n, r, dy * gv), _colsum(dxm), _colsum(dxm * (n * gv)), _colsum(dy * n)


def _mm_row_epilogue(name, mm_args, mm_specs, dx_fn, rows_in, vecs_in, n_vec_out, epilogue, tm, exchange=()):
    nmm, nr, nv, ne = len(mm_args), len(rows_in), len(vecs_in), len(exchange)
    m = rows_in[0][0].shape[0]
    last = m // tm - 1

    def kern(*refs):
        ins = refs[:nmm + nr + nv]
        o_ref = refs[nmm + nr + nv + ne]
        vo = refs[nmm + nr + nv + ne + 1:nmm + nr + nv + ne + 1 + n_vec_out]
        if ne:
            srcs = refs[nmm + nr + nv:nmm + nr + nv + ne]
            rest = refs[nmm + nr + nv + ne + 1 + n_vec_out:]
            start, finish = _exchange_phases(srcs, rest[:ne], *rest[ne:])
            pl.when(pl.program_id(0) == 0)(start)

        @pl.when(pl.program_id(0) == 0)
        def _():
            for r in vo:
                r[...] = jnp.zeros(r.shape, F32)

        row_out, incs = epilogue(dx_fn(*ins[:nmm]), [r[...] for r in ins[nmm:nmm + nr]],
                                 [r[...] for r in ins[nmm + nr:]])
        o_ref[...] = row_out
        for r, inc in zip(vo, incs):
            r[...] += inc
        if ne:
            pl.when(pl.program_id(0) == last)(finish)

    row = pl.BlockSpec((tm, D_MODEL), lambda i: (i, 0))
    vec = pl.BlockSpec((1, D_MODEL), lambda i: (0, 0))
    row_specs = [row if fn is None else pl.BlockSpec((tm, D_MODEL), lambda i, fn=fn: (fn(i), 0)) for _, fn in rows_in]
    return pl.pallas_call(
        kern, name=name, grid=(m // tm,), in_specs=list(mm_specs) + row_specs + [vec] * nv + [_ANY] * ne,
        out_specs=[row] + [vec] * n_vec_out + [_ANY] * ne,
        out_shape=[jax.ShapeDtypeStruct((m, D_MODEL), F32)] + [jax.ShapeDtypeStruct((1, D_MODEL), F32)] * n_vec_out
        + [jax.ShapeDtypeStruct(e.shape, e.dtype) for e in exchange],
        scratch_shapes=_exchange_scratch(ne) if ne else [],
        compiler_params=_params())(*mm_args, *[r for r, _ in rows_in], *vecs_in, *exchange)


def _norm_bwd_epilogue(dxm, rows, vecs):
    if len(rows) > 2:
        dxm = dxm + rows[2]
    dx, dsh, dsc, dg = _modnorm_bwd_tile(rows[0], dxm, vecs[0], vecs[1])
    return rows[1] + dx, [dsh, dsc, dg]


def _lnsilu_bwd_epilogue(ds, rows, vecs):
    xh, r = _ln_stats(rows[0])
    ln = xh * vecs[0] + vecs[1]
    sg = _sigmoid(ln)
    dln = ds * (sg * (1.0 + ln * (1.0 - sg)))
    return _ln_bwd(xh, r, dln * vecs[0]), [_colsum(dln * xh), _colsum(dln)]


def _modnorm_bwd(name, h, dxs, dh_in, g, sc, tm):
    ndx = len(dxs)

    def body(ri, vi, ro, vo):
        dxm = ri[1][...].astype(F32)
        for r in ri[2:1 + ndx]:
            dxm = dxm + r[...].astype(F32)
        dx, dsh, dsc, dg = _modnorm_bwd_tile(ri[0][...], dxm, vi[0][...], vi[1][...])
        vo[0][...] += dsh
        vo[1][...] += dsc
        vo[2][...] += dg
        if dh_in is not None:
            ro[0][...] = ri[1 + ndx][...] + dx

    row_ins = [_ri(h)] + list(dxs) + ([_ri(dh_in)] if dh_in is not None else [])
    row_outs = [(D_MODEL, F32)] if dh_in is not None else []
    outs = _rowwise(name, body, h.shape[0] // tm, tm, row_ins, [g, sc], row_outs, [(1, D_MODEL)] * 3)
    if dh_in is None:
        return (None, *outs)
    return tuple(outs)


def _resgate_bwd(name, dh, y, gt, tm):
    def body(ri, vi, ro, vo):
        d = ri[0][...]
        dy = d * vi[0][...]
        ro[0][...] = dy.astype(BF16)
        vo[0][...] += _colsum(d * ri[1][...])
        vo[1][...] += _colsum(dy)

    return _rowwise(name, body, dh.shape[0] // tm, tm, [_ri(dh), _ri(y)], [gt], [(D_MODEL, BF16)],
                    [(1, D_MODEL)] * 2)


def _head_mean(v, bd):
    hi = v.astype(BF16)
    lo = (v - hi.astype(F32)).astype(BF16)
    s = jnp.dot(hi, bd, preferred_element_type=F32) + jnp.dot(lo, bd, preferred_element_type=F32)
    return s * (1.0 / HEAD_DIM)


def _swap16(v):
    w = v.shape[1]
    lane = lax.broadcasted_iota(jnp.int32, v.shape, 1)
    return jnp.where((lane & 16) == 0, pltpu.roll(v, w - 16, 1), pltpu.roll(v, 16, 1))


def _rope(v, cos, sin):
    return v * cos + _swap16(v) * sin


def _rope_bwd(d, cos, sin):
    return d * cos + _swap16(d * sin)


def _q_wide(t):
    return jnp.concatenate([t] * (ATTN_W // KV_W), axis=1)


def _qk_prep_fwd(name, p_qkv, cos, sin, gq, gk, bdq, bdk, tm):
    def body(ri, vi, ro, vo):
        pq = ri[0][...]
        pkv = ri[1][...]
        cosv = ri[2][...]
        sinv = ri[3][...]
        rq = lax.rsqrt(_head_mean(pq * pq, vi[2][...]) + EPS)
        ro[0][...] = (_rope(pq * rq * vi[0][...], _q_wide(cosv), _q_wide(sinv)) * (HEAD_DIM ** -0.5)).astype(BF16)
        pk = pkv[:, :KV_W]
        rk = lax.rsqrt(_head_mean(pk * pk, vi[3][...]) + EPS)
        ro[1][...] = _rope(pk * rk * vi[1][...], cosv, sinv).astype(BF16)
        ro[2][...] = pkv[:, KV_W:].astype(BF16)

    return _rowwise(name, body, p_qkv.shape[0] // tm, tm,
                    [_ri(p_qkv, ATTN_W, 0), _ri(p_qkv, 2 * KV_W, 2), _ri(cos), _ri(sin)], [gq, gk, bdq, bdk],
                    [(ATTN_W, BF16), (KV_W, BF16), (KV_W, BF16)], [])


def _qk_prep_bwd(name, p_qkv, cos, sin, dq, n_ctx_tiles, dk, dv, gq, gk, bdq, bdk, tm):
    def norm_rope_bwd(p, gain, bd, cosv, sinv, dout):
        r = lax.rsqrt(_head_mean(p * p, bd) + EPS)
        n = p * r
        dqn = _rope_bwd(dout, cosv, sinv)
        gy = dqn * gain
        return r * (gy - n * _head_mean(gy * n, bd)), _colsum(dqn * n)

    def body(ri, vi, ro, vo):
        pkv = ri[1][...]
        cosv = ri[2][...]
        sinv = ri[3][...]
        dqv = jnp.where(pl.program_id(0) >= n_ctx_tiles, ri[4][...] * (HEAD_DIM ** -0.5), 0.0)
        dxq, dgq = norm_rope_bwd(ri[0][...], vi[0][...], vi[2][...], _q_wide(cosv), _q_wide(sinv), dqv)
        dxk, dgk = norm_rope_bwd(pkv[:, :KV_W], vi[1][...], vi[3][...], cosv, sinv, ri[5][...])
        ro[0][:, :ATTN_W] = dxq.astype(BF16)
        ro[0][:, ATTN_W:ATTN_W + KV_W] = dxk.astype(BF16)
        ro[0][:, ATTN_W + KV_W:] = ri[6][...].astype(BF16)
        vo[0][...] += dgq
        vo[1][...] += dgk

    return _rowwise(name, body, p_qkv.shape[0] // tm, tm,
                    [_ri(p_qkv, ATTN_W, 0), _ri(p_qkv, 2 * KV_W, 2), _ri(cos), _ri(sin),
                     _ri(dq, row_fn=lambda i: jnp.maximum(i - n_ctx_tiles, 0)), _ri(dk), _ri(dv)],
                    [gq, gk, bdq, bdk], [(QKV_W, BF16)], [(1, ATTN_W), (1, KV_W)])


def _sg_fwd(name, p_sg, w_sp, bsp, tm):
    def body(ri, vi, ro, vo):
        p = ri[0][...]
        gu, _ = _gelu(p[:, :SG_W])
        gv, _ = _gelu(p[:, SG_W:])
        for g in range(N_SG):
            sl = slice(g * CHUNK, (g + 1) * CHUNK)
            vn, _ = _ln_stats(gv[:, sl])
            vnb = vn.astype(BF16)
            for m in range(tm // CHUNK):
                rs = slice(m * CHUNK, (m + 1) * CHUNK)
                mixed = jnp.dot(vi[0][g], vnb[rs], preferred_element_type=F32) + vi[1][:, sl]
                ro[0][rs, sl] = (gu[rs, sl] * mixed).astype(BF16)

    return _rowwise(name, body, p_sg.shape[0] // tm, tm, [_ri(p_sg)], [w_sp, bsp], [(SG_W, BF16)], [])[0]


def _sg_bwd(name, p_sg, dcat, w_sp, w_sp_t, bsp, tm):
    def body(ri, vi, ro, vo):
        p = ri[0][...]
        dsg = ri[1][...]
        su = p[:, :SG_W]
        sv = p[:, SG_W:]
        gu, tu = _gelu(su)
        gv, tv = _gelu(sv)
        for g in range(N_SG):
            sl = slice(g * CHUNK, (g + 1) * CHUNK)
            vn, r = _ln_stats(gv[:, sl])
            vnb = vn.astype(BF16)
            for m in range(tm // CHUNK):
                rs = slice(m * CHUNK, (m + 1) * CHUNK)
                mixed = jnp.dot(vi[0][g], vnb[rs], preferred_element_type=F32) + vi[2][:, sl]
                d_o = dsg[rs, sl]
                dm = d_o * gu[rs, sl]
                vo[1][:, sl] += dm
                dmb = dm.astype(BF16)
                vo[0][g] += lax.dot_general(dmb, vnb[rs], (((1,), (1,)), ((), ())), preferred_element_type=F32)
                dvn = jnp.dot(vi[1][g], dmb, preferred_element_type=F32)
                dgv = _ln_bwd(vn[rs], r[rs], dvn)
                ro[0][rs, sl] = (d_o * mixed * _gelu_grad(su[rs, sl], tu[rs, sl])).astype(BF16)
                ro[0][rs, SG_W + g * CHUNK:SG_W + (g + 1) * CHUNK] = (
                    dgv * _gelu_grad(sv[rs, sl], tv[rs, sl])).astype(BF16)

    return _rowwise(name, body, p_sg.shape[0] // tm, tm, [_ri(p_sg), _ri(dcat, SG_W, 1)], [w_sp, w_sp_t, bsp],
                    [(2 * SG_W, BF16)], [(N_SG, CHUNK, CHUNK), (CHUNK, SG_W)])


def _glu_fwd(name, ag, b, tm):
    def body(ri, vi, ro, vo):
        v = ri[0][...] + vi[0][...]
        ro[0][...] = v[:, :D_MODEL] * _sigmoid(v[:, D_MODEL:])

    return _rowwise(name, body, ag.shape[0] // tm, tm, [_ri(ag)], [b], [(D_MODEL, F32)], [])[0]


def _glu_bwd(name, ag, dhg, b, tm):
    def body(ri, vi, ro, vo):
        v = ri[0][...] + vi[0][...]
        d = ri[1][...]
        a = v[:, :D_MODEL]
        sg = _sigmoid(v[:, D_MODEL:])
        da = d * sg
        dgate = d * a * sg * (1.0 - sg)
        ro[0][:, :D_MODEL] = da.astype(BF16)
        ro[0][:, D_MODEL:] = dgate.astype(BF16)
        vo[0][:, :D_MODEL] += _colsum(da)
        vo[0][:, D_MODEL:] += _colsum(dgate)

    return _rowwise(name, body, ag.shape[0] // tm, tm, [_ri(ag), _ri(dhg)], [b], [(2 * D_MODEL, BF16)],
                    [(1, 2 * D_MODEL)])


def _lnsilu_fwd(name, hc, g, b, tm):
    def body(ri, vi, ro, vo):
        xh, _ = _ln_stats(ri[0][...])
        ln = xh * vi[0][...] + vi[1][...]
        ro[0][...] = (ln * _sigmoid(ln)).astype(BF16)

    return _rowwise(name, body, hc.shape[0] // tm, tm, [_ri(hc)], [g, b], [(D_MODEL, BF16)], [])[0]


def _conv_blocks(seq):
    cb = 128
    tt = 128 if seq % 128 == 0 else seq
    return cb, tt


def _conv_taps(win, tt):
    n = win.shape[0]
    for s in range(8):
        ws = win if s == 0 else pltpu.roll(win, n - s, 0)
        for q in range(4):
            j = 8 * q + s - 1
            if 0 <= j < CONV_W:
                yield j, ws[8 * q:8 * q + tt, :]


def _fill_padded(pad_ref, x_ref, seq):
    zeros = jnp.zeros((CONV_HALO, pad_ref.shape[1]), F32)
    pad_ref[0:CONV_HALO, :] = zeros
    pad_ref[seq + CONV_HALO:seq + 2 * CONV_HALO, :] = zeros
    pad_ref[CONV_HALO:seq + CONV_HALO, :] = x_ref[...]


def _dwconv(name, xin, w, b):
    seq = xin.shape[0]
    cb, tt = _conv_blocks(seq)

    def kern(x_ref, w_ref, b_ref, o_ref, pad_ref):
        _fill_padded(pad_ref, x_ref, seq)
        wv = w_ref[...]
        bv = b_ref[...]

        def step(t, carry):
            base = pl.multiple_of(t * tt, tt)
            win = pad_ref[pl.ds(base, tt + 2 * CONV_HALO), :]
            acc = jnp.zeros((tt, cb), F32) + bv
            for j, rows in _conv_taps(win, tt):
                acc = acc + wv[j:j + 1, :] * rows
            o_ref[pl.ds(base, tt), :] = acc
            return carry

        lax.fori_loop(0, seq // tt, step, 0)

    return pl.pallas_call(
        kern, name=name, grid=(D_MODEL // cb,),
        in_specs=[pl.BlockSpec((seq, cb), lambda j: (0, j)), pl.BlockSpec((CONV_W, cb), lambda j: (0, j)),
                  pl.BlockSpec((1, cb), lambda j: (0, j))],
        out_specs=pl.BlockSpec((seq, cb), lambda j: (0, j)),
        out_shape=jax.ShapeDtypeStruct((seq, D_MODEL), F32),
        scratch_shapes=[pltpu.VMEM((seq + 2 * CONV_HALO, cb), F32)],
        compiler_params=_params())(xin, w, b)


def _dwconv_wgrad(name, xin, dout):
    seq = xin.shape[0]
    cb, tt = _conv_blocks(seq)

    def kern(x_ref, d_ref, dw_ref, db_ref, pad_ref):
        _fill_padded(pad_ref, x_ref, seq)
        dw_ref[...] = jnp.zeros(dw_ref.shape, F32)
        db_ref[...] = jnp.zeros(db_ref.shape, F32)

        def step(t, carry):
            base = pl.multiple_of(t * tt, tt)
            win = pad_ref[pl.ds(base, tt + 2 * CONV_HALO), :]
            d = d_ref[pl.ds(base, tt), :]
            db_ref[...] += _colsum(d)
            for j, rows in _conv_taps(win, tt):
                dw_ref[j:j + 1, :] += _colsum(d * rows)
            return carry

        lax.fori_loop(0, seq // tt, step, 0)

    return pl.pallas_call(
        kern, name=name, grid=(D_MODEL // cb,),
        in_specs=[pl.BlockSpec((seq, cb), lambda j: (0, j)), pl.BlockSpec((seq, cb), lambda j: (0, j))],
        out_specs=[pl.BlockSpec((CONV_W, cb), lambda j: (0, j)), pl.BlockSpec((1, cb), lambda j: (0, j))],
        out_shape=[jax.ShapeDtypeStruct((CONV_W, D_MODEL), F32), jax.ShapeDtypeStruct((1, D_MODEL), F32)],
        scratch_shapes=[pltpu.VMEM((seq + 2 * CONV_HALO, cb), F32)],
        compiler_params=_params())(xin, dout)


def _mm(name, a, b, out_sds, grid, a_spec, b_spec, o_spec, contract, k_axis=None, bias=None, pair_send=()):
    dn = (contract, ((), ()))
    ns = len(pair_send)
    nb = 0 if bias is None else 1
    n_steps = math.prod(grid)

    def kern(*refs):
        a_ref, b_ref = refs[:2]
        bias_ref = refs[2] if nb else None
        o_ref = refs[2 + nb + ns]
        if ns:
            copies = _pair_send_copies(refs[2 + nb:2 + nb + ns], refs[3 + nb + ns:3 + nb + 2 * ns],
                                       *refs[3 + nb + 2 * ns:])
            step = pl.program_id(0)
            for ax in range(1, len(grid)):
                step = step * grid[ax] + pl.program_id(ax)

            @pl.when(step == 0)
            def _():
                for cp in copies():
                    cp.start()

        p = lax.dot_general(a_ref[...].astype(BF16), b_ref[...].astype(BF16), dn, preferred_element_type=F32)
        if bias is not None:
            p = p + bias_ref[...]
        if k_axis is None:
            o_ref[...] = p.astype(o_ref.dtype)
        else:
            k = pl.program_id(k_axis)

            @pl.when(k == 0)
            def _():
                o_ref[...] = p

            @pl.when(k > 0)
            def _():
                o_ref[...] += p

        if ns:
            @pl.when(step == n_steps - 1)
            def _():
                for cp in copies():
                    cp.wait()

    in_specs = [a_spec, b_spec]
    args = [a, b]
    if bias is not None:
        in_specs.append(pl.BlockSpec(bias.shape, lambda *_: (0,) * bias.ndim))
        args.append(bias)
    if not ns:
        return pl.pallas_call(kern, name=name, grid=grid, in_specs=in_specs, out_specs=o_spec, out_shape=out_sds,
                              compiler_params=_params())(*args)
    return pl.pallas_call(
        kern, name=name, grid=grid, in_specs=in_specs + [_ANY] * ns, out_specs=[o_spec] + [_ANY] * ns,
        out_shape=[out_sds] + [jax.ShapeDtypeStruct((N_CHIP,) + g.shape[2:], g.dtype) for g in pair_send],
        scratch_shapes=[pltpu.SemaphoreType.DMA((ns, N_CHIP)), pltpu.SemaphoreType.DMA((ns, N_CHIP))],
        compiler_params=_params())(*args, *pair_send)


def _mm_rows(name, a, b, tm, trans_b=False, bias=None):
    m, k = a.shape
    n = b.shape[0] if trans_b else b.shape[1]
    contract = ((1,), (1,)) if trans_b else ((1,), (0,))
    return _mm(name, a, b, jax.ShapeDtypeStruct((m, n), F32), (m // tm,),
               pl.BlockSpec((tm, k), lambda i: (i, 0)), pl.BlockSpec(b.shape, lambda i: (0, 0)),
               pl.BlockSpec((tm, n), lambda i: (i, 0)), contract, bias=bias)


def _mm_rows_res(name, a, b, h, gt, tm, bias=None, norm=None, final=None):
    m, k = a.shape
    n = b.shape[1]
    nb = 0 if bias is None else 1
    extra = list(norm or ()) + list(final or ())

    def kern(a_ref, b_ref, h_ref, gt_ref, *rest):
        y = jnp.dot(a_ref[...], b_ref[...], preferred_element_type=F32)
        if nb:
            y = y + rest[0][...]
        ex = rest[nb:nb + len(extra)]
        outs = rest[nb + len(extra):]
        outs[0][...] = y
        hn = h_ref[...] + gt_ref[...] * y
        if final is None:
            outs[1][...] = hn
        if norm is not None:
            nn, _ = _rms(hn)
            outs[2][...] = ((nn * ex[0][...]) * (1.0 + ex[1][...]) + ex[2][...]).astype(BF16)
        if final is not None:
            @pl.when(pl.program_id(0) == 0)
            def _():
                outs[2][...] = jnp.zeros(outs[2].shape, F32)
                outs[3][...] = jnp.zeros(outs[3].shape, F32)

            gv = ex[1][...]
            nn, r = _rms(hn)
            e = nn * gv - ex[0][...]
            outs[3][...] += _colsum(e * e) * (0.5 / D_MODEL)
            dout = e * (1.0 / D_MODEL)
            outs[2][...] += _colsum(dout * nn)
            outs[1][...] = _rms_bwd(nn, r, dout * gv)

    row = pl.BlockSpec((tm, n), lambda i: (i, 0))
    vec = pl.BlockSpec((1, n), lambda i: (0, 0))
    sds = jax.ShapeDtypeStruct((m, n), F32)
    vsd = jax.ShapeDtypeStruct((1, n), F32)
    in_specs = [pl.BlockSpec((tm, k), lambda i: (i, 0)), pl.BlockSpec(b.shape, lambda i: (0, 0)), row, vec] + [vec] * nb
    out_specs, out_shape = [row, row], [sds, sds]
    if norm is not None:
        in_specs += [vec] * 3
        out_specs.append(row)
        out_shape.append(jax.ShapeDtypeStruct((m, n), BF16))
    if final is not None:
        in_specs += [row, vec]
        out_specs += [vec, vec]
        out_shape += [vsd, vsd]
    return pl.pallas_call(kern, name=name, grid=(m // tm,), in_specs=in_specs, out_specs=out_specs,
                          out_shape=out_shape, compiler_params=_params())(
        a, b, h, gt, *([bias] if nb else []), *extra)


def _ffn_in_swiglu(name, xf, wblk, tm, gather=()):
    m, k = xf.shape
    nblk = wblk.shape[2]
    ng = len(gather)
    n_i = m // tm
    last = 2 * n_i - 1

    def kern(x_ref, wg_ref, wu_ref, *rest):
        gu_ref, a_ref = rest[ng:ng + 2]
        if ng:
            start, forward, finish = _gather_phases(rest[:ng], rest[ng + 2:2 * ng + 2], *rest[2 * ng + 2:])
            step = pl.program_id(0) * n_i + pl.program_id(1)
            pl.when(step == 0)(start)
            pl.when(step == (3 * last) // 4)(forward)
        xv = x_ref[...]
        g = jnp.dot(xv, wg_ref[...], preferred_element_type=F32)
        u = jnp.dot(xv, wu_ref[...], preferred_element_type=F32)
        gu_ref[0] = g.astype(BF16)
        gu_ref[1] = u.astype(BF16)
        a_ref[...] = (g * _sigmoid(g) * u).astype(BF16)
        if ng:
            pl.when(step == last)(finish)

    return pl.pallas_call(
        kern, name=name, grid=(2, n_i),
        in_specs=[pl.BlockSpec((tm, k), lambda j, i: (i, 0)), pl.BlockSpec((None, k, nblk), lambda j, i: (j, 0, 0)),
                  pl.BlockSpec((None, k, nblk), lambda j, i: (j + 2, 0, 0))] + [_ANY] * ng,
        out_specs=[pl.BlockSpec((2, tm, nblk), lambda j, i: (0, i, j)),
                   pl.BlockSpec((tm, nblk), lambda j, i: (i, j))] + [_ANY] * ng,
        out_shape=[jax.ShapeDtypeStruct((2, m, 2 * nblk), BF16), jax.ShapeDtypeStruct((m, 2 * nblk), BF16)]
        + [jax.ShapeDtypeStruct((N_DEV,) + g.shape, g.dtype) for g in gather],
        scratch_shapes=_gather_scratch(ng) if ng else [],
        compiler_params=_params())(xf, wblk, wblk, *gather)


def _ffn_out_dx_swiglu(name, dy, wo, gu, tm):
    m, k = dy.shape
    nblk = gu.shape[2] // 2

    def kern(dy_ref, w_ref, gu_ref, o_ref):
        da = lax.dot_general(dy_ref[...], w_ref[...], (((1,), (1,)), ((), ())), preferred_element_type=F32)
        g = gu_ref[0].astype(F32)
        sg = _sigmoid(g)
        o_ref[0] = (da * gu_ref[1].astype(F32) * (sg * (1.0 + g * (1.0 - sg)))).astype(BF16)
        o_ref[1] = (da * (g * sg)).astype(BF16)

    return pl.pallas_call(
        kern, name=name, grid=(2, m // tm),
        in_specs=[pl.BlockSpec((tm, k), lambda j, i: (i, 0)), pl.BlockSpec((nblk, k), lambda j, i: (j, 0)),
                  pl.BlockSpec((2, tm, nblk), lambda j, i: (0, i, j))],
        out_specs=pl.BlockSpec((2, tm, nblk), lambda j, i: (0, i, j)),
        out_shape=jax.ShapeDtypeStruct(gu.shape, BF16), compiler_params=_params())(dy, wo, gu)


def _ffn_in_dx_norm_bwd(name, dgu, wblk, h, dh_in, g, sc, tm):
    nb, k, nblk = wblk.shape

    def dx_fn(a_ref, w_ref):
        acc = None
        for j in range(nb):
            p = lax.dot_general(a_ref[j // 2, :, (j % 2) * nblk:(j % 2 + 1) * nblk], w_ref[j],
                                (((1,), (1,)), ((), ())), preferred_element_type=F32)
            acc = p if acc is None else acc + p
        return acc

    return _mm_row_epilogue(name, [dgu, wblk],
                            [pl.BlockSpec((2, tm, 2 * nblk), lambda i: (0, i, 0)),
                             pl.BlockSpec(wblk.shape, lambda i: (0, 0, 0))], dx_fn, [(h, None), (dh_in, None)],
                            [g, sc], 3, _norm_bwd_epilogue, tm)


def _mm_t_epilogue(name, a, b, rows_in, vecs_in, n_vec_out, epilogue, tm, exchange=()):
    def dx_fn(a_ref, b_ref):
        return lax.dot_general(a_ref[...], b_ref[...], (((1,), (1,)), ((), ())), preferred_element_type=F32)

    return _mm_row_epilogue(name, [a, b], [pl.BlockSpec((tm, a.shape[1]), lambda i: (i, 0)),
                                          pl.BlockSpec(b.shape, lambda i: (0, 0))], dx_fn, rows_in, vecs_in,
                            n_vec_out, epilogue, tm, exchange)


def _mm_halves_wgrad_blk(name, a, b2, tk, pair_send=()):
    t, k = a.shape
    nblk = b2.shape[2] // 2
    return _mm(name, a, b2, jax.ShapeDtypeStruct((N_CHIP, k, nblk), F32), (N_CHIP, t // tk),
               pl.BlockSpec((tk, k), lambda j, i: (i, 0)),
               pl.BlockSpec((None, tk, nblk), lambda j, i: (j // 2, i, j % 2)),
               pl.BlockSpec((None, k, nblk), lambda j, i: (j, 0, 0)), ((0,), (0,)), k_axis=1, pair_send=pair_send)


def _mm_wgrad(name, a, b, tk):
    t, k = a.shape
    n = b.shape[1]
    tn = n if k * n * 4 <= 4 * 1024 * 1024 else 512
    return _mm(name, a, b, jax.ShapeDtypeStruct((k, n), F32), (n // tn, t // tk),
               pl.BlockSpec((tk, k), lambda j, i: (i, 0)), pl.BlockSpec((tk, tn), lambda j, i: (i, j)),
               pl.BlockSpec((k, tn), lambda j, i: (0, j)), ((0,), (0,)), k_axis=1)


def _attn_fwd(name, q_all, k, v, n_ctx, tq, gather=()):
    r = k.shape[1]
    s = q_all.shape[0] - n_ctx
    gw = Q_GROUP * HEAD_DIM
    ng = len(gather)
    n_i = s // tq
    last = N_KV_HEADS * n_i - 1
    forward_at = (7 * last) // 8

    def kern(q_ref, k_ref, v_ref, *rest):
        o_ref, lse_ref = rest[ng], rest[ng + 1]
        if ng:
            start, forward, finish = _gather_phases(rest[:ng], rest[ng + 2:2 * ng + 2], *rest[2 * ng + 2:])
            step = pl.program_id(0) * n_i + pl.program_id(1)
            pl.when(step == 0)(start)
            pl.when(step == forward_at)(forward)
        kv = k_ref[...]
        vv = v_ref[...]
        for g in range(Q_GROUP):
            sl = slice(g * HEAD_DIM, (g + 1) * HEAD_DIM)
            sc = lax.dot_general(q_ref[:, sl], kv, (((1,), (1,)), ((), ())), preferred_element_type=F32)
            m = jnp.max(sc, axis=-1, keepdims=True)
            p = jnp.exp(sc - m)
            ol = jnp.dot(p.astype(BF16), vv, preferred_element_type=F32)
            l = ol[:, HEAD_DIM:HEAD_DIM + 1]
            o_ref[:, sl] = ol[:, :HEAD_DIM] / l
            lse_ref[g] = m + jnp.log(l)
        if ng:
            pl.when(step == last)(finish)

    return pl.pallas_call(
        kern, name=name, grid=(N_KV_HEADS, n_i),
        in_specs=[pl.BlockSpec((tq, gw), lambda j, i: (i + n_ctx // tq, j)),
                  pl.BlockSpec((None, r, HEAD_DIM), lambda j, i: (j, 0, 0)),
                  pl.BlockSpec((None, r, 2 * HEAD_DIM), lambda j, i: (j, 0, 0))] + [_ANY] * ng,
        out_specs=[pl.BlockSpec((tq, gw), lambda j, i: (i, j)),
                   pl.BlockSpec((Q_GROUP, tq, 1), lambda j, i: (j, i, 0))] + [_ANY] * ng,
        out_shape=[jax.ShapeDtypeStruct((s, ATTN_W), F32), jax.ShapeDtypeStruct((N_Q_HEADS, s, 1), F32)]
        + [jax.ShapeDtypeStruct((N_DEV,) + a.shape, a.dtype) for a in gather],
        scratch_shapes=_gather_scratch(ng) if ng else [],
        compiler_params=_params())(q_all, k, v, *gather)


def _attn_bwd(name, q_all, k, v, o, lse, dcat, n_ctx, tq, exchange=()):
    nkv, r, _ = k.shape
    s = o.shape[0]
    gw = Q_GROUP * HEAD_DIM
    ne = len(exchange)
    n_i = s // tq
    last = nkv * n_i - 1

    def kern(q_ref, k_ref, v_ref, o_ref, lse_ref, do_ref, *rest):
        dq_ref, dk_ref, dv_ref = rest[ne:ne + 3]
        ds_all, p_all = rest[2 * ne + 3:2 * ne + 5]
        if ne:
            start, finish = _exchange_phases(rest[:ne], rest[ne + 3:2 * ne + 3], *rest[2 * ne + 5:])
            step = pl.program_id(0) * n_i + pl.program_id(1)
            pl.when(step == 0)(start)

        @pl.when(pl.program_id(1) == 0)
        def _():
            dk_ref[...] = jnp.zeros(dk_ref.shape, F32)
            dv_ref[...] = jnp.zeros(dv_ref.shape, F32)

        kv = k_ref[...]
        vv = v_ref[...]
        qs = []
        dos = []
        for g in range(Q_GROUP):
            sl = slice(g * HEAD_DIM, (g + 1) * HEAD_DIM)
            rows = slice(g * tq, (g + 1) * tq)
            qv = q_ref[:, sl]
            dov = do_ref[:, sl]
            delta = jnp.sum(dov * o_ref[:, sl], axis=-1, keepdims=True)
            sc = lax.dot_general(qv, kv, (((1,), (1,)), ((), ())), preferred_element_type=F32)
            p = jnp.exp(sc - lse_ref[g])
            dob = dov.astype(BF16)
            dp = lax.dot_general(dob, vv, (((1,), (1,)), ((), ())), preferred_element_type=F32)
            dsb = (p * (dp - delta)).astype(BF16)
            dq_ref[:, sl] = jnp.dot(dsb, kv, preferred_element_type=F32)
            ds_all[rows, :] = dsb
            p_all[rows, :] = p.astype(BF16)
            qs.append(qv)
            dos.append(dob)
        dk_ref[...] += lax.dot_general(jnp.concatenate(qs, axis=0), ds_all[...], (((0,), (0,)), ((), ())),
                                       preferred_element_type=F32)
        dv_ref[...] += lax.dot_general(jnp.concatenate(dos, axis=0), p_all[...], (((0,), (0,)), ((), ())),
                                       preferred_element_type=F32)
        if ne:
            pl.when(step == last)(finish)

    kvmap = lambda j, i: (j, 0, 0)
    return pl.pallas_call(
        kern, name=name, grid=(nkv, n_i),
        in_specs=[pl.BlockSpec((tq, gw), lambda j, i: (i + n_ctx // tq, j)), pl.BlockSpec((None, r, HEAD_DIM), kvmap),
                  pl.BlockSpec((None, r, HEAD_DIM), kvmap), pl.BlockSpec((tq, gw), lambda j, i: (i, j)),
                  pl.BlockSpec((Q_GROUP, tq, 1), lambda j, i: (j, i, 0)),
                  pl.BlockSpec((tq, gw), lambda j, i: (i, j))] + [_ANY] * ne,
        out_specs=[pl.BlockSpec((tq, gw), lambda j, i: (i, j)), pl.BlockSpec((None, HEAD_DIM, r), kvmap),
                   pl.BlockSpec((None, HEAD_DIM, r), kvmap)] + [_ANY] * ne,
        out_shape=[jax.ShapeDtypeStruct((s, ATTN_W), F32), jax.ShapeDtypeStruct((nkv, HEAD_DIM, r), F32),
                   jax.ShapeDtypeStruct((nkv, HEAD_DIM, r), F32)]
        + [jax.ShapeDtypeStruct(a.shape, a.dtype) for a in exchange],
        scratch_shapes=[pltpu.VMEM((Q_GROUP * tq, r), BF16)] * 2 + (_exchange_scratch(ne) if ne else []),
        compiler_params=_params())(q_all, k, v, o, lse, dcat, *exchange)


def _mod_fwd(name, c16, wm, bm):
    n = wm.shape[2]

    def kern(c_ref, w_ref, b_ref, o_ref):
        cv = c_ref[...]
        a = (cv * _sigmoid(cv)).astype(BF16)
        o_ref[...] = jnp.dot(a, w_ref[...].astype(BF16), preferred_element_type=F32) + b_ref[...]

    return pl.pallas_call(
        kern, name=name, grid=(2,),
        in_specs=[pl.BlockSpec(c16.shape, lambda l: (0, 0)), pl.BlockSpec((None, D_MODEL, n), lambda l: (l, 0, 0)),
                  pl.BlockSpec((None, 1, n), lambda l: (l, 0, 0))],
        out_specs=pl.BlockSpec((None, 16, n), lambda l: (l, 0, 0)),
        out_shape=jax.ShapeDtypeStruct((2, 16, n), F32), compiler_params=_params())(c16, wm, bm)


def _mod_wgrad(name, c16, dm):
    n = dm.shape[2]
    tn = 512

    def kern(c_ref, d_ref, o_ref):
        cv = c_ref[...]
        a = cv * _sigmoid(cv)
        o_ref[...] = lax.dot_general(a, d_ref[...], (((0,), (0,)), ((), ())), preferred_element_type=F32,
                                     precision=lax.Precision.HIGHEST)

    return pl.pallas_call(
        kern, name=name, grid=(2, n // tn),
        in_specs=[pl.BlockSpec(c16.shape, lambda l, j: (0, 0)), pl.BlockSpec((None, 16, tn), lambda l, j: (l, 0, j))],
        out_specs=pl.BlockSpec((None, D_MODEL, tn), lambda l, j: (l, 0, j)),
        out_shape=jax.ShapeDtypeStruct((2, D_MODEL, n), F32), compiler_params=_params())(c16, dm)


def _cctx_grad(name, parts, c_ctx):
    def kern(p_ref, c_ref, o_ref):
        d = p_ref[0, 0:1, :] + p_ref[2, 0:1, :] + p_ref[4, 0:1, :] + p_ref[6, 0:1, :]
        cv = c_ref[...]
        sg = _sigmoid(cv)
        o_ref[...] = d * (sg * (1.0 + cv * (1.0 - sg)))

    return pl.pallas_call(kern, name=name, out_shape=jax.ShapeDtypeStruct((1, D_MODEL), F32),
                          compiler_params=_params())(parts, c_ctx)


def _sum_slots(name, g, tr):
    n, rows, cols = g.shape

    def kern(g_ref, o_ref):
        acc = g_ref[0].astype(F32)
        for i in range(1, n):
            acc = acc + g_ref[i].astype(F32)
        o_ref[...] = acc

    return pl.pallas_call(kern, name=name, grid=(rows // tr,),
                          in_specs=[pl.BlockSpec((n, tr, cols), lambda i: (0, i, 0))],
                          out_specs=pl.BlockSpec((tr, cols), lambda i: (i, 0)),
                          out_shape=jax.ShapeDtypeStruct((rows, cols), F32), compiler_params=_params())(g)


def _sum_into(name, g, cvec, n_layers, layer, prev=None):
    n, r, cc = g.shape
    tr = _row_tile(r, cc, 512 * 1024)

    def kern(c_ref, g_ref, *rest):
        acc = g_ref[0].astype(F32)
        for i in range(1, n):
            acc = acc + g_ref[i].astype(F32)
        rest[-1][...] = acc

    in_specs = [pl.BlockSpec((n, tr, cc), lambda i, c_ref: (0, i, 0))]
    args = [cvec, g]
    aliases = {}
    if prev is not None:
        in_specs.append(_ANY)
        args.append(prev)
        aliases = {2: 0}
    grid_spec = pltpu.PrefetchScalarGridSpec(
        num_scalar_prefetch=1, grid=(r // tr,), in_specs=in_specs,
        out_specs=pl.BlockSpec((None, None, tr, cc), lambda i, c_ref: (layer, c_ref[0], i, 0)))
    return pl.pallas_call(kern, name=name, grid_spec=grid_spec,
                          out_shape=jax.ShapeDtypeStruct((n_layers, 2, r, cc), F32),
                          input_output_aliases=aliases, compiler_params=_params())(*args)


def _row_tile(rows, cols, max_bytes=1024 * 1024):
    if rows * cols * 4 <= 2 * max_bytes:
        return rows
    best = None
    for t in range(16, rows + 1, 16):
        if rows % t == 0 and t * cols * 4 <= max_bytes:
            best = t
    assert best is not None, (rows, cols)
    return best


def _adamw(name, w, g, m, v, gather=()):
    shape = w.shape
    cols = shape[-1]
    rows = w.size // cols
    tr = _row_tile(rows, cols)
    ng = len(gather)
    last = rows // tr - 1

    def kern(w_ref, g_ref, m_ref, v_ref, *rest):
        d_ref, nm_ref, nv_ref = rest[ng:ng + 3]
        if ng:
            start, forward, finish = _gather_phases(rest[:ng], rest[ng + 3:2 * ng + 3], *rest[2 * ng + 3:])
            pl.when(pl.program_id(0) == 0)(start)
            pl.when(pl.program_id(0) == last // 2)(forward)
        gv = g_ref[...]
        m2 = ADAM_B1 * m_ref[...] + (1.0 - ADAM_B1) * gv
        v2 = ADAM_B2 * v_ref[...] + (1.0 - ADAM_B2) * (gv * gv)
        m_hat = m2 / (1.0 - ADAM_B1 ** ADAM_STEP)
        v_hat = v2 / (1.0 - ADAM_B2 ** ADAM_STEP)
        d_ref[...] = -ADAM_LR * (m_hat / (jnp.sqrt(v_hat) + ADAM_EPS) + ADAM_WD * w_ref[...])
        nm_ref[...] = m2
        nv_ref[...] = v2
        if ng:
            pl.when(pl.program_id(0) == last)(finish)

    spec = pl.BlockSpec((tr, cols), lambda i: (i, 0))
    sds = jax.ShapeDtypeStruct((rows, cols), F32)
    outs = pl.pallas_call(
        kern, name=name, grid=(rows // tr,), in_specs=[spec] * 4 + [_ANY] * ng, out_specs=[spec] * 3 + [_ANY] * ng,
        out_shape=[sds] * 3 + [jax.ShapeDtypeStruct((N_DEV,) + a.shape, a.dtype) for a in gather],
        scratch_shapes=_gather_scratch(ng) if ng else [], compiler_params=_params())(
        w.reshape(rows, cols), g.reshape(rows, cols), m.reshape(rows, cols), v.reshape(rows, cols), *gather)
    return tuple(o.reshape(shape) for o in outs[:3]) + tuple(outs[3:])


_ANY = pl.BlockSpec(memory_space=pl.ANY)


def _mesh_pos():
    return lax.axis_index("x"), lax.axis_index("y"), lax.axis_index("c")


def _gather_phases(srcs, outs, send_sems, recv_sems, local_sems):
    n = len(srcs)
    x, y, c = _mesh_pos()
    me = (x, y, c)
    sibling = (x, y, 1 - c)
    chips = [(1 - x, y), (x, 1 - y), (1 - x, 1 - y)]

    def slot(px, py, pc):
        return 4 * px + 2 * py + pc

    def copy(t, k, s, to, src=None):
        dst = outs[t].at[s]
        return pltpu.make_async_remote_copy(src_ref=dst if src is None else src, dst_ref=dst,
                                            send_sem=send_sems.at[t, k], recv_sem=recv_sems.at[t, k],
                                            device_id=to, device_id_type=MESH)

    def mine(t):
        return pltpu.make_async_copy(srcs[t], outs[t].at[slot(*me)], local_sems.at[t])

    def first(t):
        return [copy(t, 0, slot(*me), sibling, src=srcs[t])] + [
            copy(t, 1 + j, slot(*me), (px, py, c), src=srcs[t]) for j, (px, py) in enumerate(chips)]

    def passed(t, j):
        px, py = chips[j]
        return copy(t, 4 + j, slot(px, py, c), sibling)

    def start():
        for t in range(n):
            mine(t).start()
        for t in range(n):
            for cp in first(t):
                cp.start()

    def forward():
        for j, (px, py) in enumerate(chips):
            for t in range(n):
                copy(t, 1 + j, slot(px, py, c), me).wait_recv()
                passed(t, j).start()

    def finish():
        for t in range(n):
            copy(t, 0, slot(x, y, 1 - c), me).wait_recv()
            for j, (px, py) in enumerate(chips):
                copy(t, 4 + j, slot(px, py, 1 - c), me).wait_recv()
        for t in range(n):
            for cp in first(t) + [passed(t, j) for j in range(len(chips))]:
                cp.wait_send()
            mine(t).wait()

    return start, forward, finish


def _gather_scratch(n):
    return [pltpu.SemaphoreType.DMA((n, 7)), pltpu.SemaphoreType.DMA((n, 7)), pltpu.SemaphoreType.DMA((n,))]


def _allgather(name, items):
    n = len(items)

    def body(*refs):
        start, forward, finish = _gather_phases(refs[:n], refs[n:2 * n], *refs[2 * n:])
        start()
        forward()
        finish()

    return pl.pallas_call(
        body, name=name, in_specs=[_ANY] * n, out_specs=[_ANY] * n,
        out_shape=[jax.ShapeDtypeStruct((N_DEV,) + a.shape, a.dtype) for a in items],
        scratch_shapes=_gather_scratch(n), compiler_params=_params())(*items)


def _pair_send_copies(srcs, bufs, send_sems, recv_sems):
    x, y, c = _mesh_pos()

    def copies():
        return [pltpu.make_async_remote_copy(
            src_ref=srcs[t].at[k, 1 - c], dst_ref=bufs[t].at[k], send_sem=send_sems.at[t, k],
            recv_sem=recv_sems.at[t, k], device_id=(x, y, 1 - c), device_id_type=MESH)
            for t in range(len(srcs)) for k in range(N_CHIP)]

    return copies


def _rs_pair_send(name, gs):
    n = len(gs)

    def body(*refs):
        copies = _pair_send_copies(refs[:n], refs[n:2 * n], *refs[2 * n:])
        for cp in copies():
            cp.start()
        for cp in copies():
            cp.wait()

    return pl.pallas_call(
        body, name=name, in_specs=[_ANY] * n, out_specs=[_ANY] * n,
        out_shape=[jax.ShapeDtypeStruct((N_CHIP,) + g.shape[2:], g.dtype) for g in gs],
        scratch_shapes=[pltpu.SemaphoreType.DMA((n, N_CHIP)), pltpu.SemaphoreType.DMA((n, N_CHIP))],
        compiler_params=_params())(*gs)


def _rs_pair_add(name, g, buf, cvec):
    _, _, r, cc = g.shape
    tr = _row_tile(r, cc, 2 * 1024 * 1024)

    def kern(c_ref, g_ref, b_ref, o_ref):
        o_ref[...] = (g_ref[...] + b_ref[...]).astype(BF16)

    grid_spec = pltpu.PrefetchScalarGridSpec(
        num_scalar_prefetch=1, grid=(N_CHIP, r // tr),
        in_specs=[pl.BlockSpec((None, None, tr, cc), lambda k, i, c_ref: (k, c_ref[0], i, 0)),
                  pl.BlockSpec((None, tr, cc), lambda k, i, c_ref: (k, i, 0))],
        out_specs=pl.BlockSpec((None, tr, cc), lambda k, i, c_ref: (k, i, 0)))
    return pl.pallas_call(kern, name=name, grid_spec=grid_spec,
                          out_shape=jax.ShapeDtypeStruct((N_CHIP, r, cc), BF16),
                          compiler_params=_params())(cvec, g, buf)


def _exchange_phases(srcs, bufs, send_sems, recv_sems, local_sems):
    n = len(srcs)
    x, y, c = _mesh_pos()
    kme = 2 * x + y
    chips = [(1 - x, y), (x, 1 - y), (1 - x, 1 - y)]

    def copies():
        local = [pltpu.make_async_copy(srcs[t].at[kme], bufs[t].at[kme], local_sems.at[t]) for t in range(n)]
        remote = []
        for t in range(n):
            for j, (px, py) in enumerate(chips):
                remote.append(pltpu.make_async_remote_copy(
                    src_ref=srcs[t].at[2 * px + py], dst_ref=bufs[t].at[kme], send_sem=send_sems.at[t, j],
                    recv_sem=recv_sems.at[t, j], device_id=(px, py, c), device_id_type=MESH))
        return local, remote

    def start():
        local, remote = copies()
        for cp in local + remote:
            cp.start()

    def finish():
        local, remote = copies()
        for cp in remote + local:
            cp.wait()

    return start, finish


def _exchange_scratch(n):
    return [pltpu.SemaphoreType.DMA((n, 3)), pltpu.SemaphoreType.DMA((n, 3)), pltpu.SemaphoreType.DMA((n,))]


def _rs_chip_exchange(name, ss):
    n = len(ss)

    def body(*refs):
        start, finish = _exchange_phases(refs[:n], refs[n:2 * n], *refs[2 * n:])
        start()
        finish()

    return pl.pallas_call(
        body, name=name, in_specs=[_ANY] * n, out_specs=[_ANY] * n,
        out_shape=[jax.ShapeDtypeStruct(s.shape, s.dtype) for s in ss],
        scratch_shapes=_exchange_scratch(n), compiler_params=_params())(*ss)


def _rs_pair_share(name, dsts):
    nd = len(dsts)

    def body(*refs):
        ins = refs[:nd]
        outs = refs[nd:2 * nd]
        send_sems, recv_sems = refs[2 * nd:]
        x, y, c = _mesh_pos()
        copies = []
        for d in range(nd):
            for l in range(dsts[d].shape[0]):
                copies.append(pltpu.make_async_remote_copy(
                    src_ref=ins[d].at[l, c], dst_ref=outs[d].at[l, c], send_sem=send_sems.at[d, l],
                    recv_sem=recv_sems.at[d, l], device_id=(x, y, 1 - c), device_id_type=MESH))
        for cp in copies:
            cp.start()
        for cp in copies:
            cp.wait()

    return pl.pallas_call(
        body, name=name, in_specs=[_ANY] * nd, out_specs=[_ANY] * nd,
        out_shape=[jax.ShapeDtypeStruct(a.shape, a.dtype) for a in dsts],
        input_output_aliases={d: d for d in range(nd)},
        scratch_shapes=[pltpu.SemaphoreType.DMA((nd, 2)), pltpu.SemaphoreType.DMA((nd, 2))],
        compiler_params=_params())(*dsts)


def _rope_tables(seq, n_ctx):
    t = jnp.arange(seq)
    row = (t // GRID_W).astype(F32)
    col = (t % GRID_W).astype(F32)
    inv = ROPE_THETA ** (-jnp.arange(0, HEAD_DIM // 2, 2, dtype=F32) / (HEAD_DIM // 2))
    ang_r = row[:, None] * inv[None, :]
    ang_c = col[:, None] * inv[None, :]
    cos = jnp.concatenate([jnp.cos(ang_r)] * 2 + [jnp.cos(ang_c)] * 2, axis=1)
    sin = jnp.concatenate([-jnp.sin(ang_r), jnp.sin(ang_r), -jnp.sin(ang_c), jnp.sin(ang_c)], axis=1)
    cos = jnp.concatenate([jnp.ones((n_ctx, HEAD_DIM), F32), cos], axis=0)
    sin = jnp.concatenate([jnp.zeros((n_ctx, HEAD_DIM), F32), sin], axis=0)
    return jnp.tile(cos, (1, N_KV_HEADS)), jnp.tile(sin, (1, N_KV_HEADS))


def _to_heads(a, nh):
    return a.reshape(a.shape[0], nh, HEAD_DIM).transpose(1, 0, 2)


def _sample_step(x, ctx, tgt, mod, cmod, W, pending, cvec):
    W = dict(W)
    seq = x.shape[0]
    n_ctx = ctx.shape[0]
    tm = min(256, seq)
    tmc = min(256, n_ctx)
    tmm = 512 if seq % 512 == 0 else tm
    tmw = 1024 if seq % 1024 == 0 else tmm
    tmr = (n_ctx + seq) // 4 if (n_ctx + seq) % 64 == 0 else tm
    tq = min(256, seq)
    assert n_ctx % tm == 0 and seq % tm == 0

    def mv(l, j):
        return mod[l, j * D_MODEL:(j + 1) * D_MODEL].reshape(1, D_MODEL)

    csh1 = cmod[:D_MODEL].reshape(1, D_MODEL)
    csc1 = cmod[D_MODEL:2 * D_MODEL].reshape(1, D_MODEL)
    g_mix = [W['g_mix'][l:l + 1] for l in range(2)]
    g_ffn = [W['g_ffn'][l:l + 1] for l in range(2)]
    cos, sin = _rope_tables(seq, n_ctx)
    gq = jnp.tile(W['q_gain'], (1, N_Q_HEADS))
    gk = jnp.tile(W['k_gain'], (1, N_KV_HEADS))
    bdq = jnp.kron(jnp.eye(N_Q_HEADS, dtype=F32), jnp.ones((HEAD_DIM, HEAD_DIM), F32)).astype(BF16)
    bdk = bdq[:KV_W, :KV_W]
    w_sp = W['w_sp'].astype(BF16)
    w_sp_t = w_sp.transpose(0, 2, 1)
    bsp = jnp.broadcast_to(W['b_sp'].T[:, :, None], (CHUNK, N_SG, CHUNK)).reshape(CHUNK, SG_W)

    def ffn_fwd(l, h_mid, xf, gather=(), **tail):
        gu, a, *gathered = _ffn_in_swiglu(f'l{l}_ffn_in', xf, W['ffn_in'][l], tmw, gather=gather)
        if gathered:
            W['pw2'] = gathered[0].reshape(D_MODEL, D_MODEL)
            W['ffn_out'][1] = gathered[1].reshape(D_FF, D_MODEL)
        f, *outs = _mm_rows_res(f'l{l}_ffn_out', a, W['ffn_out'][l], h_mid, mv(l, 5), tmm, **tail)
        return outs, (h_mid, xf, gu, a, f)

    def blocked(a):
        cols = a.shape[-1]
        return a.reshape(N_CHIP, 2, a.size // (2 * N_CHIP * cols), cols)

    def ffn_bwd(l, dh_out, saved, pair_send=()):
        h_mid, xf, gu, a, f = saved
        dy, d_gt2, _ = _resgate_bwd(f'l{l}_ffn_res_bwd', dh_out, f, mv(l, 5), tm)
        dgu = _ffn_out_dx_swiglu(f'l{l}_ffn_out_dx', dy, W['ffn_out'][l], gu, tmw)
        d_wo = _mm_wgrad(f'l{l}_ffn_out_dw', a, dy, tmw)
        d_wi = _mm_halves_wgrad_blk(f'l{l}_ffn_in_dw', xf, dgu, tmw, pair_send=pair_send)
        if pair_send:
            d_wi, *sent = d_wi
        else:
            sent = []
        dh_mid, d_sh2, d_sc2, d_g = _ffn_in_dx_norm_bwd(f'l{l}_ffn_in_dx', dgu, W['ffn_in'][l], h_mid, dh_out,
                                                        g_ffn[l], mv(l, 4), tm)
        return dh_mid, d_wi, d_wo, d_g, (d_sh2, d_sc2, d_gt2), sent

    xm0 = _modnorm_fwd('l0_mix_norm', x, g_mix[0], mv(0, 1), mv(0, 0), tm)
    xc0 = _modnorm_fwd('l0_ctx_norm', ctx, g_mix[0], csc1, csh1, tmc)
    xall = jnp.concatenate([xc0, xm0], axis=0)
    p_qkv = _mm_rows('l0_qkv', xall, W['w_qkv'], tmr)
    p_sg = _mm_rows('l0_sg_in', xm0, W['w_sgp'], tmw)
    q_all, k_all, v_all = _qk_prep_fwd('l0_qk_prep', p_qkv, cos, sin, gq, gk, bdq, bdk, tm)
    kh = _to_heads(k_all, N_KV_HEADS)
    vh = _to_heads(v_all, N_KV_HEADS)
    v_ones = jnp.concatenate([vh, jnp.ones(vh.shape[:2] + (1,), BF16),
                              jnp.zeros(vh.shape[:2] + (HEAD_DIM - 1,), BF16)], axis=2)
    o, lse, *gw = _attn_fwd('l0_attn', q_all, kh, v_ones, n_ctx, tq, gather=pending[:5])
    W['w_out'] = gw[0].reshape(D_MODEL, D_MODEL)
    W['ffn_in'] = [gw[1].reshape(N_CHIP, D_MODEL, FF_BLK), gw[4].reshape(N_CHIP, D_MODEL, FF_BLK)]
    W['ffn_out'] = [gw[2].reshape(D_FF, D_MODEL), None]
    W['pw1'] = gw[3].reshape(N_CHIP, D_MODEL, PW1_BLK).transpose(1, 0, 2).reshape(D_MODEL, 2 * D_MODEL)
    sg = _sg_fwd('l0_sg', p_sg, w_sp, bsp, tm)
    cat = jnp.concatenate([o.astype(BF16), sg], axis=1)
    y0, h1, xf0 = _mm_rows_res('l0_out', cat, W['w_out'], x, mv(0, 2), tmw, norm=(g_ffn[0], mv(0, 4), mv(0, 3)))
    (h2, xm1), ffn0 = ffn_fwd(0, h1, xf0, gather=pending[5:], norm=(g_mix[1], mv(1, 1), mv(1, 0)))

    ag = _mm_rows('l1_pw1', xm1, W['pw1'], tmm)
    hg = _glu_fwd('l1_glu', ag, W['b_pw1'], tm)
    hc = _dwconv('l1_conv', hg, W['w_dw'], W['b_dw'])
    s1 = _lnsilu_fwd('l1_ln_silu', hc, W['ln_g'], W['ln_b'], tm)
    y1, h3, xf1 = _mm_rows_res('l1_pw2', s1, W['pw2'], h2, mv(1, 2), tmw, bias=W['b_pw2'],
                               norm=(g_ffn[1], mv(1, 4), mv(1, 3)))
    (dh4, d_g_final, loss_vec), ffn1 = ffn_fwd(1, h3, xf1, final=(tgt, W['g_final']))

    dh3, d_wi1, d_wo1, d_gffn1, dmod1_ffn, _ = ffn_bwd(1, dh4, ffn1)
    dy1, d_gt1_1, d_b_pw2 = _resgate_bwd('l1_mix_res_bwd', dh3, y1, mv(1, 2), tm)
    d_pw2 = _mm_wgrad('l1_pw2_dw', s1, dy1, tmw)
    dhc, d_ln_g, d_ln_b = _mm_t_epilogue('l1_pw2_dx', dy1, W['pw2'], [(hc, None)], [W['ln_g'], W['ln_b']], 2,
                                         _lnsilu_bwd_epilogue, tmm)
    dhg = _dwconv('l1_conv_dx', dhc, W['w_dw'][::-1], jnp.zeros((1, D_MODEL), F32))
    d_w_dw, d_b_dw = _dwconv_wgrad('l1_conv_dw', hg, dhc)
    dag, d_b_pw1 = _glu_bwd('l1_glu_bwd', ag, dhg, W['b_pw1'], tm)
    d_pw1 = _mm_wgrad('l1_pw1_dw', xm1, dag, tmw).reshape(D_MODEL, N_CHIP, PW1_BLK).transpose(1, 0, 2)
    dh2, d_sh1_1, d_sc1_1, d_gmix1 = _mm_t_epilogue('l1_pw1_dx', dag, W['pw1'], [(h2, None), (dh3, None)],
                                                    [g_mix[1], mv(1, 1)], 3, _norm_bwd_epilogue, tmm)

    gs1 = [blocked(g) for g in (d_wi1, d_wo1, d_pw1, d_pw2)]
    dh1, d_wi0, d_wo0, d_gffn0, dmod0_ffn, sib1 = ffn_bwd(0, dh2, ffn0, pair_send=gs1)
    dy0, d_gt1_0, _ = _resgate_bwd('l0_mix_res_bwd', dh1, y0, mv(0, 2), tm)
    dcat = _mm_rows('l0_out_dx', dy0, W['w_out'], tmw, trans_b=True)
    d_w_out = _mm_wgrad('l0_out_dw', cat, dy0, tmw)

    gs0 = [blocked(g) for g in (d_wi0, d_wo0, d_w_out)]
    sib0 = _rs_pair_send('rs_pair_send', gs0)
    gs = [gs0[0], gs1[0], gs0[1], gs1[1], gs0[2], gs1[2], gs1[3]]
    sib = [sib0[0], sib1[0], sib0[1], sib1[1], sib0[2], sib1[2], sib1[3]]
    ss = [_rs_pair_add(f'rs_pair_add{t}', gs[t], sib[t], cvec) for t in range(len(gs))]
    dq, dk, dv, *xs = _attn_bwd('l0_attn_bwd', q_all, kh, vh, o, lse, dcat, n_ctx, tq, exchange=ss)
    dk, dv = [t.transpose(2, 0, 1).reshape(n_ctx + seq, KV_W) for t in (dk, dv)]
    dp_qkv, d_gq, d_gk = _qk_prep_bwd('l0_qk_prep_bwd', p_qkv, cos, sin, dq, n_ctx // tm, dk, dv, gq, gk, bdq, bdk,
                                      tm)
    dp_sg, d_w_sp, d_bsp = _sg_bwd('l0_sg_bwd', p_sg, dcat, w_sp, w_sp_t, bsp, tm)
    d_w_qkv = _mm_wgrad('l0_qkv_dw', xall, dp_qkv, tmr)
    d_w_sgp = _mm_wgrad('l0_sg_in_dw', xm0, dp_sg, tmw)
    dxall = _mm_rows('l0_qkv_dx', dp_qkv, W['w_qkv'], tmr, trans_b=True)
    g_in = blocked(jnp.concatenate([d_w_qkv, d_w_sgp], axis=1).reshape(D_MODEL, N_CHIP, -1).transpose(1, 0, 2))
    sib_in = _rs_pair_send('rs_pair_send_w_in', [g_in])
    ss_in = _rs_pair_add('rs_pair_add_w_in', g_in, sib_in[0], cvec)
    dx, d_sh1_0, d_sc1_0, d_gmix0, *xs_in = _mm_t_epilogue(
        'l0_sg_in_dx', dp_sg, W['w_sgp'], [(x, None), (dh1, None), (dxall, lambda i: i + n_ctx // tm)],
        [g_mix[0], mv(0, 1)], 3, _norm_bwd_epilogue, tm, exchange=[ss_in])
    _, d_csh1, d_csc1, d_gmix0c = _modnorm_bwd('l0_ctx_norm_bwd', ctx, [_ri(dxall)], None, g_mix[0], csc1, tmc)

    t_ffn_in = _sum_into('rs_sum_ffn_in1', xs[1], cvec, 2, 1, prev=_sum_into('rs_sum_ffn_in0', xs[0], cvec, 2, 0))
    t_ffn_out = _sum_into('rs_sum_ffn_out1', xs[3], cvec, 2, 1, prev=_sum_into('rs_sum_ffn_out0', xs[2], cvec, 2, 0))
    reduced = _rs_pair_share('rs_pair_share', [
        t_ffn_in, t_ffn_out, _sum_into('rs_sum_w_in', xs_in[0], cvec, 1, 0), _sum_into('rs_sum_w_out', xs[4], cvec, 1, 0),
        _sum_into('rs_sum_pw1', xs[5], cvec, 1, 0), _sum_into('rs_sum_pw2', xs[6], cvec, 1, 0)])
    big = dict(zip(['w_ffn_in', 'w_ffn_out', 'w_in', 'w_out', 'w_pw1', 'w_pw2'], reduced))

    zero = jnp.zeros((1, D_MODEL), F32)
    dmod = jnp.concatenate([d_sh1_0, d_sc1_0, d_gt1_0, *dmod0_ffn, d_sh1_1, d_sc1_1, d_gt1_1, *dmod1_ffn], axis=0)
    dcmod = jnp.concatenate([d_csh1, d_csc1, zero, zero, zero, zero], axis=0)
    grads = dict(
        g_mix=jnp.concatenate([d_gmix0 + d_gmix0c, d_gmix1], axis=0),
        g_ffn=jnp.concatenate([d_gffn0, d_gffn1], axis=0), g_final=d_g_final, gq=d_gq, gk=d_gk,
        w_sp=d_w_sp, bsp=d_bsp, b_pw1=d_b_pw1, w_dw=d_w_dw, b_dw=d_b_dw, ln_g=d_ln_g, ln_b=d_ln_b, b_pw2=d_b_pw2,
        dmod=dmod, dcmod=dcmod)
    return loss_vec, dx, grads, big


def kernel(x, c, ctx, c_ctx, w_mod, b_mod, g_mix, g_ffn, w_ffn_in, w_ffn_out, w_in, q_gain, k_gain, w_sp, b_sp, w_out, w_pw1, b_pw1, w_dw, b_dw, ln_g, ln_b, w_pw2, b_pw2, g_final, loss_target, m_c_ctx, m_w_mod, m_b_mod, m_g_mix, m_g_ffn, m_w_ffn_in, m_w_ffn_out, m_w_in, m_q_gain, m_k_gain, m_w_sp, m_b_sp, m_w_out, m_w_pw1, m_b_pw1, m_w_dw, m_b_dw, m_ln_g, m_ln_b, m_w_pw2, m_b_pw2, m_g_final, v_c_ctx, v_w_mod, v_b_mod, v_g_mix, v_g_ffn, v_w_ffn_in, v_w_ffn_out, v_w_in, v_q_gain, v_k_gain, v_w_sp, v_b_sp, v_w_out, v_w_pw1, v_b_pw1, v_w_dw, v_b_dw, v_ln_g, v_ln_b, v_w_pw2, v_b_pw2, v_g_final):
    given = dict(locals())
    ix, iy, ic = _mesh_pos()
    chip = 2 * ix + iy
    me = 2 * chip + ic
    d = D_MODEL
    q4 = d // N_CHIP

    small = jnp.concatenate([c.reshape(4, q4), b_pw1.reshape(2, q4), w_dw[0], b_dw, ln_g, ln_b, b_pw2,
                             jnp.zeros((7, q4), F32)], axis=0)
    def my_half(a):
        r = a.shape[0] // 2
        return lax.dynamic_slice_in_dim(a, ic * r, r, axis=0).astype(BF16)

    sgath, g_w_in = _allgather('ag_first', [small, my_half(w_in[0])])
    c_all = sgath[:, 0:4].reshape(N_DEV, d)
    per_chip = sgath[0::2]
    b_pw1_f = per_chip[:, 4:6].reshape(1, 2 * d)
    w_dw_f = per_chip[:, 6:6 + CONV_W].transpose(1, 0, 2).reshape(CONV_W, d)
    b_dw_f, ln_g_f, ln_b_f, b_pw2_f = [per_chip[:, 37 + i].reshape(1, d) for i in range(4)]

    c16 = jnp.concatenate([c_all, c_ctx[None, :], jnp.zeros((7, d), F32)], axis=0)
    bm_sh = lax.dynamic_slice_in_dim(b_mod, chip * MOD_BLK, MOD_BLK, axis=1).reshape(2, 1, MOD_BLK)
    mod_piece = _mod_fwd('mod_fwd', c16, w_mod, bm_sh)
    mgath = _allgather('ag_mod', [mod_piece.reshape(32, MOD_BLK)])[0]
    mod_all = mgath[0::2].reshape(N_CHIP, 2, 16, MOD_BLK).transpose(1, 2, 0, 3).reshape(2, 16, MOD_W)
    mod_me = lax.dynamic_index_in_dim(mod_all, me, axis=1, keepdims=False)
    cmod = mod_all[0, N_DEV]

    w_in_f = g_w_in.reshape(N_CHIP, d, -1).transpose(1, 0, 2).reshape(d, -1)
    W = dict(
        w_qkv=w_in_f[:, :QKV_W], w_sgp=w_in_f[:, QKV_W:],
        g_mix=g_mix, g_ffn=g_ffn, g_final=g_final.reshape(1, d), q_gain=q_gain, k_gain=k_gain, w_sp=w_sp[0],
        b_sp=b_sp[0], b_pw1=b_pw1_f, w_dw=w_dw_f, b_dw=b_dw_f, ln_g=ln_g_f, ln_b=ln_b_f, b_pw2=b_pw2_f)
    pending = [my_half(w_out[0]), my_half(w_ffn_in[0]), my_half(w_ffn_out[0]), my_half(w_pw1[0]),
               my_half(w_ffn_in[1]), my_half(w_pw2[0]), my_half(w_ffn_out[1])]
    cvec = jnp.reshape(ic, (1,)).astype(jnp.int32)

    loss_vec, dx, G, big = _sample_step(x[0], ctx[0], loss_target[0], mod_me, cmod, W, pending, cvec)
    loss = lax.psum(jnp.sum(loss_vec), ("x", "y", "c"))

    pack = jnp.concatenate([
        G['g_mix'], G['g_ffn'], G['g_final'],
        jnp.concatenate([G['gq'], G['gk'], jnp.zeros((1, d - ATTN_W - KV_W), F32)], axis=1),
        G['b_pw1'].reshape(2, d), G['w_dw'], G['b_dw'], G['ln_g'], G['ln_b'], G['b_pw2'],
        G['dmod'], G['dcmod'], jnp.zeros((3, d), F32),
        G['w_sp'].reshape(64, d), G['bsp'].reshape(64, d)], axis=0)
    assert pack.shape == (192, d)
    grads = dict(big)
    deltas, new_m, new_v = {}, {}, {}

    def adamw(n, gather=()):
        grads[n] = grads[n].reshape(given[n].shape)
        deltas[n], new_m[n], new_v[n], *rest = _adamw(f'adamw_{n}', given[n], grads[n], given['m_' + n],
                                                     given['v_' + n], gather)
        return rest

    pg = adamw('w_ffn_in', gather=[pack])[0]
    s8 = _sum_slots('sum_small_grads', pg, 64)

    def chip_cols(a, width):
        return lax.dynamic_slice_in_dim(a, chip * width, width, axis=a.ndim - 1)

    grads['g_mix'] = s8[0:2]
    grads['g_ffn'] = s8[2:4]
    grads['g_final'] = s8[4]
    grads['q_gain'] = s8[5, :ATTN_W].reshape(N_Q_HEADS, HEAD_DIM).sum(axis=0)[None, :]
    grads['k_gain'] = s8[5, ATTN_W:ATTN_W + KV_W].reshape(N_KV_HEADS, HEAD_DIM).sum(axis=0)[None, :]
    grads['b_pw1'] = chip_cols(s8[6:8].reshape(1, 2 * d), 2 * q4)
    grads['w_dw'] = chip_cols(s8[8:8 + CONV_W], q4)[None]
    grads['b_dw'] = chip_cols(s8[39:40], q4)
    grads['ln_g'] = chip_cols(s8[40:41], q4)
    grads['ln_b'] = chip_cols(s8[41:42], q4)
    grads['b_pw2'] = chip_cols(s8[42:43], q4)
    grads['w_sp'] = s8[64:128].reshape(1, N_SG, CHUNK, CHUNK)
    grads['b_sp'] = s8[128:192].reshape(CHUNK, N_SG, CHUNK).sum(axis=-1).T[None]

    dm_rows = pg[:, 43:55].reshape(N_DEV, 2, MOD_W)
    ctx_row = s8[55:61].reshape(1, MOD_W)
    dm = jnp.stack([jnp.concatenate([dm_rows[:, 0], ctx_row, jnp.zeros((7, MOD_W), F32)], axis=0),
                    jnp.concatenate([dm_rows[:, 1], jnp.zeros((8, MOD_W), F32)], axis=0)], axis=0)
    grads['b_mod'] = jnp.stack([s8[43:49].reshape(MOD_W) + ctx_row[0], s8[49:55].reshape(MOD_W)], axis=0)
    dm_sh = chip_cols(dm, MOD_BLK)
    grads['w_mod'] = _mod_wgrad('mod_dw', c16, dm_sh)
    d_silu = _mm_rows('mod_dx', dm_sh[0], w_mod[0], 16, trans_b=True)
    parts = _allgather('ag_cctx', [d_silu[8:16]])[0]
    grads['c_ctx'] = _cctx_grad('cctx_grad', parts, c_ctx.reshape(1, d))[0]

    for n in WEIGHT_NAMES:
        if n != 'w_ffn_in':
            adamw(n)

    return (loss, dx[None], *[grads[n] for n in WEIGHT_NAMES], *[deltas[n] for n in WEIGHT_NAMES],
            *[new_m[n] for n in WEIGHT_NAMES], *[new_v[n] for n in WEIGHT_NAMES])
```

```python
import functools
import math

import jax
import jax.numpy as jnp
from jax import lax
from jax.experimental import pallas as pl
from jax.experimental.pallas import tpu as pltpu

F32 = jnp.float32
BF16 = jnp.bfloat16
MESH = pl.DeviceIdType.MESH

D_MODEL = 1024
HEAD_DIM = 64
N_Q_HEADS = 8
N_KV_HEADS = 2
Q_GROUP = N_Q_HEADS // N_KV_HEADS
ATTN_W = N_Q_HEADS * HEAD_DIM
KV_W = N_KV_HEADS * HEAD_DIM
QKV_W = ATTN_W + 2 * KV_W
SG_W = D_MODEL - ATTN_W
N_SG = 4
CHUNK = 128
GRID_W = 64
ROPE_THETA = 10000.0
CONV_W = 31
CONV_HALO = 16
D_FF = 2816
MOD_W = 6 * D_MODEL
N_CHIP = 4
N_DEV = 8
FF_BLK = 2 * D_FF // N_CHIP
PW1_BLK = 2 * D_MODEL // N_CHIP
MOD_BLK = MOD_W // N_CHIP
EPS = 1e-6
GELU_C = math.sqrt(2.0 / math.pi)
GELU_A = 0.044715

ADAM_LR = 0.001
ADAM_B1 = 0.9
ADAM_B2 = 0.999
ADAM_EPS = 1e-08
ADAM_WD = 0.01
ADAM_STEP = 10

VMEM_LIMIT_BYTES = 56 * 1024 * 1024

WEIGHT_NAMES = ['c_ctx', 'w_mod', 'b_mod', 'g_mix', 'g_ffn', 'w_ffn_in', 'w_ffn_out', 'w_in', 'q_gain', 'k_gain',
                'w_sp', 'b_sp', 'w_out', 'w_pw1', 'b_pw1', 'w_dw', 'b_dw', 'ln_g', 'ln_b', 'w_pw2', 'b_pw2',
                'g_final']


def _params():
    return pltpu.CompilerParams(vmem_limit_bytes=VMEM_LIMIT_BYTES)


def _ri(arr, width=None, col_block=0, row_off=0, row_fn=None):
    return (arr, arr.shape[1] if width is None else width, col_block,
            (lambda i: i + row_off) if row_fn is None else row_fn)


def _rowwise(name, body, n_tiles, tm, row_ins, vec_ins, row_outs, vec_outs):
    nri, nvi, nro = len(row_ins), len(vec_ins), len(row_outs)

    def kern(*refs):
        ri = refs[:nri]
        vi = refs[nri:nri + nvi]
        ro = refs[nri + nvi:nri + nvi + nro]
        vo = refs[nri + nvi + nro:]
        if vo:
            @pl.when(pl.program_id(0) == 0)
            def _():
                for r in vo:
                    r[...] = jnp.zeros(r.shape, r.dtype)
        body(ri, vi, ro, vo)

    in_specs = [pl.BlockSpec((tm, bw), lambda i, cb=cb, rf=rf: (rf(i), cb)) for (_, bw, cb, rf) in row_ins]
    in_specs += [pl.BlockSpec(a.shape, lambda i, nd=a.ndim: (0,) * nd) for a in vec_ins]
    out_specs = [pl.BlockSpec((tm, w), lambda i: (i, 0)) for (w, _) in row_outs]
    out_specs += [pl.BlockSpec(s, lambda i, nd=len(s): (0,) * nd) for s in vec_outs]
    out_shape = [jax.ShapeDtypeStruct((n_tiles * tm, w), dt) for (w, dt) in row_outs]
    out_shape += [jax.ShapeDtypeStruct(s, F32) for s in vec_outs]
    return pl.pallas_call(kern, name=name, grid=(n_tiles,), in_specs=in_specs, out_specs=out_specs,
                          out_shape=out_shape, compiler_params=_params())(*[a for (a, _, _, _) in row_ins], *vec_ins)


def _colsum(v):
    return jnp.sum(v, axis=0, keepdims=True)


def _rowmean(v):
    return jnp.mean(v, axis=-1, keepdims=True)


def _rms(h):
    r = lax.rsqrt(_rowmean(h * h) + EPS)
    return h * r, r


def _rms_bwd(n, r, gy):
    return r * (gy - n * _rowmean(gy * n))


def _ln_stats(v):
    xc = v - _rowmean(v)
    r = lax.rsqrt(_rowmean(xc * xc) + EPS)
    return xc * r, r


def _ln_bwd(xh, r, dxh):
    return r * (dxh - _rowmean(dxh) - xh * _rowmean(dxh * xh))


def _sigmoid(v):
    return 1.0 / (1.0 + jnp.exp(-v))


def _gelu(v):
    t = jnp.tanh(GELU_C * (v + GELU_A * (v * v * v)))
    return 0.5 * v * (1.0 + t), t


def _gelu_grad(v, t):
    return 0.5 * (1.0 + t) + 0.5 * v * (1.0 - t * t) * (GELU_C * (1.0 + 3.0 * GELU_A * (v * v)))


def _modnorm_fwd(name, h, g, sc, sh, tm):
    def body(ri, vi, ro, vo):
        n, _ = _rms(ri[0][...])
        ro[0][...] = ((n * vi[0][...]) * (1.0 + vi[1][...]) + vi[2][...]).astype(BF16)

    return _rowwise(name, body, h.shape[0] // tm, tm, [_ri(h)], [g, sc, sh], [(D_MODEL, BF16)], [])[0]


def _modnorm_bwd_tile(hv, dxm, gv, scv):
    n, r = _rms(hv)
    dy = dxm * (1.0 + scv)
    return _rms_bwd(n, r, dy * gv), _colsum(dxm), _colsum(dxm * (n * gv)), _colsum(dy * n)


def _mm_row_epilogue(name, mm_args, mm_specs, dx_fn, rows_in, vecs_in, n_vec_out, epilogue, tm, gate=None,
                     exchange=()):
    rows_in = list(rows_in) + ([(gate[0], None)] if gate else [])
    vecs_in = list(vecs_in) + ([gate[1]] if gate else [])
    ng = 1 if gate else 0
    n_vec_out += 2 * ng
    nmm, nr, nv, ne = len(mm_args), len(rows_in), len(vecs_in), len(exchange)
    m = rows_in[0][0].shape[0]
    last = m // tm - 1

    def kern(*refs):
        ins = refs[:nmm + nr + nv]
        o_ref = refs[nmm + nr + nv + ne]
        vo = refs[nmm + nr + nv + ne + 1 + ng:nmm + nr + nv + ne + 1 + ng + n_vec_out]
        if ne:
            srcs = refs[nmm + nr + nv:nmm + nr + nv + ne]
            rest = refs[nmm + nr + nv + ne + 1 + ng + n_vec_out:]
            start, finish = _exchange_phases(srcs, rest[:ne], *rest[ne:])
            pl.when(pl.program_id(0) == 0)(start)

        @pl.when(pl.program_id(0) == 0)
        def _():
            for r in vo:
                r[...] = jnp.zeros(r.shape, F32)

        row_out, incs = epilogue(dx_fn(*ins[:nmm]), [r[...] for r in ins[nmm:nmm + nr - ng]],
                                 [r[...] for r in ins[nmm + nr:nmm + nr + nv - ng]])
        o_ref[...] = row_out
        if ng:
            dy = row_out * ins[nmm + nr + nv - 1][...]
            refs[nmm + nr + nv + ne + 1][...] = dy.astype(BF16)
            incs = list(incs) + [_colsum(row_out * ins[nmm + nr - 1][...]), _colsum(dy)]
        for r, inc in zip(vo, incs):
            r[...] += inc
        if ne:
            pl.when(pl.program_id(0) == last)(finish)

    row = pl.BlockSpec((tm, D_MODEL), lambda i: (i, 0))
    vec = pl.BlockSpec((1, D_MODEL), lambda i: (0, 0))
    row_specs = [row if fn is None else pl.BlockSpec((tm, D_MODEL), lambda i, fn=fn: (fn(i), 0)) for _, fn in rows_in]
    return pl.pallas_call(
        kern, name=name, grid=(m // tm,), in_specs=list(mm_specs) + row_specs + [vec] * nv + [_ANY] * ne,
        out_specs=[row] * (1 + ng) + [vec] * n_vec_out + [_ANY] * ne,
        out_shape=[jax.ShapeDtypeStruct((m, D_MODEL), F32)] + [jax.ShapeDtypeStruct((m, D_MODEL), BF16)] * ng
        + [jax.ShapeDtypeStruct((1, D_MODEL), F32)] * n_vec_out
        + [jax.ShapeDtypeStruct(e.shape, e.dtype) for e in exchange],
        scratch_shapes=_exchange_scratch(ne) if ne else [],
        compiler_params=_params())(*mm_args, *[r for r, _ in rows_in], *vecs_in, *exchange)


def _norm_bwd_epilogue(dxm, rows, vecs):
    if len(rows) > 2:
        dxm = dxm + rows[2]
    dx, dsh, dsc, dg = _modnorm_bwd_tile(rows[0], dxm, vecs[0], vecs[1])
    return rows[1] + dx, [dsh, dsc, dg]


def _lnsilu_bwd_epilogue(ds, rows, vecs):
    xh, r = _ln_stats(rows[0])
    ln = xh * vecs[0] + vecs[1]
    sg = _sigmoid(ln)
    dln = ds * (sg * (1.0 + ln * (1.0 - sg)))
    return _ln_bwd(xh, r, dln * vecs[0]), [_colsum(dln * xh), _colsum(dln)]


def _modnorm_bwd(name, h, dxs, dh_in, g, sc, tm):
    ndx = len(dxs)

    def body(ri, vi, ro, vo):
        dxm = ri[1][...].astype(F32)
        for r in ri[2:1 + ndx]:
            dxm = dxm + r[...].astype(F32)
        dx, dsh, dsc, dg = _modnorm_bwd_tile(ri[0][...], dxm, vi[0][...], vi[1][...])
        vo[0][...] += dsh
        vo[1][...] += dsc
        vo[2][...] += dg
        if dh_in is not None:
            ro[0][...] = ri[1 + ndx][...] + dx

    row_ins = [_ri(h)] + list(dxs) + ([_ri(dh_in)] if dh_in is not None else [])
    row_outs = [(D_MODEL, F32)] if dh_in is not None else []
    outs = _rowwise(name, body, h.shape[0] // tm, tm, row_ins, [g, sc], row_outs, [(1, D_MODEL)] * 3)
    if dh_in is None:
        return (None, *outs)
    return tuple(outs)


def _head_mean(v, bd):
    hi = v.astype(BF16)
    lo = (v - hi.astype(F32)).astype(BF16)
    s = jnp.dot(hi, bd, preferred_element_type=F32) + jnp.dot(lo, bd, preferred_element_type=F32)
    return s * (1.0 / HEAD_DIM)


def _swap16(v):
    w = v.shape[1]
    lane = lax.broadcasted_iota(jnp.int32, v.shape, 1)
    return jnp.where((lane & 16) == 0, pltpu.roll(v, w - 16, 1), pltpu.roll(v, 16, 1))


def _rope(v, cos, sin):
    return v * cos + _swap16(v) * sin


def _rope_bwd(d, cos, sin):
    return d * cos + _swap16(d * sin)


def _q_wide(t):
    return jnp.concatenate([t] * (ATTN_W // KV_W), axis=1)


def _qk_prep_fwd(name, p_qkv, cos, sin, gq, gk, bdq, bdk, tm):
    def body(ri, vi, ro, vo):
        pq = ri[0][...]
        pkv = ri[1][...]
        cosv = ri[2][...]
        sinv = ri[3][...]
        rq = lax.rsqrt(_head_mean(pq * pq, vi[2][...]) + EPS)
        ro[0][...] = (_rope(pq * rq * vi[0][...], _q_wide(cosv), _q_wide(sinv)) * (HEAD_DIM ** -0.5)).astype(BF16)
        pk = pkv[:, :KV_W]
        rk = lax.rsqrt(_head_mean(pk * pk, vi[3][...]) + EPS)
        ro[1][...] = _rope(pk * rk * vi[1][...], cosv, sinv).astype(BF16)
        ro[2][...] = pkv[:, KV_W:].astype(BF16)

    return _rowwise(name, body, p_qkv.shape[0] // tm, tm,
                    [_ri(p_qkv, ATTN_W, 0), _ri(p_qkv, 2 * KV_W, 2), _ri(cos), _ri(sin)], [gq, gk, bdq, bdk],
                    [(ATTN_W, BF16), (KV_W, BF16), (KV_W, BF16)], [])


def _qk_prep_bwd(name, p_qkv, cos, sin, dq, n_ctx_tiles, dk, dv, gq, gk, bdq, bdk, tm):
    def norm_rope_bwd(p, gain, bd, cosv, sinv, dout):
        r = lax.rsqrt(_head_mean(p * p, bd) + EPS)
        n = p * r
        dqn = _rope_bwd(dout, cosv, sinv)
        gy = dqn * gain
        return r * (gy - n * _head_mean(gy * n, bd)), _colsum(dqn * n)

    def body(ri, vi, ro, vo):
        pkv = ri[1][...]
        cosv = ri[2][...]
        sinv = ri[3][...]
        dqv = jnp.where(pl.program_id(0) >= n_ctx_tiles, ri[4][...] * (HEAD_DIM ** -0.5), 0.0)
        dxq, dgq = norm_rope_bwd(ri[0][...], vi[0][...], vi[2][...], _q_wide(cosv), _q_wide(sinv), dqv)
        dxk, dgk = norm_rope_bwd(pkv[:, :KV_W], vi[1][...], vi[3][...], cosv, sinv, ri[5][...])
        ro[0][:, :ATTN_W] = dxq.astype(BF16)
        ro[0][:, ATTN_W:ATTN_W + KV_W] = dxk.astype(BF16)
        ro[0][:, ATTN_W + KV_W:] = ri[6][...].astype(BF16)
        vo[0][...] += dgq
        vo[1][...] += dgk

    return _rowwise(name, body, p_qkv.shape[0] // tm, tm,
                    [_ri(p_qkv, ATTN_W, 0), _ri(p_qkv, 2 * KV_W, 2), _ri(cos), _ri(sin),
                     _ri(dq, row_fn=lambda i: jnp.maximum(i - n_ctx_tiles, 0)), _ri(dk), _ri(dv)],
                    [gq, gk, bdq, bdk], [(QKV_W, BF16)], [(1, ATTN_W), (1, KV_W)])


def _sg_fwd(name, p_sg, w_sp, bsp, tm):
    def body(ri, vi, ro, vo):
        p = ri[0][...]
        gu, _ = _gelu(p[:, :SG_W])
        gv, _ = _gelu(p[:, SG_W:])
        for g in range(N_SG):
            sl = slice(g * CHUNK, (g + 1) * CHUNK)
            vn, _ = _ln_stats(gv[:, sl])
            vnb = vn.astype(BF16)
            for m in range(tm // CHUNK):
                rs = slice(m * CHUNK, (m + 1) * CHUNK)
                mixed = jnp.dot(vi[0][g], vnb[rs], preferred_element_type=F32) + vi[1][:, sl]
                ro[0][rs, sl] = (gu[rs, sl] * mixed).astype(BF16)

    return _rowwise(name, body, p_sg.shape[0] // tm, tm, [_ri(p_sg)], [w_sp, bsp], [(SG_W, BF16)], [])[0]


def _sg_bwd(name, p_sg, dcat, w_sp, w_sp_t, bsp, tm):
    def body(ri, vi, ro, vo):
        p = ri[0][...]
        dsg = ri[1][...]
        su = p[:, :SG_W]
        sv = p[:, SG_W:]
        gu, tu = _gelu(su)
        gv, tv = _gelu(sv)
        for g in range(N_SG):
            sl = slice(g * CHUNK, (g + 1) * CHUNK)
            vn, r = _ln_stats(gv[:, sl])
            vnb = vn.astype(BF16)
            for m in range(tm // CHUNK):
                rs = slice(m * CHUNK, (m + 1) * CHUNK)
                mixed = jnp.dot(vi[0][g], vnb[rs], preferred_element_type=F32) + vi[2][:, sl]
                d_o = dsg[rs, sl]
                dm = d_o * gu[rs, sl]
                vo[1][:, sl] += dm
                dmb = dm.astype(BF16)
                vo[0][g] += lax.dot_general(dmb, vnb[rs], (((1,), (1,)), ((), ())), preferred_element_type=F32)
                dvn = jnp.dot(vi[1][g], dmb, preferred_element_type=F32)
                dgv = _ln_bwd(vn[rs], r[rs], dvn)
                ro[0][rs, sl] = (d_o * mixed * _gelu_grad(su[rs, sl], tu[rs, sl])).astype(BF16)
                ro[0][rs, SG_W + g * CHUNK:SG_W + (g + 1) * CHUNK] = (
                    dgv * _gelu_grad(sv[rs, sl], tv[rs, sl])).astype(BF16)

    return _rowwise(name, body, p_sg.shape[0] // tm, tm, [_ri(p_sg), _ri(dcat, SG_W, 1)], [w_sp, w_sp_t, bsp],
                    [(2 * SG_W, BF16)], [(N_SG, CHUNK, CHUNK), (CHUNK, SG_W)])


def _glu_bwd(name, ag, dhg, b, tm):
    def body(ri, vi, ro, vo):
        v = ri[0][...] + vi[0][...]
        d = ri[1][...]
        a = v[:, :D_MODEL]
        sg = _sigmoid(v[:, D_MODEL:])
        da = d * sg
        dgate = d * a * sg * (1.0 - sg)
        ro[0][:, :D_MODEL] = da.astype(BF16)
        ro[0][:, D_MODEL:] = dgate.astype(BF16)
        vo[0][:, :D_MODEL] += _colsum(da)
        vo[0][:, D_MODEL:] += _colsum(dgate)

    return _rowwise(name, body, ag.shape[0] // tm, tm, [_ri(ag), _ri(dhg)], [b], [(2 * D_MODEL, BF16)],
                    [(1, 2 * D_MODEL)])


def _lnsilu_fwd(name, hc, g, b, tm):
    def body(ri, vi, ro, vo):
        xh, _ = _ln_stats(ri[0][...])
        ln = xh * vi[0][...] + vi[1][...]
        ro[0][...] = (ln * _sigmoid(ln)).astype(BF16)

    return _rowwise(name, body, hc.shape[0] // tm, tm, [_ri(hc)], [g, b], [(D_MODEL, BF16)], [])[0]


def _conv_blocks(seq):
    cb = 128
    tt = 128 if seq % 128 == 0 else seq
    return cb, tt


def _conv_taps(win, tt):
    n = win.shape[0]
    for s in range(8):
        ws = win if s == 0 else pltpu.roll(win, n - s, 0)
        for q in range(4):
            j = 8 * q + s - 1
            if 0 <= j < CONV_W:
                yield j, ws[8 * q:8 * q + tt, :]


def _fill_padded(pad_ref, x_ref, seq):
    zeros = jnp.zeros((CONV_HALO, pad_ref.shape[1]), F32)
    pad_ref[0:CONV_HALO, :] = zeros
    pad_ref[seq + CONV_HALO:seq + 2 * CONV_HALO, :] = zeros
    pad_ref[CONV_HALO:seq + CONV_HALO, :] = x_ref[...]


def _dwconv(name, xin, w, b):
    seq = xin.shape[0]
    cb, tt = _conv_blocks(seq)

    def kern(x_ref, w_ref, b_ref, o_ref, pad_ref):
        _fill_padded(pad_ref, x_ref, seq)
        wv = w_ref[...]
        bv = b_ref[...]

        def step(t, carry):
            base = pl.multiple_of(t * tt, tt)
            win = pad_ref[pl.ds(base, tt + 2 * CONV_HALO), :]
            acc = jnp.zeros((tt, cb), F32) + bv
            for j, rows in _conv_taps(win, tt):
                acc = acc + wv[j:j + 1, :] * rows
            o_ref[pl.ds(base, tt), :] = acc
            return carry

        lax.fori_loop(0, seq // tt, step, 0)

    return pl.pallas_call(
        kern, name=name, grid=(D_MODEL // cb,),
        in_specs=[pl.BlockSpec((seq, cb), lambda j: (0, j)), pl.BlockSpec((CONV_W, cb), lambda j: (0, j)),
                  pl.BlockSpec((1, cb), lambda j: (0, j))],
        out_specs=pl.BlockSpec((seq, cb), lambda j: (0, j)),
        out_shape=jax.ShapeDtypeStruct((seq, D_MODEL), F32),
        scratch_shapes=[pltpu.VMEM((seq + 2 * CONV_HALO, cb), F32)],
        compiler_params=_params())(xin, w, b)


def _dwconv_wgrad(name, xin, dout):
    seq = xin.shape[0]
    cb, tt = _conv_blocks(seq)

    def kern(x_ref, d_ref, dw_ref, db_ref, pad_ref):
        _fill_padded(pad_ref, x_ref, seq)
        dw_ref[...] = jnp.zeros(dw_ref.shape, F32)
        db_ref[...] = jnp.zeros(db_ref.shape, F32)

        def step(t, carry):
            base = pl.multiple_of(t * tt, tt)
            win = pad_ref[pl.ds(base, tt + 2 * CONV_HALO), :]
            d = d_ref[pl.ds(base, tt), :]
            db_ref[...] += _colsum(d)
            for j, rows in _conv_taps(win, tt):
                dw_ref[j:j + 1, :] += _colsum(d * rows)
            return carry

        lax.fori_loop(0, seq // tt, step, 0)

    return pl.pallas_call(
        kern, name=name, grid=(D_MODEL // cb,),
        in_specs=[pl.BlockSpec((seq, cb), lambda j: (0, j)), pl.BlockSpec((seq, cb), lambda j: (0, j))],
        out_specs=[pl.BlockSpec((CONV_W, cb), lambda j: (0, j)), pl.BlockSpec((1, cb), lambda j: (0, j))],
        out_shape=[jax.ShapeDtypeStruct((CONV_W, D_MODEL), F32), jax.ShapeDtypeStruct((1, D_MODEL), F32)],
        scratch_shapes=[pltpu.VMEM((seq + 2 * CONV_HALO, cb), F32)],
        compiler_params=_params())(xin, dout)


def _mm(name, a, b, out_sds, grid, a_spec, b_spec, o_spec, contract, k_axis=None, bias=None, pair_send=()):
    dn = (contract, ((), ()))
    ns = len(pair_send)
    nb = 0 if bias is None else 1
    n_steps = math.prod(grid)

    def kern(*refs):
        a_ref, b_ref = refs[:2]
        bias_ref = refs[2] if nb else None
        o_ref = refs[2 + nb + ns]
        if ns:
            copies = _pair_send_copies(refs[2 + nb:2 + nb + ns], refs[3 + nb + ns:3 + nb + 2 * ns],
                                       *refs[3 + nb + 2 * ns:])
            step = pl.program_id(0)
            for ax in range(1, len(grid)):
                step = step * grid[ax] + pl.program_id(ax)

            @pl.when(step == 0)
            def _():
                for cp in copies():
                    cp.start()

        p = lax.dot_general(a_ref[...].astype(BF16), b_ref[...].astype(BF16), dn, preferred_element_type=F32)
        if bias is not None:
            p = p + bias_ref[...]
        if k_axis is None:
            o_ref[...] = p.astype(o_ref.dtype)
        else:
            k = pl.program_id(k_axis)

            @pl.when(k == 0)
            def _():
                o_ref[...] = p

            @pl.when(k > 0)
            def _():
                o_ref[...] += p

        if ns:
            @pl.when(step == n_steps - 1)
            def _():
                for cp in copies():
                    cp.wait()

    in_specs = [a_spec, b_spec]
    args = [a, b]
    if bias is not None:
        in_specs.append(pl.BlockSpec(bias.shape, lambda *_: (0,) * bias.ndim))
        args.append(bias)
    if not ns:
        return pl.pallas_call(kern, name=name, grid=grid, in_specs=in_specs, out_specs=o_spec, out_shape=out_sds,
                              compiler_params=_params())(*args)
    return pl.pallas_call(
        kern, name=name, grid=grid, in_specs=in_specs + [_ANY] * ns, out_specs=[o_spec] + [_ANY] * ns,
        out_shape=[out_sds] + [jax.ShapeDtypeStruct((N_CHIP,) + g.shape[2:], g.dtype) for g in pair_send],
        scratch_shapes=[pltpu.SemaphoreType.DMA((ns, N_CHIP)), pltpu.SemaphoreType.DMA((ns, N_CHIP))],
        compiler_params=_params())(*args, *pair_send)


def _mm_rows(name, a, b, tm, trans_b=False, bias=None):
    m, k = a.shape
    n = b.shape[0] if trans_b else b.shape[1]
    contract = ((1,), (1,)) if trans_b else ((1,), (0,))
    return _mm(name, a, b, jax.ShapeDtypeStruct((m, n), F32), (m // tm,),
               pl.BlockSpec((tm, k), lambda i: (i, 0)), pl.BlockSpec(b.shape, lambda i: (0, 0)),
               pl.BlockSpec((tm, n), lambda i: (i, 0)), contract, bias=bias)


def _mm_glu(name, a, b, bias, tm):
    m, k = a.shape
    n = b.shape[1]

    def kern(a_ref, b_ref, bias_ref, ag_ref, hg_ref):
        v = jnp.dot(a_ref[...], b_ref[...], preferred_element_type=F32)
        ag_ref[...] = v
        v = v + bias_ref[...]
        hg_ref[...] = v[:, :n // 2] * _sigmoid(v[:, n // 2:])

    return pl.pallas_call(
        kern, name=name, grid=(m // tm,),
        in_specs=[pl.BlockSpec((tm, k), lambda i: (i, 0)), pl.BlockSpec(b.shape, lambda i: (0, 0)),
                  pl.BlockSpec((1, n), lambda i: (0, 0))],
        out_specs=[pl.BlockSpec((tm, n), lambda i: (i, 0)), pl.BlockSpec((tm, n // 2), lambda i: (i, 0))],
        out_shape=[jax.ShapeDtypeStruct((m, n), F32), jax.ShapeDtypeStruct((m, n // 2), F32)],
        compiler_params=_params())(a, b, bias)


def _mm_rows_res(name, a, b, h, gt, tm, bias=None, norm=None, final=None):
    m, k = a.shape
    n = b.shape[1]
    nb = 0 if bias is None else 1
    extra = list(norm or ()) + list(final or ())

    def kern(a_ref, b_ref, h_ref, gt_ref, *rest):
        y = jnp.dot(a_ref[...], b_ref[...], preferred_element_type=F32)
        if nb:
            y = y + rest[0][...]
        ex = rest[nb:nb + len(extra)]
        outs = rest[nb + len(extra):]
        hn = h_ref[...] + gt_ref[...] * y
        if final is None:
            outs[0][...] = y
            outs[1][...] = hn
        if norm is not None:
            nn, _ = _rms(hn)
            outs[2][...] = ((nn * ex[0][...]) * (1.0 + ex[1][...]) + ex[2][...]).astype(BF16)
        if final is not None:
            @pl.when(pl.program_id(0) == 0)
            def _():
                for o in (outs[1], outs[2], outs[4]):
                    o[...] = jnp.zeros(o.shape, F32)

            gv = ex[1][...]
            nn, r = _rms(hn)
            e = nn * gv - ex[0][...]
            outs[2][...] += _colsum(e * e) * (0.5 / D_MODEL)
            dout = e * (1.0 / D_MODEL)
            outs[1][...] += _colsum(dout * nn)
            dh = _rms_bwd(nn, r, dout * gv)
            outs[0][...] = dh
            outs[3][...] = (dh * gt_ref[...]).astype(BF16)
            outs[4][...] += _colsum(dh * y)

    row = pl.BlockSpec((tm, n), lambda i: (i, 0))
    vec = pl.BlockSpec((1, n), lambda i: (0, 0))
    sds = jax.ShapeDtypeStruct((m, n), F32)
    vsd = jax.ShapeDtypeStruct((1, n), F32)
    in_specs = [pl.BlockSpec((tm, k), lambda i: (i, 0)), pl.BlockSpec(b.shape, lambda i: (0, 0)), row, vec] + [vec] * nb
    out_specs, out_shape = [row, row], [sds, sds]
    if norm is not None:
        in_specs += [vec] * 3
        out_specs.append(row)
        out_shape.append(jax.ShapeDtypeStruct((m, n), BF16))
    if final is not None:
        in_specs += [row, vec]
        out_specs = [row, vec, vec, row, vec]
        out_shape = [sds, vsd, vsd, jax.ShapeDtypeStruct((m, n), BF16), vsd]
    return pl.pallas_call(kern, name=name, grid=(m // tm,), in_specs=in_specs, out_specs=out_specs,
                          out_shape=out_shape, compiler_params=_params())(
        a, b, h, gt, *([bias] if nb else []), *extra)


def _ffn_in_swiglu(name, xf, wblk, tm, gather=()):
    m, k = xf.shape
    nblk = wblk.shape[2]
    ng = len(gather)
    n_i = m // tm
    last = 2 * n_i - 1

    def kern(x_ref, wg_ref, wu_ref, *rest):
        gu_ref, a_ref = rest[ng:ng + 2]
        if ng:
            start, forward, finish = _gather_phases(rest[:ng], rest[ng + 2:2 * ng + 2], *rest[2 * ng + 2:])
            step = pl.program_id(0) * n_i + pl.program_id(1)
            pl.when(step == 0)(start)
            pl.when(step == (3 * last) // 4)(forward)
        xv = x_ref[...]
        g = jnp.dot(xv, wg_ref[...], preferred_element_type=F32)
        u = jnp.dot(xv, wu_ref[...], preferred_element_type=F32)
        gu_ref[0] = g.astype(BF16)
        gu_ref[1] = u.astype(BF16)
        a_ref[...] = (g * _sigmoid(g) * u).astype(BF16)
        if ng:
            pl.when(step == last)(finish)

    return pl.pallas_call(
        kern, name=name, grid=(2, n_i),
        in_specs=[pl.BlockSpec((tm, k), lambda j, i: (i, 0)), pl.BlockSpec((None, k, nblk), lambda j, i: (j, 0, 0)),
                  pl.BlockSpec((None, k, nblk), lambda j, i: (j + 2, 0, 0))] + [_ANY] * ng,
        out_specs=[pl.BlockSpec((2, tm, nblk), lambda j, i: (0, i, j)),
                   pl.BlockSpec((tm, nblk), lambda j, i: (i, j))] + [_ANY] * ng,
        out_shape=[jax.ShapeDtypeStruct((2, m, 2 * nblk), BF16), jax.ShapeDtypeStruct((m, 2 * nblk), BF16)]
        + [jax.ShapeDtypeStruct((N_DEV,) + g.shape, g.dtype) for g in gather],
        scratch_shapes=_gather_scratch(ng) if ng else [],
        compiler_params=_params())(xf, wblk, wblk, *gather)


def _ffn_out_dx_swiglu(name, dy, wo, gu, tm):
    m, k = dy.shape
    nblk = gu.shape[2] // 2

    def kern(dy_ref, w_ref, gu_ref, o_ref):
        da = lax.dot_general(dy_ref[...], w_ref[...], (((1,), (1,)), ((), ())), preferred_element_type=F32)
        g = gu_ref[0].astype(F32)
        sg = _sigmoid(g)
        o_ref[0] = (da * gu_ref[1].astype(F32) * (sg * (1.0 + g * (1.0 - sg)))).astype(BF16)
        o_ref[1] = (da * (g * sg)).astype(BF16)

    return pl.pallas_call(
        kern, name=name, grid=(2, m // tm),
        in_specs=[pl.BlockSpec((tm, k), lambda j, i: (i, 0)), pl.BlockSpec((nblk, k), lambda j, i: (j, 0)),
                  pl.BlockSpec((2, tm, nblk), lambda j, i: (0, i, j))],
        out_specs=pl.BlockSpec((2, tm, nblk), lambda j, i: (0, i, j)),
        out_shape=jax.ShapeDtypeStruct(gu.shape, BF16), compiler_params=_params())(dy, wo, gu)


def _ffn_in_dx_norm_bwd(name, dgu, wblk, h, dh_in, g, sc, tm, gate=None):
    nb, k, nblk = wblk.shape

    def dx_fn(a_ref, w_ref):
        acc = None
        for j in range(nb):
            p = lax.dot_general(a_ref[j // 2, :, (j % 2) * nblk:(j % 2 + 1) * nblk], w_ref[j],
                                (((1,), (1,)), ((), ())), preferred_element_type=F32)
            acc = p if acc is None else acc + p
        return acc

    return _mm_row_epilogue(name, [dgu, wblk],
                            [pl.BlockSpec((2, tm, 2 * nblk), lambda i: (0, i, 0)),
                             pl.BlockSpec(wblk.shape, lambda i: (0, 0, 0))], dx_fn, [(h, None), (dh_in, None)],
                            [g, sc], 3, _norm_bwd_epilogue, tm, gate=gate)


def _mm_t_epilogue(name, a, b, rows_in, vecs_in, n_vec_out, epilogue, tm, gate=None, exchange=()):
    def dx_fn(a_ref, b_ref):
        return lax.dot_general(a_ref[...], b_ref[...], (((1,), (1,)), ((), ())), preferred_element_type=F32)

    return _mm_row_epilogue(name, [a, b], [pl.BlockSpec((tm, a.shape[1]), lambda i: (i, 0)),
                                          pl.BlockSpec(b.shape, lambda i: (0, 0))], dx_fn, rows_in, vecs_in,
                            n_vec_out, epilogue, tm, gate, exchange)


def _mm_halves_wgrad_blk(name, a, b2, tk, pair_send=()):
    t, k = a.shape
    nblk = b2.shape[2] // 2
    return _mm(name, a, b2, jax.ShapeDtypeStruct((N_CHIP, k, nblk), F32), (N_CHIP, t // tk),
               pl.BlockSpec((tk, k), lambda j, i: (i, 0)),
               pl.BlockSpec((None, tk, nblk), lambda j, i: (j // 2, i, j % 2)),
               pl.BlockSpec((None, k, nblk), lambda j, i: (j, 0, 0)), ((0,), (0,)), k_axis=1, pair_send=pair_send)


def _mm_wgrad(name, a, b, tk):
    t, k = a.shape
    n = b.shape[1]
    tn = n if k * n * 4 <= 4 * 1024 * 1024 else 512
    return _mm(name, a, b, jax.ShapeDtypeStruct((k, n), F32), (n // tn, t // tk),
               pl.BlockSpec((tk, k), lambda j, i: (i, 0)), pl.BlockSpec((tk, tn), lambda j, i: (i, j)),
               pl.BlockSpec((k, tn), lambda j, i: (0, j)), ((0,), (0,)), k_axis=1)


def _attn_fwd(name, q_all, k, v, n_ctx, tq, gather=()):
    r = k.shape[1]
    s = q_all.shape[0] - n_ctx
    gw = Q_GROUP * HEAD_DIM
    ng = len(gather)
    n_i = s // tq
    last = N_KV_HEADS * n_i - 1
    forward_at = (7 * last) // 8

    def kern(q_ref, k_ref, v_ref, *rest):
        o_ref, lse_ref = rest[ng], rest[ng + 1]
        if ng:
            start, forward, finish = _gather_phases(rest[:ng], rest[ng + 2:2 * ng + 2], *rest[2 * ng + 2:])
            step = pl.program_id(0) * n_i + pl.program_id(1)
            pl.when(step == 0)(start)
            pl.when(step == forward_at)(forward)
        kv = k_ref[...]
        vv = v_ref[...]
        for g in range(Q_GROUP):
            sl = slice(g * HEAD_DIM, (g + 1) * HEAD_DIM)
            sc = lax.dot_general(q_ref[:, sl], kv, (((1,), (1,)), ((), ())), preferred_element_type=F32)
            m = jnp.max(sc, axis=-1, keepdims=True)
            p = jnp.exp(sc - m)
            ol = jnp.dot(p.astype(BF16), vv, preferred_element_type=F32)
            l = ol[:, HEAD_DIM:HEAD_DIM + 1]
            o_ref[:, sl] = ol[:, :HEAD_DIM] / l
            lse_ref[g] = m + jnp.log(l)
        if ng:
            pl.when(step == last)(finish)

    return pl.pallas_call(
        kern, name=name, grid=(N_KV_HEADS, n_i),
        in_specs=[pl.BlockSpec((tq, gw), lambda j, i: (i + n_ctx // tq, j)),
                  pl.BlockSpec((None, r, HEAD_DIM), lambda j, i: (j, 0, 0)),
                  pl.BlockSpec((None, r, 2 * HEAD_DIM), lambda j, i: (j, 0, 0))] + [_ANY] * ng,
        out_specs=[pl.BlockSpec((tq, gw), lambda j, i: (i, j)),
                   pl.BlockSpec((Q_GROUP, tq, 1), lambda j, i: (j, i, 0))] + [_ANY] * ng,
        out_shape=[jax.ShapeDtypeStruct((s, ATTN_W), F32), jax.ShapeDtypeStruct((N_Q_HEADS, s, 1), F32)]
        + [jax.ShapeDtypeStruct((N_DEV,) + a.shape, a.dtype) for a in gather],
        scratch_shapes=_gather_scratch(ng) if ng else [],
        compiler_params=_params())(q_all, k, v, *gather)


def _attn_bwd(name, q_all, k, v, o, lse, dcat, n_ctx, tq, exchange=()):
    nkv, r, _ = k.shape
    s = o.shape[0]
    gw = Q_GROUP * HEAD_DIM
    ne = len(exchange)
    n_i = s // tq
    last = nkv * n_i - 1

    def kern(q_ref, k_ref, v_ref, o_ref, lse_ref, do_ref, *rest):
        dq_ref, dk_ref, dv_ref = rest[ne:ne + 3]
        ds_all, p_all = rest[2 * ne + 3:2 * ne + 5]
        if ne:
            start, finish = _exchange_phases(rest[:ne], rest[ne + 3:2 * ne + 3], *rest[2 * ne + 5:])
            step = pl.program_id(0) * n_i + pl.program_id(1)
            pl.when(step == 0)(start)

        @pl.when(pl.program_id(1) == 0)
        def _():
            dk_ref[...] = jnp.zeros(dk_ref.shape, F32)
            dv_ref[...] = jnp.zeros(dv_ref.shape, F32)

        kv = k_ref[...]
        vv = v_ref[...]
        qs = []
        dos = []
        for g in range(Q_GROUP):
            sl = slice(g * HEAD_DIM, (g + 1) * HEAD_DIM)
            rows = slice(g * tq, (g + 1) * tq)
            qv = q_ref[:, sl]
            dov = do_ref[:, sl]
            delta = jnp.sum(dov * o_ref[:, sl], axis=-1, keepdims=True)
            sc = lax.dot_general(qv, kv, (((1,), (1,)), ((), ())), preferred_element_type=F32)
            p = jnp.exp(sc - lse_ref[g])
            dob = dov.astype(BF16)
            dp = lax.dot_general(dob, vv, (((1,), (1,)), ((), ())), preferred_element_type=F32)
            dsb = (p * (dp - delta)).astype(BF16)
            dq_ref[:, sl] = jnp.dot(dsb, kv, preferred_element_type=F32)
            ds_all[rows, :] = dsb
            p_all[rows, :] = p.astype(BF16)
            qs.append(qv)
            dos.append(dob)
        dk_ref[...] += lax.dot_general(jnp.concatenate(qs, axis=0), ds_all[...], (((0,), (0,)), ((), ())),
                                       preferred_element_type=F32)
        dv_ref[...] += lax.dot_general(jnp.concatenate(dos, axis=0), p_all[...], (((0,), (0,)), ((), ())),
                                       preferred_element_type=F32)
        if ne:
            pl.when(step == last)(finish)

    kvmap = lambda j, i: (j, 0, 0)
    return pl.pallas_call(
        kern, name=name, grid=(nkv, n_i),
        in_specs=[pl.BlockSpec((tq, gw), lambda j, i: (i + n_ctx // tq, j)), pl.BlockSpec((None, r, HEAD_DIM), kvmap),
                  pl.BlockSpec((None, r, HEAD_DIM), kvmap), pl.BlockSpec((tq, gw), lambda j, i: (i, j)),
                  pl.BlockSpec((Q_GROUP, tq, 1), lambda j, i: (j, i, 0)),
                  pl.BlockSpec((tq, gw), lambda j, i: (i, j))] + [_ANY] * ne,
        out_specs=[pl.BlockSpec((tq, gw), lambda j, i: (i, j)), pl.BlockSpec((None, HEAD_DIM, r), kvmap),
                   pl.BlockSpec((None, HEAD_DIM, r), kvmap)] + [_ANY] * ne,
        out_shape=[jax.ShapeDtypeStruct((s, ATTN_W), F32), jax.ShapeDtypeStruct((nkv, HEAD_DIM, r), F32),
                   jax.ShapeDtypeStruct((nkv, HEAD_DIM, r), F32)]
        + [jax.ShapeDtypeStruct(a.shape, a.dtype) for a in exchange],
        scratch_shapes=[pltpu.VMEM((Q_GROUP * tq, r), BF16)] * 2 + (_exchange_scratch(ne) if ne else []),
        compiler_params=_params())(q_all, k, v, o, lse, dcat, *exchange)


def _mod_fwd(name, c16, wm, bm):
    n = wm.shape[2]

    def kern(c_ref, w_ref, b_ref, o_ref):
        cv = c_ref[...]
        a = (cv * _sigmoid(cv)).astype(BF16)
        o_ref[...] = jnp.dot(a, w_ref[...].astype(BF16), preferred_element_type=F32) + b_ref[...]

    return pl.pallas_call(
        kern, name=name, grid=(2,),
        in_specs=[pl.BlockSpec(c16.shape, lambda l: (0, 0)), pl.BlockSpec((None, D_MODEL, n), lambda l: (l, 0, 0)),
                  pl.BlockSpec((None, 1, n), lambda l: (l, 0, 0))],
        out_specs=pl.BlockSpec((None, 16, n), lambda l: (l, 0, 0)),
        out_shape=jax.ShapeDtypeStruct((2, 16, n), F32), compiler_params=_params())(c16, wm, bm)


def _mod_wgrad(name, c16, dm):
    n = dm.shape[2]
    tn = 512

    def kern(c_ref, d_ref, o_ref):
        cv = c_ref[...]
        a = cv * _sigmoid(cv)
        o_ref[...] = lax.dot_general(a, d_ref[...], (((0,), (0,)), ((), ())), preferred_element_type=F32,
                                     precision=lax.Precision.HIGHEST)

    return pl.pallas_call(
        kern, name=name, grid=(2, n // tn),
        in_specs=[pl.BlockSpec(c16.shape, lambda l, j: (0, 0)), pl.BlockSpec((None, 16, tn), lambda l, j: (l, 0, j))],
        out_specs=pl.BlockSpec((None, D_MODEL, tn), lambda l, j: (l, 0, j)),
        out_shape=jax.ShapeDtypeStruct((2, D_MODEL, n), F32), compiler_params=_params())(c16, dm)


def _cctx_grad(name, parts, c_ctx):
    def kern(p_ref, c_ref, o_ref):
        d = p_ref[0, 0:1, :] + p_ref[2, 0:1, :] + p_ref[4, 0:1, :] + p_ref[6, 0:1, :]
        cv = c_ref[...]
        sg = _sigmoid(cv)
        o_ref[...] = d * (sg * (1.0 + cv * (1.0 - sg)))

    return pl.pallas_call(kern, name=name, out_shape=jax.ShapeDtypeStruct((1, D_MODEL), F32),
                          compiler_params=_params())(parts, c_ctx)


def _sum_slots(name, g, tr):
    n, rows, cols = g.shape

    def kern(g_ref, o_ref):
        acc = g_ref[0].astype(F32)
        for i in range(1, n):
            acc = acc + g_ref[i].astype(F32)
        o_ref[...] = acc

    return pl.pallas_call(kern, name=name, grid=(rows // tr,),
                          in_specs=[pl.BlockSpec((n, tr, cols), lambda i: (0, i, 0))],
                          out_specs=pl.BlockSpec((tr, cols), lambda i: (i, 0)),
                          out_shape=jax.ShapeDtypeStruct((rows, cols), F32), compiler_params=_params())(g)


def _sum_into(name, g, cvec, n_layers, layer, prev=None):
    n, r, cc = g.shape
    tr = _row_tile(r, cc, 512 * 1024)

    def kern(c_ref, g_ref, *rest):
        acc = g_ref[0].astype(F32)
        for i in range(1, n):
            acc = acc + g_ref[i].astype(F32)
        rest[-1][...] = acc

    in_specs = [pl.BlockSpec((n, tr, cc), lambda i, c_ref: (0, i, 0))]
    args = [cvec, g]
    aliases = {}
    if prev is not None:
        in_specs.append(_ANY)
        args.append(prev)
        aliases = {2: 0}
    grid_spec = pltpu.PrefetchScalarGridSpec(
        num_scalar_prefetch=1, grid=(r // tr,), in_specs=in_specs,
        out_specs=pl.BlockSpec((None, None, tr, cc), lambda i, c_ref: (layer, c_ref[0], i, 0)))
    return pl.pallas_call(kern, name=name, grid_spec=grid_spec,
                          out_shape=jax.ShapeDtypeStruct((n_layers, 2, r, cc), F32),
                          input_output_aliases=aliases, compiler_params=_params())(*args)


def _row_tile(rows, cols, max_bytes=1024 * 1024):
    if rows * cols * 4 <= 2 * max_bytes:
        return rows
    best = None
    for t in range(16, rows + 1, 16):
        if rows % t == 0 and t * cols * 4 <= max_bytes:
            best = t
    assert best is not None, (rows, cols)
    return best


def _adamw(name, w, g, m, v, gather=()):
    shape = w.shape
    cols = shape[-1]
    rows = w.size // cols
    tr = _row_tile(rows, cols)
    ng = len(gather)
    last = rows // tr - 1

    def kern(w_ref, g_ref, m_ref, v_ref, *rest):
        d_ref, nm_ref, nv_ref = rest[ng:ng + 3]
        if ng:
            start, forward, finish = _gather_phases(rest[:ng], rest[ng + 3:2 * ng + 3], *rest[2 * ng + 3:])
            pl.when(pl.program_id(0) == 0)(start)
            pl.when(pl.program_id(0) == last // 2)(forward)
        gv = g_ref[...]
        m2 = ADAM_B1 * m_ref[...] + (1.0 - ADAM_B1) * gv
        v2 = ADAM_B2 * v_ref[...] + (1.0 - ADAM_B2) * (gv * gv)
        m_hat = m2 / (1.0 - ADAM_B1 ** ADAM_STEP)
        v_hat = v2 / (1.0 - ADAM_B2 ** ADAM_STEP)
        d_ref[...] = -ADAM_LR * (m_hat / (jnp.sqrt(v_hat) + ADAM_EPS) + ADAM_WD * w_ref[...])
        nm_ref[...] = m2
        nv_ref[...] = v2
        if ng:
            pl.when(pl.program_id(0) == last)(finish)

    spec = pl.BlockSpec((tr, cols), lambda i: (i, 0))
    sds = jax.ShapeDtypeStruct((rows, cols), F32)
    outs = pl.pallas_call(
        kern, name=name, grid=(rows // tr,), in_specs=[spec] * 4 + [_ANY] * ng, out_specs=[spec] * 3 + [_ANY] * ng,
        out_shape=[sds] * 3 + [jax.ShapeDtypeStruct((N_DEV,) + a.shape, a.dtype) for a in gather],
        scratch_shapes=_gather_scratch(ng) if ng else [], compiler_params=_params())(
        w.reshape(rows, cols), g.reshape(rows, cols), m.reshape(rows, cols), v.reshape(rows, cols), *gather)
    return tuple(o.reshape(shape) for o in outs[:3]) + tuple(outs[3:])


_ANY = pl.BlockSpec(memory_space=pl.ANY)


def _mesh_pos():
    return lax.axis_index("x"), lax.axis_index("y"), lax.axis_index("c")


def _gather_phases(srcs, outs, send_sems, recv_sems, local_sems):
    n = len(srcs)
    x, y, c = _mesh_pos()
    me = (x, y, c)
    sibling = (x, y, 1 - c)
    chips = [(1 - x, y), (x, 1 - y), (1 - x, 1 - y)]

    def slot(px, py, pc):
        return 4 * px + 2 * py + pc

    def copy(t, k, s, to, src=None):
        dst = outs[t].at[s]
        return pltpu.make_async_remote_copy(src_ref=dst if src is None else src, dst_ref=dst,
                                            send_sem=send_sems.at[t, k], recv_sem=recv_sems.at[t, k],
                                            device_id=to, device_id_type=MESH)

    def mine(t):
        return pltpu.make_async_copy(srcs[t], outs[t].at[slot(*me)], local_sems.at[t])

    def first(t):
        return [copy(t, 0, slot(*me), sibling, src=srcs[t])] + [
            copy(t, 1 + j, slot(*me), (px, py, c), src=srcs[t]) for j, (px, py) in enumerate(chips)]

    def passed(t, j):
        px, py = chips[j]
        return copy(t, 4 + j, slot(px, py, c), sibling)

    def start():
        for t in range(n):
            mine(t).start()
        for t in range(n):
            for cp in first(t):
                cp.start()

    def forward():
        for j, (px, py) in enumerate(chips):
            for t in range(n):
                copy(t, 1 + j, slot(px, py, c), me).wait_recv()
                passed(t, j).start()

    def finish():
        for t in range(n):
            copy(t, 0, slot(x, y, 1 - c), me).wait_recv()
            for j, (px, py) in enumerate(chips):
                copy(t, 4 + j, slot(px, py, 1 - c), me).wait_recv()
        for t in range(n):
            for cp in first(t) + [passed(t, j) for j in range(len(chips))]:
                cp.wait_send()
            mine(t).wait()

    return start, forward, finish


def _gather_scratch(n):
    return [pltpu.SemaphoreType.DMA((n, 7)), pltpu.SemaphoreType.DMA((n, 7)), pltpu.SemaphoreType.DMA((n,))]


def _allgather(name, items):
    n = len(items)

    def body(*refs):
        start, forward, finish = _gather_phases(refs[:n], refs[n:2 * n], *refs[2 * n:])
        start()
        forward()
        finish()

    return pl.pallas_call(
        body, name=name, in_specs=[_ANY] * n, out_specs=[_ANY] * n,
        out_shape=[jax.ShapeDtypeStruct((N_DEV,) + a.shape, a.dtype) for a in items],
        scratch_shapes=_gather_scratch(n), compiler_params=_params())(*items)


def _pair_send_copies(srcs, bufs, send_sems, recv_sems):
    x, y, c = _mesh_pos()

    def copies():
        return [pltpu.make_async_remote_copy(
            src_ref=srcs[t].at[k, 1 - c], dst_ref=bufs[t].at[k], send_sem=send_sems.at[t, k],
            recv_sem=recv_sems.at[t, k], device_id=(x, y, 1 - c), device_id_type=MESH)
            for t in range(len(srcs)) for k in range(N_CHIP)]

    return copies


def _rs_pair_send(name, gs):
    n = len(gs)

    def body(*refs):
        copies = _pair_send_copies(refs[:n], refs[n:2 * n], *refs[2 * n:])
        for cp in copies():
            cp.start()
        for cp in copies():
            cp.wait()

    return pl.pallas_call(
        body, name=name, in_specs=[_ANY] * n, out_specs=[_ANY] * n,
        out_shape=[jax.ShapeDtypeStruct((N_CHIP,) + g.shape[2:], g.dtype) for g in gs],
        scratch_shapes=[pltpu.SemaphoreType.DMA((n, N_CHIP)), pltpu.SemaphoreType.DMA((n, N_CHIP))],
        compiler_params=_params())(*gs)


def _rs_pair_add(name, g, buf, cvec):
    _, _, r, cc = g.shape
    tr = _row_tile(r, cc, 2 * 1024 * 1024)

    def kern(c_ref, g_ref, b_ref, o_ref):
        o_ref[...] = (g_ref[...] + b_ref[...]).astype(BF16)

    grid_spec = pltpu.PrefetchScalarGridSpec(
        num_scalar_prefetch=1, grid=(N_CHIP, r // tr),
        in_specs=[pl.BlockSpec((None, None, tr, cc), lambda k, i, c_ref: (k, c_ref[0], i, 0)),
                  pl.BlockSpec((None, tr, cc), lambda k, i, c_ref: (k, i, 0))],
        out_specs=pl.BlockSpec((None, tr, cc), lambda k, i, c_ref: (k, i, 0)))
    return pl.pallas_call(kern, name=name, grid_spec=grid_spec,
                          out_shape=jax.ShapeDtypeStruct((N_CHIP, r, cc), BF16),
                          compiler_params=_params())(cvec, g, buf)


def _exchange_phases(srcs, bufs, send_sems, recv_sems, local_sems):
    n = len(srcs)
    x, y, c = _mesh_pos()
    kme = 2 * x + y
    chips = [(1 - x, y), (x, 1 - y), (1 - x, 1 - y)]

    def copies():
        local = [pltpu.make_async_copy(srcs[t].at[kme], bufs[t].at[kme], local_sems.at[t]) for t in range(n)]
        remote = []
        for t in range(n):
            for j, (px, py) in enumerate(chips):
                remote.append(pltpu.make_async_remote_copy(
                    src_ref=srcs[t].at[2 * px + py], dst_ref=bufs[t].at[kme], send_sem=send_sems.at[t, j],
                    recv_sem=recv_sems.at[t, j], device_id=(px, py, c), device_id_type=MESH))
        return local, remote

    def start():
        local, remote = copies()
        for cp in local + remote:
            cp.start()

    def finish():
        local, remote = copies()
        for cp in remote + local:
            cp.wait()

    return start, finish


def _exchange_scratch(n):
    return [pltpu.SemaphoreType.DMA((n, 3)), pltpu.SemaphoreType.DMA((n, 3)), pltpu.SemaphoreType.DMA((n,))]


def _rs_chip_exchange(name, ss):
    n = len(ss)

    def body(*refs):
        start, finish = _exchange_phases(refs[:n], refs[n:2 * n], *refs[2 * n:])
        start()
        finish()

    return pl.pallas_call(
        body, name=name, in_specs=[_ANY] * n, out_specs=[_ANY] * n,
        out_shape=[jax.ShapeDtypeStruct(s.shape, s.dtype) for s in ss],
        scratch_shapes=_exchange_scratch(n), compiler_params=_params())(*ss)


def _rs_pair_share(name, dsts):
    nd = len(dsts)

    def body(*refs):
        ins = refs[:nd]
        outs = refs[nd:2 * nd]
        send_sems, recv_sems = refs[2 * nd:]
        x, y, c = _mesh_pos()
        copies = []
        for d in range(nd):
            for l in range(dsts[d].shape[0]):
                copies.append(pltpu.make_async_remote_copy(
                    src_ref=ins[d].at[l, c], dst_ref=outs[d].at[l, c], send_sem=send_sems.at[d, l],
                    recv_sem=recv_sems.at[d, l], device_id=(x, y, 1 - c), device_id_type=MESH))
        for cp in copies:
            cp.start()
        for cp in copies:
            cp.wait()

    return pl.pallas_call(
        body, name=name, in_specs=[_ANY] * nd, out_specs=[_ANY] * nd,
        out_shape=[jax.ShapeDtypeStruct(a.shape, a.dtype) for a in dsts],
        input_output_aliases={d: d for d in range(nd)},
        scratch_shapes=[pltpu.SemaphoreType.DMA((nd, 2)), pltpu.SemaphoreType.DMA((nd, 2))],
        compiler_params=_params())(*dsts)


def _rope_tables(seq, n_ctx):
    t = jnp.arange(seq)
    row = (t // GRID_W).astype(F32)
    col = (t % GRID_W).astype(F32)
    inv = ROPE_THETA ** (-jnp.arange(0, HEAD_DIM // 2, 2, dtype=F32) / (HEAD_DIM // 2))
    ang_r = row[:, None] * inv[None, :]
    ang_c = col[:, None] * inv[None, :]
    cos = jnp.concatenate([jnp.cos(ang_r)] * 2 + [jnp.cos(ang_c)] * 2, axis=1)
    sin = jnp.concatenate([-jnp.sin(ang_r), jnp.sin(ang_r), -jnp.sin(ang_c), jnp.sin(ang_c)], axis=1)
    cos = jnp.concatenate([jnp.ones((n_ctx, HEAD_DIM), F32), cos], axis=0)
    sin = jnp.concatenate([jnp.zeros((n_ctx, HEAD_DIM), F32), sin], axis=0)
    return jnp.tile(cos, (1, N_KV_HEADS)), jnp.tile(sin, (1, N_KV_HEADS))


def _to_heads(a, nh):
    return a.reshape(a.shape[0], nh, HEAD_DIM).transpose(1, 0, 2)


def _sample_step(x, ctx, tgt, mod, cmod, W, pending, cvec):
    W = dict(W)
    seq = x.shape[0]
    n_ctx = ctx.shape[0]
    tm = min(256, seq)
    tmc = min(256, n_ctx)
    tmm = 512 if seq % 512 == 0 else tm
    tmw = 1024 if seq % 1024 == 0 else tmm
    tmr = (n_ctx + seq) // 4 if (n_ctx + seq) % 64 == 0 else tm
    tq = min(256, seq)
    assert n_ctx % tm == 0 and seq % tm == 0

    def mv(l, j):
        return mod[l, j * D_MODEL:(j + 1) * D_MODEL].reshape(1, D_MODEL)

    csh1 = cmod[:D_MODEL].reshape(1, D_MODEL)
    csc1 = cmod[D_MODEL:2 * D_MODEL].reshape(1, D_MODEL)
    g_mix = [W['g_mix'][l:l + 1] for l in range(2)]
    g_ffn = [W['g_ffn'][l:l + 1] for l in range(2)]
    cos, sin = _rope_tables(seq, n_ctx)
    gq = jnp.tile(W['q_gain'], (1, N_Q_HEADS))
    gk = jnp.tile(W['k_gain'], (1, N_KV_HEADS))
    bdq = jnp.kron(jnp.eye(N_Q_HEADS, dtype=F32), jnp.ones((HEAD_DIM, HEAD_DIM), F32)).astype(BF16)
    bdk = bdq[:KV_W, :KV_W]
    w_sp = W['w_sp'].astype(BF16)
    w_sp_t = w_sp.transpose(0, 2, 1)
    bsp = jnp.broadcast_to(W['b_sp'].T[:, :, None], (CHUNK, N_SG, CHUNK)).reshape(CHUNK, SG_W)

    def ffn_fwd(l, h_mid, xf, gather=(), **tail):
        gu, a, *gathered = _ffn_in_swiglu(f'l{l}_ffn_in', xf, W['ffn_in'][l], tmw, gather=gather)
        if gathered:
            W['pw2'] = gathered[0].reshape(D_MODEL, D_MODEL)
            W['ffn_out'][1] = gathered[1].reshape(D_FF, D_MODEL)
        outs = _mm_rows_res(f'l{l}_ffn_out', a, W['ffn_out'][l], h_mid, mv(l, 5), tmm, **tail)
        return outs, (h_mid, xf, gu, a)

    def blocked(a):
        cols = a.shape[-1]
        return a.reshape(N_CHIP, 2, a.size // (2 * N_CHIP * cols), cols)

    def ffn_bwd(l, dh_out, dy, y_mix, saved, pair_send=()):
        h_mid, xf, gu, a = saved
        dgu = _ffn_out_dx_swiglu(f'l{l}_ffn_out_dx', dy, W['ffn_out'][l], gu, tmw)
        d_wo = _mm_wgrad(f'l{l}_ffn_out_dw', a, dy, tmw)
        d_wi = _mm_halves_wgrad_blk(f'l{l}_ffn_in_dw', xf, dgu, tmw, pair_send=pair_send)
        if pair_send:
            d_wi, *sent = d_wi
        else:
            sent = []
        dh_mid, dy_mix, d_sh2, d_sc2, d_g, d_gt1, sum_dy = _ffn_in_dx_norm_bwd(
            f'l{l}_ffn_in_dx', dgu, W['ffn_in'][l], h_mid, dh_out, g_ffn[l], mv(l, 4), tm, gate=(y_mix, mv(l, 2)))
        return dh_mid, dy_mix, d_wi, d_wo, d_g, (d_sh2, d_sc2), d_gt1, sum_dy, sent

    xm0 = _modnorm_fwd('l0_mix_norm', x, g_mix[0], mv(0, 1), mv(0, 0), tm)
    xc0 = _modnorm_fwd('l0_ctx_norm', ctx, g_mix[0], csc1, csh1, tmc)
    xall = jnp.concatenate([xc0, xm0], axis=0)
    p_qkv = _mm_rows('l0_qkv', xall, W['w_qkv'], tmr)
    p_sg = _mm_rows('l0_sg_in', xm0, W['w_sgp'], tmw)
    q_all, k_all, v_all = _qk_prep_fwd('l0_qk_prep', p_qkv, cos, sin, gq, gk, bdq, bdk, tm)
    kh = _to_heads(k_all, N_KV_HEADS)
    vh = _to_heads(v_all, N_KV_HEADS)
    v_ones = jnp.concatenate([vh, jnp.ones(vh.shape[:2] + (1,), BF16),
                              jnp.zeros(vh.shape[:2] + (HEAD_DIM - 1,), BF16)], axis=2)
    o, lse, *gw = _attn_fwd('l0_attn', q_all, kh, v_ones, n_ctx, tq, gather=pending[:5])
    W['w_out'] = gw[0].reshape(D_MODEL, D_MODEL)
    W['ffn_in'] = [gw[1].reshape(N_CHIP, D_MODEL, FF_BLK), gw[4].reshape(N_CHIP, D_MODEL, FF_BLK)]
    W['ffn_out'] = [gw[2].reshape(D_FF, D_MODEL), None]
    W['pw1'] = gw[3].reshape(N_CHIP, D_MODEL, PW1_BLK).transpose(1, 0, 2).reshape(D_MODEL, 2 * D_MODEL)
    sg = _sg_fwd('l0_sg', p_sg, w_sp, bsp, tm)
    cat = jnp.concatenate([o.astype(BF16), sg], axis=1)
    y0, h1, xf0 = _mm_rows_res('l0_out', cat, W['w_out'], x, mv(0, 2), tmw, norm=(g_ffn[0], mv(0, 4), mv(0, 3)))
    (f0, h2, xm1), ffn0 = ffn_fwd(0, h1, xf0, gather=pending[5:], norm=(g_mix[1], mv(1, 1), mv(1, 0)))

    ag, hg = _mm_glu('l1_pw1', xm1, W['pw1'], W['b_pw1'], tmm)
    hc = _dwconv('l1_conv', hg, W['w_dw'], W['b_dw'])
    s1 = _lnsilu_fwd('l1_ln_silu', hc, W['ln_g'], W['ln_b'], tm)
    y1, h3, xf1 = _mm_rows_res('l1_pw2', s1, W['pw2'], h2, mv(1, 2), tmw, bias=W['b_pw2'],
                               norm=(g_ffn[1], mv(1, 4), mv(1, 3)))
    (dh4, d_g_final, loss_vec, dyf1, d_gt2_1), ffn1 = ffn_fwd(1, h3, xf1, final=(tgt, W['g_final']))

    dh3, dy1, d_wi1, d_wo1, d_gffn1, dmod1_ffn, d_gt1_1, d_b_pw2, _ = ffn_bwd(1, dh4, dyf1, y1, ffn1)
    d_pw2 = _mm_wgrad('l1_pw2_dw', s1, dy1, tmw)
    dhc, d_ln_g, d_ln_b = _mm_t_epilogue('l1_pw2_dx', dy1, W['pw2'], [(hc, None)], [W['ln_g'], W['ln_b']], 2,
                                         _lnsilu_bwd_epilogue, tmm)
    dhg = _dwconv('l1_conv_dx', dhc, W['w_dw'][::-1], jnp.zeros((1, D_MODEL), F32))
    d_w_dw, d_b_dw = _dwconv_wgrad('l1_conv_dw', hg, dhc)
    dag, d_b_pw1 = _glu_bwd('l1_glu_bwd', ag, dhg, W['b_pw1'], tm)
    d_pw1 = _mm_wgrad('l1_pw1_dw', xm1, dag, tmw).reshape(D_MODEL, N_CHIP, PW1_BLK).transpose(1, 0, 2)
    dh2, dyf0, d_sh1_1, d_sc1_1, d_gmix1, d_gt2_0, _ = _mm_t_epilogue(
        'l1_pw1_dx', dag, W['pw1'], [(h2, None), (dh3, None)], [g_mix[1], mv(1, 1)], 3, _norm_bwd_epilogue, tmm,
        gate=(f0, mv(0, 5)))

    gs1 = [blocked(g) for g in (d_wi1, d_wo1, d_pw1, d_pw2)]
    dh1, dy0, d_wi0, d_wo0, d_gffn0, dmod0_ffn, d_gt1_0, _, sib1 = ffn_bwd(0, dh2, dyf0, y0, ffn0, pair_send=gs1)
    dcat = _mm_rows('l0_out_dx', dy0, W['w_out'], tmw, trans_b=True)
    d_w_out = _mm_wgrad('l0_out_dw', cat, dy0, tmw)

    gs0 = [blocked(g) for g in (d_wi0, d_wo0, d_w_out)]
    sib0 = _rs_pair_send('rs_pair_send', gs0)
    gs = [gs0[0], gs1[0], gs0[1], gs1[1], gs0[2], gs1[2], gs1[3]]
    sib = [sib0[0], sib1[0], sib0[1], sib1[1], sib0[2], sib1[2], sib1[3]]
    ss = [_rs_pair_add(f'rs_pair_add{t}', gs[t], sib[t], cvec) for t in range(len(gs))]
    dq, dk, dv, *xs = _attn_bwd('l0_attn_bwd', q_all, kh, vh, o, lse, dcat, n_ctx, tq, exchange=ss)
    dk, dv = [t.transpose(2, 0, 1).reshape(n_ctx + seq, KV_W) for t in (dk, dv)]
    dp_qkv, d_gq, d_gk = _qk_prep_bwd('l0_qk_prep_bwd', p_qkv, cos, sin, dq, n_ctx // tm, dk, dv, gq, gk, bdq, bdk,
                                      tm)
    dp_sg, d_w_sp, d_bsp = _sg_bwd('l0_sg_bwd', p_sg, dcat, w_sp, w_sp_t, bsp, tm)
    d_w_qkv = _mm_wgrad('l0_qkv_dw', xall, dp_qkv, tmr)
    d_w_sgp = _mm_wgrad('l0_sg_in_dw', xm0, dp_sg, tmw)
    dxall = _mm_rows('l0_qkv_dx', dp_qkv, W['w_qkv'], tmr, trans_b=True)
    g_in = blocked(jnp.concatenate([d_w_qkv, d_w_sgp], axis=1).reshape(D_MODEL, N_CHIP, -1).transpose(1, 0, 2))
    sib_in = _rs_pair_send('rs_pair_send_w_in', [g_in])
    ss_in = _rs_pair_add('rs_pair_add_w_in', g_in, sib_in[0], cvec)
    dx, d_sh1_0, d_sc1_0, d_gmix0, *xs_in = _mm_t_epilogue(
        'l0_sg_in_dx', dp_sg, W['w_sgp'], [(x, None), (dh1, None), (dxall, lambda i: i + n_ctx // tm)],
        [g_mix[0], mv(0, 1)], 3, _norm_bwd_epilogue, tm, exchange=[ss_in])
    _, d_csh1, d_csc1, d_gmix0c = _modnorm_bwd('l0_ctx_norm_bwd', ctx, [_ri(dxall)], None, g_mix[0], csc1, tmc)

    t_ffn_in = _sum_into('rs_sum_ffn_in1', xs[1], cvec, 2, 1, prev=_sum_into('rs_sum_ffn_in0', xs[0], cvec, 2, 0))
    t_ffn_out = _sum_into('rs_sum_ffn_out1', xs[3], cvec, 2, 1, prev=_sum_into('rs_sum_ffn_out0', xs[2], cvec, 2, 0))
    reduced = _rs_pair_share('rs_pair_share', [
        t_ffn_in, t_ffn_out, _sum_into('rs_sum_w_in', xs_in[0], cvec, 1, 0), _sum_into('rs_sum_w_out', xs[4], cvec, 1, 0),
        _sum_into('rs_sum_pw1', xs[5], cvec, 1, 0), _sum_into('rs_sum_pw2', xs[6], cvec, 1, 0)])
    big = dict(zip(['w_ffn_in', 'w_ffn_out', 'w_in', 'w_out', 'w_pw1', 'w_pw2'], reduced))

    zero = jnp.zeros((1, D_MODEL), F32)
    dmod = jnp.concatenate([d_sh1_0, d_sc1_0, d_gt1_0, *dmod0_ffn, d_gt2_0, d_sh1_1, d_sc1_1, d_gt1_1, *dmod1_ffn,
                            d_gt2_1], axis=0)
    dcmod = jnp.concatenate([d_csh1, d_csc1, zero, zero, zero, zero], axis=0)
    grads = dict(
        g_mix=jnp.concatenate([d_gmix0 + d_gmix0c, d_gmix1], axis=0),
        g_ffn=jnp.concatenate([d_gffn0, d_gffn1], axis=0), g_final=d_g_final, gq=d_gq, gk=d_gk,
        w_sp=d_w_sp, bsp=d_bsp, b_pw1=d_b_pw1, w_dw=d_w_dw, b_dw=d_b_dw, ln_g=d_ln_g, ln_b=d_ln_b, b_pw2=d_b_pw2,
        dmod=dmod, dcmod=dcmod)
    return loss_vec, dx, grads, big


def kernel(x, c, ctx, c_ctx, w_mod, b_mod, g_mix, g_ffn, w_ffn_in, w_ffn_out, w_in, q_gain, k_gain, w_sp, b_sp, w_out, w_pw1, b_pw1, w_dw, b_dw, ln_g, ln_b, w_pw2, b_pw2, g_final, loss_target, m_c_ctx, m_w_mod, m_b_mod, m_g_mix, m_g_ffn, m_w_ffn_in, m_w_ffn_out, m_w_in, m_q_gain, m_k_gain, m_w_sp, m_b_sp, m_w_out, m_w_pw1, m_b_pw1, m_w_dw, m_b_dw, m_ln_g, m_ln_b, m_w_pw2, m_b_pw2, m_g_final, v_c_ctx, v_w_mod, v_b_mod, v_g_mix, v_g_ffn, v_w_ffn_in, v_w_ffn_out, v_w_in, v_q_gain, v_k_gain, v_w_sp, v_b_sp, v_w_out, v_w_pw1, v_b_pw1, v_w_dw, v_b_dw, v_ln_g, v_ln_b, v_w_pw2, v_b_pw2, v_g_final):
    given = dict(locals())
    ix, iy, ic = _mesh_pos()
    chip = 2 * ix + iy
    me = 2 * chip + ic
    d = D_MODEL
    q4 = d // N_CHIP

    small = jnp.concatenate([c.reshape(4, q4), b_pw1.reshape(2, q4), w_dw[0], b_dw, ln_g, ln_b, b_pw2,
                             jnp.zeros((7, q4), F32)], axis=0)
    def my_half(a):
        r = a.shape[0] // 2
        return lax.dynamic_slice_in_dim(a, ic * r, r, axis=0).astype(BF16)

    sgath, g_w_in = _allgather('ag_first', [small, my_half(w_in[0])])
    c_all = sgath[:, 0:4].reshape(N_DEV, d)
    per_chip = sgath[0::2]
    b_pw1_f = per_chip[:, 4:6].reshape(1, 2 * d)
    w_dw_f = per_chip[:, 6:6 + CONV_W].transpose(1, 0, 2).reshape(CONV_W, d)
    b_dw_f, ln_g_f, ln_b_f, b_pw2_f = [per_chip[:, 37 + i].reshape(1, d) for i in range(4)]

    c16 = jnp.concatenate([c_all, c_ctx[None, :], jnp.zeros((7, d), F32)], axis=0)
    bm_sh = lax.dynamic_slice_in_dim(b_mod, chip * MOD_BLK, MOD_BLK, axis=1).reshape(2, 1, MOD_BLK)
    mod_piece = _mod_fwd('mod_fwd', c16, w_mod, bm_sh)
    mgath = _allgather('ag_mod', [mod_piece.reshape(32, MOD_BLK)])[0]
    mod_all = mgath[0::2].reshape(N_CHIP, 2, 16, MOD_BLK).transpose(1, 2, 0, 3).reshape(2, 16, MOD_W)
    mod_me = lax.dynamic_index_in_dim(mod_all, me, axis=1, keepdims=False)
    cmod = mod_all[0, N_DEV]

    w_in_f = g_w_in.reshape(N_CHIP, d, -1).transpose(1, 0, 2).reshape(d, -1)
    W = dict(
        w_qkv=w_in_f[:, :QKV_W], w_sgp=w_in_f[:, QKV_W:],
        g_mix=g_mix, g_ffn=g_ffn, g_final=g_final.reshape(1, d), q_gain=q_gain, k_gain=k_gain, w_sp=w_sp[0],
        b_sp=b_sp[0], b_pw1=b_pw1_f, w_dw=w_dw_f, b_dw=b_dw_f, ln_g=ln_g_f, ln_b=ln_b_f, b_pw2=b_pw2_f)
    pending = [my_half(w_out[0]), my_half(w_ffn_in[0]), my_half(w_ffn_out[0]), my_half(w_pw1[0]),
               my_half(w_ffn_in[1]), my_half(w_pw2[0]), my_half(w_ffn_out[1])]
    cvec = jnp.reshape(ic, (1,)).astype(jnp.int32)

    loss_vec, dx, G, big = _sample_step(x[0], ctx[0], loss_target[0], mod_me, cmod, W, pending, cvec)
    loss = lax.psum(jnp.sum(loss_vec), ("x", "y", "c"))

    pack = jnp.concatenate([
        G['g_mix'], G['g_ffn'], G['g_final'],
        jnp.concatenate([G['gq'], G['gk'], jnp.zeros((1, d - ATTN_W - KV_W), F32)], axis=1),
        G['b_pw1'].reshape(2, d), G['w_dw'], G['b_dw'], G['ln_g'], G['ln_b'], G['b_pw2'],
        G['dmod'], G['dcmod'], jnp.zeros((3, d), F32),
        G['w_sp'].reshape(64, d),
        jnp.pad(G['bsp'].reshape(CHUNK, N_SG, CHUNK).sum(axis=-1).reshape(1, CHUNK * N_SG), ((0, 7), (0, d - SG_W)))],
        axis=0)
    assert pack.shape == (136, d)
    grads = dict(big)
    deltas, new_m, new_v = {}, {}, {}

    def adamw(n, gather=()):
        grads[n] = grads[n].reshape(given[n].shape)
        deltas[n], new_m[n], new_v[n], *rest = _adamw(f'adamw_{n}', given[n], grads[n], given['m_' + n],
                                                     given['v_' + n], gather)
        return rest

    pg = adamw('w_ffn_in', gather=[pack])[0]
    s8 = _sum_slots('sum_small_grads', pg, 136)

    def chip_cols(a, width):
        return lax.dynamic_slice_in_dim(a, chip * width, width, axis=a.ndim - 1)

    grads['g_mix'] = s8[0:2]
    grads['g_ffn'] = s8[2:4]
    grads['g_final'] = s8[4]
    grads['q_gain'] = s8[5, :ATTN_W].reshape(N_Q_HEADS, HEAD_DIM).sum(axis=0)[None, :]
    grads['k_gain'] = s8[5, ATTN_W:ATTN_W + KV_W].reshape(N_KV_HEADS, HEAD_DIM).sum(axis=0)[None, :]
    grads['b_pw1'] = chip_cols(s8[6:8].reshape(1, 2 * d), 2 * q4)
    grads['w_dw'] = chip_cols(s8[8:8 + CONV_W], q4)[None]
    grads['b_dw'] = chip_cols(s8[39:40], q4)
    grads['ln_g'] = chip_cols(s8[40:41], q4)
    grads['ln_b'] = chip_cols(s8[41:42], q4)
    grads['b_pw2'] = chip_cols(s8[42:43], q4)
    grads['w_sp'] = s8[64:128].reshape(1, N_SG, CHUNK, CHUNK)
    grads['b_sp'] = s8[128, :SG_W].reshape(CHUNK, N_SG).T[None]

    dm_rows = pg[:, 43:55].reshape(N_DEV, 2, MOD_W)
    ctx_row = s8[55:61].reshape(1, MOD_W)
    dm = jnp.stack([jnp.concatenate([dm_rows[:, 0], ctx_row, jnp.zeros((7, MOD_W), F32)], axis=0),
                    jnp.concatenate([dm_rows[:, 1], jnp.zeros((8, MOD_W), F32)], axis=0)], axis=0)
    grads['b_mod'] = jnp.stack([s8[43:49].reshape(MOD_W) + ctx_row[0], s8[49:55].reshape(MOD_W)], axis=0)
    dm_sh = chip_cols(dm, MOD_BLK)
    grads['w_mod'] = _mod_wgrad('mod_dw', c16, dm_sh)
    d_silu = _mm_rows('mod_dx', dm_sh[0], w_mod[0], 16, trans_b=True)
    parts = _allgather('ag_cctx', [d_silu[8:16]])[0]
    grads['c_ctx'] = _cctx_grad('cctx_grad', parts, c_ctx.reshape(1, d))[0]

    for n in WEIGHT_NAMES:
        if n != 'w_ffn_in':
            adamw(n)

    return (loss, dx[None], *[grads[n] for n in WEIGHT_NAMES], *[deltas[n] for n in WEIGHT_NAMES],
            *[new_m[n] for n in WEIGHT_NAMES], *[new_v[n] for n in WEIGHT_NAMES])
```

```python
import functools
import math

import jax
import jax.numpy as jnp
from jax import lax
from jax.experimental import pallas as pl
from jax.experimental.pallas import tpu as pltpu

F32 = jnp.float32
BF16 = jnp.bfloat16
MESH = pl.DeviceIdType.MESH

D_MODEL = 1024
HEAD_DIM = 64
N_Q_HEADS = 8
N_KV_HEADS = 2
Q_GROUP = N_Q_HEADS // N_KV_HEADS
ATTN_W = N_Q_HEADS * HEAD_DIM
KV_W = N_KV_HEADS * HEAD_DIM
QKV_W = ATTN_W + 2 * KV_W
SG_W = D_MODEL - ATTN_W
N_SG = 4
CHUNK = 128
GRID_W = 64
ROPE_THETA = 10000.0
CONV_W = 31
CONV_HALO = 16
D_FF = 2816
MOD_W = 6 * D_MODEL
N_CHIP = 4
N_DEV = 8
FF_BLK = 2 * D_FF // N_CHIP
PW1_BLK = 2 * D_MODEL // N_CHIP
MOD_BLK = MOD_W // N_CHIP
EPS = 1e-6
GELU_C = math.sqrt(2.0 / math.pi)
GELU_A = 0.044715

ADAM_LR = 0.001
ADAM_B1 = 0.9
ADAM_B2 = 0.999
ADAM_EPS = 1e-08
ADAM_WD = 0.01
ADAM_STEP = 10

SMALL_ROWS = 136

VMEM_LIMIT_BYTES = 56 * 1024 * 1024

WEIGHT_NAMES = ['c_ctx', 'w_mod', 'b_mod', 'g_mix', 'g_ffn', 'w_ffn_in', 'w_ffn_out', 'w_in', 'q_gain', 'k_gain',
                'w_sp', 'b_sp', 'w_out', 'w_pw1', 'b_pw1', 'w_dw', 'b_dw', 'ln_g', 'ln_b', 'w_pw2', 'b_pw2',
                'g_final']


def _params():
    return pltpu.CompilerParams(vmem_limit_bytes=VMEM_LIMIT_BYTES)


def _ri(arr, width=None, col_block=0, row_off=0, row_fn=None):
    return (arr, arr.shape[1] if width is None else width, col_block,
            (lambda i: i + row_off) if row_fn is None else row_fn)


def _rowwise(name, body, n_tiles, tm, row_ins, vec_ins, row_outs, vec_outs):
    nri, nvi, nro = len(row_ins), len(vec_ins), len(row_outs)

    def kern(*refs):
        ri = refs[:nri]
        vi = refs[nri:nri + nvi]
        ro = refs[nri + nvi:nri + nvi + nro]
        vo = refs[nri + nvi + nro:]
        if vo:
            @pl.when(pl.program_id(0) == 0)
            def _():
                for r in vo:
                    r[...] = jnp.zeros(r.shape, r.dtype)
        body(ri, vi, ro, vo)

    in_specs = [pl.BlockSpec((tm, bw), lambda i, cb=cb, rf=rf: (rf(i), cb)) for (_, bw, cb, rf) in row_ins]
    in_specs += [pl.BlockSpec(a.shape, lambda i, nd=a.ndim: (0,) * nd) for a in vec_ins]
    out_specs = [pl.BlockSpec((tm, w), lambda i: (i, 0)) for (w, _) in row_outs]
    out_specs += [pl.BlockSpec(s, lambda i, nd=len(s): (0,) * nd) for s in vec_outs]
    out_shape = [jax.ShapeDtypeStruct((n_tiles * tm, w), dt) for (w, dt) in row_outs]
    out_shape += [jax.ShapeDtypeStruct(s, F32) for s in vec_outs]
    return pl.pallas_call(kern, name=name, grid=(n_tiles,), in_specs=in_specs, out_specs=out_specs,
                          out_shape=out_shape, compiler_params=_params())(*[a for (a, _, _, _) in row_ins], *vec_ins)


def _colsum(v):
    return jnp.sum(v, axis=0, keepdims=True)


def _rowmean(v):
    return jnp.mean(v, axis=-1, keepdims=True)


def _rms(h):
    r = lax.rsqrt(_rowmean(h * h) + EPS)
    return h * r, r


def _rms_bwd(n, r, gy):
    return r * (gy - n * _rowmean(gy * n))


def _ln_stats(v):
    xc = v - _rowmean(v)
    r = lax.rsqrt(_rowmean(xc * xc) + EPS)
    return xc * r, r


def _ln_bwd(xh, r, dxh):
    return r * (dxh - _rowmean(dxh) - xh * _rowmean(dxh * xh))


def _sigmoid(v):
    return 1.0 / (1.0 + jnp.exp(-v))


def _gelu(v):
    t = jnp.tanh(GELU_C * (v + GELU_A * (v * v * v)))
    return 0.5 * v * (1.0 + t), t


def _gelu_grad(v, t):
    return 0.5 * (1.0 + t) + 0.5 * v * (1.0 - t * t) * (GELU_C * (1.0 + 3.0 * GELU_A * (v * v)))


def _modnorm_fwd(name, h, g, sc, sh, tm):
    def body(ri, vi, ro, vo):
        n, _ = _rms(ri[0][...])
        ro[0][...] = ((n * vi[0][...]) * (1.0 + vi[1][...]) + vi[2][...]).astype(BF16)

    return _rowwise(name, body, h.shape[0] // tm, tm, [_ri(h)], [g, sc, sh], [(D_MODEL, BF16)], [])[0]


def _modnorm_bwd_tile(hv, dxm, gv, scv):
    n, r = _rms(hv)
    dy = dxm * (1.0 + scv)
    return _rms_bwd(n, r, dy * gv), _colsum(dxm), _colsum(dxm * (n * gv)), _colsum(dy * n)


def _mm_row_epilogue(name, mm_args, mm_specs, dx_fn, rows_in, vecs_in, n_vec_out, epilogue, tm, gate=None,
                     exchange=()):
    rows_in = list(rows_in) + ([(gate[0], None)] if gate else [])
    vecs_in = list(vecs_in) + ([gate[1]] if gate else [])
    ng = 1 if gate else 0
    n_vec_out += 2 * ng
    nmm, nr, nv, ne = len(mm_args), len(rows_in), len(vecs_in), len(exchange)
    m = rows_in[0][0].shape[0]
    last = m // tm - 1

    def kern(*refs):
        ins = refs[:nmm + nr + nv]
        o_ref = refs[nmm + nr + nv + ne]
        vo = refs[nmm + nr + nv + ne + 1 + ng:nmm + nr + nv + ne + 1 + ng + n_vec_out]
        if ne:
            srcs = refs[nmm + nr + nv:nmm + nr + nv + ne]
            rest = refs[nmm + nr + nv + ne + 1 + ng + n_vec_out:]
            start, finish = _exchange_phases(srcs, rest[:ne], *rest[ne:])
            pl.when(pl.program_id(0) == 0)(start)

        @pl.when(pl.program_id(0) == 0)
        def _():
            for r in vo:
                r[...] = jnp.zeros(r.shape, F32)

        row_out, incs = epilogue(dx_fn(*ins[:nmm]), [r[...] for r in ins[nmm:nmm + nr - ng]],
                                 [r[...] for r in ins[nmm + nr:nmm + nr + nv - ng]])
        o_ref[...] = row_out
        if ng:
            dy = row_out * ins[nmm + nr + nv - 1][...]
            refs[nmm + nr + nv + ne + 1][...] = dy.astype(BF16)
            incs = list(incs) + [_colsum(row_out * ins[nmm + nr - 1][...]), _colsum(dy)]
        for r, inc in zip(vo, incs):
            r[...] += inc
        if ne:
            pl.when(pl.program_id(0) == last)(finish)

    row = pl.BlockSpec((tm, D_MODEL), lambda i: (i, 0))
    vec = pl.BlockSpec((1, D_MODEL), lambda i: (0, 0))
    row_specs = [row if fn is None else pl.BlockSpec((tm, D_MODEL), lambda i, fn=fn: (fn(i), 0)) for _, fn in rows_in]
    return pl.pallas_call(
        kern, name=name, grid=(m // tm,), in_specs=list(mm_specs) + row_specs + [vec] * nv + [_ANY] * ne,
        out_specs=[row] * (1 + ng) + [vec] * n_vec_out + [_ANY] * ne,
        out_shape=[jax.ShapeDtypeStruct((m, D_MODEL), F32)] + [jax.ShapeDtypeStruct((m, D_MODEL), BF16)] * ng
        + [jax.ShapeDtypeStruct((1, D_MODEL), F32)] * n_vec_out
        + [jax.ShapeDtypeStruct(e.shape, e.dtype) for e in exchange],
        scratch_shapes=_exchange_scratch(ne) if ne else [],
        compiler_params=_params())(*mm_args, *[r for r, _ in rows_in], *vecs_in, *exchange)


def _norm_bwd_epilogue(dxm, rows, vecs):
    if len(rows) > 2:
        dxm = dxm + rows[2]
    dx, dsh, dsc, dg = _modnorm_bwd_tile(rows[0], dxm, vecs[0], vecs[1])
    return rows[1] + dx, [dsh, dsc, dg]


def _lnsilu_bwd_epilogue(ds, rows, vecs):
    xh, r = _ln_stats(rows[0])
    ln = xh * vecs[0] + vecs[1]
    sg = _sigmoid(ln)
    dln = ds * (sg * (1.0 + ln * (1.0 - sg)))
    return _ln_bwd(xh, r, dln * vecs[0]), [_colsum(dln * xh), _colsum(dln)]


def _modnorm_bwd(name, h, dxs, dh_in, g, sc, tm):
    ndx = len(dxs)

    def body(ri, vi, ro, vo):
        dxm = ri[1][...].astype(F32)
        for r in ri[2:1 + ndx]:
            dxm = dxm + r[...].astype(F32)
        dx, dsh, dsc, dg = _modnorm_bwd_tile(ri[0][...], dxm, vi[0][...], vi[1][...])
        vo[0][...] += dsh
        vo[1][...] += dsc
        vo[2][...] += dg
        if dh_in is not None:
            ro[0][...] = ri[1 + ndx][...] + dx

    row_ins = [_ri(h)] + list(dxs) + ([_ri(dh_in)] if dh_in is not None else [])
    row_outs = [(D_MODEL, F32)] if dh_in is not None else []
    outs = _rowwise(name, body, h.shape[0] // tm, tm, row_ins, [g, sc], row_outs, [(1, D_MODEL)] * 3)
    if dh_in is None:
        return (None, *outs)
    return tuple(outs)


def _head_mean(v, bd):
    hi = v.astype(BF16)
    lo = (v - hi.astype(F32)).astype(BF16)
    s = jnp.dot(hi, bd, preferred_element_type=F32) + jnp.dot(lo, bd, preferred_element_type=F32)
    return s * (1.0 / HEAD_DIM)


def _swap16(v):
    w = v.shape[1]
    lane = lax.broadcasted_iota(jnp.int32, v.shape, 1)
    return jnp.where((lane & 16) == 0, pltpu.roll(v, w - 16, 1), pltpu.roll(v, 16, 1))


def _rope(v, cos, sin):
    return v * cos + _swap16(v) * sin


def _rope_bwd(d, cos, sin):
    return d * cos + _swap16(d * sin)


def _q_wide(t):
    return jnp.concatenate([t] * (ATTN_W // KV_W), axis=1)


def _mm_qk_prep(name, xall, w_qkv, cos, sin, gq, gk, bdq, bdk, tm):
    r, k = xall.shape

    def kern(x_ref, w_ref, cos_ref, sin_ref, gq_ref, gk_ref, bdq_ref, bdk_ref, p_ref, q_ref, k_ref, v_ref):
        p = jnp.dot(x_ref[...], w_ref[...], preferred_element_type=F32)
        p_ref[...] = p
        cosv = cos_ref[...]
        sinv = sin_ref[...]
        pq = p[:, :ATTN_W]
        rq = lax.rsqrt(_head_mean(pq * pq, bdq_ref[...]) + EPS)
        q_ref[...] = (_rope(pq * rq * gq_ref[...], _q_wide(cosv), _q_wide(sinv)) * (HEAD_DIM ** -0.5)).astype(BF16)
        pk = p[:, ATTN_W:ATTN_W + KV_W]
        rk = lax.rsqrt(_head_mean(pk * pk, bdk_ref[...]) + EPS)
        k_ref[...] = _rope(pk * rk * gk_ref[...], cosv, sinv).astype(BF16)
        v_ref[...] = p[:, ATTN_W + KV_W:].astype(BF16)

    def rows(w):
        return pl.BlockSpec((tm, w), lambda i: (i, 0))

    def whole(a):
        return pl.BlockSpec(a.shape, lambda i: (0, 0))

    return pl.pallas_call(
        kern, name=name, grid=(r // tm,),
        in_specs=[rows(k), whole(w_qkv), rows(KV_W), rows(KV_W), whole(gq), whole(gk), whole(bdq), whole(bdk)],
        out_specs=[rows(QKV_W), rows(ATTN_W), rows(KV_W), rows(KV_W)],
        out_shape=[jax.ShapeDtypeStruct((r, QKV_W), F32), jax.ShapeDtypeStruct((r, ATTN_W), BF16),
                   jax.ShapeDtypeStruct((r, KV_W), BF16), jax.ShapeDtypeStruct((r, KV_W), BF16)],
        compiler_params=_params())(xall, w_qkv, cos, sin, gq, gk, bdq, bdk)


def _qk_prep_bwd(name, p_qkv, cos, sin, dq, n_ctx_tiles, dk, dv, gq, gk, bdq, bdk, tm):
    def norm_rope_bwd(p, gain, bd, cosv, sinv, dout):
        r = lax.rsqrt(_head_mean(p * p, bd) + EPS)
        n = p * r
        dqn = _rope_bwd(dout, cosv, sinv)
        gy = dqn * gain
        return r * (gy - n * _head_mean(gy * n, bd)), _colsum(dqn * n)

    def body(ri, vi, ro, vo):
        pkv = ri[1][...]
        cosv = ri[2][...]
        sinv = ri[3][...]
        dqv = jnp.where(pl.program_id(0) >= n_ctx_tiles, ri[4][...] * (HEAD_DIM ** -0.5), 0.0)
        dxq, dgq = norm_rope_bwd(ri[0][...], vi[0][...], vi[2][...], _q_wide(cosv), _q_wide(sinv), dqv)
        dxk, dgk = norm_rope_bwd(pkv[:, :KV_W], vi[1][...], vi[3][...], cosv, sinv, ri[5][...])
        ro[0][:, :ATTN_W] = dxq.astype(BF16)
        ro[0][:, ATTN_W:ATTN_W + KV_W] = dxk.astype(BF16)
        ro[0][:, ATTN_W + KV_W:] = ri[6][...].astype(BF16)
        vo[0][...] += dgq
        vo[1][...] += dgk

    return _rowwise(name, body, p_qkv.shape[0] // tm, tm,
                    [_ri(p_qkv, ATTN_W, 0), _ri(p_qkv, 2 * KV_W, 2), _ri(cos), _ri(sin),
                     _ri(dq, row_fn=lambda i: jnp.maximum(i - n_ctx_tiles, 0)), _ri(dk), _ri(dv)],
                    [gq, gk, bdq, bdk], [(QKV_W, BF16)], [(1, ATTN_W), (1, KV_W)])


def _mm_sg(name, xm, w_sgp, w_sp, bsp, tm):
    m, k = xm.shape

    def kern(x_ref, w_ref, wsp_ref, bsp_ref, p_ref, o_ref):
        p = jnp.dot(x_ref[...], w_ref[...], preferred_element_type=F32)
        p_ref[...] = p
        gu, _ = _gelu(p[:, :SG_W])
        gv, _ = _gelu(p[:, SG_W:])
        for g in range(N_SG):
            sl = slice(g * CHUNK, (g + 1) * CHUNK)
            vn, _ = _ln_stats(gv[:, sl])
            vnb = vn.astype(BF16)
            for c in range(tm // CHUNK):
                rs = slice(c * CHUNK, (c + 1) * CHUNK)
                mixed = jnp.dot(wsp_ref[g], vnb[rs], preferred_element_type=F32) + bsp_ref[:, sl]
                o_ref[rs, sl] = (gu[rs, sl] * mixed).astype(BF16)

    return pl.pallas_call(
        kern, name=name, grid=(m // tm,),
        in_specs=[pl.BlockSpec((tm, k), lambda i: (i, 0)), pl.BlockSpec(w_sgp.shape, lambda i: (0, 0)),
                  pl.BlockSpec(w_sp.shape, lambda i: (0, 0, 0)), pl.BlockSpec(bsp.shape, lambda i: (0, 0))],
        out_specs=[pl.BlockSpec((tm, 2 * SG_W), lambda i: (i, 0)), pl.BlockSpec((tm, SG_W), lambda i: (i, 0))],
        out_shape=[jax.ShapeDtypeStruct((m, 2 * SG_W), F32), jax.ShapeDtypeStruct((m, SG_W), BF16)],
        compiler_params=_params())(xm, w_sgp, w_sp, bsp)


def _sg_bwd(name, p_sg, dcat, w_sp, w_sp_t, bsp, tm):
    def body(ri, vi, ro, vo):
        p = ri[0][...]
        dsg = ri[1][...]
        su = p[:, :SG_W]
        sv = p[:, SG_W:]
        gu, tu = _gelu(su)
        gv, tv = _gelu(sv)
        for g in range(N_SG):
            sl = slice(g * CHUNK, (g + 1) * CHUNK)
            vn, r = _ln_stats(gv[:, sl])
            vnb = vn.astype(BF16)
            for m in range(tm // CHUNK):
                rs = slice(m * CHUNK, (m + 1) * CHUNK)
                mixed = jnp.dot(vi[0][g], vnb[rs], preferred_element_type=F32) + vi[2][:, sl]
                d_o = dsg[rs, sl]
                dm = d_o * gu[rs, sl]
                vo[1][:, sl] += dm
                dmb = dm.astype(BF16)
                vo[0][g] += lax.dot_general(dmb, vnb[rs], (((1,), (1,)), ((), ())), preferred_element_type=F32)
                dvn = jnp.dot(vi[1][g], dmb, preferred_element_type=F32)
                dgv = _ln_bwd(vn[rs], r[rs], dvn)
                ro[0][rs, sl] = (d_o * mixed * _gelu_grad(su[rs, sl], tu[rs, sl])).astype(BF16)
                ro[0][rs, SG_W + g * CHUNK:SG_W + (g + 1) * CHUNK] = (
                    dgv * _gelu_grad(sv[rs, sl], tv[rs, sl])).astype(BF16)

    return _rowwise(name, body, p_sg.shape[0] // tm, tm, [_ri(p_sg), _ri(dcat, SG_W, 1)], [w_sp, w_sp_t, bsp],
                    [(2 * SG_W, BF16)], [(N_SG, CHUNK, CHUNK), (CHUNK, SG_W)])


def _glu_bwd(name, ag, dhg, b, tm):
    def body(ri, vi, ro, vo):
        v = ri[0][...] + vi[0][...]
        d = ri[1][...]
        a = v[:, :D_MODEL]
        sg = _sigmoid(v[:, D_MODEL:])
        da = d * sg
        dgate = d * a * sg * (1.0 - sg)
        ro[0][:, :D_MODEL] = da.astype(BF16)
        ro[0][:, D_MODEL:] = dgate.astype(BF16)
        vo[0][:, :D_MODEL] += _colsum(da)
        vo[0][:, D_MODEL:] += _colsum(dgate)

    return _rowwise(name, body, ag.shape[0] // tm, tm, [_ri(ag), _ri(dhg)], [b], [(2 * D_MODEL, BF16)],
                    [(1, 2 * D_MODEL)])


def _lnsilu_fwd(name, hc, g, b, tm):
    def body(ri, vi, ro, vo):
        xh, _ = _ln_stats(ri[0][...])
        ln = xh * vi[0][...] + vi[1][...]
        ro[0][...] = (ln * _sigmoid(ln)).astype(BF16)

    return _rowwise(name, body, hc.shape[0] // tm, tm, [_ri(hc)], [g, b], [(D_MODEL, BF16)], [])[0]


def _conv_blocks(seq):
    cb = 128
    tt = 128 if seq % 128 == 0 else seq
    return cb, tt


def _conv_taps(win, tt):
    n = win.shape[0]
    for s in range(8):
        ws = win if s == 0 else pltpu.roll(win, n - s, 0)
        for q in range(4):
            j = 8 * q + s - 1
            if 0 <= j < CONV_W:
                yield j, ws[8 * q:8 * q + tt, :]


def _fill_padded(pad_ref, x_ref, seq):
    zeros = jnp.zeros((CONV_HALO, pad_ref.shape[1]), F32)
    pad_ref[0:CONV_HALO, :] = zeros
    pad_ref[seq + CONV_HALO:seq + 2 * CONV_HALO, :] = zeros
    pad_ref[CONV_HALO:seq + CONV_HALO, :] = x_ref[...]


def _dwconv(name, xin, w, b):
    seq = xin.shape[0]
    cb, tt = _conv_blocks(seq)

    def kern(x_ref, w_ref, b_ref, o_ref, pad_ref):
        _fill_padded(pad_ref, x_ref, seq)
        wv = w_ref[...]
        bv = b_ref[...]

        def step(t, carry):
            base = pl.multiple_of(t * tt, tt)
            win = pad_ref[pl.ds(base, tt + 2 * CONV_HALO), :]
            acc = jnp.zeros((tt, cb), F32) + bv
            for j, rows in _conv_taps(win, tt):
                acc = acc + wv[j:j + 1, :] * rows
            o_ref[pl.ds(base, tt), :] = acc
            return carry

        lax.fori_loop(0, seq // tt, step, 0)

    return pl.pallas_call(
        kern, name=name, grid=(D_MODEL // cb,),
        in_specs=[pl.BlockSpec((seq, cb), lambda j: (0, j)), pl.BlockSpec((CONV_W, cb), lambda j: (0, j)),
                  pl.BlockSpec((1, cb), lambda j: (0, j))],
        out_specs=pl.BlockSpec((seq, cb), lambda j: (0, j)),
        out_shape=jax.ShapeDtypeStruct((seq, D_MODEL), F32),
        scratch_shapes=[pltpu.VMEM((seq + 2 * CONV_HALO, cb), F32)],
        compiler_params=_params())(xin, w, b)


def _dwconv_wgrad(name, xin, dout):
    seq = xin.shape[0]
    cb, tt = _conv_blocks(seq)

    def kern(x_ref, d_ref, dw_ref, db_ref, pad_ref):
        _fill_padded(pad_ref, x_ref, seq)
        dw_ref[...] = jnp.zeros(dw_ref.shape, F32)
        db_ref[...] = jnp.zeros(db_ref.shape, F32)

        def step(t, carry):
            base = pl.multiple_of(t * tt, tt)
            win = pad_ref[pl.ds(base, tt + 2 * CONV_HALO), :]
            d = d_ref[pl.ds(base, tt), :]
            db_ref[...] += _colsum(d)
            for j, rows in _conv_taps(win, tt):
                dw_ref[j:j + 1, :] += _colsum(d * rows)
            return carry

        lax.fori_loop(0, seq // tt, step, 0)

    return pl.pallas_call(
        kern, name=name, grid=(D_MODEL // cb,),
        in_specs=[pl.BlockSpec((seq, cb), lambda j: (0, j)), pl.BlockSpec((seq, cb), lambda j: (0, j))],
        out_specs=[pl.BlockSpec((CONV_W, cb), lambda j: (0, j)), pl.BlockSpec((1, cb), lambda j: (0, j))],
        out_shape=[jax.ShapeDtypeStruct((CONV_W, D_MODEL), F32), jax.ShapeDtypeStruct((1, D_MODEL), F32)],
        scratch_shapes=[pltpu.VMEM((seq + 2 * CONV_HALO, cb), F32)],
        compiler_params=_params())(xin, dout)


def _mm(name, a, b, out_sds, grid, a_spec, b_spec, o_spec, contract, k_axis=None, bias=None, pair_send=()):
    dn = (contract, ((), ()))
    ns = len(pair_send)
    nb = 0 if bias is None else 1
    n_steps = math.prod(grid)

    def kern(*refs):
        a_ref, b_ref = refs[:2]
        bias_ref = refs[2] if nb else None
        o_ref = refs[2 + nb + ns]
        if ns:
            copies = _pair_send_copies(refs[2 + nb:2 + nb + ns], refs[3 + nb + ns:3 + nb + 2 * ns],
                                       *refs[3 + nb + 2 * ns:])
            step = pl.program_id(0)
            for ax in range(1, len(grid)):
                step = step * grid[ax] + pl.program_id(ax)

            @pl.when(step == 0)
            def _():
                for cp in copies():
                    cp.start()

        p = lax.dot_general(a_ref[...].astype(BF16), b_ref[...].astype(BF16), dn, preferred_element_type=F32)
        if bias is not None:
            p = p + bias_ref[...]
        if k_axis is None:
            o_ref[...] = p.astype(o_ref.dtype)
        else:
            k = pl.program_id(k_axis)

            @pl.when(k == 0)
            def _():
                o_ref[...] = p

            @pl.when(k > 0)
            def _():
                o_ref[...] += p

        if ns:
            @pl.when(step == n_steps - 1)
            def _():
                for cp in copies():
                    cp.wait()

    in_specs = [a_spec, b_spec]
    args = [a, b]
    if bias is not None:
        in_specs.append(pl.BlockSpec(bias.shape, lambda *_: (0,) * bias.ndim))
        args.append(bias)
    if not ns:
        return pl.pallas_call(kern, name=name, grid=grid, in_specs=in_specs, out_specs=o_spec, out_shape=out_sds,
                              compiler_params=_params())(*args)
    return pl.pallas_call(
        kern, name=name, grid=grid, in_specs=in_specs + [_ANY] * ns, out_specs=[o_spec] + [_ANY] * ns,
        out_shape=[out_sds] + [jax.ShapeDtypeStruct((N_CHIP,) + g.shape[2:], g.dtype) for g in pair_send],
        scratch_shapes=[pltpu.SemaphoreType.DMA((ns, N_CHIP)), pltpu.SemaphoreType.DMA((ns, N_CHIP))],
        compiler_params=_params())(*args, *pair_send)


def _mm_rows(name, a, b, tm, trans_b=False, bias=None):
    m, k = a.shape
    n = b.shape[0] if trans_b else b.shape[1]
    contract = ((1,), (1,)) if trans_b else ((1,), (0,))
    return _mm(name, a, b, jax.ShapeDtypeStruct((m, n), F32), (m // tm,),
               pl.BlockSpec((tm, k), lambda i: (i, 0)), pl.BlockSpec(b.shape, lambda i: (0, 0)),
               pl.BlockSpec((tm, n), lambda i: (i, 0)), contract, bias=bias)


def _mm_glu(name, a, b, bias, tm):
    m, k = a.shape
    n = b.shape[1]

    def kern(a_ref, b_ref, bias_ref, ag_ref, hg_ref):
        v = jnp.dot(a_ref[...], b_ref[...], preferred_element_type=F32)
        ag_ref[...] = v
        v = v + bias_ref[...]
        hg_ref[...] = v[:, :n // 2] * _sigmoid(v[:, n // 2:])

    return pl.pallas_call(
        kern, name=name, grid=(m // tm,),
        in_specs=[pl.BlockSpec((tm, k), lambda i: (i, 0)), pl.BlockSpec(b.shape, lambda i: (0, 0)),
                  pl.BlockSpec((1, n), lambda i: (0, 0))],
        out_specs=[pl.BlockSpec((tm, n), lambda i: (i, 0)), pl.BlockSpec((tm, n // 2), lambda i: (i, 0))],
        out_shape=[jax.ShapeDtypeStruct((m, n), F32), jax.ShapeDtypeStruct((m, n // 2), F32)],
        compiler_params=_params())(a, b, bias)


def _mm_rows_res(name, a, b, h, gt, tm, bias=None, norm=None, final=None):
    m, k = a.shape
    n = b.shape[1]
    nb = 0 if bias is None else 1
    extra = list(norm or ()) + list(final or ())

    def kern(a_ref, b_ref, h_ref, gt_ref, *rest):
        y = jnp.dot(a_ref[...], b_ref[...], preferred_element_type=F32)
        if nb:
            y = y + rest[0][...]
        ex = rest[nb:nb + len(extra)]
        outs = rest[nb + len(extra):]
        hn = h_ref[...] + gt_ref[...] * y
        if final is None:
            outs[0][...] = y
            outs[1][...] = hn
        if norm is not None:
            nn, _ = _rms(hn)
            outs[2][...] = ((nn * ex[0][...]) * (1.0 + ex[1][...]) + ex[2][...]).astype(BF16)
        if final is not None:
            @pl.when(pl.program_id(0) == 0)
            def _():
                for o in (outs[1], outs[2], outs[4]):
                    o[...] = jnp.zeros(o.shape, F32)

            gv = ex[1][...]
            nn, r = _rms(hn)
            e = nn * gv - ex[0][...]
            outs[2][...] += _colsum(e * e) * (0.5 / D_MODEL)
            dout = e * (1.0 / D_MODEL)
            outs[1][...] += _colsum(dout * nn)
            dh = _rms_bwd(nn, r, dout * gv)
            outs[0][...] = dh
            outs[3][...] = (dh * gt_ref[...]).astype(BF16)
            outs[4][...] += _colsum(dh * y)

    row = pl.BlockSpec((tm, n), lambda i: (i, 0))
    vec = pl.BlockSpec((1, n), lambda i: (0, 0))
    sds = jax.ShapeDtypeStruct((m, n), F32)
    vsd = jax.ShapeDtypeStruct((1, n), F32)
    in_specs = [pl.BlockSpec((tm, k), lambda i: (i, 0)), pl.BlockSpec(b.shape, lambda i: (0, 0)), row, vec] + [vec] * nb
    out_specs, out_shape = [row, row], [sds, sds]
    if norm is not None:
        in_specs += [vec] * 3
        out_specs.append(row)
        out_shape.append(jax.ShapeDtypeStruct((m, n), BF16))
    if final is not None:
        in_specs += [row, vec]
        out_specs = [row, vec, vec, row, vec]
        out_shape = [sds, vsd, vsd, jax.ShapeDtypeStruct((m, n), BF16), vsd]
    return pl.pallas_call(kern, name=name, grid=(m // tm,), in_specs=in_specs, out_specs=out_specs,
                          out_shape=out_shape, compiler_params=_params())(
        a, b, h, gt, *([bias] if nb else []), *extra)


def _ffn_in_swiglu(name, xf, wblk, tm, gather=()):
    m, k = xf.shape
    nblk = wblk.shape[2]
    ng = len(gather)
    n_i = m // tm
    last = 2 * n_i - 1

    def kern(x_ref, wg_ref, wu_ref, *rest):
        gu_ref, a_ref = rest[ng:ng + 2]
        if ng:
            start, forward, finish = _gather_phases(rest[:ng], rest[ng + 2:2 * ng + 2], *rest[2 * ng + 2:])
            step = pl.program_id(0) * n_i + pl.program_id(1)
            pl.when(step == 0)(start)
            pl.when(step == (3 * last) // 4)(forward)
        xv = x_ref[...]
        g = jnp.dot(xv, wg_ref[...], preferred_element_type=F32)
        u = jnp.dot(xv, wu_ref[...], preferred_element_type=F32)
        gu_ref[0] = g.astype(BF16)
        gu_ref[1] = u.astype(BF16)
        a_ref[...] = (g * _sigmoid(g) * u).astype(BF16)
        if ng:
            pl.when(step == last)(finish)

    return pl.pallas_call(
        kern, name=name, grid=(2, n_i),
        in_specs=[pl.BlockSpec((tm, k), lambda j, i: (i, 0)), pl.BlockSpec((None, k, nblk), lambda j, i: (j, 0, 0)),
                  pl.BlockSpec((None, k, nblk), lambda j, i: (j + 2, 0, 0))] + [_ANY] * ng,
        out_specs=[pl.BlockSpec((2, tm, nblk), lambda j, i: (0, i, j)),
                   pl.BlockSpec((tm, nblk), lambda j, i: (i, j))] + [_ANY] * ng,
        out_shape=[jax.ShapeDtypeStruct((2, m, 2 * nblk), BF16), jax.ShapeDtypeStruct((m, 2 * nblk), BF16)]
        + [jax.ShapeDtypeStruct((N_DEV,) + g.shape, g.dtype) for g in gather],
        scratch_shapes=_gather_scratch(ng) if ng else [],
        compiler_params=_params())(xf, wblk, wblk, *gather)


def _ffn_out_dx_swiglu(name, dy, wo, gu, tm):
    m, k = dy.shape
    nblk = gu.shape[2] // 2

    def kern(dy_ref, w_ref, gu_ref, o_ref):
        da = lax.dot_general(dy_ref[...], w_ref[...], (((1,), (1,)), ((), ())), preferred_element_type=F32)
        g = gu_ref[0].astype(F32)
        sg = _sigmoid(g)
        o_ref[0] = (da * gu_ref[1].astype(F32) * (sg * (1.0 + g * (1.0 - sg)))).astype(BF16)
        o_ref[1] = (da * (g * sg)).astype(BF16)

    return pl.pallas_call(
        kern, name=name, grid=(2, m // tm),
        in_specs=[pl.BlockSpec((tm, k), lambda j, i: (i, 0)), pl.BlockSpec((nblk, k), lambda j, i: (j, 0)),
                  pl.BlockSpec((2, tm, nblk), lambda j, i: (0, i, j))],
        out_specs=pl.BlockSpec((2, tm, nblk), lambda j, i: (0, i, j)),
        out_shape=jax.ShapeDtypeStruct(gu.shape, BF16), compiler_params=_params())(dy, wo, gu)


def _ffn_in_dx_norm_bwd(name, dgu, wblk, h, dh_in, g, sc, tm, gate=None):
    nb, k, nblk = wblk.shape

    def dx_fn(a_ref, w_ref):
        acc = None
        for j in range(nb):
            p = lax.dot_general(a_ref[j // 2, :, (j % 2) * nblk:(j % 2 + 1) * nblk], w_ref[j],
                                (((1,), (1,)), ((), ())), preferred_element_type=F32)
            acc = p if acc is None else acc + p
        return acc

    return _mm_row_epilogue(name, [dgu, wblk],
                            [pl.BlockSpec((2, tm, 2 * nblk), lambda i: (0, i, 0)),
                             pl.BlockSpec(wblk.shape, lambda i: (0, 0, 0))], dx_fn, [(h, None), (dh_in, None)],
                            [g, sc], 3, _norm_bwd_epilogue, tm, gate=gate)


def _mm_t_epilogue(name, a, b, rows_in, vecs_in, n_vec_out, epilogue, tm, gate=None, exchange=()):
    def dx_fn(a_ref, b_ref):
        return lax.dot_general(a_ref[...], b_ref[...], (((1,), (1,)), ((), ())), preferred_element_type=F32)

    return _mm_row_epilogue(name, [a, b], [pl.BlockSpec((tm, a.shape[1]), lambda i: (i, 0)),
                                          pl.BlockSpec(b.shape, lambda i: (0, 0))], dx_fn, rows_in, vecs_in,
                            n_vec_out, epilogue, tm, gate, exchange)


def _mm_halves_wgrad_blk(name, a, b2, tk, pair_send=()):
    t, k = a.shape
    nblk = b2.shape[2] // 2
    return _mm(name, a, b2, jax.ShapeDtypeStruct((N_CHIP, k, nblk), F32), (N_CHIP, t // tk),
               pl.BlockSpec((tk, k), lambda j, i: (i, 0)),
               pl.BlockSpec((None, tk, nblk), lambda j, i: (j // 2, i, j % 2)),
               pl.BlockSpec((None, k, nblk), lambda j, i: (j, 0, 0)), ((0,), (0,)), k_axis=1, pair_send=pair_send)


def _mm_wgrad(name, a, b, tk):
    t, k = a.shape
    n = b.shape[1]
    tn = n if k * n * 4 <= 4 * 1024 * 1024 else 512
    return _mm(name, a, b, jax.ShapeDtypeStruct((k, n), F32), (n // tn, t // tk),
               pl.BlockSpec((tk, k), lambda j, i: (i, 0)), pl.BlockSpec((tk, tn), lambda j, i: (i, j)),
               pl.BlockSpec((k, tn), lambda j, i: (0, j)), ((0,), (0,)), k_axis=1)


def _attn_fwd(name, q_all, k, v, n_ctx, tq, gather=()):
    r = k.shape[1]
    s = q_all.shape[0] - n_ctx
    gw = Q_GROUP * HEAD_DIM
    ng = len(gather)
    n_i = s // tq
    last = N_KV_HEADS * n_i - 1
    forward_at = (7 * last) // 8

    def kern(q_ref, k_ref, v_ref, *rest):
        o_ref, lse_ref = rest[ng], rest[ng + 1]
        if ng:
            start, forward, finish = _gather_phases(rest[:ng], rest[ng + 2:2 * ng + 2], *rest[2 * ng + 2:])
            step = pl.program_id(0) * n_i + pl.program_id(1)
            pl.when(step == 0)(start)
            pl.when(step == forward_at)(forward)
        kv = k_ref[...]
        vv = v_ref[...]
        for g in range(Q_GROUP):
            sl = slice(g * HEAD_DIM, (g + 1) * HEAD_DIM)
            sc = lax.dot_general(q_ref[:, sl], kv, (((1,), (1,)), ((), ())), preferred_element_type=F32)
            m = jnp.max(sc, axis=-1, keepdims=True)
            p = jnp.exp(sc - m)
            ol = jnp.dot(p.astype(BF16), vv, preferred_element_type=F32)
            l = ol[:, HEAD_DIM:HEAD_DIM + 1]
            o_ref[:, sl] = ol[:, :HEAD_DIM] / l
            lse_ref[g] = m + jnp.log(l)
        if ng:
            pl.when(step == last)(finish)

    return pl.pallas_call(
        kern, name=name, grid=(N_KV_HEADS, n_i),
        in_specs=[pl.BlockSpec((tq, gw), lambda j, i: (i + n_ctx // tq, j)),
                  pl.BlockSpec((None, r, HEAD_DIM), lambda j, i: (j, 0, 0)),
                  pl.BlockSpec((None, r, 2 * HEAD_DIM), lambda j, i: (j, 0, 0))] + [_ANY] * ng,
        out_specs=[pl.BlockSpec((tq, gw), lambda j, i: (i, j)),
                   pl.BlockSpec((Q_GROUP, tq, 1), lambda j, i: (j, i, 0))] + [_ANY] * ng,
        out_shape=[jax.ShapeDtypeStruct((s, ATTN_W), F32), jax.ShapeDtypeStruct((N_Q_HEADS, s, 1), F32)]
        + [jax.ShapeDtypeStruct((N_DEV,) + a.shape, a.dtype) for a in gather],
        scratch_shapes=_gather_scratch(ng) if ng else [],
        compiler_params=_params())(q_all, k, v, *gather)


def _attn_bwd(name, q_all, k, v, o, lse, dcat, n_ctx, tq, exchange=()):
    nkv, r, _ = k.shape
    s = o.shape[0]
    gw = Q_GROUP * HEAD_DIM
    ne = len(exchange)
    n_i = s // tq
    last = nkv * n_i - 1

    def kern(q_ref, k_ref, v_ref, o_ref, lse_ref, do_ref, *rest):
        dq_ref, dk_ref, dv_ref = rest[ne:ne + 3]
        ds_all, p_all = rest[2 * ne + 3:2 * ne + 5]
        if ne:
            start, finish = _exchange_phases(rest[:ne], rest[ne + 3:2 * ne + 3], *rest[2 * ne + 5:])
            step = pl.program_id(0) * n_i + pl.program_id(1)
            pl.when(step == 0)(start)

        @pl.when(pl.program_id(1) == 0)
        def _():
            dk_ref[...] = jnp.zeros(dk_ref.shape, F32)
            dv_ref[...] = jnp.zeros(dv_ref.shape, F32)

        kv = k_ref[...]
        vv = v_ref[...]
        qs = []
        dos = []
        for g in range(Q_GROUP):
            sl = slice(g * HEAD_DIM, (g + 1) * HEAD_DIM)
            rows = slice(g * tq, (g + 1) * tq)
            qv = q_ref[:, sl]
            dov = do_ref[:, sl]
            delta = jnp.sum(dov * o_ref[:, sl], axis=-1, keepdims=True)
            sc = lax.dot_general(qv, kv, (((1,), (1,)), ((), ())), preferred_element_type=F32)
            p = jnp.exp(sc - lse_ref[g])
            dob = dov.astype(BF16)
            dp = lax.dot_general(dob, vv, (((1,), (1,)), ((), ())), preferred_element_type=F32)
            dsb = (p * (dp - delta)).astype(BF16)
            dq_ref[:, sl] = jnp.dot(dsb, kv, preferred_element_type=F32)
            ds_all[rows, :] = dsb
            p_all[rows, :] = p.astype(BF16)
            qs.append(qv)
            dos.append(dob)
        dk_ref[...] += lax.dot_general(jnp.concatenate(qs, axis=0), ds_all[...], (((0,), (0,)), ((), ())),
                                       preferred_element_type=F32)
        dv_ref[...] += lax.dot_general(jnp.concatenate(dos, axis=0), p_all[...], (((0,), (0,)), ((), ())),
                                       preferred_element_type=F32)
        if ne:
            pl.when(step == last)(finish)

    kvmap = lambda j, i: (j, 0, 0)
    return pl.pallas_call(
        kern, name=name, grid=(nkv, n_i),
        in_specs=[pl.BlockSpec((tq, gw), lambda j, i: (i + n_ctx // tq, j)), pl.BlockSpec((None, r, HEAD_DIM), kvmap),
                  pl.BlockSpec((None, r, HEAD_DIM), kvmap), pl.BlockSpec((tq, gw), lambda j, i: (i, j)),
                  pl.BlockSpec((Q_GROUP, tq, 1), lambda j, i: (j, i, 0)),
                  pl.BlockSpec((tq, gw), lambda j, i: (i, j))] + [_ANY] * ne,
        out_specs=[pl.BlockSpec((tq, gw), lambda j, i: (i, j)), pl.BlockSpec((None, HEAD_DIM, r), kvmap),
                   pl.BlockSpec((None, HEAD_DIM, r), kvmap)] + [_ANY] * ne,
        out_shape=[jax.ShapeDtypeStruct((s, ATTN_W), F32), jax.ShapeDtypeStruct((nkv, HEAD_DIM, r), F32),
                   jax.ShapeDtypeStruct((nkv, HEAD_DIM, r), F32)]
        + [jax.ShapeDtypeStruct(a.shape, a.dtype) for a in exchange],
        scratch_shapes=[pltpu.VMEM((Q_GROUP * tq, r), BF16)] * 2 + (_exchange_scratch(ne) if ne else []),
        compiler_params=_params())(q_all, k, v, o, lse, dcat, *exchange)


def _mod_fwd(name, c16, wm, bm):
    n = wm.shape[2]

    def kern(c_ref, w_ref, b_ref, o_ref):
        cv = c_ref[...]
        a = (cv * _sigmoid(cv)).astype(BF16)
        o_ref[...] = jnp.dot(a, w_ref[...].astype(BF16), preferred_element_type=F32) + b_ref[...]

    return pl.pallas_call(
        kern, name=name, grid=(2,),
        in_specs=[pl.BlockSpec(c16.shape, lambda l: (0, 0)), pl.BlockSpec((None, D_MODEL, n), lambda l: (l, 0, 0)),
                  pl.BlockSpec((None, 1, n), lambda l: (l, 0, 0))],
        out_specs=pl.BlockSpec((None, 16, n), lambda l: (l, 0, 0)),
        out_shape=jax.ShapeDtypeStruct((2, 16, n), F32), compiler_params=_params())(c16, wm, bm)


def _mod_wgrad(name, c16, dm):
    n = dm.shape[2]
    tn = 512

    def kern(c_ref, d_ref, o_ref):
        cv = c_ref[...]
        a = cv * _sigmoid(cv)
        o_ref[...] = lax.dot_general(a, d_ref[...], (((0,), (0,)), ((), ())), preferred_element_type=F32,
                                     precision=lax.Precision.HIGHEST)

    return pl.pallas_call(
        kern, name=name, grid=(2, n // tn),
        in_specs=[pl.BlockSpec(c16.shape, lambda l, j: (0, 0)), pl.BlockSpec((None, 16, tn), lambda l, j: (l, 0, j))],
        out_specs=pl.BlockSpec((None, D_MODEL, tn), lambda l, j: (l, 0, j)),
        out_shape=jax.ShapeDtypeStruct((2, D_MODEL, n), F32), compiler_params=_params())(c16, dm)


def _cctx_grad(name, parts, c_ctx):
    def kern(p_ref, c_ref, o_ref):
        d = p_ref[0, 0:1, :] + p_ref[2, 0:1, :] + p_ref[4, 0:1, :] + p_ref[6, 0:1, :]
        cv = c_ref[...]
        sg = _sigmoid(cv)
        o_ref[...] = d * (sg * (1.0 + cv * (1.0 - sg)))

    return pl.pallas_call(kern, name=name, out_shape=jax.ShapeDtypeStruct((1, D_MODEL), F32),
                          compiler_params=_params())(parts, c_ctx)


def _sum_slots(name, g, tr):
    n, rows, cols = g.shape

    def kern(g_ref, o_ref):
        acc = g_ref[0].astype(F32)
        for i in range(1, n):
            acc = acc + g_ref[i].astype(F32)
        o_ref[...] = acc

    return pl.pallas_call(kern, name=name, grid=(rows // tr,),
                          in_specs=[pl.BlockSpec((n, tr, cols), lambda i: (0, i, 0))],
                          out_specs=pl.BlockSpec((tr, cols), lambda i: (i, 0)),
                          out_shape=jax.ShapeDtypeStruct((rows, cols), F32), compiler_params=_params())(g)


def _sum_into(name, g, cvec, n_layers, layer, prev=None):
    n, r, cc = g.shape
    tr = _row_tile(r, cc, 512 * 1024)

    def kern(c_ref, g_ref, *rest):
        acc = g_ref[0].astype(F32)
        for i in range(1, n):
            acc = acc + g_ref[i].astype(F32)
        rest[-1][...] = acc

    in_specs = [pl.BlockSpec((n, tr, cc), lambda i, c_ref: (0, i, 0))]
    args = [cvec, g]
    aliases = {}
    if prev is not None:
        in_specs.append(_ANY)
        args.append(prev)
        aliases = {2: 0}
    grid_spec = pltpu.PrefetchScalarGridSpec(
        num_scalar_prefetch=1, grid=(r // tr,), in_specs=in_specs,
        out_specs=pl.BlockSpec((None, None, tr, cc), lambda i, c_ref: (layer, c_ref[0], i, 0)))
    return pl.pallas_call(kern, name=name, grid_spec=grid_spec,
                          out_shape=jax.ShapeDtypeStruct((n_layers, 2, r, cc), F32),
                          input_output_aliases=aliases, compiler_params=_params())(*args)


def _row_tile(rows, cols, max_bytes=1024 * 1024):
    if rows * cols * 4 <= 2 * max_bytes:
        return rows
    best = None
    for t in range(16, rows + 1, 16):
        if rows % t == 0 and t * cols * 4 <= max_bytes:
            best = t
    assert best is not None, (rows, cols)
    return best


def _adamw(name, w, g, m, v):
    shape = w.shape
    cols = shape[-1]
    rows = w.size // cols
    tr = _row_tile(rows, cols)

    def kern(w_ref, g_ref, m_ref, v_ref, d_ref, nm_ref, nv_ref):
        gv = g_ref[...]
        m2 = ADAM_B1 * m_ref[...] + (1.0 - ADAM_B1) * gv
        v2 = ADAM_B2 * v_ref[...] + (1.0 - ADAM_B2) * (gv * gv)
        m_hat = m2 / (1.0 - ADAM_B1 ** ADAM_STEP)
        v_hat = v2 / (1.0 - ADAM_B2 ** ADAM_STEP)
        d_ref[...] = -ADAM_LR * (m_hat / (jnp.sqrt(v_hat) + ADAM_EPS) + ADAM_WD * w_ref[...])
        nm_ref[...] = m2
        nv_ref[...] = v2

    spec = pl.BlockSpec((tr, cols), lambda i: (i, 0))
    sds = jax.ShapeDtypeStruct((rows, cols), F32)
    outs = pl.pallas_call(kern, name=name, grid=(rows // tr,), in_specs=[spec] * 4, out_specs=[spec] * 3,
                          out_shape=[sds] * 3, compiler_params=_params())(
        w.reshape(rows, cols), g.reshape(rows, cols), m.reshape(rows, cols), v.reshape(rows, cols))
    return tuple(o.reshape(shape) for o in outs)


_ANY = pl.BlockSpec(memory_space=pl.ANY)


def _mesh_pos():
    return lax.axis_index("x"), lax.axis_index("y"), lax.axis_index("c")


def _gather_phases(srcs, outs, send_sems, recv_sems, local_sems):
    n = len(srcs)
    x, y, c = _mesh_pos()
    me = (x, y, c)
    sibling = (x, y, 1 - c)
    chips = [(1 - x, y), (x, 1 - y), (1 - x, 1 - y)]

    def slot(px, py, pc):
        return 4 * px + 2 * py + pc

    def copy(t, k, s, to, src=None):
        dst = outs[t].at[s]
        return pltpu.make_async_remote_copy(src_ref=dst if src is None else src, dst_ref=dst,
                                            send_sem=send_sems.at[t, k], recv_sem=recv_sems.at[t, k],
                                            device_id=to, device_id_type=MESH)

    def mine(t):
        return pltpu.make_async_copy(srcs[t], outs[t].at[slot(*me)], local_sems.at[t])

    def first(t):
        return [copy(t, 0, slot(*me), sibling, src=srcs[t])] + [
            copy(t, 1 + j, slot(*me), (px, py, c), src=srcs[t]) for j, (px, py) in enumerate(chips)]

    def passed(t, j):
        px, py = chips[j]
        return copy(t, 4 + j, slot(px, py, c), sibling)

    def start():
        for t in range(n):
            mine(t).start()
        for t in range(n):
            for cp in first(t):
                cp.start()

    def forward():
        for j, (px, py) in enumerate(chips):
            for t in range(n):
                copy(t, 1 + j, slot(px, py, c), me).wait_recv()
                passed(t, j).start()

    def finish():
        for t in range(n):
            copy(t, 0, slot(x, y, 1 - c), me).wait_recv()
            for j, (px, py) in enumerate(chips):
                copy(t, 4 + j, slot(px, py, 1 - c), me).wait_recv()
        for t in range(n):
            for cp in first(t) + [passed(t, j) for j in range(len(chips))]:
                cp.wait_send()
            mine(t).wait()

    return start, forward, finish


def _gather_scratch(n):
    return [pltpu.SemaphoreType.DMA((n, 7)), pltpu.SemaphoreType.DMA((n, 7)), pltpu.SemaphoreType.DMA((n,))]


def _allgather(name, items):
    n = len(items)

    def body(*refs):
        start, forward, finish = _gather_phases(refs[:n], refs[n:2 * n], *refs[2 * n:])
        start()
        forward()
        finish()

    return pl.pallas_call(
        body, name=name, in_specs=[_ANY] * n, out_specs=[_ANY] * n,
        out_shape=[jax.ShapeDtypeStruct((N_DEV,) + a.shape, a.dtype) for a in items],
        scratch_shapes=_gather_scratch(n), compiler_params=_params())(*items)


def _pair_send_copies(srcs, bufs, send_sems, recv_sems):
    x, y, c = _mesh_pos()

    def copies():
        return [pltpu.make_async_remote_copy(
            src_ref=srcs[t].at[k, 1 - c], dst_ref=bufs[t].at[k], send_sem=send_sems.at[t, k],
            recv_sem=recv_sems.at[t, k], device_id=(x, y, 1 - c), device_id_type=MESH)
            for t in range(len(srcs)) for k in range(N_CHIP)]

    return copies


def _rs_pair_send(name, gs):
    n = len(gs)

    def body(*refs):
        copies = _pair_send_copies(refs[:n], refs[n:2 * n], *refs[2 * n:])
        for cp in copies():
            cp.start()
        for cp in copies():
            cp.wait()

    return pl.pallas_call(
        body, name=name, in_specs=[_ANY] * n, out_specs=[_ANY] * n,
        out_shape=[jax.ShapeDtypeStruct((N_CHIP,) + g.shape[2:], g.dtype) for g in gs],
        scratch_shapes=[pltpu.SemaphoreType.DMA((n, N_CHIP)), pltpu.SemaphoreType.DMA((n, N_CHIP))],
        compiler_params=_params())(*gs)


def _rs_pair_add(name, g, buf, cvec):
    _, _, r, cc = g.shape
    tr = _row_tile(r, cc, 2 * 1024 * 1024)

    def kern(c_ref, g_ref, b_ref, o_ref):
        o_ref[...] = (g_ref[...] + b_ref[...]).astype(BF16)

    grid_spec = pltpu.PrefetchScalarGridSpec(
        num_scalar_prefetch=1, grid=(N_CHIP, r // tr),
        in_specs=[pl.BlockSpec((None, None, tr, cc), lambda k, i, c_ref: (k, c_ref[0], i, 0)),
                  pl.BlockSpec((None, tr, cc), lambda k, i, c_ref: (k, i, 0))],
        out_specs=pl.BlockSpec((None, tr, cc), lambda k, i, c_ref: (k, i, 0)))
    return pl.pallas_call(kern, name=name, grid_spec=grid_spec,
                          out_shape=jax.ShapeDtypeStruct((N_CHIP, r, cc), BF16),
                          compiler_params=_params())(cvec, g, buf)


def _exchange_phases(srcs, bufs, send_sems, recv_sems, local_sems):
    n = len(srcs)
    x, y, c = _mesh_pos()
    kme = 2 * x + y
    chips = [(1 - x, y), (x, 1 - y), (1 - x, 1 - y)]

    def copies():
        local = [pltpu.make_async_copy(srcs[t].at[kme], bufs[t].at[kme], local_sems.at[t]) for t in range(n)]
        remote = []
        for t in range(n):
            for j, (px, py) in enumerate(chips):
                remote.append(pltpu.make_async_remote_copy(
                    src_ref=srcs[t].at[2 * px + py], dst_ref=bufs[t].at[kme], send_sem=send_sems.at[t, j],
                    recv_sem=recv_sems.at[t, j], device_id=(px, py, c), device_id_type=MESH))
        return local, remote

    def start():
        local, remote = copies()
        for cp in local + remote:
            cp.start()

    def finish():
        local, remote = copies()
        for cp in remote + local:
            cp.wait()

    return start, finish


def _exchange_scratch(n):
    return [pltpu.SemaphoreType.DMA((n, 3)), pltpu.SemaphoreType.DMA((n, 3)), pltpu.SemaphoreType.DMA((n,))]


def _rs_chip_exchange(name, ss):
    n = len(ss)

    def body(*refs):
        start, finish = _exchange_phases(refs[:n], refs[n:2 * n], *refs[2 * n:])
        start()
        finish()

    return pl.pallas_call(
        body, name=name, in_specs=[_ANY] * n, out_specs=[_ANY] * n,
        out_shape=[jax.ShapeDtypeStruct(s.shape, s.dtype) for s in ss],
        scratch_shapes=_exchange_scratch(n), compiler_params=_params())(*ss)


def _rs_pair_share(name, dsts, gather=()):
    nd = len(dsts)
    ng = len(gather)

    def body(*refs):
        ins = refs[:nd]
        outs = refs[nd + ng:2 * nd + ng]
        send_sems, recv_sems = refs[2 * (nd + ng):2 * (nd + ng) + 2]
        if ng:
            start, forward, finish = _gather_phases(refs[nd:nd + ng], refs[2 * nd + ng:2 * (nd + ng)],
                                                    *refs[2 * (nd + ng) + 2:])
            start()
        x, y, c = _mesh_pos()
        copies = []
        for d in range(nd):
            for l in range(dsts[d].shape[0]):
                copies.append(pltpu.make_async_remote_copy(
                    src_ref=ins[d].at[l, c], dst_ref=outs[d].at[l, c], send_sem=send_sems.at[d, l],
                    recv_sem=recv_sems.at[d, l], device_id=(x, y, 1 - c), device_id_type=MESH))
        for cp in copies:
            cp.start()
        if ng:
            forward()
        for cp in copies:
            cp.wait()
        if ng:
            finish()

    return pl.pallas_call(
        body, name=name, in_specs=[_ANY] * (nd + ng), out_specs=[_ANY] * (nd + ng),
        out_shape=[jax.ShapeDtypeStruct(a.shape, a.dtype) for a in dsts]
        + [jax.ShapeDtypeStruct((N_DEV,) + a.shape, a.dtype) for a in gather],
        input_output_aliases={d: d for d in range(nd)},
        scratch_shapes=[pltpu.SemaphoreType.DMA((nd, 2)), pltpu.SemaphoreType.DMA((nd, 2))]
        + (_gather_scratch(ng) if ng else []),
        compiler_params=_params())(*dsts, *gather)


def _rope_tables(seq, n_ctx):
    t = jnp.arange(seq)
    row = (t // GRID_W).astype(F32)
    col = (t % GRID_W).astype(F32)
    inv = ROPE_THETA ** (-jnp.arange(0, HEAD_DIM // 2, 2, dtype=F32) / (HEAD_DIM // 2))
    ang_r = row[:, None] * inv[None, :]
    ang_c = col[:, None] * inv[None, :]
    cos = jnp.concatenate([jnp.cos(ang_r)] * 2 + [jnp.cos(ang_c)] * 2, axis=1)
    sin = jnp.concatenate([-jnp.sin(ang_r), jnp.sin(ang_r), -jnp.sin(ang_c), jnp.sin(ang_c)], axis=1)
    cos = jnp.concatenate([jnp.ones((n_ctx, HEAD_DIM), F32), cos], axis=0)
    sin = jnp.concatenate([jnp.zeros((n_ctx, HEAD_DIM), F32), sin], axis=0)
    return jnp.tile(cos, (1, N_KV_HEADS)), jnp.tile(sin, (1, N_KV_HEADS))


def _to_heads(a, nh):
    return a.reshape(a.shape[0], nh, HEAD_DIM).transpose(1, 0, 2)


def _sample_step(x, ctx, tgt, mod, cmod, W, pending, cvec):
    W = dict(W)
    seq = x.shape[0]
    n_ctx = ctx.shape[0]
    tm = min(256, seq)
    tmc = min(256, n_ctx)
    tmm = 512 if seq % 512 == 0 else tm
    tmw = 1024 if seq % 1024 == 0 else tmm
    tmr = (n_ctx + seq) // 4 if (n_ctx + seq) % 64 == 0 else tm
    tq = min(256, seq)
    assert n_ctx % tm == 0 and seq % tm == 0

    def mv(l, j):
        return mod[l, j * D_MODEL:(j + 1) * D_MODEL].reshape(1, D_MODEL)

    csh1 = cmod[:D_MODEL].reshape(1, D_MODEL)
    csc1 = cmod[D_MODEL:2 * D_MODEL].reshape(1, D_MODEL)
    g_mix = [W['g_mix'][l:l + 1] for l in range(2)]
    g_ffn = [W['g_ffn'][l:l + 1] for l in range(2)]
    cos, sin = _rope_tables(seq, n_ctx)
    gq = jnp.tile(W['q_gain'], (1, N_Q_HEADS))
    gk = jnp.tile(W['k_gain'], (1, N_KV_HEADS))
    bdq = jnp.kron(jnp.eye(N_Q_HEADS, dtype=F32), jnp.ones((HEAD_DIM, HEAD_DIM), F32)).astype(BF16)
    bdk = bdq[:KV_W, :KV_W]
    w_sp = W['w_sp'].astype(BF16)
    w_sp_t = w_sp.transpose(0, 2, 1)
    bsp = jnp.broadcast_to(W['b_sp'].T[:, :, None], (CHUNK, N_SG, CHUNK)).reshape(CHUNK, SG_W)

    def ffn_fwd(l, h_mid, xf, gather=(), **tail):
        gu, a, *gathered = _ffn_in_swiglu(f'l{l}_ffn_in', xf, W['ffn_in'][l], tmw, gather=gather)
        if gathered:
            W['pw2'] = gathered[0].reshape(D_MODEL, D_MODEL)
            W['ffn_out'][1] = gathered[1].reshape(D_FF, D_MODEL)
        outs = _mm_rows_res(f'l{l}_ffn_out', a, W['ffn_out'][l], h_mid, mv(l, 5), tmm, **tail)
        return outs, (h_mid, xf, gu, a)

    def blocked(a):
        cols = a.shape[-1]
        return a.reshape(N_CHIP, 2, a.size // (2 * N_CHIP * cols), cols)

    def ffn_bwd(l, dh_out, dy, y_mix, saved, pair_send=()):
        h_mid, xf, gu, a = saved
        dgu = _ffn_out_dx_swiglu(f'l{l}_ffn_out_dx', dy, W['ffn_out'][l], gu, tmw)
        d_wo = _mm_wgrad(f'l{l}_ffn_out_dw', a, dy, tmw)
        d_wi = _mm_halves_wgrad_blk(f'l{l}_ffn_in_dw', xf, dgu, tmw, pair_send=pair_send)
        if pair_send:
            d_wi, *sent = d_wi
        else:
            sent = []
        dh_mid, dy_mix, d_sh2, d_sc2, d_g, d_gt1, sum_dy = _ffn_in_dx_norm_bwd(
            f'l{l}_ffn_in_dx', dgu, W['ffn_in'][l], h_mid, dh_out, g_ffn[l], mv(l, 4), tm, gate=(y_mix, mv(l, 2)))
        return dh_mid, dy_mix, d_wi, d_wo, d_g, (d_sh2, d_sc2), d_gt1, sum_dy, sent

    xm0 = _modnorm_fwd('l0_mix_norm', x, g_mix[0], mv(0, 1), mv(0, 0), tm)
    xc0 = _modnorm_fwd('l0_ctx_norm', ctx, g_mix[0], csc1, csh1, tmc)
    xall = jnp.concatenate([xc0, xm0], axis=0)
    p_qkv, q_all, k_all, v_all = _mm_qk_prep('l0_qkv', xall, W['w_qkv'], cos, sin, gq, gk, bdq, bdk, tmr)
    p_sg, sg = _mm_sg('l0_sg_in', xm0, W['w_sgp'], w_sp, bsp, tmm)
    kh = _to_heads(k_all, N_KV_HEADS)
    vh = _to_heads(v_all, N_KV_HEADS)
    v_ones = jnp.concatenate([vh, jnp.ones(vh.shape[:2] + (1,), BF16),
                              jnp.zeros(vh.shape[:2] + (HEAD_DIM - 1,), BF16)], axis=2)
    o, lse, *gw = _attn_fwd('l0_attn', q_all, kh, v_ones, n_ctx, tq, gather=pending[:5])
    W['w_out'] = gw[0].reshape(D_MODEL, D_MODEL)
    W['ffn_in'] = [gw[1].reshape(N_CHIP, D_MODEL, FF_BLK), gw[4].reshape(N_CHIP, D_MODEL, FF_BLK)]
    W['ffn_out'] = [gw[2].reshape(D_FF, D_MODEL), None]
    W['pw1'] = gw[3].reshape(N_CHIP, D_MODEL, PW1_BLK).transpose(1, 0, 2).reshape(D_MODEL, 2 * D_MODEL)
    cat = jnp.concatenate([o.astype(BF16), sg], axis=1)
    y0, h1, xf0 = _mm_rows_res('l0_out', cat, W['w_out'], x, mv(0, 2), tmw, norm=(g_ffn[0], mv(0, 4), mv(0, 3)))
    (f0, h2, xm1), ffn0 = ffn_fwd(0, h1, xf0, gather=pending[5:], norm=(g_mix[1], mv(1, 1), mv(1, 0)))

    ag, hg = _mm_glu('l1_pw1', xm1, W['pw1'], W['b_pw1'], tmm)
    hc = _dwconv('l1_conv', hg, W['w_dw'], W['b_dw'])
    s1 = _lnsilu_fwd('l1_ln_silu', hc, W['ln_g'], W['ln_b'], tm)
    y1, h3, xf1 = _mm_rows_res('l1_pw2', s1, W['pw2'], h2, mv(1, 2), tmw, bias=W['b_pw2'],
                               norm=(g_ffn[1], mv(1, 4), mv(1, 3)))
    (dh4, d_g_final, loss_vec, dyf1, d_gt2_1), ffn1 = ffn_fwd(1, h3, xf1, final=(tgt, W['g_final']))

    dh3, dy1, d_wi1, d_wo1, d_gffn1, dmod1_ffn, d_gt1_1, d_b_pw2, _ = ffn_bwd(1, dh4, dyf1, y1, ffn1)
    d_pw2 = _mm_wgrad('l1_pw2_dw', s1, dy1, tmw)
    dhc, d_ln_g, d_ln_b = _mm_t_epilogue('l1_pw2_dx', dy1, W['pw2'], [(hc, None)], [W['ln_g'], W['ln_b']], 2,
                                         _lnsilu_bwd_epilogue, tmm)
    dhg = _dwconv('l1_conv_dx', dhc, W['w_dw'][::-1], jnp.zeros((1, D_MODEL), F32))
    d_w_dw, d_b_dw = _dwconv_wgrad('l1_conv_dw', hg, dhc)
    dag, d_b_pw1 = _glu_bwd('l1_glu_bwd', ag, dhg, W['b_pw1'], tm)
    d_pw1 = _mm_wgrad('l1_pw1_dw', xm1, dag, tmw).reshape(D_MODEL, N_CHIP, PW1_BLK).transpose(1, 0, 2)
    dh2, dyf0, d_sh1_1, d_sc1_1, d_gmix1, d_gt2_0, _ = _mm_t_epilogue(
        'l1_pw1_dx', dag, W['pw1'], [(h2, None), (dh3, None)], [g_mix[1], mv(1, 1)], 3, _norm_bwd_epilogue, tmm,
        gate=(f0, mv(0, 5)))

    gs1 = [blocked(g) for g in (d_wi1, d_wo1, d_pw1, d_pw2)]
    dh1, dy0, d_wi0, d_wo0, d_gffn0, dmod0_ffn, d_gt1_0, _, sib1 = ffn_bwd(0, dh2, dyf0, y0, ffn0, pair_send=gs1)
    dcat = _mm_rows('l0_out_dx', dy0, W['w_out'], tmw, trans_b=True)
    d_w_out = _mm_wgrad('l0_out_dw', cat, dy0, tmw)

    gs0 = [blocked(g) for g in (d_wi0, d_wo0, d_w_out)]
    sib0 = _rs_pair_send('rs_pair_send', gs0)
    gs = [gs0[0], gs1[0], gs0[1], gs1[1], gs0[2], gs1[2], gs1[3]]
    sib = [sib0[0], sib1[0], sib0[1], sib1[1], sib0[2], sib1[2], sib1[3]]
    ss = [_rs_pair_add(f'rs_pair_add{t}', gs[t], sib[t], cvec) for t in range(len(gs))]
    dq, dk, dv, *xs = _attn_bwd('l0_attn_bwd', q_all, kh, vh, o, lse, dcat, n_ctx, tq, exchange=ss)
    dk, dv = [t.transpose(2, 0, 1).reshape(n_ctx + seq, KV_W) for t in (dk, dv)]
    dp_qkv, d_gq, d_gk = _qk_prep_bwd('l0_qk_prep_bwd', p_qkv, cos, sin, dq, n_ctx // tm, dk, dv, gq, gk, bdq, bdk,
                                      tm)
    dp_sg, d_w_sp, d_bsp = _sg_bwd('l0_sg_bwd', p_sg, dcat, w_sp, w_sp_t, bsp, tm)
    d_w_qkv = _mm_wgrad('l0_qkv_dw', xall, dp_qkv, tmr)
    d_w_sgp = _mm_wgrad('l0_sg_in_dw', xm0, dp_sg, tmw)
    dxall = _mm_rows('l0_qkv_dx', dp_qkv, W['w_qkv'], tmr, trans_b=True)
    g_in = blocked(jnp.concatenate([d_w_qkv, d_w_sgp], axis=1).reshape(D_MODEL, N_CHIP, -1).transpose(1, 0, 2))
    sib_in = _rs_pair_send('rs_pair_send_w_in', [g_in])
    ss_in = _rs_pair_add('rs_pair_add_w_in', g_in, sib_in[0], cvec)
    dx, d_sh1_0, d_sc1_0, d_gmix0, *xs_in = _mm_t_epilogue(
        'l0_sg_in_dx', dp_sg, W['w_sgp'], [(x, None), (dh1, None), (dxall, lambda i: i + n_ctx // tm)],
        [g_mix[0], mv(0, 1)], 3, _norm_bwd_epilogue, tm, exchange=[ss_in])
    _, d_csh1, d_csc1, d_gmix0c = _modnorm_bwd('l0_ctx_norm_bwd', ctx, [_ri(dxall)], None, g_mix[0], csc1, tmc)

    zero = jnp.zeros((1, D_MODEL), F32)
    d = D_MODEL
    pack = jnp.concatenate([
        d_gmix0 + d_gmix0c, d_gmix1, d_gffn0, d_gffn1, d_g_final,
        jnp.concatenate([d_gq, d_gk, jnp.zeros((1, d - ATTN_W - KV_W), F32)], axis=1),
        d_b_pw1.reshape(2, d), d_w_dw, d_b_dw, d_ln_g, d_ln_b, d_b_pw2,
        d_sh1_0, d_sc1_0, d_gt1_0, *dmod0_ffn, d_gt2_0, d_sh1_1, d_sc1_1, d_gt1_1, *dmod1_ffn, d_gt2_1,
        d_csh1, d_csc1, zero, zero, zero, zero, jnp.zeros((3, d), F32),
        d_w_sp.reshape(64, d),
        jnp.pad(d_bsp.reshape(CHUNK, N_SG, CHUNK).sum(axis=-1).reshape(1, CHUNK * N_SG), ((0, 7), (0, d - SG_W)))],
        axis=0)
    assert pack.shape == (SMALL_ROWS, d)

    t_ffn_in = _sum_into('rs_sum_ffn_in1', xs[1], cvec, 2, 1, prev=_sum_into('rs_sum_ffn_in0', xs[0], cvec, 2, 0))
    t_ffn_out = _sum_into('rs_sum_ffn_out1', xs[3], cvec, 2, 1, prev=_sum_into('rs_sum_ffn_out0', xs[2], cvec, 2, 0))
    *reduced, small = _rs_pair_share('rs_pair_share', [
        t_ffn_in, t_ffn_out, _sum_into('rs_sum_w_in', xs_in[0], cvec, 1, 0), _sum_into('rs_sum_w_out', xs[4], cvec, 1, 0),
        _sum_into('rs_sum_pw1', xs[5], cvec, 1, 0), _sum_into('rs_sum_pw2', xs[6], cvec, 1, 0)], gather=[pack])
    big = dict(zip(['w_ffn_in', 'w_ffn_out', 'w_in', 'w_out', 'w_pw1', 'w_pw2'], reduced))
    return loss_vec, dx, small, big


def kernel(x, c, ctx, c_ctx, w_mod, b_mod, g_mix, g_ffn, w_ffn_in, w_ffn_out, w_in, q_gain, k_gain, w_sp, b_sp, w_out, w_pw1, b_pw1, w_dw, b_dw, ln_g, ln_b, w_pw2, b_pw2, g_final, loss_target, m_c_ctx, m_w_mod, m_b_mod, m_g_mix, m_g_ffn, m_w_ffn_in, m_w_ffn_out, m_w_in, m_q_gain, m_k_gain, m_w_sp, m_b_sp, m_w_out, m_w_pw1, m_b_pw1, m_w_dw, m_b_dw, m_ln_g, m_ln_b, m_w_pw2, m_b_pw2, m_g_final, v_c_ctx, v_w_mod, v_b_mod, v_g_mix, v_g_ffn, v_w_ffn_in, v_w_ffn_out, v_w_in, v_q_gain, v_k_gain, v_w_sp, v_b_sp, v_w_out, v_w_pw1, v_b_pw1, v_w_dw, v_b_dw, v_ln_g, v_ln_b, v_w_pw2, v_b_pw2, v_g_final):
    given = dict(locals())
    ix, iy, ic = _mesh_pos()
    chip = 2 * ix + iy
    me = 2 * chip + ic
    d = D_MODEL
    q4 = d // N_CHIP

    small = jnp.concatenate([c.reshape(4, q4), b_pw1.reshape(2, q4), w_dw[0], b_dw, ln_g, ln_b, b_pw2,
                             jnp.zeros((7, q4), F32)], axis=0)
    def my_half(a):
        r = a.shape[0] // 2
        return lax.dynamic_slice_in_dim(a, ic * r, r, axis=0).astype(BF16)

    sgath, g_w_in = _allgather('ag_first', [small, my_half(w_in[0])])
    c_all = sgath[:, 0:4].reshape(N_DEV, d)
    per_chip = sgath[0::2]
    b_pw1_f = per_chip[:, 4:6].reshape(1, 2 * d)
    w_dw_f = per_chip[:, 6:6 + CONV_W].transpose(1, 0, 2).reshape(CONV_W, d)
    b_dw_f, ln_g_f, ln_b_f, b_pw2_f = [per_chip[:, 37 + i].reshape(1, d) for i in range(4)]

    c16 = jnp.concatenate([c_all, c_ctx[None, :], jnp.zeros((7, d), F32)], axis=0)
    bm_sh = lax.dynamic_slice_in_dim(b_mod, chip * MOD_BLK, MOD_BLK, axis=1).reshape(2, 1, MOD_BLK)
    mod_piece = _mod_fwd('mod_fwd', c16, w_mod, bm_sh)
    mgath = _allgather('ag_mod', [mod_piece.reshape(32, MOD_BLK)])[0]
    mod_all = mgath[0::2].reshape(N_CHIP, 2, 16, MOD_BLK).transpose(1, 2, 0, 3).reshape(2, 16, MOD_W)
    mod_me = lax.dynamic_index_in_dim(mod_all, me, axis=1, keepdims=False)
    cmod = mod_all[0, N_DEV]

    w_in_f = g_w_in.reshape(N_CHIP, d, -1).transpose(1, 0, 2).reshape(d, -1)
    W = dict(
        w_qkv=w_in_f[:, :QKV_W], w_sgp=w_in_f[:, QKV_W:],
        g_mix=g_mix, g_ffn=g_ffn, g_final=g_final.reshape(1, d), q_gain=q_gain, k_gain=k_gain, w_sp=w_sp[0],
        b_sp=b_sp[0], b_pw1=b_pw1_f, w_dw=w_dw_f, b_dw=b_dw_f, ln_g=ln_g_f, ln_b=ln_b_f, b_pw2=b_pw2_f)
    pending = [my_half(w_out[0]), my_half(w_ffn_in[0]), my_half(w_ffn_out[0]), my_half(w_pw1[0]),
               my_half(w_ffn_in[1]), my_half(w_pw2[0]), my_half(w_ffn_out[1])]
    cvec = jnp.reshape(ic, (1,)).astype(jnp.int32)

    loss_vec, dx, pg, big = _sample_step(x[0], ctx[0], loss_target[0], mod_me, cmod, W, pending, cvec)
    loss = lax.psum(jnp.sum(loss_vec), ("x", "y", "c"))

    s8 = _sum_slots('sum_small_grads', pg, SMALL_ROWS)
    grads = dict(big)

    def chip_cols(a, width):
        return lax.dynamic_slice_in_dim(a, chip * width, width, axis=a.ndim - 1)

    grads['g_mix'] = s8[0:2]
    grads['g_ffn'] = s8[2:4]
    grads['g_final'] = s8[4]
    grads['q_gain'] = s8[5, :ATTN_W].reshape(N_Q_HEADS, HEAD_DIM).sum(axis=0)[None, :]
    grads['k_gain'] = s8[5, ATTN_W:ATTN_W + KV_W].reshape(N_KV_HEADS, HEAD_DIM).sum(axis=0)[None, :]
    grads['b_pw1'] = chip_cols(s8[6:8].reshape(1, 2 * d), 2 * q4)
    grads['w_dw'] = chip_cols(s8[8:8 + CONV_W], q4)[None]
    grads['b_dw'] = chip_cols(s8[39:40], q4)
    grads['ln_g'] = chip_cols(s8[40:41], q4)
    grads['ln_b'] = chip_cols(s8[41:42], q4)
    grads['b_pw2'] = chip_cols(s8[42:43], q4)
    grads['w_sp'] = s8[64:128].reshape(1, N_SG, CHUNK, CHUNK)
    grads['b_sp'] = s8[128, :SG_W].reshape(CHUNK, N_SG).T[None]

    dm_rows = pg[:, 43:55].reshape(N_DEV, 2, MOD_W)
    ctx_row = s8[55:61].reshape(1, MOD_W)
    dm = jnp.stack([jnp.concatenate([dm_rows[:, 0], ctx_row, jnp.zeros((7, MOD_W), F32)], axis=0),
                    jnp.concatenate([dm_rows[:, 1], jnp.zeros((8, MOD_W), F32)], axis=0)], axis=0)
    grads['b_mod'] = jnp.stack([s8[43:49].reshape(MOD_W) + ctx_row[0], s8[49:55].reshape(MOD_W)], axis=0)
    dm_sh = chip_cols(dm, MOD_BLK)
    grads['w_mod'] = _mod_wgrad('mod_dw', c16, dm_sh)
    d_silu = _mm_rows('mod_dx', dm_sh[0], w_mod[0], 16, trans_b=True)
    parts = _allgather('ag_cctx', [d_silu[8:16]])[0]
    grads['c_ctx'] = _cctx_grad('cctx_grad', parts, c_ctx.reshape(1, d))[0]

    deltas, new_m, new_v = {}, {}, {}
    for n in WEIGHT_NAMES:
        grads[n] = grads[n].reshape(given[n].shape)
        deltas[n], new_m[n], new_v[n] = _adamw(f'adamw_{n}', given[n], grads[n], given['m_' + n], given['v_' + n])

    return (loss, dx[None], *[grads[n] for n in WEIGHT_NAMES], *[deltas[n] for n in WEIGHT_NAMES],
            *[new_m[n] for n in WEIGHT_NAMES], *[new_v[n] for n in WEIGHT_NAMES])
```

```python
import functools
import math

import jax
import jax.numpy as jnp
from jax import lax
from jax.experimental import pallas as pl
from jax.experimental.pallas import tpu as pltpu

F32 = jnp.float32
BF16 = jnp.bfloat16
MESH = pl.DeviceIdType.MESH

D_MODEL = 1024
HEAD_DIM = 64
N_Q_HEADS = 8
N_KV_HEADS = 2
Q_GROUP = N_Q_HEADS // N_KV_HEADS
ATTN_W = N_Q_HEADS * HEAD_DIM
KV_W = N_KV_HEADS * HEAD_DIM
QKV_W = ATTN_W + 2 * KV_W
SG_W = D_MODEL - ATTN_W
N_SG = 4
CHUNK = 128
GRID_W = 64
ROPE_THETA = 10000.0
CONV_W = 31
CONV_HALO = 16
D_FF = 2816
MOD_W = 6 * D_MODEL
N_CHIP = 4
N_DEV = 8
FF_BLK = 2 * D_FF // N_CHIP
PW1_BLK = 2 * D_MODEL // N_CHIP
MOD_BLK = MOD_W // N_CHIP
EPS = 1e-6
GELU_C = math.sqrt(2.0 / math.pi)
GELU_A = 0.044715

ADAM_LR = 0.001
ADAM_B1 = 0.9
ADAM_B2 = 0.999
ADAM_EPS = 1e-08
ADAM_WD = 0.01
ADAM_STEP = 10

SMALL_ROWS = 136

VMEM_LIMIT_BYTES = 56 * 1024 * 1024

WEIGHT_NAMES = ['c_ctx', 'w_mod', 'b_mod', 'g_mix', 'g_ffn', 'w_ffn_in', 'w_ffn_out', 'w_in', 'q_gain', 'k_gain',
                'w_sp', 'b_sp', 'w_out', 'w_pw1', 'b_pw1', 'w_dw', 'b_dw', 'ln_g', 'ln_b', 'w_pw2', 'b_pw2',
                'g_final']


def _params():
    return pltpu.CompilerParams(vmem_limit_bytes=VMEM_LIMIT_BYTES)


def _ri(arr, width=None, col_block=0, row_off=0, row_fn=None):
    return (arr, arr.shape[1] if width is None else width, col_block,
            (lambda i: i + row_off) if row_fn is None else row_fn)


def _rowwise(name, body, n_tiles, tm, row_ins, vec_ins, row_outs, vec_outs):
    nri, nvi, nro = len(row_ins), len(vec_ins), len(row_outs)

    def kern(*refs):
        ri = refs[:nri]
        vi = refs[nri:nri + nvi]
        ro = refs[nri + nvi:nri + nvi + nro]
        vo = refs[nri + nvi + nro:]
        if vo:
            @pl.when(pl.program_id(0) == 0)
            def _():
                for r in vo:
                    r[...] = jnp.zeros(r.shape, r.dtype)
        body(ri, vi, ro, vo)

    in_specs = [pl.BlockSpec((tm, bw), lambda i, cb=cb, rf=rf: (rf(i), cb)) for (_, bw, cb, rf) in row_ins]
    in_specs += [pl.BlockSpec(a.shape, lambda i, nd=a.ndim: (0,) * nd) for a in vec_ins]
    out_specs = [pl.BlockSpec((tm, w), lambda i: (i, 0)) for (w, _) in row_outs]
    out_specs += [pl.BlockSpec(s, lambda i, nd=len(s): (0,) * nd) for s in vec_outs]
    out_shape = [jax.ShapeDtypeStruct((n_tiles * tm, w), dt) for (w, dt) in row_outs]
    out_shape += [jax.ShapeDtypeStruct(s, F32) for s in vec_outs]
    return pl.pallas_call(kern, name=name, grid=(n_tiles,), in_specs=in_specs, out_specs=out_specs,
                          out_shape=out_shape, compiler_params=_params())(*[a for (a, _, _, _) in row_ins], *vec_ins)


def _colsum(v):
    return jnp.sum(v, axis=0, keepdims=True)


def _rowmean(v):
    return jnp.mean(v, axis=-1, keepdims=True)


def _rms(h):
    r = lax.rsqrt(_rowmean(h * h) + EPS)
    return h * r, r


def _rms_bwd(n, r, gy):
    return r * (gy - n * _rowmean(gy * n))


def _ln_stats(v):
    xc = v - _rowmean(v)
    r = lax.rsqrt(_rowmean(xc * xc) + EPS)
    return xc * r, r


def _ln_bwd(xh, r, dxh):
    return r * (dxh - _rowmean(dxh) - xh * _rowmean(dxh * xh))


def _sigmoid(v):
    return 1.0 / (1.0 + jnp.exp(-v))


def _gelu(v):
    t = jnp.tanh(GELU_C * (v + GELU_A * (v * v * v)))
    return 0.5 * v * (1.0 + t), t


def _gelu_grad(v, t):
    return 0.5 * (1.0 + t) + 0.5 * v * (1.0 - t * t) * (GELU_C * (1.0 + 3.0 * GELU_A * (v * v)))


def _modnorm_fwd(name, h, g, sc, sh, tm):
    def body(ri, vi, ro, vo):
        n, _ = _rms(ri[0][...])
        ro[0][...] = ((n * vi[0][...]) * (1.0 + vi[1][...]) + vi[2][...]).astype(BF16)

    return _rowwise(name, body, h.shape[0] // tm, tm, [_ri(h)], [g, sc, sh], [(D_MODEL, BF16)], [])[0]


def _modnorm_bwd_tile(hv, dxm, gv, scv):
    n, r = _rms(hv)
    dy = dxm * (1.0 + scv)
    return _rms_bwd(n, r, dy * gv), _colsum(dxm), _colsum(dxm * (n * gv)), _colsum(dy * n)


def _mm_row_epilogue(name, mm_args, mm_specs, dx_fn, rows_in, vecs_in, n_vec_out, epilogue, tm, gate=None,
                     exchange=()):
    rows_in = list(rows_in) + ([(gate[0], None)] if gate else [])
    vecs_in = list(vecs_in) + ([gate[1]] if gate else [])
    ng = 1 if gate else 0
    n_vec_out += 2 * ng
    nmm, nr, nv, ne = len(mm_args), len(rows_in), len(vecs_in), len(exchange)
    m = rows_in[0][0].shape[0]
    last = m // tm - 1

    def kern(*refs):
        ins = refs[:nmm + nr + nv]
        o_ref = refs[nmm + nr + nv + ne]
        vo = refs[nmm + nr + nv + ne + 1 + ng:nmm + nr + nv + ne + 1 + ng + n_vec_out]
        if ne:
            srcs = refs[nmm + nr + nv:nmm + nr + nv + ne]
            rest = refs[nmm + nr + nv + ne + 1 + ng + n_vec_out:]
            start, finish = _exchange_phases(srcs, rest[:ne], *rest[ne:])
            pl.when(pl.program_id(0) == 0)(start)

        @pl.when(pl.program_id(0) == 0)
        def _():
            for r in vo:
                r[...] = jnp.zeros(r.shape, F32)

        row_out, incs = epilogue(dx_fn(*ins[:nmm]), [r[...] for r in ins[nmm:nmm + nr - ng]],
                                 [r[...] for r in ins[nmm + nr:nmm + nr + nv - ng]])
        o_ref[...] = row_out
        if ng:
            dy = row_out * ins[nmm + nr + nv - 1][...]
            refs[nmm + nr + nv + ne + 1][...] = dy.astype(BF16)
            incs = list(incs) + [_colsum(row_out * ins[nmm + nr - 1][...]), _colsum(dy)]
        for r, inc in zip(vo, incs):
            r[...] += inc
        if ne:
            pl.when(pl.program_id(0) == last)(finish)

    row = pl.BlockSpec((tm, D_MODEL), lambda i: (i, 0))
    vec = pl.BlockSpec((1, D_MODEL), lambda i: (0, 0))
    row_specs = [row if fn is None else pl.BlockSpec((tm, D_MODEL), lambda i, fn=fn: (fn(i), 0)) for _, fn in rows_in]
    return pl.pallas_call(
        kern, name=name, grid=(m // tm,), in_specs=list(mm_specs) + row_specs + [vec] * nv + [_ANY] * ne,
        out_specs=[row] * (1 + ng) + [vec] * n_vec_out + [_ANY] * ne,
        out_shape=[jax.ShapeDtypeStruct((m, D_MODEL), F32)] + [jax.ShapeDtypeStruct((m, D_MODEL), BF16)] * ng
        + [jax.ShapeDtypeStruct((1, D_MODEL), F32)] * n_vec_out
        + [jax.ShapeDtypeStruct(e.shape, e.dtype) for e in exchange],
        scratch_shapes=_exchange_scratch(ne) if ne else [],
        compiler_params=_params())(*mm_args, *[r for r, _ in rows_in], *vecs_in, *exchange)


def _norm_bwd_epilogue(dxm, rows, vecs):
    if len(rows) > 2:
        dxm = dxm + rows[2]
    dx, dsh, dsc, dg = _modnorm_bwd_tile(rows[0], dxm, vecs[0], vecs[1])
    return rows[1] + dx, [dsh, dsc, dg]


def _lnsilu_bwd_epilogue(ds, rows, vecs):
    xh, r = _ln_stats(rows[0])
    ln = xh * vecs[0] + vecs[1]
    sg = _sigmoid(ln)
    dln = ds * (sg * (1.0 + ln * (1.0 - sg)))
    return _ln_bwd(xh, r, dln * vecs[0]), [_colsum(dln * xh), _colsum(dln)]


def _modnorm_bwd(name, h, dxs, dh_in, g, sc, tm):
    ndx = len(dxs)

    def body(ri, vi, ro, vo):
        dxm = ri[1][...].astype(F32)
        for r in ri[2:1 + ndx]:
            dxm = dxm + r[...].astype(F32)
        dx, dsh, dsc, dg = _modnorm_bwd_tile(ri[0][...], dxm, vi[0][...], vi[1][...])
        vo[0][...] += dsh
        vo[1][...] += dsc
        vo[2][...] += dg
        if dh_in is not None:
            ro[0][...] = ri[1 + ndx][...] + dx

    row_ins = [_ri(h)] + list(dxs) + ([_ri(dh_in)] if dh_in is not None else [])
    row_outs = [(D_MODEL, F32)] if dh_in is not None else []
    outs = _rowwise(name, body, h.shape[0] // tm, tm, row_ins, [g, sc], row_outs, [(1, D_MODEL)] * 3)
    if dh_in is None:
        return (None, *outs)
    return tuple(outs)


def _head_mean(v, bd):
    hi = v.astype(BF16)
    lo = (v - hi.astype(F32)).astype(BF16)
    s = jnp.dot(hi, bd, preferred_element_type=F32) + jnp.dot(lo, bd, preferred_element_type=F32)
    return s * (1.0 / HEAD_DIM)


def _swap16(v):
    w = v.shape[1]
    lane = lax.broadcasted_iota(jnp.int32, v.shape, 1)
    return jnp.where((lane & 16) == 0, pltpu.roll(v, w - 16, 1), pltpu.roll(v, 16, 1))


def _rope(v, cos, sin):
    return v * cos + _swap16(v) * sin


def _rope_bwd(d, cos, sin):
    return d * cos + _swap16(d * sin)


def _q_wide(t):
    return jnp.concatenate([t] * (ATTN_W // KV_W), axis=1)


def _mm_qk_prep(name, xall, w_qkv, cos, sin, gq, gk, bdq, bdk, tm):
    r, k = xall.shape

    def kern(x_ref, w_ref, cos_ref, sin_ref, gq_ref, gk_ref, bdq_ref, bdk_ref, p_ref, q_ref, k_ref, v_ref):
        p = jnp.dot(x_ref[...], w_ref[...], preferred_element_type=F32)
        p_ref[...] = p
        cosv = cos_ref[...]
        sinv = sin_ref[...]
        pq = p[:, :ATTN_W]
        rq = lax.rsqrt(_head_mean(pq * pq, bdq_ref[...]) + EPS)
        q_ref[...] = (_rope(pq * rq * gq_ref[...], _q_wide(cosv), _q_wide(sinv)) * (HEAD_DIM ** -0.5)).astype(BF16)
        pk = p[:, ATTN_W:ATTN_W + KV_W]
        rk = lax.rsqrt(_head_mean(pk * pk, bdk_ref[...]) + EPS)
        k_ref[...] = _rope(pk * rk * gk_ref[...], cosv, sinv).astype(BF16)
        v_ref[...] = p[:, ATTN_W + KV_W:].astype(BF16)

    def rows(w):
        return pl.BlockSpec((tm, w), lambda i: (i, 0))

    def whole(a):
        return pl.BlockSpec(a.shape, lambda i: (0, 0))

    return pl.pallas_call(
        kern, name=name, grid=(r // tm,),
        in_specs=[rows(k), whole(w_qkv), rows(KV_W), rows(KV_W), whole(gq), whole(gk), whole(bdq), whole(bdk)],
        out_specs=[rows(QKV_W), rows(ATTN_W), rows(KV_W), rows(KV_W)],
        out_shape=[jax.ShapeDtypeStruct((r, QKV_W), F32), jax.ShapeDtypeStruct((r, ATTN_W), BF16),
                   jax.ShapeDtypeStruct((r, KV_W), BF16), jax.ShapeDtypeStruct((r, KV_W), BF16)],
        compiler_params=_params())(xall, w_qkv, cos, sin, gq, gk, bdq, bdk)


def _qk_prep_bwd(name, p_qkv, cos, sin, dq, n_ctx_tiles, dk, dv, gq, gk, bdq, bdk, tm):
    def norm_rope_bwd(p, gain, bd, cosv, sinv, dout):
        r = lax.rsqrt(_head_mean(p * p, bd) + EPS)
        n = p * r
        dqn = _rope_bwd(dout, cosv, sinv)
        gy = dqn * gain
        return r * (gy - n * _head_mean(gy * n, bd)), _colsum(dqn * n)

    def body(ri, vi, ro, vo):
        pkv = ri[1][...]
        cosv = ri[2][...]
        sinv = ri[3][...]
        dqv = jnp.where(pl.program_id(0) >= n_ctx_tiles, ri[4][...] * (HEAD_DIM ** -0.5), 0.0)
        dxq, dgq = norm_rope_bwd(ri[0][...], vi[0][...], vi[2][...], _q_wide(cosv), _q_wide(sinv), dqv)
        dxk, dgk = norm_rope_bwd(pkv[:, :KV_W], vi[1][...], vi[3][...], cosv, sinv, ri[5][...])
        ro[0][:, :ATTN_W] = dxq.astype(BF16)
        ro[0][:, ATTN_W:ATTN_W + KV_W] = dxk.astype(BF16)
        ro[0][:, ATTN_W + KV_W:] = ri[6][...].astype(BF16)
        vo[0][...] += dgq
        vo[1][...] += dgk

    return _rowwise(name, body, p_qkv.shape[0] // tm, tm,
                    [_ri(p_qkv, ATTN_W, 0), _ri(p_qkv, 2 * KV_W, 2), _ri(cos), _ri(sin),
                     _ri(dq, row_fn=lambda i: jnp.maximum(i - n_ctx_tiles, 0)), _ri(dk), _ri(dv)],
                    [gq, gk, bdq, bdk], [(QKV_W, BF16)], [(1, ATTN_W), (1, KV_W)])


def _mm_sg(name, xm, w_sgp, w_sp, bsp, tm):
    m, k = xm.shape

    def kern(x_ref, w_ref, wsp_ref, bsp_ref, p_ref, o_ref):
        p = jnp.dot(x_ref[...], w_ref[...], preferred_element_type=F32)
        p_ref[...] = p
        gu, _ = _gelu(p[:, :SG_W])
        gv, _ = _gelu(p[:, SG_W:])
        for g in range(N_SG):
            sl = slice(g * CHUNK, (g + 1) * CHUNK)
            vn, _ = _ln_stats(gv[:, sl])
            vnb = vn.astype(BF16)
            for c in range(tm // CHUNK):
                rs = slice(c * CHUNK, (c + 1) * CHUNK)
                mixed = jnp.dot(wsp_ref[g], vnb[rs], preferred_element_type=F32) + bsp_ref[:, sl]
                o_ref[rs, sl] = (gu[rs, sl] * mixed).astype(BF16)

    return pl.pallas_call(
        kern, name=name, grid=(m // tm,),
        in_specs=[pl.BlockSpec((tm, k), lambda i: (i, 0)), pl.BlockSpec(w_sgp.shape, lambda i: (0, 0)),
                  pl.BlockSpec(w_sp.shape, lambda i: (0, 0, 0)), pl.BlockSpec(bsp.shape, lambda i: (0, 0))],
        out_specs=[pl.BlockSpec((tm, 2 * SG_W), lambda i: (i, 0)), pl.BlockSpec((tm, SG_W), lambda i: (i, 0))],
        out_shape=[jax.ShapeDtypeStruct((m, 2 * SG_W), F32), jax.ShapeDtypeStruct((m, SG_W), BF16)],
        compiler_params=_params())(xm, w_sgp, w_sp, bsp)


def _sg_bwd(name, p_sg, dcat, w_sp, w_sp_t, bsp, tm):
    def body(ri, vi, ro, vo):
        p = ri[0][...]
        dsg = ri[1][...]
        su = p[:, :SG_W]
        sv = p[:, SG_W:]
        gu, tu = _gelu(su)
        gv, tv = _gelu(sv)
        for g in range(N_SG):
            sl = slice(g * CHUNK, (g + 1) * CHUNK)
            vn, r = _ln_stats(gv[:, sl])
            vnb = vn.astype(BF16)
            for m in range(tm // CHUNK):
                rs = slice(m * CHUNK, (m + 1) * CHUNK)
                mixed = jnp.dot(vi[0][g], vnb[rs], preferred_element_type=F32) + vi[2][:, sl]
                d_o = dsg[rs, sl]
                dm = d_o * gu[rs, sl]
                vo[1][:, sl] += dm
                dmb = dm.astype(BF16)
                vo[0][g] += lax.dot_general(dmb, vnb[rs], (((1,), (1,)), ((), ())), preferred_element_type=F32)
                dvn = jnp.dot(vi[1][g], dmb, preferred_element_type=F32)
                dgv = _ln_bwd(vn[rs], r[rs], dvn)
                ro[0][rs, sl] = (d_o * mixed * _gelu_grad(su[rs, sl], tu[rs, sl])).astype(BF16)
                ro[0][rs, SG_W + g * CHUNK:SG_W + (g + 1) * CHUNK] = (
                    dgv * _gelu_grad(sv[rs, sl], tv[rs, sl])).astype(BF16)

    return _rowwise(name, body, p_sg.shape[0] // tm, tm, [_ri(p_sg), _ri(dcat, SG_W, 1)], [w_sp, w_sp_t, bsp],
                    [(2 * SG_W, BF16)], [(N_SG, CHUNK, CHUNK), (CHUNK, SG_W)])


def _glu_bwd(name, ag, dhg, b, tm):
    def body(ri, vi, ro, vo):
        v = ri[0][...] + vi[0][...]
        d = ri[1][...]
        a = v[:, :D_MODEL]
        sg = _sigmoid(v[:, D_MODEL:])
        da = d * sg
        dgate = d * a * sg * (1.0 - sg)
        ro[0][:, :D_MODEL] = da.astype(BF16)
        ro[0][:, D_MODEL:] = dgate.astype(BF16)
        vo[0][:, :D_MODEL] += _colsum(da)
        vo[0][:, D_MODEL:] += _colsum(dgate)

    return _rowwise(name, body, ag.shape[0] // tm, tm, [_ri(ag), _ri(dhg)], [b], [(2 * D_MODEL, BF16)],
                    [(1, 2 * D_MODEL)])


def _lnsilu_fwd(name, hc, g, b, tm):
    def body(ri, vi, ro, vo):
        xh, _ = _ln_stats(ri[0][...])
        ln = xh * vi[0][...] + vi[1][...]
        ro[0][...] = (ln * _sigmoid(ln)).astype(BF16)

    return _rowwise(name, body, hc.shape[0] // tm, tm, [_ri(hc)], [g, b], [(D_MODEL, BF16)], [])[0]


def _conv_blocks(seq):
    cb = 128
    tt = 128 if seq % 128 == 0 else seq
    return cb, tt


def _conv_taps(win, tt):
    n = win.shape[0]
    for s in range(8):
        ws = win if s == 0 else pltpu.roll(win, n - s, 0)
        for q in range(4):
            j = 8 * q + s - 1
            if 0 <= j < CONV_W:
                yield j, ws[8 * q:8 * q + tt, :]


def _fill_padded(pad_ref, x_ref, seq):
    zeros = jnp.zeros((CONV_HALO, pad_ref.shape[1]), F32)
    pad_ref[0:CONV_HALO, :] = zeros
    pad_ref[seq + CONV_HALO:seq + 2 * CONV_HALO, :] = zeros
    pad_ref[CONV_HALO:seq + CONV_HALO, :] = x_ref[...]


def _dwconv(name, xin, w, b):
    seq = xin.shape[0]
    cb, tt = _conv_blocks(seq)

    def kern(x_ref, w_ref, b_ref, o_ref, pad_ref):
        _fill_padded(pad_ref, x_ref, seq)
        wv = w_ref[...]
        bv = b_ref[...]

        def step(t, carry):
            base = pl.multiple_of(t * tt, tt)
            win = pad_ref[pl.ds(base, tt + 2 * CONV_HALO), :]
            acc = jnp.zeros((tt, cb), F32) + bv
            for j, rows in _conv_taps(win, tt):
                acc = acc + wv[j:j + 1, :] * rows
            o_ref[pl.ds(base, tt), :] = acc
            return carry

        lax.fori_loop(0, seq // tt, step, 0)

    return pl.pallas_call(
        kern, name=name, grid=(D_MODEL // cb,),
        in_specs=[pl.BlockSpec((seq, cb), lambda j: (0, j)), pl.BlockSpec((CONV_W, cb), lambda j: (0, j)),
                  pl.BlockSpec((1, cb), lambda j: (0, j))],
        out_specs=pl.BlockSpec((seq, cb), lambda j: (0, j)),
        out_shape=jax.ShapeDtypeStruct((seq, D_MODEL), F32),
        scratch_shapes=[pltpu.VMEM((seq + 2 * CONV_HALO, cb), F32)],
        compiler_params=_params())(xin, w, b)


def _dwconv_wgrad(name, xin, dout):
    seq = xin.shape[0]
    cb, tt = _conv_blocks(seq)

    def kern(x_ref, d_ref, dw_ref, db_ref, pad_ref):
        _fill_padded(pad_ref, x_ref, seq)
        dw_ref[...] = jnp.zeros(dw_ref.shape, F32)
        db_ref[...] = jnp.zeros(db_ref.shape, F32)

        def step(t, carry):
            base = pl.multiple_of(t * tt, tt)
            win = pad_ref[pl.ds(base, tt + 2 * CONV_HALO), :]
            d = d_ref[pl.ds(base, tt), :]
            db_ref[...] += _colsum(d)
            for j, rows in _conv_taps(win, tt):
                dw_ref[j:j + 1, :] += _colsum(d * rows)
            return carry

        lax.fori_loop(0, seq // tt, step, 0)

    return pl.pallas_call(
        kern, name=name, grid=(D_MODEL // cb,),
        in_specs=[pl.BlockSpec((seq, cb), lambda j: (0, j)), pl.BlockSpec((seq, cb), lambda j: (0, j))],
        out_specs=[pl.BlockSpec((CONV_W, cb), lambda j: (0, j)), pl.BlockSpec((1, cb), lambda j: (0, j))],
        out_shape=[jax.ShapeDtypeStruct((CONV_W, D_MODEL), F32), jax.ShapeDtypeStruct((1, D_MODEL), F32)],
        scratch_shapes=[pltpu.VMEM((seq + 2 * CONV_HALO, cb), F32)],
        compiler_params=_params())(xin, dout)


def _mm(name, a, b, out_sds, grid, a_spec, b_spec, o_spec, contract, k_axis=None, bias=None, pair_send=()):
    dn = (contract, ((), ()))
    ns = len(pair_send)
    nb = 0 if bias is None else 1
    n_steps = math.prod(grid)

    def kern(*refs):
        a_ref, b_ref = refs[:2]
        bias_ref = refs[2] if nb else None
        o_ref = refs[2 + nb + ns]
        if ns:
            copies = _pair_send_copies(refs[2 + nb:2 + nb + ns], refs[3 + nb + ns:3 + nb + 2 * ns],
                                       *refs[3 + nb + 2 * ns:])
            step = pl.program_id(0)
            for ax in range(1, len(grid)):
                step = step * grid[ax] + pl.program_id(ax)

            @pl.when(step == 0)
            def _():
                for cp in copies():
                    cp.start()

        p = lax.dot_general(a_ref[...].astype(BF16), b_ref[...].astype(BF16), dn, preferred_element_type=F32)
        if bias is not None:
            p = p + bias_ref[...]
        if k_axis is None:
            o_ref[...] = p.astype(o_ref.dtype)
        else:
            k = pl.program_id(k_axis)

            @pl.when(k == 0)
            def _():
                o_ref[...] = p

            @pl.when(k > 0)
            def _():
                o_ref[...] += p

        if ns:
            @pl.when(step == n_steps - 1)
            def _():
                for cp in copies():
                    cp.wait()

    in_specs = [a_spec, b_spec]
    args = [a, b]
    if bias is not None:
        in_specs.append(pl.BlockSpec(bias.shape, lambda *_: (0,) * bias.ndim))
        args.append(bias)
    if not ns:
        return pl.pallas_call(kern, name=name, grid=grid, in_specs=in_specs, out_specs=o_spec, out_shape=out_sds,
                              compiler_params=_params())(*args)
    return pl.pallas_call(
        kern, name=name, grid=grid, in_specs=in_specs + [_ANY] * ns, out_specs=[o_spec] + [_ANY] * ns,
        out_shape=[out_sds] + [jax.ShapeDtypeStruct((N_CHIP,) + g.shape[2:], g.dtype) for g in pair_send],
        scratch_shapes=[pltpu.SemaphoreType.DMA((ns, N_CHIP)), pltpu.SemaphoreType.DMA((ns, N_CHIP))],
        compiler_params=_params())(*args, *pair_send)


def _mm_rows(name, a, b, tm, trans_b=False, bias=None):
    m, k = a.shape
    n = b.shape[0] if trans_b else b.shape[1]
    contract = ((1,), (1,)) if trans_b else ((1,), (0,))
    return _mm(name, a, b, jax.ShapeDtypeStruct((m, n), F32), (m // tm,),
               pl.BlockSpec((tm, k), lambda i: (i, 0)), pl.BlockSpec(b.shape, lambda i: (0, 0)),
               pl.BlockSpec((tm, n), lambda i: (i, 0)), contract, bias=bias)


def _mm_glu(name, a, b, bias, tm):
    m, k = a.shape
    n = b.shape[1]

    def kern(a_ref, b_ref, bias_ref, ag_ref, hg_ref):
        v = jnp.dot(a_ref[...], b_ref[...], preferred_element_type=F32)
        ag_ref[...] = v
        v = v + bias_ref[...]
        hg_ref[...] = v[:, :n // 2] * _sigmoid(v[:, n // 2:])

    return pl.pallas_call(
        kern, name=name, grid=(m // tm,),
        in_specs=[pl.BlockSpec((tm, k), lambda i: (i, 0)), pl.BlockSpec(b.shape, lambda i: (0, 0)),
                  pl.BlockSpec((1, n), lambda i: (0, 0))],
        out_specs=[pl.BlockSpec((tm, n), lambda i: (i, 0)), pl.BlockSpec((tm, n // 2), lambda i: (i, 0))],
        out_shape=[jax.ShapeDtypeStruct((m, n), F32), jax.ShapeDtypeStruct((m, n // 2), F32)],
        compiler_params=_params())(a, b, bias)


def _mm_rows_res(name, a, b, h, gt, tm, bias=None, norm=None, final=None):
    m, k = a.shape
    n = b.shape[1]
    nb = 0 if bias is None else 1
    extra = list(norm or ()) + list(final or ())

    def kern(a_ref, b_ref, h_ref, gt_ref, *rest):
        y = jnp.dot(a_ref[...], b_ref[...], preferred_element_type=F32)
        if nb:
            y = y + rest[0][...]
        ex = rest[nb:nb + len(extra)]
        outs = rest[nb + len(extra):]
        hn = h_ref[...] + gt_ref[...] * y
        if final is None:
            outs[0][...] = y
            outs[1][...] = hn
        if norm is not None:
            nn, _ = _rms(hn)
            outs[2][...] = ((nn * ex[0][...]) * (1.0 + ex[1][...]) + ex[2][...]).astype(BF16)
        if final is not None:
            @pl.when(pl.program_id(0) == 0)
            def _():
                for o in (outs[1], outs[2], outs[4]):
                    o[...] = jnp.zeros(o.shape, F32)

            gv = ex[1][...]
            nn, r = _rms(hn)
            e = nn * gv - ex[0][...]
            outs[2][...] += _colsum(e * e) * (0.5 / D_MODEL)
            dout = e * (1.0 / D_MODEL)
            outs[1][...] += _colsum(dout * nn)
            dh = _rms_bwd(nn, r, dout * gv)
            outs[0][...] = dh
            outs[3][...] = (dh * gt_ref[...]).astype(BF16)
            outs[4][...] += _colsum(dh * y)

    row = pl.BlockSpec((tm, n), lambda i: (i, 0))
    vec = pl.BlockSpec((1, n), lambda i: (0, 0))
    sds = jax.ShapeDtypeStruct((m, n), F32)
    vsd = jax.ShapeDtypeStruct((1, n), F32)
    in_specs = [pl.BlockSpec((tm, k), lambda i: (i, 0)), pl.BlockSpec(b.shape, lambda i: (0, 0)), row, vec] + [vec] * nb
    out_specs, out_shape = [row, row], [sds, sds]
    if norm is not None:
        in_specs += [vec] * 3
        out_specs.append(row)
        out_shape.append(jax.ShapeDtypeStruct((m, n), BF16))
    if final is not None:
        in_specs += [row, vec]
        out_specs = [row, vec, vec, row, vec]
        out_shape = [sds, vsd, vsd, jax.ShapeDtypeStruct((m, n), BF16), vsd]
    return pl.pallas_call(kern, name=name, grid=(m // tm,), in_specs=in_specs, out_specs=out_specs,
                          out_shape=out_shape, compiler_params=_params())(
        a, b, h, gt, *([bias] if nb else []), *extra)


def _ffn_in_swiglu(name, xf, wblk, tm, gather=()):
    m, k = xf.shape
    nblk = wblk.shape[2]
    ng = len(gather)
    n_i = m // tm
    last = 2 * n_i - 1

    def kern(x_ref, wg_ref, wu_ref, *rest):
        gu_ref, a_ref = rest[ng:ng + 2]
        if ng:
            start, forward, finish = _gather_phases(rest[:ng], rest[ng + 2:2 * ng + 2], *rest[2 * ng + 2:])
            step = pl.program_id(0) * n_i + pl.program_id(1)
            pl.when(step == 0)(start)
            pl.when(step == (3 * last) // 4)(forward)
        xv = x_ref[...]
        g = jnp.dot(xv, wg_ref[...], preferred_element_type=F32)
        u = jnp.dot(xv, wu_ref[...], preferred_element_type=F32)
        gu_ref[0] = g.astype(BF16)
        gu_ref[1] = u.astype(BF16)
        a_ref[...] = (g * _sigmoid(g) * u).astype(BF16)
        if ng:
            pl.when(step == last)(finish)

    return pl.pallas_call(
        kern, name=name, grid=(2, n_i),
        in_specs=[pl.BlockSpec((tm, k), lambda j, i: (i, 0)), pl.BlockSpec((None, k, nblk), lambda j, i: (j, 0, 0)),
                  pl.BlockSpec((None, k, nblk), lambda j, i: (j + 2, 0, 0))] + [_ANY] * ng,
        out_specs=[pl.BlockSpec((2, tm, nblk), lambda j, i: (0, i, j)),
                   pl.BlockSpec((tm, nblk), lambda j, i: (i, j))] + [_ANY] * ng,
        out_shape=[jax.ShapeDtypeStruct((2, m, 2 * nblk), BF16), jax.ShapeDtypeStruct((m, 2 * nblk), BF16)]
        + [jax.ShapeDtypeStruct((N_DEV,) + g.shape, g.dtype) for g in gather],
        scratch_shapes=_gather_scratch(ng) if ng else [],
        compiler_params=_params())(xf, wblk, wblk, *gather)


def _ffn_out_dx_swiglu(name, dy, wo, gu, tm):
    m, k = dy.shape
    nblk = gu.shape[2] // 2

    def kern(dy_ref, w_ref, gu_ref, o_ref):
        da = lax.dot_general(dy_ref[...], w_ref[...], (((1,), (1,)), ((), ())), preferred_element_type=F32)
        g = gu_ref[0].astype(F32)
        sg = _sigmoid(g)
        o_ref[0] = (da * gu_ref[1].astype(F32) * (sg * (1.0 + g * (1.0 - sg)))).astype(BF16)
        o_ref[1] = (da * (g * sg)).astype(BF16)

    return pl.pallas_call(
        kern, name=name, grid=(2, m // tm),
        in_specs=[pl.BlockSpec((tm, k), lambda j, i: (i, 0)), pl.BlockSpec((nblk, k), lambda j, i: (j, 0)),
                  pl.BlockSpec((2, tm, nblk), lambda j, i: (0, i, j))],
        out_specs=pl.BlockSpec((2, tm, nblk), lambda j, i: (0, i, j)),
        out_shape=jax.ShapeDtypeStruct(gu.shape, BF16), compiler_params=_params())(dy, wo, gu)


def _ffn_in_dx_norm_bwd(name, dgu, wblk, h, dh_in, g, sc, tm, gate=None):
    nb, k, nblk = wblk.shape

    def dx_fn(a_ref, w_ref):
        acc = None
        for j in range(nb):
            p = lax.dot_general(a_ref[j // 2, :, (j % 2) * nblk:(j % 2 + 1) * nblk], w_ref[j],
                                (((1,), (1,)), ((), ())), preferred_element_type=F32)
            acc = p if acc is None else acc + p
        return acc

    return _mm_row_epilogue(name, [dgu, wblk],
                            [pl.BlockSpec((2, tm, 2 * nblk), lambda i: (0, i, 0)),
                             pl.BlockSpec(wblk.shape, lambda i: (0, 0, 0))], dx_fn, [(h, None), (dh_in, None)],
                            [g, sc], 3, _norm_bwd_epilogue, tm, gate=gate)


def _mm_t_epilogue(name, a, b, rows_in, vecs_in, n_vec_out, epilogue, tm, gate=None, exchange=()):
    def dx_fn(a_ref, b_ref):
        return lax.dot_general(a_ref[...], b_ref[...], (((1,), (1,)), ((), ())), preferred_element_type=F32)

    return _mm_row_epilogue(name, [a, b], [pl.BlockSpec((tm, a.shape[1]), lambda i: (i, 0)),
                                          pl.BlockSpec(b.shape, lambda i: (0, 0))], dx_fn, rows_in, vecs_in,
                            n_vec_out, epilogue, tm, gate, exchange)


def _mm_halves_wgrad_blk(name, a, b2, tk, pair_send=()):
    t, k = a.shape
    nblk = b2.shape[2] // 2
    return _mm(name, a, b2, jax.ShapeDtypeStruct((N_CHIP, k, nblk), F32), (N_CHIP, t // tk),
               pl.BlockSpec((tk, k), lambda j, i: (i, 0)),
               pl.BlockSpec((None, tk, nblk), lambda j, i: (j // 2, i, j % 2)),
               pl.BlockSpec((None, k, nblk), lambda j, i: (j, 0, 0)), ((0,), (0,)), k_axis=1, pair_send=pair_send)


def _mm_wgrad(name, a, b, tk):
    t, k = a.shape
    n = b.shape[1]
    tn = n if k * n * 4 <= 4 * 1024 * 1024 else 512
    return _mm(name, a, b, jax.ShapeDtypeStruct((k, n), F32), (n // tn, t // tk),
               pl.BlockSpec((tk, k), lambda j, i: (i, 0)), pl.BlockSpec((tk, tn), lambda j, i: (i, j)),
               pl.BlockSpec((k, tn), lambda j, i: (0, j)), ((0,), (0,)), k_axis=1)


def _attn_fwd(name, q_all, k, v, n_ctx, tq, gather=()):
    r = k.shape[1]
    s = q_all.shape[0] - n_ctx
    gw = Q_GROUP * HEAD_DIM
    ng = len(gather)
    n_i = s // tq
    last = N_KV_HEADS * n_i - 1
    forward_at = (7 * last) // 8

    def kern(q_ref, k_ref, v_ref, *rest):
        o_ref, lse_ref = rest[ng], rest[ng + 1]
        if ng:
            start, forward, finish = _gather_phases(rest[:ng], rest[ng + 2:2 * ng + 2], *rest[2 * ng + 2:])
            step = pl.program_id(0) * n_i + pl.program_id(1)
            pl.when(step == 0)(start)
            pl.when(step == forward_at)(forward)
        kv = k_ref[...]
        vv = v_ref[...]
        for g in range(Q_GROUP):
            sl = slice(g * HEAD_DIM, (g + 1) * HEAD_DIM)
            sc = lax.dot_general(q_ref[:, sl], kv, (((1,), (1,)), ((), ())), preferred_element_type=F32)
            m = jnp.max(sc, axis=-1, keepdims=True)
            p = jnp.exp((sc - m).astype(BF16))
            ol = jnp.dot(p, vv, preferred_element_type=F32)
            l = ol[:, HEAD_DIM:HEAD_DIM + 1]
            o_ref[:, sl] = ol[:, :HEAD_DIM] / l
            lse_ref[g] = m + jnp.log(l)
        if ng:
            pl.when(step == last)(finish)

    return pl.pallas_call(
        kern, name=name, grid=(N_KV_HEADS, n_i),
        in_specs=[pl.BlockSpec((tq, gw), lambda j, i: (i + n_ctx // tq, j)),
                  pl.BlockSpec((None, r, HEAD_DIM), lambda j, i: (j, 0, 0)),
                  pl.BlockSpec((None, r, 2 * HEAD_DIM), lambda j, i: (j, 0, 0))] + [_ANY] * ng,
        out_specs=[pl.BlockSpec((tq, gw), lambda j, i: (i, j)),
                   pl.BlockSpec((Q_GROUP, tq, 1), lambda j, i: (j, i, 0))] + [_ANY] * ng,
        out_shape=[jax.ShapeDtypeStruct((s, ATTN_W), F32), jax.ShapeDtypeStruct((N_Q_HEADS, s, 1), F32)]
        + [jax.ShapeDtypeStruct((N_DEV,) + a.shape, a.dtype) for a in gather],
        scratch_shapes=_gather_scratch(ng) if ng else [],
        compiler_params=_params())(q_all, k, v, *gather)


def _attn_bwd(name, q_all, k, v, o, lse, dcat, n_ctx, tq, exchange=()):
    nkv, r, _ = k.shape
    s = o.shape[0]
    gw = Q_GROUP * HEAD_DIM
    ne = len(exchange)
    n_i = s // tq
    last = nkv * n_i - 1

    def kern(q_ref, k_ref, v_ref, o_ref, lse_ref, do_ref, *rest):
        dq_ref, dk_ref, dv_ref = rest[ne:ne + 3]
        ds_all, p_all = rest[2 * ne + 3:2 * ne + 5]
        if ne:
            start, finish = _exchange_phases(rest[:ne], rest[ne + 3:2 * ne + 3], *rest[2 * ne + 5:])
            step = pl.program_id(0) * n_i + pl.program_id(1)
            pl.when(step == 0)(start)

        @pl.when(pl.program_id(1) == 0)
        def _():
            dk_ref[...] = jnp.zeros(dk_ref.shape, F32)
            dv_ref[...] = jnp.zeros(dv_ref.shape, F32)

        kv = k_ref[...]
        vv = v_ref[...]
        qs = []
        dos = []
        for g in range(Q_GROUP):
            sl = slice(g * HEAD_DIM, (g + 1) * HEAD_DIM)
            rows = slice(g * tq, (g + 1) * tq)
            qv = q_ref[:, sl]
            dov = do_ref[:, sl]
            delta = jnp.sum(dov * o_ref[:, sl], axis=-1, keepdims=True)
            sc = lax.dot_general(qv, kv, (((1,), (1,)), ((), ())), preferred_element_type=F32)
            p = jnp.exp(sc - lse_ref[g])
            dob = dov.astype(BF16)
            dp = lax.dot_general(dob, vv, (((1,), (1,)), ((), ())), preferred_element_type=F32)
            dsb = (p * (dp - delta)).astype(BF16)
            dq_ref[:, sl] = jnp.dot(dsb, kv, preferred_element_type=F32)
            ds_all[rows, :] = dsb
            p_all[rows, :] = p.astype(BF16)
            qs.append(qv)
            dos.append(dob)
        dk_ref[...] += lax.dot_general(jnp.concatenate(qs, axis=0), ds_all[...], (((0,), (0,)), ((), ())),
                                       preferred_element_type=F32)
        dv_ref[...] += lax.dot_general(jnp.concatenate(dos, axis=0), p_all[...], (((0,), (0,)), ((), ())),
                                       preferred_element_type=F32)
        if ne:
            pl.when(step == last)(finish)

    kvmap = lambda j, i: (j, 0, 0)
    return pl.pallas_call(
        kern, name=name, grid=(nkv, n_i),
        in_specs=[pl.BlockSpec((tq, gw), lambda j, i: (i + n_ctx // tq, j)), pl.BlockSpec((None, r, HEAD_DIM), kvmap),
                  pl.BlockSpec((None, r, HEAD_DIM), kvmap), pl.BlockSpec((tq, gw), lambda j, i: (i, j)),
                  pl.BlockSpec((Q_GROUP, tq, 1), lambda j, i: (j, i, 0)),
                  pl.BlockSpec((tq, gw), lambda j, i: (i, j))] + [_ANY] * ne,
        out_specs=[pl.BlockSpec((tq, gw), lambda j, i: (i, j)), pl.BlockSpec((None, HEAD_DIM, r), kvmap),
                   pl.BlockSpec((None, HEAD_DIM, r), kvmap)] + [_ANY] * ne,
        out_shape=[jax.ShapeDtypeStruct((s, ATTN_W), F32), jax.ShapeDtypeStruct((nkv, HEAD_DIM, r), F32),
                   jax.ShapeDtypeStruct((nkv, HEAD_DIM, r), F32)]
        + [jax.ShapeDtypeStruct(a.shape, a.dtype) for a in exchange],
        scratch_shapes=[pltpu.VMEM((Q_GROUP * tq, r), BF16)] * 2 + (_exchange_scratch(ne) if ne else []),
        compiler_params=_params())(q_all, k, v, o, lse, dcat, *exchange)


def _mod_fwd(name, c16, wm, bm):
    n = wm.shape[2]

    def kern(c_ref, w_ref, b_ref, o_ref):
        cv = c_ref[...]
        a = (cv * _sigmoid(cv)).astype(BF16)
        o_ref[...] = jnp.dot(a, w_ref[...].astype(BF16), preferred_element_type=F32) + b_ref[...]

    return pl.pallas_call(
        kern, name=name, grid=(2,),
        in_specs=[pl.BlockSpec(c16.shape, lambda l: (0, 0)), pl.BlockSpec((None, D_MODEL, n), lambda l: (l, 0, 0)),
                  pl.BlockSpec((None, 1, n), lambda l: (l, 0, 0))],
        out_specs=pl.BlockSpec((None, 16, n), lambda l: (l, 0, 0)),
        out_shape=jax.ShapeDtypeStruct((2, 16, n), F32), compiler_params=_params())(c16, wm, bm)


def _mod_wgrad(name, c16, dm):
    n = dm.shape[2]
    tn = 512

    def kern(c_ref, d_ref, o_ref):
        cv = c_ref[...]
        a = cv * _sigmoid(cv)
        o_ref[...] = lax.dot_general(a, d_ref[...], (((0,), (0,)), ((), ())), preferred_element_type=F32,
                                     precision=lax.Precision.HIGHEST)

    return pl.pallas_call(
        kern, name=name, grid=(2, n // tn),
        in_specs=[pl.BlockSpec(c16.shape, lambda l, j: (0, 0)), pl.BlockSpec((None, 16, tn), lambda l, j: (l, 0, j))],
        out_specs=pl.BlockSpec((None, D_MODEL, tn), lambda l, j: (l, 0, j)),
        out_shape=jax.ShapeDtypeStruct((2, D_MODEL, n), F32), compiler_params=_params())(c16, dm)


def _cctx_grad(name, parts, c_ctx):
    def kern(p_ref, c_ref, o_ref):
        d = p_ref[0, 0:1, :] + p_ref[2, 0:1, :] + p_ref[4, 0:1, :] + p_ref[6, 0:1, :]
        cv = c_ref[...]
        sg = _sigmoid(cv)
        o_ref[...] = d * (sg * (1.0 + cv * (1.0 - sg)))

    return pl.pallas_call(kern, name=name, out_shape=jax.ShapeDtypeStruct((1, D_MODEL), F32),
                          compiler_params=_params())(parts, c_ctx)


def _sum_slots(name, g, tr):
    n, rows, cols = g.shape

    def kern(g_ref, o_ref):
        acc = g_ref[0].astype(F32)
        for i in range(1, n):
            acc = acc + g_ref[i].astype(F32)
        o_ref[...] = acc

    return pl.pallas_call(kern, name=name, grid=(rows // tr,),
                          in_specs=[pl.BlockSpec((n, tr, cols), lambda i: (0, i, 0))],
                          out_specs=pl.BlockSpec((tr, cols), lambda i: (i, 0)),
                          out_shape=jax.ShapeDtypeStruct((rows, cols), F32), compiler_params=_params())(g)


def _sum_into(name, g, cvec, n_layers, layer, prev=None):
    n, r, cc = g.shape
    tr = _row_tile(r, cc, 512 * 1024)

    def kern(c_ref, g_ref, *rest):
        acc = g_ref[0].astype(F32)
        for i in range(1, n):
            acc = acc + g_ref[i].astype(F32)
        rest[-1][...] = acc

    in_specs = [pl.BlockSpec((n, tr, cc), lambda i, c_ref: (0, i, 0))]
    args = [cvec, g]
    aliases = {}
    if prev is not None:
        in_specs.append(_ANY)
        args.append(prev)
        aliases = {2: 0}
    grid_spec = pltpu.PrefetchScalarGridSpec(
        num_scalar_prefetch=1, grid=(r // tr,), in_specs=in_specs,
        out_specs=pl.BlockSpec((None, None, tr, cc), lambda i, c_ref: (layer, c_ref[0], i, 0)))
    return pl.pallas_call(kern, name=name, grid_spec=grid_spec,
                          out_shape=jax.ShapeDtypeStruct((n_layers, 2, r, cc), F32),
                          input_output_aliases=aliases, compiler_params=_params())(*args)


def _row_tile(rows, cols, max_bytes=1024 * 1024):
    if rows * cols * 4 <= 2 * max_bytes:
        return rows
    best = None
    for t in range(16, rows + 1, 16):
        if rows % t == 0 and t * cols * 4 <= max_bytes:
            best = t
    assert best is not None, (rows, cols)
    return best


def _adamw(name, w, g, m, v):
    shape = w.shape
    cols = shape[-1]
    rows = w.size // cols
    tr = _row_tile(rows, cols)

    def kern(w_ref, g_ref, m_ref, v_ref, d_ref, nm_ref, nv_ref):
        gv = g_ref[...]
        m2 = ADAM_B1 * m_ref[...] + (1.0 - ADAM_B1) * gv
        v2 = ADAM_B2 * v_ref[...] + (1.0 - ADAM_B2) * (gv * gv)
        m_hat = m2 / (1.0 - ADAM_B1 ** ADAM_STEP)
        v_hat = v2 / (1.0 - ADAM_B2 ** ADAM_STEP)
        d_ref[...] = -ADAM_LR * (m_hat / (jnp.sqrt(v_hat) + ADAM_EPS) + ADAM_WD * w_ref[...])
        nm_ref[...] = m2
        nv_ref[...] = v2

    spec = pl.BlockSpec((tr, cols), lambda i: (i, 0))
    sds = jax.ShapeDtypeStruct((rows, cols), F32)
    outs = pl.pallas_call(kern, name=name, grid=(rows // tr,), in_specs=[spec] * 4, out_specs=[spec] * 3,
                          out_shape=[sds] * 3, compiler_params=_params())(
        w.reshape(rows, cols), g.reshape(rows, cols), m.reshape(rows, cols), v.reshape(rows, cols))
    return tuple(o.reshape(shape) for o in outs)


_ANY = pl.BlockSpec(memory_space=pl.ANY)


def _mesh_pos():
    return lax.axis_index("x"), lax.axis_index("y"), lax.axis_index("c")


def _gather_phases(srcs, outs, send_sems, recv_sems, local_sems):
    n = len(srcs)
    x, y, c = _mesh_pos()
    me = (x, y, c)
    sibling = (x, y, 1 - c)
    chips = [(1 - x, y), (x, 1 - y), (1 - x, 1 - y)]

    def slot(px, py, pc):
        return 4 * px + 2 * py + pc

    def copy(t, k, s, to, src=None):
        dst = outs[t].at[s]
        return pltpu.make_async_remote_copy(src_ref=dst if src is None else src, dst_ref=dst,
                                            send_sem=send_sems.at[t, k], recv_sem=recv_sems.at[t, k],
                                            device_id=to, device_id_type=MESH)

    def mine(t):
        return pltpu.make_async_copy(srcs[t], outs[t].at[slot(*me)], local_sems.at[t])

    def first(t):
        return [copy(t, 0, slot(*me), sibling, src=srcs[t])] + [
            copy(t, 1 + j, slot(*me), (px, py, c), src=srcs[t]) for j, (px, py) in enumerate(chips)]

    def passed(t, j):
        px, py = chips[j]
        return copy(t, 4 + j, slot(px, py, c), sibling)

    def start():
        for t in range(n):
            mine(t).start()
        for t in range(n):
            for cp in first(t):
                cp.start()

    def forward():
        for j, (px, py) in enumerate(chips):
            for t in range(n):
                copy(t, 1 + j, slot(px, py, c), me).wait_recv()
                passed(t, j).start()

    def finish():
        for t in range(n):
            copy(t, 0, slot(x, y, 1 - c), me).wait_recv()
            for j, (px, py) in enumerate(chips):
                copy(t, 4 + j, slot(px, py, 1 - c), me).wait_recv()
        for t in range(n):
            for cp in first(t) + [passed(t, j) for j in range(len(chips))]:
                cp.wait_send()
            mine(t).wait()

    return start, forward, finish


def _gather_scratch(n):
    return [pltpu.SemaphoreType.DMA((n, 7)), pltpu.SemaphoreType.DMA((n, 7)), pltpu.SemaphoreType.DMA((n,))]


def _allgather(name, items):
    n = len(items)

    def body(*refs):
        start, forward, finish = _gather_phases(refs[:n], refs[n:2 * n], *refs[2 * n:])
        start()
        forward()
        finish()

    return pl.pallas_call(
        body, name=name, in_specs=[_ANY] * n, out_specs=[_ANY] * n,
        out_shape=[jax.ShapeDtypeStruct((N_DEV,) + a.shape, a.dtype) for a in items],
        scratch_shapes=_gather_scratch(n), compiler_params=_params())(*items)


def _pair_send_copies(srcs, bufs, send_sems, recv_sems):
    x, y, c = _mesh_pos()

    def copies():
        return [pltpu.make_async_remote_copy(
            src_ref=srcs[t].at[k, 1 - c], dst_ref=bufs[t].at[k], send_sem=send_sems.at[t, k],
            recv_sem=recv_sems.at[t, k], device_id=(x, y, 1 - c), device_id_type=MESH)
            for t in range(len(srcs)) for k in range(N_CHIP)]

    return copies


def _rs_pair_send(name, gs):
    n = len(gs)

    def body(*refs):
        copies = _pair_send_copies(refs[:n], refs[n:2 * n], *refs[2 * n:])
        for cp in copies():
            cp.start()
        for cp in copies():
            cp.wait()

    return pl.pallas_call(
        body, name=name, in_specs=[_ANY] * n, out_specs=[_ANY] * n,
        out_shape=[jax.ShapeDtypeStruct((N_CHIP,) + g.shape[2:], g.dtype) for g in gs],
        scratch_shapes=[pltpu.SemaphoreType.DMA((n, N_CHIP)), pltpu.SemaphoreType.DMA((n, N_CHIP))],
        compiler_params=_params())(*gs)


def _rs_pair_add(name, g, buf, cvec):
    _, _, r, cc = g.shape
    tr = _row_tile(r, cc, 2 * 1024 * 1024)

    def kern(c_ref, g_ref, b_ref, o_ref):
        o_ref[...] = (g_ref[...] + b_ref[...]).astype(BF16)

    grid_spec = pltpu.PrefetchScalarGridSpec(
        num_scalar_prefetch=1, grid=(N_CHIP, r // tr),
        in_specs=[pl.BlockSpec((None, None, tr, cc), lambda k, i, c_ref: (k, c_ref[0], i, 0)),
                  pl.BlockSpec((None, tr, cc), lambda k, i, c_ref: (k, i, 0))],
        out_specs=pl.BlockSpec((None, tr, cc), lambda k, i, c_ref: (k, i, 0)))
    return pl.pallas_call(kern, name=name, grid_spec=grid_spec,
                          out_shape=jax.ShapeDtypeStruct((N_CHIP, r, cc), BF16),
                          compiler_params=_params())(cvec, g, buf)


def _exchange_phases(srcs, bufs, send_sems, recv_sems, local_sems):
    n = len(srcs)
    x, y, c = _mesh_pos()
    kme = 2 * x + y
    chips = [(1 - x, y), (x, 1 - y), (1 - x, 1 - y)]

    def copies():
        local = [pltpu.make_async_copy(srcs[t].at[kme], bufs[t].at[kme], local_sems.at[t]) for t in range(n)]
        remote = []
        for t in range(n):
            for j, (px, py) in enumerate(chips):
                remote.append(pltpu.make_async_remote_copy(
                    src_ref=srcs[t].at[2 * px + py], dst_ref=bufs[t].at[kme], send_sem=send_sems.at[t, j],
                    recv_sem=recv_sems.at[t, j], device_id=(px, py, c), device_id_type=MESH))
        return local, remote

    def start():
        local, remote = copies()
        for cp in local + remote:
            cp.start()

    def finish():
        local, remote = copies()
        for cp in remote + local:
            cp.wait()

    return start, finish


def _exchange_scratch(n):
    return [pltpu.SemaphoreType.DMA((n, 3)), pltpu.SemaphoreType.DMA((n, 3)), pltpu.SemaphoreType.DMA((n,))]


def _rs_chip_exchange(name, ss):
    n = len(ss)

    def body(*refs):
        start, finish = _exchange_phases(refs[:n], refs[n:2 * n], *refs[2 * n:])
        start()
        finish()

    return pl.pallas_call(
        body, name=name, in_specs=[_ANY] * n, out_specs=[_ANY] * n,
        out_shape=[jax.ShapeDtypeStruct(s.shape, s.dtype) for s in ss],
        scratch_shapes=_exchange_scratch(n), compiler_params=_params())(*ss)


def _rs_pair_share(name, dsts, gather=()):
    nd = len(dsts)
    ng = len(gather)

    def body(*refs):
        ins = refs[:nd]
        outs = refs[nd + ng:2 * nd + ng]
        send_sems, recv_sems = refs[2 * (nd + ng):2 * (nd + ng) + 2]
        if ng:
            start, forward, finish = _gather_phases(refs[nd:nd + ng], refs[2 * nd + ng:2 * (nd + ng)],
                                                    *refs[2 * (nd + ng) + 2:])
            start()
        x, y, c = _mesh_pos()
        copies = []
        for d in range(nd):
            for l in range(dsts[d].shape[0]):
                copies.append(pltpu.make_async_remote_copy(
                    src_ref=ins[d].at[l, c], dst_ref=outs[d].at[l, c], send_sem=send_sems.at[d, l],
                    recv_sem=recv_sems.at[d, l], device_id=(x, y, 1 - c), device_id_type=MESH))
        for cp in copies:
            cp.start()
        if ng:
            forward()
        for cp in copies:
            cp.wait()
        if ng:
            finish()

    return pl.pallas_call(
        body, name=name, in_specs=[_ANY] * (nd + ng), out_specs=[_ANY] * (nd + ng),
        out_shape=[jax.ShapeDtypeStruct(a.shape, a.dtype) for a in dsts]
        + [jax.ShapeDtypeStruct((N_DEV,) + a.shape, a.dtype) for a in gather],
        input_output_aliases={d: d for d in range(nd)},
        scratch_shapes=[pltpu.SemaphoreType.DMA((nd, 2)), pltpu.SemaphoreType.DMA((nd, 2))]
        + (_gather_scratch(ng) if ng else []),
        compiler_params=_params())(*dsts, *gather)


def _rope_tables(seq, n_ctx):
    t = jnp.arange(seq)
    row = (t // GRID_W).astype(F32)
    col = (t % GRID_W).astype(F32)
    inv = ROPE_THETA ** (-jnp.arange(0, HEAD_DIM // 2, 2, dtype=F32) / (HEAD_DIM // 2))
    ang_r = row[:, None] * inv[None, :]
    ang_c = col[:, None] * inv[None, :]
    cos = jnp.concatenate([jnp.cos(ang_r)] * 2 + [jnp.cos(ang_c)] * 2, axis=1)
    sin = jnp.concatenate([-jnp.sin(ang_r), jnp.sin(ang_r), -jnp.sin(ang_c), jnp.sin(ang_c)], axis=1)
    cos = jnp.concatenate([jnp.ones((n_ctx, HEAD_DIM), F32), cos], axis=0)
    sin = jnp.concatenate([jnp.zeros((n_ctx, HEAD_DIM), F32), sin], axis=0)
    return jnp.tile(cos, (1, N_KV_HEADS)), jnp.tile(sin, (1, N_KV_HEADS))


def _to_heads(a, nh):
    return a.reshape(a.shape[0], nh, HEAD_DIM).transpose(1, 0, 2)


def _sample_step(x, ctx, tgt, mod, cmod, W, pending, cvec):
    W = dict(W)
    seq = x.shape[0]
    n_ctx = ctx.shape[0]
    tm = min(256, seq)
    tmc = min(256, n_ctx)
    tmm = 512 if seq % 512 == 0 else tm
    tmw = 1024 if seq % 1024 == 0 else tmm
    tmr = (n_ctx + seq) // 4 if (n_ctx + seq) % 64 == 0 else tm
    tmk = 2048 if seq % 2048 == 0 else tmw
    tq = min(256, seq)
    assert n_ctx % tm == 0 and seq % tm == 0

    def mv(l, j):
        return mod[l, j * D_MODEL:(j + 1) * D_MODEL].reshape(1, D_MODEL)

    csh1 = cmod[:D_MODEL].reshape(1, D_MODEL)
    csc1 = cmod[D_MODEL:2 * D_MODEL].reshape(1, D_MODEL)
    g_mix = [W['g_mix'][l:l + 1] for l in range(2)]
    g_ffn = [W['g_ffn'][l:l + 1] for l in range(2)]
    cos, sin = _rope_tables(seq, n_ctx)
    gq = jnp.tile(W['q_gain'], (1, N_Q_HEADS))
    gk = jnp.tile(W['k_gain'], (1, N_KV_HEADS))
    bdq = jnp.kron(jnp.eye(N_Q_HEADS, dtype=F32), jnp.ones((HEAD_DIM, HEAD_DIM), F32)).astype(BF16)
    bdk = bdq[:KV_W, :KV_W]
    w_sp = W['w_sp'].astype(BF16)
    w_sp_t = w_sp.transpose(0, 2, 1)
    bsp = jnp.broadcast_to(W['b_sp'].T[:, :, None], (CHUNK, N_SG, CHUNK)).reshape(CHUNK, SG_W)

    def ffn_fwd(l, h_mid, xf, gather=(), **tail):
        gu, a, *gathered = _ffn_in_swiglu(f'l{l}_ffn_in', xf, W['ffn_in'][l], tmw, gather=gather)
        if gathered:
            W['pw2'] = gathered[0].reshape(D_MODEL, D_MODEL)
            W['ffn_out'][1] = gathered[1].reshape(D_FF, D_MODEL)
        outs = _mm_rows_res(f'l{l}_ffn_out', a, W['ffn_out'][l], h_mid, mv(l, 5), tmm, **tail)
        return outs, (h_mid, xf, gu, a)

    def blocked(a):
        cols = a.shape[-1]
        return a.reshape(N_CHIP, 2, a.size // (2 * N_CHIP * cols), cols)

    def ffn_bwd(l, dh_out, dy, y_mix, saved, pair_send=()):
        h_mid, xf, gu, a = saved
        dgu = _ffn_out_dx_swiglu(f'l{l}_ffn_out_dx', dy, W['ffn_out'][l], gu, tmw)
        d_wo = _mm_wgrad(f'l{l}_ffn_out_dw', a, dy, tmw)
        d_wi = _mm_halves_wgrad_blk(f'l{l}_ffn_in_dw', xf, dgu, tmk, pair_send=pair_send)
        if pair_send:
            d_wi, *sent = d_wi
        else:
            sent = []
        dh_mid, dy_mix, d_sh2, d_sc2, d_g, d_gt1, sum_dy = _ffn_in_dx_norm_bwd(
            f'l{l}_ffn_in_dx', dgu, W['ffn_in'][l], h_mid, dh_out, g_ffn[l], mv(l, 4), tmm, gate=(y_mix, mv(l, 2)))
        return dh_mid, dy_mix, d_wi, d_wo, d_g, (d_sh2, d_sc2), d_gt1, sum_dy, sent

    xm0 = _modnorm_fwd('l0_mix_norm', x, g_mix[0], mv(0, 1), mv(0, 0), tm)
    xc0 = _modnorm_fwd('l0_ctx_norm', ctx, g_mix[0], csc1, csh1, tmc)
    xall = jnp.concatenate([xc0, xm0], axis=0)
    p_qkv, q_all, k_all, v_all = _mm_qk_prep('l0_qkv', xall, W['w_qkv'], cos, sin, gq, gk, bdq, bdk, tmr)
    p_sg, sg = _mm_sg('l0_sg_in', xm0, W['w_sgp'], w_sp, bsp, tmm)
    kh = _to_heads(k_all, N_KV_HEADS)
    vh = _to_heads(v_all, N_KV_HEADS)
    v_ones = jnp.concatenate([vh, jnp.ones(vh.shape[:2] + (1,), BF16),
                              jnp.zeros(vh.shape[:2] + (HEAD_DIM - 1,), BF16)], axis=2)
    o, lse, *gw = _attn_fwd('l0_attn', q_all, kh, v_ones, n_ctx, tq, gather=pending[:5])
    W['w_out'] = gw[0].reshape(D_MODEL, D_MODEL)
    W['ffn_in'] = [gw[1].reshape(N_CHIP, D_MODEL, FF_BLK), gw[4].reshape(N_CHIP, D_MODEL, FF_BLK)]
    W['ffn_out'] = [gw[2].reshape(D_FF, D_MODEL), None]
    W['pw1'] = gw[3].reshape(N_CHIP, D_MODEL, PW1_BLK).transpose(1, 0, 2).reshape(D_MODEL, 2 * D_MODEL)
    cat = jnp.concatenate([o.astype(BF16), sg], axis=1)
    y0, h1, xf0 = _mm_rows_res('l0_out', cat, W['w_out'], x, mv(0, 2), tmw, norm=(g_ffn[0], mv(0, 4), mv(0, 3)))
    (f0, h2, xm1), ffn0 = ffn_fwd(0, h1, xf0, gather=pending[5:], norm=(g_mix[1], mv(1, 1), mv(1, 0)))

    ag, hg = _mm_glu('l1_pw1', xm1, W['pw1'], W['b_pw1'], tmm)
    hc = _dwconv('l1_conv', hg, W['w_dw'], W['b_dw'])
    s1 = _lnsilu_fwd('l1_ln_silu', hc, W['ln_g'], W['ln_b'], tm)
    y1, h3, xf1 = _mm_rows_res('l1_pw2', s1, W['pw2'], h2, mv(1, 2), tmw, bias=W['b_pw2'],
                               norm=(g_ffn[1], mv(1, 4), mv(1, 3)))
    (dh4, d_g_final, loss_vec, dyf1, d_gt2_1), ffn1 = ffn_fwd(1, h3, xf1, final=(tgt, W['g_final']))

    dh3, dy1, d_wi1, d_wo1, d_gffn1, dmod1_ffn, d_gt1_1, d_b_pw2, _ = ffn_bwd(1, dh4, dyf1, y1, ffn1)
    d_pw2 = _mm_wgrad('l1_pw2_dw', s1, dy1, tmw)
    dhc, d_ln_g, d_ln_b = _mm_t_epilogue('l1_pw2_dx', dy1, W['pw2'], [(hc, None)], [W['ln_g'], W['ln_b']], 2,
                                         _lnsilu_bwd_epilogue, tmm)
    dhg = _dwconv('l1_conv_dx', dhc, W['w_dw'][::-1], jnp.zeros((1, D_MODEL), F32))
    d_w_dw, d_b_dw = _dwconv_wgrad('l1_conv_dw', hg, dhc)
    dag, d_b_pw1 = _glu_bwd('l1_glu_bwd', ag, dhg, W['b_pw1'], tm)
    d_pw1 = _mm_wgrad('l1_pw1_dw', xm1, dag, tmw).reshape(D_MODEL, N_CHIP, PW1_BLK).transpose(1, 0, 2)
    dh2, dyf0, d_sh1_1, d_sc1_1, d_gmix1, d_gt2_0, _ = _mm_t_epilogue(
        'l1_pw1_dx', dag, W['pw1'], [(h2, None), (dh3, None)], [g_mix[1], mv(1, 1)], 3, _norm_bwd_epilogue, tmm,
        gate=(f0, mv(0, 5)))

    gs1 = [blocked(g) for g in (d_wi1, d_wo1, d_pw1, d_pw2)]
    dh1, dy0, d_wi0, d_wo0, d_gffn0, dmod0_ffn, d_gt1_0, _, sib1 = ffn_bwd(0, dh2, dyf0, y0, ffn0, pair_send=gs1)
    dcat = _mm_rows('l0_out_dx', dy0, W['w_out'], tmw, trans_b=True)
    d_w_out = _mm_wgrad('l0_out_dw', cat, dy0, tmw)

    gs0 = [blocked(g) for g in (d_wi0, d_wo0, d_w_out)]
    sib0 = _rs_pair_send('rs_pair_send', gs0)
    gs = [gs0[0], gs1[0], gs0[1], gs1[1], gs0[2], gs1[2], gs1[3]]
    sib = [sib0[0], sib1[0], sib0[1], sib1[1], sib0[2], sib1[2], sib1[3]]
    ss = [_rs_pair_add(f'rs_pair_add{t}', gs[t], sib[t], cvec) for t in range(len(gs))]
    dq, dk, dv, *xs = _attn_bwd('l0_attn_bwd', q_all, kh, vh, o, lse, dcat, n_ctx, tq, exchange=ss)
    dk, dv = [t.transpose(2, 0, 1).reshape(n_ctx + seq, KV_W) for t in (dk, dv)]
    dp_qkv, d_gq, d_gk = _qk_prep_bwd('l0_qk_prep_bwd', p_qkv, cos, sin, dq, n_ctx // tm, dk, dv, gq, gk, bdq, bdk,
                                      tm)
    dp_sg, d_w_sp, d_bsp = _sg_bwd('l0_sg_bwd', p_sg, dcat, w_sp, w_sp_t, bsp, tm)
    d_w_qkv = _mm_wgrad('l0_qkv_dw', xall, dp_qkv, tmr)
    d_w_sgp = _mm_wgrad('l0_sg_in_dw', xm0, dp_sg, tmw)
    dxall = _mm_rows('l0_qkv_dx', dp_qkv, W['w_qkv'], tmr, trans_b=True)
    g_in = blocked(jnp.concatenate([d_w_qkv, d_w_sgp], axis=1).reshape(D_MODEL, N_CHIP, -1).transpose(1, 0, 2))
    sib_in = _rs_pair_send('rs_pair_send_w_in', [g_in])
    ss_in = _rs_pair_add('rs_pair_add_w_in', g_in, sib_in[0], cvec)
    dx, d_sh1_0, d_sc1_0, d_gmix0, *xs_in = _mm_t_epilogue(
        'l0_sg_in_dx', dp_sg, W['w_sgp'], [(x, None), (dh1, None), (dxall, lambda i: i + n_ctx // tm)],
        [g_mix[0], mv(0, 1)], 3, _norm_bwd_epilogue, tm, exchange=[ss_in])
    _, d_csh1, d_csc1, d_gmix0c = _modnorm_bwd('l0_ctx_norm_bwd', ctx, [_ri(dxall)], None, g_mix[0], csc1, tmc)

    zero = jnp.zeros((1, D_MODEL), F32)
    d = D_MODEL
    pack = jnp.concatenate([
        d_gmix0 + d_gmix0c, d_gmix1, d_gffn0, d_gffn1, d_g_final,
        jnp.concatenate([d_gq, d_gk, jnp.zeros((1, d - ATTN_W - KV_W), F32)], axis=1),
        d_b_pw1.reshape(2, d), d_w_dw, d_b_dw, d_ln_g, d_ln_b, d_b_pw2,
        d_sh1_0, d_sc1_0, d_gt1_0, *dmod0_ffn, d_gt2_0, d_sh1_1, d_sc1_1, d_gt1_1, *dmod1_ffn, d_gt2_1,
        d_csh1, d_csc1, zero, zero, zero, zero, jnp.zeros((3, d), F32),
        d_w_sp.reshape(64, d),
        jnp.pad(d_bsp.reshape(CHUNK, N_SG, CHUNK).sum(axis=-1).reshape(1, CHUNK * N_SG), ((0, 7), (0, d - SG_W)))],
        axis=0)
    assert pack.shape == (SMALL_ROWS, d)

    t_ffn_in = _sum_into('rs_sum_ffn_in1', xs[1], cvec, 2, 1, prev=_sum_into('rs_sum_ffn_in0', xs[0], cvec, 2, 0))
    t_ffn_out = _sum_into('rs_sum_ffn_out1', xs[3], cvec, 2, 1, prev=_sum_into('rs_sum_ffn_out0', xs[2], cvec, 2, 0))
    *reduced, small = _rs_pair_share('rs_pair_share', [
        t_ffn_in, t_ffn_out, _sum_into('rs_sum_w_in', xs_in[0], cvec, 1, 0), _sum_into('rs_sum_w_out', xs[4], cvec, 1, 0),
        _sum_into('rs_sum_pw1', xs[5], cvec, 1, 0), _sum_into('rs_sum_pw2', xs[6], cvec, 1, 0)], gather=[pack])
    big = dict(zip(['w_ffn_in', 'w_ffn_out', 'w_in', 'w_out', 'w_pw1', 'w_pw2'], reduced))
    return loss_vec, dx, small, big


def kernel(x, c, ctx, c_ctx, w_mod, b_mod, g_mix, g_ffn, w_ffn_in, w_ffn_out, w_in, q_gain, k_gain, w_sp, b_sp, w_out, w_pw1, b_pw1, w_dw, b_dw, ln_g, ln_b, w_pw2, b_pw2, g_final, loss_target, m_c_ctx, m_w_mod, m_b_mod, m_g_mix, m_g_ffn, m_w_ffn_in, m_w_ffn_out, m_w_in, m_q_gain, m_k_gain, m_w_sp, m_b_sp, m_w_out, m_w_pw1, m_b_pw1, m_w_dw, m_b_dw, m_ln_g, m_ln_b, m_w_pw2, m_b_pw2, m_g_final, v_c_ctx, v_w_mod, v_b_mod, v_g_mix, v_g_ffn, v_w_ffn_in, v_w_ffn_out, v_w_in, v_q_gain, v_k_gain, v_w_sp, v_b_sp, v_w_out, v_w_pw1, v_b_pw1, v_w_dw, v_b_dw, v_ln_g, v_ln_b, v_w_pw2, v_b_pw2, v_g_final):
    given = dict(locals())
    ix, iy, ic = _mesh_pos()
    chip = 2 * ix + iy
    me = 2 * chip + ic
    d = D_MODEL
    q4 = d // N_CHIP

    small = jnp.concatenate([c.reshape(4, q4), b_pw1.reshape(2, q4), w_dw[0], b_dw, ln_g, ln_b, b_pw2,
                             jnp.zeros((7, q4), F32)], axis=0)
    def my_half(a):
        r = a.shape[0] // 2
        return lax.dynamic_slice_in_dim(a, ic * r, r, axis=0).astype(BF16)

    sgath, g_w_in = _allgather('ag_first', [small, my_half(w_in[0])])
    c_all = sgath[:, 0:4].reshape(N_DEV, d)
    per_chip = sgath[0::2]
    b_pw1_f = per_chip[:, 4:6].reshape(1, 2 * d)
    w_dw_f = per_chip[:, 6:6 + CONV_W].transpose(1, 0, 2).reshape(CONV_W, d)
    b_dw_f, ln_g_f, ln_b_f, b_pw2_f = [per_chip[:, 37 + i].reshape(1, d) for i in range(4)]

    c16 = jnp.concatenate([c_all, c_ctx[None, :], jnp.zeros((7, d), F32)], axis=0)
    bm_sh = lax.dynamic_slice_in_dim(b_mod, chip * MOD_BLK, MOD_BLK, axis=1).reshape(2, 1, MOD_BLK)
    mod_piece = _mod_fwd('mod_fwd', c16, w_mod, bm_sh)
    mgath = _allgather('ag_mod', [mod_piece.reshape(32, MOD_BLK)])[0]
    mod_all = mgath[0::2].reshape(N_CHIP, 2, 16, MOD_BLK).transpose(1, 2, 0, 3).reshape(2, 16, MOD_W)
    mod_me = lax.dynamic_index_in_dim(mod_all, me, axis=1, keepdims=False)
    cmod = mod_all[0, N_DEV]

    w_in_f = g_w_in.reshape(N_CHIP, d, -1).transpose(1, 0, 2).reshape(d, -1)
    W = dict(
        w_qkv=w_in_f[:, :QKV_W], w_sgp=w_in_f[:, QKV_W:],
        g_mix=g_mix, g_ffn=g_ffn, g_final=g_final.reshape(1, d), q_gain=q_gain, k_gain=k_gain, w_sp=w_sp[0],
        b_sp=b_sp[0], b_pw1=b_pw1_f, w_dw=w_dw_f, b_dw=b_dw_f, ln_g=ln_g_f, ln_b=ln_b_f, b_pw2=b_pw2_f)
    pending = [my_half(w_out[0]), my_half(w_ffn_in[0]), my_half(w_ffn_out[0]), my_half(w_pw1[0]),
               my_half(w_ffn_in[1]), my_half(w_pw2[0]), my_half(w_ffn_out[1])]
    cvec = jnp.reshape(ic, (1,)).astype(jnp.int32)

    loss_vec, dx, pg, big = _sample_step(x[0], ctx[0], loss_target[0], mod_me, cmod, W, pending, cvec)
    loss = lax.psum(jnp.sum(loss_vec), ("x", "y", "c"))

    s8 = _sum_slots('sum_small_grads', pg, SMALL_ROWS)
    grads = dict(big)

    def chip_cols(a, width):
        return lax.dynamic_slice_in_dim(a, chip * width, width, axis=a.ndim - 1)

    grads['g_mix'] = s8[0:2]
    grads['g_ffn'] = s8[2:4]
    grads['g_final'] = s8[4]
    grads['q_gain'] = s8[5, :ATTN_W].reshape(N_Q_HEADS, HEAD_DIM).sum(axis=0)[None, :]
    grads['k_gain'] = s8[5, ATTN_W:ATTN_W + KV_W].reshape(N_KV_HEADS, HEAD_DIM).sum(axis=0)[None, :]
    grads['b_pw1'] = chip_cols(s8[6:8].reshape(1, 2 * d), 2 * q4)
    grads['w_dw'] = chip_cols(s8[8:8 + CONV_W], q4)[None]
    grads['b_dw'] = chip_cols(s8[39:40], q4)
    grads['ln_g'] = chip_cols(s8[40:41], q4)
    grads['ln_b'] = chip_cols(s8[41:42], q4)
    grads['b_pw2'] = chip_cols(s8[42:43], q4)
    grads['w_sp'] = s8[64:128].reshape(1, N_SG, CHUNK, CHUNK)
    grads['b_sp'] = s8[128, :SG_W].reshape(CHUNK, N_SG).T[None]

    dm_rows = pg[:, 43:55].reshape(N_DEV, 2, MOD_W)
    ctx_row = s8[55:61].reshape(1, MOD_W)
    dm = jnp.stack([jnp.concatenate([dm_rows[:, 0], ctx_row, jnp.zeros((7, MOD_W), F32)], axis=0),
                    jnp.concatenate([dm_rows[:, 1], jnp.zeros((8, MOD_W), F32)], axis=0)], axis=0)
    grads['b_mod'] = jnp.stack([s8[43:49].reshape(MOD_W) + ctx_row[0], s8[49:55].reshape(MOD_W)], axis=0)
    dm_sh = chip_cols(dm, MOD_BLK)
    grads['w_mod'] = _mod_wgrad('mod_dw', c16, dm_sh)
    d_silu = _mm_rows('mod_dx', dm_sh[0], w_mod[0], 16, trans_b=True)
    parts = _allgather('ag_cctx', [d_silu[8:16]])[0]
    grads['c_ctx'] = _cctx_grad('cctx_grad', parts, c_ctx.reshape(1, d))[0]

    deltas, new_m, new_v = {}, {}, {}
    for n in WEIGHT_NAMES:
        grads[n] = grads[n].reshape(given[n].shape)
        deltas[n], new_m[n], new_v[n] = _adamw(f'adamw_{n}', given[n], grads[n], given['m_' + n], given['v_' + n])

    return (loss, dx[None], *[grads[n] for n in WEIGHT_NAMES], *[deltas[n] for n in WEIGHT_NAMES],
            *[new_m[n] for n in WEIGHT_NAMES], *[new_v[n] for n in WEIGHT_NAMES])
```

```python
import functools
import math

import jax
import jax.numpy as jnp
from jax import lax
from jax.experimental import pallas as pl
from jax.experimental.pallas import tpu as pltpu

F32 = jnp.float32
BF16 = jnp.bfloat16
MESH = pl.DeviceIdType.MESH

D_MODEL = 1024
HEAD_DIM = 64
N_Q_HEADS = 8
N_KV_HEADS = 2
Q_GROUP = N_Q_HEADS // N_KV_HEADS
ATTN_W = N_Q_HEADS * HEAD_DIM
KV_W = N_KV_HEADS * HEAD_DIM
QKV_W = ATTN_W + 2 * KV_W
SG_W = D_MODEL - ATTN_W
N_SG = 4
CHUNK = 128
GRID_W = 64
ROPE_THETA = 10000.0
CONV_W = 31
CONV_HALO = 16
D_FF = 2816
MOD_W = 6 * D_MODEL
N_CHIP = 4
N_DEV = 8
FF_BLK = 2 * D_FF // N_CHIP
PW1_BLK = 2 * D_MODEL // N_CHIP
MOD_BLK = MOD_W // N_CHIP
EPS = 1e-6
GELU_C = math.sqrt(2.0 / math.pi)
GELU_A = 0.044715

ADAM_LR = 0.001
ADAM_B1 = 0.9
ADAM_B2 = 0.999
ADAM_EPS = 1e-08
ADAM_WD = 0.01
ADAM_STEP = 10

SMALL_ROWS = 136

VMEM_LIMIT_BYTES = 56 * 1024 * 1024

WEIGHT_NAMES = ['c_ctx', 'w_mod', 'b_mod', 'g_mix', 'g_ffn', 'w_ffn_in', 'w_ffn_out', 'w_in', 'q_gain', 'k_gain',
                'w_sp', 'b_sp', 'w_out', 'w_pw1', 'b_pw1', 'w_dw', 'b_dw', 'ln_g', 'ln_b', 'w_pw2', 'b_pw2',
                'g_final']


def _params():
    return pltpu.CompilerParams(vmem_limit_bytes=VMEM_LIMIT_BYTES)


def _ri(arr, width=None, col_block=0, row_off=0, row_fn=None):
    return (arr, arr.shape[1] if width is None else width, col_block,
            (lambda i: i + row_off) if row_fn is None else row_fn)


def _rowwise(name, body, n_tiles, tm, row_ins, vec_ins, row_outs, vec_outs):
    nri, nvi, nro = len(row_ins), len(vec_ins), len(row_outs)

    def kern(*refs):
        ri = refs[:nri]
        vi = refs[nri:nri + nvi]
        ro = refs[nri + nvi:nri + nvi + nro]
        vo = refs[nri + nvi + nro:]
        if vo:
            @pl.when(pl.program_id(0) == 0)
            def _():
                for r in vo:
                    r[...] = jnp.zeros(r.shape, r.dtype)
        body(ri, vi, ro, vo)

    in_specs = [pl.BlockSpec((tm, bw), lambda i, cb=cb, rf=rf: (rf(i), cb)) for (_, bw, cb, rf) in row_ins]
    in_specs += [pl.BlockSpec(a.shape, lambda i, nd=a.ndim: (0,) * nd) for a in vec_ins]
    out_specs = [pl.BlockSpec((tm, w), lambda i: (i, 0)) for (w, _) in row_outs]
    out_specs += [pl.BlockSpec(s, lambda i, nd=len(s): (0,) * nd) for s in vec_outs]
    out_shape = [jax.ShapeDtypeStruct((n_tiles * tm, w), dt) for (w, dt) in row_outs]
    out_shape += [jax.ShapeDtypeStruct(s, F32) for s in vec_outs]
    return pl.pallas_call(kern, name=name, grid=(n_tiles,), in_specs=in_specs, out_specs=out_specs,
                          out_shape=out_shape, compiler_params=_params())(*[a for (a, _, _, _) in row_ins], *vec_ins)


def _colsum(v):
    return jnp.sum(v, axis=0, keepdims=True)


def _rowmean(v):
    return jnp.mean(v, axis=-1, keepdims=True)


def _rms(h):
    r = lax.rsqrt(_rowmean(h * h) + EPS)
    return h * r, r


def _rms_bwd(n, r, gy):
    return r * (gy - n * _rowmean(gy * n))


def _ln_stats(v):
    xc = v - _rowmean(v)
    r = lax.rsqrt(_rowmean(xc * xc) + EPS)
    return xc * r, r


def _ln_bwd(xh, r, dxh):
    return r * (dxh - _rowmean(dxh) - xh * _rowmean(dxh * xh))


def _sigmoid(v):
    return 1.0 / (1.0 + jnp.exp(-v))


def _gelu(v):
    t = jnp.tanh(GELU_C * (v + GELU_A * (v * v * v)))
    return 0.5 * v * (1.0 + t), t


def _gelu_grad(v, t):
    return 0.5 * (1.0 + t) + 0.5 * v * (1.0 - t * t) * (GELU_C * (1.0 + 3.0 * GELU_A * (v * v)))


def _modnorm_fwd(name, h, g, sc, sh, tm):
    def body(ri, vi, ro, vo):
        n, _ = _rms(ri[0][...])
        ro[0][...] = ((n * vi[0][...]) * (1.0 + vi[1][...]) + vi[2][...]).astype(BF16)

    return _rowwise(name, body, h.shape[0] // tm, tm, [_ri(h)], [g, sc, sh], [(D_MODEL, BF16)], [])[0]


def _modnorm_bwd_tile(hv, dxm, gv, scv):
    n, r = _rms(hv)
    dy = dxm * (1.0 + scv)
    return _rms_bwd(n, r, dy * gv), _colsum(dxm), _colsum(dxm * (n * gv)), _colsum(dy * n)


def _mm_row_epilogue(name, mm_args, mm_specs, dx_fn, rows_in, vecs_in, n_vec_out, epilogue, tm, gate=None,
                     exchange=()):
    rows_in = list(rows_in) + ([(gate[0], None)] if gate else [])
    vecs_in = list(vecs_in) + ([gate[1]] if gate else [])
    ng = 1 if gate else 0
    n_vec_out += 2 * ng
    nmm, nr, nv, ne = len(mm_args), len(rows_in), len(vecs_in), len(exchange)
    m = rows_in[0][0].shape[0]
    last = m // tm - 1

    def kern(*refs):
        ins = refs[:nmm + nr + nv]
        o_ref = refs[nmm + nr + nv + ne]
        vo = refs[nmm + nr + nv + ne + 1 + ng:nmm + nr + nv + ne + 1 + ng + n_vec_out]
        if ne:
            srcs = refs[nmm + nr + nv:nmm + nr + nv + ne]
            rest = refs[nmm + nr + nv + ne + 1 + ng + n_vec_out:]
            start, finish = _exchange_phases(srcs, rest[:ne], *rest[ne:])
            pl.when(pl.program_id(0) == 0)(start)

        @pl.when(pl.program_id(0) == 0)
        def _():
            for r in vo:
                r[...] = jnp.zeros(r.shape, F32)

        row_out, incs = epilogue(dx_fn(*ins[:nmm]), [r[...] for r in ins[nmm:nmm + nr - ng]],
                                 [r[...] for r in ins[nmm + nr:nmm + nr + nv - ng]])
        o_ref[...] = row_out
        if ng:
            dy = row_out * ins[nmm + nr + nv - 1][...]
            refs[nmm + nr + nv + ne + 1][...] = dy.astype(BF16)
            incs = list(incs) + [_colsum(row_out * ins[nmm + nr - 1][...]), _colsum(dy)]
        for r, inc in zip(vo, incs):
            r[...] += inc
        if ne:
            pl.when(pl.program_id(0) == last)(finish)

    row = pl.BlockSpec((tm, D_MODEL), lambda i: (i, 0))
    vec = pl.BlockSpec((1, D_MODEL), lambda i: (0, 0))
    row_specs = [row if fn is None else pl.BlockSpec((tm, D_MODEL), lambda i, fn=fn: (fn(i), 0)) for _, fn in rows_in]
    return pl.pallas_call(
        kern, name=name, grid=(m // tm,), in_specs=list(mm_specs) + row_specs + [vec] * nv + [_ANY] * ne,
        out_specs=[row] * (1 + ng) + [vec] * n_vec_out + [_ANY] * ne,
        out_shape=[jax.ShapeDtypeStruct((m, D_MODEL), F32)] + [jax.ShapeDtypeStruct((m, D_MODEL), BF16)] * ng
        + [jax.ShapeDtypeStruct((1, D_MODEL), F32)] * n_vec_out
        + [jax.ShapeDtypeStruct(e.shape, e.dtype) for e in exchange],
        scratch_shapes=_exchange_scratch(ne) if ne else [],
        compiler_params=_params())(*mm_args, *[r for r, _ in rows_in], *vecs_in, *exchange)


def _norm_bwd_epilogue(dxm, rows, vecs):
    if len(rows) > 2:
        dxm = dxm + rows[2]
    dx, dsh, dsc, dg = _modnorm_bwd_tile(rows[0], dxm, vecs[0], vecs[1])
    return rows[1] + dx, [dsh, dsc, dg]


def _lnsilu_bwd_epilogue(ds, rows, vecs):
    xh, r = _ln_stats(rows[0])
    ln = xh * vecs[0] + vecs[1]
    sg = _sigmoid(ln)
    dln = ds * (sg * (1.0 + ln * (1.0 - sg)))
    return _ln_bwd(xh, r, dln * vecs[0]), [_colsum(dln * xh), _colsum(dln)]


def _modnorm_bwd(name, h, dxs, dh_in, g, sc, tm):
    ndx = len(dxs)

    def body(ri, vi, ro, vo):
        dxm = ri[1][...].astype(F32)
        for r in ri[2:1 + ndx]:
            dxm = dxm + r[...].astype(F32)
        dx, dsh, dsc, dg = _modnorm_bwd_tile(ri[0][...], dxm, vi[0][...], vi[1][...])
        vo[0][...] += dsh
        vo[1][...] += dsc
        vo[2][...] += dg
        if dh_in is not None:
            ro[0][...] = ri[1 + ndx][...] + dx

    row_ins = [_ri(h)] + list(dxs) + ([_ri(dh_in)] if dh_in is not None else [])
    row_outs = [(D_MODEL, F32)] if dh_in is not None else []
    outs = _rowwise(name, body, h.shape[0] // tm, tm, row_ins, [g, sc], row_outs, [(1, D_MODEL)] * 3)
    if dh_in is None:
        return (None, *outs)
    return tuple(outs)


def _head_mean(v, bd):
    hi = v.astype(BF16)
    lo = (v - hi.astype(F32)).astype(BF16)
    s = jnp.dot(hi, bd, preferred_element_type=F32) + jnp.dot(lo, bd, preferred_element_type=F32)
    return s * (1.0 / HEAD_DIM)


def _swap16(v):
    w = v.shape[1]
    lane = lax.broadcasted_iota(jnp.int32, v.shape, 1)
    return jnp.where((lane & 16) == 0, pltpu.roll(v, w - 16, 1), pltpu.roll(v, 16, 1))


def _rope(v, cos, sin):
    return v * cos + _swap16(v) * sin


def _rope_bwd(d, cos, sin):
    return d * cos + _swap16(d * sin)


def _q_wide(t):
    return jnp.concatenate([t] * (ATTN_W // KV_W), axis=1)


def _mm_qk_prep(name, xall, w_qkv, cos, sin, gq, gk, bdq, bdk, tm):
    r, k = xall.shape

    def kern(x_ref, w_ref, cos_ref, sin_ref, gq_ref, gk_ref, bdq_ref, bdk_ref, p_ref, q_ref, k_ref, v_ref):
        p = jnp.dot(x_ref[...], w_ref[...], preferred_element_type=F32)
        p_ref[...] = p
        cosv = cos_ref[...]
        sinv = sin_ref[...]
        pq = p[:, :ATTN_W]
        rq = lax.rsqrt(_head_mean(pq * pq, bdq_ref[...]) + EPS)
        q_ref[...] = (_rope(pq * rq * gq_ref[...], _q_wide(cosv), _q_wide(sinv)) * (HEAD_DIM ** -0.5)).astype(BF16)
        pk = p[:, ATTN_W:ATTN_W + KV_W]
        rk = lax.rsqrt(_head_mean(pk * pk, bdk_ref[...]) + EPS)
        k_ref[...] = _rope(pk * rk * gk_ref[...], cosv, sinv).astype(BF16)
        v_ref[...] = p[:, ATTN_W + KV_W:].astype(BF16)

    def rows(w):
        return pl.BlockSpec((tm, w), lambda i: (i, 0))

    def whole(a):
        return pl.BlockSpec(a.shape, lambda i: (0, 0))

    return pl.pallas_call(
        kern, name=name, grid=(r // tm,),
        in_specs=[rows(k), whole(w_qkv), rows(KV_W), rows(KV_W), whole(gq), whole(gk), whole(bdq), whole(bdk)],
        out_specs=[rows(QKV_W), rows(ATTN_W), rows(KV_W), rows(KV_W)],
        out_shape=[jax.ShapeDtypeStruct((r, QKV_W), F32), jax.ShapeDtypeStruct((r, ATTN_W), BF16),
                   jax.ShapeDtypeStruct((r, KV_W), BF16), jax.ShapeDtypeStruct((r, KV_W), BF16)],
        compiler_params=_params())(xall, w_qkv, cos, sin, gq, gk, bdq, bdk)


def _qk_prep_bwd(name, p_qkv, cos, sin, dq, n_ctx_tiles, dk, dv, gq, gk, bdq, bdk, tm):
    def norm_rope_bwd(p, gain, bd, cosv, sinv, dout):
        r = lax.rsqrt(_head_mean(p * p, bd) + EPS)
        n = p * r
        dqn = _rope_bwd(dout, cosv, sinv)
        gy = dqn * gain
        return r * (gy - n * _head_mean(gy * n, bd)), _colsum(dqn * n)

    def body(ri, vi, ro, vo):
        pkv = ri[1][...]
        cosv = ri[2][...]
        sinv = ri[3][...]
        dqv = jnp.where(pl.program_id(0) >= n_ctx_tiles, ri[4][...] * (HEAD_DIM ** -0.5), 0.0)
        dxq, dgq = norm_rope_bwd(ri[0][...], vi[0][...], vi[2][...], _q_wide(cosv), _q_wide(sinv), dqv)
        dxk, dgk = norm_rope_bwd(pkv[:, :KV_W], vi[1][...], vi[3][...], cosv, sinv, ri[5][...])
        ro[0][:, :ATTN_W] = dxq.astype(BF16)
        ro[0][:, ATTN_W:ATTN_W + KV_W] = dxk.astype(BF16)
        ro[0][:, ATTN_W + KV_W:] = ri[6][...].astype(BF16)
        vo[0][...] += dgq
        vo[1][...] += dgk

    return _rowwise(name, body, p_qkv.shape[0] // tm, tm,
                    [_ri(p_qkv, ATTN_W, 0), _ri(p_qkv, 2 * KV_W, 2), _ri(cos), _ri(sin),
                     _ri(dq, row_fn=lambda i: jnp.maximum(i - n_ctx_tiles, 0)), _ri(dk), _ri(dv)],
                    [gq, gk, bdq, bdk], [(QKV_W, BF16)], [(1, ATTN_W), (1, KV_W)])


def _mm_sg(name, xm, w_sgp, w_sp, bsp, tm):
    m, k = xm.shape

    def kern(x_ref, w_ref, wsp_ref, bsp_ref, p_ref, o_ref):
        p = jnp.dot(x_ref[...], w_ref[...], preferred_element_type=F32)
        p_ref[...] = p
        gu, _ = _gelu(p[:, :SG_W])
        gv, _ = _gelu(p[:, SG_W:])
        for g in range(N_SG):
            sl = slice(g * CHUNK, (g + 1) * CHUNK)
            vn, _ = _ln_stats(gv[:, sl])
            vnb = vn.astype(BF16)
            for c in range(tm // CHUNK):
                rs = slice(c * CHUNK, (c + 1) * CHUNK)
                mixed = jnp.dot(wsp_ref[g], vnb[rs], preferred_element_type=F32) + bsp_ref[:, sl]
                o_ref[rs, sl] = (gu[rs, sl] * mixed).astype(BF16)

    return pl.pallas_call(
        kern, name=name, grid=(m // tm,),
        in_specs=[pl.BlockSpec((tm, k), lambda i: (i, 0)), pl.BlockSpec(w_sgp.shape, lambda i: (0, 0)),
                  pl.BlockSpec(w_sp.shape, lambda i: (0, 0, 0)), pl.BlockSpec(bsp.shape, lambda i: (0, 0))],
        out_specs=[pl.BlockSpec((tm, 2 * SG_W), lambda i: (i, 0)), pl.BlockSpec((tm, SG_W), lambda i: (i, 0))],
        out_shape=[jax.ShapeDtypeStruct((m, 2 * SG_W), F32), jax.ShapeDtypeStruct((m, SG_W), BF16)],
        compiler_params=_params())(xm, w_sgp, w_sp, bsp)


def _sg_bwd(name, p_sg, dcat, w_sp, w_sp_t, bsp, tm):
    def body(ri, vi, ro, vo):
        p = ri[0][...]
        dsg = ri[1][...]
        su = p[:, :SG_W]
        sv = p[:, SG_W:]
        gu, tu = _gelu(su)
        gv, tv = _gelu(sv)
        for g in range(N_SG):
            sl = slice(g * CHUNK, (g + 1) * CHUNK)
            vn, r = _ln_stats(gv[:, sl])
            vnb = vn.astype(BF16)
            for m in range(tm // CHUNK):
                rs = slice(m * CHUNK, (m + 1) * CHUNK)
                mixed = jnp.dot(vi[0][g], vnb[rs], preferred_element_type=F32) + vi[2][:, sl]
                d_o = dsg[rs, sl]
                dm = d_o * gu[rs, sl]
                vo[1][:, sl] += dm
                dmb = dm.astype(BF16)
                vo[0][g] += lax.dot_general(dmb, vnb[rs], (((1,), (1,)), ((), ())), preferred_element_type=F32)
                dvn = jnp.dot(vi[1][g], dmb, preferred_element_type=F32)
                dgv = _ln_bwd(vn[rs], r[rs], dvn)
                ro[0][rs, sl] = (d_o * mixed * _gelu_grad(su[rs, sl], tu[rs, sl])).astype(BF16)
                ro[0][rs, SG_W + g * CHUNK:SG_W + (g + 1) * CHUNK] = (
                    dgv * _gelu_grad(sv[rs, sl], tv[rs, sl])).astype(BF16)

    return _rowwise(name, body, p_sg.shape[0] // tm, tm, [_ri(p_sg), _ri(dcat, SG_W, 1)], [w_sp, w_sp_t, bsp],
                    [(2 * SG_W, BF16)], [(N_SG, CHUNK, CHUNK), (CHUNK, SG_W)])


def _glu_bwd(name, ag, dhg, b, tm):
    def body(ri, vi, ro, vo):
        v = ri[0][...] + vi[0][...]
        d = ri[1][...]
        a = v[:, :D_MODEL]
        sg = _sigmoid(v[:, D_MODEL:])
        da = d * sg
        dgate = d * a * sg * (1.0 - sg)
        ro[0][:, :D_MODEL] = da.astype(BF16)
        ro[0][:, D_MODEL:] = dgate.astype(BF16)
        vo[0][:, :D_MODEL] += _colsum(da)
        vo[0][:, D_MODEL:] += _colsum(dgate)

    return _rowwise(name, body, ag.shape[0] // tm, tm, [_ri(ag), _ri(dhg)], [b], [(2 * D_MODEL, BF16)],
                    [(1, 2 * D_MODEL)])


def _lnsilu_fwd(name, hc, g, b, tm):
    def body(ri, vi, ro, vo):
        xh, _ = _ln_stats(ri[0][...])
        ln = xh * vi[0][...] + vi[1][...]
        ro[0][...] = (ln * _sigmoid(ln)).astype(BF16)

    return _rowwise(name, body, hc.shape[0] // tm, tm, [_ri(hc)], [g, b], [(D_MODEL, BF16)], [])[0]


def _conv_blocks(seq):
    cb = 128
    tt = 128 if seq % 128 == 0 else seq
    return cb, tt


def _conv_taps(win, tt):
    n = win.shape[0]
    for s in range(8):
        ws = win if s == 0 else pltpu.roll(win, n - s, 0)
        for q in range(4):
            j = 8 * q + s - 1
            if 0 <= j < CONV_W:
                yield j, ws[8 * q:8 * q + tt, :]


def _fill_padded(pad_ref, x_ref, seq):
    zeros = jnp.zeros((CONV_HALO, pad_ref.shape[1]), F32)
    pad_ref[0:CONV_HALO, :] = zeros
    pad_ref[seq + CONV_HALO:seq + 2 * CONV_HALO, :] = zeros
    pad_ref[CONV_HALO:seq + CONV_HALO, :] = x_ref[...]


def _dwconv(name, xin, w, b, gather=()):
    seq = xin.shape[0]
    cb, tt = _conv_blocks(seq)
    ng = len(gather)
    last = D_MODEL // cb - 1

    def kern(x_ref, w_ref, b_ref, *rest):
        o_ref, pad_ref = rest[ng], rest[2 * ng + 1]
        if ng:
            start, forward, finish = _gather_phases(rest[:ng], rest[ng + 1:2 * ng + 1], *rest[2 * ng + 2:])
            pl.when(pl.program_id(0) == 0)(start)
            pl.when(pl.program_id(0) == (3 * last) // 4)(forward)
        _fill_padded(pad_ref, x_ref, seq)
        wv = w_ref[...]
        bv = b_ref[...]

        def step(t, carry):
            base = pl.multiple_of(t * tt, tt)
            win = pad_ref[pl.ds(base, tt + 2 * CONV_HALO), :]
            acc = jnp.zeros((tt, cb), F32) + bv
            for j, rows in _conv_taps(win, tt):
                acc = acc + wv[j:j + 1, :] * rows
            o_ref[pl.ds(base, tt), :] = acc
            return carry

        lax.fori_loop(0, seq // tt, step, 0)
        if ng:
            pl.when(pl.program_id(0) == last)(finish)

    outs = pl.pallas_call(
        kern, name=name, grid=(D_MODEL // cb,),
        in_specs=[pl.BlockSpec((seq, cb), lambda j: (0, j)), pl.BlockSpec((CONV_W, cb), lambda j: (0, j)),
                  pl.BlockSpec((1, cb), lambda j: (0, j))] + [_ANY] * ng,
        out_specs=[pl.BlockSpec((seq, cb), lambda j: (0, j))] + [_ANY] * ng,
        out_shape=[jax.ShapeDtypeStruct((seq, D_MODEL), F32)]
        + [jax.ShapeDtypeStruct((N_DEV,) + g.shape, g.dtype) for g in gather],
        scratch_shapes=[pltpu.VMEM((seq + 2 * CONV_HALO, cb), F32)] + (_gather_scratch(ng) if ng else []),
        compiler_params=_params())(xin, w, b, *gather)
    return outs if ng else outs[0]


def _dwconv_wgrad(name, xin, dout):
    seq = xin.shape[0]
    cb, tt = _conv_blocks(seq)

    def kern(x_ref, d_ref, dw_ref, db_ref, pad_ref):
        _fill_padded(pad_ref, x_ref, seq)
        dw_ref[...] = jnp.zeros(dw_ref.shape, F32)
        db_ref[...] = jnp.zeros(db_ref.shape, F32)

        def step(t, carry):
            base = pl.multiple_of(t * tt, tt)
            win = pad_ref[pl.ds(base, tt + 2 * CONV_HALO), :]
            d = d_ref[pl.ds(base, tt), :]
            db_ref[...] += _colsum(d)
            for j, rows in _conv_taps(win, tt):
                dw_ref[j:j + 1, :] += _colsum(d * rows)
            return carry

        lax.fori_loop(0, seq // tt, step, 0)

    return pl.pallas_call(
        kern, name=name, grid=(D_MODEL // cb,),
        in_specs=[pl.BlockSpec((seq, cb), lambda j: (0, j)), pl.BlockSpec((seq, cb), lambda j: (0, j))],
        out_specs=[pl.BlockSpec((CONV_W, cb), lambda j: (0, j)), pl.BlockSpec((1, cb), lambda j: (0, j))],
        out_shape=[jax.ShapeDtypeStruct((CONV_W, D_MODEL), F32), jax.ShapeDtypeStruct((1, D_MODEL), F32)],
        scratch_shapes=[pltpu.VMEM((seq + 2 * CONV_HALO, cb), F32)],
        compiler_params=_params())(xin, dout)


def _mm(name, a, b, out_sds, grid, a_spec, b_spec, o_spec, contract, k_axis=None, bias=None, pair_send=()):
    dn = (contract, ((), ()))
    ns = len(pair_send)
    nb = 0 if bias is None else 1
    n_steps = math.prod(grid)

    def kern(*refs):
        a_ref, b_ref = refs[:2]
        bias_ref = refs[2] if nb else None
        o_ref = refs[2 + nb + ns]
        if ns:
            copies = _pair_send_copies(refs[2 + nb:2 + nb + ns], refs[3 + nb + ns:3 + nb + 2 * ns],
                                       *refs[3 + nb + 2 * ns:])
            step = pl.program_id(0)
            for ax in range(1, len(grid)):
                step = step * grid[ax] + pl.program_id(ax)

            @pl.when(step == 0)
            def _():
                for cp in copies():
                    cp.start()

        p = lax.dot_general(a_ref[...].astype(BF16), b_ref[...].astype(BF16), dn, preferred_element_type=F32)
        if bias is not None:
            p = p + bias_ref[...]
        if k_axis is None:
            o_ref[...] = p.astype(o_ref.dtype)
        else:
            k = pl.program_id(k_axis)

            @pl.when(k == 0)
            def _():
                o_ref[...] = p

            @pl.when(k > 0)
            def _():
                o_ref[...] += p

        if ns:
            @pl.when(step == n_steps - 1)
            def _():
                for cp in copies():
                    cp.wait()

    in_specs = [a_spec, b_spec]
    args = [a, b]
    if bias is not None:
        in_specs.append(pl.BlockSpec(bias.shape, lambda *_: (0,) * bias.ndim))
        args.append(bias)
    if not ns:
        return pl.pallas_call(kern, name=name, grid=grid, in_specs=in_specs, out_specs=o_spec, out_shape=out_sds,
                              compiler_params=_params())(*args)
    return pl.pallas_call(
        kern, name=name, grid=grid, in_specs=in_specs + [_ANY] * ns, out_specs=[o_spec] + [_ANY] * ns,
        out_shape=[out_sds] + [jax.ShapeDtypeStruct((N_CHIP,) + g.shape[2:], g.dtype) for g in pair_send],
        scratch_shapes=[pltpu.SemaphoreType.DMA((ns, N_CHIP)), pltpu.SemaphoreType.DMA((ns, N_CHIP))],
        compiler_params=_params())(*args, *pair_send)


def _mm_rows(name, a, b, tm, trans_b=False, bias=None):
    m, k = a.shape
    n = b.shape[0] if trans_b else b.shape[1]
    contract = ((1,), (1,)) if trans_b else ((1,), (0,))
    return _mm(name, a, b, jax.ShapeDtypeStruct((m, n), F32), (m // tm,),
               pl.BlockSpec((tm, k), lambda i: (i, 0)), pl.BlockSpec(b.shape, lambda i: (0, 0)),
               pl.BlockSpec((tm, n), lambda i: (i, 0)), contract, bias=bias)


def _mm_glu(name, a, b, bias, tm):
    m, k = a.shape
    n = b.shape[1]

    def kern(a_ref, b_ref, bias_ref, ag_ref, hg_ref):
        v = jnp.dot(a_ref[...], b_ref[...], preferred_element_type=F32)
        ag_ref[...] = v
        v = v + bias_ref[...]
        hg_ref[...] = v[:, :n // 2] * _sigmoid(v[:, n // 2:])

    return pl.pallas_call(
        kern, name=name, grid=(m // tm,),
        in_specs=[pl.BlockSpec((tm, k), lambda i: (i, 0)), pl.BlockSpec(b.shape, lambda i: (0, 0)),
                  pl.BlockSpec((1, n), lambda i: (0, 0))],
        out_specs=[pl.BlockSpec((tm, n), lambda i: (i, 0)), pl.BlockSpec((tm, n // 2), lambda i: (i, 0))],
        out_shape=[jax.ShapeDtypeStruct((m, n), F32), jax.ShapeDtypeStruct((m, n // 2), F32)],
        compiler_params=_params())(a, b, bias)


def _mm_rows_res(name, a, b, h, gt, tm, bias=None, norm=None, final=None):
    m, k = a.shape
    n = b.shape[1]
    nb = 0 if bias is None else 1
    extra = list(norm or ()) + list(final or ())

    def kern(a_ref, b_ref, h_ref, gt_ref, *rest):
        y = jnp.dot(a_ref[...], b_ref[...], preferred_element_type=F32)
        if nb:
            y = y + rest[0][...]
        ex = rest[nb:nb + len(extra)]
        outs = rest[nb + len(extra):]
        hn = h_ref[...] + gt_ref[...] * y
        if final is None:
            outs[0][...] = y
            outs[1][...] = hn
        if norm is not None:
            nn, _ = _rms(hn)
            outs[2][...] = ((nn * ex[0][...]) * (1.0 + ex[1][...]) + ex[2][...]).astype(BF16)
        if final is not None:
            @pl.when(pl.program_id(0) == 0)
            def _():
                for o in (outs[1], outs[2], outs[4]):
                    o[...] = jnp.zeros(o.shape, F32)

            gv = ex[1][...]
            nn, r = _rms(hn)
            e = nn * gv - ex[0][...]
            outs[2][...] += _colsum(e * e) * (0.5 / D_MODEL)
            dout = e * (1.0 / D_MODEL)
            outs[1][...] += _colsum(dout * nn)
            dh = _rms_bwd(nn, r, dout * gv)
            outs[0][...] = dh
            outs[3][...] = (dh * gt_ref[...]).astype(BF16)
            outs[4][...] += _colsum(dh * y)

    row = pl.BlockSpec((tm, n), lambda i: (i, 0))
    vec = pl.BlockSpec((1, n), lambda i: (0, 0))
    sds = jax.ShapeDtypeStruct((m, n), F32)
    vsd = jax.ShapeDtypeStruct((1, n), F32)
    in_specs = [pl.BlockSpec((tm, k), lambda i: (i, 0)), pl.BlockSpec(b.shape, lambda i: (0, 0)), row, vec] + [vec] * nb
    out_specs, out_shape = [row, row], [sds, sds]
    if norm is not None:
        in_specs += [vec] * 3
        out_specs.append(row)
        out_shape.append(jax.ShapeDtypeStruct((m, n), BF16))
    if final is not None:
        in_specs += [row, vec]
        out_specs = [row, vec, vec, row, vec]
        out_shape = [sds, vsd, vsd, jax.ShapeDtypeStruct((m, n), BF16), vsd]
    return pl.pallas_call(kern, name=name, grid=(m // tm,), in_specs=in_specs, out_specs=out_specs,
                          out_shape=out_shape, compiler_params=_params())(
        a, b, h, gt, *([bias] if nb else []), *extra)


def _ffn_in_swiglu(name, xf, wblk, tm, gather=()):
    m, k = xf.shape
    nblk = wblk.shape[2]
    ng = len(gather)
    n_i = m // tm
    last = 2 * n_i - 1

    def kern(x_ref, wg_ref, wu_ref, *rest):
        gu_ref, a_ref = rest[ng:ng + 2]
        if ng:
            start, forward, finish = _gather_phases(rest[:ng], rest[ng + 2:2 * ng + 2], *rest[2 * ng + 2:])
            step = pl.program_id(0) * n_i + pl.program_id(1)
            pl.when(step == 0)(start)
            pl.when(step == (3 * last) // 4)(forward)
        xv = x_ref[...]
        g = jnp.dot(xv, wg_ref[...], preferred_element_type=F32)
        u = jnp.dot(xv, wu_ref[...], preferred_element_type=F32)
        gu_ref[0] = g.astype(BF16)
        gu_ref[1] = u.astype(BF16)
        a_ref[...] = (g * _sigmoid(g) * u).astype(BF16)
        if ng:
            pl.when(step == last)(finish)

    return pl.pallas_call(
        kern, name=name, grid=(2, n_i),
        in_specs=[pl.BlockSpec((tm, k), lambda j, i: (i, 0)), pl.BlockSpec((None, k, nblk), lambda j, i: (j, 0, 0)),
                  pl.BlockSpec((None, k, nblk), lambda j, i: (j + 2, 0, 0))] + [_ANY] * ng,
        out_specs=[pl.BlockSpec((2, tm, nblk), lambda j, i: (0, i, j)),
                   pl.BlockSpec((tm, nblk), lambda j, i: (i, j))] + [_ANY] * ng,
        out_shape=[jax.ShapeDtypeStruct((2, m, 2 * nblk), BF16), jax.ShapeDtypeStruct((m, 2 * nblk), BF16)]
        + [jax.ShapeDtypeStruct((N_DEV,) + g.shape, g.dtype) for g in gather],
        scratch_shapes=_gather_scratch(ng) if ng else [],
        compiler_params=_params())(xf, wblk, wblk, *gather)


def _ffn_out_dx_swiglu(name, dy, wo, gu, tm):
    m, k = dy.shape
    nblk = gu.shape[2] // 2

    def kern(dy_ref, w_ref, gu_ref, o_ref):
        da = lax.dot_general(dy_ref[...], w_ref[...], (((1,), (1,)), ((), ())), preferred_element_type=F32)
        g = gu_ref[0].astype(F32)
        sg = _sigmoid(g)
        o_ref[0] = (da * gu_ref[1].astype(F32) * (sg * (1.0 + g * (1.0 - sg)))).astype(BF16)
        o_ref[1] = (da * (g * sg)).astype(BF16)

    return pl.pallas_call(
        kern, name=name, grid=(2, m // tm),
        in_specs=[pl.BlockSpec((tm, k), lambda j, i: (i, 0)), pl.BlockSpec((nblk, k), lambda j, i: (j, 0)),
                  pl.BlockSpec((2, tm, nblk), lambda j, i: (0, i, j))],
        out_specs=pl.BlockSpec((2, tm, nblk), lambda j, i: (0, i, j)),
        out_shape=jax.ShapeDtypeStruct(gu.shape, BF16), compiler_params=_params())(dy, wo, gu)


def _ffn_in_dx_norm_bwd(name, dgu, wblk, h, dh_in, g, sc, tm, gate=None):
    nb, k, nblk = wblk.shape

    def dx_fn(a_ref, w_ref):
        acc = None
        for j in range(nb):
            p = lax.dot_general(a_ref[j // 2, :, (j % 2) * nblk:(j % 2 + 1) * nblk], w_ref[j],
                                (((1,), (1,)), ((), ())), preferred_element_type=F32)
            acc = p if acc is None else acc + p
        return acc

    return _mm_row_epilogue(name, [dgu, wblk],
                            [pl.BlockSpec((2, tm, 2 * nblk), lambda i: (0, i, 0)),
                             pl.BlockSpec(wblk.shape, lambda i: (0, 0, 0))], dx_fn, [(h, None), (dh_in, None)],
                            [g, sc], 3, _norm_bwd_epilogue, tm, gate=gate)


def _mm_t_epilogue(name, a, b, rows_in, vecs_in, n_vec_out, epilogue, tm, gate=None, exchange=()):
    def dx_fn(a_ref, b_ref):
        return lax.dot_general(a_ref[...], b_ref[...], (((1,), (1,)), ((), ())), preferred_element_type=F32)

    return _mm_row_epilogue(name, [a, b], [pl.BlockSpec((tm, a.shape[1]), lambda i: (i, 0)),
                                          pl.BlockSpec(b.shape, lambda i: (0, 0))], dx_fn, rows_in, vecs_in,
                            n_vec_out, epilogue, tm, gate, exchange)


def _mm_halves_wgrad_blk(name, a, b2, tk, pair_send=()):
    t, k = a.shape
    nblk = b2.shape[2] // 2
    return _mm(name, a, b2, jax.ShapeDtypeStruct((N_CHIP, k, nblk), F32), (N_CHIP, t // tk),
               pl.BlockSpec((tk, k), lambda j, i: (i, 0)),
               pl.BlockSpec((None, tk, nblk), lambda j, i: (j // 2, i, j % 2)),
               pl.BlockSpec((None, k, nblk), lambda j, i: (j, 0, 0)), ((0,), (0,)), k_axis=1, pair_send=pair_send)


def _mm_wgrad(name, a, b, tk):
    t, k = a.shape
    n = b.shape[1]
    tn = n if k * n * 4 <= 4 * 1024 * 1024 else 512
    return _mm(name, a, b, jax.ShapeDtypeStruct((k, n), F32), (n // tn, t // tk),
               pl.BlockSpec((tk, k), lambda j, i: (i, 0)), pl.BlockSpec((tk, tn), lambda j, i: (i, j)),
               pl.BlockSpec((k, tn), lambda j, i: (0, j)), ((0,), (0,)), k_axis=1)


def _attn_fwd(name, q_all, k, v, n_ctx, tq, gather=()):
    r = k.shape[1]
    s = q_all.shape[0] - n_ctx
    gw = Q_GROUP * HEAD_DIM
    ng = len(gather)
    n_i = s // tq
    last = N_KV_HEADS * n_i - 1
    forward_at = (7 * last) // 8

    def kern(q_ref, k_ref, v_ref, *rest):
        o_ref, lse_ref = rest[ng], rest[ng + 1]
        if ng:
            start, forward, finish = _gather_phases(rest[:ng], rest[ng + 2:2 * ng + 2], *rest[2 * ng + 2:])
            step = pl.program_id(0) * n_i + pl.program_id(1)
            pl.when(step == 0)(start)
            pl.when(step == forward_at)(forward)
        kv = k_ref[...]
        vv = v_ref[...]
        for g in range(Q_GROUP):
            sl = slice(g * HEAD_DIM, (g + 1) * HEAD_DIM)
            sc = lax.dot_general(q_ref[:, sl], kv, (((1,), (1,)), ((), ())), preferred_element_type=F32)
            m = jnp.max(sc, axis=-1, keepdims=True)
            p = jnp.exp((sc - m).astype(BF16))
            ol = jnp.dot(p, vv, preferred_element_type=F32)
            l = ol[:, HEAD_DIM:HEAD_DIM + 1]
            o_ref[:, sl] = ol[:, :HEAD_DIM] / l
            lse_ref[g] = m + jnp.log(l)
        if ng:
            pl.when(step == last)(finish)

    return pl.pallas_call(
        kern, name=name, grid=(N_KV_HEADS, n_i),
        in_specs=[pl.BlockSpec((tq, gw), lambda j, i: (i + n_ctx // tq, j)),
                  pl.BlockSpec((None, r, HEAD_DIM), lambda j, i: (j, 0, 0)),
                  pl.BlockSpec((None, r, 2 * HEAD_DIM), lambda j, i: (j, 0, 0))] + [_ANY] * ng,
        out_specs=[pl.BlockSpec((tq, gw), lambda j, i: (i, j)),
                   pl.BlockSpec((Q_GROUP, tq, 1), lambda j, i: (j, i, 0))] + [_ANY] * ng,
        out_shape=[jax.ShapeDtypeStruct((s, ATTN_W), F32), jax.ShapeDtypeStruct((N_Q_HEADS, s, 1), F32)]
        + [jax.ShapeDtypeStruct((N_DEV,) + a.shape, a.dtype) for a in gather],
        scratch_shapes=_gather_scratch(ng) if ng else [],
        compiler_params=_params())(q_all, k, v, *gather)


def _attn_bwd(name, q_all, k, v, o, lse, dcat, n_ctx, tq, exchange=()):
    nkv, r, _ = k.shape
    s = o.shape[0]
    gw = Q_GROUP * HEAD_DIM
    ne = len(exchange)
    n_i = s // tq
    last = nkv * n_i - 1

    def kern(q_ref, k_ref, v_ref, o_ref, lse_ref, do_ref, *rest):
        dq_ref, dk_ref, dv_ref = rest[ne:ne + 3]
        ds_all, p_all = rest[2 * ne + 3:2 * ne + 5]
        if ne:
            start, finish = _exchange_phases(rest[:ne], rest[ne + 3:2 * ne + 3], *rest[2 * ne + 5:])
            step = pl.program_id(0) * n_i + pl.program_id(1)
            pl.when(step == 0)(start)

        @pl.when(pl.program_id(1) == 0)
        def _():
            dk_ref[...] = jnp.zeros(dk_ref.shape, F32)
            dv_ref[...] = jnp.zeros(dv_ref.shape, F32)

        kv = k_ref[...]
        vv = v_ref[...]
        qs = []
        dos = []
        for g in range(Q_GROUP):
            sl = slice(g * HEAD_DIM, (g + 1) * HEAD_DIM)
            rows = slice(g * tq, (g + 1) * tq)
            qv = q_ref[:, sl]
            dov = do_ref[:, sl]
            delta = jnp.sum(dov * o_ref[:, sl], axis=-1, keepdims=True)
            sc = lax.dot_general(qv, kv, (((1,), (1,)), ((), ())), preferred_element_type=F32)
            p = jnp.exp(sc - lse_ref[g])
            dob = dov.astype(BF16)
            dp = lax.dot_general(dob, vv, (((1,), (1,)), ((), ())), preferred_element_type=F32)
            dsb = (p * (dp - delta)).astype(BF16)
            dq_ref[:, sl] = jnp.dot(dsb, kv, preferred_element_type=F32)
            ds_all[rows, :] = dsb
            p_all[rows, :] = p.astype(BF16)
            qs.append(qv)
            dos.append(dob)
        dk_ref[...] += lax.dot_general(jnp.concatenate(qs, axis=0), ds_all[...], (((0,), (0,)), ((), ())),
                                       preferred_element_type=F32)
        dv_ref[...] += lax.dot_general(jnp.concatenate(dos, axis=0), p_all[...], (((0,), (0,)), ((), ())),
                                       preferred_element_type=F32)
        if ne:
            pl.when(step == last)(finish)

    kvmap = lambda j, i: (j, 0, 0)
    return pl.pallas_call(
        kern, name=name, grid=(nkv, n_i),
        in_specs=[pl.BlockSpec((tq, gw), lambda j, i: (i + n_ctx // tq, j)), pl.BlockSpec((None, r, HEAD_DIM), kvmap),
                  pl.BlockSpec((None, r, HEAD_DIM), kvmap), pl.BlockSpec((tq, gw), lambda j, i: (i, j)),
                  pl.BlockSpec((Q_GROUP, tq, 1), lambda j, i: (j, i, 0)),
                  pl.BlockSpec((tq, gw), lambda j, i: (i, j))] + [_ANY] * ne,
        out_specs=[pl.BlockSpec((tq, gw), lambda j, i: (i, j)), pl.BlockSpec((None, HEAD_DIM, r), kvmap),
                   pl.BlockSpec((None, HEAD_DIM, r), kvmap)] + [_ANY] * ne,
        out_shape=[jax.ShapeDtypeStruct((s, ATTN_W), F32), jax.ShapeDtypeStruct((nkv, HEAD_DIM, r), F32),
                   jax.ShapeDtypeStruct((nkv, HEAD_DIM, r), F32)]
        + [jax.ShapeDtypeStruct(a.shape, a.dtype) for a in exchange],
        scratch_shapes=[pltpu.VMEM((Q_GROUP * tq, r), BF16)] * 2 + (_exchange_scratch(ne) if ne else []),
        compiler_params=_params())(q_all, k, v, o, lse, dcat, *exchange)


def _mod_fwd(name, c16, wm, bm):
    n = wm.shape[2]

    def kern(c_ref, w_ref, b_ref, o_ref):
        cv = c_ref[...]
        a = (cv * _sigmoid(cv)).astype(BF16)
        o_ref[...] = jnp.dot(a, w_ref[...].astype(BF16), preferred_element_type=F32) + b_ref[...]

    return pl.pallas_call(
        kern, name=name, grid=(2,),
        in_specs=[pl.BlockSpec(c16.shape, lambda l: (0, 0)), pl.BlockSpec((None, D_MODEL, n), lambda l: (l, 0, 0)),
                  pl.BlockSpec((None, 1, n), lambda l: (l, 0, 0))],
        out_specs=pl.BlockSpec((None, 16, n), lambda l: (l, 0, 0)),
        out_shape=jax.ShapeDtypeStruct((2, 16, n), F32), compiler_params=_params())(c16, wm, bm)


def _mod_wgrad(name, c16, dm):
    n = dm.shape[2]
    tn = 512

    def kern(c_ref, d_ref, o_ref):
        cv = c_ref[...]
        a = cv * _sigmoid(cv)
        o_ref[...] = lax.dot_general(a, d_ref[...], (((0,), (0,)), ((), ())), preferred_element_type=F32,
                                     precision=lax.Precision.HIGHEST)

    return pl.pallas_call(
        kern, name=name, grid=(2, n // tn),
        in_specs=[pl.BlockSpec(c16.shape, lambda l, j: (0, 0)), pl.BlockSpec((None, 16, tn), lambda l, j: (l, 0, j))],
        out_specs=pl.BlockSpec((None, D_MODEL, tn), lambda l, j: (l, 0, j)),
        out_shape=jax.ShapeDtypeStruct((2, D_MODEL, n), F32), compiler_params=_params())(c16, dm)


def _cctx_grad(name, parts, c_ctx):
    def kern(p_ref, c_ref, o_ref):
        d = p_ref[0, 0:1, :] + p_ref[2, 0:1, :] + p_ref[4, 0:1, :] + p_ref[6, 0:1, :]
        cv = c_ref[...]
        sg = _sigmoid(cv)
        o_ref[...] = d * (sg * (1.0 + cv * (1.0 - sg)))

    return pl.pallas_call(kern, name=name, out_shape=jax.ShapeDtypeStruct((1, D_MODEL), F32),
                          compiler_params=_params())(parts, c_ctx)


def _sum_slots(name, g, tr):
    n, rows, cols = g.shape

    def kern(g_ref, o_ref):
        acc = g_ref[0].astype(F32)
        for i in range(1, n):
            acc = acc + g_ref[i].astype(F32)
        o_ref[...] = acc

    return pl.pallas_call(kern, name=name, grid=(rows // tr,),
                          in_specs=[pl.BlockSpec((n, tr, cols), lambda i: (0, i, 0))],
                          out_specs=pl.BlockSpec((tr, cols), lambda i: (i, 0)),
                          out_shape=jax.ShapeDtypeStruct((rows, cols), F32), compiler_params=_params())(g)


def _sum_into(name, g, cvec, n_layers, layer, prev=None):
    n, r, cc = g.shape
    tr = _row_tile(r, cc, 512 * 1024)

    def kern(c_ref, g_ref, *rest):
        acc = g_ref[0].astype(F32)
        for i in range(1, n):
            acc = acc + g_ref[i].astype(F32)
        rest[-1][...] = acc

    in_specs = [pl.BlockSpec((n, tr, cc), lambda i, c_ref: (0, i, 0))]
    args = [cvec, g]
    aliases = {}
    if prev is not None:
        in_specs.append(_ANY)
        args.append(prev)
        aliases = {2: 0}
    grid_spec = pltpu.PrefetchScalarGridSpec(
        num_scalar_prefetch=1, grid=(r // tr,), in_specs=in_specs,
        out_specs=pl.BlockSpec((None, None, tr, cc), lambda i, c_ref: (layer, c_ref[0], i, 0)))
    return pl.pallas_call(kern, name=name, grid_spec=grid_spec,
                          out_shape=jax.ShapeDtypeStruct((n_layers, 2, r, cc), F32),
                          input_output_aliases=aliases, compiler_params=_params())(*args)


def _row_tile(rows, cols, max_bytes=1024 * 1024):
    if rows * cols * 4 <= 2 * max_bytes:
        return rows
    best = None
    for t in range(16, rows + 1, 16):
        if rows % t == 0 and t * cols * 4 <= max_bytes:
            best = t
    assert best is not None, (rows, cols)
    return best


def _adamw(name, w, g, m, v):
    shape = w.shape
    cols = shape[-1]
    rows = w.size // cols
    tr = _row_tile(rows, cols)

    def kern(w_ref, g_ref, m_ref, v_ref, d_ref, nm_ref, nv_ref):
        gv = g_ref[...]
        m2 = ADAM_B1 * m_ref[...] + (1.0 - ADAM_B1) * gv
        v2 = ADAM_B2 * v_ref[...] + (1.0 - ADAM_B2) * (gv * gv)
        m_hat = m2 / (1.0 - ADAM_B1 ** ADAM_STEP)
        v_hat = v2 / (1.0 - ADAM_B2 ** ADAM_STEP)
        d_ref[...] = -ADAM_LR * (m_hat / (jnp.sqrt(v_hat) + ADAM_EPS) + ADAM_WD * w_ref[...])
        nm_ref[...] = m2
        nv_ref[...] = v2

    spec = pl.BlockSpec((tr, cols), lambda i: (i, 0))
    sds = jax.ShapeDtypeStruct((rows, cols), F32)
    outs = pl.pallas_call(kern, name=name, grid=(rows // tr,), in_specs=[spec] * 4, out_specs=[spec] * 3,
                          out_shape=[sds] * 3, compiler_params=_params())(
        w.reshape(rows, cols), g.reshape(rows, cols), m.reshape(rows, cols), v.reshape(rows, cols))
    return tuple(o.reshape(shape) for o in outs)


_ANY = pl.BlockSpec(memory_space=pl.ANY)


def _mesh_pos():
    return lax.axis_index("x"), lax.axis_index("y"), lax.axis_index("c")


def _gather_phases(srcs, outs, send_sems, recv_sems, local_sems):
    n = len(srcs)
    x, y, c = _mesh_pos()
    me = (x, y, c)
    sibling = (x, y, 1 - c)
    chips = [(1 - x, y), (x, 1 - y), (1 - x, 1 - y)]

    def slot(px, py, pc):
        return 4 * px + 2 * py + pc

    def copy(t, k, s, to, src=None):
        dst = outs[t].at[s]
        return pltpu.make_async_remote_copy(src_ref=dst if src is None else src, dst_ref=dst,
                                            send_sem=send_sems.at[t, k], recv_sem=recv_sems.at[t, k],
                                            device_id=to, device_id_type=MESH)

    def mine(t):
        return pltpu.make_async_copy(srcs[t], outs[t].at[slot(*me)], local_sems.at[t])

    def first(t):
        return [copy(t, 0, slot(*me), sibling, src=srcs[t])] + [
            copy(t, 1 + j, slot(*me), (px, py, c), src=srcs[t]) for j, (px, py) in enumerate(chips)]

    def passed(t, j):
        px, py = chips[j]
        return copy(t, 4 + j, slot(px, py, c), sibling)

    def start():
        for t in range(n):
            mine(t).start()
        for t in range(n):
            for cp in first(t):
                cp.start()

    def forward():
        for j, (px, py) in enumerate(chips):
            for t in range(n):
                copy(t, 1 + j, slot(px, py, c), me).wait_recv()
                passed(t, j).start()

    def finish():
        for t in range(n):
            copy(t, 0, slot(x, y, 1 - c), me).wait_recv()
            for j, (px, py) in enumerate(chips):
                copy(t, 4 + j, slot(px, py, 1 - c), me).wait_recv()
        for t in range(n):
            for cp in first(t) + [passed(t, j) for j in range(len(chips))]:
                cp.wait_send()
            mine(t).wait()

    return start, forward, finish


def _gather_scratch(n):
    return [pltpu.SemaphoreType.DMA((n, 7)), pltpu.SemaphoreType.DMA((n, 7)), pltpu.SemaphoreType.DMA((n,))]


def _allgather(name, items):
    n = len(items)

    def body(*refs):
        start, forward, finish = _gather_phases(refs[:n], refs[n:2 * n], *refs[2 * n:])
        start()
        forward()
        finish()

    return pl.pallas_call(
        body, name=name, in_specs=[_ANY] * n, out_specs=[_ANY] * n,
        out_shape=[jax.ShapeDtypeStruct((N_DEV,) + a.shape, a.dtype) for a in items],
        scratch_shapes=_gather_scratch(n), compiler_params=_params())(*items)


def _pair_send_copies(srcs, bufs, send_sems, recv_sems):
    x, y, c = _mesh_pos()

    def copies():
        return [pltpu.make_async_remote_copy(
            src_ref=srcs[t].at[k, 1 - c], dst_ref=bufs[t].at[k], send_sem=send_sems.at[t, k],
            recv_sem=recv_sems.at[t, k], device_id=(x, y, 1 - c), device_id_type=MESH)
            for t in range(len(srcs)) for k in range(N_CHIP)]

    return copies


def _rs_pair_send(name, gs):
    n = len(gs)

    def body(*refs):
        copies = _pair_send_copies(refs[:n], refs[n:2 * n], *refs[2 * n:])
        for cp in copies():
            cp.start()
        for cp in copies():
            cp.wait()

    return pl.pallas_call(
        body, name=name, in_specs=[_ANY] * n, out_specs=[_ANY] * n,
        out_shape=[jax.ShapeDtypeStruct((N_CHIP,) + g.shape[2:], g.dtype) for g in gs],
        scratch_shapes=[pltpu.SemaphoreType.DMA((n, N_CHIP)), pltpu.SemaphoreType.DMA((n, N_CHIP))],
        compiler_params=_params())(*gs)


def _rs_pair_add(name, g, buf, cvec):
    _, _, r, cc = g.shape
    tr = _row_tile(r, cc, 2 * 1024 * 1024)

    def kern(c_ref, g_ref, b_ref, o_ref):
        o_ref[...] = (g_ref[...] + b_ref[...]).astype(BF16)

    grid_spec = pltpu.PrefetchScalarGridSpec(
        num_scalar_prefetch=1, grid=(N_CHIP, r // tr),
        in_specs=[pl.BlockSpec((None, None, tr, cc), lambda k, i, c_ref: (k, c_ref[0], i, 0)),
                  pl.BlockSpec((None, tr, cc), lambda k, i, c_ref: (k, i, 0))],
        out_specs=pl.BlockSpec((None, tr, cc), lambda k, i, c_ref: (k, i, 0)))
    return pl.pallas_call(kern, name=name, grid_spec=grid_spec,
                          out_shape=jax.ShapeDtypeStruct((N_CHIP, r, cc), BF16),
                          compiler_params=_params())(cvec, g, buf)


def _exchange_phases(srcs, bufs, send_sems, recv_sems, local_sems):
    n = len(srcs)
    x, y, c = _mesh_pos()
    kme = 2 * x + y
    chips = [(1 - x, y), (x, 1 - y), (1 - x, 1 - y)]

    def copies():
        local = [pltpu.make_async_copy(srcs[t].at[kme], bufs[t].at[kme], local_sems.at[t]) for t in range(n)]
        remote = []
        for t in range(n):
            for j, (px, py) in enumerate(chips):
                remote.append(pltpu.make_async_remote_copy(
                    src_ref=srcs[t].at[2 * px + py], dst_ref=bufs[t].at[kme], send_sem=send_sems.at[t, j],
                    recv_sem=recv_sems.at[t, j], device_id=(px, py, c), device_id_type=MESH))
        return local, remote

    def start():
        local, remote = copies()
        for cp in local + remote:
            cp.start()

    def finish():
        local, remote = copies()
        for cp in remote + local:
            cp.wait()

    return start, finish


def _exchange_scratch(n):
    return [pltpu.SemaphoreType.DMA((n, 3)), pltpu.SemaphoreType.DMA((n, 3)), pltpu.SemaphoreType.DMA((n,))]


def _rs_chip_exchange(name, ss):
    n = len(ss)

    def body(*refs):
        start, finish = _exchange_phases(refs[:n], refs[n:2 * n], *refs[2 * n:])
        start()
        finish()

    return pl.pallas_call(
        body, name=name, in_specs=[_ANY] * n, out_specs=[_ANY] * n,
        out_shape=[jax.ShapeDtypeStruct(s.shape, s.dtype) for s in ss],
        scratch_shapes=_exchange_scratch(n), compiler_params=_params())(*ss)


def _rs_pair_share(name, dsts, gather=()):
    nd = len(dsts)
    ng = len(gather)

    def body(*refs):
        ins = refs[:nd]
        outs = refs[nd + ng:2 * nd + ng]
        send_sems, recv_sems = refs[2 * (nd + ng):2 * (nd + ng) + 2]
        if ng:
            start, forward, finish = _gather_phases(refs[nd:nd + ng], refs[2 * nd + ng:2 * (nd + ng)],
                                                    *refs[2 * (nd + ng) + 2:])
            start()
        x, y, c = _mesh_pos()
        copies = []
        for d in range(nd):
            for l in range(dsts[d].shape[0]):
                copies.append(pltpu.make_async_remote_copy(
                    src_ref=ins[d].at[l, c], dst_ref=outs[d].at[l, c], send_sem=send_sems.at[d, l],
                    recv_sem=recv_sems.at[d, l], device_id=(x, y, 1 - c), device_id_type=MESH))
        for cp in copies:
            cp.start()
        if ng:
            forward()
        for cp in copies:
            cp.wait()
        if ng:
            finish()

    return pl.pallas_call(
        body, name=name, in_specs=[_ANY] * (nd + ng), out_specs=[_ANY] * (nd + ng),
        out_shape=[jax.ShapeDtypeStruct(a.shape, a.dtype) for a in dsts]
        + [jax.ShapeDtypeStruct((N_DEV,) + a.shape, a.dtype) for a in gather],
        input_output_aliases={d: d for d in range(nd)},
        scratch_shapes=[pltpu.SemaphoreType.DMA((nd, 2)), pltpu.SemaphoreType.DMA((nd, 2))]
        + (_gather_scratch(ng) if ng else []),
        compiler_params=_params())(*dsts, *gather)


def _rope_tables(seq, n_ctx):
    t = jnp.arange(seq)
    row = (t // GRID_W).astype(F32)
    col = (t % GRID_W).astype(F32)
    inv = ROPE_THETA ** (-jnp.arange(0, HEAD_DIM // 2, 2, dtype=F32) / (HEAD_DIM // 2))
    ang_r = row[:, None] * inv[None, :]
    ang_c = col[:, None] * inv[None, :]
    cos = jnp.concatenate([jnp.cos(ang_r)] * 2 + [jnp.cos(ang_c)] * 2, axis=1)
    sin = jnp.concatenate([-jnp.sin(ang_r), jnp.sin(ang_r), -jnp.sin(ang_c), jnp.sin(ang_c)], axis=1)
    cos = jnp.concatenate([jnp.ones((n_ctx, HEAD_DIM), F32), cos], axis=0)
    sin = jnp.concatenate([jnp.zeros((n_ctx, HEAD_DIM), F32), sin], axis=0)
    return jnp.tile(cos, (1, N_KV_HEADS)), jnp.tile(sin, (1, N_KV_HEADS))


def _to_heads(a, nh):
    return a.reshape(a.shape[0], nh, HEAD_DIM).transpose(1, 0, 2)


def _sample_step(x, ctx, tgt, mod, cmod, W, pending, cvec):
    W = dict(W)
    seq = x.shape[0]
    n_ctx = ctx.shape[0]
    tm = min(256, seq)
    tmc = min(256, n_ctx)
    tmm = 512 if seq % 512 == 0 else tm
    tmw = 1024 if seq % 1024 == 0 else tmm
    tmr = (n_ctx + seq) // 4 if (n_ctx + seq) % 64 == 0 else tm
    tmk = 2048 if seq % 2048 == 0 else tmw
    tq = min(256, seq)
    assert n_ctx % tm == 0 and seq % tm == 0

    def mv(l, j):
        return mod[l, j * D_MODEL:(j + 1) * D_MODEL].reshape(1, D_MODEL)

    csh1 = cmod[:D_MODEL].reshape(1, D_MODEL)
    csc1 = cmod[D_MODEL:2 * D_MODEL].reshape(1, D_MODEL)
    g_mix = [W['g_mix'][l:l + 1] for l in range(2)]
    g_ffn = [W['g_ffn'][l:l + 1] for l in range(2)]
    cos, sin = _rope_tables(seq, n_ctx)
    gq = jnp.tile(W['q_gain'], (1, N_Q_HEADS))
    gk = jnp.tile(W['k_gain'], (1, N_KV_HEADS))
    bdq = jnp.kron(jnp.eye(N_Q_HEADS, dtype=F32), jnp.ones((HEAD_DIM, HEAD_DIM), F32)).astype(BF16)
    bdk = bdq[:KV_W, :KV_W]
    w_sp = W['w_sp'].astype(BF16)
    w_sp_t = w_sp.transpose(0, 2, 1)
    bsp = jnp.broadcast_to(W['b_sp'].T[:, :, None], (CHUNK, N_SG, CHUNK)).reshape(CHUNK, SG_W)

    def ffn_fwd(l, h_mid, xf, gather=(), **tail):
        gu, a, *gathered = _ffn_in_swiglu(f'l{l}_ffn_in', xf, W['ffn_in'][l], tmw, gather=gather)
        if gathered:
            W['pw2'] = gathered[0].reshape(D_MODEL, D_MODEL)
            W['ffn_out'][1] = gathered[1].reshape(D_FF, D_MODEL)
        outs = _mm_rows_res(f'l{l}_ffn_out', a, W['ffn_out'][l], h_mid, mv(l, 5), tmm, **tail)
        return outs, (h_mid, xf, gu, a)

    def blocked(a):
        cols = a.shape[-1]
        return a.reshape(N_CHIP, 2, a.size // (2 * N_CHIP * cols), cols)

    def ffn_bwd(l, dh_out, dy, y_mix, saved, pair_send=()):
        h_mid, xf, gu, a = saved
        dgu = _ffn_out_dx_swiglu(f'l{l}_ffn_out_dx', dy, W['ffn_out'][l], gu, tmw)
        d_wo = _mm_wgrad(f'l{l}_ffn_out_dw', a, dy, tmw)
        d_wi = _mm_halves_wgrad_blk(f'l{l}_ffn_in_dw', xf, dgu, tmk, pair_send=pair_send)
        if pair_send:
            d_wi, *sent = d_wi
        else:
            sent = []
        dh_mid, dy_mix, d_sh2, d_sc2, d_g, d_gt1, sum_dy = _ffn_in_dx_norm_bwd(
            f'l{l}_ffn_in_dx', dgu, W['ffn_in'][l], h_mid, dh_out, g_ffn[l], mv(l, 4), tmm, gate=(y_mix, mv(l, 2)))
        return dh_mid, dy_mix, d_wi, d_wo, d_g, (d_sh2, d_sc2), d_gt1, sum_dy, sent

    xm0 = _modnorm_fwd('l0_mix_norm', x, g_mix[0], mv(0, 1), mv(0, 0), tm)
    xc0 = _modnorm_fwd('l0_ctx_norm', ctx, g_mix[0], csc1, csh1, tmc)
    xall = jnp.concatenate([xc0, xm0], axis=0)
    p_qkv, q_all, k_all, v_all = _mm_qk_prep('l0_qkv', xall, W['w_qkv'], cos, sin, gq, gk, bdq, bdk, tmr)
    p_sg, sg = _mm_sg('l0_sg_in', xm0, W['w_sgp'], w_sp, bsp, tmm)
    kh = _to_heads(k_all, N_KV_HEADS)
    vh = _to_heads(v_all, N_KV_HEADS)
    v_ones = jnp.concatenate([vh, jnp.ones(vh.shape[:2] + (1,), BF16),
                              jnp.zeros(vh.shape[:2] + (HEAD_DIM - 1,), BF16)], axis=2)
    o, lse, *gw = _attn_fwd('l0_attn', q_all, kh, v_ones, n_ctx, tq, gather=pending[:4])
    W['w_out'] = gw[0].reshape(D_MODEL, D_MODEL)
    W['ffn_in'] = [gw[1].reshape(N_CHIP, D_MODEL, FF_BLK), None]
    W['ffn_out'] = [gw[2].reshape(D_FF, D_MODEL), None]
    W['pw1'] = gw[3].reshape(N_CHIP, D_MODEL, PW1_BLK).transpose(1, 0, 2).reshape(D_MODEL, 2 * D_MODEL)
    cat = jnp.concatenate([o.astype(BF16), sg], axis=1)
    y0, h1, xf0 = _mm_rows_res('l0_out', cat, W['w_out'], x, mv(0, 2), tmw, norm=(g_ffn[0], mv(0, 4), mv(0, 3)))
    (f0, h2, xm1), ffn0 = ffn_fwd(0, h1, xf0, gather=pending[5:], norm=(g_mix[1], mv(1, 1), mv(1, 0)))

    ag, hg = _mm_glu('l1_pw1', xm1, W['pw1'], W['b_pw1'], tmm)
    hc, g_ffn_in1 = _dwconv('l1_conv', hg, W['w_dw'], W['b_dw'], gather=pending[4:5])
    W['ffn_in'][1] = g_ffn_in1.reshape(N_CHIP, D_MODEL, FF_BLK)
    s1 = _lnsilu_fwd('l1_ln_silu', hc, W['ln_g'], W['ln_b'], tm)
    y1, h3, xf1 = _mm_rows_res('l1_pw2', s1, W['pw2'], h2, mv(1, 2), tmw, bias=W['b_pw2'],
                               norm=(g_ffn[1], mv(1, 4), mv(1, 3)))
    (dh4, d_g_final, loss_vec, dyf1, d_gt2_1), ffn1 = ffn_fwd(1, h3, xf1, final=(tgt, W['g_final']))

    dh3, dy1, d_wi1, d_wo1, d_gffn1, dmod1_ffn, d_gt1_1, d_b_pw2, _ = ffn_bwd(1, dh4, dyf1, y1, ffn1)
    d_pw2 = _mm_wgrad('l1_pw2_dw', s1, dy1, tmw)
    dhc, d_ln_g, d_ln_b = _mm_t_epilogue('l1_pw2_dx', dy1, W['pw2'], [(hc, None)], [W['ln_g'], W['ln_b']], 2,
                                         _lnsilu_bwd_epilogue, tmm)
    dhg = _dwconv('l1_conv_dx', dhc, W['w_dw'][::-1], jnp.zeros((1, D_MODEL), F32))
    d_w_dw, d_b_dw = _dwconv_wgrad('l1_conv_dw', hg, dhc)
    dag, d_b_pw1 = _glu_bwd('l1_glu_bwd', ag, dhg, W['b_pw1'], tm)
    d_pw1 = _mm_wgrad('l1_pw1_dw', xm1, dag, tmw).reshape(D_MODEL, N_CHIP, PW1_BLK).transpose(1, 0, 2)
    dh2, dyf0, d_sh1_1, d_sc1_1, d_gmix1, d_gt2_0, _ = _mm_t_epilogue(
        'l1_pw1_dx', dag, W['pw1'], [(h2, None), (dh3, None)], [g_mix[1], mv(1, 1)], 3, _norm_bwd_epilogue, tmm,
        gate=(f0, mv(0, 5)))

    gs1 = [blocked(g) for g in (d_wi1, d_wo1, d_pw1, d_pw2)]
    dh1, dy0, d_wi0, d_wo0, d_gffn0, dmod0_ffn, d_gt1_0, _, sib1 = ffn_bwd(0, dh2, dyf0, y0, ffn0, pair_send=gs1)
    dcat = _mm_rows('l0_out_dx', dy0, W['w_out'], tmw, trans_b=True)
    d_w_out = _mm_wgrad('l0_out_dw', cat, dy0, tmw)

    gs0 = [blocked(g) for g in (d_wi0, d_wo0, d_w_out)]
    sib0 = _rs_pair_send('rs_pair_send', gs0)
    gs = [gs0[0], gs1[0], gs0[1], gs1[1], gs0[2], gs1[2], gs1[3]]
    sib = [sib0[0], sib1[0], sib0[1], sib1[1], sib0[2], sib1[2], sib1[3]]
    ss = [_rs_pair_add(f'rs_pair_add{t}', gs[t], sib[t], cvec) for t in range(len(gs))]
    dq, dk, dv, *xs = _attn_bwd('l0_attn_bwd', q_all, kh, vh, o, lse, dcat, n_ctx, tq, exchange=ss)
    dk, dv = [t.transpose(2, 0, 1).reshape(n_ctx + seq, KV_W) for t in (dk, dv)]
    dp_qkv, d_gq, d_gk = _qk_prep_bwd('l0_qk_prep_bwd', p_qkv, cos, sin, dq, n_ctx // tm, dk, dv, gq, gk, bdq, bdk,
                                      tm)
    dp_sg, d_w_sp, d_bsp = _sg_bwd('l0_sg_bwd', p_sg, dcat, w_sp, w_sp_t, bsp, tm)
    d_w_qkv = _mm_wgrad('l0_qkv_dw', xall, dp_qkv, tmr)
    d_w_sgp = _mm_wgrad('l0_sg_in_dw', xm0, dp_sg, tmw)
    dxall = _mm_rows('l0_qkv_dx', dp_qkv, W['w_qkv'], tmr, trans_b=True)
    g_in = blocked(jnp.concatenate([d_w_qkv, d_w_sgp], axis=1).reshape(D_MODEL, N_CHIP, -1).transpose(1, 0, 2))
    sib_in = _rs_pair_send('rs_pair_send_w_in', [g_in])
    ss_in = _rs_pair_add('rs_pair_add_w_in', g_in, sib_in[0], cvec)
    dx, d_sh1_0, d_sc1_0, d_gmix0, *xs_in = _mm_t_epilogue(
        'l0_sg_in_dx', dp_sg, W['w_sgp'], [(x, None), (dh1, None), (dxall, lambda i: i + n_ctx // tm)],
        [g_mix[0], mv(0, 1)], 3, _norm_bwd_epilogue, tm, exchange=[ss_in])
    _, d_csh1, d_csc1, d_gmix0c = _modnorm_bwd('l0_ctx_norm_bwd', ctx, [_ri(dxall)], None, g_mix[0], csc1, tmc)

    zero = jnp.zeros((1, D_MODEL), F32)
    d = D_MODEL
    pack = jnp.concatenate([
        d_gmix0 + d_gmix0c, d_gmix1, d_gffn0, d_gffn1, d_g_final,
        jnp.concatenate([d_gq, d_gk, jnp.zeros((1, d - ATTN_W - KV_W), F32)], axis=1),
        d_b_pw1.reshape(2, d), d_w_dw, d_b_dw, d_ln_g, d_ln_b, d_b_pw2,
        d_sh1_0, d_sc1_0, d_gt1_0, *dmod0_ffn, d_gt2_0, d_sh1_1, d_sc1_1, d_gt1_1, *dmod1_ffn, d_gt2_1,
        d_csh1, d_csc1, zero, zero, zero, zero, jnp.zeros((3, d), F32),
        d_w_sp.reshape(64, d),
        jnp.pad(d_bsp.reshape(CHUNK, N_SG, CHUNK).sum(axis=-1).reshape(1, CHUNK * N_SG), ((0, 7), (0, d - SG_W)))],
        axis=0)
    assert pack.shape == (SMALL_ROWS, d)

    t_ffn_in = _sum_into('rs_sum_ffn_in1', xs[1], cvec, 2, 1, prev=_sum_into('rs_sum_ffn_in0', xs[0], cvec, 2, 0))
    t_ffn_out = _sum_into('rs_sum_ffn_out1', xs[3], cvec, 2, 1, prev=_sum_into('rs_sum_ffn_out0', xs[2], cvec, 2, 0))
    *reduced, small = _rs_pair_share('rs_pair_share', [
        t_ffn_in, t_ffn_out, _sum_into('rs_sum_w_in', xs_in[0], cvec, 1, 0), _sum_into('rs_sum_w_out', xs[4], cvec, 1, 0),
        _sum_into('rs_sum_pw1', xs[5], cvec, 1, 0), _sum_into('rs_sum_pw2', xs[6], cvec, 1, 0)], gather=[pack])
    big = dict(zip(['w_ffn_in', 'w_ffn_out', 'w_in', 'w_out', 'w_pw1', 'w_pw2'], reduced))
    return loss_vec, dx, small, big


def kernel(x, c, ctx, c_ctx, w_mod, b_mod, g_mix, g_ffn, w_ffn_in, w_ffn_out, w_in, q_gain, k_gain, w_sp, b_sp, w_out, w_pw1, b_pw1, w_dw, b_dw, ln_g, ln_b, w_pw2, b_pw2, g_final, loss_target, m_c_ctx, m_w_mod, m_b_mod, m_g_mix, m_g_ffn, m_w_ffn_in, m_w_ffn_out, m_w_in, m_q_gain, m_k_gain, m_w_sp, m_b_sp, m_w_out, m_w_pw1, m_b_pw1, m_w_dw, m_b_dw, m_ln_g, m_ln_b, m_w_pw2, m_b_pw2, m_g_final, v_c_ctx, v_w_mod, v_b_mod, v_g_mix, v_g_ffn, v_w_ffn_in, v_w_ffn_out, v_w_in, v_q_gain, v_k_gain, v_w_sp, v_b_sp, v_w_out, v_w_pw1, v_b_pw1, v_w_dw, v_b_dw, v_ln_g, v_ln_b, v_w_pw2, v_b_pw2, v_g_final):
    given = dict(locals())
    ix, iy, ic = _mesh_pos()
    chip = 2 * ix + iy
    me = 2 * chip + ic
    d = D_MODEL
    q4 = d // N_CHIP

    small = jnp.concatenate([c.reshape(4, q4), b_pw1.reshape(2, q4), w_dw[0], b_dw, ln_g, ln_b, b_pw2,
                             jnp.zeros((7, q4), F32)], axis=0)
    def my_half(a):
        r = a.shape[0] // 2
        return lax.dynamic_slice_in_dim(a, ic * r, r, axis=0).astype(BF16)

    sgath, g_w_in = _allgather('ag_first', [small, my_half(w_in[0])])
    c_all = sgath[:, 0:4].reshape(N_DEV, d)
    per_chip = sgath[0::2]
    b_pw1_f = per_chip[:, 4:6].reshape(1, 2 * d)
    w_dw_f = per_chip[:, 6:6 + CONV_W].transpose(1, 0, 2).reshape(CONV_W, d)
    b_dw_f, ln_g_f, ln_b_f, b_pw2_f = [per_chip[:, 37 + i].reshape(1, d) for i in range(4)]

    c16 = jnp.concatenate([c_all, c_ctx[None, :], jnp.zeros((7, d), F32)], axis=0)
    bm_sh = lax.dynamic_slice_in_dim(b_mod, chip * MOD_BLK, MOD_BLK, axis=1).reshape(2, 1, MOD_BLK)
    mod_piece = _mod_fwd('mod_fwd', c16, w_mod, bm_sh)
    mgath = _allgather('ag_mod', [mod_piece.reshape(32, MOD_BLK)])[0]
    mod_all = mgath[0::2].reshape(N_CHIP, 2, 16, MOD_BLK).transpose(1, 2, 0, 3).reshape(2, 16, MOD_W)
    mod_me = lax.dynamic_index_in_dim(mod_all, me, axis=1, keepdims=False)
    cmod = mod_all[0, N_DEV]

    w_in_f = g_w_in.reshape(N_CHIP, d, -1).transpose(1, 0, 2).reshape(d, -1)
    W = dict(
        w_qkv=w_in_f[:, :QKV_W], w_sgp=w_in_f[:, QKV_W:],
        g_mix=g_mix, g_ffn=g_ffn, g_final=g_final.reshape(1, d), q_gain=q_gain, k_gain=k_gain, w_sp=w_sp[0],
        b_sp=b_sp[0], b_pw1=b_pw1_f, w_dw=w_dw_f, b_dw=b_dw_f, ln_g=ln_g_f, ln_b=ln_b_f, b_pw2=b_pw2_f)
    pending = [my_half(w_out[0]), my_half(w_ffn_in[0]), my_half(w_ffn_out[0]), my_half(w_pw1[0]),
               my_half(w_ffn_in[1]), my_half(w_pw2[0]), my_half(w_ffn_out[1])]
    cvec = jnp.reshape(ic, (1,)).astype(jnp.int32)

    loss_vec, dx, pg, big = _sample_step(x[0], ctx[0], loss_target[0], mod_me, cmod, W, pending, cvec)
    loss = lax.psum(jnp.sum(loss_vec), ("x", "y", "c"))

    s8 = _sum_slots('sum_small_grads', pg, SMALL_ROWS)
    grads = dict(big)

    def chip_cols(a, width):
        return lax.dynamic_slice_in_dim(a, chip * width, width, axis=a.ndim - 1)

    grads['g_mix'] = s8[0:2]
    grads['g_ffn'] = s8[2:4]
    grads['g_final'] = s8[4]
    grads['q_gain'] = s8[5, :ATTN_W].reshape(N_Q_HEADS, HEAD_DIM).sum(axis=0)[None, :]
    grads['k_gain'] = s8[5, ATTN_W:ATTN_W + KV_W].reshape(N_KV_HEADS, HEAD_DIM).sum(axis=0)[None, :]
    grads['b_pw1'] = chip_cols(s8[6:8].reshape(1, 2 * d), 2 * q4)
    grads['w_dw'] = chip_cols(s8[8:8 + CONV_W], q4)[None]
    grads['b_dw'] = chip_cols(s8[39:40], q4)
    grads['ln_g'] = chip_cols(s8[40:41], q4)
    grads['ln_b'] = chip_cols(s8[41:42], q4)
    grads['b_pw2'] = chip_cols(s8[42:43], q4)
    grads['w_sp'] = s8[64:128].reshape(1, N_SG, CHUNK, CHUNK)
    grads['b_sp'] = s8[128, :SG_W].reshape(CHUNK, N_SG).T[None]

    dm_rows = pg[:, 43:55].reshape(N_DEV, 2, MOD_W)
    ctx_row = s8[55:61].reshape(1, MOD_W)
    dm = jnp.stack([jnp.concatenate([dm_rows[:, 0], ctx_row, jnp.zeros((7, MOD_W), F32)], axis=0),
                    jnp.concatenate([dm_rows[:, 1], jnp.zeros((8, MOD_W), F32)], axis=0)], axis=0)
    grads['b_mod'] = jnp.stack([s8[43:49].reshape(MOD_W) + ctx_row[0], s8[49:55].reshape(MOD_W)], axis=0)
    dm_sh = chip_cols(dm, MOD_BLK)
    grads['w_mod'] = _mod_wgrad('mod_dw', c16, dm_sh)
    d_silu = _mm_rows('mod_dx', dm_sh[0], w_mod[0], 16, trans_b=True)
    parts = _allgather('ag_cctx', [d_silu[8:16]])[0]
    grads['c_ctx'] = _cctx_grad('cctx_grad', parts, c_ctx.reshape(1, d))[0]

    deltas, new_m, new_v = {}, {}, {}
    for n in WEIGHT_NAMES:
        grads[n] = grads[n].reshape(given[n].shape)
        deltas[n], new_m[n], new_v[n] = _adamw(f'adamw_{n}', given[n], grads[n], given['m_' + n], given['v_' + n])

    return (loss, dx[None], *[grads[n] for n in WEIGHT_NAMES], *[deltas[n] for n in WEIGHT_NAMES],
            *[new_m[n] for n in WEIGHT_NAMES], *[new_v[n] for n in WEIGHT_NAMES])
```

```python
import functools
import math

import jax
import jax.numpy as jnp
from jax import lax
from jax.experimental import pallas as pl
from jax.experimental.pallas import tpu as pltpu

F32 = jnp.float32
BF16 = jnp.bfloat16
MESH = pl.DeviceIdType.MESH

D_MODEL = 1024
HEAD_DIM = 64
N_Q_HEADS = 8
N_KV_HEADS = 2
Q_GROUP = N_Q_HEADS // N_KV_HEADS
ATTN_W = N_Q_HEADS * HEAD_DIM
KV_W = N_KV_HEADS * HEAD_DIM
QKV_W = ATTN_W + 2 * KV_W
SG_W = D_MODEL - ATTN_W
N_SG = 4
CHUNK = 128
GRID_W = 64
ROPE_THETA = 10000.0
CONV_W = 31
CONV_HALO = 16
D_FF = 2816
MOD_W = 6 * D_MODEL
N_CHIP = 4
N_DEV = 8
FF_BLK = 2 * D_FF // N_CHIP
PW1_BLK = 2 * D_MODEL // N_CHIP
MOD_BLK = MOD_W // N_CHIP
EPS = 1e-6
GELU_C = math.sqrt(2.0 / math.pi)
GELU_A = 0.044715

ADAM_LR = 0.001
ADAM_B1 = 0.9
ADAM_B2 = 0.999
ADAM_EPS = 1e-08
ADAM_WD = 0.01
ADAM_STEP = 10

SMALL_ROWS = 136

VMEM_LIMIT_BYTES = 56 * 1024 * 1024

WEIGHT_NAMES = ['c_ctx', 'w_mod', 'b_mod', 'g_mix', 'g_ffn', 'w_ffn_in', 'w_ffn_out', 'w_in', 'q_gain', 'k_gain',
                'w_sp', 'b_sp', 'w_out', 'w_pw1', 'b_pw1', 'w_dw', 'b_dw', 'ln_g', 'ln_b', 'w_pw2', 'b_pw2',
                'g_final']


def _params():
    return pltpu.CompilerParams(vmem_limit_bytes=VMEM_LIMIT_BYTES)


def _ri(arr, width=None, col_block=0, row_off=0, row_fn=None):
    return (arr, arr.shape[1] if width is None else width, col_block,
            (lambda i: i + row_off) if row_fn is None else row_fn)


def _rowwise(name, body, n_tiles, tm, row_ins, vec_ins, row_outs, vec_outs):
    nri, nvi, nro = len(row_ins), len(vec_ins), len(row_outs)

    def kern(*refs):
        ri = refs[:nri]
        vi = refs[nri:nri + nvi]
        ro = refs[nri + nvi:nri + nvi + nro]
        vo = refs[nri + nvi + nro:]
        if vo:
            @pl.when(pl.program_id(0) == 0)
            def _():
                for r in vo:
                    r[...] = jnp.zeros(r.shape, r.dtype)
        body(ri, vi, ro, vo)

    in_specs = [pl.BlockSpec((tm, bw), lambda i, cb=cb, rf=rf: (rf(i), cb)) for (_, bw, cb, rf) in row_ins]
    in_specs += [pl.BlockSpec(a.shape, lambda i, nd=a.ndim: (0,) * nd) for a in vec_ins]
    out_specs = [pl.BlockSpec((tm, w), lambda i: (i, 0)) for (w, _) in row_outs]
    out_specs += [pl.BlockSpec(s, lambda i, nd=len(s): (0,) * nd) for s in vec_outs]
    out_shape = [jax.ShapeDtypeStruct((n_tiles * tm, w), dt) for (w, dt) in row_outs]
    out_shape += [jax.ShapeDtypeStruct(s, F32) for s in vec_outs]
    return pl.pallas_call(kern, name=name, grid=(n_tiles,), in_specs=in_specs, out_specs=out_specs,
                          out_shape=out_shape, compiler_params=_params())(*[a for (a, _, _, _) in row_ins], *vec_ins)


def _colsum(v):
    return jnp.sum(v, axis=0, keepdims=True)


def _rowmean(v):
    return jnp.mean(v, axis=-1, keepdims=True)


def _rms(h):
    r = lax.rsqrt(_rowmean(h * h) + EPS)
    return h * r, r


def _rms_bwd(n, r, gy):
    return r * (gy - n * _rowmean(gy * n))


def _ln_stats(v):
    xc = v - _rowmean(v)
    r = lax.rsqrt(_rowmean(xc * xc) + EPS)
    return xc * r, r


def _ln_bwd(xh, r, dxh):
    return r * (dxh - _rowmean(dxh) - xh * _rowmean(dxh * xh))


def _sigmoid(v):
    return 1.0 / (1.0 + jnp.exp(-v))


def _gelu(v):
    t = jnp.tanh(GELU_C * (v + GELU_A * (v * v * v)))
    return 0.5 * v * (1.0 + t), t


def _gelu_grad(v, t):
    return 0.5 * (1.0 + t) + 0.5 * v * (1.0 - t * t) * (GELU_C * (1.0 + 3.0 * GELU_A * (v * v)))


def _modnorm_fwd(name, h, g, sc, sh, tm):
    def body(ri, vi, ro, vo):
        n, _ = _rms(ri[0][...])
        ro[0][...] = ((n * vi[0][...]) * (1.0 + vi[1][...]) + vi[2][...]).astype(BF16)

    return _rowwise(name, body, h.shape[0] // tm, tm, [_ri(h)], [g, sc, sh], [(D_MODEL, BF16)], [])[0]


def _modnorm_bwd_tile(hv, dxm, gv, scv):
    n, r = _rms(hv)
    dy = dxm * (1.0 + scv)
    return _rms_bwd(n, r, dy * gv), _colsum(dxm), _colsum(dxm * (n * gv)), _colsum(dy * n)


def _mm_row_epilogue(name, mm_args, mm_specs, dx_fn, rows_in, vecs_in, n_vec_out, epilogue, tm, gate=None,
                     exchange=()):
    rows_in = list(rows_in) + ([(gate[0], None)] if gate else [])
    vecs_in = list(vecs_in) + ([gate[1]] if gate else [])
    ng = 1 if gate else 0
    n_vec_out += 2 * ng
    nmm, nr, nv, ne = len(mm_args), len(rows_in), len(vecs_in), len(exchange)
    m = rows_in[0][0].shape[0]
    last = m // tm - 1

    def kern(*refs):
        ins = refs[:nmm + nr + nv]
        o_ref = refs[nmm + nr + nv + ne]
        vo = refs[nmm + nr + nv + ne + 1 + ng:nmm + nr + nv + ne + 1 + ng + n_vec_out]
        if ne:
            srcs = refs[nmm + nr + nv:nmm + nr + nv + ne]
            rest = refs[nmm + nr + nv + ne + 1 + ng + n_vec_out:]
            start, finish = _exchange_phases(srcs, rest[:ne], *rest[ne:])
            pl.when(pl.program_id(0) == 0)(start)

        @pl.when(pl.program_id(0) == 0)
        def _():
            for r in vo:
                r[...] = jnp.zeros(r.shape, F32)

        row_out, incs = epilogue(dx_fn(*ins[:nmm]), [r[...] for r in ins[nmm:nmm + nr - ng]],
                                 [r[...] for r in ins[nmm + nr:nmm + nr + nv - ng]])
        o_ref[...] = row_out
        if ng:
            dy = row_out * ins[nmm + nr + nv - 1][...]
            refs[nmm + nr + nv + ne + 1][...] = dy.astype(BF16)
            incs = list(incs) + [_colsum(row_out * ins[nmm + nr - 1][...]), _colsum(dy)]
        for r, inc in zip(vo, incs):
            r[...] += inc
        if ne:
            pl.when(pl.program_id(0) == last)(finish)

    row = pl.BlockSpec((tm, D_MODEL), lambda i: (i, 0))
    vec = pl.BlockSpec((1, D_MODEL), lambda i: (0, 0))
    row_specs = [row if fn is None else pl.BlockSpec((tm, D_MODEL), lambda i, fn=fn: (fn(i), 0)) for _, fn in rows_in]
    return pl.pallas_call(
        kern, name=name, grid=(m // tm,), in_specs=list(mm_specs) + row_specs + [vec] * nv + [_ANY] * ne,
        out_specs=[row] * (1 + ng) + [vec] * n_vec_out + [_ANY] * ne,
        out_shape=[jax.ShapeDtypeStruct((m, D_MODEL), F32)] + [jax.ShapeDtypeStruct((m, D_MODEL), BF16)] * ng
        + [jax.ShapeDtypeStruct((1, D_MODEL), F32)] * n_vec_out
        + [jax.ShapeDtypeStruct(e.shape, e.dtype) for e in exchange],
        scratch_shapes=_exchange_scratch(ne) if ne else [],
        compiler_params=_params())(*mm_args, *[r for r, _ in rows_in], *vecs_in, *exchange)


def _norm_bwd_epilogue(dxm, rows, vecs):
    if len(rows) > 2:
        dxm = dxm + rows[2]
    dx, dsh, dsc, dg = _modnorm_bwd_tile(rows[0], dxm, vecs[0], vecs[1])
    return rows[1] + dx, [dsh, dsc, dg]


def _lnsilu_bwd_epilogue(ds, rows, vecs):
    xh, r = _ln_stats(rows[0])
    ln = xh * vecs[0] + vecs[1]
    sg = _sigmoid(ln)
    dln = ds * (sg * (1.0 + ln * (1.0 - sg)))
    return _ln_bwd(xh, r, dln * vecs[0]), [_colsum(dln * xh), _colsum(dln)]


def _modnorm_bwd(name, h, dxs, dh_in, g, sc, tm):
    ndx = len(dxs)

    def body(ri, vi, ro, vo):
        dxm = ri[1][...].astype(F32)
        for r in ri[2:1 + ndx]:
            dxm = dxm + r[...].astype(F32)
        dx, dsh, dsc, dg = _modnorm_bwd_tile(ri[0][...], dxm, vi[0][...], vi[1][...])
        vo[0][...] += dsh
        vo[1][...] += dsc
        vo[2][...] += dg
        if dh_in is not None:
            ro[0][...] = ri[1 + ndx][...] + dx

    row_ins = [_ri(h)] + list(dxs) + ([_ri(dh_in)] if dh_in is not None else [])
    row_outs = [(D_MODEL, F32)] if dh_in is not None else []
    outs = _rowwise(name, body, h.shape[0] // tm, tm, row_ins, [g, sc], row_outs, [(1, D_MODEL)] * 3)
    if dh_in is None:
        return (None, *outs)
    return tuple(outs)


def _head_mean(v, bd):
    hi = v.astype(BF16)
    lo = (v - hi.astype(F32)).astype(BF16)
    s = jnp.dot(hi, bd, preferred_element_type=F32) + jnp.dot(lo, bd, preferred_element_type=F32)
    return s * (1.0 / HEAD_DIM)


def _swap16(v):
    w = v.shape[1]
    lane = lax.broadcasted_iota(jnp.int32, v.shape, 1)
    return jnp.where((lane & 16) == 0, pltpu.roll(v, w - 16, 1), pltpu.roll(v, 16, 1))


def _rope(v, cos, sin):
    return v * cos + _swap16(v) * sin


def _rope_bwd(d, cos, sin):
    return d * cos + _swap16(d * sin)


def _q_wide(t):
    return jnp.concatenate([t] * (ATTN_W // KV_W), axis=1)


def _mm_qk_prep(name, xall, w_qkv, cos, sin, gq, gk, bdq, bdk, tm):
    r, k = xall.shape

    def kern(x_ref, w_ref, cos_ref, sin_ref, gq_ref, gk_ref, bdq_ref, bdk_ref, p_ref, q_ref, k_ref, v_ref):
        p = jnp.dot(x_ref[...], w_ref[...], preferred_element_type=F32)
        p_ref[...] = p
        cosv = cos_ref[...]
        sinv = sin_ref[...]
        pq = p[:, :ATTN_W]
        rq = lax.rsqrt(_head_mean(pq * pq, bdq_ref[...]) + EPS)
        q_ref[...] = (_rope(pq * rq * gq_ref[...], _q_wide(cosv), _q_wide(sinv)) * (HEAD_DIM ** -0.5)).astype(BF16)
        pk = p[:, ATTN_W:ATTN_W + KV_W]
        rk = lax.rsqrt(_head_mean(pk * pk, bdk_ref[...]) + EPS)
        k_ref[...] = _rope(pk * rk * gk_ref[...], cosv, sinv).astype(BF16)
        v_ref[...] = p[:, ATTN_W + KV_W:].astype(BF16)

    def rows(w):
        return pl.BlockSpec((tm, w), lambda i: (i, 0))

    def whole(a):
        return pl.BlockSpec(a.shape, lambda i: (0, 0))

    return pl.pallas_call(
        kern, name=name, grid=(r // tm,),
        in_specs=[rows(k), whole(w_qkv), rows(KV_W), rows(KV_W), whole(gq), whole(gk), whole(bdq), whole(bdk)],
        out_specs=[rows(QKV_W), rows(ATTN_W), rows(KV_W), rows(KV_W)],
        out_shape=[jax.ShapeDtypeStruct((r, QKV_W), F32), jax.ShapeDtypeStruct((r, ATTN_W), BF16),
                   jax.ShapeDtypeStruct((r, KV_W), BF16), jax.ShapeDtypeStruct((r, KV_W), BF16)],
        compiler_params=_params())(xall, w_qkv, cos, sin, gq, gk, bdq, bdk)


def _qk_prep_bwd(name, p_qkv, cos, sin, dq, n_ctx_tiles, dk, dv, gq, gk, bdq, bdk, tm):
    def norm_rope_bwd(p, gain, bd, cosv, sinv, dout):
        r = lax.rsqrt(_head_mean(p * p, bd) + EPS)
        n = p * r
        dqn = _rope_bwd(dout, cosv, sinv)
        gy = dqn * gain
        return r * (gy - n * _head_mean(gy * n, bd)), _colsum(dqn * n)

    def body(ri, vi, ro, vo):
        pkv = ri[1][...]
        cosv = ri[2][...]
        sinv = ri[3][...]
        dqv = jnp.where(pl.program_id(0) >= n_ctx_tiles, ri[4][...] * (HEAD_DIM ** -0.5), 0.0)
        dxq, dgq = norm_rope_bwd(ri[0][...], vi[0][...], vi[2][...], _q_wide(cosv), _q_wide(sinv), dqv)
        dxk, dgk = norm_rope_bwd(pkv[:, :KV_W], vi[1][...], vi[3][...], cosv, sinv, ri[5][...])
        ro[0][:, :ATTN_W] = dxq.astype(BF16)
        ro[0][:, ATTN_W:ATTN_W + KV_W] = dxk.astype(BF16)
        ro[0][:, ATTN_W + KV_W:] = ri[6][...].astype(BF16)
        vo[0][...] += dgq
        vo[1][...] += dgk

    return _rowwise(name, body, p_qkv.shape[0] // tm, tm,
                    [_ri(p_qkv, ATTN_W, 0), _ri(p_qkv, 2 * KV_W, 2), _ri(cos), _ri(sin),
                     _ri(dq, row_fn=lambda i: jnp.maximum(i - n_ctx_tiles, 0)), _ri(dk), _ri(dv)],
                    [gq, gk, bdq, bdk], [(QKV_W, BF16)], [(1, ATTN_W), (1, KV_W)])


def _mm_sg(name, xm, w_sgp, w_sp, bsp, tm):
    m, k = xm.shape

    def kern(x_ref, w_ref, wsp_ref, bsp_ref, p_ref, o_ref):
        p = jnp.dot(x_ref[...], w_ref[...], preferred_element_type=F32)
        p_ref[...] = p
        gu, _ = _gelu(p[:, :SG_W])
        gv, _ = _gelu(p[:, SG_W:])
        for g in range(N_SG):
            sl = slice(g * CHUNK, (g + 1) * CHUNK)
            vn, _ = _ln_stats(gv[:, sl])
            vnb = vn.astype(BF16)
            for c in range(tm // CHUNK):
                rs = slice(c * CHUNK, (c + 1) * CHUNK)
                mixed = jnp.dot(wsp_ref[g], vnb[rs], preferred_element_type=F32) + bsp_ref[:, sl]
                o_ref[rs, sl] = (gu[rs, sl] * mixed).astype(BF16)

    return pl.pallas_call(
        kern, name=name, grid=(m // tm,),
        in_specs=[pl.BlockSpec((tm, k), lambda i: (i, 0)), pl.BlockSpec(w_sgp.shape, lambda i: (0, 0)),
                  pl.BlockSpec(w_sp.shape, lambda i: (0, 0, 0)), pl.BlockSpec(bsp.shape, lambda i: (0, 0))],
        out_specs=[pl.BlockSpec((tm, 2 * SG_W), lambda i: (i, 0)), pl.BlockSpec((tm, SG_W), lambda i: (i, 0))],
        out_shape=[jax.ShapeDtypeStruct((m, 2 * SG_W), F32), jax.ShapeDtypeStruct((m, SG_W), BF16)],
        compiler_params=_params())(xm, w_sgp, w_sp, bsp)


def _sg_bwd(name, p_sg, dcat, w_sp, w_sp_t, bsp, tm):
    def body(ri, vi, ro, vo):
        p = ri[0][...]
        dsg = ri[1][...]
        su = p[:, :SG_W]
        sv = p[:, SG_W:]
        gu, tu = _gelu(su)
        gv, tv = _gelu(sv)
        for g in range(N_SG):
            sl = slice(g * CHUNK, (g + 1) * CHUNK)
            vn, r = _ln_stats(gv[:, sl])
            vnb = vn.astype(BF16)
            for m in range(tm // CHUNK):
                rs = slice(m * CHUNK, (m + 1) * CHUNK)
                mixed = jnp.dot(vi[0][g], vnb[rs], preferred_element_type=F32) + vi[2][:, sl]
                d_o = dsg[rs, sl]
                dm = d_o * gu[rs, sl]
                vo[1][:, sl] += dm
                dmb = dm.astype(BF16)
                vo[0][g] += lax.dot_general(dmb, vnb[rs], (((1,), (1,)), ((), ())), preferred_element_type=F32)
                dvn = jnp.dot(vi[1][g], dmb, preferred_element_type=F32)
                dgv = _ln_bwd(vn[rs], r[rs], dvn)
                ro[0][rs, sl] = (d_o * mixed * _gelu_grad(su[rs, sl], tu[rs, sl])).astype(BF16)
                ro[0][rs, SG_W + g * CHUNK:SG_W + (g + 1) * CHUNK] = (
                    dgv * _gelu_grad(sv[rs, sl], tv[rs, sl])).astype(BF16)

    return _rowwise(name, body, p_sg.shape[0] // tm, tm, [_ri(p_sg), _ri(dcat, SG_W, 1)], [w_sp, w_sp_t, bsp],
                    [(2 * SG_W, BF16)], [(N_SG, CHUNK, CHUNK), (CHUNK, SG_W)])


def _glu_bwd(name, ag, dhg, b, tm):
    def body(ri, vi, ro, vo):
        v = ri[0][...] + vi[0][...]
        d = ri[1][...]
        a = v[:, :D_MODEL]
        sg = _sigmoid(v[:, D_MODEL:])
        da = d * sg
        dgate = d * a * sg * (1.0 - sg)
        ro[0][:, :D_MODEL] = da.astype(BF16)
        ro[0][:, D_MODEL:] = dgate.astype(BF16)
        vo[0][:, :D_MODEL] += _colsum(da)
        vo[0][:, D_MODEL:] += _colsum(dgate)

    return _rowwise(name, body, ag.shape[0] // tm, tm, [_ri(ag), _ri(dhg)], [b], [(2 * D_MODEL, BF16)],
                    [(1, 2 * D_MODEL)])


def _lnsilu_fwd(name, hc, g, b, tm):
    def body(ri, vi, ro, vo):
        xh, _ = _ln_stats(ri[0][...])
        ln = xh * vi[0][...] + vi[1][...]
        ro[0][...] = (ln * _sigmoid(ln)).astype(BF16)

    return _rowwise(name, body, hc.shape[0] // tm, tm, [_ri(hc)], [g, b], [(D_MODEL, BF16)], [])[0]


def _conv_blocks(seq):
    cb = 128
    tt = 128 if seq % 128 == 0 else seq
    return cb, tt


def _conv_taps(win, tt):
    n = win.shape[0]
    for s in range(8):
        ws = win if s == 0 else pltpu.roll(win, n - s, 0)
        for q in range(4):
            j = 8 * q + s - 1
            if 0 <= j < CONV_W:
                yield j, ws[8 * q:8 * q + tt, :]


def _fill_padded(pad_ref, x_ref, seq):
    zeros = jnp.zeros((CONV_HALO, pad_ref.shape[1]), F32)
    pad_ref[0:CONV_HALO, :] = zeros
    pad_ref[seq + CONV_HALO:seq + 2 * CONV_HALO, :] = zeros
    pad_ref[CONV_HALO:seq + CONV_HALO, :] = x_ref[...]


def _dwconv(name, xin, w, b):
    seq = xin.shape[0]
    cb, tt = _conv_blocks(seq)

    def kern(x_ref, w_ref, b_ref, o_ref, pad_ref):
        _fill_padded(pad_ref, x_ref, seq)
        wv = w_ref[...]
        bv = b_ref[...]

        def step(t, carry):
            base = pl.multiple_of(t * tt, tt)
            win = pad_ref[pl.ds(base, tt + 2 * CONV_HALO), :]
            acc = jnp.zeros((tt, cb), F32) + bv
            for j, rows in _conv_taps(win, tt):
                acc = acc + wv[j:j + 1, :] * rows
            o_ref[pl.ds(base, tt), :] = acc
            return carry

        lax.fori_loop(0, seq // tt, step, 0)

    return pl.pallas_call(
        kern, name=name, grid=(D_MODEL // cb,),
        in_specs=[pl.BlockSpec((seq, cb), lambda j: (0, j)), pl.BlockSpec((CONV_W, cb), lambda j: (0, j)),
                  pl.BlockSpec((1, cb), lambda j: (0, j))],
        out_specs=pl.BlockSpec((seq, cb), lambda j: (0, j)),
        out_shape=jax.ShapeDtypeStruct((seq, D_MODEL), F32),
        scratch_shapes=[pltpu.VMEM((seq + 2 * CONV_HALO, cb), F32)],
        compiler_params=_params())(xin, w, b)


def _dwconv_wgrad(name, xin, dout):
    seq = xin.shape[0]
    cb, tt = _conv_blocks(seq)

    def kern(x_ref, d_ref, dw_ref, db_ref, pad_ref):
        _fill_padded(pad_ref, x_ref, seq)
        dw_ref[...] = jnp.zeros(dw_ref.shape, F32)
        db_ref[...] = jnp.zeros(db_ref.shape, F32)

        def step(t, carry):
            base = pl.multiple_of(t * tt, tt)
            win = pad_ref[pl.ds(base, tt + 2 * CONV_HALO), :]
            d = d_ref[pl.ds(base, tt), :]
            db_ref[...] += _colsum(d)
            for j, rows in _conv_taps(win, tt):
                dw_ref[j:j + 1, :] += _colsum(d * rows)
            return carry

        lax.fori_loop(0, seq // tt, step, 0)

    return pl.pallas_call(
        kern, name=name, grid=(D_MODEL // cb,),
        in_specs=[pl.BlockSpec((seq, cb), lambda j: (0, j)), pl.BlockSpec((seq, cb), lambda j: (0, j))],
        out_specs=[pl.BlockSpec((CONV_W, cb), lambda j: (0, j)), pl.BlockSpec((1, cb), lambda j: (0, j))],
        out_shape=[jax.ShapeDtypeStruct((CONV_W, D_MODEL), F32), jax.ShapeDtypeStruct((1, D_MODEL), F32)],
        scratch_shapes=[pltpu.VMEM((seq + 2 * CONV_HALO, cb), F32)],
        compiler_params=_params())(xin, dout)


def _mm(name, a, b, out_sds, grid, a_spec, b_spec, o_spec, contract, k_axis=None, bias=None, pair_send=()):
    dn = (contract, ((), ()))
    ns = len(pair_send)
    nb = 0 if bias is None else 1
    n_steps = math.prod(grid)

    def kern(*refs):
        a_ref, b_ref = refs[:2]
        bias_ref = refs[2] if nb else None
        o_ref = refs[2 + nb + ns]
        if ns:
            copies = _pair_send_copies(refs[2 + nb:2 + nb + ns], refs[3 + nb + ns:3 + nb + 2 * ns],
                                       *refs[3 + nb + 2 * ns:])
            step = pl.program_id(0)
            for ax in range(1, len(grid)):
                step = step * grid[ax] + pl.program_id(ax)

            @pl.when(step == 0)
            def _():
                for cp in copies():
                    cp.start()

        p = lax.dot_general(a_ref[...].astype(BF16), b_ref[...].astype(BF16), dn, preferred_element_type=F32)
        if bias is not None:
            p = p + bias_ref[...]
        if k_axis is None:
            o_ref[...] = p.astype(o_ref.dtype)
        else:
            k = pl.program_id(k_axis)

            @pl.when(k == 0)
            def _():
                o_ref[...] = p

            @pl.when(k > 0)
            def _():
                o_ref[...] += p

        if ns:
            @pl.when(step == n_steps - 1)
            def _():
                for cp in copies():
                    cp.wait()

    in_specs = [a_spec, b_spec]
    args = [a, b]
    if bias is not None:
        in_specs.append(pl.BlockSpec(bias.shape, lambda *_: (0,) * bias.ndim))
        args.append(bias)
    if not ns:
        return pl.pallas_call(kern, name=name, grid=grid, in_specs=in_specs, out_specs=o_spec, out_shape=out_sds,
                              compiler_params=_params())(*args)
    return pl.pallas_call(
        kern, name=name, grid=grid, in_specs=in_specs + [_ANY] * ns, out_specs=[o_spec] + [_ANY] * ns,
        out_shape=[out_sds] + [jax.ShapeDtypeStruct((N_CHIP,) + g.shape[2:], g.dtype) for g in pair_send],
        scratch_shapes=[pltpu.SemaphoreType.DMA((ns, N_CHIP)), pltpu.SemaphoreType.DMA((ns, N_CHIP))],
        compiler_params=_params())(*args, *pair_send)


def _mm_rows(name, a, b, tm, trans_b=False, bias=None):
    m, k = a.shape
    n = b.shape[0] if trans_b else b.shape[1]
    contract = ((1,), (1,)) if trans_b else ((1,), (0,))
    return _mm(name, a, b, jax.ShapeDtypeStruct((m, n), F32), (m // tm,),
               pl.BlockSpec((tm, k), lambda i: (i, 0)), pl.BlockSpec(b.shape, lambda i: (0, 0)),
               pl.BlockSpec((tm, n), lambda i: (i, 0)), contract, bias=bias)


def _mm_glu(name, a, b, bias, tm):
    m, k = a.shape
    n = b.shape[1]

    def kern(a_ref, b_ref, bias_ref, ag_ref, hg_ref):
        v = jnp.dot(a_ref[...], b_ref[...], preferred_element_type=F32)
        ag_ref[...] = v
        v = v + bias_ref[...]
        hg_ref[...] = v[:, :n // 2] * _sigmoid(v[:, n // 2:])

    return pl.pallas_call(
        kern, name=name, grid=(m // tm,),
        in_specs=[pl.BlockSpec((tm, k), lambda i: (i, 0)), pl.BlockSpec(b.shape, lambda i: (0, 0)),
                  pl.BlockSpec((1, n), lambda i: (0, 0))],
        out_specs=[pl.BlockSpec((tm, n), lambda i: (i, 0)), pl.BlockSpec((tm, n // 2), lambda i: (i, 0))],
        out_shape=[jax.ShapeDtypeStruct((m, n), F32), jax.ShapeDtypeStruct((m, n // 2), F32)],
        compiler_params=_params())(a, b, bias)


def _mm_rows_res(name, a, b, h, gt, tm, bias=None, norm=None, final=None):
    m, k = a.shape
    n = b.shape[1]
    nb = 0 if bias is None else 1
    extra = list(norm or ()) + list(final or ())

    def kern(a_ref, b_ref, h_ref, gt_ref, *rest):
        y = jnp.dot(a_ref[...], b_ref[...], preferred_element_type=F32)
        if nb:
            y = y + rest[0][...]
        ex = rest[nb:nb + len(extra)]
        outs = rest[nb + len(extra):]
        hn = h_ref[...] + gt_ref[...] * y
        if final is None:
            outs[0][...] = y
            outs[1][...] = hn
        if norm is not None:
            nn, _ = _rms(hn)
            outs[2][...] = ((nn * ex[0][...]) * (1.0 + ex[1][...]) + ex[2][...]).astype(BF16)
        if final is not None:
            @pl.when(pl.program_id(0) == 0)
            def _():
                for o in (outs[1], outs[2], outs[4]):
                    o[...] = jnp.zeros(o.shape, F32)

            gv = ex[1][...]
            nn, r = _rms(hn)
            e = nn * gv - ex[0][...]
            outs[2][...] += _colsum(e * e) * (0.5 / D_MODEL)
            dout = e * (1.0 / D_MODEL)
            outs[1][...] += _colsum(dout * nn)
            dh = _rms_bwd(nn, r, dout * gv)
            outs[0][...] = dh
            outs[3][...] = (dh * gt_ref[...]).astype(BF16)
            outs[4][...] += _colsum(dh * y)

    row = pl.BlockSpec((tm, n), lambda i: (i, 0))
    vec = pl.BlockSpec((1, n), lambda i: (0, 0))
    sds = jax.ShapeDtypeStruct((m, n), F32)
    vsd = jax.ShapeDtypeStruct((1, n), F32)
    in_specs = [pl.BlockSpec((tm, k), lambda i: (i, 0)), pl.BlockSpec(b.shape, lambda i: (0, 0)), row, vec] + [vec] * nb
    out_specs, out_shape = [row, row], [sds, sds]
    if norm is not None:
        in_specs += [vec] * 3
        out_specs.append(row)
        out_shape.append(jax.ShapeDtypeStruct((m, n), BF16))
    if final is not None:
        in_specs += [row, vec]
        out_specs = [row, vec, vec, row, vec]
        out_shape = [sds, vsd, vsd, jax.ShapeDtypeStruct((m, n), BF16), vsd]
    return pl.pallas_call(kern, name=name, grid=(m // tm,), in_specs=in_specs, out_specs=out_specs,
                          out_shape=out_shape, compiler_params=_params())(
        a, b, h, gt, *([bias] if nb else []), *extra)


def _ffn_in_swiglu(name, xf, wblk, tm, gather=()):
    m, k = xf.shape
    nblk = wblk.shape[2]
    ng = len(gather)
    n_i = m // tm
    last = 2 * n_i - 1

    def kern(x_ref, wg_ref, wu_ref, *rest):
        gu_ref, a_ref = rest[ng:ng + 2]
        if ng:
            start, forward, finish = _gather_phases(rest[:ng], rest[ng + 2:2 * ng + 2], *rest[2 * ng + 2:])
            step = pl.program_id(0) * n_i + pl.program_id(1)
            pl.when(step == 0)(start)
            pl.when(step == (3 * last) // 4)(forward)
        xv = x_ref[...]
        g = jnp.dot(xv, wg_ref[...], preferred_element_type=F32)
        u = jnp.dot(xv, wu_ref[...], preferred_element_type=F32)
        gu_ref[0] = g.astype(BF16)
        gu_ref[1] = u.astype(BF16)
        a_ref[...] = (g * _sigmoid(g) * u).astype(BF16)
        if ng:
            pl.when(step == last)(finish)

    return pl.pallas_call(
        kern, name=name, grid=(2, n_i),
        in_specs=[pl.BlockSpec((tm, k), lambda j, i: (i, 0)), pl.BlockSpec((None, k, nblk), lambda j, i: (j, 0, 0)),
                  pl.BlockSpec((None, k, nblk), lambda j, i: (j + 2, 0, 0))] + [_ANY] * ng,
        out_specs=[pl.BlockSpec((2, tm, nblk), lambda j, i: (0, i, j)),
                   pl.BlockSpec((tm, nblk), lambda j, i: (i, j))] + [_ANY] * ng,
        out_shape=[jax.ShapeDtypeStruct((2, m, 2 * nblk), BF16), jax.ShapeDtypeStruct((m, 2 * nblk), BF16)]
        + [jax.ShapeDtypeStruct((N_DEV,) + g.shape, g.dtype) for g in gather],
        scratch_shapes=_gather_scratch(ng) if ng else [],
        compiler_params=_params())(xf, wblk, wblk, *gather)


def _ffn_out_dx_swiglu(name, dy, wo, gu, tm):
    m, k = dy.shape
    nblk = gu.shape[2] // 2

    def kern(dy_ref, w_ref, gu_ref, o_ref):
        da = lax.dot_general(dy_ref[...], w_ref[...], (((1,), (1,)), ((), ())), preferred_element_type=F32)
        g = gu_ref[0].astype(F32)
        sg = _sigmoid(g)
        o_ref[0] = (da * gu_ref[1].astype(F32) * (sg * (1.0 + g * (1.0 - sg)))).astype(BF16)
        o_ref[1] = (da * (g * sg)).astype(BF16)

    return pl.pallas_call(
        kern, name=name, grid=(2, m // tm),
        in_specs=[pl.BlockSpec((tm, k), lambda j, i: (i, 0)), pl.BlockSpec((nblk, k), lambda j, i: (j, 0)),
                  pl.BlockSpec((2, tm, nblk), lambda j, i: (0, i, j))],
        out_specs=pl.BlockSpec((2, tm, nblk), lambda j, i: (0, i, j)),
        out_shape=jax.ShapeDtypeStruct(gu.shape, BF16), compiler_params=_params())(dy, wo, gu)


def _ffn_in_dx_norm_bwd(name, dgu, wblk, h, dh_in, g, sc, tm, gate=None):
    nb, k, nblk = wblk.shape

    def dx_fn(a_ref, w_ref):
        acc = None
        for j in range(nb):
            p = lax.dot_general(a_ref[j // 2, :, (j % 2) * nblk:(j % 2 + 1) * nblk], w_ref[j],
                                (((1,), (1,)), ((), ())), preferred_element_type=F32)
            acc = p if acc is None else acc + p
        return acc

    return _mm_row_epilogue(name, [dgu, wblk],
                            [pl.BlockSpec((2, tm, 2 * nblk), lambda i: (0, i, 0)),
                             pl.BlockSpec(wblk.shape, lambda i: (0, 0, 0))], dx_fn, [(h, None), (dh_in, None)],
                            [g, sc], 3, _norm_bwd_epilogue, tm, gate=gate)


def _mm_t_epilogue(name, a, b, rows_in, vecs_in, n_vec_out, epilogue, tm, gate=None, exchange=()):
    def dx_fn(a_ref, b_ref):
        return lax.dot_general(a_ref[...], b_ref[...], (((1,), (1,)), ((), ())), preferred_element_type=F32)

    return _mm_row_epilogue(name, [a, b], [pl.BlockSpec((tm, a.shape[1]), lambda i: (i, 0)),
                                          pl.BlockSpec(b.shape, lambda i: (0, 0))], dx_fn, rows_in, vecs_in,
                            n_vec_out, epilogue, tm, gate, exchange)


def _mm_halves_wgrad_blk(name, a, b2, tk, pair_send=()):
    t, k = a.shape
    nblk = b2.shape[2] // 2
    return _mm(name, a, b2, jax.ShapeDtypeStruct((N_CHIP, k, nblk), F32), (N_CHIP, t // tk),
               pl.BlockSpec((tk, k), lambda j, i: (i, 0)),
               pl.BlockSpec((None, tk, nblk), lambda j, i: (j // 2, i, j % 2)),
               pl.BlockSpec((None, k, nblk), lambda j, i: (j, 0, 0)), ((0,), (0,)), k_axis=1, pair_send=pair_send)


def _mm_wgrad(name, a, b, tk):
    t, k = a.shape
    n = b.shape[1]
    tn = n if k * n * 4 <= 4 * 1024 * 1024 else 512
    return _mm(name, a, b, jax.ShapeDtypeStruct((k, n), F32), (n // tn, t // tk),
               pl.BlockSpec((tk, k), lambda j, i: (i, 0)), pl.BlockSpec((tk, tn), lambda j, i: (i, j)),
               pl.BlockSpec((k, tn), lambda j, i: (0, j)), ((0,), (0,)), k_axis=1)


def _attn_fwd(name, q_all, k, v, n_ctx, tq, gather=()):
    r = k.shape[1]
    s = q_all.shape[0] - n_ctx
    gw = Q_GROUP * HEAD_DIM
    ng = len(gather)
    n_i = s // tq
    last = N_KV_HEADS * n_i - 1
    forward_at = (7 * last) // 8

    def kern(q_ref, k_ref, v_ref, *rest):
        o_ref, lse_ref = rest[ng], rest[ng + 1]
        if ng:
            start, forward, finish = _gather_phases(rest[:ng], rest[ng + 2:2 * ng + 2], *rest[2 * ng + 2:])
            step = pl.program_id(0) * n_i + pl.program_id(1)
            pl.when(step == 0)(start)
            pl.when(step == forward_at)(forward)
        kv = k_ref[...]
        vv = v_ref[...]
        for g in range(Q_GROUP):
            sl = slice(g * HEAD_DIM, (g + 1) * HEAD_DIM)
            sc = lax.dot_general(q_ref[:, sl], kv, (((1,), (1,)), ((), ())), preferred_element_type=F32)
            m = jnp.max(sc, axis=-1, keepdims=True)
            p = jnp.exp((sc - m).astype(BF16))
            ol = jnp.dot(p, vv, preferred_element_type=F32)
            l = ol[:, HEAD_DIM:HEAD_DIM + 1]
            o_ref[:, sl] = ol[:, :HEAD_DIM] / l
            lse_ref[g] = m + jnp.log(l)
        if ng:
            pl.when(step == last)(finish)

    return pl.pallas_call(
        kern, name=name, grid=(N_KV_HEADS, n_i),
        in_specs=[pl.BlockSpec((tq, gw), lambda j, i: (i + n_ctx // tq, j)),
                  pl.BlockSpec((None, r, HEAD_DIM), lambda j, i: (j, 0, 0)),
                  pl.BlockSpec((None, r, 2 * HEAD_DIM), lambda j, i: (j, 0, 0))] + [_ANY] * ng,
        out_specs=[pl.BlockSpec((tq, gw), lambda j, i: (i, j)),
                   pl.BlockSpec((Q_GROUP, tq, 1), lambda j, i: (j, i, 0))] + [_ANY] * ng,
        out_shape=[jax.ShapeDtypeStruct((s, ATTN_W), F32), jax.ShapeDtypeStruct((N_Q_HEADS, s, 1), F32)]
        + [jax.ShapeDtypeStruct((N_DEV,) + a.shape, a.dtype) for a in gather],
        scratch_shapes=_gather_scratch(ng) if ng else [],
        compiler_params=_params())(q_all, k, v, *gather)


def _attn_bwd(name, q_all, k, v, o, lse, dcat, n_ctx, tq, exchange=()):
    nkv, r, _ = k.shape
    s = o.shape[0]
    gw = Q_GROUP * HEAD_DIM
    ne = len(exchange)
    n_i = s // tq
    last = nkv * n_i - 1

    def kern(q_ref, k_ref, v_ref, o_ref, lse_ref, do_ref, *rest):
        dq_ref, dk_ref, dv_ref = rest[ne:ne + 3]
        ds_all, p_all = rest[2 * ne + 3:2 * ne + 5]
        if ne:
            start, finish = _exchange_phases(rest[:ne], rest[ne + 3:2 * ne + 3], *rest[2 * ne + 5:])
            step = pl.program_id(0) * n_i + pl.program_id(1)
            pl.when(step == 0)(start)

        @pl.when(pl.program_id(1) == 0)
        def _():
            dk_ref[...] = jnp.zeros(dk_ref.shape, F32)
            dv_ref[...] = jnp.zeros(dv_ref.shape, F32)

        kv = k_ref[...]
        vv = v_ref[...]
        qs = []
        dos = []
        for g in range(Q_GROUP):
            sl = slice(g * HEAD_DIM, (g + 1) * HEAD_DIM)
            rows = slice(g * tq, (g + 1) * tq)
            qv = q_ref[:, sl]
            dov = do_ref[:, sl]
            delta = jnp.sum(dov * o_ref[:, sl], axis=-1, keepdims=True)
            sc = lax.dot_general(qv, kv, (((1,), (1,)), ((), ())), preferred_element_type=F32)
            p = jnp.exp(sc - lse_ref[g])
            dob = dov.astype(BF16)
            dp = lax.dot_general(dob, vv, (((1,), (1,)), ((), ())), preferred_element_type=F32)
            dsb = (p * (dp - delta)).astype(BF16)
            dq_ref[:, sl] = jnp.dot(dsb, kv, preferred_element_type=F32)
            ds_all[rows, :] = dsb
            p_all[rows, :] = p.astype(BF16)
            qs.append(qv)
            dos.append(dob)
        dk_ref[...] += lax.dot_general(jnp.concatenate(qs, axis=0), ds_all[...], (((0,), (0,)), ((), ())),
                                       preferred_element_type=F32)
        dv_ref[...] += lax.dot_general(jnp.concatenate(dos, axis=0), p_all[...], (((0,), (0,)), ((), ())),
                                       preferred_element_type=F32)
        if ne:
            pl.when(step == last)(finish)

    kvmap = lambda j, i: (j, 0, 0)
    return pl.pallas_call(
        kern, name=name, grid=(nkv, n_i),
        in_specs=[pl.BlockSpec((tq, gw), lambda j, i: (i + n_ctx // tq, j)), pl.BlockSpec((None, r, HEAD_DIM), kvmap),
                  pl.BlockSpec((None, r, HEAD_DIM), kvmap), pl.BlockSpec((tq, gw), lambda j, i: (i, j)),
                  pl.BlockSpec((Q_GROUP, tq, 1), lambda j, i: (j, i, 0)),
                  pl.BlockSpec((tq, gw), lambda j, i: (i, j))] + [_ANY] * ne,
        out_specs=[pl.BlockSpec((tq, gw), lambda j, i: (i, j)), pl.BlockSpec((None, HEAD_DIM, r), kvmap),
                   pl.BlockSpec((None, HEAD_DIM, r), kvmap)] + [_ANY] * ne,
        out_shape=[jax.ShapeDtypeStruct((s, ATTN_W), F32), jax.ShapeDtypeStruct((nkv, HEAD_DIM, r), F32),
                   jax.ShapeDtypeStruct((nkv, HEAD_DIM, r), F32)]
        + [jax.ShapeDtypeStruct(a.shape, a.dtype) for a in exchange],
        scratch_shapes=[pltpu.VMEM((Q_GROUP * tq, r), BF16)] * 2 + (_exchange_scratch(ne) if ne else []),
        compiler_params=_params())(q_all, k, v, o, lse, dcat, *exchange)


def _mod_fwd(name, c16, wm, bm):
    n = wm.shape[2]

    def kern(c_ref, w_ref, b_ref, o_ref):
        cv = c_ref[...]
        a = (cv * _sigmoid(cv)).astype(BF16)
        o_ref[...] = jnp.dot(a, w_ref[...].astype(BF16), preferred_element_type=F32) + b_ref[...]

    return pl.pallas_call(
        kern, name=name, grid=(2,),
        in_specs=[pl.BlockSpec(c16.shape, lambda l: (0, 0)), pl.BlockSpec((None, D_MODEL, n), lambda l: (l, 0, 0)),
                  pl.BlockSpec((None, 1, n), lambda l: (l, 0, 0))],
        out_specs=pl.BlockSpec((None, 16, n), lambda l: (l, 0, 0)),
        out_shape=jax.ShapeDtypeStruct((2, 16, n), F32), compiler_params=_params())(c16, wm, bm)


def _mod_wgrad(name, c16, dm):
    n = dm.shape[2]
    tn = 512

    def kern(c_ref, d_ref, o_ref):
        cv = c_ref[...]
        a = cv * _sigmoid(cv)
        o_ref[...] = lax.dot_general(a, d_ref[...], (((0,), (0,)), ((), ())), preferred_element_type=F32,
                                     precision=lax.Precision.HIGHEST)

    return pl.pallas_call(
        kern, name=name, grid=(2, n // tn),
        in_specs=[pl.BlockSpec(c16.shape, lambda l, j: (0, 0)), pl.BlockSpec((None, 16, tn), lambda l, j: (l, 0, j))],
        out_specs=pl.BlockSpec((None, D_MODEL, tn), lambda l, j: (l, 0, j)),
        out_shape=jax.ShapeDtypeStruct((2, D_MODEL, n), F32), compiler_params=_params())(c16, dm)


def _cctx_grad(name, parts, c_ctx):
    def kern(p_ref, c_ref, o_ref):
        d = p_ref[0, 0:1, :] + p_ref[2, 0:1, :] + p_ref[4, 0:1, :] + p_ref[6, 0:1, :]
        cv = c_ref[...]
        sg = _sigmoid(cv)
        o_ref[...] = d * (sg * (1.0 + cv * (1.0 - sg)))

    return pl.pallas_call(kern, name=name, out_shape=jax.ShapeDtypeStruct((1, D_MODEL), F32),
                          compiler_params=_params())(parts, c_ctx)


def _sum_slots(name, g, tr):
    n, rows, cols = g.shape

    def kern(g_ref, o_ref):
        acc = g_ref[0].astype(F32)
        for i in range(1, n):
            acc = acc + g_ref[i].astype(F32)
        o_ref[...] = acc

    return pl.pallas_call(kern, name=name, grid=(rows // tr,),
                          in_specs=[pl.BlockSpec((n, tr, cols), lambda i: (0, i, 0))],
                          out_specs=pl.BlockSpec((tr, cols), lambda i: (i, 0)),
                          out_shape=jax.ShapeDtypeStruct((rows, cols), F32), compiler_params=_params())(g)


def _sum_into(name, g, cvec, n_layers, layer, prev=None):
    n, r, cc = g.shape
    tr = _row_tile(r, cc, 512 * 1024)

    def kern(c_ref, g_ref, *rest):
        acc = g_ref[0].astype(F32)
        for i in range(1, n):
            acc = acc + g_ref[i].astype(F32)
        rest[-1][...] = acc

    in_specs = [pl.BlockSpec((n, tr, cc), lambda i, c_ref: (0, i, 0))]
    args = [cvec, g]
    aliases = {}
    if prev is not None:
        in_specs.append(_ANY)
        args.append(prev)
        aliases = {2: 0}
    grid_spec = pltpu.PrefetchScalarGridSpec(
        num_scalar_prefetch=1, grid=(r // tr,), in_specs=in_specs,
        out_specs=pl.BlockSpec((None, None, tr, cc), lambda i, c_ref: (layer, c_ref[0], i, 0)))
    return pl.pallas_call(kern, name=name, grid_spec=grid_spec,
                          out_shape=jax.ShapeDtypeStruct((n_layers, 2, r, cc), F32),
                          input_output_aliases=aliases, compiler_params=_params())(*args)


def _row_tile(rows, cols, max_bytes=1024 * 1024):
    if rows * cols * 4 <= 2 * max_bytes:
        return rows
    best = None
    for t in range(16, rows + 1, 16):
        if rows % t == 0 and t * cols * 4 <= max_bytes:
            best = t
    assert best is not None, (rows, cols)
    return best


def _adamw(name, w, g, m, v):
    shape = w.shape
    cols = shape[-1]
    rows = w.size // cols
    tr = _row_tile(rows, cols)

    def kern(w_ref, g_ref, m_ref, v_ref, d_ref, nm_ref, nv_ref):
        gv = g_ref[...]
        m2 = ADAM_B1 * m_ref[...] + (1.0 - ADAM_B1) * gv
        v2 = ADAM_B2 * v_ref[...] + (1.0 - ADAM_B2) * (gv * gv)
        m_hat = m2 / (1.0 - ADAM_B1 ** ADAM_STEP)
        v_hat = v2 / (1.0 - ADAM_B2 ** ADAM_STEP)
        d_ref[...] = -ADAM_LR * (m_hat / (jnp.sqrt(v_hat) + ADAM_EPS) + ADAM_WD * w_ref[...])
        nm_ref[...] = m2
        nv_ref[...] = v2

    spec = pl.BlockSpec((tr, cols), lambda i: (i, 0))
    sds = jax.ShapeDtypeStruct((rows, cols), F32)
    outs = pl.pallas_call(kern, name=name, grid=(rows // tr,), in_specs=[spec] * 4, out_specs=[spec] * 3,
                          out_shape=[sds] * 3, compiler_params=_params())(
        w.reshape(rows, cols), g.reshape(rows, cols), m.reshape(rows, cols), v.reshape(rows, cols))
    return tuple(o.reshape(shape) for o in outs)


_ANY = pl.BlockSpec(memory_space=pl.ANY)


def _mesh_pos():
    return lax.axis_index("x"), lax.axis_index("y"), lax.axis_index("c")


def _gather_phases(srcs, outs, send_sems, recv_sems, local_sems):
    n = len(srcs)
    x, y, c = _mesh_pos()
    me = (x, y, c)
    sibling = (x, y, 1 - c)
    chips = [(1 - x, y), (x, 1 - y), (1 - x, 1 - y)]

    def slot(px, py, pc):
        return 4 * px + 2 * py + pc

    def copy(t, k, s, to, src=None):
        dst = outs[t].at[s]
        return pltpu.make_async_remote_copy(src_ref=dst if src is None else src, dst_ref=dst,
                                            send_sem=send_sems.at[t, k], recv_sem=recv_sems.at[t, k],
                                            device_id=to, device_id_type=MESH)

    def mine(t):
        return pltpu.make_async_copy(srcs[t], outs[t].at[slot(*me)], local_sems.at[t])

    def first(t):
        return [copy(t, 0, slot(*me), sibling, src=srcs[t])] + [
            copy(t, 1 + j, slot(*me), (px, py, c), src=srcs[t]) for j, (px, py) in enumerate(chips)]

    def passed(t, j):
        px, py = chips[j]
        return copy(t, 4 + j, slot(px, py, c), sibling)

    def start():
        for t in range(n):
            mine(t).start()
        for t in range(n):
            for cp in first(t):
                cp.start()

    def forward():
        for j, (px, py) in enumerate(chips):
            for t in range(n):
                copy(t, 1 + j, slot(px, py, c), me).wait_recv()
                passed(t, j).start()

    def finish():
        for t in range(n):
            copy(t, 0, slot(x, y, 1 - c), me).wait_recv()
            for j, (px, py) in enumerate(chips):
                copy(t, 4 + j, slot(px, py, 1 - c), me).wait_recv()
        for t in range(n):
            for cp in first(t) + [passed(t, j) for j in range(len(chips))]:
                cp.wait_send()
            mine(t).wait()

    return start, forward, finish


def _gather_scratch(n):
    return [pltpu.SemaphoreType.DMA((n, 7)), pltpu.SemaphoreType.DMA((n, 7)), pltpu.SemaphoreType.DMA((n,))]


def _allgather(name, items):
    n = len(items)

    def body(*refs):
        start, forward, finish = _gather_phases(refs[:n], refs[n:2 * n], *refs[2 * n:])
        start()
        forward()
        finish()

    return pl.pallas_call(
        body, name=name, in_specs=[_ANY] * n, out_specs=[_ANY] * n,
        out_shape=[jax.ShapeDtypeStruct((N_DEV,) + a.shape, a.dtype) for a in items],
        scratch_shapes=_gather_scratch(n), compiler_params=_params())(*items)


def _pair_send_copies(srcs, bufs, send_sems, recv_sems):
    x, y, c = _mesh_pos()

    def copies():
        return [pltpu.make_async_remote_copy(
            src_ref=srcs[t].at[k, 1 - c], dst_ref=bufs[t].at[k], send_sem=send_sems.at[t, k],
            recv_sem=recv_sems.at[t, k], device_id=(x, y, 1 - c), device_id_type=MESH)
            for t in range(len(srcs)) for k in range(N_CHIP)]

    return copies


def _rs_pair_send(name, gs):
    n = len(gs)

    def body(*refs):
        copies = _pair_send_copies(refs[:n], refs[n:2 * n], *refs[2 * n:])
        for cp in copies():
            cp.start()
        for cp in copies():
            cp.wait()

    return pl.pallas_call(
        body, name=name, in_specs=[_ANY] * n, out_specs=[_ANY] * n,
        out_shape=[jax.ShapeDtypeStruct((N_CHIP,) + g.shape[2:], g.dtype) for g in gs],
        scratch_shapes=[pltpu.SemaphoreType.DMA((n, N_CHIP)), pltpu.SemaphoreType.DMA((n, N_CHIP))],
        compiler_params=_params())(*gs)


def _rs_pair_add(name, g, buf, cvec):
    _, _, r, cc = g.shape
    tr = _row_tile(r, cc, 2 * 1024 * 1024)

    def kern(c_ref, g_ref, b_ref, o_ref):
        o_ref[...] = (g_ref[...] + b_ref[...]).astype(BF16)

    grid_spec = pltpu.PrefetchScalarGridSpec(
        num_scalar_prefetch=1, grid=(N_CHIP, r // tr),
        in_specs=[pl.BlockSpec((None, None, tr, cc), lambda k, i, c_ref: (k, c_ref[0], i, 0)),
                  pl.BlockSpec((None, tr, cc), lambda k, i, c_ref: (k, i, 0))],
        out_specs=pl.BlockSpec((None, tr, cc), lambda k, i, c_ref: (k, i, 0)))
    return pl.pallas_call(kern, name=name, grid_spec=grid_spec,
                          out_shape=jax.ShapeDtypeStruct((N_CHIP, r, cc), BF16),
                          compiler_params=_params())(cvec, g, buf)


def _exchange_phases(srcs, bufs, send_sems, recv_sems, local_sems):
    n = len(srcs)
    x, y, c = _mesh_pos()
    kme = 2 * x + y
    chips = [(1 - x, y), (x, 1 - y), (1 - x, 1 - y)]

    def copies():
        local = [pltpu.make_async_copy(srcs[t].at[kme], bufs[t].at[kme], local_sems.at[t]) for t in range(n)]
        remote = []
        for t in range(n):
            for j, (px, py) in enumerate(chips):
                remote.append(pltpu.make_async_remote_copy(
                    src_ref=srcs[t].at[2 * px + py], dst_ref=bufs[t].at[kme], send_sem=send_sems.at[t, j],
                    recv_sem=recv_sems.at[t, j], device_id=(px, py, c), device_id_type=MESH))
        return local, remote

    def start():
        local, remote = copies()
        for cp in local + remote:
            cp.start()

    def finish():
        local, remote = copies()
        for cp in remote + local:
            cp.wait()

    return start, finish


def _exchange_scratch(n):
    return [pltpu.SemaphoreType.DMA((n, 3)), pltpu.SemaphoreType.DMA((n, 3)), pltpu.SemaphoreType.DMA((n,))]


def _rs_chip_exchange(name, ss):
    n = len(ss)

    def body(*refs):
        start, finish = _exchange_phases(refs[:n], refs[n:2 * n], *refs[2 * n:])
        start()
        finish()

    return pl.pallas_call(
        body, name=name, in_specs=[_ANY] * n, out_specs=[_ANY] * n,
        out_shape=[jax.ShapeDtypeStruct(s.shape, s.dtype) for s in ss],
        scratch_shapes=_exchange_scratch(n), compiler_params=_params())(*ss)


def _rs_pair_share(name, dsts, gather=()):
    nd = len(dsts)
    ng = len(gather)

    def body(*refs):
        ins = refs[:nd]
        outs = refs[nd + ng:2 * nd + ng]
        send_sems, recv_sems = refs[2 * (nd + ng):2 * (nd + ng) + 2]
        if ng:
            start, forward, finish = _gather_phases(refs[nd:nd + ng], refs[2 * nd + ng:2 * (nd + ng)],
                                                    *refs[2 * (nd + ng) + 2:])
            start()
        x, y, c = _mesh_pos()
        copies = []
        for d in range(nd):
            for l in range(dsts[d].shape[0]):
                copies.append(pltpu.make_async_remote_copy(
                    src_ref=ins[d].at[l, c], dst_ref=outs[d].at[l, c], send_sem=send_sems.at[d, l],
                    recv_sem=recv_sems.at[d, l], device_id=(x, y, 1 - c), device_id_type=MESH))
        for cp in copies:
            cp.start()
        if ng:
            forward()
        for cp in copies:
            cp.wait()
        if ng:
            finish()

    return pl.pallas_call(
        body, name=name, in_specs=[_ANY] * (nd + ng), out_specs=[_ANY] * (nd + ng),
        out_shape=[jax.ShapeDtypeStruct(a.shape, a.dtype) for a in dsts]
        + [jax.ShapeDtypeStruct((N_DEV,) + a.shape, a.dtype) for a in gather],
        input_output_aliases={d: d for d in range(nd)},
        scratch_shapes=[pltpu.SemaphoreType.DMA((nd, 2)), pltpu.SemaphoreType.DMA((nd, 2))]
        + (_gather_scratch(ng) if ng else []),
        compiler_params=_params())(*dsts, *gather)


def _rope_tables(seq, n_ctx):
    t = jnp.arange(seq)
    row = (t // GRID_W).astype(F32)
    col = (t % GRID_W).astype(F32)
    inv = ROPE_THETA ** (-jnp.arange(0, HEAD_DIM // 2, 2, dtype=F32) / (HEAD_DIM // 2))
    ang_r = row[:, None] * inv[None, :]
    ang_c = col[:, None] * inv[None, :]
    cos = jnp.concatenate([jnp.cos(ang_r)] * 2 + [jnp.cos(ang_c)] * 2, axis=1)
    sin = jnp.concatenate([-jnp.sin(ang_r), jnp.sin(ang_r), -jnp.sin(ang_c), jnp.sin(ang_c)], axis=1)
    cos = jnp.concatenate([jnp.ones((n_ctx, HEAD_DIM), F32), cos], axis=0)
    sin = jnp.concatenate([jnp.zeros((n_ctx, HEAD_DIM), F32), sin], axis=0)
    return jnp.tile(cos, (1, N_KV_HEADS)), jnp.tile(sin, (1, N_KV_HEADS))


def _to_heads(a, nh):
    return a.reshape(a.shape[0], nh, HEAD_DIM).transpose(1, 0, 2)


def _sample_step(x, ctx, tgt, mod, cmod, W, pending, cvec):
    W = dict(W)
    seq = x.shape[0]
    n_ctx = ctx.shape[0]
    tm = min(256, seq)
    tmc = min(256, n_ctx)
    tmm = 512 if seq % 512 == 0 else tm
    tmw = 1024 if seq % 1024 == 0 else tmm
    tmr = (n_ctx + seq) // 4 if (n_ctx + seq) % 64 == 0 else tm
    tmk = 2048 if seq % 2048 == 0 else tmw
    tq = min(256, seq)
    assert n_ctx % tm == 0 and seq % tm == 0

    def mv(l, j):
        return mod[l, j * D_MODEL:(j + 1) * D_MODEL].reshape(1, D_MODEL)

    csh1 = cmod[:D_MODEL].reshape(1, D_MODEL)
    csc1 = cmod[D_MODEL:2 * D_MODEL].reshape(1, D_MODEL)
    g_mix = [W['g_mix'][l:l + 1] for l in range(2)]
    g_ffn = [W['g_ffn'][l:l + 1] for l in range(2)]
    cos, sin = _rope_tables(seq, n_ctx)
    gq = jnp.tile(W['q_gain'], (1, N_Q_HEADS))
    gk = jnp.tile(W['k_gain'], (1, N_KV_HEADS))
    bdq = jnp.kron(jnp.eye(N_Q_HEADS, dtype=F32), jnp.ones((HEAD_DIM, HEAD_DIM), F32)).astype(BF16)
    bdk = bdq[:KV_W, :KV_W]
    w_sp = W['w_sp'].astype(BF16)
    w_sp_t = w_sp.transpose(0, 2, 1)
    bsp = jnp.broadcast_to(W['b_sp'].T[:, :, None], (CHUNK, N_SG, CHUNK)).reshape(CHUNK, SG_W)

    def ffn_fwd(l, h_mid, xf, gather=(), **tail):
        gu, a, *gathered = _ffn_in_swiglu(f'l{l}_ffn_in', xf, W['ffn_in'][l], tmw, gather=gather)
        if gathered:
            W['pw2'] = gathered[0].reshape(D_MODEL, D_MODEL)
            W['ffn_out'][1] = gathered[1].reshape(D_FF, D_MODEL)
            W['pw1'] = gathered[2].reshape(N_CHIP, D_MODEL, PW1_BLK).transpose(1, 0, 2).reshape(D_MODEL, 2 * D_MODEL)
        outs = _mm_rows_res(f'l{l}_ffn_out', a, W['ffn_out'][l], h_mid, mv(l, 5), tmm, **tail)
        return outs, (h_mid, xf, gu, a)

    def blocked(a):
        cols = a.shape[-1]
        return a.reshape(N_CHIP, 2, a.size // (2 * N_CHIP * cols), cols)

    def ffn_bwd(l, dh_out, dy, y_mix, saved, pair_send=()):
        h_mid, xf, gu, a = saved
        dgu = _ffn_out_dx_swiglu(f'l{l}_ffn_out_dx', dy, W['ffn_out'][l], gu, tmw)
        d_wo = _mm_wgrad(f'l{l}_ffn_out_dw', a, dy, tmw)
        d_wi = _mm_halves_wgrad_blk(f'l{l}_ffn_in_dw', xf, dgu, tmk, pair_send=pair_send)
        if pair_send:
            d_wi, *sent = d_wi
        else:
            sent = []
        dh_mid, dy_mix, d_sh2, d_sc2, d_g, d_gt1, sum_dy = _ffn_in_dx_norm_bwd(
            f'l{l}_ffn_in_dx', dgu, W['ffn_in'][l], h_mid, dh_out, g_ffn[l], mv(l, 4), tmm, gate=(y_mix, mv(l, 2)))
        return dh_mid, dy_mix, d_wi, d_wo, d_g, (d_sh2, d_sc2), d_gt1, sum_dy, sent

    xm0 = _modnorm_fwd('l0_mix_norm', x, g_mix[0], mv(0, 1), mv(0, 0), tm)
    xc0 = _modnorm_fwd('l0_ctx_norm', ctx, g_mix[0], csc1, csh1, tmc)
    xall = jnp.concatenate([xc0, xm0], axis=0)
    p_qkv, q_all, k_all, v_all = _mm_qk_prep('l0_qkv', xall, W['w_qkv'], cos, sin, gq, gk, bdq, bdk, tmr)
    p_sg, sg = _mm_sg('l0_sg_in', xm0, W['w_sgp'], w_sp, bsp, tmm)
    kh = _to_heads(k_all, N_KV_HEADS)
    vh = _to_heads(v_all, N_KV_HEADS)
    v_ones = jnp.concatenate([vh, jnp.ones(vh.shape[:2] + (1,), BF16),
                              jnp.zeros(vh.shape[:2] + (HEAD_DIM - 1,), BF16)], axis=2)
    o, lse, *gw = _attn_fwd('l0_attn', q_all, kh, v_ones, n_ctx, tq, gather=pending[:4])
    W['w_out'] = gw[0].reshape(D_MODEL, D_MODEL)
    W['ffn_in'] = [gw[1].reshape(N_CHIP, D_MODEL, FF_BLK), gw[3].reshape(N_CHIP, D_MODEL, FF_BLK)]
    W['ffn_out'] = [gw[2].reshape(D_FF, D_MODEL), None]
    cat = jnp.concatenate([o.astype(BF16), sg], axis=1)
    y0, h1, xf0 = _mm_rows_res('l0_out', cat, W['w_out'], x, mv(0, 2), tmw, norm=(g_ffn[0], mv(0, 4), mv(0, 3)))
    (f0, h2, xm1), ffn0 = ffn_fwd(0, h1, xf0, gather=pending[4:], norm=(g_mix[1], mv(1, 1), mv(1, 0)))

    ag, hg = _mm_glu('l1_pw1', xm1, W['pw1'], W['b_pw1'], tmm)
    hc = _dwconv('l1_conv', hg, W['w_dw'], W['b_dw'])
    s1 = _lnsilu_fwd('l1_ln_silu', hc, W['ln_g'], W['ln_b'], tm)
    y1, h3, xf1 = _mm_rows_res('l1_pw2', s1, W['pw2'], h2, mv(1, 2), tmw, bias=W['b_pw2'],
                               norm=(g_ffn[1], mv(1, 4), mv(1, 3)))
    (dh4, d_g_final, loss_vec, dyf1, d_gt2_1), ffn1 = ffn_fwd(1, h3, xf1, final=(tgt, W['g_final']))

    dh3, dy1, d_wi1, d_wo1, d_gffn1, dmod1_ffn, d_gt1_1, d_b_pw2, _ = ffn_bwd(1, dh4, dyf1, y1, ffn1)
    d_pw2 = _mm_wgrad('l1_pw2_dw', s1, dy1, tmw)
    dhc, d_ln_g, d_ln_b = _mm_t_epilogue('l1_pw2_dx', dy1, W['pw2'], [(hc, None)], [W['ln_g'], W['ln_b']], 2,
                                         _lnsilu_bwd_epilogue, tmm)
    dhg = _dwconv('l1_conv_dx', dhc, W['w_dw'][::-1], jnp.zeros((1, D_MODEL), F32))
    d_w_dw, d_b_dw = _dwconv_wgrad('l1_conv_dw', hg, dhc)
    dag, d_b_pw1 = _glu_bwd('l1_glu_bwd', ag, dhg, W['b_pw1'], tm)
    d_pw1 = _mm_wgrad('l1_pw1_dw', xm1, dag, tmw).reshape(D_MODEL, N_CHIP, PW1_BLK).transpose(1, 0, 2)
    dh2, dyf0, d_sh1_1, d_sc1_1, d_gmix1, d_gt2_0, _ = _mm_t_epilogue(
        'l1_pw1_dx', dag, W['pw1'], [(h2, None), (dh3, None)], [g_mix[1], mv(1, 1)], 3, _norm_bwd_epilogue, tmm,
        gate=(f0, mv(0, 5)))

    gs1 = [blocked(g) for g in (d_wi1, d_wo1, d_pw1, d_pw2)]
    dh1, dy0, d_wi0, d_wo0, d_gffn0, dmod0_ffn, d_gt1_0, _, sib1 = ffn_bwd(0, dh2, dyf0, y0, ffn0, pair_send=gs1)
    dcat = _mm_rows('l0_out_dx', dy0, W['w_out'], tmw, trans_b=True)
    d_w_out = _mm_wgrad('l0_out_dw', cat, dy0, tmw)

    gs0 = [blocked(g) for g in (d_wi0, d_wo0, d_w_out)]
    sib0 = _rs_pair_send('rs_pair_send', gs0)
    gs = [gs0[0], gs1[0], gs0[1], gs1[1], gs0[2], gs1[2], gs1[3]]
    sib = [sib0[0], sib1[0], sib0[1], sib1[1], sib0[2], sib1[2], sib1[3]]
    ss = [_rs_pair_add(f'rs_pair_add{t}', gs[t], sib[t], cvec) for t in range(len(gs))]
    dq, dk, dv, *xs = _attn_bwd('l0_attn_bwd', q_all, kh, vh, o, lse, dcat, n_ctx, tq, exchange=ss)
    dk, dv = [t.transpose(2, 0, 1).reshape(n_ctx + seq, KV_W) for t in (dk, dv)]
    dp_qkv, d_gq, d_gk = _qk_prep_bwd('l0_qk_prep_bwd', p_qkv, cos, sin, dq, n_ctx // tm, dk, dv, gq, gk, bdq, bdk,
                                      tm)
    dp_sg, d_w_sp, d_bsp = _sg_bwd('l0_sg_bwd', p_sg, dcat, w_sp, w_sp_t, bsp, tm)
    d_w_qkv = _mm_wgrad('l0_qkv_dw', xall, dp_qkv, tmr)
    d_w_sgp = _mm_wgrad('l0_sg_in_dw', xm0, dp_sg, tmw)
    dxall = _mm_rows('l0_qkv_dx', dp_qkv, W['w_qkv'], tmr, trans_b=True)
    g_in = blocked(jnp.concatenate([d_w_qkv, d_w_sgp], axis=1).reshape(D_MODEL, N_CHIP, -1).transpose(1, 0, 2))
    sib_in = _rs_pair_send('rs_pair_send_w_in', [g_in])
    ss_in = _rs_pair_add('rs_pair_add_w_in', g_in, sib_in[0], cvec)
    dx, d_sh1_0, d_sc1_0, d_gmix0, *xs_in = _mm_t_epilogue(
        'l0_sg_in_dx', dp_sg, W['w_sgp'], [(x, None), (dh1, None), (dxall, lambda i: i + n_ctx // tm)],
        [g_mix[0], mv(0, 1)], 3, _norm_bwd_epilogue, tm, exchange=[ss_in])
    _, d_csh1, d_csc1, d_gmix0c = _modnorm_bwd('l0_ctx_norm_bwd', ctx, [_ri(dxall)], None, g_mix[0], csc1, tmc)

    zero = jnp.zeros((1, D_MODEL), F32)
    d = D_MODEL
    pack = jnp.concatenate([
        d_gmix0 + d_gmix0c, d_gmix1, d_gffn0, d_gffn1, d_g_final,
        jnp.concatenate([d_gq, d_gk, jnp.zeros((1, d - ATTN_W - KV_W), F32)], axis=1),
        d_b_pw1.reshape(2, d), d_w_dw, d_b_dw, d_ln_g, d_ln_b, d_b_pw2,
        d_sh1_0, d_sc1_0, d_gt1_0, *dmod0_ffn, d_gt2_0, d_sh1_1, d_sc1_1, d_gt1_1, *dmod1_ffn, d_gt2_1,
        d_csh1, d_csc1, zero, zero, zero, zero, jnp.zeros((3, d), F32),
        d_w_sp.reshape(64, d),
        jnp.pad(d_bsp.reshape(CHUNK, N_SG, CHUNK).sum(axis=-1).reshape(1, CHUNK * N_SG), ((0, 7), (0, d - SG_W)))],
        axis=0)
    assert pack.shape == (SMALL_ROWS, d)

    t_ffn_in = _sum_into('rs_sum_ffn_in1', xs[1], cvec, 2, 1, prev=_sum_into('rs_sum_ffn_in0', xs[0], cvec, 2, 0))
    t_ffn_out = _sum_into('rs_sum_ffn_out1', xs[3], cvec, 2, 1, prev=_sum_into('rs_sum_ffn_out0', xs[2], cvec, 2, 0))
    *reduced, small = _rs_pair_share('rs_pair_share', [
        t_ffn_in, t_ffn_out, _sum_into('rs_sum_w_in', xs_in[0], cvec, 1, 0), _sum_into('rs_sum_w_out', xs[4], cvec, 1, 0),
        _sum_into('rs_sum_pw1', xs[5], cvec, 1, 0), _sum_into('rs_sum_pw2', xs[6], cvec, 1, 0)], gather=[pack])
    big = dict(zip(['w_ffn_in', 'w_ffn_out', 'w_in', 'w_out', 'w_pw1', 'w_pw2'], reduced))
    return loss_vec, dx, small, big


def kernel(x, c, ctx, c_ctx, w_mod, b_mod, g_mix, g_ffn, w_ffn_in, w_ffn_out, w_in, q_gain, k_gain, w_sp, b_sp, w_out, w_pw1, b_pw1, w_dw, b_dw, ln_g, ln_b, w_pw2, b_pw2, g_final, loss_target, m_c_ctx, m_w_mod, m_b_mod, m_g_mix, m_g_ffn, m_w_ffn_in, m_w_ffn_out, m_w_in, m_q_gain, m_k_gain, m_w_sp, m_b_sp, m_w_out, m_w_pw1, m_b_pw1, m_w_dw, m_b_dw, m_ln_g, m_ln_b, m_w_pw2, m_b_pw2, m_g_final, v_c_ctx, v_w_mod, v_b_mod, v_g_mix, v_g_ffn, v_w_ffn_in, v_w_ffn_out, v_w_in, v_q_gain, v_k_gain, v_w_sp, v_b_sp, v_w_out, v_w_pw1, v_b_pw1, v_w_dw, v_b_dw, v_ln_g, v_ln_b, v_w_pw2, v_b_pw2, v_g_final):
    given = dict(locals())
    ix, iy, ic = _mesh_pos()
    chip = 2 * ix + iy
    me = 2 * chip + ic
    d = D_MODEL
    q4 = d // N_CHIP

    small = jnp.concatenate([c.reshape(4, q4), b_pw1.reshape(2, q4), w_dw[0], b_dw, ln_g, ln_b, b_pw2,
                             jnp.zeros((7, q4), F32)], axis=0)
    def my_half(a):
        r = a.shape[0] // 2
        return lax.dynamic_slice_in_dim(a, ic * r, r, axis=0).astype(BF16)

    sgath, g_w_in = _allgather('ag_first', [small, my_half(w_in[0])])
    c_all = sgath[:, 0:4].reshape(N_DEV, d)
    per_chip = sgath[0::2]
    b_pw1_f = per_chip[:, 4:6].reshape(1, 2 * d)
    w_dw_f = per_chip[:, 6:6 + CONV_W].transpose(1, 0, 2).reshape(CONV_W, d)
    b_dw_f, ln_g_f, ln_b_f, b_pw2_f = [per_chip[:, 37 + i].reshape(1, d) for i in range(4)]

    c16 = jnp.concatenate([c_all, c_ctx[None, :], jnp.zeros((7, d), F32)], axis=0)
    bm_sh = lax.dynamic_slice_in_dim(b_mod, chip * MOD_BLK, MOD_BLK, axis=1).reshape(2, 1, MOD_BLK)
    mod_piece = _mod_fwd('mod_fwd', c16, w_mod, bm_sh)
    mgath = _allgather('ag_mod', [mod_piece.reshape(32, MOD_BLK)])[0]
    mod_all = mgath[0::2].reshape(N_CHIP, 2, 16, MOD_BLK).transpose(1, 2, 0, 3).reshape(2, 16, MOD_W)
    mod_me = lax.dynamic_index_in_dim(mod_all, me, axis=1, keepdims=False)
    cmod = mod_all[0, N_DEV]

    w_in_f = g_w_in.reshape(N_CHIP, d, -1).transpose(1, 0, 2).reshape(d, -1)
    W = dict(
        w_qkv=w_in_f[:, :QKV_W], w_sgp=w_in_f[:, QKV_W:],
        g_mix=g_mix, g_ffn=g_ffn, g_final=g_final.reshape(1, d), q_gain=q_gain, k_gain=k_gain, w_sp=w_sp[0],
        b_sp=b_sp[0], b_pw1=b_pw1_f, w_dw=w_dw_f, b_dw=b_dw_f, ln_g=ln_g_f, ln_b=ln_b_f, b_pw2=b_pw2_f)
    pending = [my_half(w_out[0]), my_half(w_ffn_in[0]), my_half(w_ffn_out[0]), my_half(w_ffn_in[1]),
               my_half(w_pw2[0]), my_half(w_ffn_out[1]), my_half(w_pw1[0])]
    cvec = jnp.reshape(ic, (1,)).astype(jnp.int32)

    loss_vec, dx, pg, big = _sample_step(x[0], ctx[0], loss_target[0], mod_me, cmod, W, pending, cvec)
    loss = lax.psum(jnp.sum(loss_vec), ("x", "y", "c"))

    s8 = _sum_slots('sum_small_grads', pg, SMALL_ROWS)
    grads = dict(big)

    def chip_cols(a, width):
        return lax.dynamic_slice_in_dim(a, chip * width, width, axis=a.ndim - 1)

    grads['g_mix'] = s8[0:2]
    grads['g_ffn'] = s8[2:4]
    grads['g_final'] = s8[4]
    grads['q_gain'] = s8[5, :ATTN_W].reshape(N_Q_HEADS, HEAD_DIM).sum(axis=0)[None, :]
    grads['k_gain'] = s8[5, ATTN_W:ATTN_W + KV_W].reshape(N_KV_HEADS, HEAD_DIM).sum(axis=0)[None, :]
    grads['b_pw1'] = chip_cols(s8[6:8].reshape(1, 2 * d), 2 * q4)
    grads['w_dw'] = chip_cols(s8[8:8 + CONV_W], q4)[None]
    grads['b_dw'] = chip_cols(s8[39:40], q4)
    grads['ln_g'] = chip_cols(s8[40:41], q4)
    grads['ln_b'] = chip_cols(s8[41:42], q4)
    grads['b_pw2'] = chip_cols(s8[42:43], q4)
    grads['w_sp'] = s8[64:128].reshape(1, N_SG, CHUNK, CHUNK)
    grads['b_sp'] = s8[128, :SG_W].reshape(CHUNK, N_SG).T[None]

    dm_rows = pg[:, 43:55].reshape(N_DEV, 2, MOD_W)
    ctx_row = s8[55:61].reshape(1, MOD_W)
    dm = jnp.stack([jnp.concatenate([dm_rows[:, 0], ctx_row, jnp.zeros((7, MOD_W), F32)], axis=0),
                    jnp.concatenate([dm_rows[:, 1], jnp.zeros((8, MOD_W), F32)], axis=0)], axis=0)
    grads['b_mod'] = jnp.stack([s8[43:49].reshape(MOD_W) + ctx_row[0], s8[49:55].reshape(MOD_W)], axis=0)
    dm_sh = chip_cols(dm, MOD_BLK)
    grads['w_mod'] = _mod_wgrad('mod_dw', c16, dm_sh)
    d_silu = _mm_rows('mod_dx', dm_sh[0], w_mod[0], 16, trans_b=True)
    parts = _allgather('ag_cctx', [d_silu[8:16]])[0]
    grads['c_ctx'] = _cctx_grad('cctx_grad', parts, c_ctx.reshape(1, d))[0]

    deltas, new_m, new_v = {}, {}, {}
    for n in WEIGHT_NAMES:
        grads[n] = grads[n].reshape(given[n].shape)
        deltas[n], new_m[n], new_v[n] = _adamw(f'adamw_{n}', given[n], grads[n], given['m_' + n], given['v_' + n])

    return (loss, dx[None], *[grads[n] for n in WEIGHT_NAMES], *[deltas[n] for n in WEIGHT_NAMES],
            *[new_m[n] for n in WEIGHT_NAMES], *[new_v[n] for n in WEIGHT_NAMES])
```

```python
import functools
import math

import jax
import jax.numpy as jnp
from jax import lax
from jax.experimental import pallas as pl
from jax.experimental.pallas import tpu as pltpu

F32 = jnp.float32
BF16 = jnp.bfloat16
MESH = pl.DeviceIdType.MESH

D_MODEL = 1024
HEAD_DIM = 64
N_Q_HEADS = 8
N_KV_HEADS = 2
Q_GROUP = N_Q_HEADS // N_KV_HEADS
ATTN_W = N_Q_HEADS * HEAD_DIM
KV_W = N_KV_HEADS * HEAD_DIM
QKV_W = ATTN_W + 2 * KV_W
SG_W = D_MODEL - ATTN_W
N_SG = 4
CHUNK = 128
GRID_W = 64
ROPE_THETA = 10000.0
CONV_W = 31
CONV_HALO = 16
D_FF = 2816
MOD_W = 6 * D_MODEL
N_CHIP = 4
N_DEV = 8
FF_BLK = 2 * D_FF // N_CHIP
PW1_BLK = 2 * D_MODEL // N_CHIP
MOD_BLK = MOD_W // N_CHIP
EPS = 1e-6
GELU_C = math.sqrt(2.0 / math.pi)
GELU_A = 0.044715

ADAM_LR = 0.001
ADAM_B1 = 0.9
ADAM_B2 = 0.999
ADAM_EPS = 1e-08
ADAM_WD = 0.01
ADAM_STEP = 10

SMALL_ROWS = 136

VMEM_LIMIT_BYTES = 56 * 1024 * 1024

WEIGHT_NAMES = ['c_ctx', 'w_mod', 'b_mod', 'g_mix', 'g_ffn', 'w_ffn_in', 'w_ffn_out', 'w_in', 'q_gain', 'k_gain',
                'w_sp', 'b_sp', 'w_out', 'w_pw1', 'b_pw1', 'w_dw', 'b_dw', 'ln_g', 'ln_b', 'w_pw2', 'b_pw2',
                'g_final']


def _params():
    return pltpu.CompilerParams(vmem_limit_bytes=VMEM_LIMIT_BYTES)


def _ri(arr, width=None, col_block=0, row_off=0, row_fn=None):
    return (arr, arr.shape[1] if width is None else width, col_block,
            (lambda i: i + row_off) if row_fn is None else row_fn)


def _rowwise(name, body, n_tiles, tm, row_ins, vec_ins, row_outs, vec_outs):
    nri, nvi, nro = len(row_ins), len(vec_ins), len(row_outs)

    def kern(*refs):
        ri = refs[:nri]
        vi = refs[nri:nri + nvi]
        ro = refs[nri + nvi:nri + nvi + nro]
        vo = refs[nri + nvi + nro:]
        if vo:
            @pl.when(pl.program_id(0) == 0)
            def _():
                for r in vo:
                    r[...] = jnp.zeros(r.shape, r.dtype)
        body(ri, vi, ro, vo)

    in_specs = [pl.BlockSpec((tm, bw), lambda i, cb=cb, rf=rf: (rf(i), cb)) for (_, bw, cb, rf) in row_ins]
    in_specs += [pl.BlockSpec(a.shape, lambda i, nd=a.ndim: (0,) * nd) for a in vec_ins]
    out_specs = [pl.BlockSpec((tm, w), lambda i: (i, 0)) for (w, _) in row_outs]
    out_specs += [pl.BlockSpec(s, lambda i, nd=len(s): (0,) * nd) for s in vec_outs]
    out_shape = [jax.ShapeDtypeStruct((n_tiles * tm, w), dt) for (w, dt) in row_outs]
    out_shape += [jax.ShapeDtypeStruct(s, F32) for s in vec_outs]
    return pl.pallas_call(kern, name=name, grid=(n_tiles,), in_specs=in_specs, out_specs=out_specs,
                          out_shape=out_shape, compiler_params=_params())(*[a for (a, _, _, _) in row_ins], *vec_ins)


def _colsum(v):
    return jnp.sum(v, axis=0, keepdims=True)


def _rowmean(v):
    return jnp.mean(v, axis=-1, keepdims=True)


def _rms(h):
    r = lax.rsqrt(_rowmean(h * h) + EPS)
    return h * r, r


def _rms_bwd(n, r, gy):
    return r * (gy - n * _rowmean(gy * n))


def _ln_stats(v):
    xc = v - _rowmean(v)
    r = lax.rsqrt(_rowmean(xc * xc) + EPS)
    return xc * r, r


def _ln_bwd(xh, r, dxh):
    return r * (dxh - _rowmean(dxh) - xh * _rowmean(dxh * xh))


def _sigmoid(v):
    return 1.0 / (1.0 + jnp.exp(-v))


def _gelu(v):
    t = jnp.tanh(GELU_C * (v + GELU_A * (v * v * v)))
    return 0.5 * v * (1.0 + t), t


def _gelu_grad(v, t):
    return 0.5 * (1.0 + t) + 0.5 * v * (1.0 - t * t) * (GELU_C * (1.0 + 3.0 * GELU_A * (v * v)))


def _modnorm_fwd(name, h, g, sc, sh, tm):
    def body(ri, vi, ro, vo):
        n, _ = _rms(ri[0][...])
        ro[0][...] = ((n * vi[0][...]) * (1.0 + vi[1][...]) + vi[2][...]).astype(BF16)

    return _rowwise(name, body, h.shape[0] // tm, tm, [_ri(h)], [g, sc, sh], [(D_MODEL, BF16)], [])[0]


def _modnorm_bwd_tile(hv, dxm, gv, scv):
    n, r = _rms(hv)
    dy = dxm * (1.0 + scv)
    return _rms_bwd(n, r, dy * gv), _colsum(dxm), _colsum(dxm * (n * gv)), _colsum(dy * n)


def _mm_row_epilogue(name, mm_args, mm_specs, dx_fn, rows_in, vecs_in, n_vec_out, epilogue, tm, gate=None,
                     exchange=()):
    rows_in = list(rows_in) + ([(gate[0], None)] if gate else [])
    vecs_in = list(vecs_in) + ([gate[1]] if gate else [])
    ng = 1 if gate else 0
    n_vec_out += 2 * ng
    nmm, nr, nv, ne = len(mm_args), len(rows_in), len(vecs_in), len(exchange)
    m = rows_in[0][0].shape[0]
    last = m // tm - 1

    def kern(*refs):
        ins = refs[:nmm + nr + nv]
        o_ref = refs[nmm + nr + nv + ne]
        vo = refs[nmm + nr + nv + ne + 1 + ng:nmm + nr + nv + ne + 1 + ng + n_vec_out]
        if ne:
            srcs = refs[nmm + nr + nv:nmm + nr + nv + ne]
            rest = refs[nmm + nr + nv + ne + 1 + ng + n_vec_out:]
            start, finish = _exchange_phases(srcs, rest[:ne], *rest[ne:])
            pl.when(pl.program_id(0) == 0)(start)

        @pl.when(pl.program_id(0) == 0)
        def _():
            for r in vo:
                r[...] = jnp.zeros(r.shape, F32)

        row_out, incs = epilogue(dx_fn(*ins[:nmm]), [r[...] for r in ins[nmm:nmm + nr - ng]],
                                 [r[...] for r in ins[nmm + nr:nmm + nr + nv - ng]])
        o_ref[...] = row_out
        if ng:
            dy = row_out * ins[nmm + nr + nv - 1][...]
            refs[nmm + nr + nv + ne + 1][...] = dy.astype(BF16)
            incs = list(incs) + [_colsum(row_out * ins[nmm + nr - 1][...]), _colsum(dy)]
        for r, inc in zip(vo, incs):
            r[...] += inc
        if ne:
            pl.when(pl.program_id(0) == last)(finish)

    row = pl.BlockSpec((tm, D_MODEL), lambda i: (i, 0))
    vec = pl.BlockSpec((1, D_MODEL), lambda i: (0, 0))
    row_specs = [row if fn is None else pl.BlockSpec((tm, D_MODEL), lambda i, fn=fn: (fn(i), 0)) for _, fn in rows_in]
    return pl.pallas_call(
        kern, name=name, grid=(m // tm,), in_specs=list(mm_specs) + row_specs + [vec] * nv + [_ANY] * ne,
        out_specs=[row] * (1 + ng) + [vec] * n_vec_out + [_ANY] * ne,
        out_shape=[jax.ShapeDtypeStruct((m, D_MODEL), F32)] + [jax.ShapeDtypeStruct((m, D_MODEL), BF16)] * ng
        + [jax.ShapeDtypeStruct((1, D_MODEL), F32)] * n_vec_out
        + [jax.ShapeDtypeStruct(e.shape, e.dtype) for e in exchange],
        scratch_shapes=_exchange_scratch(ne) if ne else [],
        compiler_params=_params())(*mm_args, *[r for r, _ in rows_in], *vecs_in, *exchange)


def _norm_bwd_epilogue(dxm, rows, vecs):
    if len(rows) > 2:
        dxm = dxm + rows[2]
    dx, dsh, dsc, dg = _modnorm_bwd_tile(rows[0], dxm, vecs[0], vecs[1])
    return rows[1] + dx, [dsh, dsc, dg]


def _lnsilu_bwd_epilogue(ds, rows, vecs):
    xh, r = _ln_stats(rows[0])
    ln = xh * vecs[0] + vecs[1]
    sg = _sigmoid(ln)
    dln = ds * (sg * (1.0 + ln * (1.0 - sg)))
    return _ln_bwd(xh, r, dln * vecs[0]), [_colsum(dln * xh), _colsum(dln)]


def _modnorm_bwd(name, h, dxs, dh_in, g, sc, tm):
    ndx = len(dxs)

    def body(ri, vi, ro, vo):
        dxm = ri[1][...].astype(F32)
        for r in ri[2:1 + ndx]:
            dxm = dxm + r[...].astype(F32)
        dx, dsh, dsc, dg = _modnorm_bwd_tile(ri[0][...], dxm, vi[0][...], vi[1][...])
        vo[0][...] += dsh
        vo[1][...] += dsc
        vo[2][...] += dg
        if dh_in is not None:
            ro[0][...] = ri[1 + ndx][...] + dx

    row_ins = [_ri(h)] + list(dxs) + ([_ri(dh_in)] if dh_in is not None else [])
    row_outs = [(D_MODEL, F32)] if dh_in is not None else []
    outs = _rowwise(name, body, h.shape[0] // tm, tm, row_ins, [g, sc], row_outs, [(1, D_MODEL)] * 3)
    if dh_in is None:
        return (None, *outs)
    return tuple(outs)


def _head_mean(v, bd):
    hi = v.astype(BF16)
    lo = (v - hi.astype(F32)).astype(BF16)
    s = jnp.dot(hi, bd, preferred_element_type=F32) + jnp.dot(lo, bd, preferred_element_type=F32)
    return s * (1.0 / HEAD_DIM)


def _swap16(v):
    w = v.shape[1]
    lane = lax.broadcasted_iota(jnp.int32, v.shape, 1)
    return jnp.where((lane & 16) == 0, pltpu.roll(v, w - 16, 1), pltpu.roll(v, 16, 1))


def _rope(v, cos, sin):
    return v * cos + _swap16(v) * sin


def _rope_bwd(d, cos, sin):
    return d * cos + _swap16(d * sin)


def _q_wide(t):
    return jnp.concatenate([t] * (ATTN_W // KV_W), axis=1)


def _mm_qk_prep(name, xall, w_qkv, cos, sin, gq, gk, bdq, bdk, tm):
    r, k = xall.shape

    def kern(x_ref, w_ref, cos_ref, sin_ref, gq_ref, gk_ref, bdq_ref, bdk_ref, p_ref, q_ref, k_ref, v_ref):
        p = jnp.dot(x_ref[...], w_ref[...], preferred_element_type=F32)
        p_ref[...] = p
        cosv = cos_ref[...]
        sinv = sin_ref[...]
        pq = p[:, :ATTN_W]
        rq = lax.rsqrt(_head_mean(pq * pq, bdq_ref[...]) + EPS)
        q_ref[...] = (_rope(pq * rq * gq_ref[...], _q_wide(cosv), _q_wide(sinv)) * (HEAD_DIM ** -0.5)).astype(BF16)
        pk = p[:, ATTN_W:ATTN_W + KV_W]
        rk = lax.rsqrt(_head_mean(pk * pk, bdk_ref[...]) + EPS)
        k_ref[...] = _rope(pk * rk * gk_ref[...], cosv, sinv).astype(BF16)
        v_ref[...] = p[:, ATTN_W + KV_W:].astype(BF16)

    def rows(w):
        return pl.BlockSpec((tm, w), lambda i: (i, 0))

    def whole(a):
        return pl.BlockSpec(a.shape, lambda i: (0, 0))

    return pl.pallas_call(
        kern, name=name, grid=(r // tm,),
        in_specs=[rows(k), whole(w_qkv), rows(KV_W), rows(KV_W), whole(gq), whole(gk), whole(bdq), whole(bdk)],
        out_specs=[rows(QKV_W), rows(ATTN_W), rows(KV_W), rows(KV_W)],
        out_shape=[jax.ShapeDtypeStruct((r, QKV_W), F32), jax.ShapeDtypeStruct((r, ATTN_W), BF16),
                   jax.ShapeDtypeStruct((r, KV_W), BF16), jax.ShapeDtypeStruct((r, KV_W), BF16)],
        compiler_params=_params())(xall, w_qkv, cos, sin, gq, gk, bdq, bdk)


def _qk_prep_bwd(name, p_qkv, cos, sin, dq, n_ctx_tiles, dk, dv, gq, gk, bdq, bdk, tm):
    def norm_rope_bwd(p, gain, bd, cosv, sinv, dout):
        r = lax.rsqrt(_head_mean(p * p, bd) + EPS)
        n = p * r
        dqn = _rope_bwd(dout, cosv, sinv)
        gy = dqn * gain
        return r * (gy - n * _head_mean(gy * n, bd)), _colsum(dqn * n)

    def body(ri, vi, ro, vo):
        pkv = ri[1][...]
        cosv = ri[2][...]
        sinv = ri[3][...]
        dqv = jnp.where(pl.program_id(0) >= n_ctx_tiles, ri[4][...] * (HEAD_DIM ** -0.5), 0.0)
        dxq, dgq = norm_rope_bwd(ri[0][...], vi[0][...], vi[2][...], _q_wide(cosv), _q_wide(sinv), dqv)
        dxk, dgk = norm_rope_bwd(pkv[:, :KV_W], vi[1][...], vi[3][...], cosv, sinv, ri[5][...])
        ro[0][:, :ATTN_W] = dxq.astype(BF16)
        ro[0][:, ATTN_W:ATTN_W + KV_W] = dxk.astype(BF16)
        ro[0][:, ATTN_W + KV_W:] = ri[6][...].astype(BF16)
        vo[0][...] += dgq
        vo[1][...] += dgk

    return _rowwise(name, body, p_qkv.shape[0] // tm, tm,
                    [_ri(p_qkv, ATTN_W, 0), _ri(p_qkv, 2 * KV_W, 2), _ri(cos), _ri(sin),
                     _ri(dq, row_fn=lambda i: jnp.maximum(i - n_ctx_tiles, 0)), _ri(dk), _ri(dv)],
                    [gq, gk, bdq, bdk], [(QKV_W, BF16)], [(1, ATTN_W), (1, KV_W)])


def _mm_sg(name, xm, w_sgp, w_sp, bsp, tm):
    m, k = xm.shape

    def kern(x_ref, w_ref, wsp_ref, bsp_ref, p_ref, o_ref):
        p = jnp.dot(x_ref[...], w_ref[...], preferred_element_type=F32)
        p_ref[...] = p
        gu, _ = _gelu(p[:, :SG_W])
        gv, _ = _gelu(p[:, SG_W:])
        for g in range(N_SG):
            sl = slice(g * CHUNK, (g + 1) * CHUNK)
            vn, _ = _ln_stats(gv[:, sl])
            vnb = vn.astype(BF16)
            for c in range(tm // CHUNK):
                rs = slice(c * CHUNK, (c + 1) * CHUNK)
                mixed = jnp.dot(wsp_ref[g], vnb[rs], preferred_element_type=F32) + bsp_ref[:, sl]
                o_ref[rs, sl] = (gu[rs, sl] * mixed).astype(BF16)

    return pl.pallas_call(
        kern, name=name, grid=(m // tm,),
        in_specs=[pl.BlockSpec((tm, k), lambda i: (i, 0)), pl.BlockSpec(w_sgp.shape, lambda i: (0, 0)),
                  pl.BlockSpec(w_sp.shape, lambda i: (0, 0, 0)), pl.BlockSpec(bsp.shape, lambda i: (0, 0))],
        out_specs=[pl.BlockSpec((tm, 2 * SG_W), lambda i: (i, 0)), pl.BlockSpec((tm, SG_W), lambda i: (i, 0))],
        out_shape=[jax.ShapeDtypeStruct((m, 2 * SG_W), F32), jax.ShapeDtypeStruct((m, SG_W), BF16)],
        compiler_params=_params())(xm, w_sgp, w_sp, bsp)


def _sg_bwd(name, p_sg, dcat, w_sp, w_sp_t, bsp, tm):
    def body(ri, vi, ro, vo):
        p = ri[0][...]
        dsg = ri[1][...]
        su = p[:, :SG_W]
        sv = p[:, SG_W:]
        gu, tu = _gelu(su)
        gv, tv = _gelu(sv)
        for g in range(N_SG):
            sl = slice(g * CHUNK, (g + 1) * CHUNK)
            vn, r = _ln_stats(gv[:, sl])
            vnb = vn.astype(BF16)
            for m in range(tm // CHUNK):
                rs = slice(m * CHUNK, (m + 1) * CHUNK)
                mixed = jnp.dot(vi[0][g], vnb[rs], preferred_element_type=F32) + vi[2][:, sl]
                d_o = dsg[rs, sl]
                dm = d_o * gu[rs, sl]
                vo[1][:, sl] += dm
                dmb = dm.astype(BF16)
                vo[0][g] += lax.dot_general(dmb, vnb[rs], (((1,), (1,)), ((), ())), preferred_element_type=F32)
                dvn = jnp.dot(vi[1][g], dmb, preferred_element_type=F32)
                dgv = _ln_bwd(vn[rs], r[rs], dvn)
                ro[0][rs, sl] = (d_o * mixed * _gelu_grad(su[rs, sl], tu[rs, sl])).astype(BF16)
                ro[0][rs, SG_W + g * CHUNK:SG_W + (g + 1) * CHUNK] = (
                    dgv * _gelu_grad(sv[rs, sl], tv[rs, sl])).astype(BF16)

    return _rowwise(name, body, p_sg.shape[0] // tm, tm, [_ri(p_sg), _ri(dcat, SG_W, 1)], [w_sp, w_sp_t, bsp],
                    [(2 * SG_W, BF16)], [(N_SG, CHUNK, CHUNK), (CHUNK, SG_W)])


def _glu_bwd(name, ag, dhg, b, tm):
    def body(ri, vi, ro, vo):
        v = ri[0][...] + vi[0][...]
        d = ri[1][...]
        a = v[:, :D_MODEL]
        sg = _sigmoid(v[:, D_MODEL:])
        da = d * sg
        dgate = d * a * sg * (1.0 - sg)
        ro[0][:, :D_MODEL] = da.astype(BF16)
        ro[0][:, D_MODEL:] = dgate.astype(BF16)
        vo[0][:, :D_MODEL] += _colsum(da)
        vo[0][:, D_MODEL:] += _colsum(dgate)

    return _rowwise(name, body, ag.shape[0] // tm, tm, [_ri(ag), _ri(dhg)], [b], [(2 * D_MODEL, BF16)],
                    [(1, 2 * D_MODEL)])


def _lnsilu_fwd(name, hc, g, b, tm):
    def body(ri, vi, ro, vo):
        xh, _ = _ln_stats(ri[0][...])
        ln = xh * vi[0][...] + vi[1][...]
        ro[0][...] = (ln * _sigmoid(ln)).astype(BF16)

    return _rowwise(name, body, hc.shape[0] // tm, tm, [_ri(hc)], [g, b], [(D_MODEL, BF16)], [])[0]


def _conv_blocks(seq):
    cb = 128
    tt = 128 if seq % 128 == 0 else seq
    return cb, tt


def _conv_taps(win, tt):
    n = win.shape[0]
    for s in range(8):
        ws = win if s == 0 else pltpu.roll(win, n - s, 0)
        for q in range(4):
            j = 8 * q + s - 1
            if 0 <= j < CONV_W:
                yield j, ws[8 * q:8 * q + tt, :]


def _fill_padded(pad_ref, x_ref, seq):
    zeros = jnp.zeros((CONV_HALO, pad_ref.shape[1]), F32)
    pad_ref[0:CONV_HALO, :] = zeros
    pad_ref[seq + CONV_HALO:seq + 2 * CONV_HALO, :] = zeros
    pad_ref[CONV_HALO:seq + CONV_HALO, :] = x_ref[...]


def _dwconv(name, xin, w, b):
    seq = xin.shape[0]
    cb, tt = _conv_blocks(seq)

    def kern(x_ref, w_ref, b_ref, o_ref, pad_ref):
        _fill_padded(pad_ref, x_ref, seq)
        wv = w_ref[...]
        bv = b_ref[...]

        def step(t, carry):
            base = pl.multiple_of(t * tt, tt)
            win = pad_ref[pl.ds(base, tt + 2 * CONV_HALO), :]
            acc = jnp.zeros((tt, cb), F32) + bv
            for j, rows in _conv_taps(win, tt):
                acc = acc + wv[j:j + 1, :] * rows
            o_ref[pl.ds(base, tt), :] = acc
            return carry

        lax.fori_loop(0, seq // tt, step, 0)

    return pl.pallas_call(
        kern, name=name, grid=(D_MODEL // cb,),
        in_specs=[pl.BlockSpec((seq, cb), lambda j: (0, j)), pl.BlockSpec((CONV_W, cb), lambda j: (0, j)),
                  pl.BlockSpec((1, cb), lambda j: (0, j))],
        out_specs=pl.BlockSpec((seq, cb), lambda j: (0, j)),
        out_shape=jax.ShapeDtypeStruct((seq, D_MODEL), F32),
        scratch_shapes=[pltpu.VMEM((seq + 2 * CONV_HALO, cb), F32)],
        compiler_params=_params())(xin, w, b)


def _dwconv_wgrad(name, xin, dout):
    seq = xin.shape[0]
    cb, tt = _conv_blocks(seq)

    def kern(x_ref, d_ref, dw_ref, db_ref, pad_ref):
        _fill_padded(pad_ref, x_ref, seq)
        dw_ref[...] = jnp.zeros(dw_ref.shape, F32)
        db_ref[...] = jnp.zeros(db_ref.shape, F32)

        def step(t, carry):
            base = pl.multiple_of(t * tt, tt)
            win = pad_ref[pl.ds(base, tt + 2 * CONV_HALO), :]
            d = d_ref[pl.ds(base, tt), :]
            db_ref[...] += _colsum(d)
            for j, rows in _conv_taps(win, tt):
                dw_ref[j:j + 1, :] += _colsum(d * rows)
            return carry

        lax.fori_loop(0, seq // tt, step, 0)

    return pl.pallas_call(
        kern, name=name, grid=(D_MODEL // cb,),
        in_specs=[pl.BlockSpec((seq, cb), lambda j: (0, j)), pl.BlockSpec((seq, cb), lambda j: (0, j))],
        out_specs=[pl.BlockSpec((CONV_W, cb), lambda j: (0, j)), pl.BlockSpec((1, cb), lambda j: (0, j))],
        out_shape=[jax.ShapeDtypeStruct((CONV_W, D_MODEL), F32), jax.ShapeDtypeStruct((1, D_MODEL), F32)],
        scratch_shapes=[pltpu.VMEM((seq + 2 * CONV_HALO, cb), F32)],
        compiler_params=_params())(xin, dout)


def _mm(name, a, b, out_sds, grid, a_spec, b_spec, o_spec, contract, k_axis=None, bias=None, pair_send=()):
    dn = (contract, ((), ()))
    ns = len(pair_send)
    nb = 0 if bias is None else 1
    n_steps = math.prod(grid)

    def kern(*refs):
        a_ref, b_ref = refs[:2]
        bias_ref = refs[2] if nb else None
        o_ref = refs[2 + nb + ns]
        if ns:
            copies = _pair_send_copies(refs[2 + nb:2 + nb + ns], refs[3 + nb + ns:3 + nb + 2 * ns],
                                       *refs[3 + nb + 2 * ns:])
            step = pl.program_id(0)
            for ax in range(1, len(grid)):
                step = step * grid[ax] + pl.program_id(ax)

            @pl.when(step == 0)
            def _():
                for cp in copies():
                    cp.start()

        p = lax.dot_general(a_ref[...].astype(BF16), b_ref[...].astype(BF16), dn, preferred_element_type=F32)
        if bias is not None:
            p = p + bias_ref[...]
        if k_axis is None:
            o_ref[...] = p.astype(o_ref.dtype)
        else:
            k = pl.program_id(k_axis)

            @pl.when(k == 0)
            def _():
                o_ref[...] = p

            @pl.when(k > 0)
            def _():
                o_ref[...] += p

        if ns:
            @pl.when(step == n_steps - 1)
            def _():
                for cp in copies():
                    cp.wait()

    in_specs = [a_spec, b_spec]
    args = [a, b]
    if bias is not None:
        in_specs.append(pl.BlockSpec(bias.shape, lambda *_: (0,) * bias.ndim))
        args.append(bias)
    if not ns:
        return pl.pallas_call(kern, name=name, grid=grid, in_specs=in_specs, out_specs=o_spec, out_shape=out_sds,
                              compiler_params=_params())(*args)
    return pl.pallas_call(
        kern, name=name, grid=grid, in_specs=in_specs + [_ANY] * ns, out_specs=[o_spec] + [_ANY] * ns,
        out_shape=[out_sds] + [jax.ShapeDtypeStruct((N_CHIP,) + g.shape[2:], g.dtype) for g in pair_send],
        scratch_shapes=[pltpu.SemaphoreType.DMA((ns, N_CHIP)), pltpu.SemaphoreType.DMA((ns, N_CHIP))],
        compiler_params=_params())(*args, *pair_send)


def _mm_rows(name, a, b, tm, trans_b=False, bias=None):
    m, k = a.shape
    n = b.shape[0] if trans_b else b.shape[1]
    contract = ((1,), (1,)) if trans_b else ((1,), (0,))
    return _mm(name, a, b, jax.ShapeDtypeStruct((m, n), F32), (m // tm,),
               pl.BlockSpec((tm, k), lambda i: (i, 0)), pl.BlockSpec(b.shape, lambda i: (0, 0)),
               pl.BlockSpec((tm, n), lambda i: (i, 0)), contract, bias=bias)


def _mm_glu(name, a, b, bias, tm):
    m, k = a.shape
    n = b.shape[1]

    def kern(a_ref, b_ref, bias_ref, ag_ref, hg_ref):
        v = jnp.dot(a_ref[...], b_ref[...], preferred_element_type=F32)
        ag_ref[...] = v
        v = v + bias_ref[...]
        hg_ref[...] = v[:, :n // 2] * _sigmoid(v[:, n // 2:])

    return pl.pallas_call(
        kern, name=name, grid=(m // tm,),
        in_specs=[pl.BlockSpec((tm, k), lambda i: (i, 0)), pl.BlockSpec(b.shape, lambda i: (0, 0)),
                  pl.BlockSpec((1, n), lambda i: (0, 0))],
        out_specs=[pl.BlockSpec((tm, n), lambda i: (i, 0)), pl.BlockSpec((tm, n // 2), lambda i: (i, 0))],
        out_shape=[jax.ShapeDtypeStruct((m, n), F32), jax.ShapeDtypeStruct((m, n // 2), F32)],
        compiler_params=_params())(a, b, bias)


def _mm_rows_res(name, a, b, h, gt, tm, bias=None, norm=None, final=None):
    m, k = a.shape
    n = b.shape[1]
    nb = 0 if bias is None else 1
    extra = list(norm or ()) + list(final or ())

    def kern(a_ref, b_ref, h_ref, gt_ref, *rest):
        y = jnp.dot(a_ref[...], b_ref[...], preferred_element_type=F32)
        if nb:
            y = y + rest[0][...]
        ex = rest[nb:nb + len(extra)]
        outs = rest[nb + len(extra):]
        hn = h_ref[...] + gt_ref[...] * y
        if final is None:
            outs[0][...] = y
            outs[1][...] = hn
        if norm is not None:
            nn, _ = _rms(hn)
            outs[2][...] = ((nn * ex[0][...]) * (1.0 + ex[1][...]) + ex[2][...]).astype(BF16)
        if final is not None:
            @pl.when(pl.program_id(0) == 0)
            def _():
                for o in (outs[1], outs[2], outs[4]):
                    o[...] = jnp.zeros(o.shape, F32)

            gv = ex[1][...]
            nn, r = _rms(hn)
            e = nn * gv - ex[0][...]
            outs[2][...] += _colsum(e * e) * (0.5 / D_MODEL)
            dout = e * (1.0 / D_MODEL)
            outs[1][...] += _colsum(dout * nn)
            dh = _rms_bwd(nn, r, dout * gv)
            outs[0][...] = dh
            outs[3][...] = (dh * gt_ref[...]).astype(BF16)
            outs[4][...] += _colsum(dh * y)

    row = pl.BlockSpec((tm, n), lambda i: (i, 0))
    vec = pl.BlockSpec((1, n), lambda i: (0, 0))
    sds = jax.ShapeDtypeStruct((m, n), F32)
    vsd = jax.ShapeDtypeStruct((1, n), F32)
    in_specs = [pl.BlockSpec((tm, k), lambda i: (i, 0)), pl.BlockSpec(b.shape, lambda i: (0, 0)), row, vec] + [vec] * nb
    out_specs, out_shape = [row, row], [sds, sds]
    if norm is not None:
        in_specs += [vec] * 3
        out_specs.append(row)
        out_shape.append(jax.ShapeDtypeStruct((m, n), BF16))
    if final is not None:
        in_specs += [row, vec]
        out_specs = [row, vec, vec, row, vec]
        out_shape = [sds, vsd, vsd, jax.ShapeDtypeStruct((m, n), BF16), vsd]
    return pl.pallas_call(kern, name=name, grid=(m // tm,), in_specs=in_specs, out_specs=out_specs,
                          out_shape=out_shape, compiler_params=_params())(
        a, b, h, gt, *([bias] if nb else []), *extra)


def _ffn_in_swiglu(name, xf, wblk, tm, gather=()):
    m, k = xf.shape
    nblk = wblk.shape[2]
    ng = len(gather)
    n_i = m // tm
    last = 2 * n_i - 1

    def kern(x_ref, wg_ref, wu_ref, *rest):
        gu_ref, a_ref = rest[ng:ng + 2]
        if ng:
            start, forward, finish = _gather_phases(rest[:ng], rest[ng + 2:2 * ng + 2], *rest[2 * ng + 2:])
            step = pl.program_id(0) * n_i + pl.program_id(1)
            pl.when(step == 0)(start)
            pl.when(step == last)(forward)
        xv = x_ref[...]
        g = jnp.dot(xv, wg_ref[...], preferred_element_type=F32)
        u = jnp.dot(xv, wu_ref[...], preferred_element_type=F32)
        gu_ref[0] = g.astype(BF16)
        gu_ref[1] = u.astype(BF16)
        a_ref[...] = (g * _sigmoid(g) * u).astype(BF16)
        if ng:
            pl.when(step == last)(finish)

    return pl.pallas_call(
        kern, name=name, grid=(2, n_i),
        in_specs=[pl.BlockSpec((tm, k), lambda j, i: (i, 0)), pl.BlockSpec((None, k, nblk), lambda j, i: (j, 0, 0)),
                  pl.BlockSpec((None, k, nblk), lambda j, i: (j + 2, 0, 0))] + [_ANY] * ng,
        out_specs=[pl.BlockSpec((2, tm, nblk), lambda j, i: (0, i, j)),
                   pl.BlockSpec((tm, nblk), lambda j, i: (i, j))] + [_ANY] * ng,
        out_shape=[jax.ShapeDtypeStruct((2, m, 2 * nblk), BF16), jax.ShapeDtypeStruct((m, 2 * nblk), BF16)]
        + [jax.ShapeDtypeStruct((N_DEV,) + g.shape, g.dtype) for g in gather],
        scratch_shapes=_gather_scratch(ng) if ng else [],
        compiler_params=_params())(xf, wblk, wblk, *gather)


def _ffn_out_dx_swiglu(name, dy, wo, gu, tm):
    m, k = dy.shape
    nblk = gu.shape[2] // 2

    def kern(dy_ref, w_ref, gu_ref, o_ref):
        da = lax.dot_general(dy_ref[...], w_ref[...], (((1,), (1,)), ((), ())), preferred_element_type=F32)
        g = gu_ref[0].astype(F32)
        sg = _sigmoid(g)
        o_ref[0] = (da * gu_ref[1].astype(F32) * (sg * (1.0 + g * (1.0 - sg)))).astype(BF16)
        o_ref[1] = (da * (g * sg)).astype(BF16)

    return pl.pallas_call(
        kern, name=name, grid=(2, m // tm),
        in_specs=[pl.BlockSpec((tm, k), lambda j, i: (i, 0)), pl.BlockSpec((nblk, k), lambda j, i: (j, 0)),
                  pl.BlockSpec((2, tm, nblk), lambda j, i: (0, i, j))],
        out_specs=pl.BlockSpec((2, tm, nblk), lambda j, i: (0, i, j)),
        out_shape=jax.ShapeDtypeStruct(gu.shape, BF16), compiler_params=_params())(dy, wo, gu)


def _ffn_in_dx_norm_bwd(name, dgu, wblk, h, dh_in, g, sc, tm, gate=None):
    nb, k, nblk = wblk.shape

    def dx_fn(a_ref, w_ref):
        acc = None
        for j in range(nb):
            p = lax.dot_general(a_ref[j // 2, :, (j % 2) * nblk:(j % 2 + 1) * nblk], w_ref[j],
                                (((1,), (1,)), ((), ())), preferred_element_type=F32)
            acc = p if acc is None else acc + p
        return acc

    return _mm_row_epilogue(name, [dgu, wblk],
                            [pl.BlockSpec((2, tm, 2 * nblk), lambda i: (0, i, 0)),
                             pl.BlockSpec(wblk.shape, lambda i: (0, 0, 0))], dx_fn, [(h, None), (dh_in, None)],
                            [g, sc], 3, _norm_bwd_epilogue, tm, gate=gate)


def _mm_t_epilogue(name, a, b, rows_in, vecs_in, n_vec_out, epilogue, tm, gate=None, exchange=()):
    def dx_fn(a_ref, b_ref):
        return lax.dot_general(a_ref[...], b_ref[...], (((1,), (1,)), ((), ())), preferred_element_type=F32)

    return _mm_row_epilogue(name, [a, b], [pl.BlockSpec((tm, a.shape[1]), lambda i: (i, 0)),
                                          pl.BlockSpec(b.shape, lambda i: (0, 0))], dx_fn, rows_in, vecs_in,
                            n_vec_out, epilogue, tm, gate, exchange)


def _mm_halves_wgrad_blk(name, a, b2, tk, pair_send=()):
    t, k = a.shape
    nblk = b2.shape[2] // 2
    return _mm(name, a, b2, jax.ShapeDtypeStruct((N_CHIP, k, nblk), F32), (N_CHIP, t // tk),
               pl.BlockSpec((tk, k), lambda j, i: (i, 0)),
               pl.BlockSpec((None, tk, nblk), lambda j, i: (j // 2, i, j % 2)),
               pl.BlockSpec((None, k, nblk), lambda j, i: (j, 0, 0)), ((0,), (0,)), k_axis=1, pair_send=pair_send)


def _mm_wgrad(name, a, b, tk):
    t, k = a.shape
    n = b.shape[1]
    tn = n if k * n * 4 <= 4 * 1024 * 1024 else 512
    return _mm(name, a, b, jax.ShapeDtypeStruct((k, n), F32), (n // tn, t // tk),
               pl.BlockSpec((tk, k), lambda j, i: (i, 0)), pl.BlockSpec((tk, tn), lambda j, i: (i, j)),
               pl.BlockSpec((k, tn), lambda j, i: (0, j)), ((0,), (0,)), k_axis=1)


def _attn_fwd(name, q_all, k, v, n_ctx, tq, gather=()):
    r = k.shape[1]
    s = q_all.shape[0] - n_ctx
    gw = Q_GROUP * HEAD_DIM
    ng = len(gather)
    n_i = s // tq
    last = N_KV_HEADS * n_i - 1
    forward_at = (7 * last) // 8

    def kern(q_ref, k_ref, v_ref, *rest):
        o_ref, lse_ref = rest[ng], rest[ng + 1]
        if ng:
            start, forward, finish = _gather_phases(rest[:ng], rest[ng + 2:2 * ng + 2], *rest[2 * ng + 2:])
            step = pl.program_id(0) * n_i + pl.program_id(1)
            pl.when(step == 0)(start)
            pl.when(step == forward_at)(forward)
        kv = k_ref[...]
        vv = v_ref[...]
        for g in range(Q_GROUP):
            sl = slice(g * HEAD_DIM, (g + 1) * HEAD_DIM)
            sc = lax.dot_general(q_ref[:, sl], kv, (((1,), (1,)), ((), ())), preferred_element_type=F32)
            m = jnp.max(sc, axis=-1, keepdims=True)
            p = jnp.exp((sc - m).astype(BF16))
            ol = jnp.dot(p, vv, preferred_element_type=F32)
            l = ol[:, HEAD_DIM:HEAD_DIM + 1]
            o_ref[:, sl] = ol[:, :HEAD_DIM] / l
            lse_ref[g] = m + jnp.log(l)
        if ng:
            pl.when(step == last)(finish)

    return pl.pallas_call(
        kern, name=name, grid=(N_KV_HEADS, n_i),
        in_specs=[pl.BlockSpec((tq, gw), lambda j, i: (i + n_ctx // tq, j)),
                  pl.BlockSpec((None, r, HEAD_DIM), lambda j, i: (j, 0, 0)),
                  pl.BlockSpec((None, r, 2 * HEAD_DIM), lambda j, i: (j, 0, 0))] + [_ANY] * ng,
        out_specs=[pl.BlockSpec((tq, gw), lambda j, i: (i, j)),
                   pl.BlockSpec((Q_GROUP, tq, 1), lambda j, i: (j, i, 0))] + [_ANY] * ng,
        out_shape=[jax.ShapeDtypeStruct((s, ATTN_W), F32), jax.ShapeDtypeStruct((N_Q_HEADS, s, 1), F32)]
        + [jax.ShapeDtypeStruct((N_DEV,) + a.shape, a.dtype) for a in gather],
        scratch_shapes=_gather_scratch(ng) if ng else [],
        compiler_params=_params())(q_all, k, v, *gather)


def _attn_bwd(name, q_all, k, v, o, lse, dcat, n_ctx, tq, exchange=()):
    nkv, r, _ = k.shape
    s = o.shape[0]
    gw = Q_GROUP * HEAD_DIM
    ne = len(exchange)
    n_i = s // tq
    last = nkv * n_i - 1

    def kern(q_ref, k_ref, v_ref, o_ref, lse_ref, do_ref, *rest):
        dq_ref, dk_ref, dv_ref = rest[ne:ne + 3]
        ds_all, p_all = rest[2 * ne + 3:2 * ne + 5]
        if ne:
            start, finish = _exchange_phases(rest[:ne], rest[ne + 3:2 * ne + 3], *rest[2 * ne + 5:])
            step = pl.program_id(0) * n_i + pl.program_id(1)
            pl.when(step == 0)(start)

        @pl.when(pl.program_id(1) == 0)
        def _():
            dk_ref[...] = jnp.zeros(dk_ref.shape, F32)
            dv_ref[...] = jnp.zeros(dv_ref.shape, F32)

        kv = k_ref[...]
        vv = v_ref[...]
        qs = []
        dos = []
        for g in range(Q_GROUP):
            sl = slice(g * HEAD_DIM, (g + 1) * HEAD_DIM)
            rows = slice(g * tq, (g + 1) * tq)
            qv = q_ref[:, sl]
            dov = do_ref[:, sl]
            delta = jnp.sum(dov * o_ref[:, sl], axis=-1, keepdims=True)
            sc = lax.dot_general(qv, kv, (((1,), (1,)), ((), ())), preferred_element_type=F32)
            p = jnp.exp(sc - lse_ref[g])
            dob = dov.astype(BF16)
            dp = lax.dot_general(dob, vv, (((1,), (1,)), ((), ())), preferred_element_type=F32)
            dsb = (p * (dp - delta)).astype(BF16)
            dq_ref[:, sl] = jnp.dot(dsb, kv, preferred_element_type=F32)
            ds_all[rows, :] = dsb
            p_all[rows, :] = p.astype(BF16)
            qs.append(qv)
            dos.append(dob)
        dk_ref[...] += lax.dot_general(jnp.concatenate(qs, axis=0), ds_all[...], (((0,), (0,)), ((), ())),
                                       preferred_element_type=F32)
        dv_ref[...] += lax.dot_general(jnp.concatenate(dos, axis=0), p_all[...], (((0,), (0,)), ((), ())),
                                       preferred_element_type=F32)
        if ne:
            pl.when(step == last)(finish)

    kvmap = lambda j, i: (j, 0, 0)
    return pl.pallas_call(
        kern, name=name, grid=(nkv, n_i),
        in_specs=[pl.BlockSpec((tq, gw), lambda j, i: (i + n_ctx // tq, j)), pl.BlockSpec((None, r, HEAD_DIM), kvmap),
                  pl.BlockSpec((None, r, HEAD_DIM), kvmap), pl.BlockSpec((tq, gw), lambda j, i: (i, j)),
                  pl.BlockSpec((Q_GROUP, tq, 1), lambda j, i: (j, i, 0)),
                  pl.BlockSpec((tq, gw), lambda j, i: (i, j))] + [_ANY] * ne,
        out_specs=[pl.BlockSpec((tq, gw), lambda j, i: (i, j)), pl.BlockSpec((None, HEAD_DIM, r), kvmap),
                   pl.BlockSpec((None, HEAD_DIM, r), kvmap)] + [_ANY] * ne,
        out_shape=[jax.ShapeDtypeStruct((s, ATTN_W), F32), jax.ShapeDtypeStruct((nkv, HEAD_DIM, r), F32),
                   jax.ShapeDtypeStruct((nkv, HEAD_DIM, r), F32)]
        + [jax.ShapeDtypeStruct(a.shape, a.dtype) for a in exchange],
        scratch_shapes=[pltpu.VMEM((Q_GROUP * tq, r), BF16)] * 2 + (_exchange_scratch(ne) if ne else []),
        compiler_params=_params())(q_all, k, v, o, lse, dcat, *exchange)


def _mod_fwd(name, c16, wm, bm):
    n = wm.shape[2]

    def kern(c_ref, w_ref, b_ref, o_ref):
        cv = c_ref[...]
        a = (cv * _sigmoid(cv)).astype(BF16)
        o_ref[...] = jnp.dot(a, w_ref[...].astype(BF16), preferred_element_type=F32) + b_ref[...]

    return pl.pallas_call(
        kern, name=name, grid=(2,),
        in_specs=[pl.BlockSpec(c16.shape, lambda l: (0, 0)), pl.BlockSpec((None, D_MODEL, n), lambda l: (l, 0, 0)),
                  pl.BlockSpec((None, 1, n), lambda l: (l, 0, 0))],
        out_specs=pl.BlockSpec((None, 16, n), lambda l: (l, 0, 0)),
        out_shape=jax.ShapeDtypeStruct((2, 16, n), F32), compiler_params=_params())(c16, wm, bm)


def _mod_wgrad(name, c16, dm):
    n = dm.shape[2]
    tn = 512

    def kern(c_ref, d_ref, o_ref):
        cv = c_ref[...]
        a = cv * _sigmoid(cv)
        o_ref[...] = lax.dot_general(a, d_ref[...], (((0,), (0,)), ((), ())), preferred_element_type=F32,
                                     precision=lax.Precision.HIGHEST)

    return pl.pallas_call(
        kern, name=name, grid=(2, n // tn),
        in_specs=[pl.BlockSpec(c16.shape, lambda l, j: (0, 0)), pl.BlockSpec((None, 16, tn), lambda l, j: (l, 0, j))],
        out_specs=pl.BlockSpec((None, D_MODEL, tn), lambda l, j: (l, 0, j)),
        out_shape=jax.ShapeDtypeStruct((2, D_MODEL, n), F32), compiler_params=_params())(c16, dm)


def _cctx_grad(name, parts, c_ctx):
    def kern(p_ref, c_ref, o_ref):
        d = p_ref[0, 0:1, :] + p_ref[2, 0:1, :] + p_ref[4, 0:1, :] + p_ref[6, 0:1, :]
        cv = c_ref[...]
        sg = _sigmoid(cv)
        o_ref[...] = d * (sg * (1.0 + cv * (1.0 - sg)))

    return pl.pallas_call(kern, name=name, out_shape=jax.ShapeDtypeStruct((1, D_MODEL), F32),
                          compiler_params=_params())(parts, c_ctx)


def _sum_slots(name, g, tr):
    n, rows, cols = g.shape

    def kern(g_ref, o_ref):
        acc = g_ref[0].astype(F32)
        for i in range(1, n):
            acc = acc + g_ref[i].astype(F32)
        o_ref[...] = acc

    return pl.pallas_call(kern, name=name, grid=(rows // tr,),
                          in_specs=[pl.BlockSpec((n, tr, cols), lambda i: (0, i, 0))],
                          out_specs=pl.BlockSpec((tr, cols), lambda i: (i, 0)),
                          out_shape=jax.ShapeDtypeStruct((rows, cols), F32), compiler_params=_params())(g)


def _sum_into(name, g, cvec, n_layers, layer, prev=None):
    n, r, cc = g.shape
    tr = _row_tile(r, cc, 512 * 1024)

    def kern(c_ref, g_ref, *rest):
        acc = g_ref[0].astype(F32)
        for i in range(1, n):
            acc = acc + g_ref[i].astype(F32)
        rest[-1][...] = acc

    in_specs = [pl.BlockSpec((n, tr, cc), lambda i, c_ref: (0, i, 0))]
    args = [cvec, g]
    aliases = {}
    if prev is not None:
        in_specs.append(_ANY)
        args.append(prev)
        aliases = {2: 0}
    grid_spec = pltpu.PrefetchScalarGridSpec(
        num_scalar_prefetch=1, grid=(r // tr,), in_specs=in_specs,
        out_specs=pl.BlockSpec((None, None, tr, cc), lambda i, c_ref: (layer, c_ref[0], i, 0)))
    return pl.pallas_call(kern, name=name, grid_spec=grid_spec,
                          out_shape=jax.ShapeDtypeStruct((n_layers, 2, r, cc), F32),
                          input_output_aliases=aliases, compiler_params=_params())(*args)


def _row_tile(rows, cols, max_bytes=1024 * 1024):
    if rows * cols * 4 <= 2 * max_bytes:
        return rows
    best = None
    for t in range(16, rows + 1, 16):
        if rows % t == 0 and t * cols * 4 <= max_bytes:
            best = t
    assert best is not None, (rows, cols)
    return best


def _adamw(name, w, g, m, v):
    shape = w.shape
    cols = shape[-1]
    rows = w.size // cols
    tr = _row_tile(rows, cols)

    def kern(w_ref, g_ref, m_ref, v_ref, d_ref, nm_ref, nv_ref):
        gv = g_ref[...]
        m2 = ADAM_B1 * m_ref[...] + (1.0 - ADAM_B1) * gv
        v2 = ADAM_B2 * v_ref[...] + (1.0 - ADAM_B2) * (gv * gv)
        m_hat = m2 / (1.0 - ADAM_B1 ** ADAM_STEP)
        v_hat = v2 / (1.0 - ADAM_B2 ** ADAM_STEP)
        d_ref[...] = -ADAM_LR * (m_hat / (jnp.sqrt(v_hat) + ADAM_EPS) + ADAM_WD * w_ref[...])
        nm_ref[...] = m2
        nv_ref[...] = v2

    spec = pl.BlockSpec((tr, cols), lambda i: (i, 0))
    sds = jax.ShapeDtypeStruct((rows, cols), F32)
    outs = pl.pallas_call(kern, name=name, grid=(rows // tr,), in_specs=[spec] * 4, out_specs=[spec] * 3,
                          out_shape=[sds] * 3, compiler_params=_params())(
        w.reshape(rows, cols), g.reshape(rows, cols), m.reshape(rows, cols), v.reshape(rows, cols))
    return tuple(o.reshape(shape) for o in outs)


_ANY = pl.BlockSpec(memory_space=pl.ANY)


def _mesh_pos():
    return lax.axis_index("x"), lax.axis_index("y"), lax.axis_index("c")


def _gather_phases(srcs, outs, send_sems, recv_sems, local_sems):
    n = len(srcs)
    x, y, c = _mesh_pos()
    me = (x, y, c)
    sibling = (x, y, 1 - c)
    chips = [(1 - x, y), (x, 1 - y), (1 - x, 1 - y)]

    def slot(px, py, pc):
        return 4 * px + 2 * py + pc

    def copy(t, k, s, to, src=None):
        dst = outs[t].at[s]
        return pltpu.make_async_remote_copy(src_ref=dst if src is None else src, dst_ref=dst,
                                            send_sem=send_sems.at[t, k], recv_sem=recv_sems.at[t, k],
                                            device_id=to, device_id_type=MESH)

    def mine(t):
        return pltpu.make_async_copy(srcs[t], outs[t].at[slot(*me)], local_sems.at[t])

    def first(t):
        return [copy(t, 0, slot(*me), sibling, src=srcs[t])] + [
            copy(t, 1 + j, slot(*me), (px, py, c), src=srcs[t]) for j, (px, py) in enumerate(chips)]

    def passed(t, j):
        px, py = chips[j]
        return copy(t, 4 + j, slot(px, py, c), sibling)

    def start():
        for t in range(n):
            mine(t).start()
        for t in range(n):
            for cp in first(t):
                cp.start()

    def forward():
        for j, (px, py) in enumerate(chips):
            for t in range(n):
                copy(t, 1 + j, slot(px, py, c), me).wait_recv()
                passed(t, j).start()

    def finish():
        for t in range(n):
            copy(t, 0, slot(x, y, 1 - c), me).wait_recv()
            for j, (px, py) in enumerate(chips):
                copy(t, 4 + j, slot(px, py, 1 - c), me).wait_recv()
        for t in range(n):
            for cp in first(t) + [passed(t, j) for j in range(len(chips))]:
                cp.wait_send()
            mine(t).wait()

    return start, forward, finish


def _gather_scratch(n):
    return [pltpu.SemaphoreType.DMA((n, 7)), pltpu.SemaphoreType.DMA((n, 7)), pltpu.SemaphoreType.DMA((n,))]


def _allgather(name, items):
    n = len(items)

    def body(*refs):
        start, forward, finish = _gather_phases(refs[:n], refs[n:2 * n], *refs[2 * n:])
        start()
        forward()
        finish()

    return pl.pallas_call(
        body, name=name, in_specs=[_ANY] * n, out_specs=[_ANY] * n,
        out_shape=[jax.ShapeDtypeStruct((N_DEV,) + a.shape, a.dtype) for a in items],
        scratch_shapes=_gather_scratch(n), compiler_params=_params())(*items)


def _pair_send_copies(srcs, bufs, send_sems, recv_sems):
    x, y, c = _mesh_pos()

    def copies():
        return [pltpu.make_async_remote_copy(
            src_ref=srcs[t].at[k, 1 - c], dst_ref=bufs[t].at[k], send_sem=send_sems.at[t, k],
            recv_sem=recv_sems.at[t, k], device_id=(x, y, 1 - c), device_id_type=MESH)
            for t in range(len(srcs)) for k in range(N_CHIP)]

    return copies


def _rs_pair_send(name, gs):
    n = len(gs)

    def body(*refs):
        copies = _pair_send_copies(refs[:n], refs[n:2 * n], *refs[2 * n:])
        for cp in copies():
            cp.start()
        for cp in copies():
            cp.wait()

    return pl.pallas_call(
        body, name=name, in_specs=[_ANY] * n, out_specs=[_ANY] * n,
        out_shape=[jax.ShapeDtypeStruct((N_CHIP,) + g.shape[2:], g.dtype) for g in gs],
        scratch_shapes=[pltpu.SemaphoreType.DMA((n, N_CHIP)), pltpu.SemaphoreType.DMA((n, N_CHIP))],
        compiler_params=_params())(*gs)


def _rs_pair_add(name, g, buf, cvec):
    _, _, r, cc = g.shape
    tr = _row_tile(r, cc, 2 * 1024 * 1024)

    def kern(c_ref, g_ref, b_ref, o_ref):
        o_ref[...] = (g_ref[...] + b_ref[...]).astype(BF16)

    grid_spec = pltpu.PrefetchScalarGridSpec(
        num_scalar_prefetch=1, grid=(N_CHIP, r // tr),
        in_specs=[pl.BlockSpec((None, None, tr, cc), lambda k, i, c_ref: (k, c_ref[0], i, 0)),
                  pl.BlockSpec((None, tr, cc), lambda k, i, c_ref: (k, i, 0))],
        out_specs=pl.BlockSpec((None, tr, cc), lambda k, i, c_ref: (k, i, 0)))
    return pl.pallas_call(kern, name=name, grid_spec=grid_spec,
                          out_shape=jax.ShapeDtypeStruct((N_CHIP, r, cc), BF16),
                          compiler_params=_params())(cvec, g, buf)


def _exchange_phases(srcs, bufs, send_sems, recv_sems, local_sems):
    n = len(srcs)
    x, y, c = _mesh_pos()
    kme = 2 * x + y
    chips = [(1 - x, y), (x, 1 - y), (1 - x, 1 - y)]

    def copies():
        local = [pltpu.make_async_copy(srcs[t].at[kme], bufs[t].at[kme], local_sems.at[t]) for t in range(n)]
        remote = []
        for t in range(n):
            for j, (px, py) in enumerate(chips):
                remote.append(pltpu.make_async_remote_copy(
                    src_ref=srcs[t].at[2 * px + py], dst_ref=bufs[t].at[kme], send_sem=send_sems.at[t, j],
                    recv_sem=recv_sems.at[t, j], device_id=(px, py, c), device_id_type=MESH))
        return local, remote

    def start():
        local, remote = copies()
        for cp in local + remote:
            cp.start()

    def finish():
        local, remote = copies()
        for cp in remote + local:
            cp.wait()

    return start, finish


def _exchange_scratch(n):
    return [pltpu.SemaphoreType.DMA((n, 3)), pltpu.SemaphoreType.DMA((n, 3)), pltpu.SemaphoreType.DMA((n,))]


def _rs_chip_exchange(name, ss):
    n = len(ss)

    def body(*refs):
        start, finish = _exchange_phases(refs[:n], refs[n:2 * n], *refs[2 * n:])
        start()
        finish()

    return pl.pallas_call(
        body, name=name, in_specs=[_ANY] * n, out_specs=[_ANY] * n,
        out_shape=[jax.ShapeDtypeStruct(s.shape, s.dtype) for s in ss],
        scratch_shapes=_exchange_scratch(n), compiler_params=_params())(*ss)


def _rs_pair_share(name, dsts, gather=()):
    nd = len(dsts)
    ng = len(gather)

    def body(*refs):
        ins = refs[:nd]
        outs = refs[nd + ng:2 * nd + ng]
        send_sems, recv_sems = refs[2 * (nd + ng):2 * (nd + ng) + 2]
        if ng:
            start, forward, finish = _gather_phases(refs[nd:nd + ng], refs[2 * nd + ng:2 * (nd + ng)],
                                                    *refs[2 * (nd + ng) + 2:])
            start()
        x, y, c = _mesh_pos()
        copies = []
        for d in range(nd):
            for l in range(dsts[d].shape[0]):
                copies.append(pltpu.make_async_remote_copy(
                    src_ref=ins[d].at[l, c], dst_ref=outs[d].at[l, c], send_sem=send_sems.at[d, l],
                    recv_sem=recv_sems.at[d, l], device_id=(x, y, 1 - c), device_id_type=MESH))
        for cp in copies:
            cp.start()
        if ng:
            forward()
        for cp in copies:
            cp.wait()
        if ng:
            finish()

    return pl.pallas_call(
        body, name=name, in_specs=[_ANY] * (nd + ng), out_specs=[_ANY] * (nd + ng),
        out_shape=[jax.ShapeDtypeStruct(a.shape, a.dtype) for a in dsts]
        + [jax.ShapeDtypeStruct((N_DEV,) + a.shape, a.dtype) for a in gather],
        input_output_aliases={d: d for d in range(nd)},
        scratch_shapes=[pltpu.SemaphoreType.DMA((nd, 2)), pltpu.SemaphoreType.DMA((nd, 2))]
        + (_gather_scratch(ng) if ng else []),
        compiler_params=_params())(*dsts, *gather)


def _rope_tables(seq, n_ctx):
    t = jnp.arange(seq)
    row = (t // GRID_W).astype(F32)
    col = (t % GRID_W).astype(F32)
    inv = ROPE_THETA ** (-jnp.arange(0, HEAD_DIM // 2, 2, dtype=F32) / (HEAD_DIM // 2))
    ang_r = row[:, None] * inv[None, :]
    ang_c = col[:, None] * inv[None, :]
    cos = jnp.concatenate([jnp.cos(ang_r)] * 2 + [jnp.cos(ang_c)] * 2, axis=1)
    sin = jnp.concatenate([-jnp.sin(ang_r), jnp.sin(ang_r), -jnp.sin(ang_c), jnp.sin(ang_c)], axis=1)
    cos = jnp.concatenate([jnp.ones((n_ctx, HEAD_DIM), F32), cos], axis=0)
    sin = jnp.concatenate([jnp.zeros((n_ctx, HEAD_DIM), F32), sin], axis=0)
    return jnp.tile(cos, (1, N_KV_HEADS)), jnp.tile(sin, (1, N_KV_HEADS))


def _to_heads(a, nh):
    return a.reshape(a.shape[0], nh, HEAD_DIM).transpose(1, 0, 2)


def _sample_step(x, ctx, tgt, mod, cmod, W, pending, cvec):
    W = dict(W)
    seq = x.shape[0]
    n_ctx = ctx.shape[0]
    tm = min(256, seq)
    tmc = min(256, n_ctx)
    tmm = 512 if seq % 512 == 0 else tm
    tmw = 1024 if seq % 1024 == 0 else tmm
    tmr = (n_ctx + seq) // 4 if (n_ctx + seq) % 64 == 0 else tm
    tmk = 2048 if seq % 2048 == 0 else tmw
    tq = min(256, seq)
    assert n_ctx % tm == 0 and seq % tm == 0

    def mv(l, j):
        return mod[l, j * D_MODEL:(j + 1) * D_MODEL].reshape(1, D_MODEL)

    csh1 = cmod[:D_MODEL].reshape(1, D_MODEL)
    csc1 = cmod[D_MODEL:2 * D_MODEL].reshape(1, D_MODEL)
    g_mix = [W['g_mix'][l:l + 1] for l in range(2)]
    g_ffn = [W['g_ffn'][l:l + 1] for l in range(2)]
    cos, sin = _rope_tables(seq, n_ctx)
    gq = jnp.tile(W['q_gain'], (1, N_Q_HEADS))
    gk = jnp.tile(W['k_gain'], (1, N_KV_HEADS))
    bdq = jnp.kron(jnp.eye(N_Q_HEADS, dtype=F32), jnp.ones((HEAD_DIM, HEAD_DIM), F32)).astype(BF16)
    bdk = bdq[:KV_W, :KV_W]
    w_sp = W['w_sp'].astype(BF16)
    w_sp_t = w_sp.transpose(0, 2, 1)
    bsp = jnp.broadcast_to(W['b_sp'].T[:, :, None], (CHUNK, N_SG, CHUNK)).reshape(CHUNK, SG_W)

    def ffn_fwd(l, h_mid, xf, gather=(), **tail):
        gu, a, *gathered = _ffn_in_swiglu(f'l{l}_ffn_in', xf, W['ffn_in'][l], tmw, gather=gather)
        if gathered:
            W['pw2'] = gathered[0].reshape(D_MODEL, D_MODEL)
            W['ffn_out'][1] = gathered[1].reshape(D_FF, D_MODEL)
            W['pw1'] = gathered[2].reshape(N_CHIP, D_MODEL, PW1_BLK).transpose(1, 0, 2).reshape(D_MODEL, 2 * D_MODEL)
        outs = _mm_rows_res(f'l{l}_ffn_out', a, W['ffn_out'][l], h_mid, mv(l, 5), tmm, **tail)
        return outs, (h_mid, xf, gu, a)

    def blocked(a):
        cols = a.shape[-1]
        return a.reshape(N_CHIP, 2, a.size // (2 * N_CHIP * cols), cols)

    def ffn_bwd(l, dh_out, dy, y_mix, saved, pair_send=()):
        h_mid, xf, gu, a = saved
        dgu = _ffn_out_dx_swiglu(f'l{l}_ffn_out_dx', dy, W['ffn_out'][l], gu, tmw)
        d_wo = _mm_wgrad(f'l{l}_ffn_out_dw', a, dy, tmw)
        d_wi = _mm_halves_wgrad_blk(f'l{l}_ffn_in_dw', xf, dgu, tmk, pair_send=pair_send)
        if pair_send:
            d_wi, *sent = d_wi
        else:
            sent = []
        dh_mid, dy_mix, d_sh2, d_sc2, d_g, d_gt1, sum_dy = _ffn_in_dx_norm_bwd(
            f'l{l}_ffn_in_dx', dgu, W['ffn_in'][l], h_mid, dh_out, g_ffn[l], mv(l, 4), tmm, gate=(y_mix, mv(l, 2)))
        return dh_mid, dy_mix, d_wi, d_wo, d_g, (d_sh2, d_sc2), d_gt1, sum_dy, sent

    xm0 = _modnorm_fwd('l0_mix_norm', x, g_mix[0], mv(0, 1), mv(0, 0), tm)
    xc0 = _modnorm_fwd('l0_ctx_norm', ctx, g_mix[0], csc1, csh1, tmc)
    xall = jnp.concatenate([xc0, xm0], axis=0)
    p_qkv, q_all, k_all, v_all = _mm_qk_prep('l0_qkv', xall, W['w_qkv'], cos, sin, gq, gk, bdq, bdk, tmr)
    p_sg, sg = _mm_sg('l0_sg_in', xm0, W['w_sgp'], w_sp, bsp, tmm)
    kh = _to_heads(k_all, N_KV_HEADS)
    vh = _to_heads(v_all, N_KV_HEADS)
    v_ones = jnp.concatenate([vh, jnp.ones(vh.shape[:2] + (1,), BF16),
                              jnp.zeros(vh.shape[:2] + (HEAD_DIM - 1,), BF16)], axis=2)
    o, lse, *gw = _attn_fwd('l0_attn', q_all, kh, v_ones, n_ctx, tq, gather=pending[:4])
    W['w_out'] = gw[0].reshape(D_MODEL, D_MODEL)
    W['ffn_in'] = [gw[1].reshape(N_CHIP, D_MODEL, FF_BLK), gw[3].reshape(N_CHIP, D_MODEL, FF_BLK)]
    W['ffn_out'] = [gw[2].reshape(D_FF, D_MODEL), None]
    cat = jnp.concatenate([o.astype(BF16), sg], axis=1)
    y0, h1, xf0 = _mm_rows_res('l0_out', cat, W['w_out'], x, mv(0, 2), tmw, norm=(g_ffn[0], mv(0, 4), mv(0, 3)))
    (f0, h2, xm1), ffn0 = ffn_fwd(0, h1, xf0, gather=pending[4:], norm=(g_mix[1], mv(1, 1), mv(1, 0)))

    ag, hg = _mm_glu('l1_pw1', xm1, W['pw1'], W['b_pw1'], tmm)
    hc = _dwconv('l1_conv', hg, W['w_dw'], W['b_dw'])
    s1 = _lnsilu_fwd('l1_ln_silu', hc, W['ln_g'], W['ln_b'], tm)
    y1, h3, xf1 = _mm_rows_res('l1_pw2', s1, W['pw2'], h2, mv(1, 2), tmw, bias=W['b_pw2'],
                               norm=(g_ffn[1], mv(1, 4), mv(1, 3)))
    (dh4, d_g_final, loss_vec, dyf1, d_gt2_1), ffn1 = ffn_fwd(1, h3, xf1, final=(tgt, W['g_final']))

    dh3, dy1, d_wi1, d_wo1, d_gffn1, dmod1_ffn, d_gt1_1, d_b_pw2, _ = ffn_bwd(1, dh4, dyf1, y1, ffn1)
    d_pw2 = _mm_wgrad('l1_pw2_dw', s1, dy1, tmw)
    dhc, d_ln_g, d_ln_b = _mm_t_epilogue('l1_pw2_dx', dy1, W['pw2'], [(hc, None)], [W['ln_g'], W['ln_b']], 2,
                                         _lnsilu_bwd_epilogue, tmm)
    dhg = _dwconv('l1_conv_dx', dhc, W['w_dw'][::-1], jnp.zeros((1, D_MODEL), F32))
    d_w_dw, d_b_dw = _dwconv_wgrad('l1_conv_dw', hg, dhc)
    dag, d_b_pw1 = _glu_bwd('l1_glu_bwd', ag, dhg, W['b_pw1'], tm)
    d_pw1 = _mm_wgrad('l1_pw1_dw', xm1, dag, tmw).reshape(D_MODEL, N_CHIP, PW1_BLK).transpose(1, 0, 2)
    dh2, dyf0, d_sh1_1, d_sc1_1, d_gmix1, d_gt2_0, _ = _mm_t_epilogue(
        'l1_pw1_dx', dag, W['pw1'], [(h2, None), (dh3, None)], [g_mix[1], mv(1, 1)], 3, _norm_bwd_epilogue, tmm,
        gate=(f0, mv(0, 5)))

    gs1 = [blocked(g) for g in (d_wi1, d_wo1, d_pw1, d_pw2)]
    dh1, dy0, d_wi0, d_wo0, d_gffn0, dmod0_ffn, d_gt1_0, _, sib1 = ffn_bwd(0, dh2, dyf0, y0, ffn0, pair_send=gs1)
    dcat = _mm_rows('l0_out_dx', dy0, W['w_out'], tmw, trans_b=True)
    d_w_out = _mm_wgrad('l0_out_dw', cat, dy0, tmw)

    gs0 = [blocked(g) for g in (d_wi0, d_wo0, d_w_out)]
    sib0 = _rs_pair_send('rs_pair_send', gs0)
    gs = [gs0[0], gs1[0], gs0[1], gs1[1], gs0[2], gs1[2], gs1[3]]
    sib = [sib0[0], sib1[0], sib0[1], sib1[1], sib0[2], sib1[2], sib1[3]]
    ss = [_rs_pair_add(f'rs_pair_add{t}', gs[t], sib[t], cvec) for t in range(len(gs))]
    dq, dk, dv, *xs = _attn_bwd('l0_attn_bwd', q_all, kh, vh, o, lse, dcat, n_ctx, tq, exchange=ss)
    dk, dv = [t.transpose(2, 0, 1).reshape(n_ctx + seq, KV_W) for t in (dk, dv)]
    dp_qkv, d_gq, d_gk = _qk_prep_bwd('l0_qk_prep_bwd', p_qkv, cos, sin, dq, n_ctx // tm, dk, dv, gq, gk, bdq, bdk,
                                      tm)
    dp_sg, d_w_sp, d_bsp = _sg_bwd('l0_sg_bwd', p_sg, dcat, w_sp, w_sp_t, bsp, tm)
    d_w_qkv = _mm_wgrad('l0_qkv_dw', xall, dp_qkv, tmr)
    d_w_sgp = _mm_wgrad('l0_sg_in_dw', xm0, dp_sg, tmw)
    dxall = _mm_rows('l0_qkv_dx', dp_qkv, W['w_qkv'], tmr, trans_b=True)
    g_in = blocked(jnp.concatenate([d_w_qkv, d_w_sgp], axis=1).reshape(D_MODEL, N_CHIP, -1).transpose(1, 0, 2))
    sib_in = _rs_pair_send('rs_pair_send_w_in', [g_in])
    ss_in = _rs_pair_add('rs_pair_add_w_in', g_in, sib_in[0], cvec)
    dx, d_sh1_0, d_sc1_0, d_gmix0, *xs_in = _mm_t_epilogue(
        'l0_sg_in_dx', dp_sg, W['w_sgp'], [(x, None), (dh1, None), (dxall, lambda i: i + n_ctx // tm)],
        [g_mix[0], mv(0, 1)], 3, _norm_bwd_epilogue, tm, exchange=[ss_in])
    _, d_csh1, d_csc1, d_gmix0c = _modnorm_bwd('l0_ctx_norm_bwd', ctx, [_ri(dxall)], None, g_mix[0], csc1, tmc)

    zero = jnp.zeros((1, D_MODEL), F32)
    d = D_MODEL
    pack = jnp.concatenate([
        d_gmix0 + d_gmix0c, d_gmix1, d_gffn0, d_gffn1, d_g_final,
        jnp.concatenate([d_gq, d_gk, jnp.zeros((1, d - ATTN_W - KV_W), F32)], axis=1),
        d_b_pw1.reshape(2, d), d_w_dw, d_b_dw, d_ln_g, d_ln_b, d_b_pw2,
        d_sh1_0, d_sc1_0, d_gt1_0, *dmod0_ffn, d_gt2_0, d_sh1_1, d_sc1_1, d_gt1_1, *dmod1_ffn, d_gt2_1,
        d_csh1, d_csc1, zero, zero, zero, zero, jnp.zeros((3, d), F32),
        d_w_sp.reshape(64, d),
        jnp.pad(d_bsp.reshape(CHUNK, N_SG, CHUNK).sum(axis=-1).reshape(1, CHUNK * N_SG), ((0, 7), (0, d - SG_W)))],
        axis=0)
    assert pack.shape == (SMALL_ROWS, d)

    t_ffn_in = _sum_into('rs_sum_ffn_in1', xs[1], cvec, 2, 1, prev=_sum_into('rs_sum_ffn_in0', xs[0], cvec, 2, 0))
    t_ffn_out = _sum_into('rs_sum_ffn_out1', xs[3], cvec, 2, 1, prev=_sum_into('rs_sum_ffn_out0', xs[2], cvec, 2, 0))
    *reduced, small = _rs_pair_share('rs_pair_share', [
        t_ffn_in, t_ffn_out, _sum_into('rs_sum_w_in', xs_in[0], cvec, 1, 0), _sum_into('rs_sum_w_out', xs[4], cvec, 1, 0),
        _sum_into('rs_sum_pw1', xs[5], cvec, 1, 0), _sum_into('rs_sum_pw2', xs[6], cvec, 1, 0)], gather=[pack])
    big = dict(zip(['w_ffn_in', 'w_ffn_out', 'w_in', 'w_out', 'w_pw1', 'w_pw2'], reduced))
    return loss_vec, dx, small, big


def kernel(x, c, ctx, c_ctx, w_mod, b_mod, g_mix, g_ffn, w_ffn_in, w_ffn_out, w_in, q_gain, k_gain, w_sp, b_sp, w_out, w_pw1, b_pw1, w_dw, b_dw, ln_g, ln_b, w_pw2, b_pw2, g_final, loss_target, m_c_ctx, m_w_mod, m_b_mod, m_g_mix, m_g_ffn, m_w_ffn_in, m_w_ffn_out, m_w_in, m_q_gain, m_k_gain, m_w_sp, m_b_sp, m_w_out, m_w_pw1, m_b_pw1, m_w_dw, m_b_dw, m_ln_g, m_ln_b, m_w_pw2, m_b_pw2, m_g_final, v_c_ctx, v_w_mod, v_b_mod, v_g_mix, v_g_ffn, v_w_ffn_in, v_w_ffn_out, v_w_in, v_q_gain, v_k_gain, v_w_sp, v_b_sp, v_w_out, v_w_pw1, v_b_pw1, v_w_dw, v_b_dw, v_ln_g, v_ln_b, v_w_pw2, v_b_pw2, v_g_final):
    given = dict(locals())
    ix, iy, ic = _mesh_pos()
    chip = 2 * ix + iy
    me = 2 * chip + ic
    d = D_MODEL
    q4 = d // N_CHIP

    small = jnp.concatenate([c.reshape(4, q4), b_pw1.reshape(2, q4), w_dw[0], b_dw, ln_g, ln_b, b_pw2,
                             jnp.zeros((7, q4), F32)], axis=0)
    def my_half(a):
        r = a.shape[0] // 2
        return lax.dynamic_slice_in_dim(a, ic * r, r, axis=0).astype(BF16)

    sgath, g_w_in = _allgather('ag_first', [small, my_half(w_in[0])])
    c_all = sgath[:, 0:4].reshape(N_DEV, d)
    per_chip = sgath[0::2]
    b_pw1_f = per_chip[:, 4:6].reshape(1, 2 * d)
    w_dw_f = per_chip[:, 6:6 + CONV_W].transpose(1, 0, 2).reshape(CONV_W, d)
    b_dw_f, ln_g_f, ln_b_f, b_pw2_f = [per_chip[:, 37 + i].reshape(1, d) for i in range(4)]

    c16 = jnp.concatenate([c_all, c_ctx[None, :], jnp.zeros((7, d), F32)], axis=0)
    bm_sh = lax.dynamic_slice_in_dim(b_mod, chip * MOD_BLK, MOD_BLK, axis=1).reshape(2, 1, MOD_BLK)
    mod_piece = _mod_fwd('mod_fwd', c16, w_mod, bm_sh)
    mgath = _allgather('ag_mod', [mod_piece.reshape(32, MOD_BLK)])[0]
    mod_all = mgath[0::2].reshape(N_CHIP, 2, 16, MOD_BLK).transpose(1, 2, 0, 3).reshape(2, 16, MOD_W)
    mod_me = lax.dynamic_index_in_dim(mod_all, me, axis=1, keepdims=False)
    cmod = mod_all[0, N_DEV]

    w_in_f = g_w_in.reshape(N_CHIP, d, -1).transpose(1, 0, 2).reshape(d, -1)
    W = dict(
        w_qkv=w_in_f[:, :QKV_W], w_sgp=w_in_f[:, QKV_W:],
        g_mix=g_mix, g_ffn=g_ffn, g_final=g_final.reshape(1, d), q_gain=q_gain, k_gain=k_gain, w_sp=w_sp[0],
        b_sp=b_sp[0], b_pw1=b_pw1_f, w_dw=w_dw_f, b_dw=b_dw_f, ln_g=ln_g_f, ln_b=ln_b_f, b_pw2=b_pw2_f)
    pending = [my_half(w_out[0]), my_half(w_ffn_in[0]), my_half(w_ffn_out[0]), my_half(w_ffn_in[1]),
               my_half(w_pw2[0]), my_half(w_ffn_out[1]), my_half(w_pw1[0])]
    cvec = jnp.reshape(ic, (1,)).astype(jnp.int32)

    loss_vec, dx, pg, big = _sample_step(x[0], ctx[0], loss_target[0], mod_me, cmod, W, pending, cvec)
    loss = lax.psum(jnp.sum(loss_vec), ("x", "y", "c"))

    s8 = _sum_slots('sum_small_grads', pg, SMALL_ROWS)
    grads = dict(big)

    def chip_cols(a, width):
        return lax.dynamic_slice_in_dim(a, chip * width, width, axis=a.ndim - 1)

    grads['g_mix'] = s8[0:2]
    grads['g_ffn'] = s8[2:4]
    grads['g_final'] = s8[4]
    grads['q_gain'] = s8[5, :ATTN_W].reshape(N_Q_HEADS, HEAD_DIM).sum(axis=0)[None, :]
    grads['k_gain'] = s8[5, ATTN_W:ATTN_W + KV_W].reshape(N_KV_HEADS, HEAD_DIM).sum(axis=0)[None, :]
    grads['b_pw1'] = chip_cols(s8[6:8].reshape(1, 2 * d), 2 * q4)
    grads['w_dw'] = chip_cols(s8[8:8 + CONV_W], q4)[None]
    grads['b_dw'] = chip_cols(s8[39:40], q4)
    grads['ln_g'] = chip_cols(s8[40:41], q4)
    grads['ln_b'] = chip_cols(s8[41:42], q4)
    grads['b_pw2'] = chip_cols(s8[42:43], q4)
    grads['w_sp'] = s8[64:128].reshape(1, N_SG, CHUNK, CHUNK)
    grads['b_sp'] = s8[128, :SG_W].reshape(CHUNK, N_SG).T[None]

    dm_rows = pg[:, 43:55].reshape(N_DEV, 2, MOD_W)
    ctx_row = s8[55:61].reshape(1, MOD_W)
    dm = jnp.stack([jnp.concatenate([dm_rows[:, 0], ctx_row, jnp.zeros((7, MOD_W), F32)], axis=0),
                    jnp.concatenate([dm_rows[:, 1], jnp.zeros((8, MOD_W), F32)], axis=0)], axis=0)
    grads['b_mod'] = jnp.stack([s8[43:49].reshape(MOD_W) + ctx_row[0], s8[49:55].reshape(MOD_W)], axis=0)
    dm_sh = chip_cols(dm, MOD_BLK)
    grads['w_mod'] = _mod_wgrad('mod_dw', c16, dm_sh)
    d_silu = _mm_rows('mod_dx', dm_sh[0], w_mod[0], 16, trans_b=True)
    parts = _allgather('ag_cctx', [d_silu[8:16]])[0]
    grads['c_ctx'] = _cctx_grad('cctx_grad', parts, c_ctx.reshape(1, d))[0]

    deltas, new_m, new_v = {}, {}, {}
    for n in WEIGHT_NAMES:
        grads[n] = grads[n].reshape(given[n].shape)
        deltas[n], new_m[n], new_v[n] = _adamw(f'adamw_{n}', given[n], grads[n], given['m_' + n], given['v_' + n])

    return (loss, dx[None], *[grads[n] for n in WEIGHT_NAMES], *[deltas[n] for n in WEIGHT_NAMES],
            *[new_m[n] for n in WEIGHT_NAMES], *[new_v[n] for n in WEIGHT_NAMES])
```

```python
import functools
import math

import jax
import jax.numpy as jnp
from jax import lax
from jax.experimental import pallas as pl
from jax.experimental.pallas import tpu as pltpu

F32 = jnp.float32
BF16 = jnp.bfloat16
MESH = pl.DeviceIdType.MESH

D_MODEL = 1024
HEAD_DIM = 64
N_Q_HEADS = 8
N_KV_HEADS = 2
Q_GROUP = N_Q_HEADS // N_KV_HEADS
ATTN_W = N_Q_HEADS * HEAD_DIM
KV_W = N_KV_HEADS * HEAD_DIM
QKV_W = ATTN_W + 2 * KV_W
SG_W = D_MODEL - ATTN_W
N_SG = 4
CHUNK = 128
GRID_W = 64
ROPE_THETA = 10000.0
CONV_W = 31
CONV_HALO = 16
D_FF = 2816
MOD_W = 6 * D_MODEL
N_CHIP = 4
N_DEV = 8
FF_BLK = 2 * D_FF // N_CHIP
PW1_BLK = 2 * D_MODEL // N_CHIP
MOD_BLK = MOD_W // N_CHIP
EPS = 1e-6
GELU_C = math.sqrt(2.0 / math.pi)
GELU_A = 0.044715

ADAM_LR = 0.001
ADAM_B1 = 0.9
ADAM_B2 = 0.999
ADAM_EPS = 1e-08
ADAM_WD = 0.01
ADAM_STEP = 10

SMALL_ROWS = 136

VMEM_LIMIT_BYTES = 56 * 1024 * 1024

WEIGHT_NAMES = ['c_ctx', 'w_mod', 'b_mod', 'g_mix', 'g_ffn', 'w_ffn_in', 'w_ffn_out', 'w_in', 'q_gain', 'k_gain',
                'w_sp', 'b_sp', 'w_out', 'w_pw1', 'b_pw1', 'w_dw', 'b_dw', 'ln_g', 'ln_b', 'w_pw2', 'b_pw2',
                'g_final']


def _params():
    return pltpu.CompilerParams(vmem_limit_bytes=VMEM_LIMIT_BYTES)


def _ri(arr, width=None, col_block=0, row_off=0, row_fn=None):
    return (arr, arr.shape[1] if width is None else width, col_block,
            (lambda i: i + row_off) if row_fn is None else row_fn)


def _rowwise(name, body, n_tiles, tm, row_ins, vec_ins, row_outs, vec_outs):
    nri, nvi, nro = len(row_ins), len(vec_ins), len(row_outs)

    def kern(*refs):
        ri = refs[:nri]
        vi = refs[nri:nri + nvi]
        ro = refs[nri + nvi:nri + nvi + nro]
        vo = refs[nri + nvi + nro:]
        if vo:
            @pl.when(pl.program_id(0) == 0)
            def _():
                for r in vo:
                    r[...] = jnp.zeros(r.shape, r.dtype)
        body(ri, vi, ro, vo)

    in_specs = [pl.BlockSpec((tm, bw), lambda i, cb=cb, rf=rf: (rf(i), cb)) for (_, bw, cb, rf) in row_ins]
    in_specs += [pl.BlockSpec(a.shape, lambda i, nd=a.ndim: (0,) * nd) for a in vec_ins]
    out_specs = [pl.BlockSpec((tm, w), lambda i: (i, 0)) for (w, _) in row_outs]
    out_specs += [pl.BlockSpec(s, lambda i, nd=len(s): (0,) * nd) for s in vec_outs]
    out_shape = [jax.ShapeDtypeStruct((n_tiles * tm, w), dt) for (w, dt) in row_outs]
    out_shape += [jax.ShapeDtypeStruct(s, F32) for s in vec_outs]
    return pl.pallas_call(kern, name=name, grid=(n_tiles,), in_specs=in_specs, out_specs=out_specs,
                          out_shape=out_shape, compiler_params=_params())(*[a for (a, _, _, _) in row_ins], *vec_ins)


def _colsum(v):
    return jnp.sum(v, axis=0, keepdims=True)


def _rowmean(v):
    return jnp.mean(v, axis=-1, keepdims=True)


def _rms(h):
    r = lax.rsqrt(_rowmean(h * h) + EPS)
    return h * r, r


def _rms_bwd(n, r, gy):
    return r * (gy - n * _rowmean(gy * n))


def _ln_stats(v):
    xc = v - _rowmean(v)
    r = lax.rsqrt(_rowmean(xc * xc) + EPS)
    return xc * r, r


def _ln_bwd(xh, r, dxh):
    return r * (dxh - _rowmean(dxh) - xh * _rowmean(dxh * xh))


def _sigmoid(v):
    return 1.0 / (1.0 + jnp.exp(-v))


def _gelu(v):
    t = jnp.tanh(GELU_C * (v + GELU_A * (v * v * v)))
    return 0.5 * v * (1.0 + t), t


def _gelu_grad(v, t):
    return 0.5 * (1.0 + t) + 0.5 * v * (1.0 - t * t) * (GELU_C * (1.0 + 3.0 * GELU_A * (v * v)))


def _modnorm_fwd(name, h, g, sc, sh, tm):
    def body(ri, vi, ro, vo):
        n, _ = _rms(ri[0][...])
        ro[0][...] = ((n * vi[0][...]) * (1.0 + vi[1][...]) + vi[2][...]).astype(BF16)

    return _rowwise(name, body, h.shape[0] // tm, tm, [_ri(h)], [g, sc, sh], [(D_MODEL, BF16)], [])[0]


def _modnorm_bwd_tile(hv, dxm, gv, scv):
    n, r = _rms(hv)
    dy = dxm * (1.0 + scv)
    return _rms_bwd(n, r, dy * gv), _colsum(dxm), _colsum(dxm * (n * gv)), _colsum(dy * n)


def _mm_row_epilogue(name, mm_args, mm_specs, dx_fn, rows_in, vecs_in, n_vec_out, epilogue, tm, gate=None,
                     exchange=()):
    rows_in = list(rows_in) + ([(gate[0], None)] if gate else [])
    vecs_in = list(vecs_in) + ([gate[1]] if gate else [])
    ng = 1 if gate else 0
    n_vec_out += 2 * ng
    nmm, nr, nv, ne = len(mm_args), len(rows_in), len(vecs_in), len(exchange)
    m = rows_in[0][0].shape[0]
    last = m // tm - 1

    def kern(*refs):
        ins = refs[:nmm + nr + nv]
        o_ref = refs[nmm + nr + nv + ne]
        vo = refs[nmm + nr + nv + ne + 1 + ng:nmm + nr + nv + ne + 1 + ng + n_vec_out]
        if ne:
            srcs = refs[nmm + nr + nv:nmm + nr + nv + ne]
            rest = refs[nmm + nr + nv + ne + 1 + ng + n_vec_out:]
            start, finish = _exchange_phases(srcs, rest[:ne], *rest[ne:])
            pl.when(pl.program_id(0) == 0)(start)

        @pl.when(pl.program_id(0) == 0)
        def _():
            for r in vo:
                r[...] = jnp.zeros(r.shape, F32)

        row_out, incs = epilogue(dx_fn(*ins[:nmm]), [r[...] for r in ins[nmm:nmm + nr - ng]],
                                 [r[...] for r in ins[nmm + nr:nmm + nr + nv - ng]])
        o_ref[...] = row_out
        if ng:
            dy = row_out * ins[nmm + nr + nv - 1][...]
            refs[nmm + nr + nv + ne + 1][...] = dy.astype(BF16)
            incs = list(incs) + [_colsum(row_out * ins[nmm + nr - 1][...]), _colsum(dy)]
        for r, inc in zip(vo, incs):
            r[...] += inc
        if ne:
            pl.when(pl.program_id(0) == last)(finish)

    row = pl.BlockSpec((tm, D_MODEL), lambda i: (i, 0))
    vec = pl.BlockSpec((1, D_MODEL), lambda i: (0, 0))
    row_specs = [row if fn is None else pl.BlockSpec((tm, D_MODEL), lambda i, fn=fn: (fn(i), 0)) for _, fn in rows_in]
    return pl.pallas_call(
        kern, name=name, grid=(m // tm,), in_specs=list(mm_specs) + row_specs + [vec] * nv + [_ANY] * ne,
        out_specs=[row] * (1 + ng) + [vec] * n_vec_out + [_ANY] * ne,
        out_shape=[jax.ShapeDtypeStruct((m, D_MODEL), F32)] + [jax.ShapeDtypeStruct((m, D_MODEL), BF16)] * ng
        + [jax.ShapeDtypeStruct((1, D_MODEL), F32)] * n_vec_out
        + [jax.ShapeDtypeStruct(e.shape, e.dtype) for e in exchange],
        scratch_shapes=_exchange_scratch(ne) if ne else [],
        compiler_params=_params())(*mm_args, *[r for r, _ in rows_in], *vecs_in, *exchange)


def _norm_bwd_epilogue(dxm, rows, vecs):
    if len(rows) > 2:
        dxm = dxm + rows[2]
    dx, dsh, dsc, dg = _modnorm_bwd_tile(rows[0], dxm, vecs[0], vecs[1])
    return rows[1] + dx, [dsh, dsc, dg]


def _lnsilu_bwd_epilogue(ds, rows, vecs):
    xh, r = _ln_stats(rows[0])
    ln = xh * vecs[0] + vecs[1]
    sg = _sigmoid(ln)
    dln = ds * (sg * (1.0 + ln * (1.0 - sg)))
    return _ln_bwd(xh, r, dln * vecs[0]), [_colsum(dln * xh), _colsum(dln)]


def _modnorm_bwd(name, h, dxs, dh_in, g, sc, tm):
    ndx = len(dxs)

    def body(ri, vi, ro, vo):
        dxm = ri[1][...].astype(F32)
        for r in ri[2:1 + ndx]:
            dxm = dxm + r[...].astype(F32)
        dx, dsh, dsc, dg = _modnorm_bwd_tile(ri[0][...], dxm, vi[0][...], vi[1][...])
        vo[0][...] += dsh
        vo[1][...] += dsc
        vo[2][...] += dg
        if dh_in is not None:
            ro[0][...] = ri[1 + ndx][...] + dx

    row_ins = [_ri(h)] + list(dxs) + ([_ri(dh_in)] if dh_in is not None else [])
    row_outs = [(D_MODEL, F32)] if dh_in is not None else []
    outs = _rowwise(name, body, h.shape[0] // tm, tm, row_ins, [g, sc], row_outs, [(1, D_MODEL)] * 3)
    if dh_in is None:
        return (None, *outs)
    return tuple(outs)


def _head_mean(v, bd):
    hi = v.astype(BF16)
    lo = (v - hi.astype(F32)).astype(BF16)
    s = jnp.dot(hi, bd, preferred_element_type=F32) + jnp.dot(lo, bd, preferred_element_type=F32)
    return s * (1.0 / HEAD_DIM)


def _swap16(v):
    w = v.shape[1]
    lane = lax.broadcasted_iota(jnp.int32, v.shape, 1)
    return jnp.where((lane & 16) == 0, pltpu.roll(v, w - 16, 1), pltpu.roll(v, 16, 1))


def _rope(v, cos, sin):
    return v * cos + _swap16(v) * sin


def _rope_bwd(d, cos, sin):
    return d * cos + _swap16(d * sin)


def _q_wide(t):
    return jnp.concatenate([t] * (ATTN_W // KV_W), axis=1)


def _mm_qk_prep(name, xall, w_qkv, cos, sin, gq, gk, bdq, bdk, tm):
    r, k = xall.shape

    def kern(x_ref, w_ref, cos_ref, sin_ref, gq_ref, gk_ref, bdq_ref, bdk_ref, p_ref, q_ref, k_ref, v_ref):
        p = jnp.dot(x_ref[...], w_ref[...], preferred_element_type=F32)
        p_ref[...] = p
        cosv = cos_ref[...]
        sinv = sin_ref[...]
        pq = p[:, :ATTN_W]
        rq = lax.rsqrt(_head_mean(pq * pq, bdq_ref[...]) + EPS)
        q_ref[...] = (_rope(pq * rq * gq_ref[...], _q_wide(cosv), _q_wide(sinv)) * (HEAD_DIM ** -0.5)).astype(BF16)
        pk = p[:, ATTN_W:ATTN_W + KV_W]
        rk = lax.rsqrt(_head_mean(pk * pk, bdk_ref[...]) + EPS)
        k_ref[...] = _rope(pk * rk * gk_ref[...], cosv, sinv).astype(BF16)
        v_ref[...] = p[:, ATTN_W + KV_W:].astype(BF16)

    def rows(w):
        return pl.BlockSpec((tm, w), lambda i: (i, 0))

    def whole(a):
        return pl.BlockSpec(a.shape, lambda i: (0, 0))

    return pl.pallas_call(
        kern, name=name, grid=(r // tm,),
        in_specs=[rows(k), whole(w_qkv), rows(KV_W), rows(KV_W), whole(gq), whole(gk), whole(bdq), whole(bdk)],
        out_specs=[rows(QKV_W), rows(ATTN_W), rows(KV_W), rows(KV_W)],
        out_shape=[jax.ShapeDtypeStruct((r, QKV_W), F32), jax.ShapeDtypeStruct((r, ATTN_W), BF16),
                   jax.ShapeDtypeStruct((r, KV_W), BF16), jax.ShapeDtypeStruct((r, KV_W), BF16)],
        compiler_params=_params())(xall, w_qkv, cos, sin, gq, gk, bdq, bdk)


def _qk_prep_bwd(name, p_qkv, cos, sin, dq, n_ctx_tiles, dk, dv, gq, gk, bdq, bdk, tm):
    def norm_rope_bwd(p, gain, bd, cosv, sinv, dout):
        r = lax.rsqrt(_head_mean(p * p, bd) + EPS)
        n = p * r
        dqn = _rope_bwd(dout, cosv, sinv)
        gy = dqn * gain
        return r * (gy - n * _head_mean(gy * n, bd)), _colsum(dqn * n)

    def body(ri, vi, ro, vo):
        pkv = ri[1][...]
        cosv = ri[2][...]
        sinv = ri[3][...]
        dqv = jnp.where(pl.program_id(0) >= n_ctx_tiles, ri[4][...] * (HEAD_DIM ** -0.5), 0.0)
        dxq, dgq = norm_rope_bwd(ri[0][...], vi[0][...], vi[2][...], _q_wide(cosv), _q_wide(sinv), dqv)
        dxk, dgk = norm_rope_bwd(pkv[:, :KV_W], vi[1][...], vi[3][...], cosv, sinv, ri[5][...])
        ro[0][:, :ATTN_W] = dxq.astype(BF16)
        ro[0][:, ATTN_W:ATTN_W + KV_W] = dxk.astype(BF16)
        ro[0][:, ATTN_W + KV_W:] = ri[6][...].astype(BF16)
        vo[0][...] += dgq
        vo[1][...] += dgk

    return _rowwise(name, body, p_qkv.shape[0] // tm, tm,
                    [_ri(p_qkv, ATTN_W, 0), _ri(p_qkv, 2 * KV_W, 2), _ri(cos), _ri(sin),
                     _ri(dq, row_fn=lambda i: jnp.maximum(i - n_ctx_tiles, 0)), _ri(dk), _ri(dv)],
                    [gq, gk, bdq, bdk], [(QKV_W, BF16)], [(1, ATTN_W), (1, KV_W)])


def _mm_sg(name, xm, w_sgp, w_sp, bsp, tm):
    m, k = xm.shape

    def kern(x_ref, w_ref, wsp_ref, bsp_ref, p_ref, o_ref):
        p = jnp.dot(x_ref[...], w_ref[...], preferred_element_type=F32)
        p_ref[...] = p
        gu, _ = _gelu(p[:, :SG_W])
        gv, _ = _gelu(p[:, SG_W:])
        for g in range(N_SG):
            sl = slice(g * CHUNK, (g + 1) * CHUNK)
            vn, _ = _ln_stats(gv[:, sl])
            vnb = vn.astype(BF16)
            for c in range(tm // CHUNK):
                rs = slice(c * CHUNK, (c + 1) * CHUNK)
                mixed = jnp.dot(wsp_ref[g], vnb[rs], preferred_element_type=F32) + bsp_ref[:, sl]
                o_ref[rs, sl] = (gu[rs, sl] * mixed).astype(BF16)

    return pl.pallas_call(
        kern, name=name, grid=(m // tm,),
        in_specs=[pl.BlockSpec((tm, k), lambda i: (i, 0)), pl.BlockSpec(w_sgp.shape, lambda i: (0, 0)),
                  pl.BlockSpec(w_sp.shape, lambda i: (0, 0, 0)), pl.BlockSpec(bsp.shape, lambda i: (0, 0))],
        out_specs=[pl.BlockSpec((tm, 2 * SG_W), lambda i: (i, 0)), pl.BlockSpec((tm, SG_W), lambda i: (i, 0))],
        out_shape=[jax.ShapeDtypeStruct((m, 2 * SG_W), F32), jax.ShapeDtypeStruct((m, SG_W), BF16)],
        compiler_params=_params())(xm, w_sgp, w_sp, bsp)


def _sg_bwd(name, p_sg, dcat, w_sp, w_sp_t, bsp, tm):
    def body(ri, vi, ro, vo):
        p = ri[0][...]
        dsg = ri[1][...]
        su = p[:, :SG_W]
        sv = p[:, SG_W:]
        gu, tu = _gelu(su)
        gv, tv = _gelu(sv)
        for g in range(N_SG):
            sl = slice(g * CHUNK, (g + 1) * CHUNK)
            vn, r = _ln_stats(gv[:, sl])
            vnb = vn.astype(BF16)
            for m in range(tm // CHUNK):
                rs = slice(m * CHUNK, (m + 1) * CHUNK)
                mixed = jnp.dot(vi[0][g], vnb[rs], preferred_element_type=F32) + vi[2][:, sl]
                d_o = dsg[rs, sl]
                dm = d_o * gu[rs, sl]
                vo[1][:, sl] += dm
                dmb = dm.astype(BF16)
                vo[0][g] += lax.dot_general(dmb, vnb[rs], (((1,), (1,)), ((), ())), preferred_element_type=F32)
                dvn = jnp.dot(vi[1][g], dmb, preferred_element_type=F32)
                dgv = _ln_bwd(vn[rs], r[rs], dvn)
                ro[0][rs, sl] = (d_o * mixed * _gelu_grad(su[rs, sl], tu[rs, sl])).astype(BF16)
                ro[0][rs, SG_W + g * CHUNK:SG_W + (g + 1) * CHUNK] = (
                    dgv * _gelu_grad(sv[rs, sl], tv[rs, sl])).astype(BF16)

    return _rowwise(name, body, p_sg.shape[0] // tm, tm, [_ri(p_sg), _ri(dcat, SG_W, 1)], [w_sp, w_sp_t, bsp],
                    [(2 * SG_W, BF16)], [(N_SG, CHUNK, CHUNK), (CHUNK, SG_W)])


def _glu_bwd(name, ag, dhg, b, tm):
    def body(ri, vi, ro, vo):
        v = ri[0][...] + vi[0][...]
        d = ri[1][...]
        a = v[:, :D_MODEL]
        sg = _sigmoid(v[:, D_MODEL:])
        da = d * sg
        dgate = d * a * sg * (1.0 - sg)
        ro[0][:, :D_MODEL] = da.astype(BF16)
        ro[0][:, D_MODEL:] = dgate.astype(BF16)
        vo[0][:, :D_MODEL] += _colsum(da)
        vo[0][:, D_MODEL:] += _colsum(dgate)

    return _rowwise(name, body, ag.shape[0] // tm, tm, [_ri(ag), _ri(dhg)], [b], [(2 * D_MODEL, BF16)],
                    [(1, 2 * D_MODEL)])


def _lnsilu_fwd(name, hc, g, b, tm):
    def body(ri, vi, ro, vo):
        xh, _ = _ln_stats(ri[0][...])
        ln = xh * vi[0][...] + vi[1][...]
        ro[0][...] = (ln * _sigmoid(ln)).astype(BF16)

    return _rowwise(name, body, hc.shape[0] // tm, tm, [_ri(hc)], [g, b], [(D_MODEL, BF16)], [])[0]


def _conv_blocks(seq):
    cb = 128
    tt = 128 if seq % 128 == 0 else seq
    return cb, tt


def _conv_taps(win, tt):
    n = win.shape[0]
    for s in range(8):
        ws = win if s == 0 else pltpu.roll(win, n - s, 0)
        for q in range(4):
            j = 8 * q + s - 1
            if 0 <= j < CONV_W:
                yield j, ws[8 * q:8 * q + tt, :]


def _fill_padded(pad_ref, x_ref, seq):
    zeros = jnp.zeros((CONV_HALO, pad_ref.shape[1]), F32)
    pad_ref[0:CONV_HALO, :] = zeros
    pad_ref[seq + CONV_HALO:seq + 2 * CONV_HALO, :] = zeros
    pad_ref[CONV_HALO:seq + CONV_HALO, :] = x_ref[...]


def _dwconv(name, xin, w, b):
    seq = xin.shape[0]
    cb, tt = _conv_blocks(seq)

    def kern(x_ref, w_ref, b_ref, o_ref, pad_ref):
        _fill_padded(pad_ref, x_ref, seq)
        wv = w_ref[...]
        bv = b_ref[...]

        def step(t, carry):
            base = pl.multiple_of(t * tt, tt)
            win = pad_ref[pl.ds(base, tt + 2 * CONV_HALO), :]
            acc = jnp.zeros((tt, cb), F32) + bv
            for j, rows in _conv_taps(win, tt):
                acc = acc + wv[j:j + 1, :] * rows
            o_ref[pl.ds(base, tt), :] = acc
            return carry

        lax.fori_loop(0, seq // tt, step, 0)

    return pl.pallas_call(
        kern, name=name, grid=(D_MODEL // cb,),
        in_specs=[pl.BlockSpec((seq, cb), lambda j: (0, j)), pl.BlockSpec((CONV_W, cb), lambda j: (0, j)),
                  pl.BlockSpec((1, cb), lambda j: (0, j))],
        out_specs=pl.BlockSpec((seq, cb), lambda j: (0, j)),
        out_shape=jax.ShapeDtypeStruct((seq, D_MODEL), F32),
        scratch_shapes=[pltpu.VMEM((seq + 2 * CONV_HALO, cb), F32)],
        compiler_params=_params())(xin, w, b)


def _dwconv_wgrad(name, xin, dout):
    seq = xin.shape[0]
    cb, tt = _conv_blocks(seq)

    def kern(x_ref, d_ref, dw_ref, db_ref, pad_ref):
        _fill_padded(pad_ref, x_ref, seq)
        dw_ref[...] = jnp.zeros(dw_ref.shape, F32)
        db_ref[...] = jnp.zeros(db_ref.shape, F32)

        def step(t, carry):
            base = pl.multiple_of(t * tt, tt)
            win = pad_ref[pl.ds(base, tt + 2 * CONV_HALO), :]
            d = d_ref[pl.ds(base, tt), :]
            db_ref[...] += _colsum(d)
            for j, rows in _conv_taps(win, tt):
                dw_ref[j:j + 1, :] += _colsum(d * rows)
            return carry

        lax.fori_loop(0, seq // tt, step, 0)

    return pl.pallas_call(
        kern, name=name, grid=(D_MODEL // cb,),
        in_specs=[pl.BlockSpec((seq, cb), lambda j: (0, j)), pl.BlockSpec((seq, cb), lambda j: (0, j))],
        out_specs=[pl.BlockSpec((CONV_W, cb), lambda j: (0, j)), pl.BlockSpec((1, cb), lambda j: (0, j))],
        out_shape=[jax.ShapeDtypeStruct((CONV_W, D_MODEL), F32), jax.ShapeDtypeStruct((1, D_MODEL), F32)],
        scratch_shapes=[pltpu.VMEM((seq + 2 * CONV_HALO, cb), F32)],
        compiler_params=_params())(xin, dout)


def _mm(name, a, b, out_sds, grid, a_spec, b_spec, o_spec, contract, k_axis=None, bias=None, pair_send=()):
    dn = (contract, ((), ()))
    ns = len(pair_send)
    nb = 0 if bias is None else 1
    n_steps = math.prod(grid)

    def kern(*refs):
        a_ref, b_ref = refs[:2]
        bias_ref = refs[2] if nb else None
        o_ref = refs[2 + nb + ns]
        if ns:
            copies = _pair_send_copies(refs[2 + nb:2 + nb + ns], refs[3 + nb + ns:3 + nb + 2 * ns],
                                       *refs[3 + nb + 2 * ns:])
            step = pl.program_id(0)
            for ax in range(1, len(grid)):
                step = step * grid[ax] + pl.program_id(ax)

            @pl.when(step == 0)
            def _():
                for cp in copies():
                    cp.start()

        p = lax.dot_general(a_ref[...].astype(BF16), b_ref[...].astype(BF16), dn, preferred_element_type=F32)
        if bias is not None:
            p = p + bias_ref[...]
        if k_axis is None:
            o_ref[...] = p.astype(o_ref.dtype)
        else:
            k = pl.program_id(k_axis)

            @pl.when(k == 0)
            def _():
                o_ref[...] = p

            @pl.when(k > 0)
            def _():
                o_ref[...] += p

        if ns:
            @pl.when(step == n_steps - 1)
            def _():
                for cp in copies():
                    cp.wait()

    in_specs = [a_spec, b_spec]
    args = [a, b]
    if bias is not None:
        in_specs.append(pl.BlockSpec(bias.shape, lambda *_: (0,) * bias.ndim))
        args.append(bias)
    if not ns:
        return pl.pallas_call(kern, name=name, grid=grid, in_specs=in_specs, out_specs=o_spec, out_shape=out_sds,
                              compiler_params=_params())(*args)
    return pl.pallas_call(
        kern, name=name, grid=grid, in_specs=in_specs + [_ANY] * ns, out_specs=[o_spec] + [_ANY] * ns,
        out_shape=[out_sds] + [jax.ShapeDtypeStruct((N_CHIP,) + g.shape[2:], g.dtype) for g in pair_send],
        scratch_shapes=[pltpu.SemaphoreType.DMA((ns, N_CHIP)), pltpu.SemaphoreType.DMA((ns, N_CHIP))],
        compiler_params=_params())(*args, *pair_send)


def _mm_rows(name, a, b, tm, trans_b=False, bias=None):
    m, k = a.shape
    n = b.shape[0] if trans_b else b.shape[1]
    contract = ((1,), (1,)) if trans_b else ((1,), (0,))
    return _mm(name, a, b, jax.ShapeDtypeStruct((m, n), F32), (m // tm,),
               pl.BlockSpec((tm, k), lambda i: (i, 0)), pl.BlockSpec(b.shape, lambda i: (0, 0)),
               pl.BlockSpec((tm, n), lambda i: (i, 0)), contract, bias=bias)


def _mm_glu(name, a, b, bias, tm):
    m, k = a.shape
    n = b.shape[1]

    def kern(a_ref, b_ref, bias_ref, ag_ref, hg_ref):
        v = jnp.dot(a_ref[...], b_ref[...], preferred_element_type=F32)
        ag_ref[...] = v
        v = v + bias_ref[...]
        hg_ref[...] = v[:, :n // 2] * _sigmoid(v[:, n // 2:])

    return pl.pallas_call(
        kern, name=name, grid=(m // tm,),
        in_specs=[pl.BlockSpec((tm, k), lambda i: (i, 0)), pl.BlockSpec(b.shape, lambda i: (0, 0)),
                  pl.BlockSpec((1, n), lambda i: (0, 0))],
        out_specs=[pl.BlockSpec((tm, n), lambda i: (i, 0)), pl.BlockSpec((tm, n // 2), lambda i: (i, 0))],
        out_shape=[jax.ShapeDtypeStruct((m, n), F32), jax.ShapeDtypeStruct((m, n // 2), F32)],
        compiler_params=_params())(a, b, bias)


def _mm_rows_res(name, a, b, h, gt, tm, bias=None, norm=None, final=None):
    m, k = a.shape
    n = b.shape[1]
    nb = 0 if bias is None else 1
    extra = list(norm or ()) + list(final or ())

    def kern(a_ref, b_ref, h_ref, gt_ref, *rest):
        y = jnp.dot(a_ref[...], b_ref[...], preferred_element_type=F32)
        if nb:
            y = y + rest[0][...]
        ex = rest[nb:nb + len(extra)]
        outs = rest[nb + len(extra):]
        hn = h_ref[...] + gt_ref[...] * y
        if final is None:
            outs[0][...] = y
            outs[1][...] = hn
        if norm is not None:
            nn, _ = _rms(hn)
            outs[2][...] = ((nn * ex[0][...]) * (1.0 + ex[1][...]) + ex[2][...]).astype(BF16)
        if final is not None:
            @pl.when(pl.program_id(0) == 0)
            def _():
                for o in (outs[1], outs[2], outs[4]):
                    o[...] = jnp.zeros(o.shape, F32)

            gv = ex[1][...]
            nn, r = _rms(hn)
            e = nn * gv - ex[0][...]
            outs[2][...] += _colsum(e * e) * (0.5 / D_MODEL)
            dout = e * (1.0 / D_MODEL)
            outs[1][...] += _colsum(dout * nn)
            dh = _rms_bwd(nn, r, dout * gv)
            outs[0][...] = dh
            outs[3][...] = (dh * gt_ref[...]).astype(BF16)
            outs[4][...] += _colsum(dh * y)

    row = pl.BlockSpec((tm, n), lambda i: (i, 0))
    vec = pl.BlockSpec((1, n), lambda i: (0, 0))
    sds = jax.ShapeDtypeStruct((m, n), F32)
    vsd = jax.ShapeDtypeStruct((1, n), F32)
    in_specs = [pl.BlockSpec((tm, k), lambda i: (i, 0)), pl.BlockSpec(b.shape, lambda i: (0, 0)), row, vec] + [vec] * nb
    out_specs, out_shape = [row, row], [sds, sds]
    if norm is not None:
        in_specs += [vec] * 3
        out_specs.append(row)
        out_shape.append(jax.ShapeDtypeStruct((m, n), BF16))
    if final is not None:
        in_specs += [row, vec]
        out_specs = [row, vec, vec, row, vec]
        out_shape = [sds, vsd, vsd, jax.ShapeDtypeStruct((m, n), BF16), vsd]
    return pl.pallas_call(kern, name=name, grid=(m // tm,), in_specs=in_specs, out_specs=out_specs,
                          out_shape=out_shape, compiler_params=_params())(
        a, b, h, gt, *([bias] if nb else []), *extra)


def _ffn_in_swiglu(name, xf, wblk, tm, gather=()):
    m, k = xf.shape
    nblk = wblk.shape[2]
    ng = len(gather)
    n_i = m // tm
    last = 2 * n_i - 1

    def kern(x_ref, wg_ref, wu_ref, *rest):
        gu_ref, a_ref = rest[ng:ng + 2]
        if ng:
            start, forward, finish = _gather_phases(rest[:ng], rest[ng + 2:2 * ng + 2], *rest[2 * ng + 2:])
            step = pl.program_id(0) * n_i + pl.program_id(1)
            pl.when(step == 0)(start)
            pl.when(step == last)(forward)
        xv = x_ref[...]
        g = jnp.dot(xv, wg_ref[...], preferred_element_type=F32)
        u = jnp.dot(xv, wu_ref[...], preferred_element_type=F32)
        gu_ref[0] = g.astype(BF16)
        gu_ref[1] = u.astype(BF16)
        a_ref[...] = (g * _sigmoid(g) * u).astype(BF16)
        if ng:
            pl.when(step == last)(finish)

    return pl.pallas_call(
        kern, name=name, grid=(2, n_i),
        in_specs=[pl.BlockSpec((tm, k), lambda j, i: (i, 0)), pl.BlockSpec((None, k, nblk), lambda j, i: (j, 0, 0)),
                  pl.BlockSpec((None, k, nblk), lambda j, i: (j + 2, 0, 0))] + [_ANY] * ng,
        out_specs=[pl.BlockSpec((2, tm, nblk), lambda j, i: (0, i, j)),
                   pl.BlockSpec((tm, nblk), lambda j, i: (i, j))] + [_ANY] * ng,
        out_shape=[jax.ShapeDtypeStruct((2, m, 2 * nblk), BF16), jax.ShapeDtypeStruct((m, 2 * nblk), BF16)]
        + [jax.ShapeDtypeStruct((N_DEV,) + g.shape, g.dtype) for g in gather],
        scratch_shapes=_gather_scratch(ng) if ng else [],
        compiler_params=_params())(xf, wblk, wblk, *gather)


def _ffn_out_dx_swiglu(name, dy, wo, gu, tm):
    m, k = dy.shape
    nblk = gu.shape[2] // 2

    def kern(dy_ref, w_ref, gu_ref, o_ref):
        da = lax.dot_general(dy_ref[...], w_ref[...], (((1,), (1,)), ((), ())), preferred_element_type=F32)
        g = gu_ref[0].astype(F32)
        sg = _sigmoid(g)
        o_ref[0] = (da * gu_ref[1].astype(F32) * (sg * (1.0 + g * (1.0 - sg)))).astype(BF16)
        o_ref[1] = (da * (g * sg)).astype(BF16)

    return pl.pallas_call(
        kern, name=name, grid=(2, m // tm),
        in_specs=[pl.BlockSpec((tm, k), lambda j, i: (i, 0)), pl.BlockSpec((nblk, k), lambda j, i: (j, 0)),
                  pl.BlockSpec((2, tm, nblk), lambda j, i: (0, i, j))],
        out_specs=pl.BlockSpec((2, tm, nblk), lambda j, i: (0, i, j)),
        out_shape=jax.ShapeDtypeStruct(gu.shape, BF16), compiler_params=_params())(dy, wo, gu)


def _ffn_in_dx_norm_bwd(name, dgu, wblk, h, dh_in, g, sc, tm, gate=None):
    nb, k, nblk = wblk.shape

    def dx_fn(a_ref, w_ref):
        acc = None
        for j in range(nb):
            p = lax.dot_general(a_ref[j // 2, :, (j % 2) * nblk:(j % 2 + 1) * nblk], w_ref[j],
                                (((1,), (1,)), ((), ())), preferred_element_type=F32)
            acc = p if acc is None else acc + p
        return acc

    return _mm_row_epilogue(name, [dgu, wblk],
                            [pl.BlockSpec((2, tm, 2 * nblk), lambda i: (0, i, 0)),
                             pl.BlockSpec(wblk.shape, lambda i: (0, 0, 0))], dx_fn, [(h, None), (dh_in, None)],
                            [g, sc], 3, _norm_bwd_epilogue, tm, gate=gate)


def _mm_t_epilogue(name, a, b, rows_in, vecs_in, n_vec_out, epilogue, tm, gate=None, exchange=()):
    def dx_fn(a_ref, b_ref):
        return lax.dot_general(a_ref[...], b_ref[...], (((1,), (1,)), ((), ())), preferred_element_type=F32)

    return _mm_row_epilogue(name, [a, b], [pl.BlockSpec((tm, a.shape[1]), lambda i: (i, 0)),
                                          pl.BlockSpec(b.shape, lambda i: (0, 0))], dx_fn, rows_in, vecs_in,
                            n_vec_out, epilogue, tm, gate, exchange)


def _mm_halves_wgrad_blk(name, a, b2, tk, pair_send=()):
    t, k = a.shape
    nblk = b2.shape[2] // 2
    return _mm(name, a, b2, jax.ShapeDtypeStruct((N_CHIP, k, nblk), F32), (N_CHIP, t // tk),
               pl.BlockSpec((tk, k), lambda j, i: (i, 0)),
               pl.BlockSpec((None, tk, nblk), lambda j, i: (j // 2, i, j % 2)),
               pl.BlockSpec((None, k, nblk), lambda j, i: (j, 0, 0)), ((0,), (0,)), k_axis=1, pair_send=pair_send)


def _mm_wgrad(name, a, b, tk):
    t, k = a.shape
    n = b.shape[1]
    tn = n if k * n * 4 <= 4 * 1024 * 1024 else 512
    return _mm(name, a, b, jax.ShapeDtypeStruct((k, n), F32), (n // tn, t // tk),
               pl.BlockSpec((tk, k), lambda j, i: (i, 0)), pl.BlockSpec((tk, tn), lambda j, i: (i, j)),
               pl.BlockSpec((k, tn), lambda j, i: (0, j)), ((0,), (0,)), k_axis=1)


def _attn_fwd(name, q_all, k, v, n_ctx, tq, gather=()):
    r = k.shape[1]
    s = q_all.shape[0] - n_ctx
    gw = Q_GROUP * HEAD_DIM
    ng = len(gather)
    n_i = s // tq
    last = N_KV_HEADS * n_i - 1
    forward_at = (7 * last) // 8

    def kern(q_ref, k_ref, v_ref, *rest):
        o_ref, lse_ref = rest[ng], rest[ng + 1]
        if ng:
            start, forward, finish = _gather_phases(rest[:ng], rest[ng + 2:2 * ng + 2], *rest[2 * ng + 2:])
            step = pl.program_id(0) * n_i + pl.program_id(1)
            pl.when(step == 0)(start)
            pl.when(step == forward_at)(forward)
        kv = k_ref[...]
        vv = v_ref[...]
        for g in range(Q_GROUP):
            sl = slice(g * HEAD_DIM, (g + 1) * HEAD_DIM)
            sc = lax.dot_general(q_ref[:, sl], kv, (((1,), (1,)), ((), ())), preferred_element_type=F32)
            m = jnp.max(sc, axis=-1, keepdims=True)
            p = jnp.exp((sc - m).astype(BF16))
            ol = jnp.dot(p, vv, preferred_element_type=F32)
            l = ol[:, HEAD_DIM:HEAD_DIM + 1]
            o_ref[:, sl] = ol[:, :HEAD_DIM] / l
            lse_ref[g] = m + jnp.log(l)
        if ng:
            pl.when(step == last)(finish)

    return pl.pallas_call(
        kern, name=name, grid=(N_KV_HEADS, n_i),
        in_specs=[pl.BlockSpec((tq, gw), lambda j, i: (i + n_ctx // tq, j)),
                  pl.BlockSpec((None, r, HEAD_DIM), lambda j, i: (j, 0, 0)),
                  pl.BlockSpec((None, r, 2 * HEAD_DIM), lambda j, i: (j, 0, 0))] + [_ANY] * ng,
        out_specs=[pl.BlockSpec((tq, gw), lambda j, i: (i, j)),
                   pl.BlockSpec((Q_GROUP, tq, 1), lambda j, i: (j, i, 0))] + [_ANY] * ng,
        out_shape=[jax.ShapeDtypeStruct((s, ATTN_W), F32), jax.ShapeDtypeStruct((N_Q_HEADS, s, 1), F32)]
        + [jax.ShapeDtypeStruct((N_DEV,) + a.shape, a.dtype) for a in gather],
        scratch_shapes=_gather_scratch(ng) if ng else [],
        compiler_params=_params())(q_all, k, v, *gather)


def _attn_bwd(name, q_all, k, v, o, lse, dcat, n_ctx, tq, exchange=()):
    nkv, r, _ = k.shape
    s = o.shape[0]
    gw = Q_GROUP * HEAD_DIM
    ne = len(exchange)
    n_i = s // tq
    last = nkv * n_i - 1

    def kern(q_ref, k_ref, v_ref, o_ref, lse_ref, do_ref, *rest):
        dq_ref, dk_ref, dv_ref = rest[ne:ne + 3]
        ds_all, p_all = rest[2 * ne + 3:2 * ne + 5]
        if ne:
            start, finish = _exchange_phases(rest[:ne], rest[ne + 3:2 * ne + 3], *rest[2 * ne + 5:])
            step = pl.program_id(0) * n_i + pl.program_id(1)
            pl.when(step == 0)(start)

        @pl.when(pl.program_id(1) == 0)
        def _():
            dk_ref[...] = jnp.zeros(dk_ref.shape, F32)
            dv_ref[...] = jnp.zeros(dv_ref.shape, F32)

        kv = k_ref[...]
        vv = v_ref[...]
        qs = []
        dos = []
        for g in range(Q_GROUP):
            sl = slice(g * HEAD_DIM, (g + 1) * HEAD_DIM)
            rows = slice(g * tq, (g + 1) * tq)
            qv = q_ref[:, sl]
            dov = do_ref[:, sl]
            delta = jnp.sum(dov * o_ref[:, sl], axis=-1, keepdims=True)
            sc = lax.dot_general(qv, kv, (((1,), (1,)), ((), ())), preferred_element_type=F32)
            p = jnp.exp(sc - lse_ref[g])
            dob = dov.astype(BF16)
            dp = lax.dot_general(dob, vv, (((1,), (1,)), ((), ())), preferred_element_type=F32)
            dsb = (p * (dp - delta)).astype(BF16)
            dq_ref[:, sl] = jnp.dot(dsb, kv, preferred_element_type=F32)
            ds_all[rows, :] = dsb
            p_all[rows, :] = p.astype(BF16)
            qs.append(qv)
            dos.append(dob)
        dk_ref[...] += lax.dot_general(jnp.concatenate(qs, axis=0), ds_all[...], (((0,), (0,)), ((), ())),
                                       preferred_element_type=F32)
        dv_ref[...] += lax.dot_general(jnp.concatenate(dos, axis=0), p_all[...], (((0,), (0,)), ((), ())),
                                       preferred_element_type=F32)
        if ne:
            pl.when(step == last)(finish)

    kvmap = lambda j, i: (j, 0, 0)
    return pl.pallas_call(
        kern, name=name, grid=(nkv, n_i),
        in_specs=[pl.BlockSpec((tq, gw), lambda j, i: (i + n_ctx // tq, j)), pl.BlockSpec((None, r, HEAD_DIM), kvmap),
                  pl.BlockSpec((None, r, HEAD_DIM), kvmap), pl.BlockSpec((tq, gw), lambda j, i: (i, j)),
                  pl.BlockSpec((Q_GROUP, tq, 1), lambda j, i: (j, i, 0)),
                  pl.BlockSpec((tq, gw), lambda j, i: (i, j))] + [_ANY] * ne,
        out_specs=[pl.BlockSpec((tq, gw), lambda j, i: (i, j)), pl.BlockSpec((None, HEAD_DIM, r), kvmap),
                   pl.BlockSpec((None, HEAD_DIM, r), kvmap)] + [_ANY] * ne,
        out_shape=[jax.ShapeDtypeStruct((s, ATTN_W), F32), jax.ShapeDtypeStruct((nkv, HEAD_DIM, r), F32),
                   jax.ShapeDtypeStruct((nkv, HEAD_DIM, r), F32)]
        + [jax.ShapeDtypeStruct(a.shape, a.dtype) for a in exchange],
        scratch_shapes=[pltpu.VMEM((Q_GROUP * tq, r), BF16)] * 2 + (_exchange_scratch(ne) if ne else []),
        compiler_params=_params())(q_all, k, v, o, lse, dcat, *exchange)


def _mod_fwd(name, c16, wm, bm):
    n = wm.shape[2]

    def kern(c_ref, w_ref, b_ref, o_ref):
        cv = c_ref[...]
        a = (cv * _sigmoid(cv)).astype(BF16)
        o_ref[...] = jnp.dot(a, w_ref[...].astype(BF16), preferred_element_type=F32) + b_ref[...]

    return pl.pallas_call(
        kern, name=name, grid=(2,),
        in_specs=[pl.BlockSpec(c16.shape, lambda l: (0, 0)), pl.BlockSpec((None, D_MODEL, n), lambda l: (l, 0, 0)),
                  pl.BlockSpec((None, 1, n), lambda l: (l, 0, 0))],
        out_specs=pl.BlockSpec((None, 16, n), lambda l: (l, 0, 0)),
        out_shape=jax.ShapeDtypeStruct((2, 16, n), F32), compiler_params=_params())(c16, wm, bm)


def _mod_wgrad(name, c16, dm):
    n = dm.shape[2]
    tn = 512

    def kern(c_ref, d_ref, o_ref):
        cv = c_ref[...]
        a = cv * _sigmoid(cv)
        o_ref[...] = lax.dot_general(a, d_ref[...], (((0,), (0,)), ((), ())), preferred_element_type=F32,
                                     precision=lax.Precision.HIGHEST)

    return pl.pallas_call(
        kern, name=name, grid=(2, n // tn),
        in_specs=[pl.BlockSpec(c16.shape, lambda l, j: (0, 0)), pl.BlockSpec((None, 16, tn), lambda l, j: (l, 0, j))],
        out_specs=pl.BlockSpec((None, D_MODEL, tn), lambda l, j: (l, 0, j)),
        out_shape=jax.ShapeDtypeStruct((2, D_MODEL, n), F32), compiler_params=_params())(c16, dm)


def _cctx_grad(name, parts, c_ctx):
    def kern(p_ref, c_ref, o_ref):
        d = p_ref[0, 0:1, :] + p_ref[2, 0:1, :] + p_ref[4, 0:1, :] + p_ref[6, 0:1, :]
        cv = c_ref[...]
        sg = _sigmoid(cv)
        o_ref[...] = d * (sg * (1.0 + cv * (1.0 - sg)))

    return pl.pallas_call(kern, name=name, out_shape=jax.ShapeDtypeStruct((1, D_MODEL), F32),
                          compiler_params=_params())(parts, c_ctx)


def _sum_slots(name, g, tr):
    n, rows, cols = g.shape

    def kern(g_ref, o_ref):
        acc = g_ref[0].astype(F32)
        for i in range(1, n):
            acc = acc + g_ref[i].astype(F32)
        o_ref[...] = acc

    return pl.pallas_call(kern, name=name, grid=(rows // tr,),
                          in_specs=[pl.BlockSpec((n, tr, cols), lambda i: (0, i, 0))],
                          out_specs=pl.BlockSpec((tr, cols), lambda i: (i, 0)),
                          out_shape=jax.ShapeDtypeStruct((rows, cols), F32), compiler_params=_params())(g)


def _sum_into(name, g, cvec, n_layers, layer, prev=None):
    n, r, cc = g.shape
    tr = _row_tile(r, cc, 512 * 1024)

    def kern(c_ref, g_ref, *rest):
        acc = g_ref[0].astype(F32)
        for i in range(1, n):
            acc = acc + g_ref[i].astype(F32)
        rest[-1][...] = acc

    in_specs = [pl.BlockSpec((n, tr, cc), lambda i, c_ref: (0, i, 0))]
    args = [cvec, g]
    aliases = {}
    if prev is not None:
        in_specs.append(_ANY)
        args.append(prev)
        aliases = {2: 0}
    grid_spec = pltpu.PrefetchScalarGridSpec(
        num_scalar_prefetch=1, grid=(r // tr,), in_specs=in_specs,
        out_specs=pl.BlockSpec((None, None, tr, cc), lambda i, c_ref: (layer, c_ref[0], i, 0)))
    return pl.pallas_call(kern, name=name, grid_spec=grid_spec,
                          out_shape=jax.ShapeDtypeStruct((n_layers, 2, r, cc), F32),
                          input_output_aliases=aliases, compiler_params=_params())(*args)


def _row_tile(rows, cols, max_bytes=1024 * 1024):
    if rows * cols * 4 <= 2 * max_bytes:
        return rows
    best = None
    for t in range(16, rows + 1, 16):
        if rows % t == 0 and t * cols * 4 <= max_bytes:
            best = t
    assert best is not None, (rows, cols)
    return best


def _adamw(name, w, g, m, v):
    shape = w.shape
    cols = shape[-1]
    rows = w.size // cols
    tr = _row_tile(rows, cols)

    def kern(w_ref, g_ref, m_ref, v_ref, d_ref, nm_ref, nv_ref):
        gv = g_ref[...]
        m2 = ADAM_B1 * m_ref[...] + (1.0 - ADAM_B1) * gv
        v2 = ADAM_B2 * v_ref[...] + (1.0 - ADAM_B2) * (gv * gv)
        m_hat = m2 / (1.0 - ADAM_B1 ** ADAM_STEP)
        v_hat = v2 / (1.0 - ADAM_B2 ** ADAM_STEP)
        d_ref[...] = -ADAM_LR * (m_hat / (jnp.sqrt(v_hat) + ADAM_EPS) + ADAM_WD * w_ref[...])
        nm_ref[...] = m2
        nv_ref[...] = v2

    spec = pl.BlockSpec((tr, cols), lambda i: (i, 0))
    sds = jax.ShapeDtypeStruct((rows, cols), F32)
    outs = pl.pallas_call(kern, name=name, grid=(rows // tr,), in_specs=[spec] * 4, out_specs=[spec] * 3,
                          out_shape=[sds] * 3, compiler_params=_params())(
        w.reshape(rows, cols), g.reshape(rows, cols), m.reshape(rows, cols), v.reshape(rows, cols))
    return tuple(o.reshape(shape) for o in outs)


_ANY = pl.BlockSpec(memory_space=pl.ANY)


def _mesh_pos():
    return lax.axis_index("x"), lax.axis_index("y"), lax.axis_index("c")


def _gather_phases(srcs, outs, send_sems, recv_sems, local_sems):
    n = len(srcs)
    x, y, c = _mesh_pos()
    me = (x, y, c)
    sibling = (x, y, 1 - c)
    chips = [(1 - x, y), (x, 1 - y), (1 - x, 1 - y)]

    def slot(px, py, pc):
        return 4 * px + 2 * py + pc

    def copy(t, k, s, to, src=None):
        dst = outs[t].at[s]
        return pltpu.make_async_remote_copy(src_ref=dst if src is None else src, dst_ref=dst,
                                            send_sem=send_sems.at[t, k], recv_sem=recv_sems.at[t, k],
                                            device_id=to, device_id_type=MESH)

    def mine(t):
        return pltpu.make_async_copy(srcs[t], outs[t].at[slot(*me)], local_sems.at[t])

    def first(t):
        return [copy(t, 0, slot(*me), sibling, src=srcs[t])] + [
            copy(t, 1 + j, slot(*me), (px, py, c), src=srcs[t]) for j, (px, py) in enumerate(chips)]

    def passed(t, j):
        px, py = chips[j]
        return copy(t, 4 + j, slot(px, py, c), sibling)

    def start():
        for t in range(n):
            mine(t).start()
        for t in range(n):
            for cp in first(t):
                cp.start()

    def forward():
        for j, (px, py) in enumerate(chips):
            for t in range(n):
                copy(t, 1 + j, slot(px, py, c), me).wait_recv()
                passed(t, j).start()

    def finish():
        for t in range(n):
            copy(t, 0, slot(x, y, 1 - c), me).wait_recv()
            for j, (px, py) in enumerate(chips):
                copy(t, 4 + j, slot(px, py, 1 - c), me).wait_recv()
        for t in range(n):
            for cp in first(t) + [passed(t, j) for j in range(len(chips))]:
                cp.wait_send()
            mine(t).wait()

    return start, forward, finish


def _gather_scratch(n):
    return [pltpu.SemaphoreType.DMA((n, 7)), pltpu.SemaphoreType.DMA((n, 7)), pltpu.SemaphoreType.DMA((n,))]


def _allgather(name, items):
    n = len(items)

    def body(*refs):
        start, forward, finish = _gather_phases(refs[:n], refs[n:2 * n], *refs[2 * n:])
        start()
        forward()
        finish()

    return pl.pallas_call(
        body, name=name, in_specs=[_ANY] * n, out_specs=[_ANY] * n,
        out_shape=[jax.ShapeDtypeStruct((N_DEV,) + a.shape, a.dtype) for a in items],
        scratch_shapes=_gather_scratch(n), compiler_params=_params())(*items)


def _pair_send_copies(srcs, bufs, send_sems, recv_sems):
    x, y, c = _mesh_pos()

    def copies():
        return [pltpu.make_async_remote_copy(
            src_ref=srcs[t].at[k, 1 - c], dst_ref=bufs[t].at[k], send_sem=send_sems.at[t, k],
            recv_sem=recv_sems.at[t, k], device_id=(x, y, 1 - c), device_id_type=MESH)
            for t in range(len(srcs)) for k in range(N_CHIP)]

    return copies


def _rs_pair_send(name, gs):
    n = len(gs)

    def body(*refs):
        copies = _pair_send_copies(refs[:n], refs[n:2 * n], *refs[2 * n:])
        for cp in copies():
            cp.start()
        for cp in copies():
            cp.wait()

    return pl.pallas_call(
        body, name=name, in_specs=[_ANY] * n, out_specs=[_ANY] * n,
        out_shape=[jax.ShapeDtypeStruct((N_CHIP,) + g.shape[2:], g.dtype) for g in gs],
        scratch_shapes=[pltpu.SemaphoreType.DMA((n, N_CHIP)), pltpu.SemaphoreType.DMA((n, N_CHIP))],
        compiler_params=_params())(*gs)


def _rs_pair_add(name, g, buf, cvec):
    _, _, r, cc = g.shape
    tr = _row_tile(r, cc, 2 * 1024 * 1024)

    def kern(c_ref, g_ref, b_ref, o_ref):
        o_ref[...] = (g_ref[...] + b_ref[...]).astype(BF16)

    grid_spec = pltpu.PrefetchScalarGridSpec(
        num_scalar_prefetch=1, grid=(N_CHIP, r // tr),
        in_specs=[pl.BlockSpec((None, None, tr, cc), lambda k, i, c_ref: (k, c_ref[0], i, 0)),
                  pl.BlockSpec((None, tr, cc), lambda k, i, c_ref: (k, i, 0))],
        out_specs=pl.BlockSpec((None, tr, cc), lambda k, i, c_ref: (k, i, 0)))
    return pl.pallas_call(kern, name=name, grid_spec=grid_spec,
                          out_shape=jax.ShapeDtypeStruct((N_CHIP, r, cc), BF16),
                          compiler_params=_params())(cvec, g, buf)


def _exchange_phases(srcs, bufs, send_sems, recv_sems, local_sems):
    n = len(srcs)
    x, y, c = _mesh_pos()
    kme = 2 * x + y
    chips = [(1 - x, y), (x, 1 - y), (1 - x, 1 - y)]

    def copies():
        local = [pltpu.make_async_copy(srcs[t].at[kme], bufs[t].at[kme], local_sems.at[t]) for t in range(n)]
        remote = []
        for t in range(n):
            for j, (px, py) in enumerate(chips):
                remote.append(pltpu.make_async_remote_copy(
                    src_ref=srcs[t].at[2 * px + py], dst_ref=bufs[t].at[kme], send_sem=send_sems.at[t, j],
                    recv_sem=recv_sems.at[t, j], device_id=(px, py, c), device_id_type=MESH))
        return local, remote

    def start():
        local, remote = copies()
        for cp in local + remote:
            cp.start()

    def finish():
        local, remote = copies()
        for cp in remote + local:
            cp.wait()

    return start, finish


def _exchange_scratch(n):
    return [pltpu.SemaphoreType.DMA((n, 3)), pltpu.SemaphoreType.DMA((n, 3)), pltpu.SemaphoreType.DMA((n,))]


def _rs_chip_exchange(name, ss):
    n = len(ss)

    def body(*refs):
        start, finish = _exchange_phases(refs[:n], refs[n:2 * n], *refs[2 * n:])
        start()
        finish()

    return pl.pallas_call(
        body, name=name, in_specs=[_ANY] * n, out_specs=[_ANY] * n,
        out_shape=[jax.ShapeDtypeStruct(s.shape, s.dtype) for s in ss],
        scratch_shapes=_exchange_scratch(n), compiler_params=_params())(*ss)


def _rs_pair_share(name, dsts, gather=()):
    nd = len(dsts)
    ng = len(gather)

    def body(*refs):
        ins = refs[:nd]
        outs = refs[nd + ng:2 * nd + ng]
        send_sems, recv_sems = refs[2 * (nd + ng):2 * (nd + ng) + 2]
        if ng:
            start, forward, finish = _gather_phases(refs[nd:nd + ng], refs[2 * nd + ng:2 * (nd + ng)],
                                                    *refs[2 * (nd + ng) + 2:])
            start()
        x, y, c = _mesh_pos()
        copies = []
        for d in range(nd):
            for l in range(dsts[d].shape[0]):
                copies.append(pltpu.make_async_remote_copy(
                    src_ref=ins[d].at[l, c], dst_ref=outs[d].at[l, c], send_sem=send_sems.at[d, l],
                    recv_sem=recv_sems.at[d, l], device_id=(x, y, 1 - c), device_id_type=MESH))
        for cp in copies:
            cp.start()
        if ng:
            forward()
        for cp in copies:
            cp.wait()
        if ng:
            finish()

    return pl.pallas_call(
        body, name=name, in_specs=[_ANY] * (nd + ng), out_specs=[_ANY] * (nd + ng),
        out_shape=[jax.ShapeDtypeStruct(a.shape, a.dtype) for a in dsts]
        + [jax.ShapeDtypeStruct((N_DEV,) + a.shape, a.dtype) for a in gather],
        input_output_aliases={d: d for d in range(nd)},
        scratch_shapes=[pltpu.SemaphoreType.DMA((nd, 2)), pltpu.SemaphoreType.DMA((nd, 2))]
        + (_gather_scratch(ng) if ng else []),
        compiler_params=_params())(*dsts, *gather)


def _rope_tables(seq, n_ctx):
    t = jnp.arange(seq)
    row = (t // GRID_W).astype(F32)
    col = (t % GRID_W).astype(F32)
    inv = ROPE_THETA ** (-jnp.arange(0, HEAD_DIM // 2, 2, dtype=F32) / (HEAD_DIM // 2))
    ang_r = row[:, None] * inv[None, :]
    ang_c = col[:, None] * inv[None, :]
    cos = jnp.concatenate([jnp.cos(ang_r)] * 2 + [jnp.cos(ang_c)] * 2, axis=1)
    sin = jnp.concatenate([-jnp.sin(ang_r), jnp.sin(ang_r), -jnp.sin(ang_c), jnp.sin(ang_c)], axis=1)
    cos = jnp.concatenate([jnp.ones((n_ctx, HEAD_DIM), F32), cos], axis=0)
    sin = jnp.concatenate([jnp.zeros((n_ctx, HEAD_DIM), F32), sin], axis=0)
    return jnp.tile(cos, (1, N_KV_HEADS)), jnp.tile(sin, (1, N_KV_HEADS))


def _to_heads(a, nh):
    return a.reshape(a.shape[0], nh, HEAD_DIM).transpose(1, 0, 2)


def _sample_step(x, ctx, tgt, mod, cmod, W, pending, cvec):
    W = dict(W)
    seq = x.shape[0]
    n_ctx = ctx.shape[0]
    tm = min(256, seq)
    tmc = min(256, n_ctx)
    tmm = 512 if seq % 512 == 0 else tm
    tmw = 1024 if seq % 1024 == 0 else tmm
    tmr = (n_ctx + seq) // 4 if (n_ctx + seq) % 64 == 0 else tm
    tmk = 2048 if seq % 2048 == 0 else tmw
    tq = min(256, seq)
    assert n_ctx % tm == 0 and seq % tm == 0

    def mv(l, j):
        return mod[l, j * D_MODEL:(j + 1) * D_MODEL].reshape(1, D_MODEL)

    csh1 = cmod[:D_MODEL].reshape(1, D_MODEL)
    csc1 = cmod[D_MODEL:2 * D_MODEL].reshape(1, D_MODEL)
    g_mix = [W['g_mix'][l:l + 1] for l in range(2)]
    g_ffn = [W['g_ffn'][l:l + 1] for l in range(2)]
    cos, sin = _rope_tables(seq, n_ctx)
    gq = jnp.tile(W['q_gain'], (1, N_Q_HEADS))
    gk = jnp.tile(W['k_gain'], (1, N_KV_HEADS))
    bdq = jnp.kron(jnp.eye(N_Q_HEADS, dtype=F32), jnp.ones((HEAD_DIM, HEAD_DIM), F32)).astype(BF16)
    bdk = bdq[:KV_W, :KV_W]
    w_sp = W['w_sp'].astype(BF16)
    w_sp_t = w_sp.transpose(0, 2, 1)
    bsp = jnp.broadcast_to(W['b_sp'].T[:, :, None], (CHUNK, N_SG, CHUNK)).reshape(CHUNK, SG_W)

    def ffn_fwd(l, h_mid, xf, gather=(), **tail):
        gu, a, *gathered = _ffn_in_swiglu(f'l{l}_ffn_in', xf, W['ffn_in'][l], tmw, gather=gather)
        if gathered:
            W['ffn_out'][1] = gathered[0].reshape(D_FF, D_MODEL)
            W['pw1'] = gathered[1].reshape(N_CHIP, D_MODEL, PW1_BLK).transpose(1, 0, 2).reshape(D_MODEL, 2 * D_MODEL)
        outs = _mm_rows_res(f'l{l}_ffn_out', a, W['ffn_out'][l], h_mid, mv(l, 5), tmm, **tail)
        return outs, (h_mid, xf, gu, a)

    def blocked(a):
        cols = a.shape[-1]
        return a.reshape(N_CHIP, 2, a.size // (2 * N_CHIP * cols), cols)

    def ffn_bwd(l, dh_out, dy, y_mix, saved, pair_send=()):
        h_mid, xf, gu, a = saved
        dgu = _ffn_out_dx_swiglu(f'l{l}_ffn_out_dx', dy, W['ffn_out'][l], gu, tmw)
        d_wo = _mm_wgrad(f'l{l}_ffn_out_dw', a, dy, tmw)
        d_wi = _mm_halves_wgrad_blk(f'l{l}_ffn_in_dw', xf, dgu, tmk, pair_send=pair_send)
        if pair_send:
            d_wi, *sent = d_wi
        else:
            sent = []
        dh_mid, dy_mix, d_sh2, d_sc2, d_g, d_gt1, sum_dy = _ffn_in_dx_norm_bwd(
            f'l{l}_ffn_in_dx', dgu, W['ffn_in'][l], h_mid, dh_out, g_ffn[l], mv(l, 4), tmm, gate=(y_mix, mv(l, 2)))
        return dh_mid, dy_mix, d_wi, d_wo, d_g, (d_sh2, d_sc2), d_gt1, sum_dy, sent

    xm0 = _modnorm_fwd('l0_mix_norm', x, g_mix[0], mv(0, 1), mv(0, 0), tm)
    xc0 = _modnorm_fwd('l0_ctx_norm', ctx, g_mix[0], csc1, csh1, tmc)
    xall = jnp.concatenate([xc0, xm0], axis=0)
    p_qkv, q_all, k_all, v_all = _mm_qk_prep('l0_qkv', xall, W['w_qkv'], cos, sin, gq, gk, bdq, bdk, tmr)
    p_sg, sg = _mm_sg('l0_sg_in', xm0, W['w_sgp'], w_sp, bsp, tmm)
    kh = _to_heads(k_all, N_KV_HEADS)
    vh = _to_heads(v_all, N_KV_HEADS)
    v_ones = jnp.concatenate([vh, jnp.ones(vh.shape[:2] + (1,), BF16),
                              jnp.zeros(vh.shape[:2] + (HEAD_DIM - 1,), BF16)], axis=2)
    o, lse, *gw = _attn_fwd('l0_attn', q_all, kh, v_ones, n_ctx, tq, gather=pending[:5])
    W['w_out'] = gw[0].reshape(D_MODEL, D_MODEL)
    W['pw2'] = gw[4].reshape(D_MODEL, D_MODEL)
    W['ffn_in'] = [gw[1].reshape(N_CHIP, D_MODEL, FF_BLK), gw[3].reshape(N_CHIP, D_MODEL, FF_BLK)]
    W['ffn_out'] = [gw[2].reshape(D_FF, D_MODEL), None]
    cat = jnp.concatenate([o.astype(BF16), sg], axis=1)
    y0, h1, xf0 = _mm_rows_res('l0_out', cat, W['w_out'], x, mv(0, 2), tmw, norm=(g_ffn[0], mv(0, 4), mv(0, 3)))
    (f0, h2, xm1), ffn0 = ffn_fwd(0, h1, xf0, gather=pending[5:], norm=(g_mix[1], mv(1, 1), mv(1, 0)))

    ag, hg = _mm_glu('l1_pw1', xm1, W['pw1'], W['b_pw1'], tmm)
    hc = _dwconv('l1_conv', hg, W['w_dw'], W['b_dw'])
    s1 = _lnsilu_fwd('l1_ln_silu', hc, W['ln_g'], W['ln_b'], tm)
    y1, h3, xf1 = _mm_rows_res('l1_pw2', s1, W['pw2'], h2, mv(1, 2), tmw, bias=W['b_pw2'],
                               norm=(g_ffn[1], mv(1, 4), mv(1, 3)))
    (dh4, d_g_final, loss_vec, dyf1, d_gt2_1), ffn1 = ffn_fwd(1, h3, xf1, final=(tgt, W['g_final']))

    dh3, dy1, d_wi1, d_wo1, d_gffn1, dmod1_ffn, d_gt1_1, d_b_pw2, _ = ffn_bwd(1, dh4, dyf1, y1, ffn1)
    d_pw2 = _mm_wgrad('l1_pw2_dw', s1, dy1, tmw)
    dhc, d_ln_g, d_ln_b = _mm_t_epilogue('l1_pw2_dx', dy1, W['pw2'], [(hc, None)], [W['ln_g'], W['ln_b']], 2,
                                         _lnsilu_bwd_epilogue, tmm)
    dhg = _dwconv('l1_conv_dx', dhc, W['w_dw'][::-1], jnp.zeros((1, D_MODEL), F32))
    d_w_dw, d_b_dw = _dwconv_wgrad('l1_conv_dw', hg, dhc)
    dag, d_b_pw1 = _glu_bwd('l1_glu_bwd', ag, dhg, W['b_pw1'], tm)
    d_pw1 = _mm_wgrad('l1_pw1_dw', xm1, dag, tmw).reshape(D_MODEL, N_CHIP, PW1_BLK).transpose(1, 0, 2)
    dh2, dyf0, d_sh1_1, d_sc1_1, d_gmix1, d_gt2_0, _ = _mm_t_epilogue(
        'l1_pw1_dx', dag, W['pw1'], [(h2, None), (dh3, None)], [g_mix[1], mv(1, 1)], 3, _norm_bwd_epilogue, tmm,
        gate=(f0, mv(0, 5)))

    gs1 = [blocked(g) for g in (d_wi1, d_wo1, d_pw1, d_pw2)]
    dh1, dy0, d_wi0, d_wo0, d_gffn0, dmod0_ffn, d_gt1_0, _, sib1 = ffn_bwd(0, dh2, dyf0, y0, ffn0, pair_send=gs1)
    dcat = _mm_rows('l0_out_dx', dy0, W['w_out'], tmw, trans_b=True)
    d_w_out = _mm_wgrad('l0_out_dw', cat, dy0, tmw)

    gs0 = [blocked(g) for g in (d_wi0, d_wo0, d_w_out)]
    sib0 = _rs_pair_send('rs_pair_send', gs0)
    gs = [gs0[0], gs1[0], gs0[1], gs1[1], gs0[2], gs1[2], gs1[3]]
    sib = [sib0[0], sib1[0], sib0[1], sib1[1], sib0[2], sib1[2], sib1[3]]
    ss = [_rs_pair_add(f'rs_pair_add{t}', gs[t], sib[t], cvec) for t in range(len(gs))]
    dq, dk, dv, *xs = _attn_bwd('l0_attn_bwd', q_all, kh, vh, o, lse, dcat, n_ctx, tq, exchange=ss)
    dk, dv = [t.transpose(2, 0, 1).reshape(n_ctx + seq, KV_W) for t in (dk, dv)]
    dp_qkv, d_gq, d_gk = _qk_prep_bwd('l0_qk_prep_bwd', p_qkv, cos, sin, dq, n_ctx // tm, dk, dv, gq, gk, bdq, bdk,
                                      tm)
    dp_sg, d_w_sp, d_bsp = _sg_bwd('l0_sg_bwd', p_sg, dcat, w_sp, w_sp_t, bsp, tm)
    d_w_qkv = _mm_wgrad('l0_qkv_dw', xall, dp_qkv, tmr)
    d_w_sgp = _mm_wgrad('l0_sg_in_dw', xm0, dp_sg, tmw)
    dxall = _mm_rows('l0_qkv_dx', dp_qkv, W['w_qkv'], tmr, trans_b=True)
    g_in = blocked(jnp.concatenate([d_w_qkv, d_w_sgp], axis=1).reshape(D_MODEL, N_CHIP, -1).transpose(1, 0, 2))
    sib_in = _rs_pair_send('rs_pair_send_w_in', [g_in])
    ss_in = _rs_pair_add('rs_pair_add_w_in', g_in, sib_in[0], cvec)
    dx, d_sh1_0, d_sc1_0, d_gmix0, *xs_in = _mm_t_epilogue(
        'l0_sg_in_dx', dp_sg, W['w_sgp'], [(x, None), (dh1, None), (dxall, lambda i: i + n_ctx // tm)],
        [g_mix[0], mv(0, 1)], 3, _norm_bwd_epilogue, tm, exchange=[ss_in])
    _, d_csh1, d_csc1, d_gmix0c = _modnorm_bwd('l0_ctx_norm_bwd', ctx, [_ri(dxall)], None, g_mix[0], csc1, tmc)

    zero = jnp.zeros((1, D_MODEL), F32)
    d = D_MODEL
    pack = jnp.concatenate([
        d_gmix0 + d_gmix0c, d_gmix1, d_gffn0, d_gffn1, d_g_final,
        jnp.concatenate([d_gq, d_gk, jnp.zeros((1, d - ATTN_W - KV_W), F32)], axis=1),
        d_b_pw1.reshape(2, d), d_w_dw, d_b_dw, d_ln_g, d_ln_b, d_b_pw2,
        d_sh1_0, d_sc1_0, d_gt1_0, *dmod0_ffn, d_gt2_0, d_sh1_1, d_sc1_1, d_gt1_1, *dmod1_ffn, d_gt2_1,
        d_csh1, d_csc1, zero, zero, zero, zero, jnp.zeros((3, d), F32),
        d_w_sp.reshape(64, d),
        jnp.pad(d_bsp.reshape(CHUNK, N_SG, CHUNK).sum(axis=-1).reshape(1, CHUNK * N_SG), ((0, 7), (0, d - SG_W)))],
        axis=0)
    assert pack.shape == (SMALL_ROWS, d)

    t_ffn_in = _sum_into('rs_sum_ffn_in1', xs[1], cvec, 2, 1, prev=_sum_into('rs_sum_ffn_in0', xs[0], cvec, 2, 0))
    t_ffn_out = _sum_into('rs_sum_ffn_out1', xs[3], cvec, 2, 1, prev=_sum_into('rs_sum_ffn_out0', xs[2], cvec, 2, 0))
    *reduced, small = _rs_pair_share('rs_pair_share', [
        t_ffn_in, t_ffn_out, _sum_into('rs_sum_w_in', xs_in[0], cvec, 1, 0), _sum_into('rs_sum_w_out', xs[4], cvec, 1, 0),
        _sum_into('rs_sum_pw1', xs[5], cvec, 1, 0), _sum_into('rs_sum_pw2', xs[6], cvec, 1, 0)], gather=[pack])
    big = dict(zip(['w_ffn_in', 'w_ffn_out', 'w_in', 'w_out', 'w_pw1', 'w_pw2'], reduced))
    return loss_vec, dx, small, big


def kernel(x, c, ctx, c_ctx, w_mod, b_mod, g_mix, g_ffn, w_ffn_in, w_ffn_out, w_in, q_gain, k_gain, w_sp, b_sp, w_out, w_pw1, b_pw1, w_dw, b_dw, ln_g, ln_b, w_pw2, b_pw2, g_final, loss_target, m_c_ctx, m_w_mod, m_b_mod, m_g_mix, m_g_ffn, m_w_ffn_in, m_w_ffn_out, m_w_in, m_q_gain, m_k_gain, m_w_sp, m_b_sp, m_w_out, m_w_pw1, m_b_pw1, m_w_dw, m_b_dw, m_ln_g, m_ln_b, m_w_pw2, m_b_pw2, m_g_final, v_c_ctx, v_w_mod, v_b_mod, v_g_mix, v_g_ffn, v_w_ffn_in, v_w_ffn_out, v_w_in, v_q_gain, v_k_gain, v_w_sp, v_b_sp, v_w_out, v_w_pw1, v_b_pw1, v_w_dw, v_b_dw, v_ln_g, v_ln_b, v_w_pw2, v_b_pw2, v_g_final):
    given = dict(locals())
    ix, iy, ic = _mesh_pos()
    chip = 2 * ix + iy
    me = 2 * chip + ic
    d = D_MODEL
    q4 = d // N_CHIP

    small = jnp.concatenate([c.reshape(4, q4), b_pw1.reshape(2, q4), w_dw[0], b_dw, ln_g, ln_b, b_pw2,
                             jnp.zeros((7, q4), F32)], axis=0)
    def my_half(a):
        r = a.shape[0] // 2
        return lax.dynamic_slice_in_dim(a, ic * r, r, axis=0).astype(BF16)

    sgath, g_w_in = _allgather('ag_first', [small, my_half(w_in[0])])
    c_all = sgath[:, 0:4].reshape(N_DEV, d)
    per_chip = sgath[0::2]
    b_pw1_f = per_chip[:, 4:6].reshape(1, 2 * d)
    w_dw_f = per_chip[:, 6:6 + CONV_W].transpose(1, 0, 2).reshape(CONV_W, d)
    b_dw_f, ln_g_f, ln_b_f, b_pw2_f = [per_chip[:, 37 + i].reshape(1, d) for i in range(4)]

    c16 = jnp.concatenate([c_all, c_ctx[None, :], jnp.zeros((7, d), F32)], axis=0)
    bm_sh = lax.dynamic_slice_in_dim(b_mod, chip * MOD_BLK, MOD_BLK, axis=1).reshape(2, 1, MOD_BLK)
    mod_piece = _mod_fwd('mod_fwd', c16, w_mod, bm_sh)
    mgath = _allgather('ag_mod', [mod_piece.reshape(32, MOD_BLK)])[0]
    mod_all = mgath[0::2].reshape(N_CHIP, 2, 16, MOD_BLK).transpose(1, 2, 0, 3).reshape(2, 16, MOD_W)
    mod_me = lax.dynamic_index_in_dim(mod_all, me, axis=1, keepdims=False)
    cmod = mod_all[0, N_DEV]

    w_in_f = g_w_in.reshape(N_CHIP, d, -1).transpose(1, 0, 2).reshape(d, -1)
    W = dict(
        w_qkv=w_in_f[:, :QKV_W], w_sgp=w_in_f[:, QKV_W:],
        g_mix=g_mix, g_ffn=g_ffn, g_final=g_final.reshape(1, d), q_gain=q_gain, k_gain=k_gain, w_sp=w_sp[0],
        b_sp=b_sp[0], b_pw1=b_pw1_f, w_dw=w_dw_f, b_dw=b_dw_f, ln_g=ln_g_f, ln_b=ln_b_f, b_pw2=b_pw2_f)
    pending = [my_half(w_out[0]), my_half(w_ffn_in[0]), my_half(w_ffn_out[0]), my_half(w_ffn_in[1]),
               my_half(w_pw2[0]), my_half(w_ffn_out[1]), my_half(w_pw1[0])]
    cvec = jnp.reshape(ic, (1,)).astype(jnp.int32)

    loss_vec, dx, pg, big = _sample_step(x[0], ctx[0], loss_target[0], mod_me, cmod, W, pending, cvec)
    loss = lax.psum(jnp.sum(loss_vec), ("x", "y", "c"))

    s8 = _sum_slots('sum_small_grads', pg, SMALL_ROWS)
    grads = dict(big)

    def chip_cols(a, width):
        return lax.dynamic_slice_in_dim(a, chip * width, width, axis=a.ndim - 1)

    grads['g_mix'] = s8[0:2]
    grads['g_ffn'] = s8[2:4]
    grads['g_final'] = s8[4]
    grads['q_gain'] = s8[5, :ATTN_W].reshape(N_Q_HEADS, HEAD_DIM).sum(axis=0)[None, :]
    grads['k_gain'] = s8[5, ATTN_W:ATTN_W + KV_W].reshape(N_KV_HEADS, HEAD_DIM).sum(axis=0)[None, :]
    grads['b_pw1'] = chip_cols(s8[6:8].reshape(1, 2 * d), 2 * q4)
    grads['w_dw'] = chip_cols(s8[8:8 + CONV_W], q4)[None]
    grads['b_dw'] = chip_cols(s8[39:40], q4)
    grads['ln_g'] = chip_cols(s8[40:41], q4)
    grads['ln_b'] = chip_cols(s8[41:42], q4)
    grads['b_pw2'] = chip_cols(s8[42:43], q4)
    grads['w_sp'] = s8[64:128].reshape(1, N_SG, CHUNK, CHUNK)
    grads['b_sp'] = s8[128, :SG_W].reshape(CHUNK, N_SG).T[None]

    dm_rows = pg[:, 43:55].reshape(N_DEV, 2, MOD_W)
    ctx_row = s8[55:61].reshape(1, MOD_W)
    dm = jnp.stack([jnp.concatenate([dm_rows[:, 0], ctx_row, jnp.zeros((7, MOD_W), F32)], axis=0),
                    jnp.concatenate([dm_rows[:, 1], jnp.zeros((8, MOD_W), F32)], axis=0)], axis=0)
    grads['b_mod'] = jnp.stack([s8[43:49].reshape(MOD_W) + ctx_row[0], s8[49:55].reshape(MOD_W)], axis=0)
    dm_sh = chip_cols(dm, MOD_BLK)
    grads['w_mod'] = _mod_wgrad('mod_dw', c16, dm_sh)
    d_silu = _mm_rows('mod_dx', dm_sh[0], w_mod[0], 16, trans_b=True)
    parts = _allgather('ag_cctx', [d_silu[8:16]])[0]
    grads['c_ctx'] = _cctx_grad('cctx_grad', parts, c_ctx.reshape(1, d))[0]

    deltas, new_m, new_v = {}, {}, {}
    for n in WEIGHT_NAMES:
        grads[n] = grads[n].reshape(given[n].shape)
        deltas[n], new_m[n], new_v[n] = _adamw(f'adamw_{n}', given[n], grads[n], given['m_' + n], given['v_' + n])

    return (loss, dx[None], *[grads[n] for n in WEIGHT_NAMES], *[deltas[n] for n in WEIGHT_NAMES],
            *[new_m[n] for n in WEIGHT_NAMES], *[new_v[n] for n in WEIGHT_NAMES])
```

```python
import functools
import math

import jax
import jax.numpy as jnp
from jax import lax
from jax.experimental import pallas as pl
from jax.experimental.pallas import tpu as pltpu

F32 = jnp.float32
BF16 = jnp.bfloat16
MESH = pl.DeviceIdType.MESH

D_MODEL = 1024
HEAD_DIM = 64
N_Q_HEADS = 8
N_KV_HEADS = 2
Q_GROUP = N_Q_HEADS // N_KV_HEADS
ATTN_W = N_Q_HEADS * HEAD_DIM
KV_W = N_KV_HEADS * HEAD_DIM
QKV_W = ATTN_W + 2 * KV_W
SG_W = D_MODEL - ATTN_W
N_SG = 4
CHUNK = 128
GRID_W = 64
ROPE_THETA = 10000.0
CONV_W = 31
CONV_HALO = 16
D_FF = 2816
MOD_W = 6 * D_MODEL
N_CHIP = 4
N_DEV = 8
FF_BLK = 2 * D_FF // N_CHIP
PW1_BLK = 2 * D_MODEL // N_CHIP
MOD_BLK = MOD_W // N_CHIP
EPS = 1e-6
GELU_C = math.sqrt(2.0 / math.pi)
GELU_A = 0.044715

ADAM_LR = 0.001
ADAM_B1 = 0.9
ADAM_B2 = 0.999
ADAM_EPS = 1e-08
ADAM_WD = 0.01
ADAM_STEP = 10

SMALL_ROWS = 136

VMEM_LIMIT_BYTES = 56 * 1024 * 1024

WEIGHT_NAMES = ['c_ctx', 'w_mod', 'b_mod', 'g_mix', 'g_ffn', 'w_ffn_in', 'w_ffn_out', 'w_in', 'q_gain', 'k_gain',
                'w_sp', 'b_sp', 'w_out', 'w_pw1', 'b_pw1', 'w_dw', 'b_dw', 'ln_g', 'ln_b', 'w_pw2', 'b_pw2',
                'g_final']


def _params():
    return pltpu.CompilerParams(vmem_limit_bytes=VMEM_LIMIT_BYTES)


def _ri(arr, width=None, col_block=0, row_off=0, row_fn=None):
    return (arr, arr.shape[1] if width is None else width, col_block,
            (lambda i: i + row_off) if row_fn is None else row_fn)


def _rowwise(name, body, n_tiles, tm, row_ins, vec_ins, row_outs, vec_outs):
    nri, nvi, nro = len(row_ins), len(vec_ins), len(row_outs)

    def kern(*refs):
        ri = refs[:nri]
        vi = refs[nri:nri + nvi]
        ro = refs[nri + nvi:nri + nvi + nro]
        vo = refs[nri + nvi + nro:]
        if vo:
            @pl.when(pl.program_id(0) == 0)
            def _():
                for r in vo:
                    r[...] = jnp.zeros(r.shape, r.dtype)
        body(ri, vi, ro, vo)

    in_specs = [pl.BlockSpec((tm, bw), lambda i, cb=cb, rf=rf: (rf(i), cb)) for (_, bw, cb, rf) in row_ins]
    in_specs += [pl.BlockSpec(a.shape, lambda i, nd=a.ndim: (0,) * nd) for a in vec_ins]
    out_specs = [pl.BlockSpec((tm, w), lambda i: (i, 0)) for (w, _) in row_outs]
    out_specs += [pl.BlockSpec(s, lambda i, nd=len(s): (0,) * nd) for s in vec_outs]
    out_shape = [jax.ShapeDtypeStruct((n_tiles * tm, w), dt) for (w, dt) in row_outs]
    out_shape += [jax.ShapeDtypeStruct(s, F32) for s in vec_outs]
    return pl.pallas_call(kern, name=name, grid=(n_tiles,), in_specs=in_specs, out_specs=out_specs,
                          out_shape=out_shape, compiler_params=_params())(*[a for (a, _, _, _) in row_ins], *vec_ins)


def _colsum(v):
    return jnp.sum(v, axis=0, keepdims=True)


def _rowmean(v):
    return jnp.mean(v, axis=-1, keepdims=True)


def _rms(h):
    r = lax.rsqrt(_rowmean(h * h) + EPS)
    return h * r, r


def _rms_bwd(n, r, gy):
    return r * (gy - n * _rowmean(gy * n))


def _ln_stats(v):
    xc = v - _rowmean(v)
    r = lax.rsqrt(_rowmean(xc * xc) + EPS)
    return xc * r, r


def _ln_bwd(xh, r, dxh):
    return r * (dxh - _rowmean(dxh) - xh * _rowmean(dxh * xh))


def _sigmoid(v):
    return 1.0 / (1.0 + jnp.exp(-v))


def _gelu(v):
    t = jnp.tanh(GELU_C * (v + GELU_A * (v * v * v)))
    return 0.5 * v * (1.0 + t), t


def _gelu_grad(v, t):
    return 0.5 * (1.0 + t) + 0.5 * v * (1.0 - t * t) * (GELU_C * (1.0 + 3.0 * GELU_A * (v * v)))


def _modnorm_fwd(name, h, g, sc, sh, tm):
    def body(ri, vi, ro, vo):
        n, _ = _rms(ri[0][...])
        ro[0][...] = ((n * vi[0][...]) * (1.0 + vi[1][...]) + vi[2][...]).astype(BF16)

    return _rowwise(name, body, h.shape[0] // tm, tm, [_ri(h)], [g, sc, sh], [(D_MODEL, BF16)], [])[0]


def _modnorm_bwd_tile(hv, dxm, gv, scv):
    n, r = _rms(hv)
    dy = dxm * (1.0 + scv)
    return _rms_bwd(n, r, dy * gv), _colsum(dxm), _colsum(dxm * (n * gv)), _colsum(dy * n)


def _mm_row_epilogue(name, mm_args, mm_specs, dx_fn, rows_in, vecs_in, n_vec_out, epilogue, tm, gate=None,
                     exchange=()):
    rows_in = list(rows_in) + ([(gate[0], None)] if gate else [])
    vecs_in = list(vecs_in) + ([gate[1]] if gate else [])
    ng = 1 if gate else 0
    n_vec_out += 2 * ng
    nmm, nr, nv, ne = len(mm_args), len(rows_in), len(vecs_in), len(exchange)
    m = rows_in[0][0].shape[0]
    last = m // tm - 1

    def kern(*refs):
        ins = refs[:nmm + nr + nv]
        o_ref = refs[nmm + nr + nv + ne]
        vo = refs[nmm + nr + nv + ne + 1 + ng:nmm + nr + nv + ne + 1 + ng + n_vec_out]
        if ne:
            srcs = refs[nmm + nr + nv:nmm + nr + nv + ne]
            rest = refs[nmm + nr + nv + ne + 1 + ng + n_vec_out:]
            start, finish = _exchange_phases(srcs, rest[:ne], *rest[ne:])
            pl.when(pl.program_id(0) == 0)(start)

        @pl.when(pl.program_id(0) == 0)
        def _():
            for r in vo:
                r[...] = jnp.zeros(r.shape, F32)

        row_out, incs = epilogue(dx_fn(*ins[:nmm]), [r[...] for r in ins[nmm:nmm + nr - ng]],
                                 [r[...] for r in ins[nmm + nr:nmm + nr + nv - ng]])
        o_ref[...] = row_out
        if ng:
            dy = row_out * ins[nmm + nr + nv - 1][...]
            refs[nmm + nr + nv + ne + 1][...] = dy.astype(BF16)
            incs = list(incs) + [_colsum(row_out * ins[nmm + nr - 1][...]), _colsum(dy)]
        for r, inc in zip(vo, incs):
            r[...] += inc
        if ne:
            pl.when(pl.program_id(0) == last)(finish)

    row = pl.BlockSpec((tm, D_MODEL), lambda i: (i, 0))
    vec = pl.BlockSpec((1, D_MODEL), lambda i: (0, 0))
    row_specs = [row if fn is None else pl.BlockSpec((tm, D_MODEL), lambda i, fn=fn: (fn(i), 0)) for _, fn in rows_in]
    return pl.pallas_call(
        kern, name=name, grid=(m // tm,), in_specs=list(mm_specs) + row_specs + [vec] * nv + [_ANY] * ne,
        out_specs=[row] * (1 + ng) + [vec] * n_vec_out + [_ANY] * ne,
        out_shape=[jax.ShapeDtypeStruct((m, D_MODEL), F32)] + [jax.ShapeDtypeStruct((m, D_MODEL), BF16)] * ng
        + [jax.ShapeDtypeStruct((1, D_MODEL), F32)] * n_vec_out
        + [jax.ShapeDtypeStruct(e.shape, e.dtype) for e in exchange],
        scratch_shapes=_exchange_scratch(ne) if ne else [],
        compiler_params=_params())(*mm_args, *[r for r, _ in rows_in], *vecs_in, *exchange)


def _norm_bwd_epilogue(dxm, rows, vecs):
    if len(rows) > 2:
        dxm = dxm + rows[2]
    dx, dsh, dsc, dg = _modnorm_bwd_tile(rows[0], dxm, vecs[0], vecs[1])
    return rows[1] + dx, [dsh, dsc, dg]


def _lnsilu_bwd_epilogue(ds, rows, vecs):
    xh, r = _ln_stats(rows[0])
    ln = xh * vecs[0] + vecs[1]
    sg = _sigmoid(ln)
    dln = ds * (sg * (1.0 + ln * (1.0 - sg)))
    return _ln_bwd(xh, r, dln * vecs[0]), [_colsum(dln * xh), _colsum(dln)]


def _modnorm_bwd(name, h, dxs, dh_in, g, sc, tm):
    ndx = len(dxs)

    def body(ri, vi, ro, vo):
        dxm = ri[1][...].astype(F32)
        for r in ri[2:1 + ndx]:
            dxm = dxm + r[...].astype(F32)
        dx, dsh, dsc, dg = _modnorm_bwd_tile(ri[0][...], dxm, vi[0][...], vi[1][...])
        vo[0][...] += dsh
        vo[1][...] += dsc
        vo[2][...] += dg
        if dh_in is not None:
            ro[0][...] = ri[1 + ndx][...] + dx

    row_ins = [_ri(h)] + list(dxs) + ([_ri(dh_in)] if dh_in is not None else [])
    row_outs = [(D_MODEL, F32)] if dh_in is not None else []
    outs = _rowwise(name, body, h.shape[0] // tm, tm, row_ins, [g, sc], row_outs, [(1, D_MODEL)] * 3)
    if dh_in is None:
        return (None, *outs)
    return tuple(outs)


def _head_mean(v, bd):
    hi = v.astype(BF16)
    lo = (v - hi.astype(F32)).astype(BF16)
    s = jnp.dot(hi, bd, preferred_element_type=F32) + jnp.dot(lo, bd, preferred_element_type=F32)
    return s * (1.0 / HEAD_DIM)


def _swap16(v):
    w = v.shape[1]
    lane = lax.broadcasted_iota(jnp.int32, v.shape, 1)
    return jnp.where((lane & 16) == 0, pltpu.roll(v, w - 16, 1), pltpu.roll(v, 16, 1))


def _rope(v, cos, sin):
    return v * cos + _swap16(v) * sin


def _rope_bwd(d, cos, sin):
    return d * cos + _swap16(d * sin)


def _q_wide(t):
    return jnp.concatenate([t] * (ATTN_W // KV_W), axis=1)


def _mm_qk_prep(name, xall, w_qkv, cos, sin, gq, gk, bdq, bdk, tm):
    r, k = xall.shape

    def kern(x_ref, w_ref, cos_ref, sin_ref, gq_ref, gk_ref, bdq_ref, bdk_ref, p_ref, q_ref, k_ref, v_ref):
        p = jnp.dot(x_ref[...], w_ref[...], preferred_element_type=F32)
        p_ref[...] = p
        cosv = cos_ref[...]
        sinv = sin_ref[...]
        pq = p[:, :ATTN_W]
        rq = lax.rsqrt(_head_mean(pq * pq, bdq_ref[...]) + EPS)
        q_ref[...] = (_rope(pq * rq * gq_ref[...], _q_wide(cosv), _q_wide(sinv)) * (HEAD_DIM ** -0.5)).astype(BF16)
        pk = p[:, ATTN_W:ATTN_W + KV_W]
        rk = lax.rsqrt(_head_mean(pk * pk, bdk_ref[...]) + EPS)
        k_ref[...] = _rope(pk * rk * gk_ref[...], cosv, sinv).astype(BF16)
        v_ref[...] = p[:, ATTN_W + KV_W:].astype(BF16)

    def rows(w):
        return pl.BlockSpec((tm, w), lambda i: (i, 0))

    def whole(a):
        return pl.BlockSpec(a.shape, lambda i: (0, 0))

    return pl.pallas_call(
        kern, name=name, grid=(r // tm,),
        in_specs=[rows(k), whole(w_qkv), rows(KV_W), rows(KV_W), whole(gq), whole(gk), whole(bdq), whole(bdk)],
        out_specs=[rows(QKV_W), rows(ATTN_W), rows(KV_W), rows(KV_W)],
        out_shape=[jax.ShapeDtypeStruct((r, QKV_W), F32), jax.ShapeDtypeStruct((r, ATTN_W), BF16),
                   jax.ShapeDtypeStruct((r, KV_W), BF16), jax.ShapeDtypeStruct((r, KV_W), BF16)],
        compiler_params=_params())(xall, w_qkv, cos, sin, gq, gk, bdq, bdk)


def _qk_prep_bwd(name, p_qkv, cos, sin, dq, n_ctx_tiles, dk, dv, gq, gk, bdq, bdk, tm):
    def norm_rope_bwd(p, gain, bd, cosv, sinv, dout):
        r = lax.rsqrt(_head_mean(p * p, bd) + EPS)
        n = p * r
        dqn = _rope_bwd(dout, cosv, sinv)
        gy = dqn * gain
        return r * (gy - n * _head_mean(gy * n, bd)), _colsum(dqn * n)

    def body(ri, vi, ro, vo):
        pkv = ri[1][...]
        cosv = ri[2][...]
        sinv = ri[3][...]
        dqv = jnp.where(pl.program_id(0) >= n_ctx_tiles, ri[4][...] * (HEAD_DIM ** -0.5), 0.0)
        dxq, dgq = norm_rope_bwd(ri[0][...], vi[0][...], vi[2][...], _q_wide(cosv), _q_wide(sinv), dqv)
        dxk, dgk = norm_rope_bwd(pkv[:, :KV_W], vi[1][...], vi[3][...], cosv, sinv, ri[5][...])
        ro[0][:, :ATTN_W] = dxq.astype(BF16)
        ro[0][:, ATTN_W:ATTN_W + KV_W] = dxk.astype(BF16)
        ro[0][:, ATTN_W + KV_W:] = ri[6][...].astype(BF16)
        vo[0][...] += dgq
        vo[1][...] += dgk

    return _rowwise(name, body, p_qkv.shape[0] // tm, tm,
                    [_ri(p_qkv, ATTN_W, 0), _ri(p_qkv, 2 * KV_W, 2), _ri(cos), _ri(sin),
                     _ri(dq, row_fn=lambda i: jnp.maximum(i - n_ctx_tiles, 0)), _ri(dk), _ri(dv)],
                    [gq, gk, bdq, bdk], [(QKV_W, BF16)], [(1, ATTN_W), (1, KV_W)])


def _mm_sg(name, xm, w_sgp, w_sp, bsp, tm):
    m, k = xm.shape

    def kern(x_ref, w_ref, wsp_ref, bsp_ref, p_ref, o_ref):
        p = jnp.dot(x_ref[...], w_ref[...], preferred_element_type=F32)
        p_ref[...] = p
        gu, _ = _gelu(p[:, :SG_W])
        gv, _ = _gelu(p[:, SG_W:])
        for g in range(N_SG):
            sl = slice(g * CHUNK, (g + 1) * CHUNK)
            vn, _ = _ln_stats(gv[:, sl])
            vnb = vn.astype(BF16)
            for c in range(tm // CHUNK):
                rs = slice(c * CHUNK, (c + 1) * CHUNK)
                mixed = jnp.dot(wsp_ref[g], vnb[rs], preferred_element_type=F32) + bsp_ref[:, sl]
                o_ref[rs, sl] = (gu[rs, sl] * mixed).astype(BF16)

    return pl.pallas_call(
        kern, name=name, grid=(m // tm,),
        in_specs=[pl.BlockSpec((tm, k), lambda i: (i, 0)), pl.BlockSpec(w_sgp.shape, lambda i: (0, 0)),
                  pl.BlockSpec(w_sp.shape, lambda i: (0, 0, 0)), pl.BlockSpec(bsp.shape, lambda i: (0, 0))],
        out_specs=[pl.BlockSpec((tm, 2 * SG_W), lambda i: (i, 0)), pl.BlockSpec((tm, SG_W), lambda i: (i, 0))],
        out_shape=[jax.ShapeDtypeStruct((m, 2 * SG_W), F32), jax.ShapeDtypeStruct((m, SG_W), BF16)],
        compiler_params=_params())(xm, w_sgp, w_sp, bsp)


def _sg_bwd(name, p_sg, dcat, w_sp, w_sp_t, bsp, tm):
    def body(ri, vi, ro, vo):
        p = ri[0][...]
        dsg = ri[1][...]
        su = p[:, :SG_W]
        sv = p[:, SG_W:]
        gu, tu = _gelu(su)
        gv, tv = _gelu(sv)
        for g in range(N_SG):
            sl = slice(g * CHUNK, (g + 1) * CHUNK)
            vn, r = _ln_stats(gv[:, sl])
            vnb = vn.astype(BF16)
            for m in range(tm // CHUNK):
                rs = slice(m * CHUNK, (m + 1) * CHUNK)
                mixed = jnp.dot(vi[0][g], vnb[rs], preferred_element_type=F32) + vi[2][:, sl]
                d_o = dsg[rs, sl]
                dm = d_o * gu[rs, sl]
                vo[1][:, sl] += dm
                dmb = dm.astype(BF16)
                vo[0][g] += lax.dot_general(dmb, vnb[rs], (((1,), (1,)), ((), ())), preferred_element_type=F32)
                dvn = jnp.dot(vi[1][g], dmb, preferred_element_type=F32)
                dgv = _ln_bwd(vn[rs], r[rs], dvn)
                ro[0][rs, sl] = (d_o * mixed * _gelu_grad(su[rs, sl], tu[rs, sl])).astype(BF16)
                ro[0][rs, SG_W + g * CHUNK:SG_W + (g + 1) * CHUNK] = (
                    dgv * _gelu_grad(sv[rs, sl], tv[rs, sl])).astype(BF16)

    return _rowwise(name, body, p_sg.shape[0] // tm, tm, [_ri(p_sg), _ri(dcat, SG_W, 1)], [w_sp, w_sp_t, bsp],
                    [(2 * SG_W, BF16)], [(N_SG, CHUNK, CHUNK), (CHUNK, SG_W)])


def _glu_bwd(name, ag, dhg, b, tm):
    def body(ri, vi, ro, vo):
        v = ri[0][...] + vi[0][...]
        d = ri[1][...]
        a = v[:, :D_MODEL]
        sg = _sigmoid(v[:, D_MODEL:])
        da = d * sg
        dgate = d * a * sg * (1.0 - sg)
        ro[0][:, :D_MODEL] = da.astype(BF16)
        ro[0][:, D_MODEL:] = dgate.astype(BF16)
        vo[0][:, :D_MODEL] += _colsum(da)
        vo[0][:, D_MODEL:] += _colsum(dgate)

    return _rowwise(name, body, ag.shape[0] // tm, tm, [_ri(ag), _ri(dhg)], [b], [(2 * D_MODEL, BF16)],
                    [(1, 2 * D_MODEL)])


def _lnsilu_fwd(name, hc, g, b, tm):
    def body(ri, vi, ro, vo):
        xh, _ = _ln_stats(ri[0][...])
        ln = xh * vi[0][...] + vi[1][...]
        ro[0][...] = (ln * _sigmoid(ln)).astype(BF16)

    return _rowwise(name, body, hc.shape[0] // tm, tm, [_ri(hc)], [g, b], [(D_MODEL, BF16)], [])[0]


def _conv_blocks(seq):
    cb = 128
    tt = 128 if seq % 128 == 0 else seq
    return cb, tt


def _conv_taps(win, tt):
    n = win.shape[0]
    for s in range(8):
        ws = win if s == 0 else pltpu.roll(win, n - s, 0)
        for q in range(4):
            j = 8 * q + s - 1
            if 0 <= j < CONV_W:
                yield j, ws[8 * q:8 * q + tt, :]


def _fill_padded(pad_ref, x_ref, seq):
    zeros = jnp.zeros((CONV_HALO, pad_ref.shape[1]), F32)
    pad_ref[0:CONV_HALO, :] = zeros
    pad_ref[seq + CONV_HALO:seq + 2 * CONV_HALO, :] = zeros
    pad_ref[CONV_HALO:seq + CONV_HALO, :] = x_ref[...]


def _dwconv(name, xin, w, b):
    seq = xin.shape[0]
    cb, tt = _conv_blocks(seq)

    def kern(x_ref, w_ref, b_ref, o_ref, pad_ref):
        _fill_padded(pad_ref, x_ref, seq)
        wv = w_ref[...]
        bv = b_ref[...]

        def step(t, carry):
            base = pl.multiple_of(t * tt, tt)
            win = pad_ref[pl.ds(base, tt + 2 * CONV_HALO), :]
            acc = jnp.zeros((tt, cb), F32) + bv
            for j, rows in _conv_taps(win, tt):
                acc = acc + wv[j:j + 1, :] * rows
            o_ref[pl.ds(base, tt), :] = acc
            return carry

        lax.fori_loop(0, seq // tt, step, 0)

    return pl.pallas_call(
        kern, name=name, grid=(D_MODEL // cb,),
        in_specs=[pl.BlockSpec((seq, cb), lambda j: (0, j)), pl.BlockSpec((CONV_W, cb), lambda j: (0, j)),
                  pl.BlockSpec((1, cb), lambda j: (0, j))],
        out_specs=pl.BlockSpec((seq, cb), lambda j: (0, j)),
        out_shape=jax.ShapeDtypeStruct((seq, D_MODEL), F32),
        scratch_shapes=[pltpu.VMEM((seq + 2 * CONV_HALO, cb), F32)],
        compiler_params=_params())(xin, w, b)


def _dwconv_wgrad(name, xin, dout):
    seq = xin.shape[0]
    cb, tt = _conv_blocks(seq)

    def kern(x_ref, d_ref, dw_ref, db_ref, pad_ref):
        _fill_padded(pad_ref, x_ref, seq)
        dw_ref[...] = jnp.zeros(dw_ref.shape, F32)
        db_ref[...] = jnp.zeros(db_ref.shape, F32)

        def step(t, carry):
            base = pl.multiple_of(t * tt, tt)
            win = pad_ref[pl.ds(base, tt + 2 * CONV_HALO), :]
            d = d_ref[pl.ds(base, tt), :]
            db_ref[...] += _colsum(d)
            for j, rows in _conv_taps(win, tt):
                dw_ref[j:j + 1, :] += _colsum(d * rows)
            return carry

        lax.fori_loop(0, seq // tt, step, 0)

    return pl.pallas_call(
        kern, name=name, grid=(D_MODEL // cb,),
        in_specs=[pl.BlockSpec((seq, cb), lambda j: (0, j)), pl.BlockSpec((seq, cb), lambda j: (0, j))],
        out_specs=[pl.BlockSpec((CONV_W, cb), lambda j: (0, j)), pl.BlockSpec((1, cb), lambda j: (0, j))],
        out_shape=[jax.ShapeDtypeStruct((CONV_W, D_MODEL), F32), jax.ShapeDtypeStruct((1, D_MODEL), F32)],
        scratch_shapes=[pltpu.VMEM((seq + 2 * CONV_HALO, cb), F32)],
        compiler_params=_params())(xin, dout)


def _mm(name, a, b, out_sds, grid, a_spec, b_spec, o_spec, contract, k_axis=None, bias=None, pair_send=()):
    dn = (contract, ((), ()))
    ns = len(pair_send)
    nb = 0 if bias is None else 1
    n_steps = math.prod(grid)

    def kern(*refs):
        a_ref, b_ref = refs[:2]
        bias_ref = refs[2] if nb else None
        o_ref = refs[2 + nb + ns]
        if ns:
            copies = _pair_send_copies(refs[2 + nb:2 + nb + ns], refs[3 + nb + ns:3 + nb + 2 * ns],
                                       *refs[3 + nb + 2 * ns:])
            step = pl.program_id(0)
            for ax in range(1, len(grid)):
                step = step * grid[ax] + pl.program_id(ax)

            @pl.when(step == 0)
            def _():
                for cp in copies():
                    cp.start()

        p = lax.dot_general(a_ref[...].astype(BF16), b_ref[...].astype(BF16), dn, preferred_element_type=F32)
        if bias is not None:
            p = p + bias_ref[...]
        if k_axis is None:
            o_ref[...] = p.astype(o_ref.dtype)
        else:
            k = pl.program_id(k_axis)

            @pl.when(k == 0)
            def _():
                o_ref[...] = p

            @pl.when(k > 0)
            def _():
                o_ref[...] += p

        if ns:
            @pl.when(step == n_steps - 1)
            def _():
                for cp in copies():
                    cp.wait()

    in_specs = [a_spec, b_spec]
    args = [a, b]
    if bias is not None:
        in_specs.append(pl.BlockSpec(bias.shape, lambda *_: (0,) * bias.ndim))
        args.append(bias)
    if not ns:
        return pl.pallas_call(kern, name=name, grid=grid, in_specs=in_specs, out_specs=o_spec, out_shape=out_sds,
                              compiler_params=_params())(*args)
    return pl.pallas_call(
        kern, name=name, grid=grid, in_specs=in_specs + [_ANY] * ns, out_specs=[o_spec] + [_ANY] * ns,
        out_shape=[out_sds] + [jax.ShapeDtypeStruct((N_CHIP,) + g.shape[2:], g.dtype) for g in pair_send],
        scratch_shapes=[pltpu.SemaphoreType.DMA((ns, N_CHIP)), pltpu.SemaphoreType.DMA((ns, N_CHIP))],
        compiler_params=_params())(*args, *pair_send)


def _mm_rows(name, a, b, tm, trans_b=False, bias=None):
    m, k = a.shape
    n = b.shape[0] if trans_b else b.shape[1]
    contract = ((1,), (1,)) if trans_b else ((1,), (0,))
    return _mm(name, a, b, jax.ShapeDtypeStruct((m, n), F32), (m // tm,),
               pl.BlockSpec((tm, k), lambda i: (i, 0)), pl.BlockSpec(b.shape, lambda i: (0, 0)),
               pl.BlockSpec((tm, n), lambda i: (i, 0)), contract, bias=bias)


def _mm_glu(name, a, b, bias, tm):
    m, k = a.shape
    n = b.shape[1]

    def kern(a_ref, b_ref, bias_ref, ag_ref, hg_ref):
        v = jnp.dot(a_ref[...], b_ref[...], preferred_element_type=F32)
        ag_ref[...] = v
        v = v + bias_ref[...]
        hg_ref[...] = v[:, :n // 2] * _sigmoid(v[:, n // 2:])

    return pl.pallas_call(
        kern, name=name, grid=(m // tm,),
        in_specs=[pl.BlockSpec((tm, k), lambda i: (i, 0)), pl.BlockSpec(b.shape, lambda i: (0, 0)),
                  pl.BlockSpec((1, n), lambda i: (0, 0))],
        out_specs=[pl.BlockSpec((tm, n), lambda i: (i, 0)), pl.BlockSpec((tm, n // 2), lambda i: (i, 0))],
        out_shape=[jax.ShapeDtypeStruct((m, n), F32), jax.ShapeDtypeStruct((m, n // 2), F32)],
        compiler_params=_params())(a, b, bias)


def _mm_rows_res(name, a, b, h, gt, tm, bias=None, norm=None, final=None):
    m, k = a.shape
    n = b.shape[1]
    nb = 0 if bias is None else 1
    extra = list(norm or ()) + list(final or ())

    def kern(a_ref, b_ref, h_ref, gt_ref, *rest):
        y = jnp.dot(a_ref[...], b_ref[...], preferred_element_type=F32)
        if nb:
            y = y + rest[0][...]
        ex = rest[nb:nb + len(extra)]
        outs = rest[nb + len(extra):]
        hn = h_ref[...] + gt_ref[...] * y
        if final is None:
            outs[0][...] = y
            outs[1][...] = hn
        if norm is not None:
            nn, _ = _rms(hn)
            outs[2][...] = ((nn * ex[0][...]) * (1.0 + ex[1][...]) + ex[2][...]).astype(BF16)
        if final is not None:
            @pl.when(pl.program_id(0) == 0)
            def _():
                for o in (outs[1], outs[2], outs[4]):
                    o[...] = jnp.zeros(o.shape, F32)

            gv = ex[1][...]
            nn, r = _rms(hn)
            e = nn * gv - ex[0][...]
            outs[2][...] += _colsum(e * e) * (0.5 / D_MODEL)
            dout = e * (1.0 / D_MODEL)
            outs[1][...] += _colsum(dout * nn)
            dh = _rms_bwd(nn, r, dout * gv)
            outs[0][...] = dh
            outs[3][...] = (dh * gt_ref[...]).astype(BF16)
            outs[4][...] += _colsum(dh * y)

    row = pl.BlockSpec((tm, n), lambda i: (i, 0))
    vec = pl.BlockSpec((1, n), lambda i: (0, 0))
    sds = jax.ShapeDtypeStruct((m, n), F32)
    vsd = jax.ShapeDtypeStruct((1, n), F32)
    in_specs = [pl.BlockSpec((tm, k), lambda i: (i, 0)), pl.BlockSpec(b.shape, lambda i: (0, 0)), row, vec] + [vec] * nb
    out_specs, out_shape = [row, row], [sds, sds]
    if norm is not None:
        in_specs += [vec] * 3
        out_specs.append(row)
        out_shape.append(jax.ShapeDtypeStruct((m, n), BF16))
    if final is not None:
        in_specs += [row, vec]
        out_specs = [row, vec, vec, row, vec]
        out_shape = [sds, vsd, vsd, jax.ShapeDtypeStruct((m, n), BF16), vsd]
    return pl.pallas_call(kern, name=name, grid=(m // tm,), in_specs=in_specs, out_specs=out_specs,
                          out_shape=out_shape, compiler_params=_params())(
        a, b, h, gt, *([bias] if nb else []), *extra)


def _ffn_in_swiglu(name, xf, wblk, tm, gather=()):
    m, k = xf.shape
    nblk = wblk.shape[2]
    ng = len(gather)
    n_i = m // tm
    last = 2 * n_i - 1

    def kern(x_ref, wg_ref, wu_ref, *rest):
        gu_ref, a_ref = rest[ng:ng + 2]
        if ng:
            start, forward, finish = _gather_phases(rest[:ng], rest[ng + 2:2 * ng + 2], *rest[2 * ng + 2:])
            step = pl.program_id(0) * n_i + pl.program_id(1)
            pl.when(step == 0)(start)
            pl.when(step == last)(forward)
        xv = x_ref[...]
        g = jnp.dot(xv, wg_ref[...], preferred_element_type=F32)
        u = jnp.dot(xv, wu_ref[...], preferred_element_type=F32)
        gu_ref[0] = g.astype(BF16)
        gu_ref[1] = u.astype(BF16)
        a_ref[...] = (g * _sigmoid(g) * u).astype(BF16)
        if ng:
            pl.when(step == last)(finish)

    return pl.pallas_call(
        kern, name=name, grid=(2, n_i),
        in_specs=[pl.BlockSpec((tm, k), lambda j, i: (i, 0)), pl.BlockSpec((None, k, nblk), lambda j, i: (j, 0, 0)),
                  pl.BlockSpec((None, k, nblk), lambda j, i: (j + 2, 0, 0))] + [_ANY] * ng,
        out_specs=[pl.BlockSpec((2, tm, nblk), lambda j, i: (0, i, j)),
                   pl.BlockSpec((tm, nblk), lambda j, i: (i, j))] + [_ANY] * ng,
        out_shape=[jax.ShapeDtypeStruct((2, m, 2 * nblk), BF16), jax.ShapeDtypeStruct((m, 2 * nblk), BF16)]
        + [jax.ShapeDtypeStruct((N_DEV,) + g.shape, g.dtype) for g in gather],
        scratch_shapes=_gather_scratch(ng) if ng else [],
        compiler_params=_params())(xf, wblk, wblk, *gather)


def _ffn_out_dx_swiglu(name, dy, wo, gu, tm):
    m, k = dy.shape
    nblk = gu.shape[2] // 2

    def kern(dy_ref, w_ref, gu_ref, o_ref):
        da = lax.dot_general(dy_ref[...], w_ref[...], (((1,), (1,)), ((), ())), preferred_element_type=F32)
        g = gu_ref[0].astype(F32)
        sg = _sigmoid(g)
        o_ref[0] = (da * gu_ref[1].astype(F32) * (sg * (1.0 + g * (1.0 - sg)))).astype(BF16)
        o_ref[1] = (da * (g * sg)).astype(BF16)

    return pl.pallas_call(
        kern, name=name, grid=(2, m // tm),
        in_specs=[pl.BlockSpec((tm, k), lambda j, i: (i, 0)), pl.BlockSpec((nblk, k), lambda j, i: (j, 0)),
                  pl.BlockSpec((2, tm, nblk), lambda j, i: (0, i, j))],
        out_specs=pl.BlockSpec((2, tm, nblk), lambda j, i: (0, i, j)),
        out_shape=jax.ShapeDtypeStruct(gu.shape, BF16), compiler_params=_params())(dy, wo, gu)


def _ffn_in_dx_norm_bwd(name, dgu, wblk, h, dh_in, g, sc, tm, gate=None):
    nb, k, nblk = wblk.shape

    def dx_fn(a_ref, w_ref):
        acc = None
        for j in range(nb):
            p = lax.dot_general(a_ref[j // 2, :, (j % 2) * nblk:(j % 2 + 1) * nblk], w_ref[j],
                                (((1,), (1,)), ((), ())), preferred_element_type=F32)
            acc = p if acc is None else acc + p
        return acc

    return _mm_row_epilogue(name, [dgu, wblk],
                            [pl.BlockSpec((2, tm, 2 * nblk), lambda i: (0, i, 0)),
                             pl.BlockSpec(wblk.shape, lambda i: (0, 0, 0))], dx_fn, [(h, None), (dh_in, None)],
                            [g, sc], 3, _norm_bwd_epilogue, tm, gate=gate)


def _mm_t_epilogue(name, a, b, rows_in, vecs_in, n_vec_out, epilogue, tm, gate=None, exchange=()):
    def dx_fn(a_ref, b_ref):
        return lax.dot_general(a_ref[...], b_ref[...], (((1,), (1,)), ((), ())), preferred_element_type=F32)

    return _mm_row_epilogue(name, [a, b], [pl.BlockSpec((tm, a.shape[1]), lambda i: (i, 0)),
                                          pl.BlockSpec(b.shape, lambda i: (0, 0))], dx_fn, rows_in, vecs_in,
                            n_vec_out, epilogue, tm, gate, exchange)


def _mm_halves_wgrad_blk(name, a, b2, tk, pair_send=()):
    t, k = a.shape
    nblk = b2.shape[2] // 2
    return _mm(name, a, b2, jax.ShapeDtypeStruct((N_CHIP, k, nblk), F32), (N_CHIP, t // tk),
               pl.BlockSpec((tk, k), lambda j, i: (i, 0)),
               pl.BlockSpec((None, tk, nblk), lambda j, i: (j // 2, i, j % 2)),
               pl.BlockSpec((None, k, nblk), lambda j, i: (j, 0, 0)), ((0,), (0,)), k_axis=1, pair_send=pair_send)


def _mm_wgrad(name, a, b, tk, pair_send=()):
    t, k = a.shape
    n = b.shape[1]
    tn = n if k * n * 4 <= 4 * 1024 * 1024 else 512
    return _mm(name, a, b, jax.ShapeDtypeStruct((k, n), F32), (n // tn, t // tk),
               pl.BlockSpec((tk, k), lambda j, i: (i, 0)), pl.BlockSpec((tk, tn), lambda j, i: (i, j)),
               pl.BlockSpec((k, tn), lambda j, i: (0, j)), ((0,), (0,)), k_axis=1, pair_send=pair_send)


def _attn_fwd(name, q_all, k, v, n_ctx, tq, gather=()):
    r = k.shape[1]
    s = q_all.shape[0] - n_ctx
    gw = Q_GROUP * HEAD_DIM
    ng = len(gather)
    n_i = s // tq
    last = N_KV_HEADS * n_i - 1
    forward_at = (7 * last) // 8

    def kern(q_ref, k_ref, v_ref, *rest):
        o_ref, lse_ref = rest[ng], rest[ng + 1]
        if ng:
            start, forward, finish = _gather_phases(rest[:ng], rest[ng + 2:2 * ng + 2], *rest[2 * ng + 2:])
            step = pl.program_id(0) * n_i + pl.program_id(1)
            pl.when(step == 0)(start)
            pl.when(step == forward_at)(forward)
        kv = k_ref[...]
        vv = v_ref[...]
        for g in range(Q_GROUP):
            sl = slice(g * HEAD_DIM, (g + 1) * HEAD_DIM)
            sc = lax.dot_general(q_ref[:, sl], kv, (((1,), (1,)), ((), ())), preferred_element_type=F32)
            m = jnp.max(sc, axis=-1, keepdims=True)
            p = jnp.exp((sc - m).astype(BF16))
            ol = jnp.dot(p, vv, preferred_element_type=F32)
            l = ol[:, HEAD_DIM:HEAD_DIM + 1]
            o_ref[:, sl] = ol[:, :HEAD_DIM] / l
            lse_ref[g] = m + jnp.log(l)
        if ng:
            pl.when(step == last)(finish)

    return pl.pallas_call(
        kern, name=name, grid=(N_KV_HEADS, n_i),
        in_specs=[pl.BlockSpec((tq, gw), lambda j, i: (i + n_ctx // tq, j)),
                  pl.BlockSpec((None, r, HEAD_DIM), lambda j, i: (j, 0, 0)),
                  pl.BlockSpec((None, r, 2 * HEAD_DIM), lambda j, i: (j, 0, 0))] + [_ANY] * ng,
        out_specs=[pl.BlockSpec((tq, gw), lambda j, i: (i, j)),
                   pl.BlockSpec((Q_GROUP, tq, 1), lambda j, i: (j, i, 0))] + [_ANY] * ng,
        out_shape=[jax.ShapeDtypeStruct((s, ATTN_W), F32), jax.ShapeDtypeStruct((N_Q_HEADS, s, 1), F32)]
        + [jax.ShapeDtypeStruct((N_DEV,) + a.shape, a.dtype) for a in gather],
        scratch_shapes=_gather_scratch(ng) if ng else [],
        compiler_params=_params())(q_all, k, v, *gather)


def _attn_bwd(name, q_all, k, v, o, lse, dcat, n_ctx, tq, exchange=()):
    nkv, r, _ = k.shape
    s = o.shape[0]
    gw = Q_GROUP * HEAD_DIM
    ne = len(exchange)
    n_i = s // tq
    last = nkv * n_i - 1

    def kern(q_ref, k_ref, v_ref, o_ref, lse_ref, do_ref, *rest):
        dq_ref, dk_ref, dv_ref = rest[ne:ne + 3]
        ds_all, p_all = rest[2 * ne + 3:2 * ne + 5]
        if ne:
            start, finish = _exchange_phases(rest[:ne], rest[ne + 3:2 * ne + 3], *rest[2 * ne + 5:])
            step = pl.program_id(0) * n_i + pl.program_id(1)
            pl.when(step == 0)(start)

        @pl.when(pl.program_id(1) == 0)
        def _():
            dk_ref[...] = jnp.zeros(dk_ref.shape, F32)
            dv_ref[...] = jnp.zeros(dv_ref.shape, F32)

        kv = k_ref[...]
        vv = v_ref[...]
        qs = []
        dos = []
        for g in range(Q_GROUP):
            sl = slice(g * HEAD_DIM, (g + 1) * HEAD_DIM)
            rows = slice(g * tq, (g + 1) * tq)
            qv = q_ref[:, sl]
            dov = do_ref[:, sl]
            delta = jnp.sum(dov * o_ref[:, sl], axis=-1, keepdims=True)
            sc = lax.dot_general(qv, kv, (((1,), (1,)), ((), ())), preferred_element_type=F32)
            p = jnp.exp(sc - lse_ref[g])
            dob = dov.astype(BF16)
            dp = lax.dot_general(dob, vv, (((1,), (1,)), ((), ())), preferred_element_type=F32)
            dsb = (p * (dp - delta)).astype(BF16)
            dq_ref[:, sl] = jnp.dot(dsb, kv, preferred_element_type=F32)
            ds_all[rows, :] = dsb
            p_all[rows, :] = p.astype(BF16)
            qs.append(qv)
            dos.append(dob)
        dk_ref[...] += lax.dot_general(jnp.concatenate(qs, axis=0), ds_all[...], (((0,), (0,)), ((), ())),
                                       preferred_element_type=F32)
        dv_ref[...] += lax.dot_general(jnp.concatenate(dos, axis=0), p_all[...], (((0,), (0,)), ((), ())),
                                       preferred_element_type=F32)
        if ne:
            pl.when(step == last)(finish)

    kvmap = lambda j, i: (j, 0, 0)
    return pl.pallas_call(
        kern, name=name, grid=(nkv, n_i),
        in_specs=[pl.BlockSpec((tq, gw), lambda j, i: (i + n_ctx // tq, j)), pl.BlockSpec((None, r, HEAD_DIM), kvmap),
                  pl.BlockSpec((None, r, HEAD_DIM), kvmap), pl.BlockSpec((tq, gw), lambda j, i: (i, j)),
                  pl.BlockSpec((Q_GROUP, tq, 1), lambda j, i: (j, i, 0)),
                  pl.BlockSpec((tq, gw), lambda j, i: (i, j))] + [_ANY] * ne,
        out_specs=[pl.BlockSpec((tq, gw), lambda j, i: (i, j)), pl.BlockSpec((None, HEAD_DIM, r), kvmap),
                   pl.BlockSpec((None, HEAD_DIM, r), kvmap)] + [_ANY] * ne,
        out_shape=[jax.ShapeDtypeStruct((s, ATTN_W), F32), jax.ShapeDtypeStruct((nkv, HEAD_DIM, r), F32),
                   jax.ShapeDtypeStruct((nkv, HEAD_DIM, r), F32)]
        + [jax.ShapeDtypeStruct(a.shape, a.dtype) for a in exchange],
        scratch_shapes=[pltpu.VMEM((Q_GROUP * tq, r), BF16)] * 2 + (_exchange_scratch(ne) if ne else []),
        compiler_params=_params())(q_all, k, v, o, lse, dcat, *exchange)


def _mod_fwd(name, c16, wm, bm):
    n = wm.shape[2]

    def kern(c_ref, w_ref, b_ref, o_ref):
        cv = c_ref[...]
        a = (cv * _sigmoid(cv)).astype(BF16)
        o_ref[...] = jnp.dot(a, w_ref[...].astype(BF16), preferred_element_type=F32) + b_ref[...]

    return pl.pallas_call(
        kern, name=name, grid=(2,),
        in_specs=[pl.BlockSpec(c16.shape, lambda l: (0, 0)), pl.BlockSpec((None, D_MODEL, n), lambda l: (l, 0, 0)),
                  pl.BlockSpec((None, 1, n), lambda l: (l, 0, 0))],
        out_specs=pl.BlockSpec((None, 16, n), lambda l: (l, 0, 0)),
        out_shape=jax.ShapeDtypeStruct((2, 16, n), F32), compiler_params=_params())(c16, wm, bm)


def _mod_wgrad(name, c16, dm):
    n = dm.shape[2]
    tn = 512

    def kern(c_ref, d_ref, o_ref):
        cv = c_ref[...]
        a = cv * _sigmoid(cv)
        o_ref[...] = lax.dot_general(a, d_ref[...], (((0,), (0,)), ((), ())), preferred_element_type=F32,
                                     precision=lax.Precision.HIGHEST)

    return pl.pallas_call(
        kern, name=name, grid=(2, n // tn),
        in_specs=[pl.BlockSpec(c16.shape, lambda l, j: (0, 0)), pl.BlockSpec((None, 16, tn), lambda l, j: (l, 0, j))],
        out_specs=pl.BlockSpec((None, D_MODEL, tn), lambda l, j: (l, 0, j)),
        out_shape=jax.ShapeDtypeStruct((2, D_MODEL, n), F32), compiler_params=_params())(c16, dm)


def _cctx_grad(name, parts, c_ctx):
    def kern(p_ref, c_ref, o_ref):
        d = p_ref[0, 0:1, :] + p_ref[2, 0:1, :] + p_ref[4, 0:1, :] + p_ref[6, 0:1, :]
        cv = c_ref[...]
        sg = _sigmoid(cv)
        o_ref[...] = d * (sg * (1.0 + cv * (1.0 - sg)))

    return pl.pallas_call(kern, name=name, out_shape=jax.ShapeDtypeStruct((1, D_MODEL), F32),
                          compiler_params=_params())(parts, c_ctx)


def _sum_slots(name, g, tr):
    n, rows, cols = g.shape

    def kern(g_ref, o_ref):
        acc = g_ref[0].astype(F32)
        for i in range(1, n):
            acc = acc + g_ref[i].astype(F32)
        o_ref[...] = acc

    return pl.pallas_call(kern, name=name, grid=(rows // tr,),
                          in_specs=[pl.BlockSpec((n, tr, cols), lambda i: (0, i, 0))],
                          out_specs=pl.BlockSpec((tr, cols), lambda i: (i, 0)),
                          out_shape=jax.ShapeDtypeStruct((rows, cols), F32), compiler_params=_params())(g)


def _sum_into(name, g, cvec, n_layers, layer, prev=None):
    n, r, cc = g.shape
    tr = _row_tile(r, cc, 512 * 1024)

    def kern(c_ref, g_ref, *rest):
        acc = g_ref[0].astype(F32)
        for i in range(1, n):
            acc = acc + g_ref[i].astype(F32)
        rest[-1][...] = acc

    in_specs = [pl.BlockSpec((n, tr, cc), lambda i, c_ref: (0, i, 0))]
    args = [cvec, g]
    aliases = {}
    if prev is not None:
        in_specs.append(_ANY)
        args.append(prev)
        aliases = {2: 0}
    grid_spec = pltpu.PrefetchScalarGridSpec(
        num_scalar_prefetch=1, grid=(r // tr,), in_specs=in_specs,
        out_specs=pl.BlockSpec((None, None, tr, cc), lambda i, c_ref: (layer, c_ref[0], i, 0)))
    return pl.pallas_call(kern, name=name, grid_spec=grid_spec,
                          out_shape=jax.ShapeDtypeStruct((n_layers, 2, r, cc), F32),
                          input_output_aliases=aliases, compiler_params=_params())(*args)


def _row_tile(rows, cols, max_bytes=1024 * 1024):
    if rows * cols * 4 <= 2 * max_bytes:
        return rows
    best = None
    for t in range(16, rows + 1, 16):
        if rows % t == 0 and t * cols * 4 <= max_bytes:
            best = t
    assert best is not None, (rows, cols)
    return best


def _adamw(name, w, g, m, v):
    shape = w.shape
    cols = shape[-1]
    rows = w.size // cols
    tr = _row_tile(rows, cols)

    def kern(w_ref, g_ref, m_ref, v_ref, d_ref, nm_ref, nv_ref):
        gv = g_ref[...]
        m2 = ADAM_B1 * m_ref[...] + (1.0 - ADAM_B1) * gv
        v2 = ADAM_B2 * v_ref[...] + (1.0 - ADAM_B2) * (gv * gv)
        m_hat = m2 / (1.0 - ADAM_B1 ** ADAM_STEP)
        v_hat = v2 / (1.0 - ADAM_B2 ** ADAM_STEP)
        d_ref[...] = -ADAM_LR * (m_hat / (jnp.sqrt(v_hat) + ADAM_EPS) + ADAM_WD * w_ref[...])
        nm_ref[...] = m2
        nv_ref[...] = v2

    spec = pl.BlockSpec((tr, cols), lambda i: (i, 0))
    sds = jax.ShapeDtypeStruct((rows, cols), F32)
    outs = pl.pallas_call(kern, name=name, grid=(rows // tr,), in_specs=[spec] * 4, out_specs=[spec] * 3,
                          out_shape=[sds] * 3, compiler_params=_params())(
        w.reshape(rows, cols), g.reshape(rows, cols), m.reshape(rows, cols), v.reshape(rows, cols))
    return tuple(o.reshape(shape) for o in outs)


_ANY = pl.BlockSpec(memory_space=pl.ANY)


def _mesh_pos():
    return lax.axis_index("x"), lax.axis_index("y"), lax.axis_index("c")


def _gather_phases(srcs, outs, send_sems, recv_sems, local_sems):
    n = len(srcs)
    x, y, c = _mesh_pos()
    me = (x, y, c)
    sibling = (x, y, 1 - c)
    chips = [(1 - x, y), (x, 1 - y), (1 - x, 1 - y)]

    def slot(px, py, pc):
        return 4 * px + 2 * py + pc

    def copy(t, k, s, to, src=None):
        dst = outs[t].at[s]
        return pltpu.make_async_remote_copy(src_ref=dst if src is None else src, dst_ref=dst,
                                            send_sem=send_sems.at[t, k], recv_sem=recv_sems.at[t, k],
                                            device_id=to, device_id_type=MESH)

    def mine(t):
        return pltpu.make_async_copy(srcs[t], outs[t].at[slot(*me)], local_sems.at[t])

    def first(t):
        return [copy(t, 0, slot(*me), sibling, src=srcs[t])] + [
            copy(t, 1 + j, slot(*me), (px, py, c), src=srcs[t]) for j, (px, py) in enumerate(chips)]

    def passed(t, j):
        px, py = chips[j]
        return copy(t, 4 + j, slot(px, py, c), sibling)

    def start():
        for t in range(n):
            mine(t).start()
        for t in range(n):
            for cp in first(t):
                cp.start()

    def forward():
        for j, (px, py) in enumerate(chips):
            for t in range(n):
                copy(t, 1 + j, slot(px, py, c), me).wait_recv()
                passed(t, j).start()

    def finish():
        for t in range(n):
            copy(t, 0, slot(x, y, 1 - c), me).wait_recv()
            for j, (px, py) in enumerate(chips):
                copy(t, 4 + j, slot(px, py, 1 - c), me).wait_recv()
        for t in range(n):
            for cp in first(t) + [passed(t, j) for j in range(len(chips))]:
                cp.wait_send()
            mine(t).wait()

    return start, forward, finish


def _gather_scratch(n):
    return [pltpu.SemaphoreType.DMA((n, 7)), pltpu.SemaphoreType.DMA((n, 7)), pltpu.SemaphoreType.DMA((n,))]


def _allgather(name, items):
    n = len(items)

    def body(*refs):
        start, forward, finish = _gather_phases(refs[:n], refs[n:2 * n], *refs[2 * n:])
        start()
        forward()
        finish()

    return pl.pallas_call(
        body, name=name, in_specs=[_ANY] * n, out_specs=[_ANY] * n,
        out_shape=[jax.ShapeDtypeStruct((N_DEV,) + a.shape, a.dtype) for a in items],
        scratch_shapes=_gather_scratch(n), compiler_params=_params())(*items)


def _pair_send_copies(srcs, bufs, send_sems, recv_sems):
    x, y, c = _mesh_pos()

    def copies():
        return [pltpu.make_async_remote_copy(
            src_ref=srcs[t].at[k, 1 - c], dst_ref=bufs[t].at[k], send_sem=send_sems.at[t, k],
            recv_sem=recv_sems.at[t, k], device_id=(x, y, 1 - c), device_id_type=MESH)
            for t in range(len(srcs)) for k in range(N_CHIP)]

    return copies


def _rs_pair_send(name, gs):
    n = len(gs)

    def body(*refs):
        copies = _pair_send_copies(refs[:n], refs[n:2 * n], *refs[2 * n:])
        for cp in copies():
            cp.start()
        for cp in copies():
            cp.wait()

    return pl.pallas_call(
        body, name=name, in_specs=[_ANY] * n, out_specs=[_ANY] * n,
        out_shape=[jax.ShapeDtypeStruct((N_CHIP,) + g.shape[2:], g.dtype) for g in gs],
        scratch_shapes=[pltpu.SemaphoreType.DMA((n, N_CHIP)), pltpu.SemaphoreType.DMA((n, N_CHIP))],
        compiler_params=_params())(*gs)


def _rs_pair_add(name, g, buf, cvec):
    _, _, r, cc = g.shape
    tr = _row_tile(r, cc, 2 * 1024 * 1024)

    def kern(c_ref, g_ref, b_ref, o_ref):
        o_ref[...] = (g_ref[...] + b_ref[...]).astype(BF16)

    grid_spec = pltpu.PrefetchScalarGridSpec(
        num_scalar_prefetch=1, grid=(N_CHIP, r // tr),
        in_specs=[pl.BlockSpec((None, None, tr, cc), lambda k, i, c_ref: (k, c_ref[0], i, 0)),
                  pl.BlockSpec((None, tr, cc), lambda k, i, c_ref: (k, i, 0))],
        out_specs=pl.BlockSpec((None, tr, cc), lambda k, i, c_ref: (k, i, 0)))
    return pl.pallas_call(kern, name=name, grid_spec=grid_spec,
                          out_shape=jax.ShapeDtypeStruct((N_CHIP, r, cc), BF16),
                          compiler_params=_params())(cvec, g, buf)


def _exchange_phases(srcs, bufs, send_sems, recv_sems, local_sems):
    n = len(srcs)
    x, y, c = _mesh_pos()
    kme = 2 * x + y
    chips = [(1 - x, y), (x, 1 - y), (1 - x, 1 - y)]

    def copies():
        local = [pltpu.make_async_copy(srcs[t].at[kme], bufs[t].at[kme], local_sems.at[t]) for t in range(n)]
        remote = []
        for t in range(n):
            for j, (px, py) in enumerate(chips):
                remote.append(pltpu.make_async_remote_copy(
                    src_ref=srcs[t].at[2 * px + py], dst_ref=bufs[t].at[kme], send_sem=send_sems.at[t, j],
                    recv_sem=recv_sems.at[t, j], device_id=(px, py, c), device_id_type=MESH))
        return local, remote

    def start():
        local, remote = copies()
        for cp in local + remote:
            cp.start()

    def finish():
        local, remote = copies()
        for cp in remote + local:
            cp.wait()

    return start, finish


def _exchange_scratch(n):
    return [pltpu.SemaphoreType.DMA((n, 3)), pltpu.SemaphoreType.DMA((n, 3)), pltpu.SemaphoreType.DMA((n,))]


def _rs_chip_exchange(name, ss):
    n = len(ss)

    def body(*refs):
        start, finish = _exchange_phases(refs[:n], refs[n:2 * n], *refs[2 * n:])
        start()
        finish()

    return pl.pallas_call(
        body, name=name, in_specs=[_ANY] * n, out_specs=[_ANY] * n,
        out_shape=[jax.ShapeDtypeStruct(s.shape, s.dtype) for s in ss],
        scratch_shapes=_exchange_scratch(n), compiler_params=_params())(*ss)


def _rs_pair_share(name, dsts, gather=()):
    nd = len(dsts)
    ng = len(gather)

    def body(*refs):
        ins = refs[:nd]
        outs = refs[nd + ng:2 * nd + ng]
        send_sems, recv_sems = refs[2 * (nd + ng):2 * (nd + ng) + 2]
        if ng:
            start, forward, finish = _gather_phases(refs[nd:nd + ng], refs[2 * nd + ng:2 * (nd + ng)],
                                                    *refs[2 * (nd + ng) + 2:])
            start()
        x, y, c = _mesh_pos()
        copies = []
        for d in range(nd):
            for l in range(dsts[d].shape[0]):
                copies.append(pltpu.make_async_remote_copy(
                    src_ref=ins[d].at[l, c], dst_ref=outs[d].at[l, c], send_sem=send_sems.at[d, l],
                    recv_sem=recv_sems.at[d, l], device_id=(x, y, 1 - c), device_id_type=MESH))
        for cp in copies:
            cp.start()
        if ng:
            forward()
        for cp in copies:
            cp.wait()
        if ng:
            finish()

    return pl.pallas_call(
        body, name=name, in_specs=[_ANY] * (nd + ng), out_specs=[_ANY] * (nd + ng),
        out_shape=[jax.ShapeDtypeStruct(a.shape, a.dtype) for a in dsts]
        + [jax.ShapeDtypeStruct((N_DEV,) + a.shape, a.dtype) for a in gather],
        input_output_aliases={d: d for d in range(nd)},
        scratch_shapes=[pltpu.SemaphoreType.DMA((nd, 2)), pltpu.SemaphoreType.DMA((nd, 2))]
        + (_gather_scratch(ng) if ng else []),
        compiler_params=_params())(*dsts, *gather)


def _rope_tables(seq, n_ctx):
    t = jnp.arange(seq)
    row = (t // GRID_W).astype(F32)
    col = (t % GRID_W).astype(F32)
    inv = ROPE_THETA ** (-jnp.arange(0, HEAD_DIM // 2, 2, dtype=F32) / (HEAD_DIM // 2))
    ang_r = row[:, None] * inv[None, :]
    ang_c = col[:, None] * inv[None, :]
    cos = jnp.concatenate([jnp.cos(ang_r)] * 2 + [jnp.cos(ang_c)] * 2, axis=1)
    sin = jnp.concatenate([-jnp.sin(ang_r), jnp.sin(ang_r), -jnp.sin(ang_c), jnp.sin(ang_c)], axis=1)
    cos = jnp.concatenate([jnp.ones((n_ctx, HEAD_DIM), F32), cos], axis=0)
    sin = jnp.concatenate([jnp.zeros((n_ctx, HEAD_DIM), F32), sin], axis=0)
    return jnp.tile(cos, (1, N_KV_HEADS)), jnp.tile(sin, (1, N_KV_HEADS))


def _to_heads(a, nh):
    return a.reshape(a.shape[0], nh, HEAD_DIM).transpose(1, 0, 2)


def _sample_step(x, ctx, tgt, mod, cmod, W, pending, cvec):
    W = dict(W)
    seq = x.shape[0]
    n_ctx = ctx.shape[0]
    tm = min(256, seq)
    tmc = min(256, n_ctx)
    tmm = 512 if seq % 512 == 0 else tm
    tmw = 1024 if seq % 1024 == 0 else tmm
    tmr = (n_ctx + seq) // 4 if (n_ctx + seq) % 64 == 0 else tm
    tmk = 2048 if seq % 2048 == 0 else tmw
    tq = min(256, seq)
    assert n_ctx % tm == 0 and seq % tm == 0

    def mv(l, j):
        return mod[l, j * D_MODEL:(j + 1) * D_MODEL].reshape(1, D_MODEL)

    csh1 = cmod[:D_MODEL].reshape(1, D_MODEL)
    csc1 = cmod[D_MODEL:2 * D_MODEL].reshape(1, D_MODEL)
    g_mix = [W['g_mix'][l:l + 1] for l in range(2)]
    g_ffn = [W['g_ffn'][l:l + 1] for l in range(2)]
    cos, sin = _rope_tables(seq, n_ctx)
    gq = jnp.tile(W['q_gain'], (1, N_Q_HEADS))
    gk = jnp.tile(W['k_gain'], (1, N_KV_HEADS))
    bdq = jnp.kron(jnp.eye(N_Q_HEADS, dtype=F32), jnp.ones((HEAD_DIM, HEAD_DIM), F32)).astype(BF16)
    bdk = bdq[:KV_W, :KV_W]
    w_sp = W['w_sp'].astype(BF16)
    w_sp_t = w_sp.transpose(0, 2, 1)
    bsp = jnp.broadcast_to(W['b_sp'].T[:, :, None], (CHUNK, N_SG, CHUNK)).reshape(CHUNK, SG_W)

    def ffn_fwd(l, h_mid, xf, gather=(), **tail):
        gu, a, *gathered = _ffn_in_swiglu(f'l{l}_ffn_in', xf, W['ffn_in'][l], tmw, gather=gather)
        if gathered:
            W['pw2'] = gathered[0].reshape(D_MODEL, D_MODEL)
            W['ffn_out'][1] = gathered[1].reshape(D_FF, D_MODEL)
            W['pw1'] = gathered[2].reshape(N_CHIP, D_MODEL, PW1_BLK).transpose(1, 0, 2).reshape(D_MODEL, 2 * D_MODEL)
        outs = _mm_rows_res(f'l{l}_ffn_out', a, W['ffn_out'][l], h_mid, mv(l, 5), tmm, **tail)
        return outs, (h_mid, xf, gu, a)

    def blocked(a):
        cols = a.shape[-1]
        return a.reshape(N_CHIP, 2, a.size // (2 * N_CHIP * cols), cols)

    def ffn_bwd(l, dh_out, dy, y_mix, saved, pair_send=()):
        h_mid, xf, gu, a = saved
        dgu = _ffn_out_dx_swiglu(f'l{l}_ffn_out_dx', dy, W['ffn_out'][l], gu, tmw)
        d_wo = _mm_wgrad(f'l{l}_ffn_out_dw', a, dy, tmw)
        d_wi = _mm_halves_wgrad_blk(f'l{l}_ffn_in_dw', xf, dgu, tmk, pair_send=pair_send)
        if pair_send:
            d_wi, *sent = d_wi
        else:
            sent = []
        dh_mid, dy_mix, d_sh2, d_sc2, d_g, d_gt1, sum_dy = _ffn_in_dx_norm_bwd(
            f'l{l}_ffn_in_dx', dgu, W['ffn_in'][l], h_mid, dh_out, g_ffn[l], mv(l, 4), tmm, gate=(y_mix, mv(l, 2)))
        return dh_mid, dy_mix, d_wi, d_wo, d_g, (d_sh2, d_sc2), d_gt1, sum_dy, sent

    xm0 = _modnorm_fwd('l0_mix_norm', x, g_mix[0], mv(0, 1), mv(0, 0), tm)
    xc0 = _modnorm_fwd('l0_ctx_norm', ctx, g_mix[0], csc1, csh1, tmc)
    xall = jnp.concatenate([xc0, xm0], axis=0)
    p_qkv, q_all, k_all, v_all = _mm_qk_prep('l0_qkv', xall, W['w_qkv'], cos, sin, gq, gk, bdq, bdk, tmr)
    p_sg, sg = _mm_sg('l0_sg_in', xm0, W['w_sgp'], w_sp, bsp, tmm)
    kh = _to_heads(k_all, N_KV_HEADS)
    vh = _to_heads(v_all, N_KV_HEADS)
    v_ones = jnp.concatenate([vh, jnp.ones(vh.shape[:2] + (1,), BF16),
                              jnp.zeros(vh.shape[:2] + (HEAD_DIM - 1,), BF16)], axis=2)
    o, lse, *gw = _attn_fwd('l0_attn', q_all, kh, v_ones, n_ctx, tq, gather=pending[:4])
    W['w_out'] = gw[0].reshape(D_MODEL, D_MODEL)
    W['ffn_in'] = [gw[1].reshape(N_CHIP, D_MODEL, FF_BLK), gw[3].reshape(N_CHIP, D_MODEL, FF_BLK)]
    W['ffn_out'] = [gw[2].reshape(D_FF, D_MODEL), None]
    cat = jnp.concatenate([o.astype(BF16), sg], axis=1)
    y0, h1, xf0 = _mm_rows_res('l0_out', cat, W['w_out'], x, mv(0, 2), tmw, norm=(g_ffn[0], mv(0, 4), mv(0, 3)))
    (f0, h2, xm1), ffn0 = ffn_fwd(0, h1, xf0, gather=pending[4:], norm=(g_mix[1], mv(1, 1), mv(1, 0)))

    ag, hg = _mm_glu('l1_pw1', xm1, W['pw1'], W['b_pw1'], tmm)
    hc = _dwconv('l1_conv', hg, W['w_dw'], W['b_dw'])
    s1 = _lnsilu_fwd('l1_ln_silu', hc, W['ln_g'], W['ln_b'], tm)
    y1, h3, xf1 = _mm_rows_res('l1_pw2', s1, W['pw2'], h2, mv(1, 2), tmw, bias=W['b_pw2'],
                               norm=(g_ffn[1], mv(1, 4), mv(1, 3)))
    (dh4, d_g_final, loss_vec, dyf1, d_gt2_1), ffn1 = ffn_fwd(1, h3, xf1, final=(tgt, W['g_final']))

    dh3, dy1, d_wi1, d_wo1, d_gffn1, dmod1_ffn, d_gt1_1, d_b_pw2, _ = ffn_bwd(1, dh4, dyf1, y1, ffn1)
    d_pw2 = _mm_wgrad('l1_pw2_dw', s1, dy1, tmw)
    dhc, d_ln_g, d_ln_b = _mm_t_epilogue('l1_pw2_dx', dy1, W['pw2'], [(hc, None)], [W['ln_g'], W['ln_b']], 2,
                                         _lnsilu_bwd_epilogue, tmm)
    dhg = _dwconv('l1_conv_dx', dhc, W['w_dw'][::-1], jnp.zeros((1, D_MODEL), F32))
    d_w_dw, d_b_dw = _dwconv_wgrad('l1_conv_dw', hg, dhc)
    dag, d_b_pw1 = _glu_bwd('l1_glu_bwd', ag, dhg, W['b_pw1'], tm)
    d_pw1 = _mm_wgrad('l1_pw1_dw', xm1, dag, tmw).reshape(D_MODEL, N_CHIP, PW1_BLK).transpose(1, 0, 2)
    dh2, dyf0, d_sh1_1, d_sc1_1, d_gmix1, d_gt2_0, _ = _mm_t_epilogue(
        'l1_pw1_dx', dag, W['pw1'], [(h2, None), (dh3, None)], [g_mix[1], mv(1, 1)], 3, _norm_bwd_epilogue, tmm,
        gate=(f0, mv(0, 5)))

    gs1 = [blocked(g) for g in (d_wi1, d_wo1, d_pw1, d_pw2)]
    dh1, dy0, d_wi0, d_wo0, d_gffn0, dmod0_ffn, d_gt1_0, _, sib1 = ffn_bwd(0, dh2, dyf0, y0, ffn0, pair_send=gs1)
    dcat = _mm_rows('l0_out_dx', dy0, W['w_out'], tmw, trans_b=True)
    gs0 = [blocked(g) for g in (d_wi0, d_wo0)]
    d_w_out, *sib0 = _mm_wgrad('l0_out_dw', cat, dy0, tmw, pair_send=gs0)
    gs0.append(blocked(d_w_out))
    sib0 += _rs_pair_send('rs_pair_send', gs0[2:])

    gs = [gs0[0], gs1[0], gs0[1], gs1[1], gs0[2], gs1[2], gs1[3]]
    sib = [sib0[0], sib1[0], sib0[1], sib1[1], sib0[2], sib1[2], sib1[3]]
    ss = [_rs_pair_add(f'rs_pair_add{t}', gs[t], sib[t], cvec) for t in range(len(gs))]
    dq, dk, dv, *xs = _attn_bwd('l0_attn_bwd', q_all, kh, vh, o, lse, dcat, n_ctx, tq, exchange=ss)
    dk, dv = [t.transpose(2, 0, 1).reshape(n_ctx + seq, KV_W) for t in (dk, dv)]
    dp_qkv, d_gq, d_gk = _qk_prep_bwd('l0_qk_prep_bwd', p_qkv, cos, sin, dq, n_ctx // tm, dk, dv, gq, gk, bdq, bdk,
                                      tm)
    dp_sg, d_w_sp, d_bsp = _sg_bwd('l0_sg_bwd', p_sg, dcat, w_sp, w_sp_t, bsp, tm)
    d_w_qkv = _mm_wgrad('l0_qkv_dw', xall, dp_qkv, tmr)
    d_w_sgp = _mm_wgrad('l0_sg_in_dw', xm0, dp_sg, tmw)
    dxall = _mm_rows('l0_qkv_dx', dp_qkv, W['w_qkv'], tmr, trans_b=True)
    g_in = blocked(jnp.concatenate([d_w_qkv, d_w_sgp], axis=1).reshape(D_MODEL, N_CHIP, -1).transpose(1, 0, 2))
    sib_in = _rs_pair_send('rs_pair_send_w_in', [g_in])
    ss_in = _rs_pair_add('rs_pair_add_w_in', g_in, sib_in[0], cvec)
    dx, d_sh1_0, d_sc1_0, d_gmix0, *xs_in = _mm_t_epilogue(
        'l0_sg_in_dx', dp_sg, W['w_sgp'], [(x, None), (dh1, None), (dxall, lambda i: i + n_ctx // tm)],
        [g_mix[0], mv(0, 1)], 3, _norm_bwd_epilogue, tm, exchange=[ss_in])
    _, d_csh1, d_csc1, d_gmix0c = _modnorm_bwd('l0_ctx_norm_bwd', ctx, [_ri(dxall)], None, g_mix[0], csc1, tmc)

    zero = jnp.zeros((1, D_MODEL), F32)
    d = D_MODEL
    pack = jnp.concatenate([
        d_gmix0 + d_gmix0c, d_gmix1, d_gffn0, d_gffn1, d_g_final,
        jnp.concatenate([d_gq, d_gk, jnp.zeros((1, d - ATTN_W - KV_W), F32)], axis=1),
        d_b_pw1.reshape(2, d), d_w_dw, d_b_dw, d_ln_g, d_ln_b, d_b_pw2,
        d_sh1_0, d_sc1_0, d_gt1_0, *dmod0_ffn, d_gt2_0, d_sh1_1, d_sc1_1, d_gt1_1, *dmod1_ffn, d_gt2_1,
        d_csh1, d_csc1, zero, zero, zero, zero, jnp.zeros((3, d), F32),
        d_w_sp.reshape(64, d),
        jnp.pad(d_bsp.reshape(CHUNK, N_SG, CHUNK).sum(axis=-1).reshape(1, CHUNK * N_SG), ((0, 7), (0, d - SG_W)))],
        axis=0)
    assert pack.shape == (SMALL_ROWS, d)

    t_ffn_in = _sum_into('rs_sum_ffn_in1', xs[1], cvec, 2, 1, prev=_sum_into('rs_sum_ffn_in0', xs[0], cvec, 2, 0))
    t_ffn_out = _sum_into('rs_sum_ffn_out1', xs[3], cvec, 2, 1, prev=_sum_into('rs_sum_ffn_out0', xs[2], cvec, 2, 0))
    *reduced, small = _rs_pair_share('rs_pair_share', [
        t_ffn_in, t_ffn_out, _sum_into('rs_sum_w_in', xs_in[0], cvec, 1, 0), _sum_into('rs_sum_w_out', xs[4], cvec, 1, 0),
        _sum_into('rs_sum_pw1', xs[5], cvec, 1, 0), _sum_into('rs_sum_pw2', xs[6], cvec, 1, 0)], gather=[pack])
    big = dict(zip(['w_ffn_in', 'w_ffn_out', 'w_in', 'w_out', 'w_pw1', 'w_pw2'], reduced))
    return loss_vec, dx, small, big


def kernel(x, c, ctx, c_ctx, w_mod, b_mod, g_mix, g_ffn, w_ffn_in, w_ffn_out, w_in, q_gain, k_gain, w_sp, b_sp, w_out, w_pw1, b_pw1, w_dw, b_dw, ln_g, ln_b, w_pw2, b_pw2, g_final, loss_target, m_c_ctx, m_w_mod, m_b_mod, m_g_mix, m_g_ffn, m_w_ffn_in, m_w_ffn_out, m_w_in, m_q_gain, m_k_gain, m_w_sp, m_b_sp, m_w_out, m_w_pw1, m_b_pw1, m_w_dw, m_b_dw, m_ln_g, m_ln_b, m_w_pw2, m_b_pw2, m_g_final, v_c_ctx, v_w_mod, v_b_mod, v_g_mix, v_g_ffn, v_w_ffn_in, v_w_ffn_out, v_w_in, v_q_gain, v_k_gain, v_w_sp, v_b_sp, v_w_out, v_w_pw1, v_b_pw1, v_w_dw, v_b_dw, v_ln_g, v_ln_b, v_w_pw2, v_b_pw2, v_g_final):
    given = dict(locals())
    ix, iy, ic = _mesh_pos()
    chip = 2 * ix + iy
    me = 2 * chip + ic
    d = D_MODEL
    q4 = d // N_CHIP

    small = jnp.concatenate([c.reshape(4, q4), b_pw1.reshape(2, q4), w_dw[0], b_dw, ln_g, ln_b, b_pw2,
                             jnp.zeros((7, q4), F32)], axis=0)
    def my_half(a):
        r = a.shape[0] // 2
        return lax.dynamic_slice_in_dim(a, ic * r, r, axis=0).astype(BF16)

    sgath, g_w_in = _allgather('ag_first', [small, my_half(w_in[0])])
    c_all = sgath[:, 0:4].reshape(N_DEV, d)
    per_chip = sgath[0::2]
    b_pw1_f = per_chip[:, 4:6].reshape(1, 2 * d)
    w_dw_f = per_chip[:, 6:6 + CONV_W].transpose(1, 0, 2).reshape(CONV_W, d)
    b_dw_f, ln_g_f, ln_b_f, b_pw2_f = [per_chip[:, 37 + i].reshape(1, d) for i in range(4)]

    c16 = jnp.concatenate([c_all, c_ctx[None, :], jnp.zeros((7, d), F32)], axis=0)
    bm_sh = lax.dynamic_slice_in_dim(b_mod, chip * MOD_BLK, MOD_BLK, axis=1).reshape(2, 1, MOD_BLK)
    mod_piece = _mod_fwd('mod_fwd', c16, w_mod, bm_sh)
    mgath = _allgather('ag_mod', [mod_piece.reshape(32, MOD_BLK)])[0]
    mod_all = mgath[0::2].reshape(N_CHIP, 2, 16, MOD_BLK).transpose(1, 2, 0, 3).reshape(2, 16, MOD_W)
    mod_me = lax.dynamic_index_in_dim(mod_all, me, axis=1, keepdims=False)
    cmod = mod_all[0, N_DEV]

    w_in_f = g_w_in.reshape(N_CHIP, d, -1).transpose(1, 0, 2).reshape(d, -1)
    W = dict(
        w_qkv=w_in_f[:, :QKV_W], w_sgp=w_in_f[:, QKV_W:],
        g_mix=g_mix, g_ffn=g_ffn, g_final=g_final.reshape(1, d), q_gain=q_gain, k_gain=k_gain, w_sp=w_sp[0],
        b_sp=b_sp[0], b_pw1=b_pw1_f, w_dw=w_dw_f, b_dw=b_dw_f, ln_g=ln_g_f, ln_b=ln_b_f, b_pw2=b_pw2_f)
    pending = [my_half(w_out[0]), my_half(w_ffn_in[0]), my_half(w_ffn_out[0]), my_half(w_ffn_in[1]),
               my_half(w_pw2[0]), my_half(w_ffn_out[1]), my_half(w_pw1[0])]
    cvec = jnp.reshape(ic, (1,)).astype(jnp.int32)

    loss_vec, dx, pg, big = _sample_step(x[0], ctx[0], loss_target[0], mod_me, cmod, W, pending, cvec)
    loss = lax.psum(jnp.sum(loss_vec), ("x", "y", "c"))

    s8 = _sum_slots('sum_small_grads', pg, SMALL_ROWS)
    grads = dict(big)

    def chip_cols(a, width):
        return lax.dynamic_slice_in_dim(a, chip * width, width, axis=a.ndim - 1)

    grads['g_mix'] = s8[0:2]
    grads['g_ffn'] = s8[2:4]
    grads['g_final'] = s8[4]
    grads['q_gain'] = s8[5, :ATTN_W].reshape(N_Q_HEADS, HEAD_DIM).sum(axis=0)[None, :]
    grads['k_gain'] = s8[5, ATTN_W:ATTN_W + KV_W].reshape(N_KV_HEADS, HEAD_DIM).sum(axis=0)[None, :]
    grads['b_pw1'] = chip_cols(s8[6:8].reshape(1, 2 * d), 2 * q4)
    grads['w_dw'] = chip_cols(s8[8:8 + CONV_W], q4)[None]
    grads['b_dw'] = chip_cols(s8[39:40], q4)
    grads['ln_g'] = chip_cols(s8[40:41], q4)
    grads['ln_b'] = chip_cols(s8[41:42], q4)
    grads['b_pw2'] = chip_cols(s8[42:43], q4)
    grads['w_sp'] = s8[64:128].reshape(1, N_SG, CHUNK, CHUNK)
    grads['b_sp'] = s8[128, :SG_W].reshape(CHUNK, N_SG).T[None]

    dm_rows = pg[:, 43:55].reshape(N_DEV, 2, MOD_W)
    ctx_row = s8[55:61].reshape(1, MOD_W)
    dm = jnp.stack([jnp.concatenate([dm_rows[:, 0], ctx_row, jnp.zeros((7, MOD_W), F32)], axis=0),
                    jnp.concatenate([dm_rows[:, 1], jnp.zeros((8, MOD_W), F32)], axis=0)], axis=0)
    grads['b_mod'] = jnp.stack([s8[43:49].reshape(MOD_W) + ctx_row[0], s8[49:55].reshape(MOD_W)], axis=0)
    dm_sh = chip_cols(dm, MOD_BLK)
    grads['w_mod'] = _mod_wgrad('mod_dw', c16, dm_sh)
    d_silu = _mm_rows('mod_dx', dm_sh[0], w_mod[0], 16, trans_b=True)
    parts = _allgather('ag_cctx', [d_silu[8:16]])[0]
    grads['c_ctx'] = _cctx_grad('cctx_grad', parts, c_ctx.reshape(1, d))[0]

    deltas, new_m, new_v = {}, {}, {}
    for n in WEIGHT_NAMES:
        grads[n] = grads[n].reshape(given[n].shape)
        deltas[n], new_m[n], new_v[n] = _adamw(f'adamw_{n}', given[n], grads[n], given['m_' + n], given['v_' + n])

    return (loss, dx[None], *[grads[n] for n in WEIGHT_NAMES], *[deltas[n] for n in WEIGHT_NAMES],
            *[new_m[n] for n in WEIGHT_NAMES], *[new_v[n] for n in WEIGHT_NAMES])
```
